```python
import math
import jax, jax.numpy as jnp
from jax import lax
import numpy as np

D_MODEL = 1024
BATCH = 8
SEQ = 16384
DEPTH = 2

HEAD_DIM = 64
CONV_GROUPS = 6
GMLP_HEADS = 6
XATTN_HEADS = 4
CONV_W = CONV_GROUPS * HEAD_DIM
GMLP_W = GMLP_HEADS * HEAD_DIM
XATTN_W = XATTN_HEADS * HEAD_DIM
MIX_W = CONV_W + GMLP_W + XATTN_W
IN_W = 2 * CONV_W + 2 * GMLP_W + XATTN_W
CONV_K = 31
CHUNK = 128
N_MEM = 256
D_FF = 2752
FFN_CONV_K = 3
DEEPNORM_ALPHA = (2.0 * DEPTH) ** 0.25
DEEPNORM_BETA = (8.0 * DEPTH) ** -0.25
LN_EPS = 1e-5

kernel_name = "hybrid_conv_gmlp_xattn_deepnorm"


def _layernorm(x, g, b):
    xf = x.astype(jnp.float32)
    mu = jnp.mean(xf, axis=-1, keepdims=True)
    var = jnp.mean(jnp.square(xf - mu), axis=-1, keepdims=True)
    y = (xf - mu) * lax.rsqrt(var + LN_EPS)
    return (y * g.astype(jnp.float32) + b.astype(jnp.float32)).astype(x.dtype)


def _causal_dwconv(x, w, b):
    k, c = w.shape
    y = lax.conv_general_dilated(
        x, w[:, None, :].astype(x.dtype), window_strides=(1,), padding=[(k - 1, 0)],
        dimension_numbers=("NWC", "WIO", "NWC"), feature_group_count=c)
    return y + b.astype(x.dtype)


def _chunk_spatial_gate(u, v, w_s, b_s, ln_g, ln_b):
    bsz, seq, _ = v.shape
    v = _layernorm(v, ln_g, ln_b)
    vc = v.reshape(bsz, seq // CHUNK, CHUNK, GMLP_HEADS, HEAD_DIM)
    mask = jnp.tril(jnp.ones((CHUNK, CHUNK), dtype=bool))
    ws = jnp.where(mask[None], w_s, jnp.zeros((), w_s.dtype)).astype(v.dtype)
    mixed = jnp.einsum("hts,bnshd->bnthd", ws, vc) + b_s.T[:, :, None].astype(v.dtype)
    return u * mixed.reshape(bsz, seq, GMLP_W)


def _memory_cross_attention(q, mem, w_mk, w_mv):
    bsz, seq, _ = q.shape
    m = mem.shape[1]
    qh = q.reshape(bsz, seq, XATTN_HEADS, HEAD_DIM)
    kh = (mem @ w_mk).reshape(bsz, m, XATTN_HEADS, HEAD_DIM)
    vh = (mem @ w_mv).reshape(bsz, m, XATTN_HEADS, HEAD_DIM)
    s = jnp.einsum("bshd,bmhd->bhsm", qh, kh).astype(jnp.float32) * (1.0 / math.sqrt(HEAD_DIM))
    p = jax.nn.softmax(s, axis=-1).astype(vh.dtype)
    o = jnp.einsum("bhsm,bmhd->bshd", p, vh)
    return o.reshape(bsz, seq, XATTN_W)


def _fwd_setup_inputs(seed: int = 0) -> dict:
    key = jax.random.key(seed)
    ks = jax.random.split(key, 24)
    n = jax.random.normal
    f32 = jnp.float32
    L = DEPTH
    return {
        "x": n(ks[0], (BATCH, SEQ, D_MODEL), f32),
        "mem": n(ks[1], (BATCH, N_MEM, D_MODEL), f32),
        "w_in": n(ks[2], (L, D_MODEL, IN_W), f32) * D_MODEL ** -0.5,
        "conv_a_w": n(ks[3], (L, CONV_K, CONV_W), f32) * CONV_K ** -0.5,
        "conv_a_b": n(ks[4], (L, CONV_W), f32) * 0.02,
        "ln_a_g": 1.0 + 0.05 * n(ks[5], (L, CONV_W), f32),
        "ln_a_b": 0.02 * n(ks[6], (L, CONV_W), f32),
        "ln_v_g": 1.0 + 0.05 * n(ks[7], (L, GMLP_W), f32),
        "ln_v_b": 0.02 * n(ks[8], (L, GMLP_W), f32),
        "w_s": n(ks[9], (L, GMLP_HEADS, CHUNK, CHUNK), f32) * CHUNK ** -0.5,
        "b_s": 1.0 + 0.05 * n(ks[10], (L, GMLP_HEADS, CHUNK), f32),
        "w_mk": n(ks[11], (L, D_MODEL, XATTN_W), f32) * D_MODEL ** -0.5,
        "w_mv": n(ks[12], (L, D_MODEL, XATTN_W), f32) * (D_MODEL ** -0.5 * DEEPNORM_BETA),
        "w_out": n(ks[13], (L, MIX_W, D_MODEL), f32) * (MIX_W ** -0.5 * DEEPNORM_BETA),
        "ln1_g": 1.0 + 0.05 * n(ks[14], (L, D_MODEL), f32),
        "ln1_b": 0.02 * n(ks[15], (L, D_MODEL), f32),
        "w_up": n(ks[16], (L, D_MODEL, 2 * D_FF), f32) * D_MODEL ** -0.5,
        "conv_f_w": n(ks[17], (L, FFN_CONV_K, D_FF), f32) * FFN_CONV_K ** -0.5,
        "conv_f_b": n(ks[18], (L, D_FF), f32) * 0.02,
        "w_down": n(ks[19], (L, D_FF, D_MODEL), f32) * (D_FF ** -0.5 * DEEPNORM_BETA),
        "ln2_g": 1.0 + 0.05 * n(ks[20], (L, D_MODEL), f32),
        "ln2_b": 0.02 * n(ks[21], (L, D_MODEL), f32),
    }


def _fwd_reference(x, mem, w_in, conv_a_w, conv_a_b, ln_a_g, ln_a_b, ln_v_g, ln_v_b, w_s, b_s,
              w_mk, w_mv, w_out, ln1_g, ln1_b, w_up, conv_f_w, conv_f_b, w_down, ln2_g, ln2_b):
    o1 = CONV_W
    o2 = 2 * CONV_W
    o3 = o2 + GMLP_W
    o4 = o3 + GMLP_W
    for l in range(DEPTH):
        h = x @ w_in[l]
        a = h[..., :o1] * jax.nn.sigmoid(h[..., o1:o2])
        a = _causal_dwconv(a, conv_a_w[l], conv_a_b[l])
        a = jax.nn.silu(_layernorm(a, ln_a_g[l], ln_a_b[l]))
        u = jax.nn.gelu(h[..., o2:o3])
        v = jax.nn.gelu(h[..., o3:o4])
        g = _chunk_spatial_gate(u, v, w_s[l], b_s[l], ln_v_g[l], ln_v_b[l])
        c = _memory_cross_attention(h[..., o4:], mem, w_mk[l], w_mv[l])
        mix = jnp.concatenate([a, g, c], axis=-1) @ w_out[l]
        x = _layernorm(DEEPNORM_ALPHA * x + mix, ln1_g[l], ln1_b[l])
        up = x @ w_up[l]
        gate = _causal_dwconv(up[..., :D_FF], conv_f_w[l], conv_f_b[l])
        y = (jax.nn.silu(gate) * up[..., D_FF:]) @ w_down[l]
        x = _layernorm(DEEPNORM_ALPHA * x + y, ln2_g[l], ln2_b[l])
    return x


import jax as _jax
import jax.numpy as _jnp

TWIN_FORMAT = 'train_step'
FWD_PARAMS = ['x', 'mem', 'w_in', 'conv_a_w', 'conv_a_b', 'ln_a_g', 'ln_a_b', 'ln_v_g', 'ln_v_b', 'w_s', 'b_s', 'w_mk', 'w_mv', 'w_out', 'ln1_g', 'ln1_b', 'w_up', 'conv_f_w', 'conv_f_b', 'w_down', 'ln2_g', 'ln2_b']
TWIN_WEIGHTS = ['w_in', 'conv_a_w', 'conv_a_b', 'ln_a_g', 'ln_a_b', 'ln_v_g', 'ln_v_b', 'w_s', 'b_s', 'w_mk', 'w_mv', 'w_out', 'ln1_g', 'ln1_b', 'w_up', 'conv_f_w', 'conv_f_b', 'w_down', 'ln2_g', 'ln2_b']
TWIN_DIFF_INPUT = 'x'
TWIN_INPUTS = ['x', 'mem', 'w_in', 'conv_a_w', 'conv_a_b', 'ln_a_g', 'ln_a_b', 'ln_v_g', 'ln_v_b', 'w_s', 'b_s', 'w_mk', 'w_mv', 'w_out', 'ln1_g', 'ln1_b', 'w_up', 'conv_f_w', 'conv_f_b', 'w_down', 'ln2_g', 'ln2_b', 'loss_target', 'm_w_in', 'm_conv_a_w', 'm_conv_a_b', 'm_ln_a_g', 'm_ln_a_b', 'm_ln_v_g', 'm_ln_v_b', 'm_w_s', 'm_b_s', 'm_w_mk', 'm_w_mv', 'm_w_out', 'm_ln1_g', 'm_ln1_b', 'm_w_up', 'm_conv_f_w', 'm_conv_f_b', 'm_w_down', 'm_ln2_g', 'm_ln2_b', 'v_w_in', 'v_conv_a_w', 'v_conv_a_b', 'v_ln_a_g', 'v_ln_a_b', 'v_ln_v_g', 'v_ln_v_b', 'v_w_s', 'v_b_s', 'v_w_mk', 'v_w_mv', 'v_w_out', 'v_ln1_g', 'v_ln1_b', 'v_w_up', 'v_conv_f_w', 'v_conv_f_b', 'v_w_down', 'v_ln2_g', 'v_ln2_b']
TWIN_OUTPUTS = ['loss', 'grad_x', 'grad_w_in', 'grad_conv_a_w', 'grad_conv_a_b', 'grad_ln_a_g', 'grad_ln_a_b', 'grad_ln_v_g', 'grad_ln_v_b', 'grad_w_s', 'grad_b_s', 'grad_w_mk', 'grad_w_mv', 'grad_w_out', 'grad_ln1_g', 'grad_ln1_b', 'grad_w_up', 'grad_conv_f_w', 'grad_conv_f_b', 'grad_w_down', 'grad_ln2_g', 'grad_ln2_b', 'delta_w_in', 'delta_conv_a_w', 'delta_conv_a_b', 'delta_ln_a_g', 'delta_ln_a_b', 'delta_ln_v_g', 'delta_ln_v_b', 'delta_w_s', 'delta_b_s', 'delta_w_mk', 'delta_w_mv', 'delta_w_out', 'delta_ln1_g', 'delta_ln1_b', 'delta_w_up', 'delta_conv_f_w', 'delta_conv_f_b', 'delta_w_down', 'delta_ln2_g', 'delta_ln2_b', 'new_m_w_in', 'new_m_conv_a_w', 'new_m_conv_a_b', 'new_m_ln_a_g', 'new_m_ln_a_b', 'new_m_ln_v_g', 'new_m_ln_v_b', 'new_m_w_s', 'new_m_b_s', 'new_m_w_mk', 'new_m_w_mv', 'new_m_w_out', 'new_m_ln1_g', 'new_m_ln1_b', 'new_m_w_up', 'new_m_conv_f_w', 'new_m_conv_f_b', 'new_m_w_down', 'new_m_ln2_g', 'new_m_ln2_b', 'new_v_w_in', 'new_v_conv_a_w', 'new_v_conv_a_b', 'new_v_ln_a_g', 'new_v_ln_a_b', 'new_v_ln_v_g', 'new_v_ln_v_b', 'new_v_w_s', 'new_v_b_s', 'new_v_w_mk', 'new_v_w_mv', 'new_v_w_out', 'new_v_ln1_g', 'new_v_ln1_b', 'new_v_w_up', 'new_v_conv_f_w', 'new_v_conv_f_b', 'new_v_w_down', 'new_v_ln2_g', 'new_v_ln2_b']
TWIN_LEAF_KINDS = {'loss': 'loss', 'grad_x': 'grad_x', 'grad_w_in': 'grad_w', 'grad_conv_a_w': 'grad_w', 'grad_conv_a_b': 'grad_w', 'grad_ln_a_g': 'grad_w', 'grad_ln_a_b': 'grad_w', 'grad_ln_v_g': 'grad_w', 'grad_ln_v_b': 'grad_w', 'grad_w_s': 'grad_w', 'grad_b_s': 'grad_w', 'grad_w_mk': 'grad_w', 'grad_w_mv': 'grad_w', 'grad_w_out': 'grad_w', 'grad_ln1_g': 'grad_w', 'grad_ln1_b': 'grad_w', 'grad_w_up': 'grad_w', 'grad_conv_f_w': 'grad_w', 'grad_conv_f_b': 'grad_w', 'grad_w_down': 'grad_w', 'grad_ln2_g': 'grad_w', 'grad_ln2_b': 'grad_w', 'delta_w_in': 'delta_w', 'delta_conv_a_w': 'delta_w', 'delta_conv_a_b': 'delta_w', 'delta_ln_a_g': 'delta_w', 'delta_ln_a_b': 'delta_w', 'delta_ln_v_g': 'delta_w', 'delta_ln_v_b': 'delta_w', 'delta_w_s': 'delta_w', 'delta_b_s': 'delta_w', 'delta_w_mk': 'delta_w', 'delta_w_mv': 'delta_w', 'delta_w_out': 'delta_w', 'delta_ln1_g': 'delta_w', 'delta_ln1_b': 'delta_w', 'delta_w_up': 'delta_w', 'delta_conv_f_w': 'delta_w', 'delta_conv_f_b': 'delta_w', 'delta_w_down': 'delta_w', 'delta_ln2_g': 'delta_w', 'delta_ln2_b': 'delta_w', 'new_m_w_in': 'new_m', 'new_m_conv_a_w': 'new_m', 'new_m_conv_a_b': 'new_m', 'new_m_ln_a_g': 'new_m', 'new_m_ln_a_b': 'new_m', 'new_m_ln_v_g': 'new_m', 'new_m_ln_v_b': 'new_m', 'new_m_w_s': 'new_m', 'new_m_b_s': 'new_m', 'new_m_w_mk': 'new_m', 'new_m_w_mv': 'new_m', 'new_m_w_out': 'new_m', 'new_m_ln1_g': 'new_m', 'new_m_ln1_b': 'new_m', 'new_m_w_up': 'new_m', 'new_m_conv_f_w': 'new_m', 'new_m_conv_f_b': 'new_m', 'new_m_w_down': 'new_m', 'new_m_ln2_g': 'new_m', 'new_m_ln2_b': 'new_m', 'new_v_w_in': 'new_v', 'new_v_conv_a_w': 'new_v', 'new_v_conv_a_b': 'new_v', 'new_v_ln_a_g': 'new_v', 'new_v_ln_a_b': 'new_v', 'new_v_ln_v_g': 'new_v', 'new_v_ln_v_b': 'new_v', 'new_v_w_s': 'new_v', 'new_v_b_s': 'new_v', 'new_v_w_mk': 'new_v', 'new_v_w_mv': 'new_v', 'new_v_w_out': 'new_v', 'new_v_ln1_g': 'new_v', 'new_v_ln1_b': 'new_v', 'new_v_w_up': 'new_v', 'new_v_conv_f_w': 'new_v', 'new_v_conv_f_b': 'new_v', 'new_v_w_down': 'new_v', 'new_v_ln2_g': 'new_v', 'new_v_ln2_b': 'new_v'}


def _forward(args):
    return _fwd_reference(*[args[k] for k in FWD_PARAMS])


def _output_shape():
    def fwd():
        inp = _fwd_setup_inputs(0)
        return _fwd_reference(*[inp[k] for k in FWD_PARAMS])
    out = _jax.eval_shape(fwd)
    return out.shape, out.dtype

N_MICROBATCH = 1
ADAM_LR = 0.001
ADAM_B1 = 0.9
ADAM_B2 = 0.999
ADAM_EPS = 1e-08
ADAM_WD = 0.01
ADAM_STEP = 10
PER_EXAMPLE_BATCH_AXIS = {'x': 0, 'mem': 0, 'loss_target': 0}
SHARED_INPUTS = []
_WEIGHT_DTYPES = {'w_in': _jnp.float32, 'conv_a_w': _jnp.float32, 'conv_a_b': _jnp.float32, 'ln_a_g': _jnp.float32, 'ln_a_b': _jnp.float32, 'ln_v_g': _jnp.float32, 'ln_v_b': _jnp.float32, 'w_s': _jnp.float32, 'b_s': _jnp.float32, 'w_mk': _jnp.float32, 'w_mv': _jnp.float32, 'w_out': _jnp.float32, 'ln1_g': _jnp.float32, 'ln1_b': _jnp.float32, 'w_up': _jnp.float32, 'conv_f_w': _jnp.float32, 'conv_f_b': _jnp.float32, 'w_down': _jnp.float32, 'ln2_g': _jnp.float32, 'ln2_b': _jnp.float32}
MOMENT_SCALE = {'w_in': 7.182122e-02, 'conv_a_w': 8.595267e-02, 'conv_a_b': 5.110698e-01, 'ln_a_g': 1.990806e-01, 'ln_a_b': 3.328938e-01, 'ln_v_g': 5.889752e-02, 'ln_v_b': 6.010252e-02, 'w_s': 4.174693e-02, 'b_s': 5.933062e-02, 'w_mk': 6.849263e-03, 'w_mv': 1.623520e-02, 'w_out': 2.521729e-01, 'ln1_g': 1.185244e+01, 'ln1_b': 2.007052e+00, 'w_up': 4.787141e-02, 'conv_f_w': 4.892719e-02, 'conv_f_b': 5.105931e-02, 'w_down': 1.563616e-01, 'ln2_g': 9.233004e+01, 'ln2_b': 9.868549e+00}


def _to_microbatches(a, axis):
    t = _jnp.moveaxis(a, axis, 0)
    t = t.reshape((N_MICROBATCH, t.shape[0] // N_MICROBATCH) + t.shape[1:])
    return _jnp.moveaxis(t, 1, axis + 1)


def setup_inputs(seed: int = 0) -> dict:
    inp = _fwd_setup_inputs(seed)
    key = _jax.random.fold_in(_jax.random.key(seed), 7919)
    shape, _ = _output_shape()
    out = dict(inp)
    out["loss_target"] = _jax.random.normal(_jax.random.fold_in(key, 0), shape, _jnp.float32)
    for i, name in enumerate(TWIN_WEIGHTS):
        w = inp[name].astype(_jnp.float32)
        if MOMENT_SCALE is None:
            s = _jnp.sqrt(_jnp.mean(_jnp.square(w)) + 1e-30)
        else:
            s = MOMENT_SCALE[name]
        km, kv = _jax.random.split(_jax.random.fold_in(key, i + 1))
        out[name] = w
        out["m_" + name] = s * _jax.random.normal(km, w.shape, _jnp.float32)
        out["v_" + name] = (s * s) * _jax.random.uniform(kv, w.shape, _jnp.float32, 0.5, 1.5)
    if N_MICROBATCH > 1:
        for name, axis in PER_EXAMPLE_BATCH_AXIS.items():
            out[name] = _to_microbatches(out[name], axis)
    return {'x': out['x'], 'mem': out['mem'], 'w_in': out['w_in'], 'conv_a_w': out['conv_a_w'], 'conv_a_b': out['conv_a_b'], 'ln_a_g': out['ln_a_g'], 'ln_a_b': out['ln_a_b'], 'ln_v_g': out['ln_v_g'], 'ln_v_b': out['ln_v_b'], 'w_s': out['w_s'], 'b_s': out['b_s'], 'w_mk': out['w_mk'], 'w_mv': out['w_mv'], 'w_out': out['w_out'], 'ln1_g': out['ln1_g'], 'ln1_b': out['ln1_b'], 'w_up': out['w_up'], 'conv_f_w': out['conv_f_w'], 'conv_f_b': out['conv_f_b'], 'w_down': out['w_down'], 'ln2_g': out['ln2_g'], 'ln2_b': out['ln2_b'], 'loss_target': out['loss_target'], 'm_w_in': out['m_w_in'], 'm_conv_a_w': out['m_conv_a_w'], 'm_conv_a_b': out['m_conv_a_b'], 'm_ln_a_g': out['m_ln_a_g'], 'm_ln_a_b': out['m_ln_a_b'], 'm_ln_v_g': out['m_ln_v_g'], 'm_ln_v_b': out['m_ln_v_b'], 'm_w_s': out['m_w_s'], 'm_b_s': out['m_b_s'], 'm_w_mk': out['m_w_mk'], 'm_w_mv': out['m_w_mv'], 'm_w_out': out['m_w_out'], 'm_ln1_g': out['m_ln1_g'], 'm_ln1_b': out['m_ln1_b'], 'm_w_up': out['m_w_up'], 'm_conv_f_w': out['m_conv_f_w'], 'm_conv_f_b': out['m_conv_f_b'], 'm_w_down': out['m_w_down'], 'm_ln2_g': out['m_ln2_g'], 'm_ln2_b': out['m_ln2_b'], 'v_w_in': out['v_w_in'], 'v_conv_a_w': out['v_conv_a_w'], 'v_conv_a_b': out['v_conv_a_b'], 'v_ln_a_g': out['v_ln_a_g'], 'v_ln_a_b': out['v_ln_a_b'], 'v_ln_v_g': out['v_ln_v_g'], 'v_ln_v_b': out['v_ln_v_b'], 'v_w_s': out['v_w_s'], 'v_b_s': out['v_b_s'], 'v_w_mk': out['v_w_mk'], 'v_w_mv': out['v_w_mv'], 'v_w_out': out['v_w_out'], 'v_ln1_g': out['v_ln1_g'], 'v_ln1_b': out['v_ln1_b'], 'v_w_up': out['v_w_up'], 'v_conv_f_w': out['v_conv_f_w'], 'v_conv_f_b': out['v_conv_f_b'], 'v_w_down': out['v_w_down'], 'v_ln2_g': out['v_ln2_g'], 'v_ln2_b': out['v_ln2_b']}


def _loss(weights, diff, rest, loss_target):
    with _jax.named_scope("forward"):
        args = {**rest, TWIN_DIFF_INPUT: diff, **{k: w.astype(_WEIGHT_DTYPES[k]) for k, w in weights.items()}}
        y = _forward(args)
    with _jax.named_scope("loss_head"):
        err = _jnp.square(y.astype(_jnp.float32) - loss_target)
        return 0.5 * _jnp.sum(_jnp.mean(err, axis=-1)) if err.ndim else 0.5 * err


def _adamw(w, g, m, v):
    m = ADAM_B1 * m + (1.0 - ADAM_B1) * g
    v = ADAM_B2 * v + (1.0 - ADAM_B2) * _jnp.square(g)
    m_hat = m / (1.0 - ADAM_B1 ** ADAM_STEP)
    v_hat = v / (1.0 - ADAM_B2 ** ADAM_STEP)
    delta = -ADAM_LR * (m_hat / (_jnp.sqrt(v_hat) + ADAM_EPS) + ADAM_WD * w)
    return delta, m, v


def reference(x, mem, w_in, conv_a_w, conv_a_b, ln_a_g, ln_a_b, ln_v_g, ln_v_b, w_s, b_s, w_mk, w_mv, w_out, ln1_g, ln1_b, w_up, conv_f_w, conv_f_b, w_down, ln2_g, ln2_b, loss_target, m_w_in, m_conv_a_w, m_conv_a_b, m_ln_a_g, m_ln_a_b, m_ln_v_g, m_ln_v_b, m_w_s, m_b_s, m_w_mk, m_w_mv, m_w_out, m_ln1_g, m_ln1_b, m_w_up, m_conv_f_w, m_conv_f_b, m_w_down, m_ln2_g, m_ln2_b, v_w_in, v_conv_a_w, v_conv_a_b, v_ln_a_g, v_ln_a_b, v_ln_v_g, v_ln_v_b, v_w_s, v_b_s, v_w_mk, v_w_mv, v_w_out, v_ln1_g, v_ln1_b, v_w_up, v_conv_f_w, v_conv_f_b, v_w_down, v_ln2_g, v_ln2_b):
    given = dict(x=x, mem=mem, w_in=w_in, conv_a_w=conv_a_w, conv_a_b=conv_a_b, ln_a_g=ln_a_g, ln_a_b=ln_a_b, ln_v_g=ln_v_g, ln_v_b=ln_v_b, w_s=w_s, b_s=b_s, w_mk=w_mk, w_mv=w_mv, w_out=w_out, ln1_g=ln1_g, ln1_b=ln1_b, w_up=w_up, conv_f_w=conv_f_w, conv_f_b=conv_f_b, w_down=w_down, ln2_g=ln2_g, ln2_b=ln2_b, loss_target=loss_target, m_w_in=m_w_in, m_conv_a_w=m_conv_a_w, m_conv_a_b=m_conv_a_b, m_ln_a_g=m_ln_a_g, m_ln_a_b=m_ln_a_b, m_ln_v_g=m_ln_v_g, m_ln_v_b=m_ln_v_b, m_w_s=m_w_s, m_b_s=m_b_s, m_w_mk=m_w_mk, m_w_mv=m_w_mv, m_w_out=m_w_out, m_ln1_g=m_ln1_g, m_ln1_b=m_ln1_b, m_w_up=m_w_up, m_conv_f_w=m_conv_f_w, m_conv_f_b=m_conv_f_b, m_w_down=m_w_down, m_ln2_g=m_ln2_g, m_ln2_b=m_ln2_b, v_w_in=v_w_in, v_conv_a_w=v_conv_a_w, v_conv_a_b=v_conv_a_b, v_ln_a_g=v_ln_a_g, v_ln_a_b=v_ln_a_b, v_ln_v_g=v_ln_v_g, v_ln_v_b=v_ln_v_b, v_w_s=v_w_s, v_b_s=v_b_s, v_w_mk=v_w_mk, v_w_mv=v_w_mv, v_w_out=v_w_out, v_ln1_g=v_ln1_g, v_ln1_b=v_ln1_b, v_w_up=v_w_up, v_conv_f_w=v_conv_f_w, v_conv_f_b=v_conv_f_b, v_w_down=v_w_down, v_ln2_g=v_ln2_g, v_ln2_b=v_ln2_b)
    weights = {n: given[n] for n in TWIN_WEIGHTS}
    shared = {n: given[n] for n in SHARED_INPUTS}
    per_example = {n: given[n] for n in ['x', 'mem']}
    grad_fn = _jax.value_and_grad(_loss, argnums=(0, 1))

    def one_microbatch(ex, loss_target):
        ex = dict(ex)
        diff = ex.pop(TWIN_DIFF_INPUT)
        return grad_fn(weights, diff, {**shared, **ex}, loss_target)

    if N_MICROBATCH == 1:
        loss, (grad_w, grad_x) = one_microbatch(per_example, given["loss_target"])
    else:
        def body(carry, xs):
            loss_sum, grad_sum = carry
            l_k, (gw_k, gx_k) = one_microbatch(xs[0], xs[1])
            with _jax.named_scope("update"):
                return (loss_sum + l_k, _jax.tree.map(_jnp.add, grad_sum, gw_k)), gx_k

        init = (_jnp.zeros((), _jnp.float32), _jax.tree.map(_jnp.zeros_like, weights))
        (loss, grad_w), grad_x = _jax.lax.scan(body, init, (per_example, given["loss_target"]))
    with _jax.named_scope("update"):
        delta_w, new_m, new_v = {}, {}, {}
        for n in TWIN_WEIGHTS:
            delta_w[n], new_m[n], new_v[n] = _adamw(weights[n], grad_w[n], given["m_" + n], given["v_" + n])
    return (loss, grad_x, *[grad_w[n] for n in TWIN_WEIGHTS], *[delta_w[n] for n in TWIN_WEIGHTS],
            *[new_m[n] for n in TWIN_WEIGHTS], *[new_v[n] for n in TWIN_WEIGHTS])
```

```python
import functools
import math

import jax
import jax.numpy as jnp
from jax import lax
from jax.experimental import pallas as pl
from jax.experimental.pallas import tpu as pltpu

F32 = jnp.float32
BF16 = jnp.bfloat16

DEPTH = 2
D_MODEL = 1024
CONV_W = 384
GMLP_W = 384
XATTN_W = 256
HEAD_DIM = 64
GMLP_HEADS = 6
XATTN_HEADS = 4
IN_W = 1792
CONV_K = 31
CHUNK = 128
N_MEM = 256
D_FF = 2752
D_FF_PAD = 2816
FFN_CONV_K = 3
ALPHA = (2.0 * DEPTH) ** 0.25
LN_EPS = 1e-5
N_DEV = 8

ADAM_LR = 0.001
ADAM_B1 = 0.9
ADAM_B2 = 0.999
ADAM_EPS = 1e-08
ADAM_WD = 0.01
ADAM_STEP = 10

HALO = 32
CONV_ROWS = 32
V7X_VMEM_BYTES = 64 * 1024 * 1024
VMEM_LIMIT = V7X_VMEM_BYTES - 8 * 1024 * 1024
BLOB_LANES = 1024
REP_ROWS = 32

MESH = pl.DeviceIdType.MESH

BIG = ("w_in", "w_mk", "w_mv", "w_out", "w_up", "w_down")
BIG_AXIS = {"w_in": 1, "w_mk": 0, "w_mv": 0, "w_out": 0, "w_up": 1, "w_down": 0}
BIG_SHAPE = {"w_in": (D_MODEL, IN_W), "w_mk": (D_MODEL, XATTN_W), "w_mv": (D_MODEL, XATTN_W),
             "w_out": (D_MODEL, D_MODEL), "w_up": (D_MODEL, 2 * D_FF), "w_down": (D_FF, D_MODEL)}
SMALL_SHARDED = ("conv_a_w", "conv_f_w")
SMALL_SHAPE = {"conv_a_w": (CONV_K, CONV_W), "conv_f_w": (FFN_CONV_K, D_FF)}
REPLICATED = ("conv_a_b", "ln_a_g", "ln_a_b", "ln_v_g", "ln_v_b", "w_s", "b_s", "ln1_g", "ln1_b",
              "conv_f_b", "ln2_g", "ln2_b")
WEIGHTS = ("w_in", "conv_a_w", "conv_a_b", "ln_a_g", "ln_a_b", "ln_v_g", "ln_v_b", "w_s", "b_s", "w_mk", "w_mv",
           "w_out", "ln1_g", "ln1_b", "w_up", "conv_f_w", "conv_f_b", "w_down", "ln2_g", "ln2_b")


def _params(**kw):
    return pltpu.CompilerParams(vmem_limit_bytes=VMEM_LIMIT, **kw)


def _const(shape):
    nd = len(shape)
    return pl.BlockSpec(shape, lambda i: (0,) * nd, pipeline_mode=pl.Buffered(1))


def _acc(shape):
    nd = len(shape)
    return pl.BlockSpec(shape, lambda i: (0,) * nd)


def _sigmoid(x):
    return 1.0 / (1.0 + jnp.exp(-x))


_GELU_C = math.sqrt(2.0 / math.pi)


def _gelu(x):
    x2 = x * x
    t = jnp.tanh(_GELU_C * (x + 0.044715 * x * x2))
    g = 0.5 * x * (1.0 + t)
    dg = 0.5 * (1.0 + t) + 0.5 * x * (1.0 - t * t) * (_GELU_C * (1.0 + 3.0 * 0.044715 * x2))
    return g, dg


def _ln_stats(z):
    mu = jnp.mean(z, axis=-1, keepdims=True)
    zc = z - mu
    var = jnp.mean(zc * zc, axis=-1, keepdims=True)
    r = lax.rsqrt(var + LN_EPS)
    return zc * r, r


def _ln_bwd(dy, xh, r, g):
    dxh = dy * g
    m1 = jnp.mean(dxh, axis=-1, keepdims=True)
    m2 = jnp.mean(dxh * xh, axis=-1, keepdims=True)
    return r * (dxh - m1 - xh * m2)


def _rowsum(x):
    return jnp.sum(x, axis=0, keepdims=True)


def _dot(a, b):
    return jnp.dot(a, b, preferred_element_type=F32)


def _dot_tn(a, b):
    return lax.dot_general(a, b, (((0,), (0,)), ((), ())), preferred_element_type=F32)


def _dot_nt(a, b):
    return lax.dot_general(a, b, (((1,), (1,)), ((), ())), preferred_element_type=F32)


def _conv31_fwd(buf, w_ref, bias, out, ts):
    for r0 in range(0, ts, CONV_ROWS):
        acc = jnp.broadcast_to(bias, (CONV_ROWS, CONV_W))
        for k in range(CONV_K):
            s = r0 + HALO - (CONV_K - 1) + k
            acc = acc + w_ref[k:k + 1, :] * buf[s:s + CONV_ROWS, :]
        out[r0:r0 + CONV_ROWS, :] = acc


def _conv31_dx(dbuf, w_ref, out, ts):
    for r0 in range(0, ts, CONV_ROWS):
        acc = jnp.zeros((CONV_ROWS, CONV_W), F32)
        for k in range(CONV_K):
            s = r0 + (CONV_K - 1) - k
            acc = acc + w_ref[k:k + 1, :] * dbuf[s:s + CONV_ROWS, :]
        out[r0:r0 + CONV_ROWS, :] = acc


def _conv31_dw(buf, dbuf, dw_ref, ts):
    for k in range(CONV_K):
        part = jnp.zeros((8, CONV_W), F32)
        for r0 in range(0, ts, CONV_ROWS):
            s = r0 + HALO - (CONV_K - 1) + k
            m = dbuf[r0:r0 + CONV_ROWS, :] * buf[s:s + CONV_ROWS, :]
            for q in range(0, CONV_ROWS, 8):
                part = part + m[q:q + 8, :]
        dw_ref[k:k + 1, :] += _rowsum(part)


def _head_mask(width, h):
    lane = lax.broadcasted_iota(jnp.int32, (CHUNK, width), 1)
    return (lane >= h * HEAD_DIM) & (lane < (h + 1) * HEAD_DIM)


def _stack_heads(vn_c):
    return jnp.concatenate([jnp.where(_head_mask(GMLP_W, h), vn_c, 0.0) for h in range(GMLP_HEADS)], axis=0)


def _group_a_fwd(hf, buf, a1buf, cw_ref, va_ref, ts):
    ha = hf[:, 0:CONV_W]
    sg = _sigmoid(hf[:, CONV_W:2 * CONV_W])
    buf[HALO:HALO + ts, :] = ha * sg
    _conv31_fwd(buf, cw_ref, va_ref[0:1, :], a1buf, ts)
    a2h, ra = _ln_stats(a1buf[...])
    a2 = a2h * va_ref[1:2, :] + va_ref[2:3, :]
    sa = _sigmoid(a2)
    return dict(ha=ha, sg=sg, a2h=a2h, ra=ra, a2=a2, sa=sa, a=a2 * sa)


def _group_b_fwd(hf, va_ref, wcat_ref, bfull_ref, ts):
    hu = hf[:, 2 * CONV_W:2 * CONV_W + GMLP_W]
    hv = hf[:, 2 * CONV_W + GMLP_W:2 * CONV_W + 2 * GMLP_W]
    u, du = _gelu(hu)
    v, dv = _gelu(hv)
    vhat, rv = _ln_stats(v)
    vn = vhat * va_ref[3:4, :] + va_ref[4:5, :]
    stacks, mixed = [], []
    for c0 in range(0, ts, CHUNK):
        st = _stack_heads(vn[c0:c0 + CHUNK, :]).astype(BF16)
        stacks.append(st)
        mixed.append(_dot(wcat_ref[...], st) + bfull_ref[...])
    mixed = jnp.concatenate(mixed, axis=0) if len(mixed) > 1 else mixed[0]
    return dict(u=u, du=du, dv=dv, vhat=vhat, rv=rv, stacks=stacks, mixed=mixed, g=u * mixed)


def _group_c_fwd(hf, kt_ref, vm_ref):
    qb = hf[:, IN_W - XATTN_W:IN_W].astype(BF16)
    s_all = _dot(qb, kt_ref[...])
    ps = []
    for g in range(XATTN_HEADS):
        s = s_all[:, g * N_MEM:(g + 1) * N_MEM]
        e = jnp.exp(s - jnp.max(s, axis=-1, keepdims=True))
        ps.append(e / jnp.sum(e, axis=-1, keepdims=True))
    p_all = jnp.concatenate(ps, axis=1)
    pb = p_all.astype(BF16)
    return dict(qb=qb, p=p_all, pb=pb, c=_dot(pb, vm_ref[...]))


def _mixer_fwd_call(x, w, ts):
    seq = x.shape[0]
    n = seq // ts

    def body(x_ref, win_ref, cw_ref, va_ref, wcat_ref, bfull_ref, kt_ref, vm_ref, wout_ref, v1_ref,
             hb_ref, z1_ref, x1_ref, buf, a1buf):
        i = pl.program_id(0)

        @pl.when(i == 0)
        def _():
            buf[0:HALO, :] = jnp.zeros((HALO, CONV_W), F32)

        @pl.when(i > 0)
        def _():
            buf[0:HALO, :] = buf[ts:ts + HALO, :]

        xv = x_ref[...]
        hb = _dot(xv.astype(BF16), win_ref[...]).astype(BF16)
        hb_ref[...] = hb
        hf = hb.astype(F32)
        ga = _group_a_fwd(hf, buf, a1buf, cw_ref, va_ref, ts)
        gb = _group_b_fwd(hf, va_ref, wcat_ref, bfull_ref, ts)
        gc = _group_c_fwd(hf, kt_ref, vm_ref)
        cat = jnp.concatenate([ga["a"], gb["g"], gc["c"]], axis=1).astype(BF16)
        z1 = ALPHA * xv + _dot(cat, wout_ref[...])
        z1_ref[...] = z1
        xh, _ = _ln_stats(z1)
        x1_ref[...] = xh * v1_ref[0:1, :] + v1_ref[1:2, :]

    row = lambda width: pl.BlockSpec((ts, width), lambda i: (i, 0))
    return pl.pallas_call(
        body, name="mixer_fwd", grid=(n,),
        in_specs=[row(D_MODEL), _const((D_MODEL, IN_W)), _const((HALO, CONV_W)), _const((8, CONV_W)),
                  _const((CHUNK, GMLP_HEADS * CHUNK)), _const((CHUNK, GMLP_W)), _const((XATTN_W, XATTN_HEADS * N_MEM)),
                  _const((XATTN_HEADS * N_MEM, XATTN_W)), _const((D_MODEL, D_MODEL)), _const((8, D_MODEL))],
        out_specs=[row(IN_W), row(D_MODEL), row(D_MODEL)],
        out_shape=[jax.ShapeDtypeStruct((seq, IN_W), BF16), jax.ShapeDtypeStruct((seq, D_MODEL), F32),
                   jax.ShapeDtypeStruct((seq, D_MODEL), F32)],
        scratch_shapes=[pltpu.VMEM((ts + HALO, CONV_W), F32), pltpu.VMEM((ts, CONV_W), F32)],
        compiler_params=_params(dimension_semantics=("arbitrary",)),
    )(x, w["win"], w["cw"], w["va"], w["wcat"], w["bfull"], w["kt"], w["vm"], w["wout"], w["v1"])


def _mixer_bwd_call(dx1, z1, hb, w, ts):
    seq = dx1.shape[0]
    n = seq // ts
    halo_blocks = ts // HALO

    def body(dx1_ref, z1_ref, hb_ref, hprev_ref, cw_ref, va_ref, wcat_ref, wcatt_ref, bfull_ref, kt_ref, ktt_ref,
             vm_ref, vmt_ref, woutt_ref, v1_ref,
             dh_ref, dz1_ref, dwout_ref, dkt_ref, dvm_ref, dwcat_ref, dmsum_ref, dva_ref, dcw_ref, dv1_ref,
             buf, a1buf, dbuf, da0buf):
        i = pl.program_id(0)

        @pl.when(i == 0)
        def _():
            for ref in (dwout_ref, dkt_ref, dvm_ref, dwcat_ref, dmsum_ref, dva_ref, dcw_ref, dv1_ref):
                ref[...] = jnp.zeros(ref.shape, F32)
            dbuf[ts:ts + HALO, :] = jnp.zeros((HALO, CONV_W), F32)

        @pl.when(i > 0)
        def _():
            dbuf[ts:ts + HALO, :] = dbuf[0:HALO, :]

        dx1v = dx1_ref[...]
        xh1, r1 = _ln_stats(z1_ref[...])
        dv1_ref[0:1, :] += _rowsum(dx1v * xh1)
        dv1_ref[1:2, :] += _rowsum(dx1v)
        dz1 = _ln_bwd(dx1v, xh1, r1, v1_ref[0:1, :])
        dz1_ref[...] = dz1
        dmix = dz1.astype(BF16)

        hf = hb_ref[...].astype(F32)
        hp = hprev_ref[...].astype(F32)
        a0p = hp[:, 0:CONV_W] * _sigmoid(hp[:, CONV_W:2 * CONV_W])
        buf[0:HALO, :] = jnp.where(i == n - 1, 0.0, a0p)
        ga = _group_a_fwd(hf, buf, a1buf, cw_ref, va_ref, ts)
        gb = _group_b_fwd(hf, va_ref, wcat_ref, bfull_ref, ts)
        gc = _group_c_fwd(hf, kt_ref, vm_ref)
        cat = jnp.concatenate([ga["a"], gb["g"], gc["c"]], axis=1).astype(BF16)

        dwout_ref[...] += _dot_tn(cat, dmix)
        dcat = _dot(dmix, woutt_ref[...])
        da = dcat[:, 0:CONV_W]
        dg = dcat[:, CONV_W:CONV_W + GMLP_W]
        dc = dcat[:, CONV_W + GMLP_W:D_MODEL].astype(BF16)

        dp = _dot(dc, vmt_ref[...])
        dvm_ref[...] += _dot_tn(gc["pb"], dc)
        dss = []
        for g in range(XATTN_HEADS):
            sl = slice(g * N_MEM, (g + 1) * N_MEM)
            pg = gc["p"][:, sl]
            dpg = dp[:, sl]
            dss.append(pg * (dpg - jnp.sum(dpg * pg, axis=-1, keepdims=True)))
        ds = jnp.concatenate(dss, axis=1).astype(BF16)
        dq = _dot(ds, ktt_ref[...])
        dkt_ref[...] += _dot_tn(gc["qb"], ds)

        dmixed = dg * gb["u"]
        dhu = dg * gb["mixed"] * gb["du"]
        dvns = []
        for j, c0 in enumerate(range(0, ts, CHUNK)):
            dm = dmixed[c0:c0 + CHUNK, :]
            dmb = dm.astype(BF16)
            dmsum_ref[...] += dm
            dwcat_ref[...] += _dot_nt(dmb, gb["stacks"][j])
            dst = _dot(wcatt_ref[...], dmb)
            dvn_c = jnp.zeros((CHUNK, GMLP_W), F32)
            for h in range(GMLP_HEADS):
                dvn_c = dvn_c + jnp.where(_head_mask(GMLP_W, h), dst[h * CHUNK:(h + 1) * CHUNK, :], 0.0)
            dvns.append(dvn_c)
        dvn = jnp.concatenate(dvns, axis=0) if len(dvns) > 1 else dvns[0]
        dva_ref[3:4, :] += _rowsum(dvn * gb["vhat"])
        dva_ref[4:5, :] += _rowsum(dvn)
        dhv = _ln_bwd(dvn, gb["vhat"], gb["rv"], va_ref[3:4, :]) * gb["dv"]

        a2, sa = ga["a2"], ga["sa"]
        da2 = da * (sa * (1.0 + a2 * (1.0 - sa)))
        dva_ref[1:2, :] += _rowsum(da2 * ga["a2h"])
        dva_ref[2:3, :] += _rowsum(da2)
        da1 = _ln_bwd(da2, ga["a2h"], ga["ra"], va_ref[1:2, :])
        dva_ref[0:1, :] += _rowsum(da1)
        dbuf[0:ts, :] = da1
        _conv31_dw(buf, dbuf, dcw_ref, ts)
        _conv31_dx(dbuf, cw_ref, da0buf, ts)
        da0 = da0buf[...]
        sg = ga["sg"]
        dha = da0 * sg
        dhg = da0 * ga["ha"] * sg * (1.0 - sg)

        dh_ref[...] = jnp.concatenate([dha, dhg, dhu, dhv, dq], axis=1).astype(BF16)

    rev = lambda width: pl.BlockSpec((ts, width), lambda i: (n - 1 - i, 0))
    prev = pl.BlockSpec((HALO, 2 * CONV_W), lambda i: (jnp.maximum((n - 1 - i) * halo_blocks - 1, 0), 0))
    hc = GMLP_HEADS * CHUNK
    am = XATTN_HEADS * N_MEM
    return pl.pallas_call(
        body, name="mixer_bwd", grid=(n,),
        in_specs=[rev(D_MODEL), rev(D_MODEL), rev(IN_W), prev, _const((HALO, CONV_W)), _const((8, CONV_W)),
                  _const((CHUNK, hc)), _const((hc, CHUNK)), _const((CHUNK, GMLP_W)), _const((XATTN_W, am)),
                  _const((am, XATTN_W)), _const((am, XATTN_W)), _const((XATTN_W, am)), _const((D_MODEL, D_MODEL)),
                  _const((8, D_MODEL))],
        out_specs=[rev(IN_W), rev(D_MODEL), _acc((D_MODEL, D_MODEL)), _acc((XATTN_W, am)), _acc((am, XATTN_W)),
                   _acc((CHUNK, hc)), _acc((CHUNK, GMLP_W)), _acc((8, CONV_W)), _acc((HALO, CONV_W)),
                   _acc((8, D_MODEL))],
        out_shape=[jax.ShapeDtypeStruct((seq, IN_W), BF16), jax.ShapeDtypeStruct((seq, D_MODEL), F32),
                   jax.ShapeDtypeStruct((D_MODEL, D_MODEL), F32), jax.ShapeDtypeStruct((XATTN_W, am), F32),
                   jax.ShapeDtypeStruct((am, XATTN_W), F32), jax.ShapeDtypeStruct((CHUNK, hc), F32),
                   jax.ShapeDtypeStruct((CHUNK, GMLP_W), F32), jax.ShapeDtypeStruct((8, CONV_W), F32),
                   jax.ShapeDtypeStruct((HALO, CONV_W), F32), jax.ShapeDtypeStruct((8, D_MODEL), F32)],
        scratch_shapes=[pltpu.VMEM((ts + HALO, CONV_W), F32), pltpu.VMEM((ts, CONV_W), F32),
                        pltpu.VMEM((ts + HALO, CONV_W), F32), pltpu.VMEM((ts, CONV_W), F32)],
        compiler_params=_params(dimension_semantics=("arbitrary",)),
    )(dx1, z1, hb, hb, w["cw"], w["va"], w["wcat"], w["wcatt"], w["bfull"], w["kt"], w["ktt"], w["vm"], w["vmt"],
      w["woutt"], w["v1"])


FFN_HALO = 8


def _ffn_gate(ubuf, cf_ref, ts, lo, hi):
    g = cf_ref[3:4, lo:hi] + cf_ref[2:3, lo:hi] * ubuf[FFN_HALO:FFN_HALO + ts, lo:hi]
    g = g + cf_ref[1:2, lo:hi] * ubuf[FFN_HALO - 1:FFN_HALO - 1 + ts, lo:hi]
    return g + cf_ref[0:1, lo:hi] * ubuf[FFN_HALO - 2:FFN_HALO - 2 + ts, lo:hi]


FFN_LANE_CHUNKS = ((0, D_FF_PAD // 2), (D_FF_PAD // 2, D_FF_PAD))


def _ffn_fwd_call(x1, w, ts):
    seq = x1.shape[0]
    n = seq // ts

    def body(x1_ref, wg_ref, wv_ref, cf_ref, wdown_ref, v2_ref, ug_ref, uv_ref, z2_ref, x2_ref, ubuf):
        i = pl.program_id(0)

        @pl.when(i == 0)
        def _():
            ubuf[0:FFN_HALO, :] = jnp.zeros((FFN_HALO, D_FF_PAD), F32)

        @pl.when(i > 0)
        def _():
            ubuf[0:FFN_HALO, :] = ubuf[ts:ts + FFN_HALO, :]

        xv = x1_ref[...]
        xb = xv.astype(BF16)
        y = ALPHA * xv
        for lo, hi in FFN_LANE_CHUNKS:
            ug = _dot(xb, wg_ref[:, lo:hi]).astype(BF16)
            uv = _dot(xb, wv_ref[:, lo:hi]).astype(BF16)
            ug_ref[:, lo:hi] = ug
            uv_ref[:, lo:hi] = uv
            ubuf[FFN_HALO:FFN_HALO + ts, lo:hi] = ug.astype(F32)
            gate = _ffn_gate(ubuf, cf_ref, ts, lo, hi)
            act = (gate * _sigmoid(gate) * uv.astype(F32)).astype(BF16)
            y = y + _dot(act, wdown_ref[lo:hi, :])
        z2_ref[...] = y
        xh, _ = _ln_stats(y)
        x2_ref[...] = xh * v2_ref[0:1, :] + v2_ref[1:2, :]

    row = lambda width: pl.BlockSpec((ts, width), lambda i: (i, 0))
    return pl.pallas_call(
        body, name="ffn_fwd", grid=(n,),
        in_specs=[row(D_MODEL), _const((D_MODEL, D_FF_PAD)), _const((D_MODEL, D_FF_PAD)), _const((8, D_FF_PAD)),
                  _const((D_FF_PAD, D_MODEL)), _const((8, D_MODEL))],
        out_specs=[row(D_FF_PAD), row(D_FF_PAD), row(D_MODEL), row(D_MODEL)],
        out_shape=[jax.ShapeDtypeStruct((seq, D_FF_PAD), BF16), jax.ShapeDtypeStruct((seq, D_FF_PAD), BF16),
                   jax.ShapeDtypeStruct((seq, D_MODEL), F32), jax.ShapeDtypeStruct((seq, D_MODEL), F32)],
        scratch_shapes=[pltpu.VMEM((ts + FFN_HALO, D_FF_PAD), F32)],
        compiler_params=_params(dimension_semantics=("arbitrary",)),
    )(x1, w["wg"], w["wv"], w["cf"], w["wdown"], w["v2"])


def _ffn_bwd_call(dx2_or_target, z2, ug, uv, w, ts, last):
    seq = z2.shape[0]
    n = seq // ts
    halo_blocks = ts // 16

    def body(dx2_ref, z2_ref, ug_ref, uv_ref, uprev_ref, cf_ref, wdownt_ref, v2_ref,
             dug_ref, duv_ref, dz2_ref, dwdown_hbm, dcf_ref, dv2_ref, loss_ref,
             ubuf, dgbuf, dwacc):
        i = pl.program_id(0)

        @pl.when(i == 0)
        def _():
            dwacc[...] = jnp.zeros(dwacc.shape, F32)
            dcf_ref[...] = jnp.zeros(dcf_ref.shape, F32)
            dv2_ref[...] = jnp.zeros(dv2_ref.shape, F32)
            loss_ref[...] = jnp.zeros(loss_ref.shape, F32)
            dgbuf[ts:ts + FFN_HALO, :] = jnp.zeros((FFN_HALO, D_FF_PAD), F32)

        @pl.when(i > 0)
        def _():
            dgbuf[ts:ts + FFN_HALO, :] = dgbuf[0:FFN_HALO, :]

        xh2, r2 = _ln_stats(z2_ref[...])
        if last:
            diff = xh2 * v2_ref[0:1, :] + v2_ref[1:2, :] - dx2_ref[...]
            loss_ref[...] += jnp.sum(diff * diff) * (0.5 / D_MODEL)
            dx2v = diff * (1.0 / D_MODEL)
        else:
            dx2v = dx2_ref[...]
        dv2_ref[0:1, :] += _rowsum(dx2v * xh2)
        dv2_ref[1:2, :] += _rowsum(dx2v)
        dz2 = _ln_bwd(dx2v, xh2, r2, v2_ref[0:1, :])
        dz2_ref[...] = dz2
        dy = dz2.astype(BF16)

        up = uprev_ref[...].astype(F32)[8:16, :]
        ubuf[0:FFN_HALO, :] = jnp.where(i == n - 1, 0.0, up)
        ubuf[FFN_HALO:FFN_HALO + ts, :] = ug_ref[...].astype(F32)
        for lo, hi in FFN_LANE_CHUNKS:
            gate = _ffn_gate(ubuf, cf_ref, ts, lo, hi)
            sg = _sigmoid(gate)
            sl = gate * sg
            uvf = uv_ref[:, lo:hi].astype(F32)
            act = (sl * uvf).astype(BF16)
            dwacc[lo:hi, :] += _dot_tn(act, dy)
            dact = _dot(dy, wdownt_ref[:, lo:hi])
            duv_ref[:, lo:hi] = (dact * sl).astype(BF16)
            dgate = dact * uvf * (sg * (1.0 + gate * (1.0 - sg)))
            dgbuf[0:ts, lo:hi] = dgate
            dcf_ref[3:4, lo:hi] += _rowsum(dgate)
            for k in range(FFN_CONV_K):
                s = FFN_HALO - (FFN_CONV_K - 1) + k
                dcf_ref[k:k + 1, lo:hi] += _rowsum(dgate * ubuf[s:s + ts, lo:hi])
            dug = cf_ref[2:3, lo:hi] * dgate + cf_ref[1:2, lo:hi] * dgbuf[1:1 + ts, lo:hi]
            dug = dug + cf_ref[0:1, lo:hi] * dgbuf[2:2 + ts, lo:hi]
            dug_ref[:, lo:hi] = dug.astype(BF16)

        @pl.when(i == n - 1)
        def _():
            pltpu.sync_copy(dwacc, dwdown_hbm)

    rev = lambda width: pl.BlockSpec((ts, width), lambda i: (n - 1 - i, 0))
    prev = pl.BlockSpec((16, D_FF_PAD), lambda i: (jnp.maximum((n - 1 - i) * halo_blocks - 1, 0), 0))
    return pl.pallas_call(
        body, name="ffn_bwd_last" if last else "ffn_bwd", grid=(n,),
        in_specs=[rev(D_MODEL), rev(D_MODEL), rev(D_FF_PAD), rev(D_FF_PAD), prev, _const((8, D_FF_PAD)),
                  _const((D_MODEL, D_FF_PAD)), _const((8, D_MODEL))],
        out_specs=[rev(D_FF_PAD), rev(D_FF_PAD), rev(D_MODEL), pl.BlockSpec(memory_space=pl.ANY),
                   _acc((8, D_FF_PAD)), _acc((8, D_MODEL)), _acc((8, 128))],
        out_shape=[jax.ShapeDtypeStruct((seq, D_FF_PAD), BF16), jax.ShapeDtypeStruct((seq, D_FF_PAD), BF16),
                   jax.ShapeDtypeStruct((seq, D_MODEL), F32), jax.ShapeDtypeStruct((D_FF_PAD, D_MODEL), F32),
                   jax.ShapeDtypeStruct((8, D_FF_PAD), F32), jax.ShapeDtypeStruct((8, D_MODEL), F32),
                   jax.ShapeDtypeStruct((8, 128), F32)],
        scratch_shapes=[pltpu.VMEM((ts + FFN_HALO, D_FF_PAD), F32), pltpu.VMEM((ts + FFN_HALO, D_FF_PAD), F32),
                        pltpu.VMEM((D_FF_PAD, D_MODEL), F32)],
        compiler_params=_params(dimension_semantics=("arbitrary",)),
    )(dx2_or_target, z2, ug, uv, ug, w["cf"], w["wdownt"], w["v2"])


def _proj_bwd_call(d, wt, xin, addend, scale, ts, name):
    seq, k = d.shape
    n = seq // ts

    def body(d_ref, wt_ref, xin_ref, add_ref, dx_ref, dw_hbm, acc):
        i = pl.program_id(0)

        @pl.when(i == 0)
        def _():
            acc[...] = jnp.zeros(acc.shape, F32)

        dv = d_ref[...]
        dx_ref[...] = _dot(dv, wt_ref[...]) + scale * add_ref[...]
        acc[...] += _dot_tn(xin_ref[...].astype(BF16), dv)

        @pl.when(i == n - 1)
        def _():
            pltpu.sync_copy(acc, dw_hbm)

    row = lambda width: pl.BlockSpec((ts, width), lambda i: (i, 0))
    return pl.pallas_call(
        body, name=name, grid=(n,),
        in_specs=[row(k), _const((k, D_MODEL)), row(D_MODEL), row(D_MODEL)],
        out_specs=[row(D_MODEL), pl.BlockSpec(memory_space=pl.ANY)],
        out_shape=[jax.ShapeDtypeStruct((seq, D_MODEL), F32), jax.ShapeDtypeStruct((D_MODEL, k), F32)],
        scratch_shapes=[pltpu.VMEM((D_MODEL, k), F32)],
        compiler_params=_params(dimension_semantics=("arbitrary",)),
    )(d, wt, xin, addend)


def _small_mm_call(a, b, transpose_a, name):
    m = a.shape[1] if transpose_a else a.shape[0]

    def body(a_ref, b_ref, o_ref):
        av = a_ref[...].astype(BF16)
        bv = b_ref[...].astype(BF16)
        o_ref[...] = _dot_tn(av, bv) if transpose_a else _dot(av, bv)

    return pl.pallas_call(body, name=name, out_shape=jax.ShapeDtypeStruct((m, b.shape[1]), F32),
                          compiler_params=_params())(a, b)


def _place():
    return lax.axis_index("x"), lax.axis_index("y"), lax.axis_index("c")


def _all_gather_call(arrs, name):
    n_arr = len(arrs)

    def body(*refs):
        ins, outs = refs[:n_arr], refs[n_arr:2 * n_arr]
        send_sems, recv_sems, local_sems = refs[2 * n_arr:]
        x, y, c = _place()
        me, sibling = (x, y, c), (x, y, 1 - c)
        chips = [(1 - x, y), (x, 1 - y), (1 - x, 1 - y)]

        def slab(a, p):
            return outs[a].at[4 * p[0] + 2 * p[1] + p[2]]

        def copy(a, k, block, to, src=None):
            return pltpu.make_async_remote_copy(
                src_ref=slab(a, block) if src is None else src, dst_ref=slab(a, block),
                send_sem=send_sems.at[a, k], recv_sem=recv_sems.at[a, k], device_id=to, device_id_type=MESH)

        mine = [pltpu.make_async_copy(ins[a], slab(a, me), local_sems.at[a]) for a in range(n_arr)]
        for cp in mine:
            cp.start()
        first = []
        for a in range(n_arr):
            first.append(copy(a, 0, me, sibling, src=ins[a]))
            first += [copy(a, 1 + j, me, (*chip, c), src=ins[a]) for j, chip in enumerate(chips)]
        for cp in first:
            cp.start()
        passed = []
        for a in range(n_arr):
            for j, chip in enumerate(chips):
                copy(a, 1 + j, (*chip, c), me).wait_recv()
                cp = copy(a, 4 + j, (*chip, c), sibling)
                cp.start()
                passed.append(cp)
        for a in range(n_arr):
            copy(a, 0, sibling, me).wait_recv()
            for j, chip in enumerate(chips):
                copy(a, 4 + j, (*chip, 1 - c), me).wait_recv()
        for cp in first + passed:
            cp.wait_send()
        for cp in mine:
            cp.wait()

    hbm = pl.BlockSpec(memory_space=pl.ANY)
    return pl.pallas_call(
        body, name=name,
        in_specs=[hbm] * n_arr, out_specs=[hbm] * n_arr,
        out_shape=[jax.ShapeDtypeStruct((N_DEV,) + a.shape, a.dtype) for a in arrs],
        scratch_shapes=[pltpu.SemaphoreType.DMA((n_arr, 7)), pltpu.SemaphoreType.DMA((n_arr, 7)),
                        pltpu.SemaphoreType.DMA((n_arr,))],
    )(*arrs)


def _swap_core_call(g):
    _, rows, width = g.shape

    def body(g_ref, r_ref, send_sems, recv_sems):
        x, y, c = _place()
        copies = []
        for k in range(4):
            cp = pltpu.make_async_remote_copy(
                src_ref=g_ref.at[2 * k + (1 - c)], dst_ref=r_ref.at[k], send_sem=send_sems.at[k],
                recv_sem=recv_sems.at[k], device_id=(x, y, 1 - c), device_id_type=MESH)
            cp.start()
            copies.append(cp)
        for cp in copies:
            cp.wait()

    hbm = pl.BlockSpec(memory_space=pl.ANY)
    return pl.pallas_call(
        body, name="rs_swap_core", in_specs=[hbm], out_specs=hbm,
        out_shape=jax.ShapeDtypeStruct((4, rows, width), g.dtype),
        scratch_shapes=[pltpu.SemaphoreType.DMA((4,)), pltpu.SemaphoreType.DMA((4,))],
    )(g)


def _swap_chip_call(p):
    _, rows, width = p.shape

    def body(p_ref, r_ref, send_sems, recv_sems):
        x, y, c = _place()
        chips = [(1 - x, y), (x, 1 - y), (1 - x, 1 - y)]
        copies = []
        for j, (px, py) in enumerate(chips):
            cp = pltpu.make_async_remote_copy(
                src_ref=p_ref.at[2 * px + py], dst_ref=r_ref.at[j], send_sem=send_sems.at[j],
                recv_sem=recv_sems.at[j], device_id=(px, py, c), device_id_type=MESH)
            cp.start()
            copies.append(cp)
        for cp in copies:
            cp.wait()

    hbm = pl.BlockSpec(memory_space=pl.ANY)
    return pl.pallas_call(
        body, name="rs_swap_chip", in_specs=[hbm], out_specs=hbm,
        out_shape=jax.ShapeDtypeStruct((3, rows, width), p.dtype),
        scratch_shapes=[pltpu.SemaphoreType.DMA((3,)), pltpu.SemaphoreType.DMA((3,))],
    )(p)


ADD_ROWS = 128


def _pair_add_call(g, r, c):
    _, rows, width = g.shape

    def body(c_ref, g_ref, r_ref, o_ref):
        o_ref[...] = g_ref[...] + r_ref[...]

    return pl.pallas_call(
        body, name="rs_pair_add",
        grid_spec=pltpu.PrefetchScalarGridSpec(
            num_scalar_prefetch=1, grid=(4, rows // ADD_ROWS),
            in_specs=[pl.BlockSpec((None, ADD_ROWS, width), lambda k, i, c_ref: (2 * k + c_ref[0], i, 0)),
                      pl.BlockSpec((None, ADD_ROWS, width), lambda k, i, c_ref: (k, i, 0))],
            out_specs=pl.BlockSpec((None, ADD_ROWS, width), lambda k, i, c_ref: (k, i, 0))),
        out_shape=jax.ShapeDtypeStruct((4, rows, width), F32),
        compiler_params=_params(dimension_semantics=("arbitrary", "arbitrary")),
    )(c, g, r)


def _chip_add_call(p, r, chip):
    _, rows, width = p.shape

    def body(s_ref, p_ref, r_ref, o_ref):
        o_ref[...] = ((p_ref[...] + r_ref[0]) + r_ref[1]) + r_ref[2]

    return pl.pallas_call(
        body, name="rs_chip_add",
        grid_spec=pltpu.PrefetchScalarGridSpec(
            num_scalar_prefetch=1, grid=(rows // ADD_ROWS,),
            in_specs=[pl.BlockSpec((None, ADD_ROWS, width), lambda i, s_ref: (s_ref[0], i, 0)),
                      pl.BlockSpec((3, ADD_ROWS, width), lambda i, s_ref: (0, i, 0))],
            out_specs=pl.BlockSpec((ADD_ROWS, width), lambda i, s_ref: (i, 0))),
        out_shape=jax.ShapeDtypeStruct((rows, width), F32),
        compiler_params=_params(dimension_semantics=("arbitrary",)),
    )(chip, p, r)


def _adamw_call(w, g, m, v, name):
    rows, width = w.shape
    tr = max(t for t in range(8, 257, 8) if rows % t == 0)

    def body(w_ref, g_ref, m_ref, v_ref, d_ref, mo_ref, vo_ref):
        gv = g_ref[...]
        mn = ADAM_B1 * m_ref[...] + (1.0 - ADAM_B1) * gv
        vn = ADAM_B2 * v_ref[...] + (1.0 - ADAM_B2) * (gv * gv)
        mo_ref[...] = mn
        vo_ref[...] = vn
        m_hat = mn / (1.0 - ADAM_B1 ** ADAM_STEP)
        v_hat = vn / (1.0 - ADAM_B2 ** ADAM_STEP)
        d_ref[...] = -ADAM_LR * (m_hat / (jnp.sqrt(v_hat) + ADAM_EPS) + ADAM_WD * w_ref[...])

    spec = pl.BlockSpec((tr, width), lambda i: (i, 0))
    return pl.pallas_call(
        body, name=name, grid=(rows // tr,), in_specs=[spec] * 4, out_specs=[spec] * 3,
        out_shape=[jax.ShapeDtypeStruct((rows, width), F32)] * 3,
        compiler_params=_params(dimension_semantics=("arbitrary",)),
    )(w, g, m, v)


def _local_shape(name):
    if name in BIG:
        r, c = BIG_SHAPE[name]
        return (r, c // N_DEV) if BIG_AXIS[name] == 1 else (r // N_DEV, c)
    r, c = SMALL_SHAPE[name]
    return (r, c // N_DEV)


def _size(shape):
    return math.prod(shape)


BIG_LOCAL = sum(_size(_local_shape(nm)) for nm in BIG) * DEPTH
SMALL_LOCAL = sum(_size(_local_shape(nm)) for nm in SMALL_SHARDED) * DEPTH
SMALL_ROWS = 8
SHARD_ROWS = 2912
BLOB_ROWS = SHARD_ROWS + REP_ROWS
REP_SHAPE = {"conv_a_b": (CONV_W,), "ln_a_g": (CONV_W,), "ln_a_b": (CONV_W,), "ln_v_g": (GMLP_W,), "ln_v_b": (GMLP_W,),
             "w_s": (GMLP_HEADS, CHUNK, CHUNK), "b_s": (GMLP_HEADS, CHUNK), "ln1_g": (D_MODEL,), "ln1_b": (D_MODEL,),
             "conv_f_b": (D_FF,), "ln2_g": (D_MODEL,), "ln2_b": (D_MODEL,)}
REP_TOTAL = sum(_size(s) for s in REP_SHAPE.values()) * DEPTH
assert BIG_LOCAL % BLOB_LANES == 0 and (BIG_LOCAL // BLOB_LANES) % 16 == 0
assert SMALL_LOCAL <= SMALL_ROWS * BLOB_LANES
assert BIG_LOCAL + SMALL_LOCAL <= SHARD_ROWS * BLOB_LANES and BLOB_ROWS % ADD_ROWS == 0
assert REP_TOTAL <= N_DEV * REP_ROWS * BLOB_LANES


def _pad_to(v, n):
    return jnp.pad(v, [(0, 0)] * (v.ndim - 1) + [(0, n - v.shape[-1])])


def _shard_blob(tree):
    parts = [tree[nm][l].reshape(-1) for l in range(DEPTH) for nm in BIG]
    parts += [tree[nm].reshape(-1) for nm in SMALL_SHARDED]
    return _pad_to(jnp.concatenate(parts), SHARD_ROWS * BLOB_LANES).reshape(SHARD_ROWS, BLOB_LANES)


def _shard_unblob(blob):
    flat = blob.reshape(-1)
    out, off = {}, 0
    per_layer = {nm: [] for nm in BIG}
    for l in range(DEPTH):
        for nm in BIG:
            shp = _local_shape(nm)
            per_layer[nm].append(flat[off:off + _size(shp)].reshape(shp))
            off += _size(shp)
    for nm in BIG:
        out[nm] = jnp.stack(per_layer[nm])
    for nm in SMALL_SHARDED:
        shp = (DEPTH,) + _local_shape(nm)
        out[nm] = flat[off:off + _size(shp)].reshape(shp)
        off += _size(shp)
    return out


def _rep_blob(tree):
    flat = jnp.concatenate([tree[nm].reshape(-1) for nm in REPLICATED])
    return _pad_to(flat, N_DEV * REP_ROWS * BLOB_LANES).reshape(N_DEV * REP_ROWS, BLOB_LANES)


def _rep_unblob(blob):
    flat = blob.reshape(-1)
    out, off = {}, 0
    for nm in REPLICATED:
        shp = (DEPTH,) + REP_SHAPE[nm]
        out[nm] = flat[off:off + _size(shp)].reshape(shp)
        off += _size(shp)
    return out


def _scatter_layout(grads):
    parts = []
    for l in range(DEPTH):
        for nm in BIG:
            g = grads[nm][l]
            r, c = g.shape
            if BIG_AXIS[nm] == 1:
                g = g.reshape(r, N_DEV, c // N_DEV).transpose(1, 0, 2)
            parts.append(g.reshape(N_DEV, -1))
    for nm in SMALL_SHARDED:
        g = grads[nm]
        k, c = g.shape[1:]
        parts.append(g.reshape(DEPTH, k, N_DEV, c // N_DEV).transpose(2, 0, 1, 3).reshape(N_DEV, -1))
    sharded = _pad_to(jnp.concatenate(parts, axis=1), SHARD_ROWS * BLOB_LANES).reshape(N_DEV, SHARD_ROWS, BLOB_LANES)
    rep = _rep_blob(grads).reshape(N_DEV, REP_ROWS, BLOB_LANES)
    return jnp.concatenate([sharded, rep], axis=1)


def _gathered_weights(big, small):
    flat = big.reshape(N_DEV, -1)
    layers, off = [], 0
    for l in range(DEPTH):
        full = {}
        for nm in BIG:
            shp = _local_shape(nm)
            piece = flat[:, off:off + _size(shp)].reshape((N_DEV,) + shp)
            off += _size(shp)
            if BIG_AXIS[nm] == 1:
                piece = piece.transpose(1, 0, 2)
            full[nm] = piece.reshape(BIG_SHAPE[nm])
        layers.append(full)
    sflat = small.reshape(N_DEV, -1)
    off = 0
    for nm in SMALL_SHARDED:
        k, c = SMALL_SHAPE[nm]
        n_el = DEPTH * k * (c // N_DEV)
        piece = sflat[:, off:off + n_el].reshape(N_DEV, DEPTH, k, c // N_DEV).transpose(1, 2, 0, 3).reshape(DEPTH, k, c)
        off += n_el
        for l in range(DEPTH):
            layers[l][nm] = piece[l]
    return layers


def _head_table():
    hd = jnp.arange(XATTN_W) // HEAD_DIM
    return (hd[None, :] == jnp.arange(XATTN_HEADS)[:, None]).astype(F32)


def _layer_operands(full, p, l, mem2):
    pad_ff = lambda a: _pad_to(a, D_FF_PAD)
    w = {}
    w["win"] = full["w_in"]
    w["wint"] = full["w_in"].T
    w["wout"] = full["w_out"]
    w["woutt"] = full["w_out"].T
    w["wg"] = pad_ff(full["w_up"][:, :D_FF])
    w["wv"] = pad_ff(full["w_up"][:, D_FF:])
    w["wgt"] = w["wg"].T
    w["wvt"] = w["wv"].T
    w["wdownt"] = pad_ff(full["w_down"].T)
    w["wdown"] = w["wdownt"].T
    w["cw"] = jnp.pad(full["conv_a_w"], ((0, HALO - CONV_K), (0, 0)))
    zeros = jnp.zeros((3, CONV_W), F32)
    w["va"] = jnp.concatenate([p["conv_a_b"][l][None], p["ln_a_g"][l][None], p["ln_a_b"][l][None], p["ln_v_g"][l][None],
                               p["ln_v_b"][l][None], zeros], axis=0)
    tril = jnp.tril(jnp.ones((CHUNK, CHUNK), F32))
    w["wcat"] = (p["w_s"][l] * tril[None]).transpose(1, 0, 2).reshape(CHUNK, GMLP_HEADS * CHUNK).astype(BF16)
    w["wcatt"] = w["wcat"].T
    w["bfull"] = jnp.repeat(p["b_s"][l].T, HEAD_DIM, axis=1)
    kh = _small_mm_call(mem2, full["w_mk"], False, "mem_k")
    vh = _small_mm_call(mem2, full["w_mv"], False, "mem_v")
    hm = _head_table()
    scale = 1.0 / math.sqrt(HEAD_DIM)
    w["kt"] = (kh.T[:, None, :] * hm.T[:, :, None] * scale).reshape(XATTN_W, XATTN_HEADS * N_MEM).astype(BF16)
    w["ktt"] = w["kt"].T
    w["vm"] = (hm[:, None, :] * vh[None]).reshape(XATTN_HEADS * N_MEM, XATTN_W).astype(BF16)
    w["vmt"] = w["vm"].T
    zeros = jnp.zeros((6, D_MODEL), F32)
    w["v1"] = jnp.concatenate([p["ln1_g"][l][None], p["ln1_b"][l][None], zeros], axis=0)
    w["v2"] = jnp.concatenate([p["ln2_g"][l][None], p["ln2_b"][l][None], zeros], axis=0)
    w["cf"] = jnp.concatenate([pad_ff(full["conv_f_w"]), pad_ff(p["conv_f_b"][l][None]), jnp.zeros((4, D_FF_PAD), F32)],
                              axis=0)
    return w


TS_MIXER = 256
TS_FFN = 256
TS_PROJ = 512


def kernel(x, mem, w_in, conv_a_w, conv_a_b, ln_a_g, ln_a_b, ln_v_g, ln_v_b, w_s, b_s, w_mk, w_mv, w_out, ln1_g, ln1_b, w_up, conv_f_w, conv_f_b, w_down, ln2_g, ln2_b, loss_target, m_w_in, m_conv_a_w, m_conv_a_b, m_ln_a_g, m_ln_a_b, m_ln_v_g, m_ln_v_b, m_w_s, m_b_s, m_w_mk, m_w_mv, m_w_out, m_ln1_g, m_ln1_b, m_w_up, m_conv_f_w, m_conv_f_b, m_w_down, m_ln2_g, m_ln2_b, v_w_in, v_conv_a_w, v_conv_a_b, v_ln_a_g, v_ln_a_b, v_ln_v_g, v_ln_v_b, v_w_s, v_b_s, v_w_mk, v_w_mv, v_w_out, v_ln1_g, v_ln1_b, v_w_up, v_conv_f_w, v_conv_f_b, v_w_down, v_ln2_g, v_ln2_b):
    given = dict(locals())
    p = {nm: given[nm] for nm in WEIGHTS}
    mom_m = {nm: given["m_" + nm] for nm in WEIGHTS}
    mom_v = {nm: given["v_" + nm] for nm in WEIGHTS}
    seq = x.shape[1]
    ts_m, ts_f, ts_p = min(TS_MIXER, seq), min(TS_FFN, seq), min(TS_PROJ, seq)
    cx, cy, cc = _place()

    w_blob = _shard_blob(p)
    big_local = w_blob.reshape(-1)[:BIG_LOCAL].astype(BF16).reshape(BIG_LOCAL // BLOB_LANES, BLOB_LANES)
    small_local = _pad_to(w_blob.reshape(-1)[BIG_LOCAL:BIG_LOCAL + SMALL_LOCAL], SMALL_ROWS * BLOB_LANES)
    big_all, small_all = _all_gather_call([big_local, small_local.reshape(SMALL_ROWS, BLOB_LANES)], "gather_weights")
    full = _gathered_weights(big_all, small_all)
    mem2 = mem[0]
    ops = [_layer_operands(full[l], p, l, mem2) for l in range(DEPTH)]

    saved = []
    xl = x[0]
    for l in range(DEPTH):
        hb, z1, x1 = _mixer_fwd_call(xl, ops[l], ts_m)
        ug, uv, z2, x2 = _ffn_fwd_call(x1, ops[l], ts_f)
        saved.append(dict(x=xl, hb=hb, z1=z1, x1=x1, ug=ug, uv=uv, z2=z2))
        xl = x2

    hm = _head_table()
    tril = jnp.tril(jnp.ones((CHUNK, CHUNK), F32))
    grads = {nm: [None] * DEPTH for nm in WEIGHTS}
    dx = loss_target[0]
    loss = None
    for l in reversed(range(DEPTH)):
        s, w = saved[l], ops[l]
        last = l == DEPTH - 1
        dug, duv, dz2, dwdown, dcf, dv2, loss_acc = _ffn_bwd_call(dx, s["z2"], s["ug"], s["uv"], w, ts_f, last)
        if last:
            loss = loss_acc[0, 0]
        dxa, dwg = _proj_bwd_call(dug, w["wgt"], s["x1"], dz2, ALPHA, ts_p, "up_gate_bwd")
        dx1, dwv = _proj_bwd_call(duv, w["wvt"], s["x1"], dxa, 1.0, ts_p, "up_val_bwd")
        (dh, dz1, dwout, dkt, dvm, dwcat, dmsum, dva, dcw, dv1) = _mixer_bwd_call(dx1, s["z1"], s["hb"], w, ts_m)
        dx, dwin = _proj_bwd_call(dh, w["wint"], s["x"], dz1, ALPHA, ts_p, "in_proj_bwd")
        dkh = jnp.einsum("hd,dhm->md", hm, dkt.reshape(XATTN_W, XATTN_HEADS, N_MEM)) * (1.0 / math.sqrt(HEAD_DIM))
        dvh = jnp.einsum("hd,hmd->md", hm, dvm.reshape(XATTN_HEADS, N_MEM, XATTN_W))
        g = grads
        g["w_in"][l] = dwin
        g["w_mk"][l] = _small_mm_call(mem2, dkh, True, "mem_k_bwd")
        g["w_mv"][l] = _small_mm_call(mem2, dvh, True, "mem_v_bwd")
        g["w_out"][l] = dwout
        g["w_up"][l] = jnp.concatenate([dwg[:, :D_FF], dwv[:, :D_FF]], axis=1)
        g["w_down"][l] = dwdown[:D_FF]
        g["conv_a_w"][l] = dcw[:CONV_K]
        g["conv_a_b"][l], g["ln_a_g"][l], g["ln_a_b"][l], g["ln_v_g"][l], g["ln_v_b"][l] = (dva[k] for k in range(5))
        g["w_s"][l] = dwcat.reshape(CHUNK, GMLP_HEADS, CHUNK).transpose(1, 0, 2) * tril[None]
        g["b_s"][l] = dmsum.reshape(CHUNK, GMLP_HEADS, HEAD_DIM).sum(-1).T
        g["ln1_g"][l], g["ln1_b"][l] = dv1[0], dv1[1]
        g["conv_f_w"][l] = dcf[:FFN_CONV_K, :D_FF]
        g["conv_f_b"][l] = dcf[FFN_CONV_K, :D_FF]
        g["ln2_g"][l], g["ln2_b"][l] = dv2[0], dv2[1]
    grads = {nm: jnp.stack(v) for nm, v in grads.items()}
    grad_x = dx[None]

    slabs = _scatter_layout(grads)
    from_sibling = _swap_core_call(slabs)
    chip_sum = _pair_add_call(slabs, from_sibling, cc.reshape(1).astype(jnp.int32))
    from_chips = _swap_chip_call(chip_sum)
    total = _chip_add_call(chip_sum, from_chips, (2 * cx + cy).reshape(1).astype(jnp.int32))
    g_shard = total[:SHARD_ROWS]
    (rep_all,) = _all_gather_call([total[SHARD_ROWS:]], "gather_replicated_grads")
    g_rep = rep_all.reshape(N_DEV * REP_ROWS, BLOB_LANES)

    d_sh, m_sh, v_sh = _adamw_call(w_blob, g_shard, _shard_blob(mom_m), _shard_blob(mom_v), "adamw_sharded")
    d_rp, m_rp, v_rp = _adamw_call(_rep_blob(p), g_rep, _rep_blob(mom_m), _rep_blob(mom_v), "adamw_replicated")
    out_g = {**_shard_unblob(g_shard), **_rep_unblob(g_rep)}
    out_d = {**_shard_unblob(d_sh), **_rep_unblob(d_rp)}
    out_m = {**_shard_unblob(m_sh), **_rep_unblob(m_rp)}
    out_v = {**_shard_unblob(v_sh), **_rep_unblob(v_rp)}

    loss = lax.psum(loss, ("x", "y", "c"))
    return (loss, grad_x, *[out_g[nm] for nm in WEIGHTS], *[out_d[nm] for nm in WEIGHTS],
            *[out_m[nm] for nm in WEIGHTS], *[out_v[nm] for nm in WEIGHTS])
```

```python
import math

import jax
import jax.numpy as jnp
from jax import lax
from jax.experimental import pallas as pl
from jax.experimental.pallas import tpu as pltpu

F32 = jnp.float32
BF16 = jnp.bfloat16

DEPTH = 2
D_MODEL = 1024
CONV_W = 384
GMLP_W = 384
XATTN_W = 256
HEAD_DIM = 64
GMLP_HEADS = 6
XATTN_HEADS = 4
IN_W = 1792
CONV_K = 31
CHUNK = 128
N_MEM = 256
D_FF = 2752
D_FF_PAD = 2816
FFN_CONV_K = 3
ALPHA = (2.0 * DEPTH) ** 0.25
LN_EPS = 1e-5
N_DEV = 8

ADAM_LR = 0.001
ADAM_B1 = 0.9
ADAM_B2 = 0.999
ADAM_EPS = 1e-08
ADAM_WD = 0.01
ADAM_STEP = 10

HALO = 32
CONV_ROWS = 32
V7X_VMEM_BYTES = 64 * 1024 * 1024
VMEM_LIMIT = V7X_VMEM_BYTES - 8 * 1024 * 1024
BLOB_LANES = 1024

MESH = pl.DeviceIdType.MESH

WEIGHTS = ("w_in", "conv_a_w", "conv_a_b", "ln_a_g", "ln_a_b", "ln_v_g", "ln_v_b", "w_s", "b_s", "w_mk", "w_mv",
           "w_out", "ln1_g", "ln1_b", "w_up", "conv_f_w", "conv_f_b", "w_down", "ln2_g", "ln2_b")


def _params(**kw):
    return pltpu.CompilerParams(vmem_limit_bytes=VMEM_LIMIT, **kw)


def _const(shape):
    nd = len(shape)
    return pl.BlockSpec(shape, lambda i: (0,) * nd, pipeline_mode=pl.Buffered(1))


def _acc(shape):
    nd = len(shape)
    return pl.BlockSpec(shape, lambda i: (0,) * nd)


def _sigmoid(x):
    return 1.0 / (1.0 + jnp.exp(-x))


_GELU_C = math.sqrt(2.0 / math.pi)


def _gelu(x):
    x2 = x * x
    t = jnp.tanh(_GELU_C * (x + 0.044715 * x * x2))
    g = 0.5 * x * (1.0 + t)
    dg = 0.5 * (1.0 + t) + 0.5 * x * (1.0 - t * t) * (_GELU_C * (1.0 + 3.0 * 0.044715 * x2))
    return g, dg


def _ln_stats(z):
    mu = jnp.mean(z, axis=-1, keepdims=True)
    zc = z - mu
    var = jnp.mean(zc * zc, axis=-1, keepdims=True)
    r = lax.rsqrt(var + LN_EPS)
    return zc * r, r


def _ln_bwd(dy, xh, r, g):
    dxh = dy * g
    m1 = jnp.mean(dxh, axis=-1, keepdims=True)
    m2 = jnp.mean(dxh * xh, axis=-1, keepdims=True)
    return r * (dxh - m1 - xh * m2)


def _rowsum(x):
    return jnp.sum(x, axis=0, keepdims=True)


def _dot(a, b):
    return jnp.dot(a, b, preferred_element_type=F32)


def _dot_tn(a, b):
    return lax.dot_general(a, b, (((0,), (0,)), ((), ())), preferred_element_type=F32)


def _dot_nt(a, b):
    return lax.dot_general(a, b, (((1,), (1,)), ((), ())), preferred_element_type=F32)


def _shift_copies(buf, sh, rows):
    for b in range(1, 8):
        sh[b - 1, 0:rows, :] = buf[b:b + rows, :]


def _window(buf, sh, start):
    b = start % 8
    a = start - b
    return buf[a:a + CONV_ROWS, :] if b == 0 else sh[b - 1, a:a + CONV_ROWS, :]


def _conv31_fwd(buf, sh, w_ref, bias, out, ts):
    for r0 in range(0, ts, CONV_ROWS):
        acc = jnp.broadcast_to(bias, (CONV_ROWS, CONV_W))
        for k in range(CONV_K):
            acc = acc + w_ref[k:k + 1, :] * _window(buf, sh, r0 + HALO - (CONV_K - 1) + k)
        out[r0:r0 + CONV_ROWS, :] = acc


def _conv31_dx(dbuf, dsh, w_ref, out, ts):
    for r0 in range(0, ts, CONV_ROWS):
        acc = jnp.zeros((CONV_ROWS, CONV_W), F32)
        for k in range(CONV_K):
            acc = acc + w_ref[k:k + 1, :] * _window(dbuf, dsh, r0 + (CONV_K - 1) - k)
        out[r0:r0 + CONV_ROWS, :] = acc


def _conv31_dw(buf, sh, dbuf, dw_ref, ts):
    for k in range(CONV_K):
        part = jnp.zeros((8, CONV_W), F32)
        for r0 in range(0, ts, CONV_ROWS):
            m = dbuf[r0:r0 + CONV_ROWS, :] * _window(buf, sh, r0 + HALO - (CONV_K - 1) + k)
            for q in range(0, CONV_ROWS, 8):
                part = part + m[q:q + 8, :]
        dw_ref[k:k + 1, :] += _rowsum(part)


def _head_mask(width, h):
    lane = lax.broadcasted_iota(jnp.int32, (CHUNK, width), 1)
    return (lane >= h * HEAD_DIM) & (lane < (h + 1) * HEAD_DIM)


def _stack_heads(vn_c):
    return jnp.concatenate([jnp.where(_head_mask(GMLP_W, h), vn_c, 0.0) for h in range(GMLP_HEADS)], axis=0)


def _group_a_fwd(hf, buf, sh, a1buf, cw_ref, va_ref, ts):
    ha = hf[:, 0:CONV_W]
    sg = _sigmoid(hf[:, CONV_W:2 * CONV_W])
    buf[HALO:HALO + ts, :] = ha * sg
    _shift_copies(buf, sh, ts + HALO - 8)
    _conv31_fwd(buf, sh, cw_ref, va_ref[0:1, :], a1buf, ts)
    a2h, ra = _ln_stats(a1buf[...])
    a2 = a2h * va_ref[1:2, :] + va_ref[2:3, :]
    sa = _sigmoid(a2)
    return dict(ha=ha, sg=sg, a2h=a2h, ra=ra, a2=a2, sa=sa, a=a2 * sa)


def _group_b_fwd(hf, va_ref, wcat_ref, bfull_ref, ts):
    hu = hf[:, 2 * CONV_W:2 * CONV_W + GMLP_W]
    hv = hf[:, 2 * CONV_W + GMLP_W:2 * CONV_W + 2 * GMLP_W]
    u, du = _gelu(hu)
    v, dv = _gelu(hv)
    vhat, rv = _ln_stats(v)
    vn = vhat * va_ref[3:4, :] + va_ref[4:5, :]
    stacks, mixed = [], []
    for c0 in range(0, ts, CHUNK):
        st = _stack_heads(vn[c0:c0 + CHUNK, :]).astype(BF16)
        stacks.append(st)
        mixed.append(_dot(wcat_ref[...], st) + bfull_ref[...])
    mixed = jnp.concatenate(mixed, axis=0) if len(mixed) > 1 else mixed[0]
    return dict(u=u, du=du, dv=dv, vhat=vhat, rv=rv, stacks=stacks, mixed=mixed, g=u * mixed)


def _group_c_fwd(hf, kt_ref, vm_ref):
    qb = hf[:, IN_W - XATTN_W:IN_W].astype(BF16)
    s_all = _dot(qb, kt_ref[...])
    ps = []
    for g in range(XATTN_HEADS):
        s = s_all[:, g * N_MEM:(g + 1) * N_MEM]
        e = jnp.exp(s - jnp.max(s, axis=-1, keepdims=True))
        ps.append(e / jnp.sum(e, axis=-1, keepdims=True))
    p_all = jnp.concatenate(ps, axis=1)
    pb = p_all.astype(BF16)
    return dict(qb=qb, p=p_all, pb=pb, c=_dot(pb, vm_ref[...]))


def _mixer_fwd_call(x, w, ts):
    seq = x.shape[0]
    n = seq // ts

    def body(x_ref, win_ref, cw_ref, va_ref, wcat_ref, bfull_ref, kt_ref, vm_ref, wout_ref, v1_ref,
             hb_ref, z1_ref, x1_ref, buf, a1buf, sh):
        i = pl.program_id(0)

        @pl.when(i == 0)
        def _():
            buf[0:HALO, :] = jnp.zeros((HALO, CONV_W), F32)

        @pl.when(i > 0)
        def _():
            buf[0:HALO, :] = buf[ts:ts + HALO, :]

        xv = x_ref[...]
        hb = _dot(xv.astype(BF16), win_ref[...]).astype(BF16)
        hb_ref[...] = hb
        hf = hb.astype(F32)
        ga = _group_a_fwd(hf, buf, sh, a1buf, cw_ref, va_ref, ts)
        gb = _group_b_fwd(hf, va_ref, wcat_ref, bfull_ref, ts)
        gc = _group_c_fwd(hf, kt_ref, vm_ref)
        cat = jnp.concatenate([ga["a"], gb["g"], gc["c"]], axis=1).astype(BF16)
        z1 = ALPHA * xv + _dot(cat, wout_ref[...])
        z1_ref[...] = z1
        xh, _ = _ln_stats(z1)
        x1_ref[...] = xh * v1_ref[0:1, :] + v1_ref[1:2, :]

    row = lambda width: pl.BlockSpec((ts, width), lambda i: (i, 0))
    return pl.pallas_call(
        body, name="mixer_fwd", grid=(n,),
        in_specs=[row(D_MODEL), _const((D_MODEL, IN_W)), _const((HALO, CONV_W)), _const((8, CONV_W)),
                  _const((CHUNK, GMLP_HEADS * CHUNK)), _const((CHUNK, GMLP_W)), _const((XATTN_W, XATTN_HEADS * N_MEM)),
                  _const((XATTN_HEADS * N_MEM, XATTN_W)), _const((D_MODEL, D_MODEL)), _const((8, D_MODEL))],
        out_specs=[row(IN_W), row(D_MODEL), row(D_MODEL)],
        out_shape=[jax.ShapeDtypeStruct((seq, IN_W), BF16), jax.ShapeDtypeStruct((seq, D_MODEL), F32),
                   jax.ShapeDtypeStruct((seq, D_MODEL), F32)],
        scratch_shapes=[pltpu.VMEM((ts + HALO, CONV_W), F32), pltpu.VMEM((ts, CONV_W), F32),
                        pltpu.VMEM((7, ts + HALO, CONV_W), F32)],
        compiler_params=_params(dimension_semantics=("arbitrary",)),
    )(x, w["win"], w["cw"], w["va"], w["wcat"], w["bfull"], w["kt"], w["vm"], w["wout"], w["v1"])


def _store_blocks(acc, slabs_ref, sems, row_off, rows, first_block, n_blocks):
    copies = [pltpu.make_async_copy(acc.at[pl.ds(q * rows, rows)], slabs_ref.at[first_block + q, pl.ds(row_off, rows)],
                                    sems.at[q]) for q in range(n_blocks)]
    for cp in copies:
        cp.start()
    for cp in copies:
        cp.wait()


def _mixer_bwd_call(dx1, z1, hb, w, ts, slabs, row_off):
    seq = dx1.shape[0]
    n = seq // ts
    halo_blocks = ts // HALO

    def body(slabs_in, dx1_ref, z1_ref, hb_ref, hprev_ref, cw_ref, va_ref, wcat_ref, wcatt_ref, bfull_ref, kt_ref,
             ktt_ref, vm_ref, vmt_ref, woutt_ref, v1_ref,
             slabs_ref, dh_ref, dz1_ref, dkt_ref, dvm_ref, dwcat_ref, dmsum_ref, dva_ref, dcw_ref, dv1_ref,
             buf, a1buf, dbuf, da0buf, dwout_ref, sems, sh, dsh):
        i = pl.program_id(0)

        @pl.when(i == 0)
        def _():
            for ref in (dwout_ref, dkt_ref, dvm_ref, dwcat_ref, dmsum_ref, dva_ref, dcw_ref, dv1_ref):
                ref[...] = jnp.zeros(ref.shape, F32)
            dbuf[ts:ts + HALO, :] = jnp.zeros((HALO, CONV_W), F32)

        @pl.when(i > 0)
        def _():
            dbuf[ts:ts + HALO, :] = dbuf[0:HALO, :]

        dx1v = dx1_ref[...]
        xh1, r1 = _ln_stats(z1_ref[...])
        dv1_ref[0:1, :] += _rowsum(dx1v * xh1)
        dv1_ref[1:2, :] += _rowsum(dx1v)
        dz1 = _ln_bwd(dx1v, xh1, r1, v1_ref[0:1, :])
        dz1_ref[...] = dz1
        dmix = dz1.astype(BF16)

        hf = hb_ref[...].astype(F32)
        hp = hprev_ref[...].astype(F32)
        a0p = hp[:, 0:CONV_W] * _sigmoid(hp[:, CONV_W:2 * CONV_W])
        buf[0:HALO, :] = jnp.where(i == n - 1, 0.0, a0p)
        ga = _group_a_fwd(hf, buf, sh, a1buf, cw_ref, va_ref, ts)
        gb = _group_b_fwd(hf, va_ref, wcat_ref, bfull_ref, ts)
        gc = _group_c_fwd(hf, kt_ref, vm_ref)
        cat = jnp.concatenate([ga["a"], gb["g"], gc["c"]], axis=1).astype(BF16)

        dwout_ref[...] += _dot_tn(cat, dmix)
        dcat = _dot(dmix, woutt_ref[...])
        da = dcat[:, 0:CONV_W]
        dg = dcat[:, CONV_W:CONV_W + GMLP_W]
        dc = dcat[:, CONV_W + GMLP_W:D_MODEL].astype(BF16)

        dp = _dot(dc, vmt_ref[...])
        dvm_ref[...] += _dot_tn(gc["pb"], dc)
        dss = []
        for g in range(XATTN_HEADS):
            sl = slice(g * N_MEM, (g + 1) * N_MEM)
            pg = gc["p"][:, sl]
            dpg = dp[:, sl]
            dss.append(pg * (dpg - jnp.sum(dpg * pg, axis=-1, keepdims=True)))
        ds = jnp.concatenate(dss, axis=1).astype(BF16)
        dq = _dot(ds, ktt_ref[...])
        dkt_ref[...] += _dot_tn(gc["qb"], ds)

        dmixed = dg * gb["u"]
        dhu = dg * gb["mixed"] * gb["du"]
        dvns = []
        for j, c0 in enumerate(range(0, ts, CHUNK)):
            dm = dmixed[c0:c0 + CHUNK, :]
            dmb = dm.astype(BF16)
            dmsum_ref[...] += dm
            dwcat_ref[...] += _dot_nt(dmb, gb["stacks"][j])
            dst = _dot(wcatt_ref[...], dmb)
            dvn_c = jnp.zeros((CHUNK, GMLP_W), F32)
            for h in range(GMLP_HEADS):
                dvn_c = dvn_c + jnp.where(_head_mask(GMLP_W, h), dst[h * CHUNK:(h + 1) * CHUNK, :], 0.0)
            dvns.append(dvn_c)
        dvn = jnp.concatenate(dvns, axis=0) if len(dvns) > 1 else dvns[0]
        dva_ref[3:4, :] += _rowsum(dvn * gb["vhat"])
        dva_ref[4:5, :] += _rowsum(dvn)
        dhv = _ln_bwd(dvn, gb["vhat"], gb["rv"], va_ref[3:4, :]) * gb["dv"]

        a2, sa = ga["a2"], ga["sa"]
        da2 = da * (sa * (1.0 + a2 * (1.0 - sa)))
        dva_ref[1:2, :] += _rowsum(da2 * ga["a2h"])
        dva_ref[2:3, :] += _rowsum(da2)
        da1 = _ln_bwd(da2, ga["a2h"], ga["ra"], va_ref[1:2, :])
        dva_ref[0:1, :] += _rowsum(da1)
        dbuf[0:ts, :] = da1
        _shift_copies(dbuf, dsh, ts + HALO - 8)
        _conv31_dw(buf, sh, dbuf, dcw_ref, ts)
        _conv31_dx(dbuf, dsh, cw_ref, da0buf, ts)
        da0 = da0buf[...]
        sg = ga["sg"]
        dha = da0 * sg
        dhg = da0 * ga["ha"] * sg * (1.0 - sg)

        dh_ref[...] = jnp.concatenate([dha, dhg, dhu, dhv, dq], axis=1).astype(BF16)

        @pl.when(i == n - 1)
        def _():
            _store_blocks(dwout_ref, slabs_ref, sems, row_off, D_MODEL // N_DEV, 0, N_DEV)

    rev = lambda width: pl.BlockSpec((ts, width), lambda i: (n - 1 - i, 0))
    prev = pl.BlockSpec((HALO, 2 * CONV_W), lambda i: (jnp.maximum((n - 1 - i) * halo_blocks - 1, 0), 0))
    hbm = pl.BlockSpec(memory_space=pl.ANY)
    hc = GMLP_HEADS * CHUNK
    am = XATTN_HEADS * N_MEM
    return pl.pallas_call(
        body, name="mixer_bwd", grid=(n,),
        in_specs=[hbm, rev(D_MODEL), rev(D_MODEL), rev(IN_W), prev, _const((HALO, CONV_W)), _const((8, CONV_W)),
                  _const((CHUNK, hc)), _const((hc, CHUNK)), _const((CHUNK, GMLP_W)), _const((XATTN_W, am)),
                  _const((am, XATTN_W)), _const((am, XATTN_W)), _const((XATTN_W, am)), _const((D_MODEL, D_MODEL)),
                  _const((8, D_MODEL))],
        out_specs=[hbm, rev(IN_W), rev(D_MODEL), _acc((XATTN_W, am)), _acc((am, XATTN_W)),
                   _acc((CHUNK, hc)), _acc((CHUNK, GMLP_W)), _acc((8, CONV_W)), _acc((HALO, CONV_W)),
                   _acc((8, D_MODEL))],
        out_shape=[jax.ShapeDtypeStruct(slabs.shape, F32),
                   jax.ShapeDtypeStruct((seq, IN_W), BF16), jax.ShapeDtypeStruct((seq, D_MODEL), F32),
                   jax.ShapeDtypeStruct((XATTN_W, am), F32),
                   jax.ShapeDtypeStruct((am, XATTN_W), F32), jax.ShapeDtypeStruct((CHUNK, hc), F32),
                   jax.ShapeDtypeStruct((CHUNK, GMLP_W), F32), jax.ShapeDtypeStruct((8, CONV_W), F32),
                   jax.ShapeDtypeStruct((HALO, CONV_W), F32), jax.ShapeDtypeStruct((8, D_MODEL), F32)],
        scratch_shapes=[pltpu.VMEM((ts + HALO, CONV_W), F32), pltpu.VMEM((ts, CONV_W), F32),
                        pltpu.VMEM((ts + HALO, CONV_W), F32), pltpu.VMEM((ts, CONV_W), F32),
                        pltpu.VMEM((D_MODEL, D_MODEL), F32), pltpu.SemaphoreType.DMA((N_DEV,)),
                        pltpu.VMEM((7, ts + HALO, CONV_W), F32), pltpu.VMEM((7, ts + HALO, CONV_W), F32)],
        input_output_aliases={0: 0},
        compiler_params=_params(dimension_semantics=("arbitrary",)),
    )(slabs, dx1, z1, hb, hb, w["cw"], w["va"], w["wcat"], w["wcatt"], w["bfull"], w["kt"], w["ktt"], w["vm"],
      w["vmt"], w["woutt"], w["v1"])


FFN_HALO = 8


def _ffn_gate(ubuf, cf_ref, ts, lo, hi):
    taps = tuple(ubuf[FFN_HALO - (FFN_CONV_K - 1) + k:FFN_HALO - (FFN_CONV_K - 1) + k + ts, lo:hi]
                 for k in range(FFN_CONV_K))
    g = cf_ref[3:4, lo:hi] + cf_ref[2:3, lo:hi] * taps[2]
    g = g + cf_ref[1:2, lo:hi] * taps[1]
    return g + cf_ref[0:1, lo:hi] * taps[0], taps


FFN_LANE_CHUNKS = ((0, D_FF_PAD // 2), (D_FF_PAD // 2, D_FF_PAD))


def _ffn_fwd_call(x1, w, ts):
    seq = x1.shape[0]
    n = seq // ts

    def body(x1_ref, wg_ref, wv_ref, cf_ref, wdown_ref, v2_ref, ug_ref, uv_ref, z2_ref, x2_ref, ubuf):
        i = pl.program_id(0)

        @pl.when(i == 0)
        def _():
            ubuf[0:FFN_HALO, :] = jnp.zeros((FFN_HALO, D_FF_PAD), F32)

        @pl.when(i > 0)
        def _():
            ubuf[0:FFN_HALO, :] = ubuf[ts:ts + FFN_HALO, :]

        xv = x1_ref[...]
        xb = xv.astype(BF16)
        y = ALPHA * xv
        for lo, hi in FFN_LANE_CHUNKS:
            ug = _dot(xb, wg_ref[:, lo:hi]).astype(BF16)
            uv = _dot(xb, wv_ref[:, lo:hi]).astype(BF16)
            ug_ref[:, lo:hi] = ug
            uv_ref[:, lo:hi] = uv
            ubuf[FFN_HALO:FFN_HALO + ts, lo:hi] = ug.astype(F32)
            gate, _ = _ffn_gate(ubuf, cf_ref, ts, lo, hi)
            act = (gate * _sigmoid(gate) * uv.astype(F32)).astype(BF16)
            y = y + _dot(act, wdown_ref[lo:hi, :])
        z2_ref[...] = y
        xh, _ = _ln_stats(y)
        x2_ref[...] = xh * v2_ref[0:1, :] + v2_ref[1:2, :]

    row = lambda width: pl.BlockSpec((ts, width), lambda i: (i, 0))
    return pl.pallas_call(
        body, name="ffn_fwd", grid=(n,),
        in_specs=[row(D_MODEL), _const((D_MODEL, D_FF_PAD)), _const((D_MODEL, D_FF_PAD)), _const((8, D_FF_PAD)),
                  _const((D_FF_PAD, D_MODEL)), _const((8, D_MODEL))],
        out_specs=[row(D_FF_PAD), row(D_FF_PAD), row(D_MODEL), row(D_MODEL)],
        out_shape=[jax.ShapeDtypeStruct((seq, D_FF_PAD), BF16), jax.ShapeDtypeStruct((seq, D_FF_PAD), BF16),
                   jax.ShapeDtypeStruct((seq, D_MODEL), F32), jax.ShapeDtypeStruct((seq, D_MODEL), F32)],
        scratch_shapes=[pltpu.VMEM((ts + FFN_HALO, D_FF_PAD), F32)],
        compiler_params=_params(dimension_semantics=("arbitrary",)),
    )(x1, w["wg"], w["wv"], w["cf"], w["wdown"], w["v2"])


def _ffn_bwd_call(dx2_or_target, z2, ug, uv, w, ts, last, slabs, row_off):
    seq = z2.shape[0]
    n = seq // ts
    halo_blocks = ts // 16

    def body(slabs_in, dx2_ref, z2_ref, ug_ref, uv_ref, uprev_ref, cf_ref, wdownt_ref, v2_ref,
             slabs_ref, dug_ref, duv_ref, dz2_ref, dcf_ref, dv2_ref, loss_ref,
             ubuf, dgbuf, dwacc, sems):
        i = pl.program_id(0)

        @pl.when(i == 0)
        def _():
            dwacc[...] = jnp.zeros(dwacc.shape, F32)
            dcf_ref[...] = jnp.zeros(dcf_ref.shape, F32)
            dv2_ref[...] = jnp.zeros(dv2_ref.shape, F32)
            loss_ref[...] = jnp.zeros(loss_ref.shape, F32)
            dgbuf[ts:ts + FFN_HALO, :] = jnp.zeros((FFN_HALO, D_FF_PAD), F32)

        @pl.when(i > 0)
        def _():
            dgbuf[ts:ts + FFN_HALO, :] = dgbuf[0:FFN_HALO, :]

        xh2, r2 = _ln_stats(z2_ref[...])
        if last:
            diff = xh2 * v2_ref[0:1, :] + v2_ref[1:2, :] - dx2_ref[...]
            loss_ref[...] += jnp.sum(diff * diff) * (0.5 / D_MODEL)
            dx2v = diff * (1.0 / D_MODEL)
        else:
            dx2v = dx2_ref[...]
        dv2_ref[0:1, :] += _rowsum(dx2v * xh2)
        dv2_ref[1:2, :] += _rowsum(dx2v)
        dz2 = _ln_bwd(dx2v, xh2, r2, v2_ref[0:1, :])
        dz2_ref[...] = dz2
        dy = dz2.astype(BF16)

        up = uprev_ref[...].astype(F32)[8:16, :]
        ubuf[0:FFN_HALO, :] = jnp.where(i == n - 1, 0.0, up)
        ubuf[FFN_HALO:FFN_HALO + ts, :] = ug_ref[...].astype(F32)
        for lo, hi in FFN_LANE_CHUNKS:
            gate, taps = _ffn_gate(ubuf, cf_ref, ts, lo, hi)
            sg = _sigmoid(gate)
            sl = gate * sg
            uvf = uv_ref[:, lo:hi].astype(F32)
            act = (sl * uvf).astype(BF16)
            dwacc[lo:hi, :] += _dot_tn(act, dy)
            dact = _dot(dy, wdownt_ref[:, lo:hi])
            duv_ref[:, lo:hi] = (dact * sl).astype(BF16)
            dgate = dact * uvf * (sg * (1.0 + gate * (1.0 - sg)))
            dgbuf[0:ts, lo:hi] = dgate
            dcf_ref[3:4, lo:hi] += _rowsum(dgate)
            for k in range(FFN_CONV_K):
                dcf_ref[k:k + 1, lo:hi] += _rowsum(dgate * taps[k])
            dug = cf_ref[2:3, lo:hi] * dgate + cf_ref[1:2, lo:hi] * dgbuf[1:1 + ts, lo:hi]
            dug = dug + cf_ref[0:1, lo:hi] * dgbuf[2:2 + ts, lo:hi]
            dug_ref[:, lo:hi] = dug.astype(BF16)

        @pl.when(i == n - 1)
        def _():
            _store_blocks(dwacc, slabs_ref, sems, row_off, D_FF_PAD // N_DEV, 0, N_DEV)

    rev = lambda width: pl.BlockSpec((ts, width), lambda i: (n - 1 - i, 0))
    prev = pl.BlockSpec((16, D_FF_PAD), lambda i: (jnp.maximum((n - 1 - i) * halo_blocks - 1, 0), 0))
    hbm = pl.BlockSpec(memory_space=pl.ANY)
    return pl.pallas_call(
        body, name="ffn_bwd_last" if last else "ffn_bwd", grid=(n,),
        in_specs=[hbm, rev(D_MODEL), rev(D_MODEL), rev(D_FF_PAD), rev(D_FF_PAD), prev, _const((8, D_FF_PAD)),
                  _const((D_MODEL, D_FF_PAD)), _const((8, D_MODEL))],
        out_specs=[hbm, rev(D_FF_PAD), rev(D_FF_PAD), rev(D_MODEL),
                   _acc((8, D_FF_PAD)), _acc((8, D_MODEL)), _acc((8, 128))],
        out_shape=[jax.ShapeDtypeStruct(slabs.shape, F32),
                   jax.ShapeDtypeStruct((seq, D_FF_PAD), BF16), jax.ShapeDtypeStruct((seq, D_FF_PAD), BF16),
                   jax.ShapeDtypeStruct((seq, D_MODEL), F32),
                   jax.ShapeDtypeStruct((8, D_FF_PAD), F32), jax.ShapeDtypeStruct((8, D_MODEL), F32),
                   jax.ShapeDtypeStruct((8, 128), F32)],
        scratch_shapes=[pltpu.VMEM((ts + FFN_HALO, D_FF_PAD), F32), pltpu.VMEM((ts + FFN_HALO, D_FF_PAD), F32),
                        pltpu.VMEM((D_FF_PAD, D_MODEL), F32), pltpu.SemaphoreType.DMA((N_DEV,))],
        input_output_aliases={0: 0},
        compiler_params=_params(dimension_semantics=("arbitrary",)),
    )(slabs, dx2_or_target, z2, ug, uv, ug, w["cf"], w["wdownt"], w["v2"])


def _proj_bwd_call(d, wt, xin, addend, scale, ts, name, slabs, row_off, first_block, n_blocks):
    seq, k = d.shape
    n = seq // ts

    def body(slabs_in, d_ref, wt_ref, xin_ref, add_ref, slabs_ref, dx_ref, acc, sems):
        i = pl.program_id(0)

        @pl.when(i == 0)
        def _():
            acc[...] = jnp.zeros(acc.shape, F32)

        dv = d_ref[...]
        dx_ref[...] = _dot(dv, wt_ref[...]) + scale * add_ref[...]
        acc[...] += _dot_tn(dv, xin_ref[...].astype(BF16))

        @pl.when(i == n - 1)
        def _():
            _store_blocks(acc, slabs_ref, sems, row_off, k // n_blocks, first_block, n_blocks)

    row = lambda width: pl.BlockSpec((ts, width), lambda i: (i, 0))
    hbm = pl.BlockSpec(memory_space=pl.ANY)
    return pl.pallas_call(
        body, name=name, grid=(n,),
        in_specs=[hbm, row(k), _const((k, D_MODEL)), row(D_MODEL), row(D_MODEL)],
        out_specs=[hbm, row(D_MODEL)],
        out_shape=[jax.ShapeDtypeStruct(slabs.shape, F32), jax.ShapeDtypeStruct((seq, D_MODEL), F32)],
        scratch_shapes=[pltpu.VMEM((k, D_MODEL), F32), pltpu.SemaphoreType.DMA((n_blocks,))],
        input_output_aliases={0: 0},
        compiler_params=_params(dimension_semantics=("arbitrary",)),
    )(slabs, d, wt, xin, addend)


MEM_FOLD = BLOB_LANES // XATTN_W


def _mem_proj_call(memq, wk_flat, wv_flat):
    def body(memq_ref, wk_ref, wv_ref, kh_ref, vh_ref):
        for w_ref, o_ref in ((wk_ref, kh_ref), (wv_ref, vh_ref)):
            acc = jnp.zeros((N_MEM, XATTN_W), F32)
            for q in range(MEM_FOLD):
                acc = acc + _dot(memq_ref[q], w_ref[:, q * XATTN_W:(q + 1) * XATTN_W])
            o_ref[...] = acc

    out = jax.ShapeDtypeStruct((N_MEM, XATTN_W), F32)
    return pl.pallas_call(body, name="mem_proj", out_shape=[out, out], compiler_params=_params())(memq, wk_flat, wv_flat)


def _mem_proj_bwd_call(memq, dkh, dvh, slabs, off_k, off_v):
    rows = D_MODEL // MEM_FOLD

    def body(slabs_in, memq_ref, dkh_ref, dvh_ref, slabs_ref, acc, sems):
        for d_ref, off in ((dkh_ref, off_k), (dvh_ref, off_v)):
            dv = d_ref[...].astype(BF16)
            for q in range(MEM_FOLD):
                acc[:, q * XATTN_W:(q + 1) * XATTN_W] = _dot_tn(memq_ref[q], dv)
            _store_blocks(acc, slabs_ref, sems, off, rows // N_DEV, 0, N_DEV)

    hbm = pl.BlockSpec(memory_space=pl.ANY)
    vmem = pl.BlockSpec(memory_space=pltpu.VMEM)
    return pl.pallas_call(
        body, name="mem_proj_bwd", in_specs=[hbm, vmem, vmem, vmem], out_specs=hbm,
        out_shape=jax.ShapeDtypeStruct(slabs.shape, F32),
        scratch_shapes=[pltpu.VMEM((rows, BLOB_LANES), F32), pltpu.SemaphoreType.DMA((N_DEV,))],
        input_output_aliases={0: 0}, compiler_params=_params(),
    )(slabs, memq, dkh, dvh)


def _place():
    return lax.axis_index("x"), lax.axis_index("y"), lax.axis_index("c")


def _all_gather_call(arrs, pieces, name):
    n_in, n_p = len(arrs), len(pieces)

    def body(*refs):
        ins, outs = refs[:n_in], refs[n_in:n_in + n_p]
        send_sems, recv_sems, local_sems = refs[n_in + n_p:]
        x, y, c = _place()
        me, sibling = (x, y, c), (x, y, 1 - c)
        chips = [(1 - x, y), (x, 1 - y), (1 - x, 1 - y)]

        def src(a):
            idx, r0, rows = pieces[a]
            return ins[idx] if r0 is None else ins[idx].at[pl.ds(r0, rows)]

        def slab(a, p):
            return outs[a].at[4 * p[0] + 2 * p[1] + p[2]]

        def copy(a, k, block, to, own=False):
            return pltpu.make_async_remote_copy(
                src_ref=src(a) if own else slab(a, block), dst_ref=slab(a, block),
                send_sem=send_sems.at[a, k], recv_sem=recv_sems.at[a, k], device_id=to, device_id_type=MESH)

        mine = [pltpu.make_async_copy(src(a), slab(a, me), local_sems.at[a]) for a in range(n_p)]
        for cp in mine:
            cp.start()
        first = []
        for a in range(n_p):
            first.append(copy(a, 0, me, sibling, own=True))
            first += [copy(a, 1 + j, me, (*chip, c), own=True) for j, chip in enumerate(chips)]
        for cp in first:
            cp.start()
        passed = []
        for a in range(n_p):
            for j, chip in enumerate(chips):
                copy(a, 1 + j, (*chip, c), me).wait_recv()
                cp = copy(a, 4 + j, (*chip, c), sibling)
                cp.start()
                passed.append(cp)
        for a in range(n_p):
            copy(a, 0, sibling, me).wait_recv()
            for j, chip in enumerate(chips):
                copy(a, 4 + j, (*chip, 1 - c), me).wait_recv()
        for cp in first + passed:
            cp.wait_send()
        for cp in mine:
            cp.wait()

    def out_shape(piece):
        idx, r0, rows = piece
        a = arrs[idx]
        return jax.ShapeDtypeStruct((N_DEV,) + (a.shape if r0 is None else (rows,) + a.shape[1:]), a.dtype)

    hbm = pl.BlockSpec(memory_space=pl.ANY)
    return pl.pallas_call(
        body, name=name,
        in_specs=[hbm] * n_in, out_specs=[hbm] * n_p, out_shape=[out_shape(p) for p in pieces],
        scratch_shapes=[pltpu.SemaphoreType.DMA((n_p, 7)), pltpu.SemaphoreType.DMA((n_p, 7)),
                        pltpu.SemaphoreType.DMA((n_p,))],
    )(*arrs)


def _swap_core_call(g):
    _, rows, width = g.shape

    def body(g_ref, r_ref, send_sems, recv_sems):
        x, y, c = _place()
        copies = []
        for k in range(4):
            cp = pltpu.make_async_remote_copy(
                src_ref=g_ref.at[2 * k + (1 - c)], dst_ref=r_ref.at[k], send_sem=send_sems.at[k],
                recv_sem=recv_sems.at[k], device_id=(x, y, 1 - c), device_id_type=MESH)
            cp.start()
            copies.append(cp)
        for cp in copies:
            cp.wait()

    hbm = pl.BlockSpec(memory_space=pl.ANY)
    return pl.pallas_call(
        body, name="rs_swap_core", in_specs=[hbm], out_specs=hbm,
        out_shape=jax.ShapeDtypeStruct((4, rows, width), g.dtype),
        scratch_shapes=[pltpu.SemaphoreType.DMA((4,)), pltpu.SemaphoreType.DMA((4,))],
    )(g)


def _swap_chip_call(p):
    _, rows, width = p.shape

    def body(p_ref, r_ref, send_sems, recv_sems):
        x, y, c = _place()
        chips = [(1 - x, y), (x, 1 - y), (1 - x, 1 - y)]
        copies = []
        for j, (px, py) in enumerate(chips):
            cp = pltpu.make_async_remote_copy(
                src_ref=p_ref.at[2 * px + py], dst_ref=r_ref.at[j], send_sem=send_sems.at[j],
                recv_sem=recv_sems.at[j], device_id=(px, py, c), device_id_type=MESH)
            cp.start()
            copies.append(cp)
        for cp in copies:
            cp.wait()

    hbm = pl.BlockSpec(memory_space=pl.ANY)
    return pl.pallas_call(
        body, name="rs_swap_chip", in_specs=[hbm], out_specs=hbm,
        out_shape=jax.ShapeDtypeStruct((3, rows, width), p.dtype),
        scratch_shapes=[pltpu.SemaphoreType.DMA((3,)), pltpu.SemaphoreType.DMA((3,))],
    )(p)


ADD_ROWS = 128


def _pair_add_call(g, r, c):
    _, rows, width = g.shape

    def body(c_ref, g_ref, r_ref, o_ref):
        o_ref[...] = (g_ref[...] + r_ref[...]).astype(BF16)

    return pl.pallas_call(
        body, name="rs_pair_add",
        grid_spec=pltpu.PrefetchScalarGridSpec(
            num_scalar_prefetch=1, grid=(4, rows // ADD_ROWS),
            in_specs=[pl.BlockSpec((None, ADD_ROWS, width), lambda k, i, c_ref: (2 * k + c_ref[0], i, 0)),
                      pl.BlockSpec((None, ADD_ROWS, width), lambda k, i, c_ref: (k, i, 0))],
            out_specs=pl.BlockSpec((None, ADD_ROWS, width), lambda k, i, c_ref: (k, i, 0))),
        out_shape=jax.ShapeDtypeStruct((4, rows, width), BF16),
        compiler_params=_params(dimension_semantics=("arbitrary", "arbitrary")),
    )(c, g, r)


def _adam(w, g, m, v):
    mn = ADAM_B1 * m + (1.0 - ADAM_B1) * g
    vn = ADAM_B2 * v + (1.0 - ADAM_B2) * (g * g)
    m_hat = mn / (1.0 - ADAM_B1 ** ADAM_STEP)
    v_hat = vn / (1.0 - ADAM_B2 ** ADAM_STEP)
    return -ADAM_LR * (m_hat / (jnp.sqrt(v_hat) + ADAM_EPS) + ADAM_WD * w), mn, vn


def _chip_add_adamw_call(slabs, from_sibling, from_chips, me, chip, w, m, v):
    _, rows, width = slabs.shape

    def body(me_ref, chip_ref, own_ref, sib_ref, r_ref, w_ref, m_ref, v_ref, g_ref, d_ref, mo_ref, vo_ref):
        g = own_ref[...] + sib_ref[...]
        for j in range(3):
            g = g + r_ref[j].astype(F32)
        g_ref[...] = g
        d_ref[...], mo_ref[...], vo_ref[...] = _adam(w_ref[...], g, m_ref[...], v_ref[...])

    spec = pl.BlockSpec((ADD_ROWS, width), lambda i, me_ref, chip_ref: (i, 0))
    return pl.pallas_call(
        body, name="rs_chip_add_adamw",
        grid_spec=pltpu.PrefetchScalarGridSpec(
            num_scalar_prefetch=2, grid=(rows // ADD_ROWS,),
            in_specs=[pl.BlockSpec((None, ADD_ROWS, width), lambda i, me_ref, chip_ref: (me_ref[0], i, 0)),
                      pl.BlockSpec((None, ADD_ROWS, width), lambda i, me_ref, chip_ref: (chip_ref[0], i, 0)),
                      pl.BlockSpec((3, ADD_ROWS, width), lambda i, me_ref, chip_ref: (0, i, 0)), spec, spec, spec],
            out_specs=[spec] * 4),
        out_shape=[jax.ShapeDtypeStruct((rows, width), F32)] * 4,
        compiler_params=_params(dimension_semantics=("arbitrary",)),
    )(me, chip, slabs, from_sibling, from_chips, w, m, v)


def _adamw_whole_call(params, name):
    n = len(params)

    def body(*refs):
        ins, outs = refs[:4 * n], refs[4 * n:]
        for a in range(n):
            w_ref, g_ref, m_ref, v_ref = ins[4 * a:4 * a + 4]
            outs[3 * a][...], outs[3 * a + 1][...], outs[3 * a + 2][...] = _adam(w_ref[...], g_ref[...], m_ref[...],
                                                                              v_ref[...])

    flat = [a for p in params for a in p]
    shapes = [jax.ShapeDtypeStruct(p[0].shape, F32) for p in params for _ in range(3)]
    out = pl.pallas_call(body, name=name, out_shape=shapes, compiler_params=_params())(*flat)
    return [tuple(out[3 * a:3 * a + 3]) for a in range(n)]


GATHERED_ACCS = (("dva", (8, CONV_W)), ("dv1", (8, D_MODEL)), ("dv2", (8, D_MODEL)), ("dcf", (8, D_FF_PAD)),
                 ("dcw", (HALO, CONV_W)), ("dwcat", (CHUNK, GMLP_HEADS * CHUNK)), ("dmsum", (CHUNK, GMLP_W)))
VEC_A = ("conv_a_b", "ln_a_g", "ln_a_b", "ln_v_g", "ln_v_b")
REP_IN_KERNEL = VEC_A + ("ln1_g", "ln1_b", "ln2_g", "ln2_b", "w_s", "b_s")


def _replicated_update_call(gathered, p, mom_m, mom_v):
    n_acc = len(GATHERED_ACCS)
    n_rep = len(REP_IN_KERNEL)

    def body(*refs):
        acc_refs = refs[:DEPTH * n_acc]
        wmv = refs[DEPTH * n_acc:DEPTH * n_acc + 3 * n_rep]
        outs = refs[DEPTH * n_acc + 3 * n_rep:]
        out_par = {nm: outs[4 * a:4 * a + 4] for a, nm in enumerate(REP_IN_KERNEL)}
        out_dcf = outs[4 * n_rep:4 * n_rep + DEPTH]
        out_dcw = outs[4 * n_rep + DEPTH:4 * n_rep + 2 * DEPTH]
        par = {nm: wmv[3 * a:3 * a + 3] for a, nm in enumerate(REP_IN_KERNEL)}
        tril = (lax.broadcasted_iota(jnp.int32, (CHUNK, CHUNK), 0) >= lax.broadcasted_iota(jnp.int32, (CHUNK, CHUNK), 1))
        head = lax.broadcasted_iota(jnp.int32, (8, GMLP_W), 0) * HEAD_DIM
        lane = lax.broadcasted_iota(jnp.int32, (8, GMLP_W), 1)
        sel = jnp.where((lane >= head) & (lane < head + HEAD_DIM), 1.0, 0.0)

        def update(nm, idx, g):
            w_ref, m_ref, v_ref = par[nm]
            d, mn, vn = _adam(w_ref[idx], g, m_ref[idx], v_ref[idx])
            g_ref, d_ref, mo_ref, vo_ref = out_par[nm]
            g_ref[idx] = g
            d_ref[idx] = d
            mo_ref[idx] = mn
            vo_ref[idx] = vn

        for l in range(DEPTH):
            tot = {}
            for a, (nm, _) in enumerate(GATHERED_ACCS):
                ref = acc_refs[l * n_acc + a]
                s = ref[0]
                for j in range(1, N_DEV):
                    s = s + ref[j]
                tot[nm] = s
            out_dcf[l][...] = tot["dcf"]
            out_dcw[l][...] = tot["dcw"]
            row = (slice(l, l + 1), slice(None))
            for k, nm in enumerate(VEC_A):
                update(nm, row, tot["dva"][k:k + 1, :])
            update("ln1_g", row, tot["dv1"][0:1, :])
            update("ln1_b", row, tot["dv1"][1:2, :])
            update("ln2_g", row, tot["dv2"][0:1, :])
            update("ln2_b", row, tot["dv2"][1:2, :])
            for h in range(GMLP_HEADS):
                gw = jnp.where(tril, tot["dwcat"][:, h * CHUNK:(h + 1) * CHUNK], 0.0)
                update("w_s", (l, h), gw)
            gb = lax.dot_general(sel, tot["dmsum"], (((1,), (1,)), ((), ())), precision=lax.Precision.HIGHEST,
                                 preferred_element_type=F32)
            for h in range(GMLP_HEADS):
                update("b_s", (l, slice(h, h + 1), slice(None)), gb[h:h + 1, :])

    ins = [gathered[l][nm] for l in range(DEPTH) for nm, _ in GATHERED_ACCS]
    ins += [t[nm] for nm in REP_IN_KERNEL for t in (p, mom_m, mom_v)]
    shapes = [jax.ShapeDtypeStruct(p[nm].shape, F32) for nm in REP_IN_KERNEL for _ in range(4)]
    shapes += [jax.ShapeDtypeStruct((8, D_FF_PAD), F32)] * DEPTH + [jax.ShapeDtypeStruct((HALO, CONV_W), F32)] * DEPTH
    out = pl.pallas_call(body, name="replicated_update", out_shape=shapes, compiler_params=_params())(*ins)
    res = [{nm: out[4 * a + k] for a, nm in enumerate(REP_IN_KERNEL)} for k in range(4)]
    return res, out[4 * n_rep:4 * n_rep + DEPTH], out[4 * n_rep + DEPTH:]


FF_GROUP = D_FF // N_DEV
FF_GROUP_PAD = D_FF_PAD // N_DEV
BLOCK_ROWS = (("w_in", IN_W // N_DEV), ("w_out", D_MODEL // N_DEV), ("w_up", 2 * FF_GROUP_PAD),
              ("w_down", FF_GROUP_PAD), ("w_mk", D_MODEL // N_DEV // MEM_FOLD), ("w_mv", D_MODEL // N_DEV // MEM_FOLD))
LAYER_ROWS = sum(r for _, r in BLOCK_ROWS)
BLOB_ROWS = DEPTH * LAYER_ROWS
assert BLOB_ROWS % ADD_ROWS == 0 and all(r % 16 == 0 for _, r in BLOCK_ROWS)


def _row_off(l, name):
    off = l * LAYER_ROWS
    for nm, r in BLOCK_ROWS:
        if nm == name:
            return off
        off += r
    raise KeyError(name)


def _to_rows(name, a):
    if name == "w_in":
        return a.T
    if name == "w_up":
        t = a.T.reshape(2, FF_GROUP, D_MODEL)
        return jnp.pad(t, ((0, 0), (0, FF_GROUP_PAD - FF_GROUP), (0, 0))).reshape(2 * FF_GROUP_PAD, D_MODEL)
    if name == "w_down":
        return jnp.pad(a, ((0, FF_GROUP_PAD - FF_GROUP), (0, 0)))
    if name == "w_out":
        return a
    return a.reshape(-1, BLOB_LANES)


def _from_rows(name, r):
    if name == "w_in":
        return r.T
    if name == "w_up":
        return r.reshape(2, FF_GROUP_PAD, D_MODEL)[:, :FF_GROUP].reshape(2 * FF_GROUP, D_MODEL).T
    if name == "w_down":
        return r[:FF_GROUP]
    if name == "w_out":
        return r
    return r.reshape(D_MODEL // N_DEV, XATTN_W)


def _blob(tree):
    return jnp.concatenate([_to_rows(nm, tree[nm][l]) for l in range(DEPTH) for nm, _ in BLOCK_ROWS], axis=0)


def _unblob(blob):
    out = {}
    for nm, r in BLOCK_ROWS:
        out[nm] = jnp.stack([_from_rows(nm, blob[_row_off(l, nm):_row_off(l, nm) + r]) for l in range(DEPTH)])
    return out


def _ff_interleave(a):
    lead = a.shape[:-1]
    t = a.reshape(lead + (N_DEV, FF_GROUP))
    return jnp.pad(t, [(0, 0)] * len(lead) + [(0, 0), (0, FF_GROUP_PAD - FF_GROUP)]).reshape(lead + (D_FF_PAD,))


def _ff_deinterleave(a):
    lead = a.shape[:-1]
    return a.reshape(lead + (N_DEV, FF_GROUP_PAD))[..., :FF_GROUP].reshape(lead + (D_FF,))


def _head_table():
    hd = jnp.arange(XATTN_W) // HEAD_DIM
    return (hd[None, :] == jnp.arange(XATTN_HEADS)[:, None]).astype(F32)


def _layer_operands(full, conv_a_w, conv_f_w, p, l, memq):
    mat = {nm: full[nm].reshape(-1, BLOB_LANES) for nm in full}
    w = {}
    w["wint"] = mat["w_in"]
    w["win"] = mat["w_in"].T
    w["wout"] = mat["w_out"]
    w["woutt"] = mat["w_out"].T
    w["wgt"] = mat["w_up"][:D_FF_PAD]
    w["wvt"] = mat["w_up"][D_FF_PAD:]
    w["wg"] = w["wgt"].T
    w["wv"] = w["wvt"].T
    w["wdown"] = mat["w_down"]
    w["wdownt"] = mat["w_down"].T
    w["cw"] = conv_a_w
    zeros = jnp.zeros((3, CONV_W), F32)
    w["va"] = jnp.concatenate([p[nm][l][None] for nm in VEC_A] + [zeros], axis=0)
    tril = jnp.tril(jnp.ones((CHUNK, CHUNK), F32))
    w["wcat"] = (p["w_s"][l] * tril[None]).transpose(1, 0, 2).reshape(CHUNK, GMLP_HEADS * CHUNK).astype(BF16)
    w["wcatt"] = w["wcat"].T
    w["bfull"] = jnp.repeat(p["b_s"][l].T, HEAD_DIM, axis=1)
    kh, vh = _mem_proj_call(memq, mat["w_mk"], mat["w_mv"])
    hm = _head_table()
    scale = 1.0 / math.sqrt(HEAD_DIM)
    w["kt"] = (kh.T[:, None, :] * hm.T[:, :, None] * scale).reshape(XATTN_W, XATTN_HEADS * N_MEM).astype(BF16)
    w["ktt"] = w["kt"].T
    w["vm"] = (hm[:, None, :] * vh[None]).reshape(XATTN_HEADS * N_MEM, XATTN_W).astype(BF16)
    w["vmt"] = w["vm"].T
    zeros = jnp.zeros((6, D_MODEL), F32)
    w["v1"] = jnp.concatenate([p["ln1_g"][l][None], p["ln1_b"][l][None], zeros], axis=0)
    w["v2"] = jnp.concatenate([p["ln2_g"][l][None], p["ln2_b"][l][None], zeros], axis=0)
    w["cf"] = jnp.concatenate([conv_f_w, _ff_interleave(p["conv_f_b"][l][None]), jnp.zeros((4, D_FF_PAD), F32)], axis=0)
    return w


TS_MIXER = 256
TS_FFN = 256
TS_PROJ = 512
CONV_A_SHARD = CONV_W // N_DEV


def kernel(x, mem, w_in, conv_a_w, conv_a_b, ln_a_g, ln_a_b, ln_v_g, ln_v_b, w_s, b_s, w_mk, w_mv, w_out, ln1_g, ln1_b, w_up, conv_f_w, conv_f_b, w_down, ln2_g, ln2_b, loss_target, m_w_in, m_conv_a_w, m_conv_a_b, m_ln_a_g, m_ln_a_b, m_ln_v_g, m_ln_v_b, m_w_s, m_b_s, m_w_mk, m_w_mv, m_w_out, m_ln1_g, m_ln1_b, m_w_up, m_conv_f_w, m_conv_f_b, m_w_down, m_ln2_g, m_ln2_b, v_w_in, v_conv_a_w, v_conv_a_b, v_ln_a_g, v_ln_a_b, v_ln_v_g, v_ln_v_b, v_w_s, v_b_s, v_w_mk, v_w_mv, v_w_out, v_ln1_g, v_ln1_b, v_w_up, v_conv_f_w, v_conv_f_b, v_w_down, v_ln2_g, v_ln2_b):
    given = dict(locals())
    p = {nm: given[nm] for nm in WEIGHTS}
    mom_m = {nm: given["m_" + nm] for nm in WEIGHTS}
    mom_v = {nm: given["v_" + nm] for nm in WEIGHTS}
    seq = x.shape[1]
    ts_m, ts_f, ts_p = min(TS_MIXER, seq), min(TS_FFN, seq), min(TS_PROJ, seq)
    cx, cy, cc = _place()
    me = 4 * cx + 2 * cy + cc

    w_blob = _blob(p)
    conv_a_tile = jnp.pad(conv_a_w, ((0, 0), (0, HALO - CONV_K), (0, 128 - CONV_A_SHARD)))
    conv_f_tile = jnp.pad(conv_f_w, ((0, 0), (0, 8 - FFN_CONV_K), (0, 384 - FF_GROUP)))
    pieces = [(0, _row_off(l, nm), r) for l in range(DEPTH) for nm, r in BLOCK_ROWS] + [(1, None, 0), (2, None, 0)]
    gathered = _all_gather_call([w_blob.astype(BF16), conv_a_tile, conv_f_tile], pieces, "gather_weights")
    n_mat = len(BLOCK_ROWS)
    conv_a_all, conv_f_all = gathered[DEPTH * n_mat], gathered[DEPTH * n_mat + 1]
    memq = mem[0].reshape(N_MEM, D_MODEL // MEM_FOLD, MEM_FOLD).transpose(2, 0, 1).astype(BF16)
    ops = []
    for l in range(DEPTH):
        full = {nm: gathered[l * n_mat + a] for a, (nm, _) in enumerate(BLOCK_ROWS)}
        ca = conv_a_all[:, l, :, :CONV_A_SHARD].transpose(1, 0, 2).reshape(HALO, CONV_W)
        cf = conv_f_all[:, l, :FFN_CONV_K, :FF_GROUP_PAD].transpose(1, 0, 2).reshape(FFN_CONV_K, D_FF_PAD)
        ops.append(_layer_operands(full, ca, cf, p, l, memq))

    saved = []
    xl = x[0]
    for l in range(DEPTH):
        hb, z1, x1 = _mixer_fwd_call(xl, ops[l], ts_m)
        ug, uv, z2, x2 = _ffn_fwd_call(x1, ops[l], ts_f)
        saved.append(dict(x=xl, hb=hb, z1=z1, x1=x1, ug=ug, uv=uv, z2=z2))
        xl = x2

    hm = _head_table()
    slabs = lax.empty((N_DEV, BLOB_ROWS, BLOB_LANES), F32)
    accs = [None] * DEPTH
    dx = loss_target[0]
    loss = None
    for l in reversed(range(DEPTH)):
        s, w = saved[l], ops[l]
        last = l == DEPTH - 1
        slabs, dug, duv, dz2, dcf, dv2, loss_acc = _ffn_bwd_call(dx, s["z2"], s["ug"], s["uv"], w, ts_f, last, slabs,
                                                                 _row_off(l, "w_down"))
        if last:
            loss = loss_acc[0, 0]
        off_up = _row_off(l, "w_up")
        slabs, dxa = _proj_bwd_call(dug, w["wgt"], s["x1"], dz2, ALPHA, ts_p, "up_gate_bwd", slabs, off_up, 0, 4)
        slabs, dx1 = _proj_bwd_call(duv, w["wvt"], s["x1"], dxa, 1.0, ts_p, "up_val_bwd", slabs, off_up, 4, 4)
        (slabs, dh, dz1, dkt, dvm, dwcat, dmsum, dva, dcw, dv1) = _mixer_bwd_call(
            dx1, s["z1"], s["hb"], w, ts_m, slabs, _row_off(l, "w_out"))
        slabs, dx = _proj_bwd_call(dh, w["wint"], s["x"], dz1, ALPHA, ts_p, "in_proj_bwd", slabs, _row_off(l, "w_in"),
                                   0, N_DEV)
        dkh = jnp.einsum("hd,dhm->md", hm, dkt.reshape(XATTN_W, XATTN_HEADS, N_MEM)) * (1.0 / math.sqrt(HEAD_DIM))
        dvh = jnp.einsum("hd,hmd->md", hm, dvm.reshape(XATTN_HEADS, N_MEM, XATTN_W))
        slabs = _mem_proj_bwd_call(memq, dkh, dvh, slabs, _row_off(l, "w_mk"), _row_off(l, "w_mv"))
        accs[l] = dict(dva=dva, dv1=dv1, dv2=dv2, dcf=dcf, dcw=dcw, dwcat=dwcat, dmsum=dmsum)
    grad_x = dx[None]

    from_sibling = _swap_core_call(slabs)
    chip_sum = _pair_add_call(slabs, from_sibling, cc.reshape(1).astype(jnp.int32))
    from_chips = _swap_chip_call(chip_sum)
    g_sh, d_sh, m_sh, v_sh = _chip_add_adamw_call(
        slabs, from_sibling, from_chips, me.reshape(1).astype(jnp.int32), (2 * cx + cy).reshape(1).astype(jnp.int32),
        w_blob, _blob(mom_m), _blob(mom_v))
    outs = [_unblob(b) for b in (g_sh, d_sh, m_sh, v_sh)]

    acc_list = [accs[l][nm] for l in range(DEPTH) for nm, _ in GATHERED_ACCS]
    acc_all = _all_gather_call(acc_list, [(a, None, 0) for a in range(len(acc_list))], "gather_small_grads")
    n_acc = len(GATHERED_ACCS)
    gathered_accs = [{nm: acc_all[l * n_acc + a] for a, (nm, _) in enumerate(GATHERED_ACCS)} for l in range(DEPTH)]
    rep, dcf_sum, dcw_sum = _replicated_update_call(gathered_accs, p, mom_m, mom_v)
    for k in range(4):
        outs[k].update(rep[k])
    dcf_sum, dcw_sum = jnp.stack(dcf_sum), jnp.stack(dcw_sum)
    zero = jnp.zeros((), jnp.int32)
    g_conv_a_w = lax.dynamic_slice(dcw_sum, (zero, zero, CONV_A_SHARD * me), (DEPTH, CONV_K, CONV_A_SHARD))
    g_conv_f_w = lax.dynamic_slice(dcf_sum, (zero, zero, FF_GROUP_PAD * me), (DEPTH, FFN_CONV_K, FF_GROUP))
    g_conv_f_b = _ff_deinterleave(dcf_sum[:, FFN_CONV_K])
    conv_grads = dict(conv_a_w=g_conv_a_w, conv_f_w=g_conv_f_w, conv_f_b=g_conv_f_b)
    conv_names = tuple(conv_grads)
    upd = _adamw_whole_call([(p[nm], conv_grads[nm], mom_m[nm], mom_v[nm]) for nm in conv_names], "adamw_conv")
    for nm, (d, mn, vn) in zip(conv_names, upd):
        outs[0][nm], outs[1][nm], outs[2][nm], outs[3][nm] = conv_grads[nm], d, mn, vn

    loss = lax.psum(loss, ("x", "y", "c"))
    return (loss, grad_x, *[outs[0][nm] for nm in WEIGHTS], *[outs[1][nm] for nm in WEIGHTS],
            *[outs[2][nm] for nm in WEIGHTS], *[outs[3][nm] for nm in WEIGHTS])
```

```python
import math

import jax
import jax.numpy as jnp
from jax import lax
from jax.experimental import pallas as pl
from jax.experimental.pallas import tpu as pltpu

F32 = jnp.float32
BF16 = jnp.bfloat16

DEPTH = 2
D_MODEL = 1024
CONV_W = 384
GMLP_W = 384
XATTN_W = 256
HEAD_DIM = 64
GMLP_HEADS = 6
XATTN_HEADS = 4
IN_W = 1792
CONV_K = 31
CHUNK = 128
N_MEM = 256
D_FF = 2752
D_FF_PAD = 2816
FFN_CONV_K = 3
ALPHA = (2.0 * DEPTH) ** 0.25
LN_EPS = 1e-5
N_DEV = 8

ADAM_LR = 0.001
ADAM_B1 = 0.9
ADAM_B2 = 0.999
ADAM_EPS = 1e-08
ADAM_WD = 0.01
ADAM_STEP = 10

HALO = 32
CONV_ROWS = 32
V7X_VMEM_BYTES = 64 * 1024 * 1024
VMEM_LIMIT = V7X_VMEM_BYTES - 8 * 1024 * 1024
BLOB_LANES = 1024

MESH = pl.DeviceIdType.MESH

WEIGHTS = ("w_in", "conv_a_w", "conv_a_b", "ln_a_g", "ln_a_b", "ln_v_g", "ln_v_b", "w_s", "b_s", "w_mk", "w_mv",
           "w_out", "ln1_g", "ln1_b", "w_up", "conv_f_w", "conv_f_b", "w_down", "ln2_g", "ln2_b")


def _params(**kw):
    return pltpu.CompilerParams(vmem_limit_bytes=VMEM_LIMIT, **kw)


def _const(shape):
    nd = len(shape)
    return pl.BlockSpec(shape, lambda i: (0,) * nd, pipeline_mode=pl.Buffered(1))


def _acc(shape):
    nd = len(shape)
    return pl.BlockSpec(shape, lambda i: (0,) * nd)


def _sigmoid(x):
    return 1.0 / (1.0 + jnp.exp(-x))


_GELU_C = math.sqrt(2.0 / math.pi)


def _gelu(x):
    x2 = x * x
    t = jnp.tanh(_GELU_C * (x + 0.044715 * x * x2))
    g = 0.5 * x * (1.0 + t)
    dg = 0.5 * (1.0 + t) + 0.5 * x * (1.0 - t * t) * (_GELU_C * (1.0 + 3.0 * 0.044715 * x2))
    return g, dg


def _ln_stats(z):
    mu = jnp.mean(z, axis=-1, keepdims=True)
    zc = z - mu
    var = jnp.mean(zc * zc, axis=-1, keepdims=True)
    r = lax.rsqrt(var + LN_EPS)
    return zc * r, r


def _ln_bwd(dy, xh, r, g):
    dxh = dy * g
    m1 = jnp.mean(dxh, axis=-1, keepdims=True)
    m2 = jnp.mean(dxh * xh, axis=-1, keepdims=True)
    return r * (dxh - m1 - xh * m2)


def _rowsum(x):
    return jnp.sum(x, axis=0, keepdims=True)


def _dot(a, b):
    return jnp.dot(a, b, preferred_element_type=F32)


def _dot_tn(a, b):
    return lax.dot_general(a, b, (((0,), (0,)), ((), ())), preferred_element_type=F32)


def _dot_nt(a, b):
    return lax.dot_general(a, b, (((1,), (1,)), ((), ())), preferred_element_type=F32)


def _shift_copies(buf, sh, rows):
    for b in range(1, 8):
        sh[b - 1, 0:rows, :] = buf[b:b + rows, :]


def _window(buf, sh, start):
    b = start % 8
    a = start - b
    return buf[a:a + CONV_ROWS, :] if b == 0 else sh[b - 1, a:a + CONV_ROWS, :]


def _conv31_fwd(buf, sh, w_ref, bias, out, ts):
    for r0 in range(0, ts, CONV_ROWS):
        acc = jnp.broadcast_to(bias, (CONV_ROWS, CONV_W))
        for k in range(CONV_K):
            acc = acc + w_ref[k:k + 1, :] * _window(buf, sh, r0 + HALO - (CONV_K - 1) + k)
        out[r0:r0 + CONV_ROWS, :] = acc


def _conv31_dx(dbuf, dsh, w_ref, out, ts):
    for r0 in range(0, ts, CONV_ROWS):
        acc = jnp.zeros((CONV_ROWS, CONV_W), F32)
        for k in range(CONV_K):
            acc = acc + w_ref[k:k + 1, :] * _window(dbuf, dsh, r0 + (CONV_K - 1) - k)
        out[r0:r0 + CONV_ROWS, :] = acc


def _conv31_dw(buf, sh, dbuf, dw_ref, ts):
    for k in range(CONV_K):
        part = jnp.zeros((8, CONV_W), F32)
        for r0 in range(0, ts, CONV_ROWS):
            m = dbuf[r0:r0 + CONV_ROWS, :] * _window(buf, sh, r0 + HALO - (CONV_K - 1) + k)
            for q in range(0, CONV_ROWS, 8):
                part = part + m[q:q + 8, :]
        dw_ref[k:k + 1, :] += _rowsum(part)


def _head_mask(width, h):
    lane = lax.broadcasted_iota(jnp.int32, (CHUNK, width), 1)
    return (lane >= h * HEAD_DIM) & (lane < (h + 1) * HEAD_DIM)


def _stack_heads(vn_c):
    return jnp.concatenate([jnp.where(_head_mask(GMLP_W, h), vn_c, 0.0) for h in range(GMLP_HEADS)], axis=0)


def _group_a_fwd(hf, buf, sh, a1buf, cw_ref, va_ref, ts):
    ha = hf[:, 0:CONV_W]
    sg = _sigmoid(hf[:, CONV_W:2 * CONV_W])
    buf[HALO:HALO + ts, :] = ha * sg
    _shift_copies(buf, sh, ts + HALO - 8)
    _conv31_fwd(buf, sh, cw_ref, va_ref[0:1, :], a1buf, ts)
    a2h, ra = _ln_stats(a1buf[...])
    a2 = a2h * va_ref[1:2, :] + va_ref[2:3, :]
    sa = _sigmoid(a2)
    return dict(ha=ha, sg=sg, a2h=a2h, ra=ra, a2=a2, sa=sa, a=a2 * sa)


def _group_b_fwd(hf, va_ref, wcat_ref, bfull_ref, ts):
    hu = hf[:, 2 * CONV_W:2 * CONV_W + GMLP_W]
    hv = hf[:, 2 * CONV_W + GMLP_W:2 * CONV_W + 2 * GMLP_W]
    u, du = _gelu(hu)
    v, dv = _gelu(hv)
    vhat, rv = _ln_stats(v)
    vn = vhat * va_ref[3:4, :] + va_ref[4:5, :]
    stacks, mixed = [], []
    for c0 in range(0, ts, CHUNK):
        st = _stack_heads(vn[c0:c0 + CHUNK, :]).astype(BF16)
        stacks.append(st)
        mixed.append(_dot(wcat_ref[...], st) + bfull_ref[...])
    mixed = jnp.concatenate(mixed, axis=0) if len(mixed) > 1 else mixed[0]
    return dict(u=u, du=du, dv=dv, vhat=vhat, rv=rv, stacks=stacks, mixed=mixed, g=u * mixed)


def _group_c_fwd(hf, kt_ref, vm_ref):
    qb = hf[:, IN_W - XATTN_W:IN_W].astype(BF16)
    s_all = _dot(qb, kt_ref[...])
    ps = []
    for g in range(XATTN_HEADS):
        s = s_all[:, g * N_MEM:(g + 1) * N_MEM]
        e = jnp.exp(s - jnp.max(s, axis=-1, keepdims=True))
        ps.append(e / jnp.sum(e, axis=-1, keepdims=True))
    p_all = jnp.concatenate(ps, axis=1)
    pb = p_all.astype(BF16)
    return dict(qb=qb, p=p_all, pb=pb, c=_dot(pb, vm_ref[...]))


def _mixer_fwd_call(x, w, ts):
    seq = x.shape[0]
    n = seq // ts

    def body(x_ref, win_ref, cw_ref, va_ref, wcat_ref, bfull_ref, kt_ref, vm_ref, wout_ref, v1_ref,
             hb_ref, z1_ref, x1_ref, buf, a1buf, sh):
        i = pl.program_id(0)

        @pl.when(i == 0)
        def _():
            buf[0:HALO, :] = jnp.zeros((HALO, CONV_W), F32)

        @pl.when(i > 0)
        def _():
            buf[0:HALO, :] = buf[ts:ts + HALO, :]

        xv = x_ref[...]
        hb = _dot(xv.astype(BF16), win_ref[...]).astype(BF16)
        hb_ref[...] = hb
        hf = hb.astype(F32)
        ga = _group_a_fwd(hf, buf, sh, a1buf, cw_ref, va_ref, ts)
        gb = _group_b_fwd(hf, va_ref, wcat_ref, bfull_ref, ts)
        gc = _group_c_fwd(hf, kt_ref, vm_ref)
        cat = jnp.concatenate([ga["a"], gb["g"], gc["c"]], axis=1).astype(BF16)
        z1 = ALPHA * xv + _dot(cat, wout_ref[...])
        z1_ref[...] = z1
        xh, _ = _ln_stats(z1)
        x1_ref[...] = xh * v1_ref[0:1, :] + v1_ref[1:2, :]

    row = lambda width: pl.BlockSpec((ts, width), lambda i: (i, 0))
    return pl.pallas_call(
        body, name="mixer_fwd", grid=(n,),
        in_specs=[row(D_MODEL), _const((D_MODEL, IN_W)), _const((HALO, CONV_W)), _const((8, CONV_W)),
                  _const((CHUNK, GMLP_HEADS * CHUNK)), _const((CHUNK, GMLP_W)), _const((XATTN_W, XATTN_HEADS * N_MEM)),
                  _const((XATTN_HEADS * N_MEM, XATTN_W)), _const((D_MODEL, D_MODEL)), _const((8, D_MODEL))],
        out_specs=[row(IN_W), row(D_MODEL), row(D_MODEL)],
        out_shape=[jax.ShapeDtypeStruct((seq, IN_W), BF16), jax.ShapeDtypeStruct((seq, D_MODEL), F32),
                   jax.ShapeDtypeStruct((seq, D_MODEL), F32)],
        scratch_shapes=[pltpu.VMEM((ts + HALO, CONV_W), F32), pltpu.VMEM((ts, CONV_W), F32),
                        pltpu.VMEM((7, ts + HALO, CONV_W), F32)],
        compiler_params=_params(dimension_semantics=("arbitrary",)),
    )(x, w["win"], w["cw"], w["va"], w["wcat"], w["bfull"], w["kt"], w["vm"], w["wout"], w["v1"])


def _store_blocks(acc, slabs_ref, sems, row_off, rows, first_block, n_blocks):
    copies = [pltpu.make_async_copy(acc.at[pl.ds(q * rows, rows)], slabs_ref.at[first_block + q, pl.ds(row_off, rows)],
                                    sems.at[q]) for q in range(n_blocks)]
    for cp in copies:
        cp.start()
    for cp in copies:
        cp.wait()


def _mixer_bwd_call(dx1, z1, hb, x, w, ts, slabs, off_out, off_in):
    seq = dx1.shape[0]
    n = seq // ts
    halo_blocks = ts // HALO

    def body(slabs_in, dx1_ref, z1_ref, hb_ref, hprev_ref, x_ref, cw_ref, va_ref, wcat_ref, wcatt_ref, bfull_ref,
             kt_ref, ktt_ref, vm_ref, vmt_ref, woutt_ref, wint_ref, v1_ref,
             slabs_ref, dx_ref, dkt_ref, dvm_ref, dwcat_ref, dmsum_ref, dva_ref, dcw_ref, dv1_ref,
             buf, a1buf, dbuf, da0buf, dwout_ref, dwin_ref, sems, sh, dsh):
        i = pl.program_id(0)

        @pl.when(i == 0)
        def _():
            for ref in (dwout_ref, dwin_ref, dkt_ref, dvm_ref, dwcat_ref, dmsum_ref, dva_ref, dcw_ref, dv1_ref):
                ref[...] = jnp.zeros(ref.shape, F32)
            dbuf[ts:ts + HALO, :] = jnp.zeros((HALO, CONV_W), F32)

        @pl.when(i > 0)
        def _():
            dbuf[ts:ts + HALO, :] = dbuf[0:HALO, :]

        dx1v = dx1_ref[...]
        xh1, r1 = _ln_stats(z1_ref[...])
        dv1_ref[0:1, :] += _rowsum(dx1v * xh1)
        dv1_ref[1:2, :] += _rowsum(dx1v)
        dz1 = _ln_bwd(dx1v, xh1, r1, v1_ref[0:1, :])
        dmix = dz1.astype(BF16)

        hf = hb_ref[...].astype(F32)
        hp = hprev_ref[...].astype(F32)
        a0p = hp[:, 0:CONV_W] * _sigmoid(hp[:, CONV_W:2 * CONV_W])
        buf[0:HALO, :] = jnp.where(i == n - 1, 0.0, a0p)
        ga = _group_a_fwd(hf, buf, sh, a1buf, cw_ref, va_ref, ts)
        gb = _group_b_fwd(hf, va_ref, wcat_ref, bfull_ref, ts)
        gc = _group_c_fwd(hf, kt_ref, vm_ref)
        cat = jnp.concatenate([ga["a"], gb["g"], gc["c"]], axis=1).astype(BF16)

        dwout_ref[...] += _dot_tn(cat, dmix)
        dcat = _dot(dmix, woutt_ref[...])
        da = dcat[:, 0:CONV_W]
        dg = dcat[:, CONV_W:CONV_W + GMLP_W]
        dc = dcat[:, CONV_W + GMLP_W:D_MODEL].astype(BF16)

        dp = _dot(dc, vmt_ref[...])
        dvm_ref[...] += _dot_tn(gc["pb"], dc)
        dss = []
        for g in range(XATTN_HEADS):
            sl = slice(g * N_MEM, (g + 1) * N_MEM)
            pg = gc["p"][:, sl]
            dpg = dp[:, sl]
            dss.append(pg * (dpg - jnp.sum(dpg * pg, axis=-1, keepdims=True)))
        ds = jnp.concatenate(dss, axis=1).astype(BF16)
        dq = _dot(ds, ktt_ref[...])
        dkt_ref[...] += _dot_tn(gc["qb"], ds)

        dmixed = dg * gb["u"]
        dhu = dg * gb["mixed"] * gb["du"]
        dvns = []
        for j, c0 in enumerate(range(0, ts, CHUNK)):
            dm = dmixed[c0:c0 + CHUNK, :]
            dmb = dm.astype(BF16)
            dmsum_ref[...] += dm
            dwcat_ref[...] += _dot_nt(dmb, gb["stacks"][j])
            dst = _dot(wcatt_ref[...], dmb)
            dvn_c = jnp.zeros((CHUNK, GMLP_W), F32)
            for h in range(GMLP_HEADS):
                dvn_c = dvn_c + jnp.where(_head_mask(GMLP_W, h), dst[h * CHUNK:(h + 1) * CHUNK, :], 0.0)
            dvns.append(dvn_c)
        dvn = jnp.concatenate(dvns, axis=0) if len(dvns) > 1 else dvns[0]
        dva_ref[3:4, :] += _rowsum(dvn * gb["vhat"])
        dva_ref[4:5, :] += _rowsum(dvn)
        dhv = _ln_bwd(dvn, gb["vhat"], gb["rv"], va_ref[3:4, :]) * gb["dv"]

        a2, sa = ga["a2"], ga["sa"]
        da2 = da * (sa * (1.0 + a2 * (1.0 - sa)))
        dva_ref[1:2, :] += _rowsum(da2 * ga["a2h"])
        dva_ref[2:3, :] += _rowsum(da2)
        da1 = _ln_bwd(da2, ga["a2h"], ga["ra"], va_ref[1:2, :])
        dva_ref[0:1, :] += _rowsum(da1)
        dbuf[0:ts, :] = da1
        _shift_copies(dbuf, dsh, ts + HALO - 8)
        _conv31_dw(buf, sh, dbuf, dcw_ref, ts)
        _conv31_dx(dbuf, dsh, cw_ref, da0buf, ts)
        da0 = da0buf[...]
        sg = ga["sg"]
        dha = da0 * sg
        dhg = da0 * ga["ha"] * sg * (1.0 - sg)

        dh = jnp.concatenate([dha, dhg, dhu, dhv, dq], axis=1).astype(BF16)
        dx_ref[...] = _dot(dh, wint_ref[...]) + ALPHA * dz1
        dwin_ref[...] += _dot_tn(dh, x_ref[...].astype(BF16))

        @pl.when(i == n - 1)
        def _():
            _store_blocks(dwout_ref, slabs_ref, sems, off_out, D_MODEL // N_DEV, 0, N_DEV)
            _store_blocks(dwin_ref, slabs_ref, sems, off_in, IN_W // N_DEV, 0, N_DEV)

    rev = lambda width: pl.BlockSpec((ts, width), lambda i: (n - 1 - i, 0))
    prev = pl.BlockSpec((HALO, 2 * CONV_W), lambda i: (jnp.maximum((n - 1 - i) * halo_blocks - 1, 0), 0))
    hbm = pl.BlockSpec(memory_space=pl.ANY)
    hc = GMLP_HEADS * CHUNK
    am = XATTN_HEADS * N_MEM
    return pl.pallas_call(
        body, name="mixer_bwd", grid=(n,),
        in_specs=[hbm, rev(D_MODEL), rev(D_MODEL), rev(IN_W), prev, rev(D_MODEL), _const((HALO, CONV_W)),
                  _const((8, CONV_W)), _const((CHUNK, hc)), _const((hc, CHUNK)), _const((CHUNK, GMLP_W)),
                  _const((XATTN_W, am)), _const((am, XATTN_W)), _const((am, XATTN_W)), _const((XATTN_W, am)),
                  _const((D_MODEL, D_MODEL)), _const((IN_W, D_MODEL)), _const((8, D_MODEL))],
        out_specs=[hbm, rev(D_MODEL), _acc((XATTN_W, am)), _acc((am, XATTN_W)),
                   _acc((CHUNK, hc)), _acc((CHUNK, GMLP_W)), _acc((8, CONV_W)), _acc((HALO, CONV_W)),
                   _acc((8, D_MODEL))],
        out_shape=[jax.ShapeDtypeStruct(slabs.shape, F32), jax.ShapeDtypeStruct((seq, D_MODEL), F32),
                   jax.ShapeDtypeStruct((XATTN_W, am), F32),
                   jax.ShapeDtypeStruct((am, XATTN_W), F32), jax.ShapeDtypeStruct((CHUNK, hc), F32),
                   jax.ShapeDtypeStruct((CHUNK, GMLP_W), F32), jax.ShapeDtypeStruct((8, CONV_W), F32),
                   jax.ShapeDtypeStruct((HALO, CONV_W), F32), jax.ShapeDtypeStruct((8, D_MODEL), F32)],
        scratch_shapes=[pltpu.VMEM((ts + HALO, CONV_W), F32), pltpu.VMEM((ts, CONV_W), F32),
                        pltpu.VMEM((ts + HALO, CONV_W), F32), pltpu.VMEM((ts, CONV_W), F32),
                        pltpu.VMEM((D_MODEL, D_MODEL), F32), pltpu.VMEM((IN_W, D_MODEL), F32),
                        pltpu.SemaphoreType.DMA((N_DEV,)),
                        pltpu.VMEM((7, ts + HALO, CONV_W), F32), pltpu.VMEM((7, ts + HALO, CONV_W), F32)],
        input_output_aliases={0: 0},
        compiler_params=_params(dimension_semantics=("arbitrary",)),
    )(slabs, dx1, z1, hb, hb, x, w["cw"], w["va"], w["wcat"], w["wcatt"], w["bfull"], w["kt"], w["ktt"], w["vm"],
      w["vmt"], w["woutt"], w["wint"], w["v1"])


FFN_HALO = 8
FF_GROUP = D_FF // N_DEV
FF_GROUP_PAD = D_FF_PAD // N_DEV


def _ffn_gate(ubuf, cf_ref, ts, lo, hi):
    taps = tuple(ubuf[FFN_HALO - (FFN_CONV_K - 1) + k:FFN_HALO - (FFN_CONV_K - 1) + k + ts, lo:hi]
                 for k in range(FFN_CONV_K))
    g = cf_ref[3:4, lo:hi] + cf_ref[2:3, lo:hi] * taps[2]
    g = g + cf_ref[1:2, lo:hi] * taps[1]
    return g + cf_ref[0:1, lo:hi] * taps[0], taps


FFN_LANE_CHUNKS = ((0, D_FF_PAD // 2), (D_FF_PAD // 2, D_FF_PAD))


def _ffn_fwd_call(x1, w, ts):
    seq = x1.shape[0]
    n = seq // ts

    def body(x1_ref, wg_ref, wv_ref, cf_ref, wdown_ref, v2_ref, ug_ref, uv_ref, z2_ref, x2_ref, ubuf):
        i = pl.program_id(0)

        @pl.when(i == 0)
        def _():
            ubuf[0:FFN_HALO, :] = jnp.zeros((FFN_HALO, D_FF_PAD), F32)

        @pl.when(i > 0)
        def _():
            ubuf[0:FFN_HALO, :] = ubuf[ts:ts + FFN_HALO, :]

        xv = x1_ref[...]
        xb = xv.astype(BF16)
        y = ALPHA * xv
        for lo, hi in FFN_LANE_CHUNKS:
            ug = _dot(xb, wg_ref[:, lo:hi]).astype(BF16)
            uv = _dot(xb, wv_ref[:, lo:hi]).astype(BF16)
            ug_ref[:, lo:hi] = ug
            uv_ref[:, lo:hi] = uv
            ubuf[FFN_HALO:FFN_HALO + ts, lo:hi] = ug.astype(F32)
            gate, _ = _ffn_gate(ubuf, cf_ref, ts, lo, hi)
            act = (gate * _sigmoid(gate) * uv.astype(F32)).astype(BF16)
            y = y + _dot(act, wdown_ref[lo:hi, :])
        z2_ref[...] = y
        xh, _ = _ln_stats(y)
        x2_ref[...] = xh * v2_ref[0:1, :] + v2_ref[1:2, :]

    row = lambda width: pl.BlockSpec((ts, width), lambda i: (i, 0))
    return pl.pallas_call(
        body, name="ffn_fwd", grid=(n,),
        in_specs=[row(D_MODEL), _const((D_MODEL, D_FF_PAD)), _const((D_MODEL, D_FF_PAD)), _const((8, D_FF_PAD)),
                  _const((D_FF_PAD, D_MODEL)), _const((8, D_MODEL))],
        out_specs=[row(D_FF_PAD), row(D_FF_PAD), row(D_MODEL), row(D_MODEL)],
        out_shape=[jax.ShapeDtypeStruct((seq, D_FF_PAD), BF16), jax.ShapeDtypeStruct((seq, D_FF_PAD), BF16),
                   jax.ShapeDtypeStruct((seq, D_MODEL), F32), jax.ShapeDtypeStruct((seq, D_MODEL), F32)],
        scratch_shapes=[pltpu.VMEM((ts + FFN_HALO, D_FF_PAD), F32)],
        compiler_params=_params(dimension_semantics=("arbitrary",)),
    )(x1, w["wg"], w["wv"], w["cf"], w["wdown"], w["v2"])


FF_HALF = D_FF_PAD // 2
FF_HALF_CHUNKS = ((0, 768), (768, FF_HALF))


def _ffn_bwd_half_call(top, z2_or_addend, ug, uv, x1, w, ts, half, last, slabs, off_down, off_up):
    seq = ug.shape[0]
    n = seq // ts
    halo_blocks = ts // 16
    first = half == 0
    lane0 = half * FF_HALF

    def body(*refs):
        (slabs_in, top_ref, aux_ref, ug_ref, uv_ref, uprev_ref, x1_ref, cf_ref, wdownt_ref, wgt_ref, wvt_ref,
         v2_ref) = refs[:12]
        if first:
            slabs_ref, dx_ref, dcf_ref, dz2_ref, dv2_ref, loss_ref = refs[12:18]
            scratch = refs[18:]
        else:
            slabs_ref, dx_ref, dcf_ref = refs[12:15]
            scratch = refs[15:]
        ubuf, dgbuf, acc_down, acc_g, acc_v, sems = scratch
        i = pl.program_id(0)

        @pl.when(i == 0)
        def _():
            for ref in (acc_down, acc_g, acc_v, dcf_ref) + ((dv2_ref, loss_ref) if first else ()):
                ref[...] = jnp.zeros(ref.shape, F32)
            dgbuf[ts:ts + FFN_HALO, :] = jnp.zeros((FFN_HALO, FF_HALF), F32)

        @pl.when(i > 0)
        def _():
            dgbuf[ts:ts + FFN_HALO, :] = dgbuf[0:FFN_HALO, :]

        if first:
            xh2, r2 = _ln_stats(aux_ref[...])
            if last:
                diff = xh2 * v2_ref[0:1, :] + v2_ref[1:2, :] - top_ref[...]
                loss_ref[...] += jnp.sum(diff * diff) * (0.5 / D_MODEL)
                dx2v = diff * (1.0 / D_MODEL)
            else:
                dx2v = top_ref[...]
            dv2_ref[0:1, :] += _rowsum(dx2v * xh2)
            dv2_ref[1:2, :] += _rowsum(dx2v)
            dz2 = _ln_bwd(dx2v, xh2, r2, v2_ref[0:1, :])
            dz2_ref[...] = dz2
            dx = ALPHA * dz2
        else:
            dz2 = top_ref[...]
            dx = aux_ref[...]
        dy = dz2.astype(BF16)
        x1b = x1_ref[...].astype(BF16)

        up = uprev_ref[...].astype(F32)[8:16, :]
        ubuf[0:FFN_HALO, :] = jnp.where(i == n - 1, 0.0, up)
        ubuf[FFN_HALO:FFN_HALO + ts, :] = ug_ref[...].astype(F32)
        for lo, hi in FF_HALF_CHUNKS:
            glo, ghi = lane0 + lo, lane0 + hi
            taps = tuple(ubuf[FFN_HALO - (FFN_CONV_K - 1) + k:FFN_HALO - (FFN_CONV_K - 1) + k + ts, lo:hi]
                         for k in range(FFN_CONV_K))
            gate = cf_ref[3:4, glo:ghi] + cf_ref[2:3, glo:ghi] * taps[2]
            gate = gate + cf_ref[1:2, glo:ghi] * taps[1]
            gate = gate + cf_ref[0:1, glo:ghi] * taps[0]
            sg = _sigmoid(gate)
            sl = gate * sg
            uvf = uv_ref[:, lo:hi].astype(F32)
            act = (sl * uvf).astype(BF16)
            acc_down[lo:hi, :] += _dot_tn(act, dy)
            dact = _dot(dy, wdownt_ref[:, lo:hi])
            duv = (dact * sl).astype(BF16)
            dgate = dact * uvf * (sg * (1.0 + gate * (1.0 - sg)))
            dgbuf[0:ts, lo:hi] = dgate
            dcf_ref[3:4, lo:hi] += _rowsum(dgate)
            for k in range(FFN_CONV_K):
                dcf_ref[k:k + 1, lo:hi] += _rowsum(dgate * taps[k])
            dug = cf_ref[2:3, glo:ghi] * dgate + cf_ref[1:2, glo:ghi] * dgbuf[1:1 + ts, lo:hi]
            dug = (dug + cf_ref[0:1, glo:ghi] * dgbuf[2:2 + ts, lo:hi]).astype(BF16)
            acc_g[lo:hi, :] += _dot_tn(dug, x1b)
            acc_v[lo:hi, :] += _dot_tn(duv, x1b)
            dx = dx + _dot(dug, wgt_ref[lo:hi, :]) + _dot(duv, wvt_ref[lo:hi, :])
        dx_ref[...] = dx

        @pl.when(i == n - 1)
        def _():
            _store_blocks(acc_down, slabs_ref, sems, off_down, FF_GROUP_PAD, 4 * half, 4)
            _store_blocks(acc_g, slabs_ref, sems, off_up, 2 * FF_GROUP_PAD, 2 * half, 2)
            _store_blocks(acc_v, slabs_ref, sems, off_up, 2 * FF_GROUP_PAD, 4 + 2 * half, 2)

    rev = lambda width: pl.BlockSpec((ts, width), lambda i: (n - 1 - i, 0))
    rev_half = pl.BlockSpec((ts, FF_HALF), lambda i: (n - 1 - i, half))
    prev = pl.BlockSpec((16, FF_HALF), lambda i: (jnp.maximum((n - 1 - i) * halo_blocks - 1, 0), half))
    hbm = pl.BlockSpec(memory_space=pl.ANY)
    col_half = pl.BlockSpec((D_MODEL, FF_HALF), lambda i: (0, half), pipeline_mode=pl.Buffered(1))
    row_half = pl.BlockSpec((FF_HALF, D_MODEL), lambda i: (half, 0), pipeline_mode=pl.Buffered(1))
    out_specs = [hbm, rev(D_MODEL), _acc((8, FF_HALF))]
    out_shape = [jax.ShapeDtypeStruct(slabs.shape, F32), jax.ShapeDtypeStruct((seq, D_MODEL), F32),
                 jax.ShapeDtypeStruct((8, FF_HALF), F32)]
    if first:
        out_specs += [rev(D_MODEL), _acc((8, D_MODEL)), _acc((8, 128))]
        out_shape += [jax.ShapeDtypeStruct((seq, D_MODEL), F32), jax.ShapeDtypeStruct((8, D_MODEL), F32),
                      jax.ShapeDtypeStruct((8, 128), F32)]
    name = ("ffn_bwd_last" if last else "ffn_bwd") + ("_lo" if first else "_hi")
    return pl.pallas_call(
        body, name=name, grid=(n,),
        in_specs=[hbm, rev(D_MODEL), rev(D_MODEL), rev_half, rev_half, prev, rev(D_MODEL), _const((8, D_FF_PAD)),
                  col_half, row_half, row_half, _const((8, D_MODEL))],
        out_specs=out_specs, out_shape=out_shape,
        scratch_shapes=[pltpu.VMEM((ts + FFN_HALO, FF_HALF), F32), pltpu.VMEM((ts + FFN_HALO, FF_HALF), F32),
                        pltpu.VMEM((FF_HALF, D_MODEL), F32), pltpu.VMEM((FF_HALF, D_MODEL), F32),
                        pltpu.VMEM((FF_HALF, D_MODEL), F32), pltpu.SemaphoreType.DMA((4,))],
        input_output_aliases={0: 0},
        compiler_params=_params(dimension_semantics=("arbitrary",)),
    )(slabs, top, z2_or_addend, ug, uv, ug, x1, w["cf"], w["wdownt"], w["wgt"], w["wvt"], w["v2"])


MEM_FOLD = BLOB_LANES // XATTN_W


def _mem_proj_call(memq, wk_flat, wv_flat):
    def body(memq_ref, wk_ref, wv_ref, kh_ref, vh_ref):
        for w_ref, o_ref in ((wk_ref, kh_ref), (wv_ref, vh_ref)):
            acc = jnp.zeros((N_MEM, XATTN_W), F32)
            for q in range(MEM_FOLD):
                acc = acc + _dot(memq_ref[q], w_ref[:, q * XATTN_W:(q + 1) * XATTN_W])
            o_ref[...] = acc

    out = jax.ShapeDtypeStruct((N_MEM, XATTN_W), F32)
    return pl.pallas_call(body, name="mem_proj", out_shape=[out, out], compiler_params=_params())(memq, wk_flat, wv_flat)


def _mem_proj_bwd_call(memq, dkh, dvh, slabs, off_k, off_v):
    rows = D_MODEL // MEM_FOLD

    def body(slabs_in, memq_ref, dkh_ref, dvh_ref, slabs_ref, acc, sems):
        for d_ref, off in ((dkh_ref, off_k), (dvh_ref, off_v)):
            dv = d_ref[...].astype(BF16)
            for q in range(MEM_FOLD):
                acc[:, q * XATTN_W:(q + 1) * XATTN_W] = _dot_tn(memq_ref[q], dv)
            _store_blocks(acc, slabs_ref, sems, off, rows // N_DEV, 0, N_DEV)

    hbm = pl.BlockSpec(memory_space=pl.ANY)
    vmem = pl.BlockSpec(memory_space=pltpu.VMEM)
    return pl.pallas_call(
        body, name="mem_proj_bwd", in_specs=[hbm, vmem, vmem, vmem], out_specs=hbm,
        out_shape=jax.ShapeDtypeStruct(slabs.shape, F32),
        scratch_shapes=[pltpu.VMEM((rows, BLOB_LANES), F32), pltpu.SemaphoreType.DMA((N_DEV,))],
        input_output_aliases={0: 0}, compiler_params=_params(),
    )(slabs, memq, dkh, dvh)


def _place():
    return lax.axis_index("x"), lax.axis_index("y"), lax.axis_index("c")


def _all_gather_call(arrs, pieces, name):
    n_in, n_p = len(arrs), len(pieces)

    def body(*refs):
        ins, outs = refs[:n_in], refs[n_in:n_in + n_p]
        send_sems, recv_sems, local_sems = refs[n_in + n_p:]
        x, y, c = _place()
        me, sibling = (x, y, c), (x, y, 1 - c)
        chips = [(1 - x, y), (x, 1 - y), (1 - x, 1 - y)]

        def src(a):
            idx, r0, rows = pieces[a]
            return ins[idx] if r0 is None else ins[idx].at[pl.ds(r0, rows)]

        def slab(a, p):
            return outs[a].at[4 * p[0] + 2 * p[1] + p[2]]

        def copy(a, k, block, to, own=False):
            return pltpu.make_async_remote_copy(
                src_ref=src(a) if own else slab(a, block), dst_ref=slab(a, block),
                send_sem=send_sems.at[a, k], recv_sem=recv_sems.at[a, k], device_id=to, device_id_type=MESH)

        mine = [pltpu.make_async_copy(src(a), slab(a, me), local_sems.at[a]) for a in range(n_p)]
        for cp in mine:
            cp.start()
        first = []
        for a in range(n_p):
            first.append(copy(a, 0, me, sibling, own=True))
            first += [copy(a, 1 + j, me, (*chip, c), own=True) for j, chip in enumerate(chips)]
        for cp in first:
            cp.start()
        passed = []
        for a in range(n_p):
            for j, chip in enumerate(chips):
                copy(a, 1 + j, (*chip, c), me).wait_recv()
                cp = copy(a, 4 + j, (*chip, c), sibling)
                cp.start()
                passed.append(cp)
        for a in range(n_p):
            copy(a, 0, sibling, me).wait_recv()
            for j, chip in enumerate(chips):
                copy(a, 4 + j, (*chip, 1 - c), me).wait_recv()
        for cp in first + passed:
            cp.wait_send()
        for cp in mine:
            cp.wait()

    def out_shape(piece):
        idx, r0, rows = piece
        a = arrs[idx]
        return jax.ShapeDtypeStruct((N_DEV,) + (a.shape if r0 is None else (rows,) + a.shape[1:]), a.dtype)

    hbm = pl.BlockSpec(memory_space=pl.ANY)
    return pl.pallas_call(
        body, name=name,
        in_specs=[hbm] * n_in, out_specs=[hbm] * n_p, out_shape=[out_shape(p) for p in pieces],
        scratch_shapes=[pltpu.SemaphoreType.DMA((n_p, 7)), pltpu.SemaphoreType.DMA((n_p, 7)),
                        pltpu.SemaphoreType.DMA((n_p,))],
    )(*arrs)


def _swap_core_call(g):
    _, rows, width = g.shape

    def body(g_ref, r_ref, send_sems, recv_sems):
        x, y, c = _place()
        copies = []
        for k in range(4):
            cp = pltpu.make_async_remote_copy(
                src_ref=g_ref.at[2 * k + (1 - c)], dst_ref=r_ref.at[k], send_sem=send_sems.at[k],
                recv_sem=recv_sems.at[k], device_id=(x, y, 1 - c), device_id_type=MESH)
            cp.start()
            copies.append(cp)
        for cp in copies:
            cp.wait()

    hbm = pl.BlockSpec(memory_space=pl.ANY)
    return pl.pallas_call(
        body, name="rs_swap_core", in_specs=[hbm], out_specs=hbm,
        out_shape=jax.ShapeDtypeStruct((4, rows, width), g.dtype),
        scratch_shapes=[pltpu.SemaphoreType.DMA((4,)), pltpu.SemaphoreType.DMA((4,))],
    )(g)


def _swap_chip_call(p):
    _, rows, width = p.shape

    def body(p_ref, r_ref, send_sems, recv_sems):
        x, y, c = _place()
        chips = [(1 - x, y), (x, 1 - y), (1 - x, 1 - y)]
        copies = []
        for j, (px, py) in enumerate(chips):
            cp = pltpu.make_async_remote_copy(
                src_ref=p_ref.at[2 * px + py], dst_ref=r_ref.at[j], send_sem=send_sems.at[j],
                recv_sem=recv_sems.at[j], device_id=(px, py, c), device_id_type=MESH)
            cp.start()
            copies.append(cp)
        for cp in copies:
            cp.wait()

    hbm = pl.BlockSpec(memory_space=pl.ANY)
    return pl.pallas_call(
        body, name="rs_swap_chip", in_specs=[hbm], out_specs=hbm,
        out_shape=jax.ShapeDtypeStruct((3, rows, width), p.dtype),
        scratch_shapes=[pltpu.SemaphoreType.DMA((3,)), pltpu.SemaphoreType.DMA((3,))],
    )(p)


ADD_ROWS = 128


def _pair_add_call(g, r, c):
    _, rows, width = g.shape

    def body(c_ref, g_ref, r_ref, o_ref):
        o_ref[...] = (g_ref[...] + r_ref[...]).astype(BF16)

    return pl.pallas_call(
        body, name="rs_pair_add",
        grid_spec=pltpu.PrefetchScalarGridSpec(
            num_scalar_prefetch=1, grid=(4, rows // ADD_ROWS),
            in_specs=[pl.BlockSpec((None, ADD_ROWS, width), lambda k, i, c_ref: (2 * k + c_ref[0], i, 0)),
                      pl.BlockSpec((None, ADD_ROWS, width), lambda k, i, c_ref: (k, i, 0))],
            out_specs=pl.BlockSpec((None, ADD_ROWS, width), lambda k, i, c_ref: (k, i, 0))),
        out_shape=jax.ShapeDtypeStruct((4, rows, width), BF16),
        compiler_params=_params(dimension_semantics=("arbitrary", "arbitrary")),
    )(c, g, r)


def _adam(w, g, m, v):
    mn = ADAM_B1 * m + (1.0 - ADAM_B1) * g
    vn = ADAM_B2 * v + (1.0 - ADAM_B2) * (g * g)
    m_hat = mn / (1.0 - ADAM_B1 ** ADAM_STEP)
    v_hat = vn / (1.0 - ADAM_B2 ** ADAM_STEP)
    return -ADAM_LR * (m_hat / (jnp.sqrt(v_hat) + ADAM_EPS) + ADAM_WD * w), mn, vn


def _chip_add_adamw_call(slabs, from_sibling, from_chips, me, chip, w, m, v):
    _, rows, width = slabs.shape

    def body(me_ref, chip_ref, own_ref, sib_ref, r_ref, w_ref, m_ref, v_ref, g_ref, d_ref, mo_ref, vo_ref):
        g = own_ref[...] + sib_ref[...]
        for j in range(3):
            g = g + r_ref[j].astype(F32)
        g_ref[...] = g
        d_ref[...], mo_ref[...], vo_ref[...] = _adam(w_ref[...], g, m_ref[...], v_ref[...])

    spec = pl.BlockSpec((ADD_ROWS, width), lambda i, me_ref, chip_ref: (i, 0))
    return pl.pallas_call(
        body, name="rs_chip_add_adamw",
        grid_spec=pltpu.PrefetchScalarGridSpec(
            num_scalar_prefetch=2, grid=(rows // ADD_ROWS,),
            in_specs=[pl.BlockSpec((None, ADD_ROWS, width), lambda i, me_ref, chip_ref: (me_ref[0], i, 0)),
                      pl.BlockSpec((None, ADD_ROWS, width), lambda i, me_ref, chip_ref: (chip_ref[0], i, 0)),
                      pl.BlockSpec((3, ADD_ROWS, width), lambda i, me_ref, chip_ref: (0, i, 0)), spec, spec, spec],
            out_specs=[spec] * 4),
        out_shape=[jax.ShapeDtypeStruct((rows, width), F32)] * 4,
        compiler_params=_params(dimension_semantics=("arbitrary",)),
    )(me, chip, slabs, from_sibling, from_chips, w, m, v)


def _adamw_whole_call(params, name):
    n = len(params)

    def body(*refs):
        ins, outs = refs[:4 * n], refs[4 * n:]
        for a in range(n):
            w_ref, g_ref, m_ref, v_ref = ins[4 * a:4 * a + 4]
            outs[3 * a][...], outs[3 * a + 1][...], outs[3 * a + 2][...] = _adam(w_ref[...], g_ref[...], m_ref[...],
                                                                              v_ref[...])

    flat = [a for p in params for a in p]
    shapes = [jax.ShapeDtypeStruct(p[0].shape, F32) for p in params for _ in range(3)]
    out = pl.pallas_call(body, name=name, out_shape=shapes, compiler_params=_params())(*flat)
    return [tuple(out[3 * a:3 * a + 3]) for a in range(n)]


GATHERED_ACCS = (("dva", (8, CONV_W)), ("dv1", (8, D_MODEL)), ("dv2", (8, D_MODEL)), ("dcf", (8, D_FF_PAD)),
                 ("dcw", (HALO, CONV_W)), ("dwcat", (CHUNK, GMLP_HEADS * CHUNK)), ("dmsum", (CHUNK, GMLP_W)))
VEC_A = ("conv_a_b", "ln_a_g", "ln_a_b", "ln_v_g", "ln_v_b")
REP_IN_KERNEL = VEC_A + ("ln1_g", "ln1_b", "ln2_g", "ln2_b", "w_s", "b_s")


def _replicated_update_call(gathered, p, mom_m, mom_v):
    n_acc = len(GATHERED_ACCS)
    n_rep = len(REP_IN_KERNEL)

    def body(*refs):
        acc_refs = refs[:DEPTH * n_acc]
        wmv = refs[DEPTH * n_acc:DEPTH * n_acc + 3 * n_rep]
        outs = refs[DEPTH * n_acc + 3 * n_rep:]
        out_par = {nm: outs[4 * a:4 * a + 4] for a, nm in enumerate(REP_IN_KERNEL)}
        out_dcf = outs[4 * n_rep:4 * n_rep + DEPTH]
        out_dcw = outs[4 * n_rep + DEPTH:4 * n_rep + 2 * DEPTH]
        par = {nm: wmv[3 * a:3 * a + 3] for a, nm in enumerate(REP_IN_KERNEL)}
        tril = (lax.broadcasted_iota(jnp.int32, (CHUNK, CHUNK), 0) >= lax.broadcasted_iota(jnp.int32, (CHUNK, CHUNK), 1))
        head = lax.broadcasted_iota(jnp.int32, (8, GMLP_W), 0) * HEAD_DIM
        lane = lax.broadcasted_iota(jnp.int32, (8, GMLP_W), 1)
        sel = jnp.where((lane >= head) & (lane < head + HEAD_DIM), 1.0, 0.0)

        def update(nm, idx, g):
            w_ref, m_ref, v_ref = par[nm]
            d, mn, vn = _adam(w_ref[idx], g, m_ref[idx], v_ref[idx])
            g_ref, d_ref, mo_ref, vo_ref = out_par[nm]
            g_ref[idx] = g
            d_ref[idx] = d
            mo_ref[idx] = mn
            vo_ref[idx] = vn

        for l in range(DEPTH):
            tot = {}
            for a, (nm, _) in enumerate(GATHERED_ACCS):
                ref = acc_refs[l * n_acc + a]
                s = ref[0]
                for j in range(1, N_DEV):
                    s = s + ref[j]
                tot[nm] = s
            out_dcf[l][...] = tot["dcf"]
            out_dcw[l][...] = tot["dcw"]
            row = (slice(l, l + 1), slice(None))
            for k, nm in enumerate(VEC_A):
                update(nm, row, tot["dva"][k:k + 1, :])
            update("ln1_g", row, tot["dv1"][0:1, :])
            update("ln1_b", row, tot["dv1"][1:2, :])
            update("ln2_g", row, tot["dv2"][0:1, :])
            update("ln2_b", row, tot["dv2"][1:2, :])
            for h in range(GMLP_HEADS):
                gw = jnp.where(tril, tot["dwcat"][:, h * CHUNK:(h + 1) * CHUNK], 0.0)
                update("w_s", (l, h), gw)
            gb = lax.dot_general(sel, tot["dmsum"], (((1,), (1,)), ((), ())), precision=lax.Precision.HIGHEST,
                                 preferred_element_type=F32)
            for h in range(GMLP_HEADS):
                update("b_s", (l, slice(h, h + 1), slice(None)), gb[h:h + 1, :])

    ins = [gathered[l][nm] for l in range(DEPTH) for nm, _ in GATHERED_ACCS]
    ins += [t[nm] for nm in REP_IN_KERNEL for t in (p, mom_m, mom_v)]
    shapes = [jax.ShapeDtypeStruct(p[nm].shape, F32) for nm in REP_IN_KERNEL for _ in range(4)]
    shapes += [jax.ShapeDtypeStruct((8, D_FF_PAD), F32)] * DEPTH + [jax.ShapeDtypeStruct((HALO, CONV_W), F32)] * DEPTH
    out = pl.pallas_call(body, name="replicated_update", out_shape=shapes, compiler_params=_params())(*ins)
    res = [{nm: out[4 * a + k] for a, nm in enumerate(REP_IN_KERNEL)} for k in range(4)]
    return res, out[4 * n_rep:4 * n_rep + DEPTH], out[4 * n_rep + DEPTH:]


BLOCK_ROWS = (("w_in", IN_W // N_DEV), ("w_out", D_MODEL // N_DEV), ("w_up", 2 * FF_GROUP_PAD),
              ("w_down", FF_GROUP_PAD), ("w_mk", D_MODEL // N_DEV // MEM_FOLD), ("w_mv", D_MODEL // N_DEV // MEM_FOLD))
LAYER_ROWS = sum(r for _, r in BLOCK_ROWS)
BLOB_ROWS = DEPTH * LAYER_ROWS
assert BLOB_ROWS % ADD_ROWS == 0 and all(r % 16 == 0 for _, r in BLOCK_ROWS)


def _row_off(l, name):
    off = l * LAYER_ROWS
    for nm, r in BLOCK_ROWS:
        if nm == name:
            return off
        off += r
    raise KeyError(name)


def _to_rows(name, a):
    if name == "w_in":
        return a.T
    if name == "w_up":
        t = a.T.reshape(2, FF_GROUP, D_MODEL)
        return jnp.pad(t, ((0, 0), (0, FF_GROUP_PAD - FF_GROUP), (0, 0))).reshape(2 * FF_GROUP_PAD, D_MODEL)
    if name == "w_down":
        return jnp.pad(a, ((0, FF_GROUP_PAD - FF_GROUP), (0, 0)))
    if name == "w_out":
        return a
    return a.reshape(-1, BLOB_LANES)


def _from_rows(name, r):
    if name == "w_in":
        return r.T
    if name == "w_up":
        return r.reshape(2, FF_GROUP_PAD, D_MODEL)[:, :FF_GROUP].reshape(2 * FF_GROUP, D_MODEL).T
    if name == "w_down":
        return r[:FF_GROUP]
    if name == "w_out":
        return r
    return r.reshape(D_MODEL // N_DEV, XATTN_W)


def _blob(tree):
    return jnp.concatenate([_to_rows(nm, tree[nm][l]) for l in range(DEPTH) for nm, _ in BLOCK_ROWS], axis=0)


def _unblob(blob):
    out = {}
    for nm, r in BLOCK_ROWS:
        out[nm] = jnp.stack([_from_rows(nm, blob[_row_off(l, nm):_row_off(l, nm) + r]) for l in range(DEPTH)])
    return out


def _ff_interleave(a):
    lead = a.shape[:-1]
    t = a.reshape(lead + (N_DEV, FF_GROUP))
    return jnp.pad(t, [(0, 0)] * len(lead) + [(0, 0), (0, FF_GROUP_PAD - FF_GROUP)]).reshape(lead + (D_FF_PAD,))


def _ff_deinterleave(a):
    lead = a.shape[:-1]
    return a.reshape(lead + (N_DEV, FF_GROUP_PAD))[..., :FF_GROUP].reshape(lead + (D_FF,))


def _head_table():
    hd = jnp.arange(XATTN_W) // HEAD_DIM
    return (hd[None, :] == jnp.arange(XATTN_HEADS)[:, None]).astype(F32)


def _layer_operands(full, conv_a_w, conv_f_w, p, l, memq):
    mat = {nm: full[nm].reshape(-1, BLOB_LANES) for nm in full}
    w = {}
    w["wint"] = mat["w_in"]
    w["win"] = mat["w_in"].T
    w["wout"] = mat["w_out"]
    w["woutt"] = mat["w_out"].T
    w["wgt"] = mat["w_up"][:D_FF_PAD]
    w["wvt"] = mat["w_up"][D_FF_PAD:]
    w["wg"] = w["wgt"].T
    w["wv"] = w["wvt"].T
    w["wdown"] = mat["w_down"]
    w["wdownt"] = mat["w_down"].T
    w["cw"] = conv_a_w
    zeros = jnp.zeros((3, CONV_W), F32)
    w["va"] = jnp.concatenate([p[nm][l][None] for nm in VEC_A] + [zeros], axis=0)
    tril = jnp.tril(jnp.ones((CHUNK, CHUNK), F32))
    w["wcat"] = (p["w_s"][l] * tril[None]).transpose(1, 0, 2).reshape(CHUNK, GMLP_HEADS * CHUNK).astype(BF16)
    w["wcatt"] = w["wcat"].T
    w["bfull"] = jnp.repeat(p["b_s"][l].T, HEAD_DIM, axis=1)
    kh, vh = _mem_proj_call(memq, mat["w_mk"], mat["w_mv"])
    hm = _head_table()
    scale = 1.0 / math.sqrt(HEAD_DIM)
    w["kt"] = (kh.T[:, None, :] * hm.T[:, :, None] * scale).reshape(XATTN_W, XATTN_HEADS * N_MEM).astype(BF16)
    w["ktt"] = w["kt"].T
    w["vm"] = (hm[:, None, :] * vh[None]).reshape(XATTN_HEADS * N_MEM, XATTN_W).astype(BF16)
    w["vmt"] = w["vm"].T
    zeros = jnp.zeros((6, D_MODEL), F32)
    w["v1"] = jnp.concatenate([p["ln1_g"][l][None], p["ln1_b"][l][None], zeros], axis=0)
    w["v2"] = jnp.concatenate([p["ln2_g"][l][None], p["ln2_b"][l][None], zeros], axis=0)
    w["cf"] = jnp.concatenate([conv_f_w, _ff_interleave(p["conv_f_b"][l][None]), jnp.zeros((4, D_FF_PAD), F32)], axis=0)
    return w


TS_MIXER = 256
TS_FFN = 256
CONV_A_SHARD = CONV_W // N_DEV


def kernel(x, mem, w_in, conv_a_w, conv_a_b, ln_a_g, ln_a_b, ln_v_g, ln_v_b, w_s, b_s, w_mk, w_mv, w_out, ln1_g, ln1_b, w_up, conv_f_w, conv_f_b, w_down, ln2_g, ln2_b, loss_target, m_w_in, m_conv_a_w, m_conv_a_b, m_ln_a_g, m_ln_a_b, m_ln_v_g, m_ln_v_b, m_w_s, m_b_s, m_w_mk, m_w_mv, m_w_out, m_ln1_g, m_ln1_b, m_w_up, m_conv_f_w, m_conv_f_b, m_w_down, m_ln2_g, m_ln2_b, v_w_in, v_conv_a_w, v_conv_a_b, v_ln_a_g, v_ln_a_b, v_ln_v_g, v_ln_v_b, v_w_s, v_b_s, v_w_mk, v_w_mv, v_w_out, v_ln1_g, v_ln1_b, v_w_up, v_conv_f_w, v_conv_f_b, v_w_down, v_ln2_g, v_ln2_b):
    given = dict(locals())
    p = {nm: given[nm] for nm in WEIGHTS}
    mom_m = {nm: given["m_" + nm] for nm in WEIGHTS}
    mom_v = {nm: given["v_" + nm] for nm in WEIGHTS}
    seq = x.shape[1]
    ts_m, ts_f = min(TS_MIXER, seq), min(TS_FFN, seq)
    cx, cy, cc = _place()
    me = 4 * cx + 2 * cy + cc

    w_blob = _blob(p)
    conv_a_tile = jnp.pad(conv_a_w, ((0, 0), (0, HALO - CONV_K), (0, 128 - CONV_A_SHARD)))
    conv_f_tile = jnp.pad(conv_f_w, ((0, 0), (0, 8 - FFN_CONV_K), (0, 384 - FF_GROUP)))
    pieces = [(0, _row_off(l, nm), r) for l in range(DEPTH) for nm, r in BLOCK_ROWS] + [(1, None, 0), (2, None, 0)]
    gathered = _all_gather_call([w_blob.astype(BF16), conv_a_tile, conv_f_tile], pieces, "gather_weights")
    n_mat = len(BLOCK_ROWS)
    conv_a_all, conv_f_all = gathered[DEPTH * n_mat], gathered[DEPTH * n_mat + 1]
    memq = mem[0].reshape(N_MEM, D_MODEL // MEM_FOLD, MEM_FOLD).transpose(2, 0, 1).astype(BF16)
    ops = []
    for l in range(DEPTH):
        full = {nm: gathered[l * n_mat + a] for a, (nm, _) in enumerate(BLOCK_ROWS)}
        ca = conv_a_all[:, l, :, :CONV_A_SHARD].transpose(1, 0, 2).reshape(HALO, CONV_W)
        cf = conv_f_all[:, l, :FFN_CONV_K, :FF_GROUP_PAD].transpose(1, 0, 2).reshape(FFN_CONV_K, D_FF_PAD)
        ops.append(_layer_operands(full, ca, cf, p, l, memq))

    saved = []
    xl = x[0]
    for l in range(DEPTH):
        hb, z1, x1 = _mixer_fwd_call(xl, ops[l], ts_m)
        ug, uv, z2, x2 = _ffn_fwd_call(x1, ops[l], ts_f)
        saved.append(dict(x=xl, hb=hb, z1=z1, x1=x1, ug=ug, uv=uv, z2=z2))
        xl = x2

    hm = _head_table()
    slabs = lax.empty((N_DEV, BLOB_ROWS, BLOB_LANES), F32)
    accs = [None] * DEPTH
    dx = loss_target[0]
    loss = None
    for l in reversed(range(DEPTH)):
        s, w = saved[l], ops[l]
        last = l == DEPTH - 1
        off_down, off_up = _row_off(l, "w_down"), _row_off(l, "w_up")
        slabs, dxa, dcf_lo, dz2, dv2, loss_acc = _ffn_bwd_half_call(dx, s["z2"], s["ug"], s["uv"], s["x1"], w, ts_f, 0,
                                                                    last, slabs, off_down, off_up)
        slabs, dx1, dcf_hi = _ffn_bwd_half_call(dz2, dxa, s["ug"], s["uv"], s["x1"], w, ts_f, 1, last, slabs, off_down,
                                                off_up)
        dcf = jnp.concatenate([dcf_lo, dcf_hi], axis=1)
        if last:
            loss = loss_acc[0, 0]
        (slabs, dx, dkt, dvm, dwcat, dmsum, dva, dcw, dv1) = _mixer_bwd_call(
            dx1, s["z1"], s["hb"], s["x"], w, ts_m, slabs, _row_off(l, "w_out"), _row_off(l, "w_in"))
        dkh = jnp.einsum("hd,dhm->md", hm, dkt.reshape(XATTN_W, XATTN_HEADS, N_MEM)) * (1.0 / math.sqrt(HEAD_DIM))
        dvh = jnp.einsum("hd,hmd->md", hm, dvm.reshape(XATTN_HEADS, N_MEM, XATTN_W))
        slabs = _mem_proj_bwd_call(memq, dkh, dvh, slabs, _row_off(l, "w_mk"), _row_off(l, "w_mv"))
        accs[l] = dict(dva=dva, dv1=dv1, dv2=dv2, dcf=dcf, dcw=dcw, dwcat=dwcat, dmsum=dmsum)
    grad_x = dx[None]

    from_sibling = _swap_core_call(slabs)
    chip_sum = _pair_add_call(slabs, from_sibling, cc.reshape(1).astype(jnp.int32))
    from_chips = _swap_chip_call(chip_sum)
    g_sh, d_sh, m_sh, v_sh = _chip_add_adamw_call(
        slabs, from_sibling, from_chips, me.reshape(1).astype(jnp.int32), (2 * cx + cy).reshape(1).astype(jnp.int32),
        w_blob, _blob(mom_m), _blob(mom_v))
    outs = [_unblob(b) for b in (g_sh, d_sh, m_sh, v_sh)]

    acc_list = [accs[l][nm] for l in range(DEPTH) for nm, _ in GATHERED_ACCS]
    acc_all = _all_gather_call(acc_list, [(a, None, 0) for a in range(len(acc_list))], "gather_small_grads")
    n_acc = len(GATHERED_ACCS)
    gathered_accs = [{nm: acc_all[l * n_acc + a] for a, (nm, _) in enumerate(GATHERED_ACCS)} for l in range(DEPTH)]
    rep, dcf_sum, dcw_sum = _replicated_update_call(gathered_accs, p, mom_m, mom_v)
    for k in range(4):
        outs[k].update(rep[k])
    dcf_sum, dcw_sum = jnp.stack(dcf_sum), jnp.stack(dcw_sum)
    zero = jnp.zeros((), jnp.int32)
    g_conv_a_w = lax.dynamic_slice(dcw_sum, (zero, zero, CONV_A_SHARD * me), (DEPTH, CONV_K, CONV_A_SHARD))
    g_conv_f_w = lax.dynamic_slice(dcf_sum, (zero, zero, FF_GROUP_PAD * me), (DEPTH, FFN_CONV_K, FF_GROUP))
    g_conv_f_b = _ff_deinterleave(dcf_sum[:, FFN_CONV_K])
    conv_grads = dict(conv_a_w=g_conv_a_w, conv_f_w=g_conv_f_w, conv_f_b=g_conv_f_b)
    conv_names = tuple(conv_grads)
    upd = _adamw_whole_call([(p[nm], conv_grads[nm], mom_m[nm], mom_v[nm]) for nm in conv_names], "adamw_conv")
    for nm, (d, mn, vn) in zip(conv_names, upd):
        outs[0][nm], outs[1][nm], outs[2][nm], outs[3][nm] = conv_grads[nm], d, mn, vn

    loss = lax.psum(loss, ("x", "y", "c"))
    return (loss, grad_x, *[outs[0][nm] for nm in WEIGHTS], *[outs[1][nm] for nm in WEIGHTS],
            *[outs[2][nm] for nm in WEIGHTS], *[outs[3][nm] for nm in WEIGHTS])
```

```python
import math

import jax
import jax.numpy as jnp
from jax import lax
from jax.experimental import pallas as pl
from jax.experimental.pallas import tpu as pltpu

F32 = jnp.float32
BF16 = jnp.bfloat16

DEPTH = 2
D_MODEL = 1024
CONV_W = 384
GMLP_W = 384
XATTN_W = 256
HEAD_DIM = 64
GMLP_HEADS = 6
XATTN_HEADS = 4
IN_W = 1792
CONV_K = 31
CHUNK = 128
N_MEM = 256
D_FF = 2752
D_FF_PAD = 2816
FFN_CONV_K = 3
ALPHA = (2.0 * DEPTH) ** 0.25
LN_EPS = 1e-5
N_DEV = 8

ADAM_LR = 0.001
ADAM_B1 = 0.9
ADAM_B2 = 0.999
ADAM_EPS = 1e-08
ADAM_WD = 0.01
ADAM_STEP = 10

HALO = 32
CONV_ROWS = 32
V7X_VMEM_BYTES = 64 * 1024 * 1024
VMEM_LIMIT = V7X_VMEM_BYTES - 8 * 1024 * 1024
BLOB_LANES = 1024

MESH = pl.DeviceIdType.MESH

WEIGHTS = ("w_in", "conv_a_w", "conv_a_b", "ln_a_g", "ln_a_b", "ln_v_g", "ln_v_b", "w_s", "b_s", "w_mk", "w_mv",
           "w_out", "ln1_g", "ln1_b", "w_up", "conv_f_w", "conv_f_b", "w_down", "ln2_g", "ln2_b")


def _params(**kw):
    return pltpu.CompilerParams(vmem_limit_bytes=VMEM_LIMIT, **kw)


def _const(shape):
    nd = len(shape)
    return pl.BlockSpec(shape, lambda i: (0,) * nd, pipeline_mode=pl.Buffered(1))


def _acc(shape):
    nd = len(shape)
    return pl.BlockSpec(shape, lambda i: (0,) * nd)


class _Exchange:
    def __init__(self, arrays, out_shapes, n_copies, build):
        self.arrays, self.out_shapes, self.n_copies, self.build = list(arrays), list(out_shapes), n_copies, build


def _carry(core, n_in, n_out, exch, n_steps):
    if exch is None:
        return core
    nx_in, nx_out = len(exch.arrays), len(exch.out_shapes)

    def body(*refs):
        o0 = n_in + nx_in
        s0 = o0 + n_out + nx_out
        x_in, x_out, sems = refs[n_in:o0], refs[o0 + n_out:s0], refs[-3:]
        i = pl.program_id(0)

        @pl.when(i == 0)
        def _():
            remote, local = exch.build(x_in, x_out, *sems)
            for cp in remote + local:
                cp.start()

        core(*refs[:n_in], *refs[o0:o0 + n_out], *refs[s0:-3])

        @pl.when(i == n_steps - 1)
        def _():
            remote, local = exch.build(x_in, x_out, *sems)
            for cp in remote + local:
                cp.wait()

    return body


def _grid_call(core, name, n_steps, in_specs, out_specs, out_shape, scratch_shapes, args, aliases=None, exch=None):
    hbm = pl.BlockSpec(memory_space=pl.ANY)
    n_in, n_out = len(in_specs), len(out_specs)
    in_specs, out_specs, out_shape, scratch_shapes, args = (list(in_specs), list(out_specs), list(out_shape),
                                                            list(scratch_shapes), list(args))
    if exch is not None:
        in_specs += [hbm] * len(exch.arrays)
        out_specs += [hbm] * len(exch.out_shapes)
        out_shape += exch.out_shapes
        scratch_shapes += [pltpu.SemaphoreType.DMA((exch.n_copies,)), pltpu.SemaphoreType.DMA((exch.n_copies,)),
                           pltpu.SemaphoreType.DMA((1,))]
        args += exch.arrays
    out = pl.pallas_call(
        _carry(core, n_in, n_out, exch, n_steps), name=name, grid=(n_steps,), in_specs=in_specs, out_specs=out_specs,
        out_shape=out_shape, scratch_shapes=scratch_shapes, input_output_aliases=aliases or {},
        compiler_params=_params(dimension_semantics=("arbitrary",)))(*args)
    return list(out[:n_out]), list(out[n_out:])


def _sigmoid(x):
    return 1.0 / (1.0 + jnp.exp(-x))


_GELU_C = math.sqrt(2.0 / math.pi)


def _gelu(x):
    x2 = x * x
    t = jnp.tanh(_GELU_C * (x + 0.044715 * x * x2))
    g = 0.5 * x * (1.0 + t)
    dg = 0.5 * (1.0 + t) + 0.5 * x * (1.0 - t * t) * (_GELU_C * (1.0 + 3.0 * 0.044715 * x2))
    return g, dg


def _ln_stats(z):
    mu = jnp.mean(z, axis=-1, keepdims=True)
    zc = z - mu
    var = jnp.mean(zc * zc, axis=-1, keepdims=True)
    r = lax.rsqrt(var + LN_EPS)
    return zc * r, r


def _ln_bwd(dy, xh, r, g):
    dxh = dy * g
    m1 = jnp.mean(dxh, axis=-1, keepdims=True)
    m2 = jnp.mean(dxh * xh, axis=-1, keepdims=True)
    return r * (dxh - m1 - xh * m2)


def _rowsum(x):
    return jnp.sum(x, axis=0, keepdims=True)


def _dot(a, b):
    return jnp.dot(a, b, preferred_element_type=F32)


def _dot_tn(a, b):
    return lax.dot_general(a, b, (((0,), (0,)), ((), ())), preferred_element_type=F32)


def _dot_nt(a, b):
    return lax.dot_general(a, b, (((1,), (1,)), ((), ())), preferred_element_type=F32)


def _shift_copies(buf, sh, rows):
    for b in range(1, 8):
        sh[b - 1, 0:rows, :] = buf[b:b + rows, :]


def _window(buf, sh, start):
    b = start % 8
    a = start - b
    return buf[a:a + CONV_ROWS, :] if b == 0 else sh[b - 1, a:a + CONV_ROWS, :]


def _conv31_fwd(buf, sh, w_ref, bias, out, ts):
    for r0 in range(0, ts, CONV_ROWS):
        acc = jnp.broadcast_to(bias, (CONV_ROWS, CONV_W))
        for k in range(CONV_K):
            acc = acc + w_ref[k:k + 1, :] * _window(buf, sh, r0 + HALO - (CONV_K - 1) + k)
        out[r0:r0 + CONV_ROWS, :] = acc


def _conv31_dx(dbuf, dsh, w_ref, out, ts):
    for r0 in range(0, ts, CONV_ROWS):
        acc = jnp.zeros((CONV_ROWS, CONV_W), F32)
        for k in range(CONV_K):
            acc = acc + w_ref[k:k + 1, :] * _window(dbuf, dsh, r0 + (CONV_K - 1) - k)
        out[r0:r0 + CONV_ROWS, :] = acc


def _conv31_dw(buf, sh, dbuf, dw_ref, ts):
    for k in range(CONV_K):
        part = jnp.zeros((8, CONV_W), F32)
        for r0 in range(0, ts, CONV_ROWS):
            m = dbuf[r0:r0 + CONV_ROWS, :] * _window(buf, sh, r0 + HALO - (CONV_K - 1) + k)
            for q in range(0, CONV_ROWS, 8):
                part = part + m[q:q + 8, :]
        dw_ref[k:k + 1, :] += _rowsum(part)


def _head_mask(width, h):
    lane = lax.broadcasted_iota(jnp.int32, (CHUNK, width), 1)
    return (lane >= h * HEAD_DIM) & (lane < (h + 1) * HEAD_DIM)


def _stack_heads(vn_c):
    return jnp.concatenate([jnp.where(_head_mask(GMLP_W, h), vn_c, 0.0) for h in range(GMLP_HEADS)], axis=0)


def _group_a_fwd(hf, buf, sh, a1buf, cw_ref, va_ref, ts):
    ha = hf[:, 0:CONV_W]
    sg = _sigmoid(hf[:, CONV_W:2 * CONV_W])
    buf[HALO:HALO + ts, :] = ha * sg
    _shift_copies(buf, sh, ts + HALO - 8)
    _conv31_fwd(buf, sh, cw_ref, va_ref[0:1, :], a1buf, ts)
    a2h, ra = _ln_stats(a1buf[...])
    a2 = a2h * va_ref[1:2, :] + va_ref[2:3, :]
    sa = _sigmoid(a2)
    return dict(ha=ha, sg=sg, a2h=a2h, ra=ra, a2=a2, sa=sa, a=a2 * sa)


def _group_b_fwd(hf, va_ref, wcat_ref, bfull_ref, ts):
    hu = hf[:, 2 * CONV_W:2 * CONV_W + GMLP_W]
    hv = hf[:, 2 * CONV_W + GMLP_W:2 * CONV_W + 2 * GMLP_W]
    u, du = _gelu(hu)
    v, dv = _gelu(hv)
    vhat, rv = _ln_stats(v)
    vn = vhat * va_ref[3:4, :] + va_ref[4:5, :]
    stacks, mixed = [], []
    for c0 in range(0, ts, CHUNK):
        st = _stack_heads(vn[c0:c0 + CHUNK, :]).astype(BF16)
        stacks.append(st)
        mixed.append(_dot(wcat_ref[...], st) + bfull_ref[...])
    mixed = jnp.concatenate(mixed, axis=0) if len(mixed) > 1 else mixed[0]
    return dict(u=u, du=du, dv=dv, vhat=vhat, rv=rv, stacks=stacks, mixed=mixed, g=u * mixed)


def _group_c_fwd(hf, kt_ref, vm_ref):
    qb = hf[:, IN_W - XATTN_W:IN_W].astype(BF16)
    s_all = _dot(qb, kt_ref[...])
    ps = []
    for g in range(XATTN_HEADS):
        s = s_all[:, g * N_MEM:(g + 1) * N_MEM]
        e = jnp.exp(s - jnp.max(s, axis=-1, keepdims=True))
        ps.append(e / jnp.sum(e, axis=-1, keepdims=True))
    p_all = jnp.concatenate(ps, axis=1)
    pb = p_all.astype(BF16)
    return dict(qb=qb, p=p_all, pb=pb, c=_dot(pb, vm_ref[...]))


def _mixer_fwd_call(x, w, ts, exch=None):
    seq = x.shape[0]
    n = seq // ts

    def body(x_ref, win_ref, cw_ref, va_ref, wcat_ref, bfull_ref, kt_ref, vm_ref, wout_ref, v1_ref,
             hb_ref, z1_ref, x1_ref, buf, a1buf, sh):
        i = pl.program_id(0)

        @pl.when(i == 0)
        def _():
            buf[0:HALO, :] = jnp.zeros((HALO, CONV_W), F32)

        @pl.when(i > 0)
        def _():
            buf[0:HALO, :] = buf[ts:ts + HALO, :]

        xv = x_ref[...]
        hb = _dot(xv.astype(BF16), win_ref[...]).astype(BF16)
        hb_ref[...] = hb
        hf = hb.astype(F32)
        ga = _group_a_fwd(hf, buf, sh, a1buf, cw_ref, va_ref, ts)
        gb = _group_b_fwd(hf, va_ref, wcat_ref, bfull_ref, ts)
        gc = _group_c_fwd(hf, kt_ref, vm_ref)
        cat = jnp.concatenate([ga["a"], gb["g"], gc["c"]], axis=1).astype(BF16)
        z1 = ALPHA * xv + _dot(cat, wout_ref[...])
        z1_ref[...] = z1
        xh, _ = _ln_stats(z1)
        x1_ref[...] = xh * v1_ref[0:1, :] + v1_ref[1:2, :]

    row = lambda width: pl.BlockSpec((ts, width), lambda i: (i, 0))
    return _grid_call(
        body, "mixer_fwd", n,
        in_specs=[row(D_MODEL), _const((D_MODEL, IN_W)), _const((HALO, CONV_W)), _const((8, CONV_W)),
                  _const((CHUNK, GMLP_HEADS * CHUNK)), _const((CHUNK, GMLP_W)), _const((XATTN_W, XATTN_HEADS * N_MEM)),
                  _const((XATTN_HEADS * N_MEM, XATTN_W)), _const((D_MODEL, D_MODEL)), _const((8, D_MODEL))],
        out_specs=[row(IN_W), row(D_MODEL), row(D_MODEL)],
        out_shape=[jax.ShapeDtypeStruct((seq, IN_W), BF16), jax.ShapeDtypeStruct((seq, D_MODEL), F32),
                   jax.ShapeDtypeStruct((seq, D_MODEL), F32)],
        scratch_shapes=[pltpu.VMEM((ts + HALO, CONV_W), F32), pltpu.VMEM((ts, CONV_W), F32),
                        pltpu.VMEM((7, ts + HALO, CONV_W), F32)],
        args=(x, w["win"], w["cw"], w["va"], w["wcat"], w["bfull"], w["kt"], w["vm"], w["wout"], w["v1"]), exch=exch)


def _store_blocks(acc, slabs_ref, sems, row_off, rows, first_block, n_blocks):
    copies = [pltpu.make_async_copy(acc.at[pl.ds(q * rows, rows)], slabs_ref.at[first_block + q, pl.ds(row_off, rows)],
                                    sems.at[q]) for q in range(n_blocks)]
    for cp in copies:
        cp.start()
    for cp in copies:
        cp.wait()


def _mixer_bwd_call(dx1, z1, hb, x, w, ts, slabs, off_out, off_in):
    seq = dx1.shape[0]
    n = seq // ts
    halo_blocks = ts // HALO

    def body(slabs_in, dx1_ref, z1_ref, hb_ref, hprev_ref, x_ref, cw_ref, va_ref, wcat_ref, wcatt_ref, bfull_ref,
             kt_ref, ktt_ref, vm_ref, vmt_ref, woutt_ref, wint_ref, v1_ref,
             slabs_ref, dx_ref, dkt_ref, dvm_ref, dwcat_ref, dmsum_ref, dva_ref, dcw_ref, dv1_ref,
             buf, a1buf, dbuf, da0buf, dwout_ref, dwin_ref, sems, sh, dsh):
        i = pl.program_id(0)

        @pl.when(i == 0)
        def _():
            for ref in (dwout_ref, dwin_ref, dkt_ref, dvm_ref, dwcat_ref, dmsum_ref, dva_ref, dcw_ref, dv1_ref):
                ref[...] = jnp.zeros(ref.shape, F32)
            dbuf[ts:ts + HALO, :] = jnp.zeros((HALO, CONV_W), F32)

        @pl.when(i > 0)
        def _():
            dbuf[ts:ts + HALO, :] = dbuf[0:HALO, :]

        dx1v = dx1_ref[...]
        xh1, r1 = _ln_stats(z1_ref[...])
        dv1_ref[0:1, :] += _rowsum(dx1v * xh1)
        dv1_ref[1:2, :] += _rowsum(dx1v)
        dz1 = _ln_bwd(dx1v, xh1, r1, v1_ref[0:1, :])
        dmix = dz1.astype(BF16)

        hf = hb_ref[...].astype(F32)
        hp = hprev_ref[...].astype(F32)
        a0p = hp[:, 0:CONV_W] * _sigmoid(hp[:, CONV_W:2 * CONV_W])
        buf[0:HALO, :] = jnp.where(i == n - 1, 0.0, a0p)
        ga = _group_a_fwd(hf, buf, sh, a1buf, cw_ref, va_ref, ts)
        gb = _group_b_fwd(hf, va_ref, wcat_ref, bfull_ref, ts)
        gc = _group_c_fwd(hf, kt_ref, vm_ref)
        cat = jnp.concatenate([ga["a"], gb["g"], gc["c"]], axis=1).astype(BF16)

        dwout_ref[...] += _dot_tn(cat, dmix)
        dcat = _dot(dmix, woutt_ref[...])
        da = dcat[:, 0:CONV_W]
        dg = dcat[:, CONV_W:CONV_W + GMLP_W]
        dc = dcat[:, CONV_W + GMLP_W:D_MODEL].astype(BF16)

        dp = _dot(dc, vmt_ref[...])
        dvm_ref[...] += _dot_tn(gc["pb"], dc)
        dss = []
        for g in range(XATTN_HEADS):
            sl = slice(g * N_MEM, (g + 1) * N_MEM)
            pg = gc["p"][:, sl]
            dpg = dp[:, sl]
            dss.append(pg * (dpg - jnp.sum(dpg * pg, axis=-1, keepdims=True)))
        ds = jnp.concatenate(dss, axis=1).astype(BF16)
        dq = _dot(ds, ktt_ref[...])
        dkt_ref[...] += _dot_tn(gc["qb"], ds)

        dmixed = dg * gb["u"]
        dhu = dg * gb["mixed"] * gb["du"]
        dvns = []
        for j, c0 in enumerate(range(0, ts, CHUNK)):
            dm = dmixed[c0:c0 + CHUNK, :]
            dmb = dm.astype(BF16)
            dmsum_ref[...] += dm
            dwcat_ref[...] += _dot_nt(dmb, gb["stacks"][j])
            dst = _dot(wcatt_ref[...], dmb)
            dvn_c = jnp.zeros((CHUNK, GMLP_W), F32)
            for h in range(GMLP_HEADS):
                dvn_c = dvn_c + jnp.where(_head_mask(GMLP_W, h), dst[h * CHUNK:(h + 1) * CHUNK, :], 0.0)
            dvns.append(dvn_c)
        dvn = jnp.concatenate(dvns, axis=0) if len(dvns) > 1 else dvns[0]
        dva_ref[3:4, :] += _rowsum(dvn * gb["vhat"])
        dva_ref[4:5, :] += _rowsum(dvn)
        dhv = _ln_bwd(dvn, gb["vhat"], gb["rv"], va_ref[3:4, :]) * gb["dv"]

        a2, sa = ga["a2"], ga["sa"]
        da2 = da * (sa * (1.0 + a2 * (1.0 - sa)))
        dva_ref[1:2, :] += _rowsum(da2 * ga["a2h"])
        dva_ref[2:3, :] += _rowsum(da2)
        da1 = _ln_bwd(da2, ga["a2h"], ga["ra"], va_ref[1:2, :])
        dva_ref[0:1, :] += _rowsum(da1)
        dbuf[0:ts, :] = da1
        _shift_copies(dbuf, dsh, ts + HALO - 8)
        _conv31_dw(buf, sh, dbuf, dcw_ref, ts)
        _conv31_dx(dbuf, dsh, cw_ref, da0buf, ts)
        da0 = da0buf[...]
        sg = ga["sg"]
        dha = da0 * sg
        dhg = da0 * ga["ha"] * sg * (1.0 - sg)

        dh = jnp.concatenate([dha, dhg, dhu, dhv, dq], axis=1).astype(BF16)
        dx_ref[...] = _dot(dh, wint_ref[...]) + ALPHA * dz1
        dwin_ref[...] += _dot_tn(dh, x_ref[...].astype(BF16))

        @pl.when(i == n - 1)
        def _():
            _store_blocks(dwout_ref, slabs_ref, sems, off_out, D_MODEL // N_DEV, 0, N_DEV)
            _store_blocks(dwin_ref, slabs_ref, sems, off_in, IN_W // N_DEV, 0, N_DEV)

    rev = lambda width: pl.BlockSpec((ts, width), lambda i: (n - 1 - i, 0))
    prev = pl.BlockSpec((HALO, 2 * CONV_W), lambda i: (jnp.maximum((n - 1 - i) * halo_blocks - 1, 0), 0))
    hbm = pl.BlockSpec(memory_space=pl.ANY)
    hc = GMLP_HEADS * CHUNK
    am = XATTN_HEADS * N_MEM
    return pl.pallas_call(
        body, name="mixer_bwd", grid=(n,),
        in_specs=[hbm, rev(D_MODEL), rev(D_MODEL), rev(IN_W), prev, rev(D_MODEL), _const((HALO, CONV_W)),
                  _const((8, CONV_W)), _const((CHUNK, hc)), _const((hc, CHUNK)), _const((CHUNK, GMLP_W)),
                  _const((XATTN_W, am)), _const((am, XATTN_W)), _const((am, XATTN_W)), _const((XATTN_W, am)),
                  _const((D_MODEL, D_MODEL)), _const((IN_W, D_MODEL)), _const((8, D_MODEL))],
        out_specs=[hbm, rev(D_MODEL), _acc((XATTN_W, am)), _acc((am, XATTN_W)),
                   _acc((CHUNK, hc)), _acc((CHUNK, GMLP_W)), _acc((8, CONV_W)), _acc((HALO, CONV_W)),
                   _acc((8, D_MODEL))],
        out_shape=[jax.ShapeDtypeStruct(slabs.shape, F32), jax.ShapeDtypeStruct((seq, D_MODEL), F32),
                   jax.ShapeDtypeStruct((XATTN_W, am), F32),
                   jax.ShapeDtypeStruct((am, XATTN_W), F32), jax.ShapeDtypeStruct((CHUNK, hc), F32),
                   jax.ShapeDtypeStruct((CHUNK, GMLP_W), F32), jax.ShapeDtypeStruct((8, CONV_W), F32),
                   jax.ShapeDtypeStruct((HALO, CONV_W), F32), jax.ShapeDtypeStruct((8, D_MODEL), F32)],
        scratch_shapes=[pltpu.VMEM((ts + HALO, CONV_W), F32), pltpu.VMEM((ts, CONV_W), F32),
                        pltpu.VMEM((ts + HALO, CONV_W), F32), pltpu.VMEM((ts, CONV_W), F32),
                        pltpu.VMEM((D_MODEL, D_MODEL), F32), pltpu.VMEM((IN_W, D_MODEL), F32),
                        pltpu.SemaphoreType.DMA((N_DEV,)),
                        pltpu.VMEM((7, ts + HALO, CONV_W), F32), pltpu.VMEM((7, ts + HALO, CONV_W), F32)],
        input_output_aliases={0: 0},
        compiler_params=_params(dimension_semantics=("arbitrary",)),
    )(slabs, dx1, z1, hb, hb, x, w["cw"], w["va"], w["wcat"], w["wcatt"], w["bfull"], w["kt"], w["ktt"], w["vm"],
      w["vmt"], w["woutt"], w["wint"], w["v1"])


FFN_HALO = 8
FF_GROUP = D_FF // N_DEV
FF_GROUP_PAD = D_FF_PAD // N_DEV


def _ffn_gate(ubuf, cf_ref, ts, lo, hi):
    taps = tuple(ubuf[FFN_HALO - (FFN_CONV_K - 1) + k:FFN_HALO - (FFN_CONV_K - 1) + k + ts, lo:hi]
                 for k in range(FFN_CONV_K))
    g = cf_ref[3:4, lo:hi] + cf_ref[2:3, lo:hi] * taps[2]
    g = g + cf_ref[1:2, lo:hi] * taps[1]
    return g + cf_ref[0:1, lo:hi] * taps[0], taps


FFN_LANE_CHUNKS = ((0, D_FF_PAD // 2), (D_FF_PAD // 2, D_FF_PAD))


def _ffn_fwd_call(x1, w, ts, exch=None):
    seq = x1.shape[0]
    n = seq // ts

    def body(x1_ref, wg_ref, wv_ref, cf_ref, wdown_ref, v2_ref, ug_ref, uv_ref, z2_ref, x2_ref, ubuf):
        i = pl.program_id(0)

        @pl.when(i == 0)
        def _():
            ubuf[0:FFN_HALO, :] = jnp.zeros((FFN_HALO, D_FF_PAD), F32)

        @pl.when(i > 0)
        def _():
            ubuf[0:FFN_HALO, :] = ubuf[ts:ts + FFN_HALO, :]

        xv = x1_ref[...]
        xb = xv.astype(BF16)
        y = ALPHA * xv
        for lo, hi in FFN_LANE_CHUNKS:
            ug = _dot(xb, wg_ref[:, lo:hi]).astype(BF16)
            uv = _dot(xb, wv_ref[:, lo:hi]).astype(BF16)
            ug_ref[:, lo:hi] = ug
            uv_ref[:, lo:hi] = uv
            ubuf[FFN_HALO:FFN_HALO + ts, lo:hi] = ug.astype(F32)
            gate, _ = _ffn_gate(ubuf, cf_ref, ts, lo, hi)
            act = (gate * _sigmoid(gate) * uv.astype(F32)).astype(BF16)
            y = y + _dot(act, wdown_ref[lo:hi, :])
        z2_ref[...] = y
        xh, _ = _ln_stats(y)
        x2_ref[...] = xh * v2_ref[0:1, :] + v2_ref[1:2, :]

    row = lambda width: pl.BlockSpec((ts, width), lambda i: (i, 0))
    return _grid_call(
        body, "ffn_fwd", n,
        in_specs=[row(D_MODEL), _const((D_MODEL, D_FF_PAD)), _const((D_MODEL, D_FF_PAD)), _const((8, D_FF_PAD)),
                  _const((D_FF_PAD, D_MODEL)), _const((8, D_MODEL))],
        out_specs=[row(D_FF_PAD), row(D_FF_PAD), row(D_MODEL), row(D_MODEL)],
        out_shape=[jax.ShapeDtypeStruct((seq, D_FF_PAD), BF16), jax.ShapeDtypeStruct((seq, D_FF_PAD), BF16),
                   jax.ShapeDtypeStruct((seq, D_MODEL), F32), jax.ShapeDtypeStruct((seq, D_MODEL), F32)],
        scratch_shapes=[pltpu.VMEM((ts + FFN_HALO, D_FF_PAD), F32)],
        args=(x1, w["wg"], w["wv"], w["cf"], w["wdown"], w["v2"]), exch=exch)


def _ffn_bwd_call(dx2_or_target, z2, ug, uv, w, ts, last, slabs, row_off, exch=None):
    seq = z2.shape[0]
    n = seq // ts
    halo_blocks = ts // 16

    def body(slabs_in, dx2_ref, z2_ref, ug_ref, uv_ref, uprev_ref, cf_ref, wdownt_ref, v2_ref,
             slabs_ref, dug_ref, duv_ref, dz2_ref, dcf_ref, dv2_ref, loss_ref,
             ubuf, dgbuf, dwacc, sems):
        i = pl.program_id(0)

        @pl.when(i == 0)
        def _():
            dwacc[...] = jnp.zeros(dwacc.shape, F32)
            dcf_ref[...] = jnp.zeros(dcf_ref.shape, F32)
            dv2_ref[...] = jnp.zeros(dv2_ref.shape, F32)
            loss_ref[...] = jnp.zeros(loss_ref.shape, F32)
            dgbuf[ts:ts + FFN_HALO, :] = jnp.zeros((FFN_HALO, D_FF_PAD), F32)

        @pl.when(i > 0)
        def _():
            dgbuf[ts:ts + FFN_HALO, :] = dgbuf[0:FFN_HALO, :]

        xh2, r2 = _ln_stats(z2_ref[...])
        if last:
            diff = xh2 * v2_ref[0:1, :] + v2_ref[1:2, :] - dx2_ref[...]
            loss_ref[...] += jnp.sum(diff * diff) * (0.5 / D_MODEL)
            dx2v = diff * (1.0 / D_MODEL)
        else:
            dx2v = dx2_ref[...]
        dv2_ref[0:1, :] += _rowsum(dx2v * xh2)
        dv2_ref[1:2, :] += _rowsum(dx2v)
        dz2 = _ln_bwd(dx2v, xh2, r2, v2_ref[0:1, :])
        dz2_ref[...] = dz2
        dy = dz2.astype(BF16)

        up = uprev_ref[...].astype(F32)[8:16, :]
        ubuf[0:FFN_HALO, :] = jnp.where(i == n - 1, 0.0, up)
        ubuf[FFN_HALO:FFN_HALO + ts, :] = ug_ref[...].astype(F32)
        for lo, hi in FFN_LANE_CHUNKS:
            gate, taps = _ffn_gate(ubuf, cf_ref, ts, lo, hi)
            sg = _sigmoid(gate)
            sl = gate * sg
            uvf = uv_ref[:, lo:hi].astype(F32)
            act = (sl * uvf).astype(BF16)
            dwacc[lo:hi, :] += _dot_tn(act, dy)
            dact = _dot(dy, wdownt_ref[:, lo:hi])
            duv_ref[:, lo:hi] = (dact * sl).astype(BF16)
            dgate = dact * uvf * (sg * (1.0 + gate * (1.0 - sg)))
            dgbuf[0:ts, lo:hi] = dgate
            dcf_ref[3:4, lo:hi] += _rowsum(dgate)
            for k in range(FFN_CONV_K):
                dcf_ref[k:k + 1, lo:hi] += _rowsum(dgate * taps[k])
            dug = cf_ref[2:3, lo:hi] * dgate + cf_ref[1:2, lo:hi] * dgbuf[1:1 + ts, lo:hi]
            dug = dug + cf_ref[0:1, lo:hi] * dgbuf[2:2 + ts, lo:hi]
            dug_ref[:, lo:hi] = dug.astype(BF16)

        @pl.when(i == n - 1)
        def _():
            _store_blocks(dwacc, slabs_ref, sems, row_off, D_FF_PAD // N_DEV, 0, N_DEV)

    rev = lambda width: pl.BlockSpec((ts, width), lambda i: (n - 1 - i, 0))
    prev = pl.BlockSpec((16, D_FF_PAD), lambda i: (jnp.maximum((n - 1 - i) * halo_blocks - 1, 0), 0))
    hbm = pl.BlockSpec(memory_space=pl.ANY)
    return _grid_call(
        body, "ffn_bwd_last" if last else "ffn_bwd", n,
        in_specs=[hbm, rev(D_MODEL), rev(D_MODEL), rev(D_FF_PAD), rev(D_FF_PAD), prev, _const((8, D_FF_PAD)),
                  _const((D_MODEL, D_FF_PAD)), _const((8, D_MODEL))],
        out_specs=[hbm, rev(D_FF_PAD), rev(D_FF_PAD), rev(D_MODEL),
                   _acc((8, D_FF_PAD)), _acc((8, D_MODEL)), _acc((8, 128))],
        out_shape=[jax.ShapeDtypeStruct(slabs.shape, F32),
                   jax.ShapeDtypeStruct((seq, D_FF_PAD), BF16), jax.ShapeDtypeStruct((seq, D_FF_PAD), BF16),
                   jax.ShapeDtypeStruct((seq, D_MODEL), F32),
                   jax.ShapeDtypeStruct((8, D_FF_PAD), F32), jax.ShapeDtypeStruct((8, D_MODEL), F32),
                   jax.ShapeDtypeStruct((8, 128), F32)],
        scratch_shapes=[pltpu.VMEM((ts + FFN_HALO, D_FF_PAD), F32), pltpu.VMEM((ts + FFN_HALO, D_FF_PAD), F32),
                        pltpu.VMEM((D_FF_PAD, D_MODEL), F32), pltpu.SemaphoreType.DMA((N_DEV,))],
        args=(slabs, dx2_or_target, z2, ug, uv, ug, w["cf"], w["wdownt"], w["v2"]), aliases={0: 0}, exch=exch)


def _proj_bwd_call(d, wt, xin, addend, scale, ts, name, slabs, row_off, first_block, n_blocks, exch=None):
    seq, k = d.shape
    n = seq // ts

    def body(slabs_in, d_ref, wt_ref, xin_ref, add_ref, slabs_ref, dx_ref, acc, sems):
        i = pl.program_id(0)

        @pl.when(i == 0)
        def _():
            acc[...] = jnp.zeros(acc.shape, F32)

        dv = d_ref[...]
        dx_ref[...] = _dot(dv, wt_ref[...]) + scale * add_ref[...]
        acc[...] += _dot_tn(dv, xin_ref[...].astype(BF16))

        @pl.when(i == n - 1)
        def _():
            _store_blocks(acc, slabs_ref, sems, row_off, k // n_blocks, first_block, n_blocks)

    row = lambda width: pl.BlockSpec((ts, width), lambda i: (i, 0))
    hbm = pl.BlockSpec(memory_space=pl.ANY)
    return _grid_call(
        body, name, n,
        in_specs=[hbm, row(k), _const((k, D_MODEL)), row(D_MODEL), row(D_MODEL)],
        out_specs=[hbm, row(D_MODEL)],
        out_shape=[jax.ShapeDtypeStruct(slabs.shape, F32), jax.ShapeDtypeStruct((seq, D_MODEL), F32)],
        scratch_shapes=[pltpu.VMEM((k, D_MODEL), F32), pltpu.SemaphoreType.DMA((n_blocks,))],
        args=(slabs, d, wt, xin, addend), aliases={0: 0}, exch=exch)


MEM_FOLD = BLOB_LANES // XATTN_W


def _mem_proj_call(memq, wk_flat, wv_flat):
    def body(memq_ref, wk_ref, wv_ref, kh_ref, vh_ref):
        for w_ref, o_ref in ((wk_ref, kh_ref), (wv_ref, vh_ref)):
            acc = jnp.zeros((N_MEM, XATTN_W), F32)
            for q in range(MEM_FOLD):
                acc = acc + _dot(memq_ref[q], w_ref[:, q * XATTN_W:(q + 1) * XATTN_W])
            o_ref[...] = acc

    out = jax.ShapeDtypeStruct((N_MEM, XATTN_W), F32)
    return pl.pallas_call(body, name="mem_proj", out_shape=[out, out], compiler_params=_params())(memq, wk_flat, wv_flat)


def _mem_proj_bwd_call(memq, dkh, dvh, slabs, off_k, off_v):
    rows = D_MODEL // MEM_FOLD

    def body(slabs_in, memq_ref, dkh_ref, dvh_ref, slabs_ref, acc, sems):
        for d_ref, off in ((dkh_ref, off_k), (dvh_ref, off_v)):
            dv = d_ref[...].astype(BF16)
            for q in range(MEM_FOLD):
                acc[:, q * XATTN_W:(q + 1) * XATTN_W] = _dot_tn(memq_ref[q], dv)
            _store_blocks(acc, slabs_ref, sems, off, rows // N_DEV, 0, N_DEV)

    hbm = pl.BlockSpec(memory_space=pl.ANY)
    vmem = pl.BlockSpec(memory_space=pltpu.VMEM)
    return pl.pallas_call(
        body, name="mem_proj_bwd", in_specs=[hbm, vmem, vmem, vmem], out_specs=hbm,
        out_shape=jax.ShapeDtypeStruct(slabs.shape, F32),
        scratch_shapes=[pltpu.VMEM((rows, BLOB_LANES), F32), pltpu.SemaphoreType.DMA((N_DEV,))],
        input_output_aliases={0: 0}, compiler_params=_params(),
    )(slabs, memq, dkh, dvh)


def _place():
    return lax.axis_index("x"), lax.axis_index("y"), lax.axis_index("c")


def _all_gather_call(arrs, pieces, name):
    n_in, n_p = len(arrs), len(pieces)

    def body(*refs):
        ins, outs = refs[:n_in], refs[n_in:n_in + n_p]
        send_sems, recv_sems, local_sems = refs[n_in + n_p:]
        x, y, c = _place()
        me, sibling = (x, y, c), (x, y, 1 - c)
        chips = [(1 - x, y), (x, 1 - y), (1 - x, 1 - y)]

        def src(a):
            idx, r0, rows = pieces[a]
            return ins[idx] if r0 is None else ins[idx].at[pl.ds(r0, rows)]

        def slab(a, p):
            return outs[a].at[4 * p[0] + 2 * p[1] + p[2]]

        def copy(a, k, block, to, own=False):
            return pltpu.make_async_remote_copy(
                src_ref=src(a) if own else slab(a, block), dst_ref=slab(a, block),
                send_sem=send_sems.at[a, k], recv_sem=recv_sems.at[a, k], device_id=to, device_id_type=MESH)

        mine = [pltpu.make_async_copy(src(a), slab(a, me), local_sems.at[a]) for a in range(n_p)]
        for cp in mine:
            cp.start()
        first = []
        for a in range(n_p):
            first.append(copy(a, 0, me, sibling, own=True))
            first += [copy(a, 1 + j, me, (*chip, c), own=True) for j, chip in enumerate(chips)]
        for cp in first:
            cp.start()
        passed = []
        for a in range(n_p):
            for j, chip in enumerate(chips):
                copy(a, 1 + j, (*chip, c), me).wait_recv()
                cp = copy(a, 4 + j, (*chip, c), sibling)
                cp.start()
                passed.append(cp)
        for a in range(n_p):
            copy(a, 0, sibling, me).wait_recv()
            for j, chip in enumerate(chips):
                copy(a, 4 + j, (*chip, 1 - c), me).wait_recv()
        for cp in first + passed:
            cp.wait_send()
        for cp in mine:
            cp.wait()

    def out_shape(piece):
        idx, r0, rows = piece
        a = arrs[idx]
        return jax.ShapeDtypeStruct((N_DEV,) + (a.shape if r0 is None else (rows,) + a.shape[1:]), a.dtype)

    hbm = pl.BlockSpec(memory_space=pl.ANY)
    return pl.pallas_call(
        body, name=name,
        in_specs=[hbm] * n_in, out_specs=[hbm] * n_p, out_shape=[out_shape(p) for p in pieces],
        scratch_shapes=[pltpu.SemaphoreType.DMA((n_p, 7)), pltpu.SemaphoreType.DMA((n_p, 7)),
                        pltpu.SemaphoreType.DMA((n_p,))],
    )(*arrs)


def _flip(v, f):
    return 1 - v if f else v


def _gather_exchange(blob, r0, rows):
    def build(ins, outs, send_sems, recv_sems, local_sems):
        x, y, c = _place()
        src = ins[0].at[pl.ds(r0, rows)]
        dst = outs[0].at[4 * x + 2 * y + c]
        flips = [(fx, fy, fc) for fx in (0, 1) for fy in (0, 1) for fc in (0, 1) if fx or fy or fc]
        remote = [pltpu.make_async_remote_copy(
            src_ref=src, dst_ref=dst, send_sem=send_sems.at[k], recv_sem=recv_sems.at[k],
            device_id=(_flip(x, fx), _flip(y, fy), _flip(c, fc)), device_id_type=MESH)
            for k, (fx, fy, fc) in enumerate(flips)]
        return remote, [pltpu.make_async_copy(src, dst, local_sems.at[0])]

    return _Exchange([blob], [jax.ShapeDtypeStruct((N_DEV, rows) + blob.shape[1:], blob.dtype)], N_DEV - 1, build)


def _swap_core_copies(g_ref, r_ref, send_sems, recv_sems):
    x, y, c = _place()
    return [pltpu.make_async_remote_copy(
        src_ref=g_ref.at[2 * k + (1 - c)], dst_ref=r_ref.at[k], send_sem=send_sems.at[k], recv_sem=recv_sems.at[k],
        device_id=(x, y, 1 - c), device_id_type=MESH) for k in range(4)]


def _swap_chip_copies(p_ref, r_ref, send_sems, recv_sems):
    x, y, c = _place()
    chips = [(1 - x, y), (x, 1 - y), (1 - x, 1 - y)]
    return [pltpu.make_async_remote_copy(
        src_ref=p_ref.at[2 * px + py], dst_ref=r_ref.at[j], send_sem=send_sems.at[j], recv_sem=recv_sems.at[j],
        device_id=(px, py, c), device_id_type=MESH) for j, (px, py) in enumerate(chips)]


def _swap_exchange(a, copies, n):
    return _Exchange([a], [jax.ShapeDtypeStruct((n,) + a.shape[1:], a.dtype)], n,
                     lambda ins, outs, send_sems, recv_sems, local_sems: (copies(ins[0], outs[0], send_sems, recv_sems), []))


def _swap_call(a, copies, n, name):
    def body(a_ref, r_ref, send_sems, recv_sems):
        cps = copies(a_ref, r_ref, send_sems, recv_sems)
        for cp in cps:
            cp.start()
        for cp in cps:
            cp.wait()

    hbm = pl.BlockSpec(memory_space=pl.ANY)
    return pl.pallas_call(
        body, name=name, in_specs=[hbm], out_specs=hbm, out_shape=jax.ShapeDtypeStruct((n,) + a.shape[1:], a.dtype),
        scratch_shapes=[pltpu.SemaphoreType.DMA((n,)), pltpu.SemaphoreType.DMA((n,))],
    )(a)


ADD_ROWS = 64


def _pair_add_call(g, r, c):
    _, rows, width = g.shape

    def body(c_ref, g_ref, r_ref, o_ref):
        o_ref[...] = (g_ref[...] + r_ref[...]).astype(BF16)

    return pl.pallas_call(
        body, name="rs_pair_add",
        grid_spec=pltpu.PrefetchScalarGridSpec(
            num_scalar_prefetch=1, grid=(4, rows // ADD_ROWS),
            in_specs=[pl.BlockSpec((None, ADD_ROWS, width), lambda k, i, c_ref: (2 * k + c_ref[0], i, 0)),
                      pl.BlockSpec((None, ADD_ROWS, width), lambda k, i, c_ref: (k, i, 0))],
            out_specs=pl.BlockSpec((None, ADD_ROWS, width), lambda k, i, c_ref: (k, i, 0))),
        out_shape=jax.ShapeDtypeStruct((4, rows, width), BF16),
        compiler_params=_params(dimension_semantics=("arbitrary", "arbitrary")),
    )(c, g, r)


def _adam(w, g, m, v):
    mn = ADAM_B1 * m + (1.0 - ADAM_B1) * g
    vn = ADAM_B2 * v + (1.0 - ADAM_B2) * (g * g)
    m_hat = mn / (1.0 - ADAM_B1 ** ADAM_STEP)
    v_hat = vn / (1.0 - ADAM_B2 ** ADAM_STEP)
    return -ADAM_LR * (m_hat / (jnp.sqrt(v_hat) + ADAM_EPS) + ADAM_WD * w), mn, vn


def _chip_add_adamw_call(slabs, from_sibling, from_chips, me, chip, w, m, v):
    _, rows, width = slabs.shape

    def body(me_ref, chip_ref, own_ref, sib_ref, r_ref, w_ref, m_ref, v_ref, g_ref, d_ref, mo_ref, vo_ref):
        g = own_ref[...] + sib_ref[...]
        for j in range(3):
            g = g + r_ref[j].astype(F32)
        g_ref[...] = g
        d_ref[...], mo_ref[...], vo_ref[...] = _adam(w_ref[...], g, m_ref[...], v_ref[...])

    spec = pl.BlockSpec((ADD_ROWS, width), lambda i, me_ref, chip_ref: (i, 0))
    return pl.pallas_call(
        body, name="rs_chip_add_adamw",
        grid_spec=pltpu.PrefetchScalarGridSpec(
            num_scalar_prefetch=2, grid=(rows // ADD_ROWS,),
            in_specs=[pl.BlockSpec((None, ADD_ROWS, width), lambda i, me_ref, chip_ref: (me_ref[0], i, 0)),
                      pl.BlockSpec((None, ADD_ROWS, width), lambda i, me_ref, chip_ref: (chip_ref[0], i, 0)),
                      pl.BlockSpec((3, ADD_ROWS, width), lambda i, me_ref, chip_ref: (0, i, 0)), spec, spec, spec],
            out_specs=[spec] * 4),
        out_shape=[jax.ShapeDtypeStruct((rows, width), F32)] * 4,
        compiler_params=_params(dimension_semantics=("arbitrary",)),
    )(me, chip, slabs, from_sibling, from_chips, w, m, v)


def _adamw_whole_call(params, name):
    n = len(params)

    def body(*refs):
        ins, outs = refs[:4 * n], refs[4 * n:]
        for a in range(n):
            w_ref, g_ref, m_ref, v_ref = ins[4 * a:4 * a + 4]
            outs[3 * a][...], outs[3 * a + 1][...], outs[3 * a + 2][...] = _adam(w_ref[...], g_ref[...], m_ref[...],
                                                                              v_ref[...])

    flat = [a for p in params for a in p]
    shapes = [jax.ShapeDtypeStruct(p[0].shape, F32) for p in params for _ in range(3)]
    out = pl.pallas_call(body, name=name, out_shape=shapes, compiler_params=_params())(*flat)
    return [tuple(out[3 * a:3 * a + 3]) for a in range(n)]


GATHERED_ACCS = (("dva", (8, CONV_W)), ("dv1", (8, D_MODEL)), ("dv2", (8, D_MODEL)), ("dcf", (8, D_FF_PAD)),
                 ("dcw", (HALO, CONV_W)), ("dwcat", (CHUNK, GMLP_HEADS * CHUNK)), ("dmsum", (CHUNK, GMLP_W)))
VEC_A = ("conv_a_b", "ln_a_g", "ln_a_b", "ln_v_g", "ln_v_b")
REP_IN_KERNEL = VEC_A + ("ln1_g", "ln1_b", "ln2_g", "ln2_b", "w_s", "b_s")


def _replicated_update_call(gathered, p, mom_m, mom_v):
    n_acc = len(GATHERED_ACCS)
    n_rep = len(REP_IN_KERNEL)

    def body(*refs):
        acc_refs = refs[:DEPTH * n_acc]
        wmv = refs[DEPTH * n_acc:DEPTH * n_acc + 3 * n_rep]
        outs = refs[DEPTH * n_acc + 3 * n_rep:]
        out_par = {nm: outs[4 * a:4 * a + 4] for a, nm in enumerate(REP_IN_KERNEL)}
        out_dcf = outs[4 * n_rep:4 * n_rep + DEPTH]
        out_dcw = outs[4 * n_rep + DEPTH:4 * n_rep + 2 * DEPTH]
        par = {nm: wmv[3 * a:3 * a + 3] for a, nm in enumerate(REP_IN_KERNEL)}
        tril = (lax.broadcasted_iota(jnp.int32, (CHUNK, CHUNK), 0) >= lax.broadcasted_iota(jnp.int32, (CHUNK, CHUNK), 1))
        head = lax.broadcasted_iota(jnp.int32, (8, GMLP_W), 0) * HEAD_DIM
        lane = lax.broadcasted_iota(jnp.int32, (8, GMLP_W), 1)
        sel = jnp.where((lane >= head) & (lane < head + HEAD_DIM), 1.0, 0.0)

        def update(nm, idx, g):
            w_ref, m_ref, v_ref = par[nm]
            d, mn, vn = _adam(w_ref[idx], g, m_ref[idx], v_ref[idx])
            g_ref, d_ref, mo_ref, vo_ref = out_par[nm]
            g_ref[idx] = g
            d_ref[idx] = d
            mo_ref[idx] = mn
            vo_ref[idx] = vn

        for l in range(DEPTH):
            tot = {}
            for a, (nm, _) in enumerate(GATHERED_ACCS):
                ref = acc_refs[l * n_acc + a]
                s = ref[0]
                for j in range(1, N_DEV):
                    s = s + ref[j]
                tot[nm] = s
            out_dcf[l][...] = tot["dcf"]
            out_dcw[l][...] = tot["dcw"]
            row = (slice(l, l + 1), slice(None))
            for k, nm in enumerate(VEC_A):
                update(nm, row, tot["dva"][k:k + 1, :])
            update("ln1_g", row, tot["dv1"][0:1, :])
            update("ln1_b", row, tot["dv1"][1:2, :])
            update("ln2_g", row, tot["dv2"][0:1, :])
            update("ln2_b", row, tot["dv2"][1:2, :])
            for h in range(GMLP_HEADS):
                gw = jnp.where(tril, tot["dwcat"][:, h * CHUNK:(h + 1) * CHUNK], 0.0)
                update("w_s", (l, h), gw)
            gb = lax.dot_general(sel, tot["dmsum"], (((1,), (1,)), ((), ())), precision=lax.Precision.HIGHEST,
                                 preferred_element_type=F32)
            for h in range(GMLP_HEADS):
                update("b_s", (l, slice(h, h + 1), slice(None)), gb[h:h + 1, :])

    ins = [gathered[l][nm] for l in range(DEPTH) for nm, _ in GATHERED_ACCS]
    ins += [t[nm] for nm in REP_IN_KERNEL for t in (p, mom_m, mom_v)]
    shapes = [jax.ShapeDtypeStruct(p[nm].shape, F32) for nm in REP_IN_KERNEL for _ in range(4)]
    shapes += [jax.ShapeDtypeStruct((8, D_FF_PAD), F32)] * DEPTH + [jax.ShapeDtypeStruct((HALO, CONV_W), F32)] * DEPTH
    out = pl.pallas_call(body, name="replicated_update", out_shape=shapes, compiler_params=_params())(*ins)
    res = [{nm: out[4 * a + k] for a, nm in enumerate(REP_IN_KERNEL)} for k in range(4)]
    return res, out[4 * n_rep:4 * n_rep + DEPTH], out[4 * n_rep + DEPTH:]


BLOCK_ROWS = (("w_in", IN_W // N_DEV), ("w_out", D_MODEL // N_DEV), ("w_up", 2 * FF_GROUP_PAD),
              ("w_down", FF_GROUP_PAD), ("w_mk", D_MODEL // N_DEV // MEM_FOLD), ("w_mv", D_MODEL // N_DEV // MEM_FOLD))
LAYER_ROWS = sum(r for _, r in BLOCK_ROWS)
assert LAYER_ROWS % ADD_ROWS == 0 and all(r % 16 == 0 for _, r in BLOCK_ROWS)


def _row_off(name):
    off = 0
    for nm, r in BLOCK_ROWS:
        if nm == name:
            return off
        off += r
    raise KeyError(name)


def _to_rows(name, a):
    if name == "w_in":
        return a.T
    if name == "w_up":
        t = a.T.reshape(2, FF_GROUP, D_MODEL)
        return jnp.pad(t, ((0, 0), (0, FF_GROUP_PAD - FF_GROUP), (0, 0))).reshape(2 * FF_GROUP_PAD, D_MODEL)
    if name == "w_down":
        return jnp.pad(a, ((0, FF_GROUP_PAD - FF_GROUP), (0, 0)))
    if name == "w_out":
        return a
    return a.reshape(-1, BLOB_LANES)


def _from_rows(name, r):
    if name == "w_in":
        return r.T
    if name == "w_up":
        return r.reshape(2, FF_GROUP_PAD, D_MODEL)[:, :FF_GROUP].reshape(2 * FF_GROUP, D_MODEL).T
    if name == "w_down":
        return r[:FF_GROUP]
    if name == "w_out":
        return r
    return r.reshape(D_MODEL // N_DEV, XATTN_W)


def _blob(tree, l):
    return jnp.concatenate([_to_rows(nm, tree[nm][l]) for nm, _ in BLOCK_ROWS], axis=0)


def _unblob(blobs):
    return {nm: jnp.stack([_from_rows(nm, b[_row_off(nm):_row_off(nm) + r]) for b in blobs]) for nm, r in BLOCK_ROWS}


def _ff_interleave(a):
    lead = a.shape[:-1]
    t = a.reshape(lead + (N_DEV, FF_GROUP))
    return jnp.pad(t, [(0, 0)] * len(lead) + [(0, 0), (0, FF_GROUP_PAD - FF_GROUP)]).reshape(lead + (D_FF_PAD,))


def _ff_deinterleave(a):
    lead = a.shape[:-1]
    return a.reshape(lead + (N_DEV, FF_GROUP_PAD))[..., :FF_GROUP].reshape(lead + (D_FF,))


def _head_table():
    hd = jnp.arange(XATTN_W) // HEAD_DIM
    return (hd[None, :] == jnp.arange(XATTN_HEADS)[:, None]).astype(F32)


def _mixer_operands(mat, conv_a_w, p, l, memq):
    w = {}
    w["wint"] = mat["w_in"]
    w["win"] = mat["w_in"].T
    w["wout"] = mat["w_out"]
    w["woutt"] = mat["w_out"].T
    w["cw"] = conv_a_w
    zeros = jnp.zeros((3, CONV_W), F32)
    w["va"] = jnp.concatenate([p[nm][l][None] for nm in VEC_A] + [zeros], axis=0)
    tril = jnp.tril(jnp.ones((CHUNK, CHUNK), F32))
    w["wcat"] = (p["w_s"][l] * tril[None]).transpose(1, 0, 2).reshape(CHUNK, GMLP_HEADS * CHUNK).astype(BF16)
    w["wcatt"] = w["wcat"].T
    w["bfull"] = jnp.repeat(p["b_s"][l].T, HEAD_DIM, axis=1)
    kh, vh = _mem_proj_call(memq, mat["w_mk"], mat["w_mv"])
    hm = _head_table()
    scale = 1.0 / math.sqrt(HEAD_DIM)
    w["kt"] = (kh.T[:, None, :] * hm.T[:, :, None] * scale).reshape(XATTN_W, XATTN_HEADS * N_MEM).astype(BF16)
    w["ktt"] = w["kt"].T
    w["vm"] = (hm[:, None, :] * vh[None]).reshape(XATTN_HEADS * N_MEM, XATTN_W).astype(BF16)
    w["vmt"] = w["vm"].T
    zeros = jnp.zeros((6, D_MODEL), F32)
    w["v1"] = jnp.concatenate([p["ln1_g"][l][None], p["ln1_b"][l][None], zeros], axis=0)
    return w


def _ffn_operands(w_up, w_down, conv_f_w, p, l):
    w = {}
    w["wgt"] = w_up[:D_FF_PAD]
    w["wvt"] = w_up[D_FF_PAD:]
    w["wg"] = w["wgt"].T
    w["wv"] = w["wvt"].T
    w["wdown"] = w_down
    w["wdownt"] = w_down.T
    zeros = jnp.zeros((6, D_MODEL), F32)
    w["v2"] = jnp.concatenate([p["ln2_g"][l][None], p["ln2_b"][l][None], zeros], axis=0)
    w["cf"] = jnp.concatenate([conv_f_w, _ff_interleave(p["conv_f_b"][l][None]), jnp.zeros((4, D_FF_PAD), F32)], axis=0)
    return w


TS_MIXER = 256
TS_FFN = 256
TS_PROJ = 512
CONV_A_SHARD = CONV_W // N_DEV


def kernel(x, mem, w_in, conv_a_w, conv_a_b, ln_a_g, ln_a_b, ln_v_g, ln_v_b, w_s, b_s, w_mk, w_mv, w_out, ln1_g, ln1_b, w_up, conv_f_w, conv_f_b, w_down, ln2_g, ln2_b, loss_target, m_w_in, m_conv_a_w, m_conv_a_b, m_ln_a_g, m_ln_a_b, m_ln_v_g, m_ln_v_b, m_w_s, m_b_s, m_w_mk, m_w_mv, m_w_out, m_ln1_g, m_ln1_b, m_w_up, m_conv_f_w, m_conv_f_b, m_w_down, m_ln2_g, m_ln2_b, v_w_in, v_conv_a_w, v_conv_a_b, v_ln_a_g, v_ln_a_b, v_ln_v_g, v_ln_v_b, v_w_s, v_b_s, v_w_mk, v_w_mv, v_w_out, v_ln1_g, v_ln1_b, v_w_up, v_conv_f_w, v_conv_f_b, v_w_down, v_ln2_g, v_ln2_b):
    given = dict(locals())
    p = {nm: given[nm] for nm in WEIGHTS}
    mom_m = {nm: given["m_" + nm] for nm in WEIGHTS}
    mom_v = {nm: given["v_" + nm] for nm in WEIGHTS}
    seq = x.shape[1]
    ts_m, ts_f, ts_p = min(TS_MIXER, seq), min(TS_FFN, seq), min(TS_PROJ, seq)
    cx, cy, cc = _place()
    me = 4 * cx + 2 * cy + cc

    blobs = [_blob(p, l) for l in range(DEPTH)]
    blobs_bf = [b.astype(BF16) for b in blobs]
    conv_a_tile = jnp.pad(conv_a_w, ((0, 0), (0, HALO - CONV_K), (0, 128 - CONV_A_SHARD)))
    conv_f_tile = jnp.pad(conv_f_w, ((0, 0), (0, 8 - FFN_CONV_K), (0, 384 - FF_GROUP)))
    rows = dict(BLOCK_ROWS)
    mixer_names = ("w_in", "w_out", "w_mk", "w_mv")
    pieces = [(0, _row_off(nm), rows[nm]) for nm in mixer_names] + [(1, None, 0), (2, None, 0)]
    first = _all_gather_call([blobs_bf[0], conv_a_tile, conv_f_tile], pieces, "gather_weights")
    conv_a_all, conv_f_all = first[len(mixer_names)], first[len(mixer_names) + 1]
    conv_a = [conv_a_all[:, l, :, :CONV_A_SHARD].transpose(1, 0, 2).reshape(HALO, CONV_W) for l in range(DEPTH)]
    conv_f = [conv_f_all[:, l, :FFN_CONV_K, :FF_GROUP_PAD].transpose(1, 0, 2).reshape(FFN_CONV_K, D_FF_PAD)
              for l in range(DEPTH)]
    memq = mem[0].reshape(N_MEM, D_MODEL // MEM_FOLD, MEM_FOLD).transpose(2, 0, 1).astype(BF16)

    def matrix(gathered, nm, base):
        lo = _row_off(nm) - base
        return gathered[:, lo:lo + rows[nm]].reshape(-1, BLOB_LANES)

    ffn_base = _row_off("w_up")
    ops0 = _mixer_operands({nm: first[a].reshape(-1, BLOB_LANES) for a, nm in enumerate(mixer_names)}, conv_a[0], p, 0,
                           memq)
    (hb, z1, x1), (ffn0,) = _mixer_fwd_call(x[0], ops0, ts_m,
                                            _gather_exchange(blobs_bf[0], ffn_base, rows["w_up"] + rows["w_down"]))
    ops0.update(_ffn_operands(matrix(ffn0, "w_up", ffn_base), matrix(ffn0, "w_down", ffn_base), conv_f[0], p, 0))
    (ug, uv, z2, x2), (all1,) = _ffn_fwd_call(x1, ops0, ts_f, _gather_exchange(blobs_bf[1], 0, LAYER_ROWS))
    saved = [dict(x=x[0], hb=hb, z1=z1, x1=x1, ug=ug, uv=uv, z2=z2)]
    ops1 = _mixer_operands({nm: matrix(all1, nm, 0) for nm in mixer_names}, conv_a[1], p, 1, memq)
    ops1.update(_ffn_operands(matrix(all1, "w_up", 0), matrix(all1, "w_down", 0), conv_f[1], p, 1))
    (hb, z1, x1), _ = _mixer_fwd_call(x2, ops1, ts_m)
    (ug, uv, z2, _), _ = _ffn_fwd_call(x1, ops1, ts_f)
    saved.append(dict(x=x2, hb=hb, z1=z1, x1=x1, ug=ug, uv=uv, z2=z2))
    ops = [ops0, ops1]

    hm = _head_table()
    core_id = cc.reshape(1).astype(jnp.int32)
    slabs = [lax.empty((N_DEV, LAYER_ROWS, BLOB_LANES), F32) for _ in range(DEPTH)]
    accs = [None] * DEPTH
    from_sibling, from_chips = [None] * DEPTH, [None] * DEPTH
    dx = loss_target[0]
    loss = None
    for l in reversed(range(DEPTH)):
        s, w = saved[l], ops[l]
        last = l == DEPTH - 1
        ride = None if last else _swap_exchange(slabs[l + 1], _swap_core_copies, 4)
        (sl, dug, duv, dz2, dcf, dv2, loss_acc), got = _ffn_bwd_call(dx, s["z2"], s["ug"], s["uv"], w, ts_f, last,
                                                                     slabs[l], _row_off("w_down"), ride)
        if last:
            loss = loss_acc[0, 0]
        else:
            from_sibling[l + 1] = got[0]
            chip_sum = _pair_add_call(slabs[l + 1], from_sibling[l + 1], core_id)
        off_up = _row_off("w_up")
        (sl, dxa), _ = _proj_bwd_call(dug, w["wgt"], s["x1"], dz2, ALPHA, ts_p, "up_gate_bwd", sl, off_up, 0, 4)
        ride = None if last else _swap_exchange(chip_sum, _swap_chip_copies, 3)
        (sl, dx1), got = _proj_bwd_call(duv, w["wvt"], s["x1"], dxa, 1.0, ts_p, "up_val_bwd", sl, off_up, 4, 4, ride)
        if not last:
            from_chips[l + 1] = got[0]
        (sl, dx, dkt, dvm, dwcat, dmsum, dva, dcw, dv1) = _mixer_bwd_call(
            dx1, s["z1"], s["hb"], s["x"], w, ts_m, sl, _row_off("w_out"), _row_off("w_in"))
        dkh = jnp.einsum("hd,dhm->md", hm, dkt.reshape(XATTN_W, XATTN_HEADS, N_MEM)) * (1.0 / math.sqrt(HEAD_DIM))
        dvh = jnp.einsum("hd,hmd->md", hm, dvm.reshape(XATTN_HEADS, N_MEM, XATTN_W))
        slabs[l] = _mem_proj_bwd_call(memq, dkh, dvh, sl, _row_off("w_mk"), _row_off("w_mv"))
        accs[l] = dict(dva=dva, dv1=dv1, dv2=dv2, dcf=dcf, dcw=dcw, dwcat=dwcat, dmsum=dmsum)
    grad_x = dx[None]
    from_sibling[0] = _swap_call(slabs[0], _swap_core_copies, 4, "rs_swap_core")
    chip_sum = _pair_add_call(slabs[0], from_sibling[0], core_id)
    from_chips[0] = _swap_call(chip_sum, _swap_chip_copies, 3, "rs_swap_chip")
    me_id, chip_id = me.reshape(1).astype(jnp.int32), (2 * cx + cy).reshape(1).astype(jnp.int32)
    per_layer = [_chip_add_adamw_call(slabs[l], from_sibling[l], from_chips[l], me_id, chip_id, blobs[l],
                                      _blob(mom_m, l), _blob(mom_v, l)) for l in range(DEPTH)]
    outs = [_unblob([per_layer[l][k] for l in range(DEPTH)]) for k in range(4)]

    acc_list = [accs[l][nm] for l in range(DEPTH) for nm, _ in GATHERED_ACCS]
    acc_all = _all_gather_call(acc_list, [(a, None, 0) for a in range(len(acc_list))], "gather_small_grads")
    n_acc = len(GATHERED_ACCS)
    gathered_accs = [{nm: acc_all[l * n_acc + a] for a, (nm, _) in enumerate(GATHERED_ACCS)} for l in range(DEPTH)]
    rep, dcf_sum, dcw_sum = _replicated_update_call(gathered_accs, p, mom_m, mom_v)
    for k in range(4):
        outs[k].update(rep[k])
    dcf_sum, dcw_sum = jnp.stack(dcf_sum), jnp.stack(dcw_sum)
    zero = jnp.zeros((), jnp.int32)
    g_conv_a_w = lax.dynamic_slice(dcw_sum, (zero, zero, CONV_A_SHARD * me), (DEPTH, CONV_K, CONV_A_SHARD))
    g_conv_f_w = lax.dynamic_slice(dcf_sum, (zero, zero, FF_GROUP_PAD * me), (DEPTH, FFN_CONV_K, FF_GROUP))
    g_conv_f_b = _ff_deinterleave(dcf_sum[:, FFN_CONV_K])
    conv_grads = dict(conv_a_w=g_conv_a_w, conv_f_w=g_conv_f_w, conv_f_b=g_conv_f_b)
    conv_names = tuple(conv_grads)
    upd = _adamw_whole_call([(p[nm], conv_grads[nm], mom_m[nm], mom_v[nm]) for nm in conv_names], "adamw_conv")
    for nm, (d, mn, vn) in zip(conv_names, upd):
        outs[0][nm], outs[1][nm], outs[2][nm], outs[3][nm] = conv_grads[nm], d, mn, vn

    loss = lax.psum(loss, ("x", "y", "c"))
    return (loss, grad_x, *[outs[0][nm] for nm in WEIGHTS], *[outs[1][nm] for nm in WEIGHTS],
            *[outs[2][nm] for nm in WEIGHTS], *[outs[3][nm] for nm in WEIGHTS])
```

```python
import math

import jax
import jax.numpy as jnp
from jax import lax
from jax.experimental import pallas as pl
from jax.experimental.pallas import tpu as pltpu

F32 = jnp.float32
BF16 = jnp.bfloat16

DEPTH = 2
D_MODEL = 1024
CONV_W = 384
GMLP_W = 384
XATTN_W = 256
HEAD_DIM = 64
GMLP_HEADS = 6
XATTN_HEADS = 4
IN_W = 1792
CONV_K = 31
CHUNK = 128
N_MEM = 256
D_FF = 2752
D_FF_PAD = 2816
FFN_CONV_K = 3
ALPHA = (2.0 * DEPTH) ** 0.25
LN_EPS = 1e-5
N_DEV = 8

ADAM_LR = 0.001
ADAM_B1 = 0.9
ADAM_B2 = 0.999
ADAM_EPS = 1e-08
ADAM_WD = 0.01
ADAM_STEP = 10

HALO = 32
CONV_ROWS = 32
V7X_VMEM_BYTES = 64 * 1024 * 1024
VMEM_LIMIT = V7X_VMEM_BYTES - 8 * 1024 * 1024
BLOB_LANES = 1024

MESH = pl.DeviceIdType.MESH

WEIGHTS = ("w_in", "conv_a_w", "conv_a_b", "ln_a_g", "ln_a_b", "ln_v_g", "ln_v_b", "w_s", "b_s", "w_mk", "w_mv",
           "w_out", "ln1_g", "ln1_b", "w_up", "conv_f_w", "conv_f_b", "w_down", "ln2_g", "ln2_b")


def _params(**kw):
    return pltpu.CompilerParams(vmem_limit_bytes=VMEM_LIMIT, **kw)


def _const(shape):
    nd = len(shape)
    return pl.BlockSpec(shape, lambda i: (0,) * nd, pipeline_mode=pl.Buffered(1))


def _acc(shape):
    nd = len(shape)
    return pl.BlockSpec(shape, lambda i: (0,) * nd)


class _Exchange:
    def __init__(self, arrays, out_shapes, n_copies, build):
        self.arrays, self.out_shapes, self.n_copies, self.build = list(arrays), list(out_shapes), n_copies, build


def _carry(core, n_in, n_out, exch, n_steps):
    if exch is None:
        return core
    nx_in, nx_out = len(exch.arrays), len(exch.out_shapes)

    def body(*refs):
        o0 = n_in + nx_in
        s0 = o0 + n_out + nx_out
        x_in, x_out, sems = refs[n_in:o0], refs[o0 + n_out:s0], refs[-3:]
        i = pl.program_id(0)

        @pl.when(i == 0)
        def _():
            remote, local = exch.build(x_in, x_out, *sems)
            for cp in remote + local:
                cp.start()

        core(*refs[:n_in], *refs[o0:o0 + n_out], *refs[s0:-3])

        @pl.when(i == n_steps - 1)
        def _():
            remote, local = exch.build(x_in, x_out, *sems)
            for cp in remote + local:
                cp.wait()

    return body


def _grid_call(core, name, n_steps, in_specs, out_specs, out_shape, scratch_shapes, args, aliases=None, exch=None):
    hbm = pl.BlockSpec(memory_space=pl.ANY)
    n_in, n_out = len(in_specs), len(out_specs)
    in_specs, out_specs, out_shape, scratch_shapes, args = (list(in_specs), list(out_specs), list(out_shape),
                                                            list(scratch_shapes), list(args))
    if exch is not None:
        in_specs += [hbm] * len(exch.arrays)
        out_specs += [hbm] * len(exch.out_shapes)
        out_shape += exch.out_shapes
        scratch_shapes += [pltpu.SemaphoreType.DMA((exch.n_copies,)), pltpu.SemaphoreType.DMA((exch.n_copies,)),
                           pltpu.SemaphoreType.DMA((1,))]
        args += exch.arrays
    out = pl.pallas_call(
        _carry(core, n_in, n_out, exch, n_steps), name=name, grid=(n_steps,), in_specs=in_specs, out_specs=out_specs,
        out_shape=out_shape, scratch_shapes=scratch_shapes, input_output_aliases=aliases or {},
        compiler_params=_params(dimension_semantics=("arbitrary",)))(*args)
    return list(out[:n_out]), list(out[n_out:])


def _sigmoid(x):
    return 1.0 / (1.0 + jnp.exp(-x))


_GELU_C = math.sqrt(2.0 / math.pi)


def _gelu(x):
    x2 = x * x
    t = jnp.tanh(_GELU_C * (x + 0.044715 * x * x2))
    g = 0.5 * x * (1.0 + t)
    dg = 0.5 * (1.0 + t) + 0.5 * x * (1.0 - t * t) * (_GELU_C * (1.0 + 3.0 * 0.044715 * x2))
    return g, dg


def _ln_stats(z):
    mu = jnp.mean(z, axis=-1, keepdims=True)
    zc = z - mu
    var = jnp.mean(zc * zc, axis=-1, keepdims=True)
    r = lax.rsqrt(var + LN_EPS)
    return zc * r, r


def _ln_bwd(dy, xh, r, g):
    dxh = dy * g
    m1 = jnp.mean(dxh, axis=-1, keepdims=True)
    m2 = jnp.mean(dxh * xh, axis=-1, keepdims=True)
    return r * (dxh - m1 - xh * m2)


def _rowsum(x):
    return jnp.sum(x, axis=0, keepdims=True)


def _dot(a, b):
    return jnp.dot(a, b, preferred_element_type=F32)


def _dot_tn(a, b):
    return lax.dot_general(a, b, (((0,), (0,)), ((), ())), preferred_element_type=F32)


def _dot_nt(a, b):
    return lax.dot_general(a, b, (((1,), (1,)), ((), ())), preferred_element_type=F32)


def _shift_copies(buf, sh, rows):
    for b in range(1, 8):
        sh[b - 1, 0:rows, :] = buf[b:b + rows, :]


def _window(buf, sh, start):
    b = start % 8
    a = start - b
    return buf[a:a + CONV_ROWS, :] if b == 0 else sh[b - 1, a:a + CONV_ROWS, :]


def _conv31_fwd(buf, sh, w_ref, bias, out, ts):
    for r0 in range(0, ts, CONV_ROWS):
        acc = jnp.broadcast_to(bias, (CONV_ROWS, CONV_W))
        for k in range(CONV_K):
            acc = acc + w_ref[k:k + 1, :] * _window(buf, sh, r0 + HALO - (CONV_K - 1) + k)
        out[r0:r0 + CONV_ROWS, :] = acc


def _conv31_dx(dbuf, dsh, w_ref, out, ts):
    for r0 in range(0, ts, CONV_ROWS):
        acc = jnp.zeros((CONV_ROWS, CONV_W), F32)
        for k in range(CONV_K):
            acc = acc + w_ref[k:k + 1, :] * _window(dbuf, dsh, r0 + (CONV_K - 1) - k)
        out[r0:r0 + CONV_ROWS, :] = acc


def _conv31_dw(buf, sh, dbuf, dw_ref, ts):
    for k in range(CONV_K):
        part = jnp.zeros((8, CONV_W), F32)
        for r0 in range(0, ts, CONV_ROWS):
            m = dbuf[r0:r0 + CONV_ROWS, :] * _window(buf, sh, r0 + HALO - (CONV_K - 1) + k)
            for q in range(0, CONV_ROWS, 8):
                part = part + m[q:q + 8, :]
        dw_ref[k:k + 1, :] += _rowsum(part)


def _head_mask(width, h):
    lane = lax.broadcasted_iota(jnp.int32, (CHUNK, width), 1)
    return (lane >= h * HEAD_DIM) & (lane < (h + 1) * HEAD_DIM)


def _stack_heads(vn_c):
    return jnp.concatenate([jnp.where(_head_mask(GMLP_W, h), vn_c, 0.0) for h in range(GMLP_HEADS)], axis=0)


def _group_a_fwd(hf, buf, sh, a1_ref, cw_ref, va_ref, ts, conv=True):
    ha = hf[:, 0:CONV_W]
    sg = _sigmoid(hf[:, CONV_W:2 * CONV_W])
    buf[HALO:HALO + ts, :] = ha * sg
    _shift_copies(buf, sh, ts + HALO - 8)
    if conv:
        _conv31_fwd(buf, sh, cw_ref, va_ref[0:1, :], a1_ref, ts)
    a2h, ra = _ln_stats(a1_ref[...])
    a2 = a2h * va_ref[1:2, :] + va_ref[2:3, :]
    sa = _sigmoid(a2)
    return dict(ha=ha, sg=sg, a2h=a2h, ra=ra, a2=a2, sa=sa, a=a2 * sa)


def _group_b_fwd(hf, va_ref, wcat_ref, bfull_ref, ts):
    hu = hf[:, 2 * CONV_W:2 * CONV_W + GMLP_W]
    hv = hf[:, 2 * CONV_W + GMLP_W:2 * CONV_W + 2 * GMLP_W]
    u, du = _gelu(hu)
    v, dv = _gelu(hv)
    vhat, rv = _ln_stats(v)
    vn = vhat * va_ref[3:4, :] + va_ref[4:5, :]
    stacks, mixed = [], []
    for c0 in range(0, ts, CHUNK):
        st = _stack_heads(vn[c0:c0 + CHUNK, :]).astype(BF16)
        stacks.append(st)
        mixed.append(_dot(wcat_ref[...], st) + bfull_ref[...])
    mixed = jnp.concatenate(mixed, axis=0) if len(mixed) > 1 else mixed[0]
    return dict(u=u, du=du, dv=dv, vhat=vhat, rv=rv, stacks=stacks, mixed=mixed, g=u * mixed)


def _group_c_fwd(hf, kt_ref, vm_ref):
    qb = hf[:, IN_W - XATTN_W:IN_W].astype(BF16)
    s_all = _dot(qb, kt_ref[...])
    ps = []
    for g in range(XATTN_HEADS):
        s = s_all[:, g * N_MEM:(g + 1) * N_MEM]
        e = jnp.exp(s - jnp.max(s, axis=-1, keepdims=True))
        ps.append(e / jnp.sum(e, axis=-1, keepdims=True))
    p_all = jnp.concatenate(ps, axis=1)
    pb = p_all.astype(BF16)
    return dict(qb=qb, p=p_all, pb=pb, c=_dot(pb, vm_ref[...]))


def _mixer_fwd_call(x, w, ts, exch=None):
    seq = x.shape[0]
    n = seq // ts

    def body(x_ref, win_ref, cw_ref, va_ref, wcat_ref, bfull_ref, kt_ref, vm_ref, wout_ref, v1_ref,
             hb_ref, z1_ref, x1_ref, a1buf, buf, sh):
        i = pl.program_id(0)

        @pl.when(i == 0)
        def _():
            buf[0:HALO, :] = jnp.zeros((HALO, CONV_W), F32)

        @pl.when(i > 0)
        def _():
            buf[0:HALO, :] = buf[ts:ts + HALO, :]

        xv = x_ref[...]
        hb = _dot(xv.astype(BF16), win_ref[...]).astype(BF16)
        hb_ref[...] = hb
        hf = hb.astype(F32)
        ga = _group_a_fwd(hf, buf, sh, a1buf, cw_ref, va_ref, ts)
        gb = _group_b_fwd(hf, va_ref, wcat_ref, bfull_ref, ts)
        gc = _group_c_fwd(hf, kt_ref, vm_ref)
        cat = jnp.concatenate([ga["a"], gb["g"], gc["c"]], axis=1).astype(BF16)
        z1 = ALPHA * xv + _dot(cat, wout_ref[...])
        z1_ref[...] = z1
        xh, _ = _ln_stats(z1)
        x1_ref[...] = xh * v1_ref[0:1, :] + v1_ref[1:2, :]

    row = lambda width: pl.BlockSpec((ts, width), lambda i: (i, 0))
    return _grid_call(
        body, "mixer_fwd", n,
        in_specs=[row(D_MODEL), _const((D_MODEL, IN_W)), _const((HALO, CONV_W)), _const((8, CONV_W)),
                  _const((CHUNK, GMLP_HEADS * CHUNK)), _const((CHUNK, GMLP_W)), _const((XATTN_W, XATTN_HEADS * N_MEM)),
                  _const((XATTN_HEADS * N_MEM, XATTN_W)), _const((D_MODEL, D_MODEL)), _const((8, D_MODEL))],
        out_specs=[row(IN_W), row(D_MODEL), row(D_MODEL), row(CONV_W)],
        out_shape=[jax.ShapeDtypeStruct((seq, IN_W), BF16), jax.ShapeDtypeStruct((seq, D_MODEL), F32),
                   jax.ShapeDtypeStruct((seq, D_MODEL), F32), jax.ShapeDtypeStruct((seq, CONV_W), F32)],
        scratch_shapes=[pltpu.VMEM((ts + HALO, CONV_W), F32), pltpu.VMEM((7, ts + HALO, CONV_W), F32)],
        args=(x, w["win"], w["cw"], w["va"], w["wcat"], w["bfull"], w["kt"], w["vm"], w["wout"], w["v1"]), exch=exch)


def _store_blocks(acc, slabs_ref, sems, row_off, rows, first_block, n_blocks):
    copies = [pltpu.make_async_copy(acc.at[pl.ds(q * rows, rows)], slabs_ref.at[first_block + q, pl.ds(row_off, rows)],
                                    sems.at[q]) for q in range(n_blocks)]
    for cp in copies:
        cp.start()
    for cp in copies:
        cp.wait()


def _mixer_bwd_call(dx1, z1, hb, a1, x, w, ts, slabs, off_out, off_in):
    seq = dx1.shape[0]
    n = seq // ts
    halo_blocks = ts // HALO

    def body(slabs_in, dx1_ref, z1_ref, hb_ref, hprev_ref, a1_ref, x_ref, cw_ref, va_ref, wcat_ref, wcatt_ref, bfull_ref,
             kt_ref, ktt_ref, vm_ref, vmt_ref, woutt_ref, wint_ref, v1_ref,
             slabs_ref, dx_ref, dkt_ref, dvm_ref, dwcat_ref, dmsum_ref, dva_ref, dcw_ref, dv1_ref,
             buf, dbuf, da0buf, dwout_ref, dwin_ref, sems, sh, dsh):
        i = pl.program_id(0)

        @pl.when(i == 0)
        def _():
            for ref in (dwout_ref, dwin_ref, dkt_ref, dvm_ref, dwcat_ref, dmsum_ref, dva_ref, dcw_ref, dv1_ref):
                ref[...] = jnp.zeros(ref.shape, F32)
            dbuf[ts:ts + HALO, :] = jnp.zeros((HALO, CONV_W), F32)

        @pl.when(i > 0)
        def _():
            dbuf[ts:ts + HALO, :] = dbuf[0:HALO, :]

        dx1v = dx1_ref[...]
        xh1, r1 = _ln_stats(z1_ref[...])
        dv1_ref[0:1, :] += _rowsum(dx1v * xh1)
        dv1_ref[1:2, :] += _rowsum(dx1v)
        dz1 = _ln_bwd(dx1v, xh1, r1, v1_ref[0:1, :])
        dmix = dz1.astype(BF16)

        hf = hb_ref[...].astype(F32)
        hp = hprev_ref[...].astype(F32)
        a0p = hp[:, 0:CONV_W] * _sigmoid(hp[:, CONV_W:2 * CONV_W])
        buf[0:HALO, :] = jnp.where(i == n - 1, 0.0, a0p)
        ga = _group_a_fwd(hf, buf, sh, a1_ref, cw_ref, va_ref, ts, conv=False)
        gb = _group_b_fwd(hf, va_ref, wcat_ref, bfull_ref, ts)
        gc = _group_c_fwd(hf, kt_ref, vm_ref)
        cat = jnp.concatenate([ga["a"], gb["g"], gc["c"]], axis=1).astype(BF16)

        dwout_ref[...] += _dot_tn(cat, dmix)
        dcat = _dot(dmix, woutt_ref[...])
        da = dcat[:, 0:CONV_W]
        dg = dcat[:, CONV_W:CONV_W + GMLP_W]
        dc = dcat[:, CONV_W + GMLP_W:D_MODEL].astype(BF16)

        dp = _dot(dc, vmt_ref[...])
        dvm_ref[...] += _dot_tn(gc["pb"], dc)
        dss = []
        for g in range(XATTN_HEADS):
            sl = slice(g * N_MEM, (g + 1) * N_MEM)
            pg = gc["p"][:, sl]
            dpg = dp[:, sl]
            dss.append(pg * (dpg - jnp.sum(dpg * pg, axis=-1, keepdims=True)))
        ds = jnp.concatenate(dss, axis=1).astype(BF16)
        dq = _dot(ds, ktt_ref[...])
        dkt_ref[...] += _dot_tn(gc["qb"], ds)

        dmixed = dg * gb["u"]
        dhu = dg * gb["mixed"] * gb["du"]
        dvns = []
        for j, c0 in enumerate(range(0, ts, CHUNK)):
            dm = dmixed[c0:c0 + CHUNK, :]
            dmb = dm.astype(BF16)
            dmsum_ref[...] += dm
            dwcat_ref[...] += _dot_nt(dmb, gb["stacks"][j])
            dst = _dot(wcatt_ref[...], dmb)
            dvn_c = jnp.zeros((CHUNK, GMLP_W), F32)
            for h in range(GMLP_HEADS):
                dvn_c = dvn_c + jnp.where(_head_mask(GMLP_W, h), dst[h * CHUNK:(h + 1) * CHUNK, :], 0.0)
            dvns.append(dvn_c)
        dvn = jnp.concatenate(dvns, axis=0) if len(dvns) > 1 else dvns[0]
        dva_ref[3:4, :] += _rowsum(dvn * gb["vhat"])
        dva_ref[4:5, :] += _rowsum(dvn)
        dhv = _ln_bwd(dvn, gb["vhat"], gb["rv"], va_ref[3:4, :]) * gb["dv"]

        a2, sa = ga["a2"], ga["sa"]
        da2 = da * (sa * (1.0 + a2 * (1.0 - sa)))
        dva_ref[1:2, :] += _rowsum(da2 * ga["a2h"])
        dva_ref[2:3, :] += _rowsum(da2)
        da1 = _ln_bwd(da2, ga["a2h"], ga["ra"], va_ref[1:2, :])
        dva_ref[0:1, :] += _rowsum(da1)
        dbuf[0:ts, :] = da1
        _shift_copies(dbuf, dsh, ts + HALO - 8)
        _conv31_dw(buf, sh, dbuf, dcw_ref, ts)
        _conv31_dx(dbuf, dsh, cw_ref, da0buf, ts)
        da0 = da0buf[...]
        sg = ga["sg"]
        dha = da0 * sg
        dhg = da0 * ga["ha"] * sg * (1.0 - sg)

        dh = jnp.concatenate([dha, dhg, dhu, dhv, dq], axis=1).astype(BF16)
        dx_ref[...] = _dot(dh, wint_ref[...]) + ALPHA * dz1
        dwin_ref[...] += _dot_tn(dh, x_ref[...].astype(BF16))

        @pl.when(i == n - 1)
        def _():
            _store_blocks(dwout_ref, slabs_ref, sems, off_out, D_MODEL // N_DEV, 0, N_DEV)
            _store_blocks(dwin_ref, slabs_ref, sems, off_in, IN_W // N_DEV, 0, N_DEV)

    rev = lambda width: pl.BlockSpec((ts, width), lambda i: (n - 1 - i, 0))
    prev = pl.BlockSpec((HALO, 2 * CONV_W), lambda i: (jnp.maximum((n - 1 - i) * halo_blocks - 1, 0), 0))
    hbm = pl.BlockSpec(memory_space=pl.ANY)
    hc = GMLP_HEADS * CHUNK
    am = XATTN_HEADS * N_MEM
    return pl.pallas_call(
        body, name="mixer_bwd", grid=(n,),
        in_specs=[hbm, rev(D_MODEL), rev(D_MODEL), rev(IN_W), prev, rev(CONV_W), rev(D_MODEL), _const((HALO, CONV_W)),
                  _const((8, CONV_W)), _const((CHUNK, hc)), _const((hc, CHUNK)), _const((CHUNK, GMLP_W)),
                  _const((XATTN_W, am)), _const((am, XATTN_W)), _const((am, XATTN_W)), _const((XATTN_W, am)),
                  _const((D_MODEL, D_MODEL)), _const((IN_W, D_MODEL)), _const((8, D_MODEL))],
        out_specs=[hbm, rev(D_MODEL), _acc((XATTN_W, am)), _acc((am, XATTN_W)),
                   _acc((CHUNK, hc)), _acc((CHUNK, GMLP_W)), _acc((8, CONV_W)), _acc((HALO, CONV_W)),
                   _acc((8, D_MODEL))],
        out_shape=[jax.ShapeDtypeStruct(slabs.shape, F32), jax.ShapeDtypeStruct((seq, D_MODEL), F32),
                   jax.ShapeDtypeStruct((XATTN_W, am), F32),
                   jax.ShapeDtypeStruct((am, XATTN_W), F32), jax.ShapeDtypeStruct((CHUNK, hc), F32),
                   jax.ShapeDtypeStruct((CHUNK, GMLP_W), F32), jax.ShapeDtypeStruct((8, CONV_W), F32),
                   jax.ShapeDtypeStruct((HALO, CONV_W), F32), jax.ShapeDtypeStruct((8, D_MODEL), F32)],
        scratch_shapes=[pltpu.VMEM((ts + HALO, CONV_W), F32),
                        pltpu.VMEM((ts + HALO, CONV_W), F32), pltpu.VMEM((ts, CONV_W), F32),
                        pltpu.VMEM((D_MODEL, D_MODEL), F32), pltpu.VMEM((IN_W, D_MODEL), F32),
                        pltpu.SemaphoreType.DMA((N_DEV,)),
                        pltpu.VMEM((7, ts + HALO, CONV_W), F32), pltpu.VMEM((7, ts + HALO, CONV_W), F32)],
        input_output_aliases={0: 0},
        compiler_params=_params(dimension_semantics=("arbitrary",)),
    )(slabs, dx1, z1, hb, hb, a1, x, w["cw"], w["va"], w["wcat"], w["wcatt"], w["bfull"], w["kt"], w["ktt"], w["vm"],
      w["vmt"], w["woutt"], w["wint"], w["v1"])


FFN_HALO = 8
FF_GROUP = D_FF // N_DEV
FF_GROUP_PAD = D_FF_PAD // N_DEV


def _ffn_taps(ubuf, ts, lo, hi):
    return tuple(ubuf[FFN_HALO - (FFN_CONV_K - 1) + k:FFN_HALO - (FFN_CONV_K - 1) + k + ts, lo:hi]
                 for k in range(FFN_CONV_K))


def _ffn_gate(ubuf, cf_ref, ts, lo, hi):
    taps = _ffn_taps(ubuf, ts, lo, hi)
    g = cf_ref[3:4, lo:hi] + cf_ref[2:3, lo:hi] * taps[2]
    g = g + cf_ref[1:2, lo:hi] * taps[1]
    return g + cf_ref[0:1, lo:hi] * taps[0]


FFN_LANE_CHUNKS = ((0, D_FF_PAD // 2), (D_FF_PAD // 2, D_FF_PAD))


def _ffn_fwd_call(x1, w, ts, exch=None):
    seq = x1.shape[0]
    n = seq // ts

    def body(x1_ref, wg_ref, wv_ref, cf_ref, wdown_ref, v2_ref, ug_ref, uv_ref, sl_ref, dsl_ref, z2_ref, x2_ref, ubuf):
        i = pl.program_id(0)

        @pl.when(i == 0)
        def _():
            ubuf[0:FFN_HALO, :] = jnp.zeros((FFN_HALO, D_FF_PAD), F32)

        @pl.when(i > 0)
        def _():
            ubuf[0:FFN_HALO, :] = ubuf[ts:ts + FFN_HALO, :]

        xv = x1_ref[...]
        xb = xv.astype(BF16)
        y = ALPHA * xv
        for lo, hi in FFN_LANE_CHUNKS:
            ug = _dot(xb, wg_ref[:, lo:hi]).astype(BF16)
            uv = _dot(xb, wv_ref[:, lo:hi]).astype(BF16)
            ug_ref[:, lo:hi] = ug
            uv_ref[:, lo:hi] = uv
            ubuf[FFN_HALO:FFN_HALO + ts, lo:hi] = ug.astype(F32)
            gate = _ffn_gate(ubuf, cf_ref, ts, lo, hi)
            sg = _sigmoid(gate)
            sl = gate * sg
            sl_ref[:, lo:hi] = sl.astype(BF16)
            dsl_ref[:, lo:hi] = (sg * (1.0 + gate * (1.0 - sg))).astype(BF16)
            act = (sl * uv.astype(F32)).astype(BF16)
            y = y + _dot(act, wdown_ref[lo:hi, :])
        z2_ref[...] = y
        xh, _ = _ln_stats(y)
        x2_ref[...] = xh * v2_ref[0:1, :] + v2_ref[1:2, :]

    row = lambda width: pl.BlockSpec((ts, width), lambda i: (i, 0))
    return _grid_call(
        body, "ffn_fwd", n,
        in_specs=[row(D_MODEL), _const((D_MODEL, D_FF_PAD)), _const((D_MODEL, D_FF_PAD)), _const((8, D_FF_PAD)),
                  _const((D_FF_PAD, D_MODEL)), _const((8, D_MODEL))],
        out_specs=[row(D_FF_PAD)] * 4 + [row(D_MODEL), row(D_MODEL)],
        out_shape=[jax.ShapeDtypeStruct((seq, D_FF_PAD), BF16)] * 4 + [jax.ShapeDtypeStruct((seq, D_MODEL), F32)] * 2,
        scratch_shapes=[pltpu.VMEM((ts + FFN_HALO, D_FF_PAD), F32)],
        args=(x1, w["wg"], w["wv"], w["cf"], w["wdown"], w["v2"]), exch=exch)


def _ffn_bwd_call(dx2_or_target, z2, ug, uv, sl, dsl, w, ts, last, slabs, row_off, exch=None):
    seq = z2.shape[0]
    n = seq // ts
    halo_blocks = ts // 16

    def body(slabs_in, dx2_ref, z2_ref, ug_ref, uv_ref, sl_ref, dsl_ref, uprev_ref, cf_ref, wdownt_ref, v2_ref,
             slabs_ref, dug_ref, duv_ref, dz2_ref, dcf_ref, dv2_ref, loss_ref,
             ubuf, dgbuf, dwacc, sems):
        i = pl.program_id(0)

        @pl.when(i == 0)
        def _():
            dwacc[...] = jnp.zeros(dwacc.shape, F32)
            dcf_ref[...] = jnp.zeros(dcf_ref.shape, F32)
            dv2_ref[...] = jnp.zeros(dv2_ref.shape, F32)
            loss_ref[...] = jnp.zeros(loss_ref.shape, F32)
            dgbuf[ts:ts + FFN_HALO, :] = jnp.zeros((FFN_HALO, D_FF_PAD), F32)

        @pl.when(i > 0)
        def _():
            dgbuf[ts:ts + FFN_HALO, :] = dgbuf[0:FFN_HALO, :]

        xh2, r2 = _ln_stats(z2_ref[...])
        if last:
            diff = xh2 * v2_ref[0:1, :] + v2_ref[1:2, :] - dx2_ref[...]
            loss_ref[...] += jnp.sum(diff * diff) * (0.5 / D_MODEL)
            dx2v = diff * (1.0 / D_MODEL)
        else:
            dx2v = dx2_ref[...]
        dv2_ref[0:1, :] += _rowsum(dx2v * xh2)
        dv2_ref[1:2, :] += _rowsum(dx2v)
        dz2 = _ln_bwd(dx2v, xh2, r2, v2_ref[0:1, :])
        dz2_ref[...] = dz2
        dy = dz2.astype(BF16)

        up = uprev_ref[...].astype(F32)[8:16, :]
        ubuf[0:FFN_HALO, :] = jnp.where(i == n - 1, 0.0, up)
        ubuf[FFN_HALO:FFN_HALO + ts, :] = ug_ref[...].astype(F32)
        for lo, hi in FFN_LANE_CHUNKS:
            taps = _ffn_taps(ubuf, ts, lo, hi)
            sl = sl_ref[:, lo:hi].astype(F32)
            uvf = uv_ref[:, lo:hi].astype(F32)
            act = (sl * uvf).astype(BF16)
            dwacc[lo:hi, :] += _dot_tn(act, dy)
            dact = _dot(dy, wdownt_ref[:, lo:hi])
            duv_ref[:, lo:hi] = (dact * sl).astype(BF16)
            dgate = dact * uvf * dsl_ref[:, lo:hi].astype(F32)
            dgbuf[0:ts, lo:hi] = dgate
            dcf_ref[3:4, lo:hi] += _rowsum(dgate)
            for k in range(FFN_CONV_K):
                dcf_ref[k:k + 1, lo:hi] += _rowsum(dgate * taps[k])
            dug = cf_ref[2:3, lo:hi] * dgate + cf_ref[1:2, lo:hi] * dgbuf[1:1 + ts, lo:hi]
            dug = dug + cf_ref[0:1, lo:hi] * dgbuf[2:2 + ts, lo:hi]
            dug_ref[:, lo:hi] = dug.astype(BF16)

        @pl.when(i == n - 1)
        def _():
            _store_blocks(dwacc, slabs_ref, sems, row_off, D_FF_PAD // N_DEV, 0, N_DEV)

    rev = lambda width: pl.BlockSpec((ts, width), lambda i: (n - 1 - i, 0))
    prev = pl.BlockSpec((16, D_FF_PAD), lambda i: (jnp.maximum((n - 1 - i) * halo_blocks - 1, 0), 0))
    hbm = pl.BlockSpec(memory_space=pl.ANY)
    return _grid_call(
        body, "ffn_bwd_last" if last else "ffn_bwd", n,
        in_specs=[hbm, rev(D_MODEL), rev(D_MODEL)] + [rev(D_FF_PAD)] * 4 + [prev, _const((8, D_FF_PAD)),
                                                                           _const((D_MODEL, D_FF_PAD)), _const((8, D_MODEL))],
        out_specs=[hbm, rev(D_FF_PAD), rev(D_FF_PAD), rev(D_MODEL),
                   _acc((8, D_FF_PAD)), _acc((8, D_MODEL)), _acc((8, 128))],
        out_shape=[jax.ShapeDtypeStruct(slabs.shape, F32),
                   jax.ShapeDtypeStruct((seq, D_FF_PAD), BF16), jax.ShapeDtypeStruct((seq, D_FF_PAD), BF16),
                   jax.ShapeDtypeStruct((seq, D_MODEL), F32),
                   jax.ShapeDtypeStruct((8, D_FF_PAD), F32), jax.ShapeDtypeStruct((8, D_MODEL), F32),
                   jax.ShapeDtypeStruct((8, 128), F32)],
        scratch_shapes=[pltpu.VMEM((ts + FFN_HALO, D_FF_PAD), F32), pltpu.VMEM((ts + FFN_HALO, D_FF_PAD), F32),
                        pltpu.VMEM((D_FF_PAD, D_MODEL), F32), pltpu.SemaphoreType.DMA((N_DEV,))],
        args=(slabs, dx2_or_target, z2, ug, uv, sl, dsl, ug, w["cf"], w["wdownt"], w["v2"]), aliases={0: 0}, exch=exch)


def _proj_bwd_call(d, wt, xin, addend, scale, ts, name, slabs, row_off, first_block, n_blocks, exch=None):
    seq, k = d.shape
    n = seq // ts

    def body(slabs_in, d_ref, wt_ref, xin_ref, add_ref, slabs_ref, dx_ref, acc, sems):
        i = pl.program_id(0)

        @pl.when(i == 0)
        def _():
            acc[...] = jnp.zeros(acc.shape, F32)

        dv = d_ref[...]
        dx_ref[...] = _dot(dv, wt_ref[...]) + scale * add_ref[...]
        acc[...] += _dot_tn(dv, xin_ref[...].astype(BF16))

        @pl.when(i == n - 1)
        def _():
            _store_blocks(acc, slabs_ref, sems, row_off, k // n_blocks, first_block, n_blocks)

    row = lambda width: pl.BlockSpec((ts, width), lambda i: (i, 0))
    hbm = pl.BlockSpec(memory_space=pl.ANY)
    return _grid_call(
        body, name, n,
        in_specs=[hbm, row(k), _const((k, D_MODEL)), row(D_MODEL), row(D_MODEL)],
        out_specs=[hbm, row(D_MODEL)],
        out_shape=[jax.ShapeDtypeStruct(slabs.shape, F32), jax.ShapeDtypeStruct((seq, D_MODEL), F32)],
        scratch_shapes=[pltpu.VMEM((k, D_MODEL), F32), pltpu.SemaphoreType.DMA((n_blocks,))],
        args=(slabs, d, wt, xin, addend), aliases={0: 0}, exch=exch)


MEM_FOLD = BLOB_LANES // XATTN_W


def _mem_proj_call(memq, wk_flat, wv_flat):
    def body(memq_ref, wk_ref, wv_ref, kh_ref, vh_ref):
        for w_ref, o_ref in ((wk_ref, kh_ref), (wv_ref, vh_ref)):
            acc = jnp.zeros((N_MEM, XATTN_W), F32)
            for q in range(MEM_FOLD):
                acc = acc + _dot(memq_ref[q], w_ref[:, q * XATTN_W:(q + 1) * XATTN_W])
            o_ref[...] = acc

    out = jax.ShapeDtypeStruct((N_MEM, XATTN_W), F32)
    return pl.pallas_call(body, name="mem_proj", out_shape=[out, out], compiler_params=_params())(memq, wk_flat, wv_flat)


def _mem_proj_bwd_call(memq, dkh, dvh, slabs, off_k, off_v):
    rows = D_MODEL // MEM_FOLD

    def body(slabs_in, memq_ref, dkh_ref, dvh_ref, slabs_ref, acc, sems):
        for d_ref, off in ((dkh_ref, off_k), (dvh_ref, off_v)):
            dv = d_ref[...].astype(BF16)
            for q in range(MEM_FOLD):
                acc[:, q * XATTN_W:(q + 1) * XATTN_W] = _dot_tn(memq_ref[q], dv)
            _store_blocks(acc, slabs_ref, sems, off, rows // N_DEV, 0, N_DEV)

    hbm = pl.BlockSpec(memory_space=pl.ANY)
    vmem = pl.BlockSpec(memory_space=pltpu.VMEM)
    return pl.pallas_call(
        body, name="mem_proj_bwd", in_specs=[hbm, vmem, vmem, vmem], out_specs=hbm,
        out_shape=jax.ShapeDtypeStruct(slabs.shape, F32),
        scratch_shapes=[pltpu.VMEM((rows, BLOB_LANES), F32), pltpu.SemaphoreType.DMA((N_DEV,))],
        input_output_aliases={0: 0}, compiler_params=_params(),
    )(slabs, memq, dkh, dvh)


def _place():
    return lax.axis_index("x"), lax.axis_index("y"), lax.axis_index("c")


def _all_gather_call(arrs, pieces, name):
    n_in, n_p = len(arrs), len(pieces)

    def body(*refs):
        ins, outs = refs[:n_in], refs[n_in:n_in + n_p]
        send_sems, recv_sems, local_sems = refs[n_in + n_p:]
        x, y, c = _place()
        me, sibling = (x, y, c), (x, y, 1 - c)
        chips = [(1 - x, y), (x, 1 - y), (1 - x, 1 - y)]

        def src(a):
            idx, r0, rows = pieces[a]
            return ins[idx] if r0 is None else ins[idx].at[pl.ds(r0, rows)]

        def slab(a, p):
            return outs[a].at[4 * p[0] + 2 * p[1] + p[2]]

        def copy(a, k, block, to, own=False):
            return pltpu.make_async_remote_copy(
                src_ref=src(a) if own else slab(a, block), dst_ref=slab(a, block),
                send_sem=send_sems.at[a, k], recv_sem=recv_sems.at[a, k], device_id=to, device_id_type=MESH)

        mine = [pltpu.make_async_copy(src(a), slab(a, me), local_sems.at[a]) for a in range(n_p)]
        for cp in mine:
            cp.start()
        first = []
        for a in range(n_p):
            first.append(copy(a, 0, me, sibling, own=True))
            first += [copy(a, 1 + j, me, (*chip, c), own=True) for j, chip in enumerate(chips)]
        for cp in first:
            cp.start()
        passed = []
        for a in range(n_p):
            for j, chip in enumerate(chips):
                copy(a, 1 + j, (*chip, c), me).wait_recv()
                cp = copy(a, 4 + j, (*chip, c), sibling)
                cp.start()
                passed.append(cp)
        for a in range(n_p):
            copy(a, 0, sibling, me).wait_recv()
            for j, chip in enumerate(chips):
                copy(a, 4 + j, (*chip, 1 - c), me).wait_recv()
        for cp in first + passed:
            cp.wait_send()
        for cp in mine:
            cp.wait()

    def out_shape(piece):
        idx, r0, rows = piece
        a = arrs[idx]
        return jax.ShapeDtypeStruct((N_DEV,) + (a.shape if r0 is None else (rows,) + a.shape[1:]), a.dtype)

    hbm = pl.BlockSpec(memory_space=pl.ANY)
    return pl.pallas_call(
        body, name=name,
        in_specs=[hbm] * n_in, out_specs=[hbm] * n_p, out_shape=[out_shape(p) for p in pieces],
        scratch_shapes=[pltpu.SemaphoreType.DMA((n_p, 7)), pltpu.SemaphoreType.DMA((n_p, 7)),
                        pltpu.SemaphoreType.DMA((n_p,))],
    )(*arrs)


def _flip(v, f):
    return 1 - v if f else v


def _gather_exchange(blob, r0, rows):
    def build(ins, outs, send_sems, recv_sems, local_sems):
        x, y, c = _place()
        src = ins[0].at[pl.ds(r0, rows)]
        dst = outs[0].at[4 * x + 2 * y + c]
        flips = [(fx, fy, fc) for fx in (0, 1) for fy in (0, 1) for fc in (0, 1) if fx or fy or fc]
        remote = [pltpu.make_async_remote_copy(
            src_ref=src, dst_ref=dst, send_sem=send_sems.at[k], recv_sem=recv_sems.at[k],
            device_id=(_flip(x, fx), _flip(y, fy), _flip(c, fc)), device_id_type=MESH)
            for k, (fx, fy, fc) in enumerate(flips)]
        return remote, [pltpu.make_async_copy(src, dst, local_sems.at[0])]

    return _Exchange([blob], [jax.ShapeDtypeStruct((N_DEV, rows) + blob.shape[1:], blob.dtype)], N_DEV - 1, build)


def _swap_core_copies(g_ref, r_ref, send_sems, recv_sems):
    x, y, c = _place()
    return [pltpu.make_async_remote_copy(
        src_ref=g_ref.at[2 * k + (1 - c)], dst_ref=r_ref.at[k], send_sem=send_sems.at[k], recv_sem=recv_sems.at[k],
        device_id=(x, y, 1 - c), device_id_type=MESH) for k in range(4)]


def _swap_chip_copies(p_ref, r_ref, send_sems, recv_sems):
    x, y, c = _place()
    chips = [(1 - x, y), (x, 1 - y), (1 - x, 1 - y)]
    return [pltpu.make_async_remote_copy(
        src_ref=p_ref.at[2 * px + py], dst_ref=r_ref.at[j], send_sem=send_sems.at[j], recv_sem=recv_sems.at[j],
        device_id=(px, py, c), device_id_type=MESH) for j, (px, py) in enumerate(chips)]


def _swap_exchange(a, copies, n):
    return _Exchange([a], [jax.ShapeDtypeStruct((n,) + a.shape[1:], a.dtype)], n,
                     lambda ins, outs, send_sems, recv_sems, local_sems: (copies(ins[0], outs[0], send_sems, recv_sems), []))


def _swap_call(a, copies, n, name):
    def body(a_ref, r_ref, send_sems, recv_sems):
        cps = copies(a_ref, r_ref, send_sems, recv_sems)
        for cp in cps:
            cp.start()
        for cp in cps:
            cp.wait()

    hbm = pl.BlockSpec(memory_space=pl.ANY)
    return pl.pallas_call(
        body, name=name, in_specs=[hbm], out_specs=hbm, out_shape=jax.ShapeDtypeStruct((n,) + a.shape[1:], a.dtype),
        scratch_shapes=[pltpu.SemaphoreType.DMA((n,)), pltpu.SemaphoreType.DMA((n,))],
    )(a)


ADD_ROWS = 184


def _pair_add_call(g, r, c):
    _, rows, width = g.shape

    def body(c_ref, g_ref, r_ref, o_ref):
        o_ref[...] = (g_ref[...] + r_ref[...]).astype(BF16)

    return pl.pallas_call(
        body, name="rs_pair_add",
        grid_spec=pltpu.PrefetchScalarGridSpec(
            num_scalar_prefetch=1, grid=(4, rows // ADD_ROWS),
            in_specs=[pl.BlockSpec((None, ADD_ROWS, width), lambda k, i, c_ref: (2 * k + c_ref[0], i, 0)),
                      pl.BlockSpec((None, ADD_ROWS, width), lambda k, i, c_ref: (k, i, 0))],
            out_specs=pl.BlockSpec((None, ADD_ROWS, width), lambda k, i, c_ref: (k, i, 0))),
        out_shape=jax.ShapeDtypeStruct((4, rows, width), BF16),
        compiler_params=_params(dimension_semantics=("arbitrary", "arbitrary")),
    )(c, g, r)


def _adam(w, g, m, v):
    mn = ADAM_B1 * m + (1.0 - ADAM_B1) * g
    vn = ADAM_B2 * v + (1.0 - ADAM_B2) * (g * g)
    m_hat = mn / (1.0 - ADAM_B1 ** ADAM_STEP)
    v_hat = vn / (1.0 - ADAM_B2 ** ADAM_STEP)
    return -ADAM_LR * (m_hat / (jnp.sqrt(v_hat) + ADAM_EPS) + ADAM_WD * w), mn, vn


def _chip_add_adamw_call(slabs, from_sibling, from_chips, me, chip, w, m, v):
    _, rows, width = slabs.shape

    def body(me_ref, chip_ref, own_ref, sib_ref, r_ref, w_ref, m_ref, v_ref, g_ref, d_ref, mo_ref, vo_ref):
        g = own_ref[...] + sib_ref[...]
        for j in range(3):
            g = g + r_ref[j].astype(F32)
        g_ref[...] = g
        d_ref[...], mo_ref[...], vo_ref[...] = _adam(w_ref[...], g, m_ref[...], v_ref[...])

    spec = pl.BlockSpec((ADD_ROWS, width), lambda i, me_ref, chip_ref: (i, 0))
    return pl.pallas_call(
        body, name="rs_chip_add_adamw",
        grid_spec=pltpu.PrefetchScalarGridSpec(
            num_scalar_prefetch=2, grid=(rows // ADD_ROWS,),
            in_specs=[pl.BlockSpec((None, ADD_ROWS, width), lambda i, me_ref, chip_ref: (me_ref[0], i, 0)),
                      pl.BlockSpec((None, ADD_ROWS, width), lambda i, me_ref, chip_ref: (chip_ref[0], i, 0)),
                      pl.BlockSpec((3, ADD_ROWS, width), lambda i, me_ref, chip_ref: (0, i, 0)), spec, spec, spec],
            out_specs=[spec] * 4),
        out_shape=[jax.ShapeDtypeStruct((rows, width), F32)] * 4,
        compiler_params=_params(dimension_semantics=("arbitrary",)),
    )(me, chip, slabs, from_sibling, from_chips, w, m, v)


def _adamw_whole_call(params, name):
    n = len(params)

    def body(*refs):
        ins, outs = refs[:4 * n], refs[4 * n:]
        for a in range(n):
            w_ref, g_ref, m_ref, v_ref = ins[4 * a:4 * a + 4]
            outs[3 * a][...], outs[3 * a + 1][...], outs[3 * a + 2][...] = _adam(w_ref[...], g_ref[...], m_ref[...],
                                                                              v_ref[...])

    flat = [a for p in params for a in p]
    shapes = [jax.ShapeDtypeStruct(p[0].shape, F32) for p in params for _ in range(3)]
    out = pl.pallas_call(body, name=name, out_shape=shapes, compiler_params=_params())(*flat)
    return [tuple(out[3 * a:3 * a + 3]) for a in range(n)]


GATHERED_ACCS = (("dva", (8, CONV_W)), ("dv1", (8, D_MODEL)), ("dv2", (8, D_MODEL)), ("dcf", (8, D_FF_PAD)),
                 ("dcw", (HALO, CONV_W)), ("dwcat", (CHUNK, GMLP_HEADS * CHUNK)), ("dmsum", (CHUNK, GMLP_W)))
VEC_A = ("conv_a_b", "ln_a_g", "ln_a_b", "ln_v_g", "ln_v_b")
REP_IN_KERNEL = VEC_A + ("ln1_g", "ln1_b", "ln2_g", "ln2_b", "w_s", "b_s")


def _replicated_update_call(gathered, p, mom_m, mom_v):
    n_acc = len(GATHERED_ACCS)
    n_rep = len(REP_IN_KERNEL)

    def body(*refs):
        acc_refs = refs[:DEPTH * n_acc]
        wmv = refs[DEPTH * n_acc:DEPTH * n_acc + 3 * n_rep]
        outs = refs[DEPTH * n_acc + 3 * n_rep:]
        out_par = {nm: outs[4 * a:4 * a + 4] for a, nm in enumerate(REP_IN_KERNEL)}
        out_dcf = outs[4 * n_rep:4 * n_rep + DEPTH]
        out_dcw = outs[4 * n_rep + DEPTH:4 * n_rep + 2 * DEPTH]
        par = {nm: wmv[3 * a:3 * a + 3] for a, nm in enumerate(REP_IN_KERNEL)}
        tril = (lax.broadcasted_iota(jnp.int32, (CHUNK, CHUNK), 0) >= lax.broadcasted_iota(jnp.int32, (CHUNK, CHUNK), 1))
        head = lax.broadcasted_iota(jnp.int32, (8, GMLP_W), 0) * HEAD_DIM
        lane = lax.broadcasted_iota(jnp.int32, (8, GMLP_W), 1)
        sel = jnp.where((lane >= head) & (lane < head + HEAD_DIM), 1.0, 0.0)

        def update(nm, idx, g):
            w_ref, m_ref, v_ref = par[nm]
            d, mn, vn = _adam(w_ref[idx], g, m_ref[idx], v_ref[idx])
            g_ref, d_ref, mo_ref, vo_ref = out_par[nm]
            g_ref[idx] = g
            d_ref[idx] = d
            mo_ref[idx] = mn
            vo_ref[idx] = vn

        for l in range(DEPTH):
            tot = {}
            for a, (nm, _) in enumerate(GATHERED_ACCS):
                ref = acc_refs[l * n_acc + a]
                s = ref[0]
                for j in range(1, N_DEV):
                    s = s + ref[j]
                tot[nm] = s
            out_dcf[l][...] = tot["dcf"]
            out_dcw[l][...] = tot["dcw"]
            row = (slice(l, l + 1), slice(None))
            for k, nm in enumerate(VEC_A):
                update(nm, row, tot["dva"][k:k + 1, :])
            update("ln1_g", row, tot["dv1"][0:1, :])
            update("ln1_b", row, tot["dv1"][1:2, :])
            update("ln2_g", row, tot["dv2"][0:1, :])
            update("ln2_b", row, tot["dv2"][1:2, :])
            for h in range(GMLP_HEADS):
                gw = jnp.where(tril, tot["dwcat"][:, h * CHUNK:(h + 1) * CHUNK], 0.0)
                update("w_s", (l, h), gw)
            gb = lax.dot_general(sel, tot["dmsum"], (((1,), (1,)), ((), ())), precision=lax.Precision.HIGHEST,
                                 preferred_element_type=F32)
            for h in range(GMLP_HEADS):
                update("b_s", (l, slice(h, h + 1), slice(None)), gb[h:h + 1, :])

    ins = [gathered[l][nm] for l in range(DEPTH) for nm, _ in GATHERED_ACCS]
    ins += [t[nm] for nm in REP_IN_KERNEL for t in (p, mom_m, mom_v)]
    shapes = [jax.ShapeDtypeStruct(p[nm].shape, F32) for nm in REP_IN_KERNEL for _ in range(4)]
    shapes += [jax.ShapeDtypeStruct((8, D_FF_PAD), F32)] * DEPTH + [jax.ShapeDtypeStruct((HALO, CONV_W), F32)] * DEPTH
    out = pl.pallas_call(body, name="replicated_update", out_shape=shapes, compiler_params=_params())(*ins)
    res = [{nm: out[4 * a + k] for a, nm in enumerate(REP_IN_KERNEL)} for k in range(4)]
    return res, out[4 * n_rep:4 * n_rep + DEPTH], out[4 * n_rep + DEPTH:]


BLOCK_ROWS = (("w_in", IN_W // N_DEV), ("w_out", D_MODEL // N_DEV), ("w_up", 2 * FF_GROUP_PAD),
              ("w_down", FF_GROUP_PAD), ("w_mk", D_MODEL // N_DEV // MEM_FOLD), ("w_mv", D_MODEL // N_DEV // MEM_FOLD))
LAYER_ROWS = sum(r for _, r in BLOCK_ROWS)
assert LAYER_ROWS % ADD_ROWS == 0 and all(r % 16 == 0 for _, r in BLOCK_ROWS)


def _row_off(name):
    off = 0
    for nm, r in BLOCK_ROWS:
        if nm == name:
            return off
        off += r
    raise KeyError(name)


def _to_rows(name, a):
    if name == "w_in":
        return a.T
    if name == "w_up":
        t = a.T.reshape(2, FF_GROUP, D_MODEL)
        return jnp.pad(t, ((0, 0), (0, FF_GROUP_PAD - FF_GROUP), (0, 0))).reshape(2 * FF_GROUP_PAD, D_MODEL)
    if name == "w_down":
        return jnp.pad(a, ((0, FF_GROUP_PAD - FF_GROUP), (0, 0)))
    if name == "w_out":
        return a
    return a.reshape(-1, BLOB_LANES)


def _from_rows(name, r):
    if name == "w_in":
        return r.T
    if name == "w_up":
        return r.reshape(2, FF_GROUP_PAD, D_MODEL)[:, :FF_GROUP].reshape(2 * FF_GROUP, D_MODEL).T
    if name == "w_down":
        return r[:FF_GROUP]
    if name == "w_out":
        return r
    return r.reshape(D_MODEL // N_DEV, XATTN_W)


def _blob(tree, l):
    return jnp.concatenate([_to_rows(nm, tree[nm][l]) for nm, _ in BLOCK_ROWS], axis=0)


def _unblob(blobs):
    return {nm: jnp.stack([_from_rows(nm, b[_row_off(nm):_row_off(nm) + r]) for b in blobs]) for nm, r in BLOCK_ROWS}


def _ff_interleave(a):
    lead = a.shape[:-1]
    t = a.reshape(lead + (N_DEV, FF_GROUP))
    return jnp.pad(t, [(0, 0)] * len(lead) + [(0, 0), (0, FF_GROUP_PAD - FF_GROUP)]).reshape(lead + (D_FF_PAD,))


def _ff_deinterleave(a):
    lead = a.shape[:-1]
    return a.reshape(lead + (N_DEV, FF_GROUP_PAD))[..., :FF_GROUP].reshape(lead + (D_FF,))


def _head_table():
    hd = jnp.arange(XATTN_W) // HEAD_DIM
    return (hd[None, :] == jnp.arange(XATTN_HEADS)[:, None]).astype(F32)


def _mixer_operands(mat, conv_a_w, p, l, memq):
    w = {}
    w["wint"] = mat["w_in"]
    w["win"] = mat["w_in"].T
    w["wout"] = mat["w_out"]
    w["woutt"] = mat["w_out"].T
    w["cw"] = conv_a_w
    zeros = jnp.zeros((3, CONV_W), F32)
    w["va"] = jnp.concatenate([p[nm][l][None] for nm in VEC_A] + [zeros], axis=0)
    tril = jnp.tril(jnp.ones((CHUNK, CHUNK), F32))
    w["wcat"] = (p["w_s"][l] * tril[None]).transpose(1, 0, 2).reshape(CHUNK, GMLP_HEADS * CHUNK).astype(BF16)
    w["wcatt"] = w["wcat"].T
    w["bfull"] = jnp.repeat(p["b_s"][l].T, HEAD_DIM, axis=1)
    kh, vh = _mem_proj_call(memq, mat["w_mk"], mat["w_mv"])
    hm = _head_table()
    scale = 1.0 / math.sqrt(HEAD_DIM)
    w["kt"] = (kh.T[:, None, :] * hm.T[:, :, None] * scale).reshape(XATTN_W, XATTN_HEADS * N_MEM).astype(BF16)
    w["ktt"] = w["kt"].T
    w["vm"] = (hm[:, None, :] * vh[None]).reshape(XATTN_HEADS * N_MEM, XATTN_W).astype(BF16)
    w["vmt"] = w["vm"].T
    zeros = jnp.zeros((6, D_MODEL), F32)
    w["v1"] = jnp.concatenate([p["ln1_g"][l][None], p["ln1_b"][l][None], zeros], axis=0)
    return w


def _ffn_operands(w_up, w_down, conv_f_w, p, l):
    w = {}
    w["wgt"] = w_up[:D_FF_PAD]
    w["wvt"] = w_up[D_FF_PAD:]
    w["wg"] = w["wgt"].T
    w["wv"] = w["wvt"].T
    w["wdown"] = w_down
    w["wdownt"] = w_down.T
    zeros = jnp.zeros((6, D_MODEL), F32)
    w["v2"] = jnp.concatenate([p["ln2_g"][l][None], p["ln2_b"][l][None], zeros], axis=0)
    w["cf"] = jnp.concatenate([conv_f_w, _ff_interleave(p["conv_f_b"][l][None]), jnp.zeros((4, D_FF_PAD), F32)], axis=0)
    return w


TS_MIXER = 256
TS_FFN = 256
TS_PROJ = 512
CONV_A_SHARD = CONV_W // N_DEV


def kernel(x, mem, w_in, conv_a_w, conv_a_b, ln_a_g, ln_a_b, ln_v_g, ln_v_b, w_s, b_s, w_mk, w_mv, w_out, ln1_g, ln1_b, w_up, conv_f_w, conv_f_b, w_down, ln2_g, ln2_b, loss_target, m_w_in, m_conv_a_w, m_conv_a_b, m_ln_a_g, m_ln_a_b, m_ln_v_g, m_ln_v_b, m_w_s, m_b_s, m_w_mk, m_w_mv, m_w_out, m_ln1_g, m_ln1_b, m_w_up, m_conv_f_w, m_conv_f_b, m_w_down, m_ln2_g, m_ln2_b, v_w_in, v_conv_a_w, v_conv_a_b, v_ln_a_g, v_ln_a_b, v_ln_v_g, v_ln_v_b, v_w_s, v_b_s, v_w_mk, v_w_mv, v_w_out, v_ln1_g, v_ln1_b, v_w_up, v_conv_f_w, v_conv_f_b, v_w_down, v_ln2_g, v_ln2_b):
    given = dict(locals())
    p = {nm: given[nm] for nm in WEIGHTS}
    mom_m = {nm: given["m_" + nm] for nm in WEIGHTS}
    mom_v = {nm: given["v_" + nm] for nm in WEIGHTS}
    seq = x.shape[1]
    ts_m, ts_f, ts_p = min(TS_MIXER, seq), min(TS_FFN, seq), min(TS_PROJ, seq)
    cx, cy, cc = _place()
    me = 4 * cx + 2 * cy + cc

    blobs = [_blob(p, l) for l in range(DEPTH)]
    blobs_bf = [b.astype(BF16) for b in blobs]
    conv_a_tile = jnp.pad(conv_a_w, ((0, 0), (0, HALO - CONV_K), (0, 128 - CONV_A_SHARD)))
    conv_f_tile = jnp.pad(conv_f_w, ((0, 0), (0, 8 - FFN_CONV_K), (0, 384 - FF_GROUP)))
    rows = dict(BLOCK_ROWS)
    mixer_names = ("w_in", "w_out", "w_mk", "w_mv")
    pieces = [(0, _row_off(nm), rows[nm]) for nm in mixer_names] + [(1, None, 0), (2, None, 0)]
    first = _all_gather_call([blobs_bf[0], conv_a_tile, conv_f_tile], pieces, "gather_weights")
    conv_a_all, conv_f_all = first[len(mixer_names)], first[len(mixer_names) + 1]
    conv_a = [conv_a_all[:, l, :, :CONV_A_SHARD].transpose(1, 0, 2).reshape(HALO, CONV_W) for l in range(DEPTH)]
    conv_f = [conv_f_all[:, l, :FFN_CONV_K, :FF_GROUP_PAD].transpose(1, 0, 2).reshape(FFN_CONV_K, D_FF_PAD)
              for l in range(DEPTH)]
    memq = mem[0].reshape(N_MEM, D_MODEL // MEM_FOLD, MEM_FOLD).transpose(2, 0, 1).astype(BF16)

    def matrix(gathered, nm, base):
        lo = _row_off(nm) - base
        return gathered[:, lo:lo + rows[nm]].reshape(-1, BLOB_LANES)

    ffn_base = _row_off("w_up")
    ops0 = _mixer_operands({nm: first[a].reshape(-1, BLOB_LANES) for a, nm in enumerate(mixer_names)}, conv_a[0], p, 0,
                           memq)
    (hb, z1, x1, a1), (ffn0,) = _mixer_fwd_call(x[0], ops0, ts_m,
                                            _gather_exchange(blobs_bf[0], ffn_base, rows["w_up"] + rows["w_down"]))
    ops0.update(_ffn_operands(matrix(ffn0, "w_up", ffn_base), matrix(ffn0, "w_down", ffn_base), conv_f[0], p, 0))
    (ug, uv, sl, dsl, z2, x2), (all1,) = _ffn_fwd_call(x1, ops0, ts_f, _gather_exchange(blobs_bf[1], 0, LAYER_ROWS))
    saved = [dict(x=x[0], hb=hb, z1=z1, x1=x1, a1=a1, ug=ug, uv=uv, sl=sl, dsl=dsl, z2=z2)]
    ops1 = _mixer_operands({nm: matrix(all1, nm, 0) for nm in mixer_names}, conv_a[1], p, 1, memq)
    ops1.update(_ffn_operands(matrix(all1, "w_up", 0), matrix(all1, "w_down", 0), conv_f[1], p, 1))
    (hb, z1, x1, a1), _ = _mixer_fwd_call(x2, ops1, ts_m)
    (ug, uv, sl, dsl, z2, _), _ = _ffn_fwd_call(x1, ops1, ts_f)
    saved.append(dict(x=x2, hb=hb, z1=z1, x1=x1, a1=a1, ug=ug, uv=uv, sl=sl, dsl=dsl, z2=z2))
    ops = [ops0, ops1]

    hm = _head_table()
    core_id = cc.reshape(1).astype(jnp.int32)
    slabs = [lax.empty((N_DEV, LAYER_ROWS, BLOB_LANES), F32) for _ in range(DEPTH)]
    accs = [None] * DEPTH
    from_sibling, from_chips = [None] * DEPTH, [None] * DEPTH
    dx = loss_target[0]
    loss = None
    for l in reversed(range(DEPTH)):
        s, w = saved[l], ops[l]
        last = l == DEPTH - 1
        ride = None if last else _swap_exchange(slabs[l + 1], _swap_core_copies, 4)
        (sl, dug, duv, dz2, dcf, dv2, loss_acc), got = _ffn_bwd_call(
            dx, s["z2"], s["ug"], s["uv"], s["sl"], s["dsl"], w, ts_f, last, slabs[l], _row_off("w_down"), ride)
        if last:
            loss = loss_acc[0, 0]
        else:
            from_sibling[l + 1] = got[0]
            chip_sum = _pair_add_call(slabs[l + 1], from_sibling[l + 1], core_id)
        off_up = _row_off("w_up")
        (sl, dxa), _ = _proj_bwd_call(dug, w["wgt"], s["x1"], dz2, ALPHA, ts_p, "up_gate_bwd", sl, off_up, 0, 4)
        ride = None if last else _swap_exchange(chip_sum, _swap_chip_copies, 3)
        (sl, dx1), got = _proj_bwd_call(duv, w["wvt"], s["x1"], dxa, 1.0, ts_p, "up_val_bwd", sl, off_up, 4, 4, ride)
        if not last:
            from_chips[l + 1] = got[0]
        (sl, dx, dkt, dvm, dwcat, dmsum, dva, dcw, dv1) = _mixer_bwd_call(
            dx1, s["z1"], s["hb"], s["a1"], s["x"], w, ts_m, sl, _row_off("w_out"), _row_off("w_in"))
        dkh = jnp.einsum("hd,dhm->md", hm, dkt.reshape(XATTN_W, XATTN_HEADS, N_MEM)) * (1.0 / math.sqrt(HEAD_DIM))
        dvh = jnp.einsum("hd,hmd->md", hm, dvm.reshape(XATTN_HEADS, N_MEM, XATTN_W))
        slabs[l] = _mem_proj_bwd_call(memq, dkh, dvh, sl, _row_off("w_mk"), _row_off("w_mv"))
        accs[l] = dict(dva=dva, dv1=dv1, dv2=dv2, dcf=dcf, dcw=dcw, dwcat=dwcat, dmsum=dmsum)
    grad_x = dx[None]
    from_sibling[0] = _swap_call(slabs[0], _swap_core_copies, 4, "rs_swap_core")
    chip_sum = _pair_add_call(slabs[0], from_sibling[0], core_id)
    from_chips[0] = _swap_call(chip_sum, _swap_chip_copies, 3, "rs_swap_chip")
    me_id, chip_id = me.reshape(1).astype(jnp.int32), (2 * cx + cy).reshape(1).astype(jnp.int32)
    per_layer = [_chip_add_adamw_call(slabs[l], from_sibling[l], from_chips[l], me_id, chip_id, blobs[l],
                                      _blob(mom_m, l), _blob(mom_v, l)) for l in range(DEPTH)]
    outs = [_unblob([per_layer[l][k] for l in range(DEPTH)]) for k in range(4)]

    acc_list = [accs[l][nm] for l in range(DEPTH) for nm, _ in GATHERED_ACCS]
    acc_all = _all_gather_call(acc_list, [(a, None, 0) for a in range(len(acc_list))], "gather_small_grads")
    n_acc = len(GATHERED_ACCS)
    gathered_accs = [{nm: acc_all[l * n_acc + a] for a, (nm, _) in enumerate(GATHERED_ACCS)} for l in range(DEPTH)]
    rep, dcf_sum, dcw_sum = _replicated_update_call(gathered_accs, p, mom_m, mom_v)
    for k in range(4):
        outs[k].update(rep[k])
    dcf_sum, dcw_sum = jnp.stack(dcf_sum), jnp.stack(dcw_sum)
    zero = jnp.zeros((), jnp.int32)
    g_conv_a_w = lax.dynamic_slice(dcw_sum, (zero, zero, CONV_A_SHARD * me), (DEPTH, CONV_K, CONV_A_SHARD))
    g_conv_f_w = lax.dynamic_slice(dcf_sum, (zero, zero, FF_GROUP_PAD * me), (DEPTH, FFN_CONV_K, FF_GROUP))
    g_conv_f_b = _ff_deinterleave(dcf_sum[:, FFN_CONV_K])
    conv_grads = dict(conv_a_w=g_conv_a_w, conv_f_w=g_conv_f_w, conv_f_b=g_conv_f_b)
    conv_names = tuple(conv_grads)
    upd = _adamw_whole_call([(p[nm], conv_grads[nm], mom_m[nm], mom_v[nm]) for nm in conv_names], "adamw_conv")
    for nm, (d, mn, vn) in zip(conv_names, upd):
        outs[0][nm], outs[1][nm], outs[2][nm], outs[3][nm] = conv_grads[nm], d, mn, vn

    loss = lax.psum(loss, ("x", "y", "c"))
    return (loss, grad_x, *[outs[0][nm] for nm in WEIGHTS], *[outs[1][nm] for nm in WEIGHTS],
            *[outs[2][nm] for nm in WEIGHTS], *[outs[3][nm] for nm in WEIGHTS])
```

```python
import math

import jax
import jax.numpy as jnp
from jax import lax
from jax.experimental import pallas as pl
from jax.experimental.pallas import tpu as pltpu

F32 = jnp.float32
BF16 = jnp.bfloat16

DEPTH = 2
D_MODEL = 1024
CONV_W = 384
GMLP_W = 384
XATTN_W = 256
HEAD_DIM = 64
GMLP_HEADS = 6
XATTN_HEADS = 4
IN_W = 1792
CONV_K = 31
CHUNK = 128
N_MEM = 256
D_FF = 2752
D_FF_PAD = 2816
FFN_CONV_K = 3
ALPHA = (2.0 * DEPTH) ** 0.25
LN_EPS = 1e-5
N_DEV = 8

ADAM_LR = 0.001
ADAM_B1 = 0.9
ADAM_B2 = 0.999
ADAM_EPS = 1e-08
ADAM_WD = 0.01
ADAM_STEP = 10

HALO = 32
CONV_ROWS = 32
V7X_VMEM_BYTES = 64 * 1024 * 1024
VMEM_LIMIT = V7X_VMEM_BYTES - 8 * 1024 * 1024
BLOB_LANES = 1024

MESH = pl.DeviceIdType.MESH

WEIGHTS = ("w_in", "conv_a_w", "conv_a_b", "ln_a_g", "ln_a_b", "ln_v_g", "ln_v_b", "w_s", "b_s", "w_mk", "w_mv",
           "w_out", "ln1_g", "ln1_b", "w_up", "conv_f_w", "conv_f_b", "w_down", "ln2_g", "ln2_b")


def _params(**kw):
    return pltpu.CompilerParams(vmem_limit_bytes=VMEM_LIMIT, **kw)


def _const(shape):
    nd = len(shape)
    return pl.BlockSpec(shape, lambda i: (0,) * nd, pipeline_mode=pl.Buffered(1))


def _acc(shape):
    nd = len(shape)
    return pl.BlockSpec(shape, lambda i: (0,) * nd)


class _Exchange:
    def __init__(self, arrays, out_shapes, n_copies, build):
        self.arrays, self.out_shapes, self.n_copies, self.build = list(arrays), list(out_shapes), n_copies, build


def _carry(core, n_in, n_out, exch, n_steps):
    if exch is None:
        return core
    nx_in, nx_out = len(exch.arrays), len(exch.out_shapes)

    def body(*refs):
        o0 = n_in + nx_in
        s0 = o0 + n_out + nx_out
        x_in, x_out, sems = refs[n_in:o0], refs[o0 + n_out:s0], refs[-3:]
        i = pl.program_id(0)

        @pl.when(i == 0)
        def _():
            remote, local = exch.build(x_in, x_out, *sems)
            for cp in remote + local:
                cp.start()

        core(*refs[:n_in], *refs[o0:o0 + n_out], *refs[s0:-3])

        @pl.when(i == n_steps - 1)
        def _():
            remote, local = exch.build(x_in, x_out, *sems)
            for cp in remote + local:
                cp.wait()

    return body


def _grid_call(core, name, n_steps, in_specs, out_specs, out_shape, scratch_shapes, args, aliases=None, exch=None):
    hbm = pl.BlockSpec(memory_space=pl.ANY)
    n_in, n_out = len(in_specs), len(out_specs)
    in_specs, out_specs, out_shape, scratch_shapes, args = (list(in_specs), list(out_specs), list(out_shape),
                                                            list(scratch_shapes), list(args))
    if exch is not None:
        in_specs += [hbm] * len(exch.arrays)
        out_specs += [hbm] * len(exch.out_shapes)
        out_shape += exch.out_shapes
        scratch_shapes += [pltpu.SemaphoreType.DMA((exch.n_copies,)), pltpu.SemaphoreType.DMA((exch.n_copies,)),
                           pltpu.SemaphoreType.DMA((1,))]
        args += exch.arrays
    out = pl.pallas_call(
        _carry(core, n_in, n_out, exch, n_steps), name=name, grid=(n_steps,), in_specs=in_specs, out_specs=out_specs,
        out_shape=out_shape, scratch_shapes=scratch_shapes, input_output_aliases=aliases or {},
        compiler_params=_params(dimension_semantics=("arbitrary",)))(*args)
    return list(out[:n_out]), list(out[n_out:])


def _sigmoid(x):
    return 1.0 / (1.0 + jnp.exp(-x))


_GELU_C = math.sqrt(2.0 / math.pi)


def _gelu(x):
    x2 = x * x
    t = jnp.tanh(_GELU_C * (x + 0.044715 * x * x2))
    g = 0.5 * x * (1.0 + t)
    dg = 0.5 * (1.0 + t) + 0.5 * x * (1.0 - t * t) * (_GELU_C * (1.0 + 3.0 * 0.044715 * x2))
    return g, dg


def _ln_stats(z):
    mu = jnp.mean(z, axis=-1, keepdims=True)
    zc = z - mu
    var = jnp.mean(zc * zc, axis=-1, keepdims=True)
    r = lax.rsqrt(var + LN_EPS)
    return zc * r, r


def _ln_bwd(dy, xh, r, g):
    dxh = dy * g
    m1 = jnp.mean(dxh, axis=-1, keepdims=True)
    m2 = jnp.mean(dxh * xh, axis=-1, keepdims=True)
    return r * (dxh - m1 - xh * m2)


def _rowsum(x):
    return jnp.sum(x, axis=0, keepdims=True)


def _dot(a, b):
    return jnp.dot(a, b, preferred_element_type=F32)


def _dot_tn(a, b):
    return lax.dot_general(a, b, (((0,), (0,)), ((), ())), preferred_element_type=F32)


def _dot_nt(a, b):
    return lax.dot_general(a, b, (((1,), (1,)), ((), ())), preferred_element_type=F32)


def _shift_copies(buf, sh, rows):
    for b in range(1, 8):
        sh[b - 1, 0:rows, :] = buf[b:b + rows, :]


def _window(buf, sh, start):
    b = start % 8
    a = start - b
    return buf[a:a + CONV_ROWS, :] if b == 0 else sh[b - 1, a:a + CONV_ROWS, :]


def _conv31_fwd(buf, sh, w_ref, bias, out, ts):
    for r0 in range(0, ts, CONV_ROWS):
        acc = jnp.broadcast_to(bias, (CONV_ROWS, CONV_W))
        for k in range(CONV_K):
            acc = acc + w_ref[k:k + 1, :] * _window(buf, sh, r0 + HALO - (CONV_K - 1) + k)
        out[r0:r0 + CONV_ROWS, :] = acc


def _conv31_dx(dbuf, dsh, w_ref, out, ts):
    for r0 in range(0, ts, CONV_ROWS):
        acc = jnp.zeros((CONV_ROWS, CONV_W), F32)
        for k in range(CONV_K):
            acc = acc + w_ref[k:k + 1, :] * _window(dbuf, dsh, r0 + (CONV_K - 1) - k)
        out[r0:r0 + CONV_ROWS, :] = acc


def _conv31_dw(buf, sh, dbuf, dw_ref, ts):
    for k in range(CONV_K):
        part = jnp.zeros((8, CONV_W), F32)
        for r0 in range(0, ts, CONV_ROWS):
            m = dbuf[r0:r0 + CONV_ROWS, :] * _window(buf, sh, r0 + HALO - (CONV_K - 1) + k)
            for q in range(0, CONV_ROWS, 8):
                part = part + m[q:q + 8, :]
        dw_ref[k:k + 1, :] += _rowsum(part)


def _head_mask(width, h):
    lane = lax.broadcasted_iota(jnp.int32, (CHUNK, width), 1)
    return (lane >= h * HEAD_DIM) & (lane < (h + 1) * HEAD_DIM)


def _stack_heads(vn_c):
    return jnp.concatenate([jnp.where(_head_mask(GMLP_W, h), vn_c, 0.0) for h in range(GMLP_HEADS)], axis=0)


def _group_a_fwd(hf, buf, sh, a1_ref, cw_ref, va_ref, ts, conv=True):
    ha = hf[:, 0:CONV_W]
    sg = _sigmoid(hf[:, CONV_W:2 * CONV_W])
    buf[HALO:HALO + ts, :] = ha * sg
    _shift_copies(buf, sh, ts + HALO - 8)
    if conv:
        _conv31_fwd(buf, sh, cw_ref, va_ref[0:1, :], a1_ref, ts)
    a2h, ra = _ln_stats(a1_ref[...])
    a2 = a2h * va_ref[1:2, :] + va_ref[2:3, :]
    sa = _sigmoid(a2)
    return dict(ha=ha, sg=sg, a2h=a2h, ra=ra, a2=a2, sa=sa, a=a2 * sa)


def _group_b_fwd(hf, va_ref, wcat_ref, bfull_ref, ts):
    hu = hf[:, 2 * CONV_W:2 * CONV_W + GMLP_W]
    hv = hf[:, 2 * CONV_W + GMLP_W:2 * CONV_W + 2 * GMLP_W]
    u, du = _gelu(hu)
    v, dv = _gelu(hv)
    vhat, rv = _ln_stats(v)
    vn = vhat * va_ref[3:4, :] + va_ref[4:5, :]
    stacks, mixed = [], []
    for c0 in range(0, ts, CHUNK):
        st = _stack_heads(vn[c0:c0 + CHUNK, :]).astype(BF16)
        stacks.append(st)
        mixed.append(_dot(wcat_ref[...], st) + bfull_ref[...])
    mixed = jnp.concatenate(mixed, axis=0) if len(mixed) > 1 else mixed[0]
    return dict(u=u, du=du, dv=dv, vhat=vhat, rv=rv, stacks=stacks, mixed=mixed, g=u * mixed)


GROUP_B_SAVED = ("u", "du", "dv", "vhat", "mixed")


def _group_c_fwd(hf, kt_ref, vm_ref):
    qb = hf[:, IN_W - XATTN_W:IN_W].astype(BF16)
    s_all = _dot(qb, kt_ref[...])
    ps = []
    for g in range(XATTN_HEADS):
        s = s_all[:, g * N_MEM:(g + 1) * N_MEM]
        e = jnp.exp(s - jnp.max(s, axis=-1, keepdims=True))
        ps.append(e / jnp.sum(e, axis=-1, keepdims=True))
    p_all = jnp.concatenate(ps, axis=1)
    pb = p_all.astype(BF16)
    return dict(qb=qb, p=p_all, pb=pb, c=_dot(pb, vm_ref[...]))


def _mixer_fwd_call(x, w, ts, exch=None):
    seq = x.shape[0]
    n = seq // ts

    def body(x_ref, win_ref, cw_ref, va_ref, wcat_ref, bfull_ref, kt_ref, vm_ref, wout_ref, v1_ref,
             hb_ref, z1_ref, x1_ref, a1buf, cat_ref, p_ref, gs_ref, rv_ref, buf, sh):
        i = pl.program_id(0)

        @pl.when(i == 0)
        def _():
            buf[0:HALO, :] = jnp.zeros((HALO, CONV_W), F32)

        @pl.when(i > 0)
        def _():
            buf[0:HALO, :] = buf[ts:ts + HALO, :]

        xv = x_ref[...]
        hb = _dot(xv.astype(BF16), win_ref[...]).astype(BF16)
        hb_ref[...] = hb
        hf = hb.astype(F32)
        ga = _group_a_fwd(hf, buf, sh, a1buf, cw_ref, va_ref, ts)
        gb = _group_b_fwd(hf, va_ref, wcat_ref, bfull_ref, ts)
        gc = _group_c_fwd(hf, kt_ref, vm_ref)
        cat = jnp.concatenate([ga["a"], gb["g"], gc["c"]], axis=1).astype(BF16)
        cat_ref[...] = cat
        p_ref[...] = gc["pb"]
        gs_ref[...] = jnp.concatenate([gb[k] for k in GROUP_B_SAVED], axis=1).astype(BF16)
        rv_ref[...] = jnp.broadcast_to(gb["rv"], (ts, 128))
        z1 = ALPHA * xv + _dot(cat, wout_ref[...])
        z1_ref[...] = z1
        xh, _ = _ln_stats(z1)
        x1_ref[...] = xh * v1_ref[0:1, :] + v1_ref[1:2, :]

    row = lambda width: pl.BlockSpec((ts, width), lambda i: (i, 0))
    return _grid_call(
        body, "mixer_fwd", n,
        in_specs=[row(D_MODEL), _const((D_MODEL, IN_W)), _const((HALO, CONV_W)), _const((8, CONV_W)),
                  _const((CHUNK, GMLP_HEADS * CHUNK)), _const((CHUNK, GMLP_W)), _const((XATTN_W, XATTN_HEADS * N_MEM)),
                  _const((XATTN_HEADS * N_MEM, XATTN_W)), _const((D_MODEL, D_MODEL)), _const((8, D_MODEL))],
        out_specs=[row(IN_W), row(D_MODEL), row(D_MODEL), row(CONV_W), row(D_MODEL), row(XATTN_HEADS * N_MEM),
                   row(len(GROUP_B_SAVED) * GMLP_W), row(128)],
        out_shape=[jax.ShapeDtypeStruct((seq, IN_W), BF16), jax.ShapeDtypeStruct((seq, D_MODEL), F32),
                   jax.ShapeDtypeStruct((seq, D_MODEL), F32), jax.ShapeDtypeStruct((seq, CONV_W), F32),
                   jax.ShapeDtypeStruct((seq, D_MODEL), BF16), jax.ShapeDtypeStruct((seq, XATTN_HEADS * N_MEM), BF16),
                   jax.ShapeDtypeStruct((seq, len(GROUP_B_SAVED) * GMLP_W), BF16),
                   jax.ShapeDtypeStruct((seq, 128), F32)],
        scratch_shapes=[pltpu.VMEM((ts + HALO, CONV_W), F32), pltpu.VMEM((7, ts + HALO, CONV_W), F32)],
        args=(x, w["win"], w["cw"], w["va"], w["wcat"], w["bfull"], w["kt"], w["vm"], w["wout"], w["v1"]), exch=exch)


def _store_blocks(acc, slabs_ref, sems, row_off, rows, first_block, n_blocks):
    copies = [pltpu.make_async_copy(acc.at[pl.ds(q * rows, rows)], slabs_ref.at[first_block + q, pl.ds(row_off, rows)],
                                    sems.at[q]) for q in range(n_blocks)]
    for cp in copies:
        cp.start()
    for cp in copies:
        cp.wait()


def _mixer_bwd_call(dx1, z1, hb, a1, cat, p, gs, rv, x, w, ts, slabs, off_out, off_in):
    seq = dx1.shape[0]
    n = seq // ts
    halo_blocks = ts // HALO

    def body(slabs_in, dx1_ref, z1_ref, hb_ref, hprev_ref, a1_ref, cat_ref, p_ref, gs_ref, rv_ref, x_ref, cw_ref, va_ref,
             wcatt_ref, ktt_ref, vmt_ref, woutt_ref, wint_ref, v1_ref,
             slabs_ref, dx_ref, dkt_ref, dvm_ref, dwcat_ref, dmsum_ref, dva_ref, dcw_ref, dv1_ref,
             buf, dbuf, da0buf, dwout_ref, dwin_ref, sems, sh, dsh):
        i = pl.program_id(0)

        @pl.when(i == 0)
        def _():
            for ref in (dwout_ref, dwin_ref, dkt_ref, dvm_ref, dwcat_ref, dmsum_ref, dva_ref, dcw_ref, dv1_ref):
                ref[...] = jnp.zeros(ref.shape, F32)
            dbuf[ts:ts + HALO, :] = jnp.zeros((HALO, CONV_W), F32)

        @pl.when(i > 0)
        def _():
            dbuf[ts:ts + HALO, :] = dbuf[0:HALO, :]

        dx1v = dx1_ref[...]
        xh1, r1 = _ln_stats(z1_ref[...])
        dv1_ref[0:1, :] += _rowsum(dx1v * xh1)
        dv1_ref[1:2, :] += _rowsum(dx1v)
        dz1 = _ln_bwd(dx1v, xh1, r1, v1_ref[0:1, :])
        dmix = dz1.astype(BF16)

        hf = hb_ref[...].astype(F32)
        hp = hprev_ref[...].astype(F32)
        a0p = hp[:, 0:CONV_W] * _sigmoid(hp[:, CONV_W:2 * CONV_W])
        buf[0:HALO, :] = jnp.where(i == n - 1, 0.0, a0p)
        ga = _group_a_fwd(hf, buf, sh, a1_ref, cw_ref, va_ref, ts, conv=False)
        gb = {k: gs_ref[:, j * GMLP_W:(j + 1) * GMLP_W].astype(F32) for j, k in enumerate(GROUP_B_SAVED)}
        vn = gb["vhat"] * va_ref[3:4, :] + va_ref[4:5, :]
        pb = p_ref[...]
        gc = dict(qb=hf[:, IN_W - XATTN_W:IN_W].astype(BF16), pb=pb, p=pb.astype(F32))

        dwout_ref[...] += _dot_tn(cat_ref[...], dmix)
        dcat = _dot(dmix, woutt_ref[...])
        da = dcat[:, 0:CONV_W]
        dg = dcat[:, CONV_W:CONV_W + GMLP_W]
        dc = dcat[:, CONV_W + GMLP_W:D_MODEL].astype(BF16)

        dp = _dot(dc, vmt_ref[...])
        dvm_ref[...] += _dot_tn(gc["pb"], dc)
        dss = []
        for g in range(XATTN_HEADS):
            sl = slice(g * N_MEM, (g + 1) * N_MEM)
            pg = gc["p"][:, sl]
            dpg = dp[:, sl]
            dss.append(pg * (dpg - jnp.sum(dpg * pg, axis=-1, keepdims=True)))
        ds = jnp.concatenate(dss, axis=1).astype(BF16)
        dq = _dot(ds, ktt_ref[...])
        dkt_ref[...] += _dot_tn(gc["qb"], ds)

        dmixed = dg * gb["u"]
        dhu = dg * gb["mixed"] * gb["du"]
        dvns = []
        for j, c0 in enumerate(range(0, ts, CHUNK)):
            dm = dmixed[c0:c0 + CHUNK, :]
            dmb = dm.astype(BF16)
            dmsum_ref[...] += dm
            dwcat_ref[...] += _dot_nt(dmb, _stack_heads(vn[c0:c0 + CHUNK, :]).astype(BF16))
            dst = _dot(wcatt_ref[...], dmb)
            dvn_c = jnp.zeros((CHUNK, GMLP_W), F32)
            for h in range(GMLP_HEADS):
                dvn_c = dvn_c + jnp.where(_head_mask(GMLP_W, h), dst[h * CHUNK:(h + 1) * CHUNK, :], 0.0)
            dvns.append(dvn_c)
        dvn = jnp.concatenate(dvns, axis=0) if len(dvns) > 1 else dvns[0]
        dva_ref[3:4, :] += _rowsum(dvn * gb["vhat"])
        dva_ref[4:5, :] += _rowsum(dvn)
        dhv = _ln_bwd(dvn, gb["vhat"], rv_ref[:, 0:1], va_ref[3:4, :]) * gb["dv"]

        a2, sa = ga["a2"], ga["sa"]
        da2 = da * (sa * (1.0 + a2 * (1.0 - sa)))
        dva_ref[1:2, :] += _rowsum(da2 * ga["a2h"])
        dva_ref[2:3, :] += _rowsum(da2)
        da1 = _ln_bwd(da2, ga["a2h"], ga["ra"], va_ref[1:2, :])
        dva_ref[0:1, :] += _rowsum(da1)
        dbuf[0:ts, :] = da1
        _shift_copies(dbuf, dsh, ts + HALO - 8)
        _conv31_dw(buf, sh, dbuf, dcw_ref, ts)
        _conv31_dx(dbuf, dsh, cw_ref, da0buf, ts)
        da0 = da0buf[...]
        sg = ga["sg"]
        dha = da0 * sg
        dhg = da0 * ga["ha"] * sg * (1.0 - sg)

        dh = jnp.concatenate([dha, dhg, dhu, dhv, dq], axis=1).astype(BF16)
        dx_ref[...] = _dot(dh, wint_ref[...]) + ALPHA * dz1
        dwin_ref[...] += _dot_tn(dh, x_ref[...].astype(BF16))

        @pl.when(i == n - 1)
        def _():
            _store_blocks(dwout_ref, slabs_ref, sems, off_out, D_MODEL // N_DEV, 0, N_DEV)
            _store_blocks(dwin_ref, slabs_ref, sems, off_in, IN_W // N_DEV, 0, N_DEV)

    rev = lambda width: pl.BlockSpec((ts, width), lambda i: (n - 1 - i, 0))
    prev = pl.BlockSpec((HALO, 2 * CONV_W), lambda i: (jnp.maximum((n - 1 - i) * halo_blocks - 1, 0), 0))
    hbm = pl.BlockSpec(memory_space=pl.ANY)
    hc = GMLP_HEADS * CHUNK
    am = XATTN_HEADS * N_MEM
    return pl.pallas_call(
        body, name="mixer_bwd", grid=(n,),
        in_specs=[hbm, rev(D_MODEL), rev(D_MODEL), rev(IN_W), prev, rev(CONV_W), rev(D_MODEL), rev(am),
                  rev(len(GROUP_B_SAVED) * GMLP_W), rev(128), rev(D_MODEL), _const((HALO, CONV_W)),
                  _const((8, CONV_W)), _const((hc, CHUNK)), _const((am, XATTN_W)), _const((XATTN_W, am)),
                  _const((D_MODEL, D_MODEL)), _const((IN_W, D_MODEL)), _const((8, D_MODEL))],
        out_specs=[hbm, rev(D_MODEL), _acc((XATTN_W, am)), _acc((am, XATTN_W)),
                   _acc((CHUNK, hc)), _acc((CHUNK, GMLP_W)), _acc((8, CONV_W)), _acc((HALO, CONV_W)),
                   _acc((8, D_MODEL))],
        out_shape=[jax.ShapeDtypeStruct(slabs.shape, F32), jax.ShapeDtypeStruct((seq, D_MODEL), F32),
                   jax.ShapeDtypeStruct((XATTN_W, am), F32),
                   jax.ShapeDtypeStruct((am, XATTN_W), F32), jax.ShapeDtypeStruct((CHUNK, hc), F32),
                   jax.ShapeDtypeStruct((CHUNK, GMLP_W), F32), jax.ShapeDtypeStruct((8, CONV_W), F32),
                   jax.ShapeDtypeStruct((HALO, CONV_W), F32), jax.ShapeDtypeStruct((8, D_MODEL), F32)],
        scratch_shapes=[pltpu.VMEM((ts + HALO, CONV_W), F32),
                        pltpu.VMEM((ts + HALO, CONV_W), F32), pltpu.VMEM((ts, CONV_W), F32),
                        pltpu.VMEM((D_MODEL, D_MODEL), F32), pltpu.VMEM((IN_W, D_MODEL), F32),
                        pltpu.SemaphoreType.DMA((N_DEV,)),
                        pltpu.VMEM((7, ts + HALO, CONV_W), F32), pltpu.VMEM((7, ts + HALO, CONV_W), F32)],
        input_output_aliases={0: 0},
        compiler_params=_params(dimension_semantics=("arbitrary",)),
    )(slabs, dx1, z1, hb, hb, a1, cat, p, gs, rv, x, w["cw"], w["va"], w["wcatt"], w["ktt"], w["vmt"], w["woutt"],
      w["wint"], w["v1"])


FFN_HALO = 8
FF_GROUP = D_FF // N_DEV
FF_GROUP_PAD = D_FF_PAD // N_DEV


def _ffn_taps(ubuf, ts, lo, hi):
    return tuple(ubuf[FFN_HALO - (FFN_CONV_K - 1) + k:FFN_HALO - (FFN_CONV_K - 1) + k + ts, lo:hi]
                 for k in range(FFN_CONV_K))


def _ffn_gate(ubuf, cf_ref, ts, lo, hi):
    taps = _ffn_taps(ubuf, ts, lo, hi)
    g = cf_ref[3:4, lo:hi] + cf_ref[2:3, lo:hi] * taps[2]
    g = g + cf_ref[1:2, lo:hi] * taps[1]
    return g + cf_ref[0:1, lo:hi] * taps[0]


FFN_LANE_CHUNKS = ((0, D_FF_PAD // 2), (D_FF_PAD // 2, D_FF_PAD))
FFN_BWD_CHUNK = 256
FFN_BWD_CHUNKS = tuple((lo, lo + FFN_BWD_CHUNK) for lo in range(0, D_FF_PAD, FFN_BWD_CHUNK))


def _ffn_fwd_call(x1, w, ts, exch=None):
    seq = x1.shape[0]
    n = seq // ts

    def body(x1_ref, wg_ref, wv_ref, cf_ref, wdown_ref, v2_ref, ug_ref, uv_ref, sl_ref, dsl_ref, z2_ref, x2_ref, ubuf):
        i = pl.program_id(0)

        @pl.when(i == 0)
        def _():
            ubuf[0:FFN_HALO, :] = jnp.zeros((FFN_HALO, D_FF_PAD), F32)

        @pl.when(i > 0)
        def _():
            ubuf[0:FFN_HALO, :] = ubuf[ts:ts + FFN_HALO, :]

        xv = x1_ref[...]
        xb = xv.astype(BF16)
        y = ALPHA * xv
        for lo, hi in FFN_LANE_CHUNKS:
            ug = _dot(xb, wg_ref[:, lo:hi]).astype(BF16)
            uv = _dot(xb, wv_ref[:, lo:hi]).astype(BF16)
            ug_ref[:, lo:hi] = ug
            uv_ref[:, lo:hi] = uv
            ubuf[FFN_HALO:FFN_HALO + ts, lo:hi] = ug.astype(F32)
            gate = _ffn_gate(ubuf, cf_ref, ts, lo, hi)
            sg = _sigmoid(gate)
            sl = gate * sg
            sl_ref[:, lo:hi] = sl.astype(BF16)
            dsl_ref[:, lo:hi] = (sg * (1.0 + gate * (1.0 - sg))).astype(BF16)
            act = (sl * uv.astype(F32)).astype(BF16)
            y = y + _dot(act, wdown_ref[lo:hi, :])
        z2_ref[...] = y
        xh, _ = _ln_stats(y)
        x2_ref[...] = xh * v2_ref[0:1, :] + v2_ref[1:2, :]

    row = lambda width: pl.BlockSpec((ts, width), lambda i: (i, 0))
    return _grid_call(
        body, "ffn_fwd", n,
        in_specs=[row(D_MODEL), _const((D_MODEL, D_FF_PAD)), _const((D_MODEL, D_FF_PAD)), _const((8, D_FF_PAD)),
                  _const((D_FF_PAD, D_MODEL)), _const((8, D_MODEL))],
        out_specs=[row(D_FF_PAD)] * 4 + [row(D_MODEL), row(D_MODEL)],
        out_shape=[jax.ShapeDtypeStruct((seq, D_FF_PAD), BF16)] * 4 + [jax.ShapeDtypeStruct((seq, D_MODEL), F32)] * 2,
        scratch_shapes=[pltpu.VMEM((ts + FFN_HALO, D_FF_PAD), F32)],
        args=(x1, w["wg"], w["wv"], w["cf"], w["wdown"], w["v2"]), exch=exch)


def _ffn_bwd_call(dx2_or_target, z2, ug, uv, sl, dsl, w, ts, last, slabs, row_off, exch=None):
    seq = z2.shape[0]
    n = seq // ts
    halo_blocks = ts // 16

    def body(slabs_in, dx2_ref, z2_ref, ug_ref, uv_ref, sl_ref, dsl_ref, uprev_ref, cf_ref, wdownt_ref, v2_ref,
             slabs_ref, dug_ref, duv_ref, dz2_ref, dcf_ref, dv2_ref, loss_ref,
             ubuf, dgbuf, dwacc, sems):
        i = pl.program_id(0)

        @pl.when(i == 0)
        def _():
            dwacc[...] = jnp.zeros(dwacc.shape, F32)
            dcf_ref[...] = jnp.zeros(dcf_ref.shape, F32)
            dv2_ref[...] = jnp.zeros(dv2_ref.shape, F32)
            loss_ref[...] = jnp.zeros(loss_ref.shape, F32)
            dgbuf[ts:ts + FFN_HALO, :] = jnp.zeros((FFN_HALO, D_FF_PAD), F32)

        @pl.when(i > 0)
        def _():
            dgbuf[ts:ts + FFN_HALO, :] = dgbuf[0:FFN_HALO, :]

        xh2, r2 = _ln_stats(z2_ref[...])
        if last:
            diff = xh2 * v2_ref[0:1, :] + v2_ref[1:2, :] - dx2_ref[...]
            loss_ref[...] += jnp.sum(diff * diff) * (0.5 / D_MODEL)
            dx2v = diff * (1.0 / D_MODEL)
        else:
            dx2v = dx2_ref[...]
        dv2_ref[0:1, :] += _rowsum(dx2v * xh2)
        dv2_ref[1:2, :] += _rowsum(dx2v)
        dz2 = _ln_bwd(dx2v, xh2, r2, v2_ref[0:1, :])
        dz2_ref[...] = dz2
        dy = dz2.astype(BF16)

        up = uprev_ref[...].astype(F32)[8:16, :]
        ubuf[0:FFN_HALO, :] = jnp.where(i == n - 1, 0.0, up)
        ubuf[FFN_HALO:FFN_HALO + ts, :] = ug_ref[...].astype(F32)
        for lo, hi in FFN_BWD_CHUNKS:
            taps = _ffn_taps(ubuf, ts, lo, hi)
            sl = sl_ref[:, lo:hi].astype(F32)
            uvf = uv_ref[:, lo:hi].astype(F32)
            act = (sl * uvf).astype(BF16)
            dwacc[lo:hi, :] += _dot_tn(act, dy)
            dact = _dot(dy, wdownt_ref[:, lo:hi])
            duv_ref[:, lo:hi] = (dact * sl).astype(BF16)
            dgate = dact * uvf * dsl_ref[:, lo:hi].astype(F32)
            dgbuf[0:ts, lo:hi] = dgate
            dcf_ref[3:4, lo:hi] += _rowsum(dgate)
            for k in range(FFN_CONV_K):
                dcf_ref[k:k + 1, lo:hi] += _rowsum(dgate * taps[k])
            dug = cf_ref[2:3, lo:hi] * dgate + cf_ref[1:2, lo:hi] * dgbuf[1:1 + ts, lo:hi]
            dug = dug + cf_ref[0:1, lo:hi] * dgbuf[2:2 + ts, lo:hi]
            dug_ref[:, lo:hi] = dug.astype(BF16)

        @pl.when(i == n - 1)
        def _():
            _store_blocks(dwacc, slabs_ref, sems, row_off, D_FF_PAD // N_DEV, 0, N_DEV)

    rev = lambda width: pl.BlockSpec((ts, width), lambda i: (n - 1 - i, 0))
    prev = pl.BlockSpec((16, D_FF_PAD), lambda i: (jnp.maximum((n - 1 - i) * halo_blocks - 1, 0), 0))
    hbm = pl.BlockSpec(memory_space=pl.ANY)
    return _grid_call(
        body, "ffn_bwd_last" if last else "ffn_bwd", n,
        in_specs=[hbm, rev(D_MODEL), rev(D_MODEL)] + [rev(D_FF_PAD)] * 4 + [prev, _const((8, D_FF_PAD)),
                                                                           _const((D_MODEL, D_FF_PAD)), _const((8, D_MODEL))],
        out_specs=[hbm, rev(D_FF_PAD), rev(D_FF_PAD), rev(D_MODEL),
                   _acc((8, D_FF_PAD)), _acc((8, D_MODEL)), _acc((8, 128))],
        out_shape=[jax.ShapeDtypeStruct(slabs.shape, F32),
                   jax.ShapeDtypeStruct((seq, D_FF_PAD), BF16), jax.ShapeDtypeStruct((seq, D_FF_PAD), BF16),
                   jax.ShapeDtypeStruct((seq, D_MODEL), F32),
                   jax.ShapeDtypeStruct((8, D_FF_PAD), F32), jax.ShapeDtypeStruct((8, D_MODEL), F32),
                   jax.ShapeDtypeStruct((8, 128), F32)],
        scratch_shapes=[pltpu.VMEM((ts + FFN_HALO, D_FF_PAD), F32), pltpu.VMEM((ts + FFN_HALO, D_FF_PAD), F32),
                        pltpu.VMEM((D_FF_PAD, D_MODEL), F32), pltpu.SemaphoreType.DMA((N_DEV,))],
        args=(slabs, dx2_or_target, z2, ug, uv, sl, dsl, ug, w["cf"], w["wdownt"], w["v2"]), aliases={0: 0}, exch=exch)


def _proj_bwd_call(d, wt, xin, addend, scale, ts, name, slabs, row_off, first_block, n_blocks, exch=None):
    seq, k = d.shape
    n = seq // ts

    def body(slabs_in, d_ref, wt_ref, xin_ref, add_ref, slabs_ref, dx_ref, acc, sems):
        i = pl.program_id(0)

        @pl.when(i == 0)
        def _():
            acc[...] = jnp.zeros(acc.shape, F32)

        dv = d_ref[...]
        dx_ref[...] = _dot(dv, wt_ref[...]) + scale * add_ref[...]
        acc[...] += _dot_tn(dv, xin_ref[...].astype(BF16))

        @pl.when(i == n - 1)
        def _():
            _store_blocks(acc, slabs_ref, sems, row_off, k // n_blocks, first_block, n_blocks)

    row = lambda width: pl.BlockSpec((ts, width), lambda i: (i, 0))
    hbm = pl.BlockSpec(memory_space=pl.ANY)
    return _grid_call(
        body, name, n,
        in_specs=[hbm, row(k), _const((k, D_MODEL)), row(D_MODEL), row(D_MODEL)],
        out_specs=[hbm, row(D_MODEL)],
        out_shape=[jax.ShapeDtypeStruct(slabs.shape, F32), jax.ShapeDtypeStruct((seq, D_MODEL), F32)],
        scratch_shapes=[pltpu.VMEM((k, D_MODEL), F32), pltpu.SemaphoreType.DMA((n_blocks,))],
        args=(slabs, d, wt, xin, addend), aliases={0: 0}, exch=exch)


MEM_FOLD = BLOB_LANES // XATTN_W


def _mem_proj_call(memq, wk_flat, wv_flat):
    def body(memq_ref, wk_ref, wv_ref, kh_ref, vh_ref):
        for w_ref, o_ref in ((wk_ref, kh_ref), (wv_ref, vh_ref)):
            acc = jnp.zeros((N_MEM, XATTN_W), F32)
            for q in range(MEM_FOLD):
                acc = acc + _dot(memq_ref[q], w_ref[:, q * XATTN_W:(q + 1) * XATTN_W])
            o_ref[...] = acc

    out = jax.ShapeDtypeStruct((N_MEM, XATTN_W), F32)
    return pl.pallas_call(body, name="mem_proj", out_shape=[out, out], compiler_params=_params())(memq, wk_flat, wv_flat)


def _mem_proj_bwd_call(memq, dkh, dvh, slabs, off_k, off_v):
    rows = D_MODEL // MEM_FOLD

    def body(slabs_in, memq_ref, dkh_ref, dvh_ref, slabs_ref, acc, sems):
        for d_ref, off in ((dkh_ref, off_k), (dvh_ref, off_v)):
            dv = d_ref[...].astype(BF16)
            for q in range(MEM_FOLD):
                acc[:, q * XATTN_W:(q + 1) * XATTN_W] = _dot_tn(memq_ref[q], dv)
            _store_blocks(acc, slabs_ref, sems, off, rows // N_DEV, 0, N_DEV)

    hbm = pl.BlockSpec(memory_space=pl.ANY)
    vmem = pl.BlockSpec(memory_space=pltpu.VMEM)
    return pl.pallas_call(
        body, name="mem_proj_bwd", in_specs=[hbm, vmem, vmem, vmem], out_specs=hbm,
        out_shape=jax.ShapeDtypeStruct(slabs.shape, F32),
        scratch_shapes=[pltpu.VMEM((rows, BLOB_LANES), F32), pltpu.SemaphoreType.DMA((N_DEV,))],
        input_output_aliases={0: 0}, compiler_params=_params(),
    )(slabs, memq, dkh, dvh)


def _place():
    return lax.axis_index("x"), lax.axis_index("y"), lax.axis_index("c")


def _all_gather_call(arrs, pieces, name):
    n_in, n_p = len(arrs), len(pieces)

    def body(*refs):
        ins, outs = refs[:n_in], refs[n_in:n_in + n_p]
        send_sems, recv_sems, local_sems = refs[n_in + n_p:]
        x, y, c = _place()
        me, sibling = (x, y, c), (x, y, 1 - c)
        chips = [(1 - x, y), (x, 1 - y), (1 - x, 1 - y)]

        def src(a):
            idx, r0, rows = pieces[a]
            return ins[idx] if r0 is None else ins[idx].at[pl.ds(r0, rows)]

        def slab(a, p):
            return outs[a].at[4 * p[0] + 2 * p[1] + p[2]]

        def copy(a, k, block, to, own=False):
            return pltpu.make_async_remote_copy(
                src_ref=src(a) if own else slab(a, block), dst_ref=slab(a, block),
                send_sem=send_sems.at[a, k], recv_sem=recv_sems.at[a, k], device_id=to, device_id_type=MESH)

        mine = [pltpu.make_async_copy(src(a), slab(a, me), local_sems.at[a]) for a in range(n_p)]
        for cp in mine:
            cp.start()
        first = []
        for a in range(n_p):
            first.append(copy(a, 0, me, sibling, own=True))
            first += [copy(a, 1 + j, me, (*chip, c), own=True) for j, chip in enumerate(chips)]
        for cp in first:
            cp.start()
        passed = []
        for a in range(n_p):
            for j, chip in enumerate(chips):
                copy(a, 1 + j, (*chip, c), me).wait_recv()
                cp = copy(a, 4 + j, (*chip, c), sibling)
                cp.start()
                passed.append(cp)
        for a in range(n_p):
            copy(a, 0, sibling, me).wait_recv()
            for j, chip in enumerate(chips):
                copy(a, 4 + j, (*chip, 1 - c), me).wait_recv()
        for cp in first + passed:
            cp.wait_send()
        for cp in mine:
            cp.wait()

    def out_shape(piece):
        idx, r0, rows = piece
        a = arrs[idx]
        return jax.ShapeDtypeStruct((N_DEV,) + (a.shape if r0 is None else (rows,) + a.shape[1:]), a.dtype)

    hbm = pl.BlockSpec(memory_space=pl.ANY)
    return pl.pallas_call(
        body, name=name,
        in_specs=[hbm] * n_in, out_specs=[hbm] * n_p, out_shape=[out_shape(p) for p in pieces],
        scratch_shapes=[pltpu.SemaphoreType.DMA((n_p, 7)), pltpu.SemaphoreType.DMA((n_p, 7)),
                        pltpu.SemaphoreType.DMA((n_p,))],
    )(*arrs)


def _flip(v, f):
    return 1 - v if f else v


def _gather_exchange(blob, r0, rows):
    def build(ins, outs, send_sems, recv_sems, local_sems):
        x, y, c = _place()
        src = ins[0].at[pl.ds(r0, rows)]
        dst = outs[0].at[4 * x + 2 * y + c]
        flips = [(fx, fy, fc) for fx in (0, 1) for fy in (0, 1) for fc in (0, 1) if fx or fy or fc]
        remote = [pltpu.make_async_remote_copy(
            src_ref=src, dst_ref=dst, send_sem=send_sems.at[k], recv_sem=recv_sems.at[k],
            device_id=(_flip(x, fx), _flip(y, fy), _flip(c, fc)), device_id_type=MESH)
            for k, (fx, fy, fc) in enumerate(flips)]
        return remote, [pltpu.make_async_copy(src, dst, local_sems.at[0])]

    return _Exchange([blob], [jax.ShapeDtypeStruct((N_DEV, rows) + blob.shape[1:], blob.dtype)], N_DEV - 1, build)


def _swap_core_copies(g_ref, r_ref, send_sems, recv_sems):
    x, y, c = _place()
    return [pltpu.make_async_remote_copy(
        src_ref=g_ref.at[2 * k + (1 - c)], dst_ref=r_ref.at[k], send_sem=send_sems.at[k], recv_sem=recv_sems.at[k],
        device_id=(x, y, 1 - c), device_id_type=MESH) for k in range(4)]


def _swap_chip_copies(p_ref, r_ref, send_sems, recv_sems):
    x, y, c = _place()
    chips = [(1 - x, y), (x, 1 - y), (1 - x, 1 - y)]
    return [pltpu.make_async_remote_copy(
        src_ref=p_ref.at[2 * px + py], dst_ref=r_ref.at[j], send_sem=send_sems.at[j], recv_sem=recv_sems.at[j],
        device_id=(px, py, c), device_id_type=MESH) for j, (px, py) in enumerate(chips)]


def _swap_exchange(a, copies, n):
    return _Exchange([a], [jax.ShapeDtypeStruct((n,) + a.shape[1:], a.dtype)], n,
                     lambda ins, outs, send_sems, recv_sems, local_sems: (copies(ins[0], outs[0], send_sems, recv_sems), []))


def _swap_call(a, copies, n, name):
    def body(a_ref, r_ref, send_sems, recv_sems):
        cps = copies(a_ref, r_ref, send_sems, recv_sems)
        for cp in cps:
            cp.start()
        for cp in cps:
            cp.wait()

    hbm = pl.BlockSpec(memory_space=pl.ANY)
    return pl.pallas_call(
        body, name=name, in_specs=[hbm], out_specs=hbm, out_shape=jax.ShapeDtypeStruct((n,) + a.shape[1:], a.dtype),
        scratch_shapes=[pltpu.SemaphoreType.DMA((n,)), pltpu.SemaphoreType.DMA((n,))],
    )(a)


ADD_ROWS = 184


def _pair_add_call(g, r, c):
    _, rows, width = g.shape

    def body(c_ref, g_ref, r_ref, o_ref):
        o_ref[...] = (g_ref[...] + r_ref[...]).astype(BF16)

    return pl.pallas_call(
        body, name="rs_pair_add",
        grid_spec=pltpu.PrefetchScalarGridSpec(
            num_scalar_prefetch=1, grid=(4, rows // ADD_ROWS),
            in_specs=[pl.BlockSpec((None, ADD_ROWS, width), lambda k, i, c_ref: (2 * k + c_ref[0], i, 0)),
                      pl.BlockSpec((None, ADD_ROWS, width), lambda k, i, c_ref: (k, i, 0))],
            out_specs=pl.BlockSpec((None, ADD_ROWS, width), lambda k, i, c_ref: (k, i, 0))),
        out_shape=jax.ShapeDtypeStruct((4, rows, width), BF16),
        compiler_params=_params(dimension_semantics=("arbitrary", "arbitrary")),
    )(c, g, r)


def _adam(w, g, m, v):
    mn = ADAM_B1 * m + (1.0 - ADAM_B1) * g
    vn = ADAM_B2 * v + (1.0 - ADAM_B2) * (g * g)
    m_hat = mn / (1.0 - ADAM_B1 ** ADAM_STEP)
    v_hat = vn / (1.0 - ADAM_B2 ** ADAM_STEP)
    return -ADAM_LR * (m_hat / (jnp.sqrt(v_hat) + ADAM_EPS) + ADAM_WD * w), mn, vn


def _chip_add_adamw_call(slabs, from_sibling, from_chips, me, chip, w, m, v):
    _, rows, width = slabs.shape

    def body(me_ref, chip_ref, own_ref, sib_ref, r_ref, w_ref, m_ref, v_ref, g_ref, d_ref, mo_ref, vo_ref):
        g = own_ref[...] + sib_ref[...]
        for j in range(3):
            g = g + r_ref[j].astype(F32)
        g_ref[...] = g
        d_ref[...], mo_ref[...], vo_ref[...] = _adam(w_ref[...], g, m_ref[...], v_ref[...])

    spec = pl.BlockSpec((ADD_ROWS, width), lambda i, me_ref, chip_ref: (i, 0))
    return pl.pallas_call(
        body, name="rs_chip_add_adamw",
        grid_spec=pltpu.PrefetchScalarGridSpec(
            num_scalar_prefetch=2, grid=(rows // ADD_ROWS,),
            in_specs=[pl.BlockSpec((None, ADD_ROWS, width), lambda i, me_ref, chip_ref: (me_ref[0], i, 0)),
                      pl.BlockSpec((None, ADD_ROWS, width), lambda i, me_ref, chip_ref: (chip_ref[0], i, 0)),
                      pl.BlockSpec((3, ADD_ROWS, width), lambda i, me_ref, chip_ref: (0, i, 0)), spec, spec, spec],
            out_specs=[spec] * 4),
        out_shape=[jax.ShapeDtypeStruct((rows, width), F32)] * 4,
        compiler_params=_params(dimension_semantics=("arbitrary",)),
    )(me, chip, slabs, from_sibling, from_chips, w, m, v)


def _adamw_whole_call(params, name):
    n = len(params)

    def body(*refs):
        ins, outs = refs[:4 * n], refs[4 * n:]
        for a in range(n):
            w_ref, g_ref, m_ref, v_ref = ins[4 * a:4 * a + 4]
            outs[3 * a][...], outs[3 * a + 1][...], outs[3 * a + 2][...] = _adam(w_ref[...], g_ref[...], m_ref[...],
                                                                              v_ref[...])

    flat = [a for p in params for a in p]
    shapes = [jax.ShapeDtypeStruct(p[0].shape, F32) for p in params for _ in range(3)]
    out = pl.pallas_call(body, name=name, out_shape=shapes, compiler_params=_params())(*flat)
    return [tuple(out[3 * a:3 * a + 3]) for a in range(n)]


GATHERED_ACCS = (("dva", (8, CONV_W)), ("dv1", (8, D_MODEL)), ("dv2", (8, D_MODEL)), ("dcf", (8, D_FF_PAD)),
                 ("dcw", (HALO, CONV_W)), ("dwcat", (CHUNK, GMLP_HEADS * CHUNK)), ("dmsum", (CHUNK, GMLP_W)))
VEC_A = ("conv_a_b", "ln_a_g", "ln_a_b", "ln_v_g", "ln_v_b")
REP_IN_KERNEL = VEC_A + ("ln1_g", "ln1_b", "ln2_g", "ln2_b", "w_s", "b_s")


def _replicated_update_call(gathered, p, mom_m, mom_v):
    n_acc = len(GATHERED_ACCS)
    n_rep = len(REP_IN_KERNEL)

    def body(*refs):
        acc_refs = refs[:DEPTH * n_acc]
        wmv = refs[DEPTH * n_acc:DEPTH * n_acc + 3 * n_rep]
        outs = refs[DEPTH * n_acc + 3 * n_rep:]
        out_par = {nm: outs[4 * a:4 * a + 4] for a, nm in enumerate(REP_IN_KERNEL)}
        out_dcf = outs[4 * n_rep:4 * n_rep + DEPTH]
        out_dcw = outs[4 * n_rep + DEPTH:4 * n_rep + 2 * DEPTH]
        par = {nm: wmv[3 * a:3 * a + 3] for a, nm in enumerate(REP_IN_KERNEL)}
        tril = (lax.broadcasted_iota(jnp.int32, (CHUNK, CHUNK), 0) >= lax.broadcasted_iota(jnp.int32, (CHUNK, CHUNK), 1))
        head = lax.broadcasted_iota(jnp.int32, (8, GMLP_W), 0) * HEAD_DIM
        lane = lax.broadcasted_iota(jnp.int32, (8, GMLP_W), 1)
        sel = jnp.where((lane >= head) & (lane < head + HEAD_DIM), 1.0, 0.0)

        def update(nm, idx, g):
            w_ref, m_ref, v_ref = par[nm]
            d, mn, vn = _adam(w_ref[idx], g, m_ref[idx], v_ref[idx])
            g_ref, d_ref, mo_ref, vo_ref = out_par[nm]
            g_ref[idx] = g
            d_ref[idx] = d
            mo_ref[idx] = mn
            vo_ref[idx] = vn

        for l in range(DEPTH):
            tot = {}
            for a, (nm, _) in enumerate(GATHERED_ACCS):
                ref = acc_refs[l * n_acc + a]
                s = ref[0]
                for j in range(1, N_DEV):
                    s = s + ref[j]
                tot[nm] = s
            out_dcf[l][...] = tot["dcf"]
            out_dcw[l][...] = tot["dcw"]
            row = (slice(l, l + 1), slice(None))
            for k, nm in enumerate(VEC_A):
                update(nm, row, tot["dva"][k:k + 1, :])
            update("ln1_g", row, tot["dv1"][0:1, :])
            update("ln1_b", row, tot["dv1"][1:2, :])
            update("ln2_g", row, tot["dv2"][0:1, :])
            update("ln2_b", row, tot["dv2"][1:2, :])
            for h in range(GMLP_HEADS):
                gw = jnp.where(tril, tot["dwcat"][:, h * CHUNK:(h + 1) * CHUNK], 0.0)
                update("w_s", (l, h), gw)
            gb = lax.dot_general(sel, tot["dmsum"], (((1,), (1,)), ((), ())), precision=lax.Precision.HIGHEST,
                                 preferred_element_type=F32)
            for h in range(GMLP_HEADS):
                update("b_s", (l, slice(h, h + 1), slice(None)), gb[h:h + 1, :])

    ins = [gathered[l][nm] for l in range(DEPTH) for nm, _ in GATHERED_ACCS]
    ins += [t[nm] for nm in REP_IN_KERNEL for t in (p, mom_m, mom_v)]
    shapes = [jax.ShapeDtypeStruct(p[nm].shape, F32) for nm in REP_IN_KERNEL for _ in range(4)]
    shapes += [jax.ShapeDtypeStruct((8, D_FF_PAD), F32)] * DEPTH + [jax.ShapeDtypeStruct((HALO, CONV_W), F32)] * DEPTH
    out = pl.pallas_call(body, name="replicated_update", out_shape=shapes, compiler_params=_params())(*ins)
    res = [{nm: out[4 * a + k] for a, nm in enumerate(REP_IN_KERNEL)} for k in range(4)]
    return res, out[4 * n_rep:4 * n_rep + DEPTH], out[4 * n_rep + DEPTH:]


BLOCK_ROWS = (("w_in", IN_W // N_DEV), ("w_out", D_MODEL // N_DEV), ("w_up", 2 * FF_GROUP_PAD),
              ("w_down", FF_GROUP_PAD), ("w_mk", D_MODEL // N_DEV // MEM_FOLD), ("w_mv", D_MODEL // N_DEV // MEM_FOLD))
LAYER_ROWS = sum(r for _, r in BLOCK_ROWS)
assert LAYER_ROWS % ADD_ROWS == 0 and all(r % 16 == 0 for _, r in BLOCK_ROWS)


def _row_off(name):
    off = 0
    for nm, r in BLOCK_ROWS:
        if nm == name:
            return off
        off += r
    raise KeyError(name)


def _to_rows(name, a):
    if name == "w_in":
        return a.T
    if name == "w_up":
        t = a.T.reshape(2, FF_GROUP, D_MODEL)
        return jnp.pad(t, ((0, 0), (0, FF_GROUP_PAD - FF_GROUP), (0, 0))).reshape(2 * FF_GROUP_PAD, D_MODEL)
    if name == "w_down":
        return jnp.pad(a, ((0, FF_GROUP_PAD - FF_GROUP), (0, 0)))
    if name == "w_out":
        return a
    return a.reshape(-1, BLOB_LANES)


def _from_rows(name, r):
    if name == "w_in":
        return r.T
    if name == "w_up":
        return r.reshape(2, FF_GROUP_PAD, D_MODEL)[:, :FF_GROUP].reshape(2 * FF_GROUP, D_MODEL).T
    if name == "w_down":
        return r[:FF_GROUP]
    if name == "w_out":
        return r
    return r.reshape(D_MODEL // N_DEV, XATTN_W)


def _blob(tree, l):
    return jnp.concatenate([_to_rows(nm, tree[nm][l]) for nm, _ in BLOCK_ROWS], axis=0)


def _unblob(blobs):
    return {nm: jnp.stack([_from_rows(nm, b[_row_off(nm):_row_off(nm) + r]) for b in blobs]) for nm, r in BLOCK_ROWS}


def _ff_interleave(a):
    lead = a.shape[:-1]
    t = a.reshape(lead + (N_DEV, FF_GROUP))
    return jnp.pad(t, [(0, 0)] * len(lead) + [(0, 0), (0, FF_GROUP_PAD - FF_GROUP)]).reshape(lead + (D_FF_PAD,))


def _ff_deinterleave(a):
    lead = a.shape[:-1]
    return a.reshape(lead + (N_DEV, FF_GROUP_PAD))[..., :FF_GROUP].reshape(lead + (D_FF,))


def _head_table():
    hd = jnp.arange(XATTN_W) // HEAD_DIM
    return (hd[None, :] == jnp.arange(XATTN_HEADS)[:, None]).astype(F32)


def _mixer_operands(mat, conv_a_w, p, l, memq):
    w = {}
    w["wint"] = mat["w_in"]
    w["win"] = mat["w_in"].T
    w["wout"] = mat["w_out"]
    w["woutt"] = mat["w_out"].T
    w["cw"] = conv_a_w
    zeros = jnp.zeros((3, CONV_W), F32)
    w["va"] = jnp.concatenate([p[nm][l][None] for nm in VEC_A] + [zeros], axis=0)
    tril = jnp.tril(jnp.ones((CHUNK, CHUNK), F32))
    w["wcat"] = (p["w_s"][l] * tril[None]).transpose(1, 0, 2).reshape(CHUNK, GMLP_HEADS * CHUNK).astype(BF16)
    w["wcatt"] = w["wcat"].T
    w["bfull"] = jnp.repeat(p["b_s"][l].T, HEAD_DIM, axis=1)
    kh, vh = _mem_proj_call(memq, mat["w_mk"], mat["w_mv"])
    hm = _head_table()
    scale = 1.0 / math.sqrt(HEAD_DIM)
    w["kt"] = (kh.T[:, None, :] * hm.T[:, :, None] * scale).reshape(XATTN_W, XATTN_HEADS * N_MEM).astype(BF16)
    w["ktt"] = w["kt"].T
    w["vm"] = (hm[:, None, :] * vh[None]).reshape(XATTN_HEADS * N_MEM, XATTN_W).astype(BF16)
    w["vmt"] = w["vm"].T
    zeros = jnp.zeros((6, D_MODEL), F32)
    w["v1"] = jnp.concatenate([p["ln1_g"][l][None], p["ln1_b"][l][None], zeros], axis=0)
    return w


def _ffn_operands(w_up, w_down, conv_f_w, p, l):
    w = {}
    w["wgt"] = w_up[:D_FF_PAD]
    w["wvt"] = w_up[D_FF_PAD:]
    w["wg"] = w["wgt"].T
    w["wv"] = w["wvt"].T
    w["wdown"] = w_down
    w["wdownt"] = w_down.T
    zeros = jnp.zeros((6, D_MODEL), F32)
    w["v2"] = jnp.concatenate([p["ln2_g"][l][None], p["ln2_b"][l][None], zeros], axis=0)
    w["cf"] = jnp.concatenate([conv_f_w, _ff_interleave(p["conv_f_b"][l][None]), jnp.zeros((4, D_FF_PAD), F32)], axis=0)
    return w


TS_MIXER = 256
TS_FFN = 256
TS_PROJ = 512
CONV_A_SHARD = CONV_W // N_DEV


def kernel(x, mem, w_in, conv_a_w, conv_a_b, ln_a_g, ln_a_b, ln_v_g, ln_v_b, w_s, b_s, w_mk, w_mv, w_out, ln1_g, ln1_b, w_up, conv_f_w, conv_f_b, w_down, ln2_g, ln2_b, loss_target, m_w_in, m_conv_a_w, m_conv_a_b, m_ln_a_g, m_ln_a_b, m_ln_v_g, m_ln_v_b, m_w_s, m_b_s, m_w_mk, m_w_mv, m_w_out, m_ln1_g, m_ln1_b, m_w_up, m_conv_f_w, m_conv_f_b, m_w_down, m_ln2_g, m_ln2_b, v_w_in, v_conv_a_w, v_conv_a_b, v_ln_a_g, v_ln_a_b, v_ln_v_g, v_ln_v_b, v_w_s, v_b_s, v_w_mk, v_w_mv, v_w_out, v_ln1_g, v_ln1_b, v_w_up, v_conv_f_w, v_conv_f_b, v_w_down, v_ln2_g, v_ln2_b):
    given = dict(locals())
    p = {nm: given[nm] for nm in WEIGHTS}
    mom_m = {nm: given["m_" + nm] for nm in WEIGHTS}
    mom_v = {nm: given["v_" + nm] for nm in WEIGHTS}
    seq = x.shape[1]
    ts_m, ts_f, ts_p = min(TS_MIXER, seq), min(TS_FFN, seq), min(TS_PROJ, seq)
    cx, cy, cc = _place()
    me = 4 * cx + 2 * cy + cc

    blobs = [_blob(p, l) for l in range(DEPTH)]
    blobs_bf = [b.astype(BF16) for b in blobs]
    conv_a_tile = jnp.pad(conv_a_w, ((0, 0), (0, HALO - CONV_K), (0, 128 - CONV_A_SHARD)))
    conv_f_tile = jnp.pad(conv_f_w, ((0, 0), (0, 8 - FFN_CONV_K), (0, 384 - FF_GROUP)))
    rows = dict(BLOCK_ROWS)
    mixer_names = ("w_in", "w_out", "w_mk", "w_mv")
    pieces = [(0, _row_off(nm), rows[nm]) for nm in mixer_names] + [(1, None, 0), (2, None, 0)]
    first = _all_gather_call([blobs_bf[0], conv_a_tile, conv_f_tile], pieces, "gather_weights")
    conv_a_all, conv_f_all = first[len(mixer_names)], first[len(mixer_names) + 1]
    conv_a = [conv_a_all[:, l, :, :CONV_A_SHARD].transpose(1, 0, 2).reshape(HALO, CONV_W) for l in range(DEPTH)]
    conv_f = [conv_f_all[:, l, :FFN_CONV_K, :FF_GROUP_PAD].transpose(1, 0, 2).reshape(FFN_CONV_K, D_FF_PAD)
              for l in range(DEPTH)]
    memq = mem[0].reshape(N_MEM, D_MODEL // MEM_FOLD, MEM_FOLD).transpose(2, 0, 1).astype(BF16)

    def matrix(gathered, nm, base):
        lo = _row_off(nm) - base
        return gathered[:, lo:lo + rows[nm]].reshape(-1, BLOB_LANES)

    ffn_base = _row_off("w_up")
    ops0 = _mixer_operands({nm: first[a].reshape(-1, BLOB_LANES) for a, nm in enumerate(mixer_names)}, conv_a[0], p, 0,
                           memq)
    mixer_saved = ("hb", "z1", "x1", "a1", "cat", "p", "gs", "rv")
    mixed0, (ffn0,) = _mixer_fwd_call(x[0], ops0, ts_m,
                                            _gather_exchange(blobs_bf[0], ffn_base, rows["w_up"] + rows["w_down"]))
    ops0.update(_ffn_operands(matrix(ffn0, "w_up", ffn_base), matrix(ffn0, "w_down", ffn_base), conv_f[0], p, 0))
    saved = [dict(zip(mixer_saved, mixed0), x=x[0])]
    (ug, uv, sl, dsl, z2, x2), (all1,) = _ffn_fwd_call(saved[0]["x1"], ops0, ts_f,
                                                       _gather_exchange(blobs_bf[1], 0, LAYER_ROWS))
    saved[0].update(ug=ug, uv=uv, sl=sl, dsl=dsl, z2=z2)
    ops1 = _mixer_operands({nm: matrix(all1, nm, 0) for nm in mixer_names}, conv_a[1], p, 1, memq)
    ops1.update(_ffn_operands(matrix(all1, "w_up", 0), matrix(all1, "w_down", 0), conv_f[1], p, 1))
    mixed1, _ = _mixer_fwd_call(x2, ops1, ts_m)
    saved.append(dict(zip(mixer_saved, mixed1), x=x2))
    (ug, uv, sl, dsl, z2, _), _ = _ffn_fwd_call(saved[1]["x1"], ops1, ts_f)
    saved[1].update(ug=ug, uv=uv, sl=sl, dsl=dsl, z2=z2)
    ops = [ops0, ops1]

    hm = _head_table()
    core_id = cc.reshape(1).astype(jnp.int32)
    slabs = [lax.empty((N_DEV, LAYER_ROWS, BLOB_LANES), F32) for _ in range(DEPTH)]
    accs = [None] * DEPTH
    from_sibling, from_chips = [None] * DEPTH, [None] * DEPTH
    dx = loss_target[0]
    loss = None
    for l in reversed(range(DEPTH)):
        s, w = saved[l], ops[l]
        last = l == DEPTH - 1
        ride = None if last else _swap_exchange(slabs[l + 1], _swap_core_copies, 4)
        (sl, dug, duv, dz2, dcf, dv2, loss_acc), got = _ffn_bwd_call(
            dx, s["z2"], s["ug"], s["uv"], s["sl"], s["dsl"], w, ts_f, last, slabs[l], _row_off("w_down"), ride)
        if last:
            loss = loss_acc[0, 0]
        else:
            from_sibling[l + 1] = got[0]
            chip_sum = _pair_add_call(slabs[l + 1], from_sibling[l + 1], core_id)
        off_up = _row_off("w_up")
        (sl, dxa), _ = _proj_bwd_call(dug, w["wgt"], s["x1"], dz2, ALPHA, ts_p, "up_gate_bwd", sl, off_up, 0, 4)
        ride = None if last else _swap_exchange(chip_sum, _swap_chip_copies, 3)
        (sl, dx1), got = _proj_bwd_call(duv, w["wvt"], s["x1"], dxa, 1.0, ts_p, "up_val_bwd", sl, off_up, 4, 4, ride)
        if not last:
            from_chips[l + 1] = got[0]
        (sl, dx, dkt, dvm, dwcat, dmsum, dva, dcw, dv1) = _mixer_bwd_call(
            dx1, s["z1"], s["hb"], s["a1"], s["cat"], s["p"], s["gs"], s["rv"], s["x"], w, ts_m, sl, _row_off("w_out"),
            _row_off("w_in"))
        dkh = jnp.einsum("hd,dhm->md", hm, dkt.reshape(XATTN_W, XATTN_HEADS, N_MEM)) * (1.0 / math.sqrt(HEAD_DIM))
        dvh = jnp.einsum("hd,hmd->md", hm, dvm.reshape(XATTN_HEADS, N_MEM, XATTN_W))
        slabs[l] = _mem_proj_bwd_call(memq, dkh, dvh, sl, _row_off("w_mk"), _row_off("w_mv"))
        accs[l] = dict(dva=dva, dv1=dv1, dv2=dv2, dcf=dcf, dcw=dcw, dwcat=dwcat, dmsum=dmsum)
    grad_x = dx[None]
    from_sibling[0] = _swap_call(slabs[0], _swap_core_copies, 4, "rs_swap_core")
    chip_sum = _pair_add_call(slabs[0], from_sibling[0], core_id)
    from_chips[0] = _swap_call(chip_sum, _swap_chip_copies, 3, "rs_swap_chip")
    me_id, chip_id = me.reshape(1).astype(jnp.int32), (2 * cx + cy).reshape(1).astype(jnp.int32)
    per_layer = [_chip_add_adamw_call(slabs[l], from_sibling[l], from_chips[l], me_id, chip_id, blobs[l],
                                      _blob(mom_m, l), _blob(mom_v, l)) for l in range(DEPTH)]
    outs = [_unblob([per_layer[l][k] for l in range(DEPTH)]) for k in range(4)]

    acc_list = [accs[l][nm] for l in range(DEPTH) for nm, _ in GATHERED_ACCS]
    acc_all = _all_gather_call(acc_list, [(a, None, 0) for a in range(len(acc_list))], "gather_small_grads")
    n_acc = len(GATHERED_ACCS)
    gathered_accs = [{nm: acc_all[l * n_acc + a] for a, (nm, _) in enumerate(GATHERED_ACCS)} for l in range(DEPTH)]
    rep, dcf_sum, dcw_sum = _replicated_update_call(gathered_accs, p, mom_m, mom_v)
    for k in range(4):
        outs[k].update(rep[k])
    dcf_sum, dcw_sum = jnp.stack(dcf_sum), jnp.stack(dcw_sum)
    zero = jnp.zeros((), jnp.int32)
    g_conv_a_w = lax.dynamic_slice(dcw_sum, (zero, zero, CONV_A_SHARD * me), (DEPTH, CONV_K, CONV_A_SHARD))
    g_conv_f_w = lax.dynamic_slice(dcf_sum, (zero, zero, FF_GROUP_PAD * me), (DEPTH, FFN_CONV_K, FF_GROUP))
    g_conv_f_b = _ff_deinterleave(dcf_sum[:, FFN_CONV_K])
    conv_grads = dict(conv_a_w=g_conv_a_w, conv_f_w=g_conv_f_w, conv_f_b=g_conv_f_b)
    conv_names = tuple(conv_grads)
    upd = _adamw_whole_call([(p[nm], conv_grads[nm], mom_m[nm], mom_v[nm]) for nm in conv_names], "adamw_conv")
    for nm, (d, mn, vn) in zip(conv_names, upd):
        outs[0][nm], outs[1][nm], outs[2][nm], outs[3][nm] = conv_grads[nm], d, mn, vn

    loss = lax.psum(loss, ("x", "y", "c"))
    return (loss, grad_x, *[outs[0][nm] for nm in WEIGHTS], *[outs[1][nm] for nm in WEIGHTS],
            *[outs[2][nm] for nm in WEIGHTS], *[outs[3][nm] for nm in WEIGHTS])
```

```python
import math

import jax
import jax.numpy as jnp
from jax import lax
from jax.experimental import pallas as pl
from jax.experimental.pallas import tpu as pltpu

F32 = jnp.float32
BF16 = jnp.bfloat16

DEPTH = 2
D_MODEL = 1024
CONV_W = 384
GMLP_W = 384
XATTN_W = 256
HEAD_DIM = 64
GMLP_HEADS = 6
XATTN_HEADS = 4
IN_W = 1792
CONV_K = 31
CHUNK = 128
N_MEM = 256
D_FF = 2752
D_FF_PAD = 2816
FFN_CONV_K = 3
ALPHA = (2.0 * DEPTH) ** 0.25
LN_EPS = 1e-5
N_DEV = 8

ADAM_LR = 0.001
ADAM_B1 = 0.9
ADAM_B2 = 0.999
ADAM_EPS = 1e-08
ADAM_WD = 0.01
ADAM_STEP = 10

HALO = 32
CONV_ROWS = 32
V7X_VMEM_BYTES = 64 * 1024 * 1024
VMEM_LIMIT = V7X_VMEM_BYTES - 8 * 1024 * 1024
BLOB_LANES = 1024

MESH = pl.DeviceIdType.MESH

WEIGHTS = ("w_in", "conv_a_w", "conv_a_b", "ln_a_g", "ln_a_b", "ln_v_g", "ln_v_b", "w_s", "b_s", "w_mk", "w_mv",
           "w_out", "ln1_g", "ln1_b", "w_up", "conv_f_w", "conv_f_b", "w_down", "ln2_g", "ln2_b")


def _params(**kw):
    return pltpu.CompilerParams(vmem_limit_bytes=VMEM_LIMIT, **kw)


def _const(shape):
    nd = len(shape)
    return pl.BlockSpec(shape, lambda i: (0,) * nd, pipeline_mode=pl.Buffered(1))


def _acc(shape):
    nd = len(shape)
    return pl.BlockSpec(shape, lambda i: (0,) * nd)


class _Exchange:
    def __init__(self, arrays, out_shapes, n_copies, build, n_local=1):
        self.arrays, self.out_shapes, self.n_copies, self.build = list(arrays), list(out_shapes), n_copies, build
        self.n_local = n_local


def _carry(core, n_in, n_out, exch, n_steps):
    if exch is None:
        return core
    nx_in, nx_out = len(exch.arrays), len(exch.out_shapes)

    def body(*refs):
        o0 = n_in + nx_in
        s0 = o0 + n_out + nx_out
        x_in, x_out, sems = refs[n_in:o0], refs[o0 + n_out:s0], refs[-3:]
        i = pl.program_id(0)

        @pl.when(i == 0)
        def _():
            remote, local = exch.build(x_in, x_out, *sems)
            for cp in remote + local:
                cp.start()

        core(*refs[:n_in], *refs[o0:o0 + n_out], *refs[s0:-3])

        @pl.when(i == n_steps - 1)
        def _():
            remote, local = exch.build(x_in, x_out, *sems)
            for cp in remote + local:
                cp.wait()

    return body


def _grid_call(core, name, n_steps, in_specs, out_specs, out_shape, scratch_shapes, args, aliases=None, exch=None):
    hbm = pl.BlockSpec(memory_space=pl.ANY)
    n_in, n_out = len(in_specs), len(out_specs)
    in_specs, out_specs, out_shape, scratch_shapes, args = (list(in_specs), list(out_specs), list(out_shape),
                                                            list(scratch_shapes), list(args))
    if exch is not None:
        in_specs += [hbm] * len(exch.arrays)
        out_specs += [hbm] * len(exch.out_shapes)
        out_shape += exch.out_shapes
        scratch_shapes += [pltpu.SemaphoreType.DMA((exch.n_copies,)), pltpu.SemaphoreType.DMA((exch.n_copies,)),
                           pltpu.SemaphoreType.DMA((exch.n_local,))]
        args += exch.arrays
    out = pl.pallas_call(
        _carry(core, n_in, n_out, exch, n_steps), name=name, grid=(n_steps,), in_specs=in_specs, out_specs=out_specs,
        out_shape=out_shape, scratch_shapes=scratch_shapes, input_output_aliases=aliases or {},
        compiler_params=_params(dimension_semantics=("arbitrary",)))(*args)
    return list(out[:n_out]), list(out[n_out:])


def _sigmoid(x):
    return 1.0 / (1.0 + jnp.exp(-x))


_GELU_C = math.sqrt(2.0 / math.pi)


def _gelu(x):
    x2 = x * x
    t = jnp.tanh(_GELU_C * (x + 0.044715 * x * x2))
    g = 0.5 * x * (1.0 + t)
    dg = 0.5 * (1.0 + t) + 0.5 * x * (1.0 - t * t) * (_GELU_C * (1.0 + 3.0 * 0.044715 * x2))
    return g, dg


def _ln_stats(z):
    mu = jnp.mean(z, axis=-1, keepdims=True)
    zc = z - mu
    var = jnp.mean(zc * zc, axis=-1, keepdims=True)
    r = lax.rsqrt(var + LN_EPS)
    return zc * r, r


def _ln_bwd(dy, xh, r, g):
    dxh = dy * g
    m1 = jnp.mean(dxh, axis=-1, keepdims=True)
    m2 = jnp.mean(dxh * xh, axis=-1, keepdims=True)
    return r * (dxh - m1 - xh * m2)


def _rowsum(x):
    return jnp.sum(x, axis=0, keepdims=True)


def _dot(a, b):
    return jnp.dot(a, b, preferred_element_type=F32)


def _dot_tn(a, b):
    return lax.dot_general(a, b, (((0,), (0,)), ((), ())), preferred_element_type=F32)


def _dot_nt(a, b):
    return lax.dot_general(a, b, (((1,), (1,)), ((), ())), preferred_element_type=F32)


def _shift_copies(buf, sh, rows):
    for b in range(1, 8):
        sh[b - 1, 0:rows, :] = buf[b:b + rows, :]


def _window(buf, sh, start):
    b = start % 8
    a = start - b
    return buf[a:a + CONV_ROWS, :] if b == 0 else sh[b - 1, a:a + CONV_ROWS, :]


def _conv31_fwd(buf, sh, w_ref, bias, out, ts):
    for r0 in range(0, ts, CONV_ROWS):
        acc = jnp.broadcast_to(bias, (CONV_ROWS, CONV_W))
        for k in range(CONV_K):
            acc = acc + w_ref[k:k + 1, :] * _window(buf, sh, r0 + HALO - (CONV_K - 1) + k)
        out[r0:r0 + CONV_ROWS, :] = acc


def _conv31_dx(dbuf, dsh, w_ref, out, ts):
    for r0 in range(0, ts, CONV_ROWS):
        acc = jnp.zeros((CONV_ROWS, CONV_W), F32)
        for k in range(CONV_K):
            acc = acc + w_ref[k:k + 1, :] * _window(dbuf, dsh, r0 + (CONV_K - 1) - k)
        out[r0:r0 + CONV_ROWS, :] = acc


def _conv31_dw(buf, sh, dbuf, dw_ref, ts):
    for k in range(CONV_K):
        part = jnp.zeros((8, CONV_W), F32)
        for r0 in range(0, ts, CONV_ROWS):
            m = dbuf[r0:r0 + CONV_ROWS, :] * _window(buf, sh, r0 + HALO - (CONV_K - 1) + k)
            for q in range(0, CONV_ROWS, 8):
                part = part + m[q:q + 8, :]
        dw_ref[k:k + 1, :] += _rowsum(part)


def _head_mask(width, h):
    lane = lax.broadcasted_iota(jnp.int32, (CHUNK, width), 1)
    return (lane >= h * HEAD_DIM) & (lane < (h + 1) * HEAD_DIM)


def _stack_heads(vn_c):
    return jnp.concatenate([jnp.where(_head_mask(GMLP_W, h), vn_c, 0.0) for h in range(GMLP_HEADS)], axis=0)


def _group_a_fwd(hf, buf, sh, a1_ref, cw_ref, va_ref, ts, conv=True):
    ha = hf[:, 0:CONV_W]
    sg = _sigmoid(hf[:, CONV_W:2 * CONV_W])
    buf[HALO:HALO + ts, :] = ha * sg
    _shift_copies(buf, sh, ts + HALO - 8)
    if conv:
        _conv31_fwd(buf, sh, cw_ref, va_ref[0:1, :], a1_ref, ts)
    a2h, ra = _ln_stats(a1_ref[...])
    a2 = a2h * va_ref[1:2, :] + va_ref[2:3, :]
    sa = _sigmoid(a2)
    return dict(ha=ha, sg=sg, a2h=a2h, ra=ra, a2=a2, sa=sa, a=a2 * sa)


GROUP_LANES = ((0, 2 * CONV_W), (2 * CONV_W, 2 * CONV_W + 2 * GMLP_W), (2 * CONV_W + 2 * GMLP_W, IN_W))


def _group_b_fwd(hf, va_ref, wcat_ref, bfull_ref, ts):
    hu = hf[:, 0:GMLP_W]
    hv = hf[:, GMLP_W:2 * GMLP_W]
    u, du = _gelu(hu)
    v, dv = _gelu(hv)
    vhat, rv = _ln_stats(v)
    vn = vhat * va_ref[3:4, :] + va_ref[4:5, :]
    stacks, mixed = [], []
    for c0 in range(0, ts, CHUNK):
        st = _stack_heads(vn[c0:c0 + CHUNK, :]).astype(BF16)
        stacks.append(st)
        mixed.append(_dot(wcat_ref[...], st) + bfull_ref[...])
    mixed = jnp.concatenate(mixed, axis=0) if len(mixed) > 1 else mixed[0]
    return dict(u=u, du=du, dv=dv, vhat=vhat, rv=rv, stacks=stacks, mixed=mixed, g=u * mixed)


GROUP_B_SAVED = ("u", "du", "dv", "vhat", "mixed")


def _group_c_fwd(qb, kt_ref, vm_ref):
    s_all = _dot(qb, kt_ref[...])
    ps = []
    for g in range(XATTN_HEADS):
        s = s_all[:, g * N_MEM:(g + 1) * N_MEM]
        e = jnp.exp(s - jnp.max(s, axis=-1, keepdims=True))
        ps.append(e / jnp.sum(e, axis=-1, keepdims=True))
    p_all = jnp.concatenate(ps, axis=1)
    pb = p_all.astype(BF16)
    return dict(qb=qb, p=p_all, pb=pb, c=_dot(pb, vm_ref[...]))


def _mixer_fwd_call(x, w, ts, exch=None):
    seq = x.shape[0]
    n = seq // ts

    def body(x_ref, win_ref, cw_ref, va_ref, wcat_ref, bfull_ref, kt_ref, vm_ref, wout_ref, v1_ref,
             hb_ref, z1_ref, x1_ref, a1buf, cat_ref, p_ref, gs_ref, rv_ref, buf, sh):
        i = pl.program_id(0)

        @pl.when(i == 0)
        def _():
            buf[0:HALO, :] = jnp.zeros((HALO, CONV_W), F32)

        @pl.when(i > 0)
        def _():
            buf[0:HALO, :] = buf[ts:ts + HALO, :]

        xv = x_ref[...]
        hb = _dot(xv.astype(BF16), win_ref[...]).astype(BF16)
        hb_ref[...] = hb
        hf = hb.astype(F32)
        (a_lo, a_hi), (b_lo, b_hi), (c_lo, c_hi) = GROUP_LANES
        ga = _group_a_fwd(hf[:, a_lo:a_hi], buf, sh, a1buf, cw_ref, va_ref, ts)
        gb = _group_b_fwd(hf[:, b_lo:b_hi], va_ref, wcat_ref, bfull_ref, ts)
        gc = _group_c_fwd(hb[:, c_lo:c_hi], kt_ref, vm_ref)
        cat = jnp.concatenate([ga["a"], gb["g"], gc["c"]], axis=1).astype(BF16)
        cat_ref[...] = cat
        p_ref[...] = gc["pb"]
        gs_ref[...] = jnp.concatenate([gb[k] for k in GROUP_B_SAVED], axis=1).astype(BF16)
        rv_ref[...] = jnp.broadcast_to(gb["rv"], (ts, 128))
        z1 = ALPHA * xv + _dot(cat, wout_ref[...])
        z1_ref[...] = z1
        xh, _ = _ln_stats(z1)
        x1_ref[...] = xh * v1_ref[0:1, :] + v1_ref[1:2, :]

    row = lambda width: pl.BlockSpec((ts, width), lambda i: (i, 0))
    return _grid_call(
        body, "mixer_fwd", n,
        in_specs=[row(D_MODEL), _const((D_MODEL, IN_W)), _const((HALO, CONV_W)), _const((8, CONV_W)),
                  _const((CHUNK, GMLP_HEADS * CHUNK)), _const((CHUNK, GMLP_W)), _const((XATTN_W, XATTN_HEADS * N_MEM)),
                  _const((XATTN_HEADS * N_MEM, XATTN_W)), _const((D_MODEL, D_MODEL)), _const((8, D_MODEL))],
        out_specs=[row(IN_W), row(D_MODEL), row(D_MODEL), row(CONV_W), row(D_MODEL), row(XATTN_HEADS * N_MEM),
                   row(len(GROUP_B_SAVED) * GMLP_W), row(128)],
        out_shape=[jax.ShapeDtypeStruct((seq, IN_W), BF16), jax.ShapeDtypeStruct((seq, D_MODEL), F32),
                   jax.ShapeDtypeStruct((seq, D_MODEL), F32), jax.ShapeDtypeStruct((seq, CONV_W), F32),
                   jax.ShapeDtypeStruct((seq, D_MODEL), BF16), jax.ShapeDtypeStruct((seq, XATTN_HEADS * N_MEM), BF16),
                   jax.ShapeDtypeStruct((seq, len(GROUP_B_SAVED) * GMLP_W), BF16),
                   jax.ShapeDtypeStruct((seq, 128), F32)],
        scratch_shapes=[pltpu.VMEM((ts + HALO, CONV_W), F32), pltpu.VMEM((7, ts + HALO, CONV_W), F32)],
        args=(x, w["win"], w["cw"], w["va"], w["wcat"], w["bfull"], w["kt"], w["vm"], w["wout"], w["v1"]), exch=exch)


def _store_blocks(acc, slabs_ref, sems, row_off, rows, first_block, n_blocks):
    copies = [pltpu.make_async_copy(acc.at[pl.ds(q * rows, rows)], slabs_ref.at[first_block + q, pl.ds(row_off, rows)],
                                    sems.at[q]) for q in range(n_blocks)]
    for cp in copies:
        cp.start()
    for cp in copies:
        cp.wait()


def _mixer_bwd_call(dx1, z1, hb, a1, cat, p, gs, rv, x, w, ts, slabs, off_out, off_in):
    seq = dx1.shape[0]
    n = seq // ts
    halo_blocks = ts // HALO

    def body(slabs_in, dx1_ref, z1_ref, hb_ref, hprev_ref, a1_ref, cat_ref, p_ref, gs_ref, rv_ref, x_ref, cw_ref, va_ref,
             wcatt_ref, ktt_ref, vmt_ref, woutt_ref, wint_ref, v1_ref,
             slabs_ref, dx_ref, dkt_ref, dvm_ref, dwcat_ref, dmsum_ref, dva_ref, dcw_ref, dv1_ref,
             buf, dbuf, da0buf, dwout_ref, dwin_ref, sems, sh, dsh):
        i = pl.program_id(0)

        @pl.when(i == 0)
        def _():
            for ref in (dwout_ref, dwin_ref, dkt_ref, dvm_ref, dwcat_ref, dmsum_ref, dva_ref, dcw_ref, dv1_ref):
                ref[...] = jnp.zeros(ref.shape, F32)
            dbuf[ts:ts + HALO, :] = jnp.zeros((HALO, CONV_W), F32)

        @pl.when(i > 0)
        def _():
            dbuf[ts:ts + HALO, :] = dbuf[0:HALO, :]

        dx1v = dx1_ref[...]
        xh1, r1 = _ln_stats(z1_ref[...])
        dv1_ref[0:1, :] += _rowsum(dx1v * xh1)
        dv1_ref[1:2, :] += _rowsum(dx1v)
        dz1 = _ln_bwd(dx1v, xh1, r1, v1_ref[0:1, :])
        dmix = dz1.astype(BF16)

        hf = hb_ref[:, 0:2 * CONV_W].astype(F32)
        hp = hprev_ref[...].astype(F32)
        a0p = hp[:, 0:CONV_W] * _sigmoid(hp[:, CONV_W:2 * CONV_W])
        buf[0:HALO, :] = jnp.where(i == n - 1, 0.0, a0p)
        ga = _group_a_fwd(hf, buf, sh, a1_ref, cw_ref, va_ref, ts, conv=False)
        gb = {k: gs_ref[:, j * GMLP_W:(j + 1) * GMLP_W].astype(F32) for j, k in enumerate(GROUP_B_SAVED)}
        vn = gb["vhat"] * va_ref[3:4, :] + va_ref[4:5, :]
        pb = p_ref[...]
        gc = dict(qb=hb_ref[:, IN_W - XATTN_W:IN_W], pb=pb, p=pb.astype(F32))

        dwout_ref[...] += _dot_tn(cat_ref[...], dmix)
        dcat = _dot(dmix, woutt_ref[...])
        da = dcat[:, 0:CONV_W]
        dg = dcat[:, CONV_W:CONV_W + GMLP_W]
        dc = dcat[:, CONV_W + GMLP_W:D_MODEL].astype(BF16)

        dp = _dot(dc, vmt_ref[...])
        dvm_ref[...] += _dot_tn(gc["pb"], dc)
        dss = []
        for g in range(XATTN_HEADS):
            sl = slice(g * N_MEM, (g + 1) * N_MEM)
            pg = gc["p"][:, sl]
            dpg = dp[:, sl]
            dss.append(pg * (dpg - jnp.sum(dpg * pg, axis=-1, keepdims=True)))
        ds = jnp.concatenate(dss, axis=1).astype(BF16)
        dq = _dot(ds, ktt_ref[...])
        dkt_ref[...] += _dot_tn(gc["qb"], ds)

        dmixed = dg * gb["u"]
        dhu = dg * gb["mixed"] * gb["du"]
        dvns = []
        for j, c0 in enumerate(range(0, ts, CHUNK)):
            dm = dmixed[c0:c0 + CHUNK, :]
            dmb = dm.astype(BF16)
            dmsum_ref[...] += dm
            dwcat_ref[...] += _dot_nt(dmb, _stack_heads(vn[c0:c0 + CHUNK, :]).astype(BF16))
            dst = _dot(wcatt_ref[...], dmb)
            dvn_c = jnp.zeros((CHUNK, GMLP_W), F32)
            for h in range(GMLP_HEADS):
                dvn_c = dvn_c + jnp.where(_head_mask(GMLP_W, h), dst[h * CHUNK:(h + 1) * CHUNK, :], 0.0)
            dvns.append(dvn_c)
        dvn = jnp.concatenate(dvns, axis=0) if len(dvns) > 1 else dvns[0]
        dva_ref[3:4, :] += _rowsum(dvn * gb["vhat"])
        dva_ref[4:5, :] += _rowsum(dvn)
        dhv = _ln_bwd(dvn, gb["vhat"], rv_ref[:, 0:1], va_ref[3:4, :]) * gb["dv"]

        a2, sa = ga["a2"], ga["sa"]
        da2 = da * (sa * (1.0 + a2 * (1.0 - sa)))
        dva_ref[1:2, :] += _rowsum(da2 * ga["a2h"])
        dva_ref[2:3, :] += _rowsum(da2)
        da1 = _ln_bwd(da2, ga["a2h"], ga["ra"], va_ref[1:2, :])
        dva_ref[0:1, :] += _rowsum(da1)
        dbuf[0:ts, :] = da1
        _shift_copies(dbuf, dsh, ts + HALO - 8)
        _conv31_dw(buf, sh, dbuf, dcw_ref, ts)
        _conv31_dx(dbuf, dsh, cw_ref, da0buf, ts)
        da0 = da0buf[...]
        sg = ga["sg"]
        dha = da0 * sg
        dhg = da0 * ga["ha"] * sg * (1.0 - sg)

        dh = jnp.concatenate([dha, dhg, dhu, dhv, dq], axis=1).astype(BF16)
        dx_ref[...] = _dot(dh, wint_ref[...]) + ALPHA * dz1
        dwin_ref[...] += _dot_tn(dh, x_ref[...].astype(BF16))

        @pl.when(i == n - 1)
        def _():
            _store_blocks(dwout_ref, slabs_ref, sems, off_out, D_MODEL // N_DEV, 0, N_DEV)
            _store_blocks(dwin_ref, slabs_ref, sems, off_in, IN_W // N_DEV, 0, N_DEV)

    rev = lambda width: pl.BlockSpec((ts, width), lambda i: (n - 1 - i, 0))
    prev = pl.BlockSpec((HALO, 2 * CONV_W), lambda i: (jnp.maximum((n - 1 - i) * halo_blocks - 1, 0), 0))
    hbm = pl.BlockSpec(memory_space=pl.ANY)
    hc = GMLP_HEADS * CHUNK
    am = XATTN_HEADS * N_MEM
    return pl.pallas_call(
        body, name="mixer_bwd", grid=(n,),
        in_specs=[hbm, rev(D_MODEL), rev(D_MODEL), rev(IN_W), prev, rev(CONV_W), rev(D_MODEL), rev(am),
                  rev(len(GROUP_B_SAVED) * GMLP_W), rev(128), rev(D_MODEL), _const((HALO, CONV_W)),
                  _const((8, CONV_W)), _const((hc, CHUNK)), _const((am, XATTN_W)), _const((XATTN_W, am)),
                  _const((D_MODEL, D_MODEL)), _const((IN_W, D_MODEL)), _const((8, D_MODEL))],
        out_specs=[hbm, rev(D_MODEL), _acc((XATTN_W, am)), _acc((am, XATTN_W)),
                   _acc((CHUNK, hc)), _acc((CHUNK, GMLP_W)), _acc((8, CONV_W)), _acc((HALO, CONV_W)),
                   _acc((8, D_MODEL))],
        out_shape=[jax.ShapeDtypeStruct(slabs.shape, F32), jax.ShapeDtypeStruct((seq, D_MODEL), F32),
                   jax.ShapeDtypeStruct((XATTN_W, am), F32),
                   jax.ShapeDtypeStruct((am, XATTN_W), F32), jax.ShapeDtypeStruct((CHUNK, hc), F32),
                   jax.ShapeDtypeStruct((CHUNK, GMLP_W), F32), jax.ShapeDtypeStruct((8, CONV_W), F32),
                   jax.ShapeDtypeStruct((HALO, CONV_W), F32), jax.ShapeDtypeStruct((8, D_MODEL), F32)],
        scratch_shapes=[pltpu.VMEM((ts + HALO, CONV_W), F32),
                        pltpu.VMEM((ts + HALO, CONV_W), F32), pltpu.VMEM((ts, CONV_W), F32),
                        pltpu.VMEM((D_MODEL, D_MODEL), F32), pltpu.VMEM((IN_W, D_MODEL), F32),
                        pltpu.SemaphoreType.DMA((N_DEV,)),
                        pltpu.VMEM((7, ts + HALO, CONV_W), F32), pltpu.VMEM((7, ts + HALO, CONV_W), F32)],
        input_output_aliases={0: 0},
        compiler_params=_params(dimension_semantics=("arbitrary",)),
    )(slabs, dx1, z1, hb, hb, a1, cat, p, gs, rv, x, w["cw"], w["va"], w["wcatt"], w["ktt"], w["vmt"], w["woutt"],
      w["wint"], w["v1"])


FFN_HALO = 8
FF_GROUP = D_FF // N_DEV
FF_GROUP_PAD = D_FF_PAD // N_DEV


def _ffn_taps(ubuf, ts, lo, hi):
    return tuple(ubuf[FFN_HALO - (FFN_CONV_K - 1) + k:FFN_HALO - (FFN_CONV_K - 1) + k + ts, lo:hi]
                 for k in range(FFN_CONV_K))


def _ffn_gate(ubuf, cf_ref, ts, lo, hi):
    taps = _ffn_taps(ubuf, ts, lo, hi)
    g = cf_ref[3:4, lo:hi] + cf_ref[2:3, lo:hi] * taps[2]
    g = g + cf_ref[1:2, lo:hi] * taps[1]
    return g + cf_ref[0:1, lo:hi] * taps[0]


FFN_LANE_CHUNKS = ((0, D_FF_PAD // 2), (D_FF_PAD // 2, D_FF_PAD))
FFN_BWD_CHUNK = 256
FFN_BWD_CHUNKS = tuple((lo, lo + FFN_BWD_CHUNK) for lo in range(0, D_FF_PAD, FFN_BWD_CHUNK))


def _ffn_fwd_call(x1, w, ts, exch=None):
    seq = x1.shape[0]
    n = seq // ts

    def body(x1_ref, wg_ref, wv_ref, cf_ref, wdown_ref, v2_ref, ug_ref, uv_ref, sl_ref, dsl_ref, z2_ref, x2_ref, ubuf,
             act_buf):
        i = pl.program_id(0)

        @pl.when(i == 0)
        def _():
            ubuf[0:FFN_HALO, :] = jnp.zeros((FFN_HALO, D_FF_PAD), F32)

        @pl.when(i > 0)
        def _():
            ubuf[0:FFN_HALO, :] = ubuf[ts:ts + FFN_HALO, :]

        xv = x1_ref[...]
        xb = xv.astype(BF16)
        for lo, hi in FFN_BWD_CHUNKS:
            ug = _dot(xb, wg_ref[:, lo:hi]).astype(BF16)
            uv = _dot(xb, wv_ref[:, lo:hi]).astype(BF16)
            ug_ref[:, lo:hi] = ug
            uv_ref[:, lo:hi] = uv
            ubuf[FFN_HALO:FFN_HALO + ts, lo:hi] = ug.astype(F32)
            gate = _ffn_gate(ubuf, cf_ref, ts, lo, hi)
            sg = _sigmoid(gate)
            sl = gate * sg
            sl_ref[:, lo:hi] = sl.astype(BF16)
            dsl_ref[:, lo:hi] = (sg * (1.0 + gate * (1.0 - sg))).astype(BF16)
            act_buf[:, lo:hi] = (sl * uv.astype(F32)).astype(BF16)
        y = ALPHA * xv + _dot(act_buf[...], wdown_ref[...])
        z2_ref[...] = y
        xh, _ = _ln_stats(y)
        x2_ref[...] = xh * v2_ref[0:1, :] + v2_ref[1:2, :]

    row = lambda width: pl.BlockSpec((ts, width), lambda i: (i, 0))
    return _grid_call(
        body, "ffn_fwd", n,
        in_specs=[row(D_MODEL), _const((D_MODEL, D_FF_PAD)), _const((D_MODEL, D_FF_PAD)), _const((8, D_FF_PAD)),
                  _const((D_FF_PAD, D_MODEL)), _const((8, D_MODEL))],
        out_specs=[row(D_FF_PAD)] * 4 + [row(D_MODEL), row(D_MODEL)],
        out_shape=[jax.ShapeDtypeStruct((seq, D_FF_PAD), BF16)] * 4 + [jax.ShapeDtypeStruct((seq, D_MODEL), F32)] * 2,
        scratch_shapes=[pltpu.VMEM((ts + FFN_HALO, D_FF_PAD), F32), pltpu.VMEM((ts, D_FF_PAD), BF16)],
        args=(x1, w["wg"], w["wv"], w["cf"], w["wdown"], w["v2"]), exch=exch)


def _ffn_bwd_call(dx2_or_target, z2, ug, uv, sl, dsl, w, ts, last, slabs, row_off, exch=None):
    seq = z2.shape[0]
    n = seq // ts
    halo_blocks = ts // 16

    def body(slabs_in, dx2_ref, z2_ref, ug_ref, uv_ref, sl_ref, dsl_ref, uprev_ref, cf_ref, wdownt_ref, v2_ref,
             slabs_ref, dug_ref, duv_ref, dz2_ref, dcf_ref, dv2_ref, loss_ref,
             ubuf, dgbuf, dwacc, sems):
        i = pl.program_id(0)

        @pl.when(i == 0)
        def _():
            dwacc[...] = jnp.zeros(dwacc.shape, F32)
            dcf_ref[...] = jnp.zeros(dcf_ref.shape, F32)
            dv2_ref[...] = jnp.zeros(dv2_ref.shape, F32)
            loss_ref[...] = jnp.zeros(loss_ref.shape, F32)
            dgbuf[ts:ts + FFN_HALO, :] = jnp.zeros((FFN_HALO, D_FF_PAD), F32)

        @pl.when(i > 0)
        def _():
            dgbuf[ts:ts + FFN_HALO, :] = dgbuf[0:FFN_HALO, :]

        xh2, r2 = _ln_stats(z2_ref[...])
        if last:
            diff = xh2 * v2_ref[0:1, :] + v2_ref[1:2, :] - dx2_ref[...]
            loss_ref[...] += jnp.sum(diff * diff) * (0.5 / D_MODEL)
            dx2v = diff * (1.0 / D_MODEL)
        else:
            dx2v = dx2_ref[...]
        dv2_ref[0:1, :] += _rowsum(dx2v * xh2)
        dv2_ref[1:2, :] += _rowsum(dx2v)
        dz2 = _ln_bwd(dx2v, xh2, r2, v2_ref[0:1, :])
        dz2_ref[...] = dz2
        dy = dz2.astype(BF16)

        up = uprev_ref[...].astype(F32)[8:16, :]
        ubuf[0:FFN_HALO, :] = jnp.where(i == n - 1, 0.0, up)
        ubuf[FFN_HALO:FFN_HALO + ts, :] = ug_ref[...].astype(F32)
        for lo, hi in FFN_BWD_CHUNKS:
            taps = _ffn_taps(ubuf, ts, lo, hi)
            sl = sl_ref[:, lo:hi].astype(F32)
            uvf = uv_ref[:, lo:hi].astype(F32)
            act = (sl * uvf).astype(BF16)
            dwacc[lo:hi, :] += _dot_tn(act, dy)
            dact = _dot(dy, wdownt_ref[:, lo:hi])
            duv_ref[:, lo:hi] = (dact * sl).astype(BF16)
            dgate = dact * uvf * dsl_ref[:, lo:hi].astype(F32)
            dgbuf[0:ts, lo:hi] = dgate
            dcf_ref[3:4, lo:hi] += _rowsum(dgate)
            for k in range(FFN_CONV_K):
                dcf_ref[k:k + 1, lo:hi] += _rowsum(dgate * taps[k])
            dug = cf_ref[2:3, lo:hi] * dgate + cf_ref[1:2, lo:hi] * dgbuf[1:1 + ts, lo:hi]
            dug = dug + cf_ref[0:1, lo:hi] * dgbuf[2:2 + ts, lo:hi]
            dug_ref[:, lo:hi] = dug.astype(BF16)

        @pl.when(i == n - 1)
        def _():
            _store_blocks(dwacc, slabs_ref, sems, row_off, D_FF_PAD // N_DEV, 0, N_DEV)

    rev = lambda width: pl.BlockSpec((ts, width), lambda i: (n - 1 - i, 0))
    prev = pl.BlockSpec((16, D_FF_PAD), lambda i: (jnp.maximum((n - 1 - i) * halo_blocks - 1, 0), 0))
    hbm = pl.BlockSpec(memory_space=pl.ANY)
    return _grid_call(
        body, "ffn_bwd_last" if last else "ffn_bwd", n,
        in_specs=[hbm, rev(D_MODEL), rev(D_MODEL)] + [rev(D_FF_PAD)] * 4 + [prev, _const((8, D_FF_PAD)),
                                                                           _const((D_MODEL, D_FF_PAD)), _const((8, D_MODEL))],
        out_specs=[hbm, rev(D_FF_PAD), rev(D_FF_PAD), rev(D_MODEL),
                   _acc((8, D_FF_PAD)), _acc((8, D_MODEL)), _acc((8, 128))],
        out_shape=[jax.ShapeDtypeStruct(slabs.shape, F32),
                   jax.ShapeDtypeStruct((seq, D_FF_PAD), BF16), jax.ShapeDtypeStruct((seq, D_FF_PAD), BF16),
                   jax.ShapeDtypeStruct((seq, D_MODEL), F32),
                   jax.ShapeDtypeStruct((8, D_FF_PAD), F32), jax.ShapeDtypeStruct((8, D_MODEL), F32),
                   jax.ShapeDtypeStruct((8, 128), F32)],
        scratch_shapes=[pltpu.VMEM((ts + FFN_HALO, D_FF_PAD), F32), pltpu.VMEM((ts + FFN_HALO, D_FF_PAD), F32),
                        pltpu.VMEM((D_FF_PAD, D_MODEL), F32), pltpu.SemaphoreType.DMA((N_DEV,))],
        args=(slabs, dx2_or_target, z2, ug, uv, sl, dsl, ug, w["cf"], w["wdownt"], w["v2"]), aliases={0: 0}, exch=exch)


def _proj_bwd_call(d, wt, xin, addend, scale, ts, name, slabs, row_off, first_block, n_blocks, exch=None):
    seq, k = d.shape
    n = seq // ts

    def body(slabs_in, d_ref, wt_ref, xin_ref, add_ref, slabs_ref, dx_ref, acc, sems):
        i = pl.program_id(0)

        @pl.when(i == 0)
        def _():
            acc[...] = jnp.zeros(acc.shape, F32)

        dv = d_ref[...]
        dx_ref[...] = _dot(dv, wt_ref[...]) + scale * add_ref[...]
        acc[...] += _dot_tn(dv, xin_ref[...].astype(BF16))

        @pl.when(i == n - 1)
        def _():
            _store_blocks(acc, slabs_ref, sems, row_off, k // n_blocks, first_block, n_blocks)

    row = lambda width: pl.BlockSpec((ts, width), lambda i: (i, 0))
    hbm = pl.BlockSpec(memory_space=pl.ANY)
    return _grid_call(
        body, name, n,
        in_specs=[hbm, row(k), _const((k, D_MODEL)), row(D_MODEL), row(D_MODEL)],
        out_specs=[hbm, row(D_MODEL)],
        out_shape=[jax.ShapeDtypeStruct(slabs.shape, F32), jax.ShapeDtypeStruct((seq, D_MODEL), F32)],
        scratch_shapes=[pltpu.VMEM((k, D_MODEL), F32), pltpu.SemaphoreType.DMA((n_blocks,))],
        args=(slabs, d, wt, xin, addend), aliases={0: 0}, exch=exch)


MEM_FOLD = BLOB_LANES // XATTN_W


def _mem_proj_call(memq, wk_flat, wv_flat):
    def body(memq_ref, wk_ref, wv_ref, kh_ref, vh_ref):
        for w_ref, o_ref in ((wk_ref, kh_ref), (wv_ref, vh_ref)):
            acc = jnp.zeros((N_MEM, XATTN_W), F32)
            for q in range(MEM_FOLD):
                acc = acc + _dot(memq_ref[q], w_ref[:, q * XATTN_W:(q + 1) * XATTN_W])
            o_ref[...] = acc

    out = jax.ShapeDtypeStruct((N_MEM, XATTN_W), F32)
    return pl.pallas_call(body, name="mem_proj", out_shape=[out, out], compiler_params=_params())(memq, wk_flat, wv_flat)


def _mem_proj_bwd_call(memq, dkh, dvh, slabs, off_k, off_v):
    rows = D_MODEL // MEM_FOLD

    def body(slabs_in, memq_ref, dkh_ref, dvh_ref, slabs_ref, acc, sems):
        for d_ref, off in ((dkh_ref, off_k), (dvh_ref, off_v)):
            dv = d_ref[...].astype(BF16)
            for q in range(MEM_FOLD):
                acc[:, q * XATTN_W:(q + 1) * XATTN_W] = _dot_tn(memq_ref[q], dv)
            _store_blocks(acc, slabs_ref, sems, off, rows // N_DEV, 0, N_DEV)

    hbm = pl.BlockSpec(memory_space=pl.ANY)
    vmem = pl.BlockSpec(memory_space=pltpu.VMEM)
    return pl.pallas_call(
        body, name="mem_proj_bwd", in_specs=[hbm, vmem, vmem, vmem], out_specs=hbm,
        out_shape=jax.ShapeDtypeStruct(slabs.shape, F32),
        scratch_shapes=[pltpu.VMEM((rows, BLOB_LANES), F32), pltpu.SemaphoreType.DMA((N_DEV,))],
        input_output_aliases={0: 0}, compiler_params=_params(),
    )(slabs, memq, dkh, dvh)


def _place():
    return lax.axis_index("x"), lax.axis_index("y"), lax.axis_index("c")


def _all_gather_call(arrs, pieces, name, swap=None):
    n_in, n_p = len(arrs), len(pieces)
    n_sw = 0 if swap is None else 1

    def body(*refs):
        ins, outs = refs[:n_in], refs[n_in + n_sw:n_in + n_sw + n_p]
        send_sems, recv_sems, local_sems = refs[n_in + 2 * n_sw + n_p:n_in + 2 * n_sw + n_p + 3]
        swapped = []
        if swap is not None:
            swapped = swap[1](refs[n_in], refs[n_in + n_sw + n_p], *refs[-2:])
            for cp in swapped:
                cp.start()
        x, y, c = _place()
        me, sibling = (x, y, c), (x, y, 1 - c)
        chips = [(1 - x, y), (x, 1 - y), (1 - x, 1 - y)]

        def src(a):
            idx, r0, rows = pieces[a]
            return ins[idx] if r0 is None else ins[idx].at[pl.ds(r0, rows)]

        def slab(a, p):
            return outs[a].at[4 * p[0] + 2 * p[1] + p[2]]

        def copy(a, k, block, to, own=False):
            return pltpu.make_async_remote_copy(
                src_ref=src(a) if own else slab(a, block), dst_ref=slab(a, block),
                send_sem=send_sems.at[a, k], recv_sem=recv_sems.at[a, k], device_id=to, device_id_type=MESH)

        mine = [pltpu.make_async_copy(src(a), slab(a, me), local_sems.at[a]) for a in range(n_p)]
        for cp in mine:
            cp.start()
        first = []
        for a in range(n_p):
            first.append(copy(a, 0, me, sibling, own=True))
            first += [copy(a, 1 + j, me, (*chip, c), own=True) for j, chip in enumerate(chips)]
        for cp in first:
            cp.start()
        passed = []
        for a in range(n_p):
            for j, chip in enumerate(chips):
                copy(a, 1 + j, (*chip, c), me).wait_recv()
                cp = copy(a, 4 + j, (*chip, c), sibling)
                cp.start()
                passed.append(cp)
        for a in range(n_p):
            copy(a, 0, sibling, me).wait_recv()
            for j, chip in enumerate(chips):
                copy(a, 4 + j, (*chip, 1 - c), me).wait_recv()
        for cp in first + passed:
            cp.wait_send()
        for cp in mine:
            cp.wait()
        for cp in swapped:
            cp.wait()

    def out_shape(piece):
        idx, r0, rows = piece
        a = arrs[idx]
        return jax.ShapeDtypeStruct((N_DEV,) + (a.shape if r0 is None else (rows,) + a.shape[1:]), a.dtype)

    hbm = pl.BlockSpec(memory_space=pl.ANY)
    out_shapes = [out_shape(p) for p in pieces]
    scratch = [pltpu.SemaphoreType.DMA((n_p, 7)), pltpu.SemaphoreType.DMA((n_p, 7)), pltpu.SemaphoreType.DMA((n_p,))]
    args = list(arrs)
    if swap is not None:
        a, _, n = swap
        args.append(a)
        out_shapes.append(jax.ShapeDtypeStruct((n,) + a.shape[1:], a.dtype))
        scratch += [pltpu.SemaphoreType.DMA((n,)), pltpu.SemaphoreType.DMA((n,))]
    return pl.pallas_call(
        body, name=name, in_specs=[hbm] * len(args), out_specs=[hbm] * len(out_shapes), out_shape=out_shapes,
        scratch_shapes=scratch,
    )(*args)


def _flip(v, f):
    return 1 - v if f else v


def _gather_exchange(arrs, pieces):
    n_peers = N_DEV - 1

    def build(ins, outs, send_sems, recv_sems, local_sems):
        x, y, c = _place()
        flips = [(fx, fy, fc) for fx in (0, 1) for fy in (0, 1) for fc in (0, 1) if fx or fy or fc]
        remote, local = [], []
        for a, (idx, r0, rows) in enumerate(pieces):
            src = ins[idx] if r0 is None else ins[idx].at[pl.ds(r0, rows)]
            dst = outs[a].at[4 * x + 2 * y + c]
            remote += [pltpu.make_async_remote_copy(
                src_ref=src, dst_ref=dst, send_sem=send_sems.at[n_peers * a + k], recv_sem=recv_sems.at[n_peers * a + k],
                device_id=(_flip(x, fx), _flip(y, fy), _flip(c, fc)), device_id_type=MESH)
                for k, (fx, fy, fc) in enumerate(flips)]
            local.append(pltpu.make_async_copy(src, dst, local_sems.at[a]))
        return remote, local

    def out_shape(piece):
        idx, r0, rows = piece
        a = arrs[idx]
        return jax.ShapeDtypeStruct((N_DEV,) + (a.shape if r0 is None else (rows,) + a.shape[1:]), a.dtype)

    return _Exchange(arrs, [out_shape(p) for p in pieces], n_peers * len(pieces), build, n_local=len(pieces))


def _swap_core_copies(g_ref, r_ref, send_sems, recv_sems):
    x, y, c = _place()
    return [pltpu.make_async_remote_copy(
        src_ref=g_ref.at[2 * k + (1 - c)], dst_ref=r_ref.at[k], send_sem=send_sems.at[k], recv_sem=recv_sems.at[k],
        device_id=(x, y, 1 - c), device_id_type=MESH) for k in range(4)]


def _swap_chip_copies(p_ref, r_ref, send_sems, recv_sems):
    x, y, c = _place()
    chips = [(1 - x, y), (x, 1 - y), (1 - x, 1 - y)]
    return [pltpu.make_async_remote_copy(
        src_ref=p_ref.at[2 * px + py], dst_ref=r_ref.at[j], send_sem=send_sems.at[j], recv_sem=recv_sems.at[j],
        device_id=(px, py, c), device_id_type=MESH) for j, (px, py) in enumerate(chips)]


def _swap_exchange(a, copies, n):
    return _Exchange([a], [jax.ShapeDtypeStruct((n,) + a.shape[1:], a.dtype)], n,
                     lambda ins, outs, send_sems, recv_sems, local_sems: (copies(ins[0], outs[0], send_sems, recv_sems), []))


def _swap_call(a, copies, n, name):
    def body(a_ref, r_ref, send_sems, recv_sems):
        cps = copies(a_ref, r_ref, send_sems, recv_sems)
        for cp in cps:
            cp.start()
        for cp in cps:
            cp.wait()

    hbm = pl.BlockSpec(memory_space=pl.ANY)
    return pl.pallas_call(
        body, name=name, in_specs=[hbm], out_specs=hbm, out_shape=jax.ShapeDtypeStruct((n,) + a.shape[1:], a.dtype),
        scratch_shapes=[pltpu.SemaphoreType.DMA((n,)), pltpu.SemaphoreType.DMA((n,))],
    )(a)


ADD_ROWS = 184


def _pair_add_call(g, r, c):
    _, rows, width = g.shape

    def body(c_ref, g_ref, r_ref, o_ref):
        o_ref[...] = (g_ref[...] + r_ref[...]).astype(BF16)

    return pl.pallas_call(
        body, name="rs_pair_add",
        grid_spec=pltpu.PrefetchScalarGridSpec(
            num_scalar_prefetch=1, grid=(4, rows // ADD_ROWS),
            in_specs=[pl.BlockSpec((None, ADD_ROWS, width), lambda k, i, c_ref: (2 * k + c_ref[0], i, 0)),
                      pl.BlockSpec((None, ADD_ROWS, width), lambda k, i, c_ref: (k, i, 0))],
            out_specs=pl.BlockSpec((None, ADD_ROWS, width), lambda k, i, c_ref: (k, i, 0))),
        out_shape=jax.ShapeDtypeStruct((4, rows, width), BF16),
        compiler_params=_params(dimension_semantics=("arbitrary", "arbitrary")),
    )(c, g, r)


def _adam(w, g, m, v):
    mn = ADAM_B1 * m + (1.0 - ADAM_B1) * g
    vn = ADAM_B2 * v + (1.0 - ADAM_B2) * (g * g)
    m_hat = mn / (1.0 - ADAM_B1 ** ADAM_STEP)
    v_hat = vn / (1.0 - ADAM_B2 ** ADAM_STEP)
    return -ADAM_LR * (m_hat / (jnp.sqrt(v_hat) + ADAM_EPS) + ADAM_WD * w), mn, vn


def _chip_add_adamw_call(slabs, from_sibling, from_chips, me, chip, w, m, v):
    _, rows, width = slabs.shape

    def body(me_ref, chip_ref, own_ref, sib_ref, r_ref, w_ref, m_ref, v_ref, g_ref, d_ref, mo_ref, vo_ref):
        g = own_ref[...] + sib_ref[...]
        for j in range(3):
            g = g + r_ref[j].astype(F32)
        g_ref[...] = g
        d_ref[...], mo_ref[...], vo_ref[...] = _adam(w_ref[...], g, m_ref[...], v_ref[...])

    spec = pl.BlockSpec((ADD_ROWS, width), lambda i, me_ref, chip_ref: (i, 0))
    return pl.pallas_call(
        body, name="rs_chip_add_adamw",
        grid_spec=pltpu.PrefetchScalarGridSpec(
            num_scalar_prefetch=2, grid=(rows // ADD_ROWS,),
            in_specs=[pl.BlockSpec((None, ADD_ROWS, width), lambda i, me_ref, chip_ref: (me_ref[0], i, 0)),
                      pl.BlockSpec((None, ADD_ROWS, width), lambda i, me_ref, chip_ref: (chip_ref[0], i, 0)),
                      pl.BlockSpec((3, ADD_ROWS, width), lambda i, me_ref, chip_ref: (0, i, 0)), spec, spec, spec],
            out_specs=[spec] * 4),
        out_shape=[jax.ShapeDtypeStruct((rows, width), F32)] * 4,
        compiler_params=_params(dimension_semantics=("arbitrary",)),
    )(me, chip, slabs, from_sibling, from_chips, w, m, v)


def _adamw_whole_call(params, name):
    n = len(params)

    def body(*refs):
        ins, outs = refs[:4 * n], refs[4 * n:]
        for a in range(n):
            w_ref, g_ref, m_ref, v_ref = ins[4 * a:4 * a + 4]
            outs[3 * a][...], outs[3 * a + 1][...], outs[3 * a + 2][...] = _adam(w_ref[...], g_ref[...], m_ref[...],
                                                                              v_ref[...])

    flat = [a for p in params for a in p]
    shapes = [jax.ShapeDtypeStruct(p[0].shape, F32) for p in params for _ in range(3)]
    out = pl.pallas_call(body, name=name, out_shape=shapes, compiler_params=_params())(*flat)
    return [tuple(out[3 * a:3 * a + 3]) for a in range(n)]


GATHERED_ACCS = (("dva", (8, CONV_W)), ("dv1", (8, D_MODEL)), ("dv2", (8, D_MODEL)), ("dcf", (8, D_FF_PAD)),
                 ("dcw", (HALO, CONV_W)), ("dwcat", (CHUNK, GMLP_HEADS * CHUNK)), ("dmsum", (CHUNK, GMLP_W)))
VEC_A = ("conv_a_b", "ln_a_g", "ln_a_b", "ln_v_g", "ln_v_b")
REP_IN_KERNEL = VEC_A + ("ln1_g", "ln1_b", "ln2_g", "ln2_b", "w_s", "b_s")


def _replicated_update_call(gathered, p, mom_m, mom_v):
    n_acc = len(GATHERED_ACCS)
    n_rep = len(REP_IN_KERNEL)

    def body(*refs):
        acc_refs = refs[:DEPTH * n_acc]
        wmv = refs[DEPTH * n_acc:DEPTH * n_acc + 3 * n_rep]
        outs = refs[DEPTH * n_acc + 3 * n_rep:]
        out_par = {nm: outs[4 * a:4 * a + 4] for a, nm in enumerate(REP_IN_KERNEL)}
        out_dcf = outs[4 * n_rep:4 * n_rep + DEPTH]
        out_dcw = outs[4 * n_rep + DEPTH:4 * n_rep + 2 * DEPTH]
        par = {nm: wmv[3 * a:3 * a + 3] for a, nm in enumerate(REP_IN_KERNEL)}
        tril = (lax.broadcasted_iota(jnp.int32, (CHUNK, CHUNK), 0) >= lax.broadcasted_iota(jnp.int32, (CHUNK, CHUNK), 1))
        head = lax.broadcasted_iota(jnp.int32, (8, GMLP_W), 0) * HEAD_DIM
        lane = lax.broadcasted_iota(jnp.int32, (8, GMLP_W), 1)
        sel = jnp.where((lane >= head) & (lane < head + HEAD_DIM), 1.0, 0.0)

        def update(nm, idx, g):
            w_ref, m_ref, v_ref = par[nm]
            d, mn, vn = _adam(w_ref[idx], g, m_ref[idx], v_ref[idx])
            g_ref, d_ref, mo_ref, vo_ref = out_par[nm]
            g_ref[idx] = g
            d_ref[idx] = d
            mo_ref[idx] = mn
            vo_ref[idx] = vn

        for l in range(DEPTH):
            tot = {}
            for a, (nm, _) in enumerate(GATHERED_ACCS):
                ref = acc_refs[l * n_acc + a]
                s = ref[0]
                for j in range(1, N_DEV):
                    s = s + ref[j]
                tot[nm] = s
            out_dcf[l][...] = tot["dcf"]
            out_dcw[l][...] = tot["dcw"]
            row = (slice(l, l + 1), slice(None))
            for k, nm in enumerate(VEC_A):
                update(nm, row, tot["dva"][k:k + 1, :])
            update("ln1_g", row, tot["dv1"][0:1, :])
            update("ln1_b", row, tot["dv1"][1:2, :])
            update("ln2_g", row, tot["dv2"][0:1, :])
            update("ln2_b", row, tot["dv2"][1:2, :])
            for h in range(GMLP_HEADS):
                gw = jnp.where(tril, tot["dwcat"][:, h * CHUNK:(h + 1) * CHUNK], 0.0)
                update("w_s", (l, h), gw)
            gb = lax.dot_general(sel, tot["dmsum"], (((1,), (1,)), ((), ())), precision=lax.Precision.HIGHEST,
                                 preferred_element_type=F32)
            for h in range(GMLP_HEADS):
                update("b_s", (l, slice(h, h + 1), slice(None)), gb[h:h + 1, :])

    ins = [gathered[l][nm] for l in range(DEPTH) for nm, _ in GATHERED_ACCS]
    ins += [t[nm] for nm in REP_IN_KERNEL for t in (p, mom_m, mom_v)]
    shapes = [jax.ShapeDtypeStruct(p[nm].shape, F32) for nm in REP_IN_KERNEL for _ in range(4)]
    shapes += [jax.ShapeDtypeStruct((8, D_FF_PAD), F32)] * DEPTH + [jax.ShapeDtypeStruct((HALO, CONV_W), F32)] * DEPTH
    out = pl.pallas_call(body, name="replicated_update", out_shape=shapes, compiler_params=_params())(*ins)
    res = [{nm: out[4 * a + k] for a, nm in enumerate(REP_IN_KERNEL)} for k in range(4)]
    return res, out[4 * n_rep:4 * n_rep + DEPTH], out[4 * n_rep + DEPTH:]


BLOCK_ROWS = (("w_in", IN_W // N_DEV), ("w_out", D_MODEL // N_DEV), ("w_up", 2 * FF_GROUP_PAD),
              ("w_down", FF_GROUP_PAD), ("w_mk", D_MODEL // N_DEV // MEM_FOLD), ("w_mv", D_MODEL // N_DEV // MEM_FOLD))
LAYER_ROWS = sum(r for _, r in BLOCK_ROWS)
assert LAYER_ROWS % ADD_ROWS == 0 and all(r % 16 == 0 for _, r in BLOCK_ROWS)


def _row_off(name):
    off = 0
    for nm, r in BLOCK_ROWS:
        if nm == name:
            return off
        off += r
    raise KeyError(name)


def _to_rows(name, a):
    if name == "w_in":
        return a.T
    if name == "w_up":
        t = a.T.reshape(2, FF_GROUP, D_MODEL)
        return jnp.pad(t, ((0, 0), (0, FF_GROUP_PAD - FF_GROUP), (0, 0))).reshape(2 * FF_GROUP_PAD, D_MODEL)
    if name == "w_down":
        return jnp.pad(a, ((0, FF_GROUP_PAD - FF_GROUP), (0, 0)))
    if name == "w_out":
        return a
    return a.reshape(-1, BLOB_LANES)


def _from_rows(name, r):
    if name == "w_in":
        return r.T
    if name == "w_up":
        return r.reshape(2, FF_GROUP_PAD, D_MODEL)[:, :FF_GROUP].reshape(2 * FF_GROUP, D_MODEL).T
    if name == "w_down":
        return r[:FF_GROUP]
    if name == "w_out":
        return r
    return r.reshape(D_MODEL // N_DEV, XATTN_W)


def _blob(tree, l):
    return jnp.concatenate([_to_rows(nm, tree[nm][l]) for nm, _ in BLOCK_ROWS], axis=0)


def _unblob(blobs):
    return {nm: jnp.stack([_from_rows(nm, b[_row_off(nm):_row_off(nm) + r]) for b in blobs]) for nm, r in BLOCK_ROWS}


def _ff_interleave(a):
    lead = a.shape[:-1]
    t = a.reshape(lead + (N_DEV, FF_GROUP))
    return jnp.pad(t, [(0, 0)] * len(lead) + [(0, 0), (0, FF_GROUP_PAD - FF_GROUP)]).reshape(lead + (D_FF_PAD,))


def _ff_deinterleave(a):
    lead = a.shape[:-1]
    return a.reshape(lead + (N_DEV, FF_GROUP_PAD))[..., :FF_GROUP].reshape(lead + (D_FF,))


def _head_table():
    hd = jnp.arange(XATTN_W) // HEAD_DIM
    return (hd[None, :] == jnp.arange(XATTN_HEADS)[:, None]).astype(F32)


def _mixer_operands(mat, conv_a_w, p, l, memq):
    w = {}
    w["wint"] = mat["w_in"]
    w["win"] = mat["w_in"].T
    w["wout"] = mat["w_out"]
    w["woutt"] = mat["w_out"].T
    w["cw"] = conv_a_w
    zeros = jnp.zeros((3, CONV_W), F32)
    w["va"] = jnp.concatenate([p[nm][l][None] for nm in VEC_A] + [zeros], axis=0)
    tril = jnp.tril(jnp.ones((CHUNK, CHUNK), F32))
    w["wcat"] = (p["w_s"][l] * tril[None]).transpose(1, 0, 2).reshape(CHUNK, GMLP_HEADS * CHUNK).astype(BF16)
    w["wcatt"] = w["wcat"].T
    w["bfull"] = jnp.repeat(p["b_s"][l].T, HEAD_DIM, axis=1)
    kh, vh = _mem_proj_call(memq, mat["w_mk"], mat["w_mv"])
    hm = _head_table()
    scale = 1.0 / math.sqrt(HEAD_DIM)
    w["kt"] = (kh.T[:, None, :] * hm.T[:, :, None] * scale).reshape(XATTN_W, XATTN_HEADS * N_MEM).astype(BF16)
    w["ktt"] = w["kt"].T
    w["vm"] = (hm[:, None, :] * vh[None]).reshape(XATTN_HEADS * N_MEM, XATTN_W).astype(BF16)
    w["vmt"] = w["vm"].T
    zeros = jnp.zeros((6, D_MODEL), F32)
    w["v1"] = jnp.concatenate([p["ln1_g"][l][None], p["ln1_b"][l][None], zeros], axis=0)
    return w


def _ffn_operands(w_up, w_down, conv_f_w, p, l):
    w = {}
    w["wgt"] = w_up[:D_FF_PAD]
    w["wvt"] = w_up[D_FF_PAD:]
    w["wg"] = w["wgt"].T
    w["wv"] = w["wvt"].T
    w["wdown"] = w_down
    w["wdownt"] = w_down.T
    zeros = jnp.zeros((6, D_MODEL), F32)
    w["v2"] = jnp.concatenate([p["ln2_g"][l][None], p["ln2_b"][l][None], zeros], axis=0)
    w["cf"] = jnp.concatenate([conv_f_w, _ff_interleave(p["conv_f_b"][l][None]), jnp.zeros((4, D_FF_PAD), F32)], axis=0)
    return w


TS_MIXER = 256
TS_FFN = 256
TS_PROJ = 512
CONV_A_SHARD = CONV_W // N_DEV


def kernel(x, mem, w_in, conv_a_w, conv_a_b, ln_a_g, ln_a_b, ln_v_g, ln_v_b, w_s, b_s, w_mk, w_mv, w_out, ln1_g, ln1_b, w_up, conv_f_w, conv_f_b, w_down, ln2_g, ln2_b, loss_target, m_w_in, m_conv_a_w, m_conv_a_b, m_ln_a_g, m_ln_a_b, m_ln_v_g, m_ln_v_b, m_w_s, m_b_s, m_w_mk, m_w_mv, m_w_out, m_ln1_g, m_ln1_b, m_w_up, m_conv_f_w, m_conv_f_b, m_w_down, m_ln2_g, m_ln2_b, v_w_in, v_conv_a_w, v_conv_a_b, v_ln_a_g, v_ln_a_b, v_ln_v_g, v_ln_v_b, v_w_s, v_b_s, v_w_mk, v_w_mv, v_w_out, v_ln1_g, v_ln1_b, v_w_up, v_conv_f_w, v_conv_f_b, v_w_down, v_ln2_g, v_ln2_b):
    given = dict(locals())
    p = {nm: given[nm] for nm in WEIGHTS}
    mom_m = {nm: given["m_" + nm] for nm in WEIGHTS}
    mom_v = {nm: given["v_" + nm] for nm in WEIGHTS}
    seq = x.shape[1]
    ts_m, ts_f, ts_p = min(TS_MIXER, seq), min(TS_FFN, seq), min(TS_PROJ, seq)
    cx, cy, cc = _place()
    me = 4 * cx + 2 * cy + cc

    blobs = [_blob(p, l) for l in range(DEPTH)]
    blobs_bf = [b.astype(BF16) for b in blobs]
    conv_a_tile = jnp.pad(conv_a_w, ((0, 0), (0, HALO - CONV_K), (0, 128 - CONV_A_SHARD)))
    conv_f_tile = jnp.pad(conv_f_w, ((0, 0), (0, 8 - FFN_CONV_K), (0, 384 - FF_GROUP)))
    rows = dict(BLOCK_ROWS)
    mixer_names = ("w_in", "w_out", "w_mk", "w_mv")
    pieces = [(0, _row_off(nm), rows[nm]) for nm in mixer_names] + [(1, None, 0), (2, None, 0)]
    first = _all_gather_call([blobs_bf[0], conv_a_tile, conv_f_tile], pieces, "gather_weights")
    conv_a_all, conv_f_all = first[len(mixer_names)], first[len(mixer_names) + 1]
    conv_a = [conv_a_all[:, l, :, :CONV_A_SHARD].transpose(1, 0, 2).reshape(HALO, CONV_W) for l in range(DEPTH)]
    conv_f = [conv_f_all[:, l, :FFN_CONV_K, :FF_GROUP_PAD].transpose(1, 0, 2).reshape(FFN_CONV_K, D_FF_PAD)
              for l in range(DEPTH)]
    memq = mem[0].reshape(N_MEM, D_MODEL // MEM_FOLD, MEM_FOLD).transpose(2, 0, 1).astype(BF16)

    def matrix(gathered, nm, base):
        lo = _row_off(nm) - base
        return gathered[:, lo:lo + rows[nm]].reshape(-1, BLOB_LANES)

    ffn_base = _row_off("w_up")
    ops0 = _mixer_operands({nm: first[a].reshape(-1, BLOB_LANES) for a, nm in enumerate(mixer_names)}, conv_a[0], p, 0,
                           memq)
    mixer_saved = ("hb", "z1", "x1", "a1", "cat", "p", "gs", "rv")
    mixed0, (ffn0,) = _mixer_fwd_call(
        x[0], ops0, ts_m, _gather_exchange([blobs_bf[0]], [(0, ffn_base, rows["w_up"] + rows["w_down"])]))
    ops0.update(_ffn_operands(matrix(ffn0, "w_up", ffn_base), matrix(ffn0, "w_down", ffn_base), conv_f[0], p, 0))
    saved = [dict(zip(mixer_saved, mixed0), x=x[0])]
    (ug, uv, sl, dsl, z2, x2), (all1,) = _ffn_fwd_call(saved[0]["x1"], ops0, ts_f,
                                                       _gather_exchange([blobs_bf[1]], [(0, 0, LAYER_ROWS)]))
    saved[0].update(ug=ug, uv=uv, sl=sl, dsl=dsl, z2=z2)
    ops1 = _mixer_operands({nm: matrix(all1, nm, 0) for nm in mixer_names}, conv_a[1], p, 1, memq)
    ops1.update(_ffn_operands(matrix(all1, "w_up", 0), matrix(all1, "w_down", 0), conv_f[1], p, 1))
    mixed1, _ = _mixer_fwd_call(x2, ops1, ts_m)
    saved.append(dict(zip(mixer_saved, mixed1), x=x2))
    (ug, uv, sl, dsl, z2, _), _ = _ffn_fwd_call(saved[1]["x1"], ops1, ts_f)
    saved[1].update(ug=ug, uv=uv, sl=sl, dsl=dsl, z2=z2)
    ops = [ops0, ops1]

    hm = _head_table()
    core_id = cc.reshape(1).astype(jnp.int32)
    slabs = [lax.empty((N_DEV, LAYER_ROWS, BLOB_LANES), F32) for _ in range(DEPTH)]
    accs, gathered_accs = [None] * DEPTH, [None] * DEPTH
    acc_names = [nm for nm, _ in GATHERED_ACCS]
    whole = [(a, None, 0) for a in range(len(acc_names))]

    def acc_list(l):
        return [accs[l][nm] for nm in acc_names]

    from_sibling, from_chips = [None] * DEPTH, [None] * DEPTH
    dx = loss_target[0]
    loss = None
    for l in reversed(range(DEPTH)):
        s, w = saved[l], ops[l]
        last = l == DEPTH - 1
        ride = None if last else _swap_exchange(slabs[l + 1], _swap_core_copies, 4)
        (sl, dug, duv, dz2, dcf, dv2, loss_acc), got = _ffn_bwd_call(
            dx, s["z2"], s["ug"], s["uv"], s["sl"], s["dsl"], w, ts_f, last, slabs[l], _row_off("w_down"), ride)
        if last:
            loss = loss_acc[0, 0]
        else:
            from_sibling[l + 1] = got[0]
            chip_sum = _pair_add_call(slabs[l + 1], from_sibling[l + 1], core_id)
        off_up = _row_off("w_up")
        ride = None if last else _gather_exchange(acc_list(l + 1), whole)
        (sl, dxa), got = _proj_bwd_call(dug, w["wgt"], s["x1"], dz2, ALPHA, ts_p, "up_gate_bwd", sl, off_up, 0, 4, ride)
        if not last:
            gathered_accs[l + 1] = dict(zip(acc_names, got))
        ride = None if last else _swap_exchange(chip_sum, _swap_chip_copies, 3)
        (sl, dx1), got = _proj_bwd_call(duv, w["wvt"], s["x1"], dxa, 1.0, ts_p, "up_val_bwd", sl, off_up, 4, 4, ride)
        if not last:
            from_chips[l + 1] = got[0]
        (sl, dx, dkt, dvm, dwcat, dmsum, dva, dcw, dv1) = _mixer_bwd_call(
            dx1, s["z1"], s["hb"], s["a1"], s["cat"], s["p"], s["gs"], s["rv"], s["x"], w, ts_m, sl, _row_off("w_out"),
            _row_off("w_in"))
        dkh = jnp.einsum("hd,dhm->md", hm, dkt.reshape(XATTN_W, XATTN_HEADS, N_MEM)) * (1.0 / math.sqrt(HEAD_DIM))
        dvh = jnp.einsum("hd,hmd->md", hm, dvm.reshape(XATTN_HEADS, N_MEM, XATTN_W))
        slabs[l] = _mem_proj_bwd_call(memq, dkh, dvh, sl, _row_off("w_mk"), _row_off("w_mv"))
        accs[l] = dict(dva=dva, dv1=dv1, dv2=dv2, dcf=dcf, dcw=dcw, dwcat=dwcat, dmsum=dmsum)
    grad_x = dx[None]
    *got, from_sibling[0] = _all_gather_call(acc_list(0), whole, "gather_small_grads",
                                            swap=(slabs[0], _swap_core_copies, 4))
    gathered_accs[0] = dict(zip(acc_names, got))
    chip_sum = _pair_add_call(slabs[0], from_sibling[0], core_id)
    from_chips[0] = _swap_call(chip_sum, _swap_chip_copies, 3, "rs_swap_chip")
    me_id, chip_id = me.reshape(1).astype(jnp.int32), (2 * cx + cy).reshape(1).astype(jnp.int32)
    per_layer = [_chip_add_adamw_call(slabs[l], from_sibling[l], from_chips[l], me_id, chip_id, blobs[l],
                                      _blob(mom_m, l), _blob(mom_v, l)) for l in range(DEPTH)]
    outs = [_unblob([per_layer[l][k] for l in range(DEPTH)]) for k in range(4)]

    rep, dcf_sum, dcw_sum = _replicated_update_call(gathered_accs, p, mom_m, mom_v)
    for k in range(4):
        outs[k].update(rep[k])
    dcf_sum, dcw_sum = jnp.stack(dcf_sum), jnp.stack(dcw_sum)
    zero = jnp.zeros((), jnp.int32)
    g_conv_a_w = lax.dynamic_slice(dcw_sum, (zero, zero, CONV_A_SHARD * me), (DEPTH, CONV_K, CONV_A_SHARD))
    g_conv_f_w = lax.dynamic_slice(dcf_sum, (zero, zero, FF_GROUP_PAD * me), (DEPTH, FFN_CONV_K, FF_GROUP))
    g_conv_f_b = _ff_deinterleave(dcf_sum[:, FFN_CONV_K])
    conv_grads = dict(conv_a_w=g_conv_a_w, conv_f_w=g_conv_f_w, conv_f_b=g_conv_f_b)
    conv_names = tuple(conv_grads)
    upd = _adamw_whole_call([(p[nm], conv_grads[nm], mom_m[nm], mom_v[nm]) for nm in conv_names], "adamw_conv")
    for nm, (d, mn, vn) in zip(conv_names, upd):
        outs[0][nm], outs[1][nm], outs[2][nm], outs[3][nm] = conv_grads[nm], d, mn, vn

    loss = lax.psum(loss, ("x", "y", "c"))
    return (loss, grad_x, *[outs[0][nm] for nm in WEIGHTS], *[outs[1][nm] for nm in WEIGHTS],
            *[outs[2][nm] for nm in WEIGHTS], *[outs[3][nm] for nm in WEIGHTS])
```

```python
import math

import jax
import jax.numpy as jnp
from jax import lax
from jax.experimental import pallas as pl
from jax.experimental.pallas import tpu as pltpu

F32 = jnp.float32
BF16 = jnp.bfloat16

DEPTH = 2
D_MODEL = 1024
CONV_W = 384
GMLP_W = 384
XATTN_W = 256
HEAD_DIM = 64
GMLP_HEADS = 6
XATTN_HEADS = 4
IN_W = 1792
CONV_K = 31
CHUNK = 128
N_MEM = 256
D_FF = 2752
D_FF_PAD = 2816
FFN_CONV_K = 3
ALPHA = (2.0 * DEPTH) ** 0.25
LN_EPS = 1e-5
N_DEV = 8

ADAM_LR = 0.001
ADAM_B1 = 0.9
ADAM_B2 = 0.999
ADAM_EPS = 1e-08
ADAM_WD = 0.01
ADAM_STEP = 10

HALO = 32
CONV_ROWS = 32
V7X_VMEM_BYTES = 64 * 1024 * 1024
VMEM_LIMIT = V7X_VMEM_BYTES - 8 * 1024 * 1024
BLOB_LANES = 1024

MESH = pl.DeviceIdType.MESH

WEIGHTS = ("w_in", "conv_a_w", "conv_a_b", "ln_a_g", "ln_a_b", "ln_v_g", "ln_v_b", "w_s", "b_s", "w_mk", "w_mv",
           "w_out", "ln1_g", "ln1_b", "w_up", "conv_f_w", "conv_f_b", "w_down", "ln2_g", "ln2_b")


def _params(**kw):
    return pltpu.CompilerParams(vmem_limit_bytes=VMEM_LIMIT, **kw)


def _const(shape):
    nd = len(shape)
    return pl.BlockSpec(shape, lambda i: (0,) * nd, pipeline_mode=pl.Buffered(1))


def _acc(shape):
    nd = len(shape)
    return pl.BlockSpec(shape, lambda i: (0,) * nd)


class _Exchange:
    def __init__(self, arrays, out_shapes, n_copies, build, n_local=1):
        self.arrays, self.out_shapes, self.n_copies, self.build = list(arrays), list(out_shapes), n_copies, build
        self.n_local = n_local


def _carry(core, n_in, n_out, exch, n_steps):
    if exch is None:
        return core
    nx_in, nx_out = len(exch.arrays), len(exch.out_shapes)

    def body(*refs):
        o0 = n_in + nx_in
        s0 = o0 + n_out + nx_out
        x_in, x_out, sems = refs[n_in:o0], refs[o0 + n_out:s0], refs[-3:]
        i = pl.program_id(0)

        @pl.when(i == 0)
        def _():
            remote, local = exch.build(x_in, x_out, *sems)
            for cp in remote + local:
                cp.start()

        core(*refs[:n_in], *refs[o0:o0 + n_out], *refs[s0:-3])

        @pl.when(i == n_steps - 1)
        def _():
            remote, local = exch.build(x_in, x_out, *sems)
            for cp in remote + local:
                cp.wait()

    return body


def _grid_call(core, name, n_steps, in_specs, out_specs, out_shape, scratch_shapes, args, aliases=None, exch=None):
    hbm = pl.BlockSpec(memory_space=pl.ANY)
    n_in, n_out = len(in_specs), len(out_specs)
    in_specs, out_specs, out_shape, scratch_shapes, args = (list(in_specs), list(out_specs), list(out_shape),
                                                            list(scratch_shapes), list(args))
    if exch is not None:
        in_specs += [hbm] * len(exch.arrays)
        out_specs += [hbm] * len(exch.out_shapes)
        out_shape += exch.out_shapes
        scratch_shapes += [pltpu.SemaphoreType.DMA((exch.n_copies,)), pltpu.SemaphoreType.DMA((exch.n_copies,)),
                           pltpu.SemaphoreType.DMA((exch.n_local,))]
        args += exch.arrays
    out = pl.pallas_call(
        _carry(core, n_in, n_out, exch, n_steps), name=name, grid=(n_steps,), in_specs=in_specs, out_specs=out_specs,
        out_shape=out_shape, scratch_shapes=scratch_shapes, input_output_aliases=aliases or {},
        compiler_params=_params(dimension_semantics=("arbitrary",)))(*args)
    return list(out[:n_out]), list(out[n_out:])


def _sigmoid(x):
    return 1.0 / (1.0 + jnp.exp(-x))


_GELU_C = math.sqrt(2.0 / math.pi)


def _gelu(x):
    x2 = x * x
    t = jnp.tanh(_GELU_C * (x + 0.044715 * x * x2))
    g = 0.5 * x * (1.0 + t)
    dg = 0.5 * (1.0 + t) + 0.5 * x * (1.0 - t * t) * (_GELU_C * (1.0 + 3.0 * 0.044715 * x2))
    return g, dg


def _ln_stats(z):
    mu = jnp.mean(z, axis=-1, keepdims=True)
    zc = z - mu
    var = jnp.mean(zc * zc, axis=-1, keepdims=True)
    r = lax.rsqrt(var + LN_EPS)
    return zc * r, r


def _ln_bwd(dy, xh, r, g):
    dxh = dy * g
    m1 = jnp.mean(dxh, axis=-1, keepdims=True)
    m2 = jnp.mean(dxh * xh, axis=-1, keepdims=True)
    return r * (dxh - m1 - xh * m2)


def _rowsum(x):
    return jnp.sum(x, axis=0, keepdims=True)


def _dot(a, b):
    return jnp.dot(a, b, preferred_element_type=F32)


def _dot_tn(a, b):
    return lax.dot_general(a, b, (((0,), (0,)), ((), ())), preferred_element_type=F32)


def _dot_nt(a, b):
    return lax.dot_general(a, b, (((1,), (1,)), ((), ())), preferred_element_type=F32)


def _shift_copies(buf, sh, rows):
    for b in range(1, 8):
        sh[b - 1, 0:rows, :] = buf[b:b + rows, :]


def _window(buf, sh, start):
    b = start % 8
    a = start - b
    return buf[a:a + CONV_ROWS, :] if b == 0 else sh[b - 1, a:a + CONV_ROWS, :]


def _conv31_fwd(buf, sh, w_ref, bias, out, ts):
    for r0 in range(0, ts, CONV_ROWS):
        acc = jnp.broadcast_to(bias, (CONV_ROWS, CONV_W))
        for k in range(CONV_K):
            acc = acc + w_ref[k:k + 1, :] * _window(buf, sh, r0 + HALO - (CONV_K - 1) + k)
        out[r0:r0 + CONV_ROWS, :] = acc


def _conv31_dx(dbuf, dsh, w_ref, out, ts):
    for r0 in range(0, ts, CONV_ROWS):
        acc = jnp.zeros((CONV_ROWS, CONV_W), F32)
        for k in range(CONV_K):
            acc = acc + w_ref[k:k + 1, :] * _window(dbuf, dsh, r0 + (CONV_K - 1) - k)
        out[r0:r0 + CONV_ROWS, :] = acc


def _conv31_dw(buf, sh, dbuf, dw_ref, ts):
    for k in range(CONV_K):
        part = jnp.zeros((8, CONV_W), F32)
        for r0 in range(0, ts, CONV_ROWS):
            m = dbuf[r0:r0 + CONV_ROWS, :] * _window(buf, sh, r0 + HALO - (CONV_K - 1) + k)
            for q in range(0, CONV_ROWS, 8):
                part = part + m[q:q + 8, :]
        dw_ref[k:k + 1, :] += _rowsum(part)


def _head_mask(width, h):
    lane = lax.broadcasted_iota(jnp.int32, (CHUNK, width), 1)
    return (lane >= h * HEAD_DIM) & (lane < (h + 1) * HEAD_DIM)


def _stack_heads(vn_c):
    return jnp.concatenate([jnp.where(_head_mask(GMLP_W, h), vn_c, 0.0) for h in range(GMLP_HEADS)], axis=0)


def _group_a_fwd(hf, buf, sh, a1_ref, cw_ref, va_ref, ts, conv=True):
    ha = hf[:, 0:CONV_W]
    sg = _sigmoid(hf[:, CONV_W:2 * CONV_W])
    buf[HALO:HALO + ts, :] = ha * sg
    _shift_copies(buf, sh, ts + HALO - 8)
    if conv:
        _conv31_fwd(buf, sh, cw_ref, va_ref[0:1, :], a1_ref, ts)
    a2h, ra = _ln_stats(a1_ref[...])
    a2 = a2h * va_ref[1:2, :] + va_ref[2:3, :]
    sa = _sigmoid(a2)
    return dict(ha=ha, sg=sg, a2h=a2h, ra=ra, a2=a2, sa=sa, a=a2 * sa)


GROUP_LANES = ((0, 2 * CONV_W), (2 * CONV_W, 2 * CONV_W + 2 * GMLP_W), (2 * CONV_W + 2 * GMLP_W, IN_W))


def _group_b_fwd(hf, va_ref, wcat_ref, bfull_ref, ts):
    hu = hf[:, 0:GMLP_W]
    hv = hf[:, GMLP_W:2 * GMLP_W]
    u, du = _gelu(hu)
    v, dv = _gelu(hv)
    vhat, rv = _ln_stats(v)
    vn = vhat * va_ref[3:4, :] + va_ref[4:5, :]
    stacks, mixed = [], []
    for c0 in range(0, ts, CHUNK):
        st = _stack_heads(vn[c0:c0 + CHUNK, :]).astype(BF16)
        stacks.append(st)
        mixed.append(_dot(wcat_ref[...], st) + bfull_ref[...])
    mixed = jnp.concatenate(mixed, axis=0) if len(mixed) > 1 else mixed[0]
    return dict(u=u, du=du, dv=dv, vhat=vhat, rv=rv, stacks=stacks, mixed=mixed, g=u * mixed)


GROUP_B_SAVED = ("u", "du", "dv", "vhat", "mixed")


def _group_c_fwd(qb, kt_ref, vm_ref):
    s_all = _dot(qb, kt_ref[...])
    ps = []
    for g in range(XATTN_HEADS):
        s = s_all[:, g * N_MEM:(g + 1) * N_MEM]
        e = jnp.exp(s - jnp.max(s, axis=-1, keepdims=True))
        ps.append(e / jnp.sum(e, axis=-1, keepdims=True))
    p_all = jnp.concatenate(ps, axis=1)
    pb = p_all.astype(BF16)
    return dict(qb=qb, p=p_all, pb=pb, c=_dot(pb, vm_ref[...]))


def _mixer_fwd_call(x, w, ts, exch=None):
    seq = x.shape[0]
    n = seq // ts

    def body(x_ref, win_ref, cw_ref, va_ref, wcat_ref, bfull_ref, kt_ref, vm_ref, wout_ref, v1_ref,
             hb_ref, z1_ref, x1_ref, a1buf, cat_ref, p_ref, gs_ref, rv_ref, buf, sh):
        i = pl.program_id(0)

        @pl.when(i == 0)
        def _():
            buf[0:HALO, :] = jnp.zeros((HALO, CONV_W), F32)

        @pl.when(i > 0)
        def _():
            buf[0:HALO, :] = buf[ts:ts + HALO, :]

        xv = x_ref[...]
        hb = _dot(xv.astype(BF16), win_ref[...]).astype(BF16)
        hb_ref[...] = hb
        hf = hb.astype(F32)
        (a_lo, a_hi), (b_lo, b_hi), (c_lo, c_hi) = GROUP_LANES
        ga = _group_a_fwd(hf[:, a_lo:a_hi], buf, sh, a1buf, cw_ref, va_ref, ts)
        gb = _group_b_fwd(hf[:, b_lo:b_hi], va_ref, wcat_ref, bfull_ref, ts)
        gc = _group_c_fwd(hb[:, c_lo:c_hi], kt_ref, vm_ref)
        cat = jnp.concatenate([ga["a"], gb["g"], gc["c"]], axis=1).astype(BF16)
        cat_ref[...] = cat
        p_ref[...] = gc["pb"]
        gs_ref[...] = jnp.concatenate([gb[k] for k in GROUP_B_SAVED], axis=1).astype(BF16)
        rv_ref[...] = jnp.broadcast_to(gb["rv"], (ts, 128))
        z1 = ALPHA * xv + _dot(cat, wout_ref[...])
        z1_ref[...] = z1
        xh, _ = _ln_stats(z1)
        x1_ref[...] = xh * v1_ref[0:1, :] + v1_ref[1:2, :]

    row = lambda width: pl.BlockSpec((ts, width), lambda i: (i, 0))
    return _grid_call(
        body, "mixer_fwd", n,
        in_specs=[row(D_MODEL), _const((D_MODEL, IN_W)), _const((HALO, CONV_W)), _const((8, CONV_W)),
                  _const((CHUNK, GMLP_HEADS * CHUNK)), _const((CHUNK, GMLP_W)), _const((XATTN_W, XATTN_HEADS * N_MEM)),
                  _const((XATTN_HEADS * N_MEM, XATTN_W)), _const((D_MODEL, D_MODEL)), _const((8, D_MODEL))],
        out_specs=[row(IN_W), row(D_MODEL), row(D_MODEL), row(CONV_W), row(D_MODEL), row(XATTN_HEADS * N_MEM),
                   row(len(GROUP_B_SAVED) * GMLP_W), row(128)],
        out_shape=[jax.ShapeDtypeStruct((seq, IN_W), BF16), jax.ShapeDtypeStruct((seq, D_MODEL), F32),
                   jax.ShapeDtypeStruct((seq, D_MODEL), F32), jax.ShapeDtypeStruct((seq, CONV_W), F32),
                   jax.ShapeDtypeStruct((seq, D_MODEL), BF16), jax.ShapeDtypeStruct((seq, XATTN_HEADS * N_MEM), BF16),
                   jax.ShapeDtypeStruct((seq, len(GROUP_B_SAVED) * GMLP_W), BF16),
                   jax.ShapeDtypeStruct((seq, 128), F32)],
        scratch_shapes=[pltpu.VMEM((ts + HALO, CONV_W), F32), pltpu.VMEM((7, ts + HALO, CONV_W), F32)],
        args=(x, w["win"], w["cw"], w["va"], w["wcat"], w["bfull"], w["kt"], w["vm"], w["wout"], w["v1"]), exch=exch)


def _store_blocks(acc, slabs_ref, sems, row_off, rows, first_block, n_blocks):
    copies = [pltpu.make_async_copy(acc.at[pl.ds(q * rows, rows)], slabs_ref.at[first_block + q, pl.ds(row_off, rows)],
                                    sems.at[q]) for q in range(n_blocks)]
    for cp in copies:
        cp.start()
    for cp in copies:
        cp.wait()


def _mixer_bwd_call(dx1, z1, hb, a1, cat, p, gs, rv, x, w, ts, slabs, off_out, off_in):
    seq = dx1.shape[0]
    n = seq // ts
    halo_blocks = ts // HALO

    def body(slabs_in, dx1_ref, z1_ref, hb_ref, hprev_ref, a1_ref, cat_ref, p_ref, gs_ref, rv_ref, x_ref, cw_ref, va_ref,
             wcatt_ref, ktt_ref, vmt_ref, woutt_ref, wint_ref, v1_ref,
             slabs_ref, dx_ref, dkt_ref, dvm_ref, dwcat_ref, dmsum_ref, dva_ref, dcw_ref, dv1_ref,
             buf, dbuf, da0buf, dwout_ref, dwin_ref, sems, sh, dsh):
        i = pl.program_id(0)

        @pl.when(i == 0)
        def _():
            for ref in (dwout_ref, dwin_ref, dkt_ref, dvm_ref, dwcat_ref, dmsum_ref, dva_ref, dcw_ref, dv1_ref):
                ref[...] = jnp.zeros(ref.shape, F32)
            dbuf[ts:ts + HALO, :] = jnp.zeros((HALO, CONV_W), F32)

        @pl.when(i > 0)
        def _():
            dbuf[ts:ts + HALO, :] = dbuf[0:HALO, :]

        dx1v = dx1_ref[...]
        xh1, r1 = _ln_stats(z1_ref[...])
        dv1_ref[0:1, :] += _rowsum(dx1v * xh1)
        dv1_ref[1:2, :] += _rowsum(dx1v)
        dz1 = _ln_bwd(dx1v, xh1, r1, v1_ref[0:1, :])
        dmix = dz1.astype(BF16)

        hf = hb_ref[:, 0:2 * CONV_W].astype(F32)
        hp = hprev_ref[...].astype(F32)
        a0p = hp[:, 0:CONV_W] * _sigmoid(hp[:, CONV_W:2 * CONV_W])
        buf[0:HALO, :] = jnp.where(i == n - 1, 0.0, a0p)
        ga = _group_a_fwd(hf, buf, sh, a1_ref, cw_ref, va_ref, ts, conv=False)
        gb = {k: gs_ref[:, j * GMLP_W:(j + 1) * GMLP_W].astype(F32) for j, k in enumerate(GROUP_B_SAVED)}
        vn = gb["vhat"] * va_ref[3:4, :] + va_ref[4:5, :]
        pb = p_ref[...]
        gc = dict(qb=hb_ref[:, IN_W - XATTN_W:IN_W], pb=pb, p=pb.astype(F32))

        dwout_ref[...] += _dot_tn(cat_ref[...], dmix)
        dcat = _dot(dmix, woutt_ref[...])
        da = dcat[:, 0:CONV_W]
        dg = dcat[:, CONV_W:CONV_W + GMLP_W]
        dc = dcat[:, CONV_W + GMLP_W:D_MODEL].astype(BF16)

        dp = _dot(dc, vmt_ref[...])
        dvm_ref[...] += _dot_tn(gc["pb"], dc)
        dss = []
        for g in range(XATTN_HEADS):
            sl = slice(g * N_MEM, (g + 1) * N_MEM)
            pg = gc["p"][:, sl]
            dpg = dp[:, sl]
            dss.append(pg * (dpg - jnp.sum(dpg * pg, axis=-1, keepdims=True)))
        ds = jnp.concatenate(dss, axis=1).astype(BF16)
        dq = _dot(ds, ktt_ref[...])
        dkt_ref[...] += _dot_tn(gc["qb"], ds)

        dmixed = dg * gb["u"]
        dhu = dg * gb["mixed"] * gb["du"]
        dvns = []
        for j, c0 in enumerate(range(0, ts, CHUNK)):
            dm = dmixed[c0:c0 + CHUNK, :]
            dmb = dm.astype(BF16)
            dmsum_ref[...] += dm
            dwcat_ref[...] += _dot_nt(dmb, _stack_heads(vn[c0:c0 + CHUNK, :]).astype(BF16))
            dst = _dot(wcatt_ref[...], dmb)
            dvn_c = jnp.zeros((CHUNK, GMLP_W), F32)
            for h in range(GMLP_HEADS):
                dvn_c = dvn_c + jnp.where(_head_mask(GMLP_W, h), dst[h * CHUNK:(h + 1) * CHUNK, :], 0.0)
            dvns.append(dvn_c)
        dvn = jnp.concatenate(dvns, axis=0) if len(dvns) > 1 else dvns[0]
        dva_ref[3:4, :] += _rowsum(dvn * gb["vhat"])
        dva_ref[4:5, :] += _rowsum(dvn)
        dhv = _ln_bwd(dvn, gb["vhat"], rv_ref[:, 0:1], va_ref[3:4, :]) * gb["dv"]

        a2, sa = ga["a2"], ga["sa"]
        da2 = da * (sa * (1.0 + a2 * (1.0 - sa)))
        dva_ref[1:2, :] += _rowsum(da2 * ga["a2h"])
        dva_ref[2:3, :] += _rowsum(da2)
        da1 = _ln_bwd(da2, ga["a2h"], ga["ra"], va_ref[1:2, :])
        dva_ref[0:1, :] += _rowsum(da1)
        dbuf[0:ts, :] = da1
        _shift_copies(dbuf, dsh, ts + HALO - 8)
        _conv31_dw(buf, sh, dbuf, dcw_ref, ts)
        _conv31_dx(dbuf, dsh, cw_ref, da0buf, ts)
        da0 = da0buf[...]
        sg = ga["sg"]
        dha = da0 * sg
        dhg = da0 * ga["ha"] * sg * (1.0 - sg)

        dh = jnp.concatenate([dha, dhg, dhu, dhv, dq], axis=1).astype(BF16)
        dx_ref[...] = _dot(dh, wint_ref[...]) + ALPHA * dz1
        dwin_ref[...] += _dot_tn(dh, x_ref[...].astype(BF16))

        @pl.when(i == n - 1)
        def _():
            _store_blocks(dwout_ref, slabs_ref, sems, off_out, D_MODEL // N_DEV, 0, N_DEV)
            _store_blocks(dwin_ref, slabs_ref, sems, off_in, IN_W // N_DEV, 0, N_DEV)

    rev = lambda width: pl.BlockSpec((ts, width), lambda i: (n - 1 - i, 0))
    prev = pl.BlockSpec((HALO, 2 * CONV_W), lambda i: (jnp.maximum((n - 1 - i) * halo_blocks - 1, 0), 0))
    hbm = pl.BlockSpec(memory_space=pl.ANY)
    hc = GMLP_HEADS * CHUNK
    am = XATTN_HEADS * N_MEM
    return pl.pallas_call(
        body, name="mixer_bwd", grid=(n,),
        in_specs=[hbm, rev(D_MODEL), rev(D_MODEL), rev(IN_W), prev, rev(CONV_W), rev(D_MODEL), rev(am),
                  rev(len(GROUP_B_SAVED) * GMLP_W), rev(128), rev(D_MODEL), _const((HALO, CONV_W)),
                  _const((8, CONV_W)), _const((hc, CHUNK)), _const((am, XATTN_W)), _const((XATTN_W, am)),
                  _const((D_MODEL, D_MODEL)), _const((IN_W, D_MODEL)), _const((8, D_MODEL))],
        out_specs=[hbm, rev(D_MODEL), _acc((XATTN_W, am)), _acc((am, XATTN_W)),
                   _acc((CHUNK, hc)), _acc((CHUNK, GMLP_W)), _acc((8, CONV_W)), _acc((HALO, CONV_W)),
                   _acc((8, D_MODEL))],
        out_shape=[jax.ShapeDtypeStruct(slabs.shape, F32), jax.ShapeDtypeStruct((seq, D_MODEL), F32),
                   jax.ShapeDtypeStruct((XATTN_W, am), F32),
                   jax.ShapeDtypeStruct((am, XATTN_W), F32), jax.ShapeDtypeStruct((CHUNK, hc), F32),
                   jax.ShapeDtypeStruct((CHUNK, GMLP_W), F32), jax.ShapeDtypeStruct((8, CONV_W), F32),
                   jax.ShapeDtypeStruct((HALO, CONV_W), F32), jax.ShapeDtypeStruct((8, D_MODEL), F32)],
        scratch_shapes=[pltpu.VMEM((ts + HALO, CONV_W), F32),
                        pltpu.VMEM((ts + HALO, CONV_W), F32), pltpu.VMEM((ts, CONV_W), F32),
                        pltpu.VMEM((D_MODEL, D_MODEL), F32), pltpu.VMEM((IN_W, D_MODEL), F32),
                        pltpu.SemaphoreType.DMA((N_DEV,)),
                        pltpu.VMEM((7, ts + HALO, CONV_W), F32), pltpu.VMEM((7, ts + HALO, CONV_W), F32)],
        input_output_aliases={0: 0},
        compiler_params=_params(dimension_semantics=("arbitrary",)),
    )(slabs, dx1, z1, hb, hb, a1, cat, p, gs, rv, x, w["cw"], w["va"], w["wcatt"], w["ktt"], w["vmt"], w["woutt"],
      w["wint"], w["v1"])


FFN_HALO = 8
FF_GROUP = D_FF // N_DEV
FF_GROUP_PAD = D_FF_PAD // N_DEV


def _ffn_taps(ubuf, ts, lo, hi):
    return tuple(ubuf[FFN_HALO - (FFN_CONV_K - 1) + k:FFN_HALO - (FFN_CONV_K - 1) + k + ts, lo:hi]
                 for k in range(FFN_CONV_K))


def _ffn_gate(ubuf, cf_ref, ts, lo, hi):
    taps = _ffn_taps(ubuf, ts, lo, hi)
    g = cf_ref[3:4, lo:hi] + cf_ref[2:3, lo:hi] * taps[2]
    g = g + cf_ref[1:2, lo:hi] * taps[1]
    return g + cf_ref[0:1, lo:hi] * taps[0]


FFN_LANE_CHUNKS = ((0, D_FF_PAD // 2), (D_FF_PAD // 2, D_FF_PAD))
FFN_BWD_CHUNK = 256
FFN_BWD_CHUNKS = tuple((lo, lo + FFN_BWD_CHUNK) for lo in range(0, D_FF_PAD, FFN_BWD_CHUNK))


def _ffn_fwd_call(x1, w, ts, exch=None):
    seq = x1.shape[0]
    n = seq // ts

    def body(x1_ref, wg_ref, wv_ref, cf_ref, wdown_ref, v2_ref, ug_ref, uv_ref, sl_ref, dsl_ref, z2_ref, x2_ref, ubuf,
             act_buf):
        i = pl.program_id(0)

        @pl.when(i == 0)
        def _():
            ubuf[0:FFN_HALO, :] = jnp.zeros((FFN_HALO, D_FF_PAD), F32)

        @pl.when(i > 0)
        def _():
            ubuf[0:FFN_HALO, :] = ubuf[ts:ts + FFN_HALO, :]

        xv = x1_ref[...]
        xb = xv.astype(BF16)
        for lo, hi in FFN_BWD_CHUNKS:
            ug = _dot(xb, wg_ref[:, lo:hi]).astype(BF16)
            uv = _dot(xb, wv_ref[:, lo:hi]).astype(BF16)
            ug_ref[:, lo:hi] = ug
            uv_ref[:, lo:hi] = uv
            ubuf[FFN_HALO:FFN_HALO + ts, lo:hi] = ug.astype(F32)
            gate = _ffn_gate(ubuf, cf_ref, ts, lo, hi)
            sg = _sigmoid(gate)
            sl = gate * sg
            sl_ref[:, lo:hi] = sl.astype(BF16)
            dsl_ref[:, lo:hi] = (sg * (1.0 + gate * (1.0 - sg))).astype(BF16)
            act_buf[:, lo:hi] = (sl * uv.astype(F32)).astype(BF16)
        y = ALPHA * xv + _dot(act_buf[...], wdown_ref[...])
        z2_ref[...] = y
        xh, _ = _ln_stats(y)
        x2_ref[...] = xh * v2_ref[0:1, :] + v2_ref[1:2, :]

    row = lambda width: pl.BlockSpec((ts, width), lambda i: (i, 0))
    return _grid_call(
        body, "ffn_fwd", n,
        in_specs=[row(D_MODEL), _const((D_MODEL, D_FF_PAD)), _const((D_MODEL, D_FF_PAD)), _const((8, D_FF_PAD)),
                  _const((D_FF_PAD, D_MODEL)), _const((8, D_MODEL))],
        out_specs=[row(D_FF_PAD)] * 4 + [row(D_MODEL), row(D_MODEL)],
        out_shape=[jax.ShapeDtypeStruct((seq, D_FF_PAD), BF16)] * 4 + [jax.ShapeDtypeStruct((seq, D_MODEL), F32)] * 2,
        scratch_shapes=[pltpu.VMEM((ts + FFN_HALO, D_FF_PAD), F32), pltpu.VMEM((ts, D_FF_PAD), BF16)],
        args=(x1, w["wg"], w["wv"], w["cf"], w["wdown"], w["v2"]), exch=exch)


def _ffn_bwd_call(dx2_or_target, z2, ug, uv, sl, dsl, w, ts, last, slabs, row_off, exch=None):
    seq = z2.shape[0]
    n = seq // ts
    halo_blocks = ts // 16

    def body(slabs_in, dx2_ref, z2_ref, ug_ref, uv_ref, sl_ref, dsl_ref, uprev_ref, cf_ref, wdownt_ref, v2_ref,
             slabs_ref, dug_ref, duv_ref, dz2_ref, dcf_ref, dv2_ref, loss_ref,
             ubuf, dgbuf, dwacc, sems):
        i = pl.program_id(0)

        @pl.when(i == 0)
        def _():
            dwacc[...] = jnp.zeros(dwacc.shape, F32)
            dcf_ref[...] = jnp.zeros(dcf_ref.shape, F32)
            dv2_ref[...] = jnp.zeros(dv2_ref.shape, F32)
            loss_ref[...] = jnp.zeros(loss_ref.shape, F32)
            dgbuf[ts:ts + FFN_HALO, :] = jnp.zeros((FFN_HALO, D_FF_PAD), F32)

        @pl.when(i > 0)
        def _():
            dgbuf[ts:ts + FFN_HALO, :] = dgbuf[0:FFN_HALO, :]

        xh2, r2 = _ln_stats(z2_ref[...])
        if last:
            diff = xh2 * v2_ref[0:1, :] + v2_ref[1:2, :] - dx2_ref[...]
            loss_ref[...] += jnp.sum(diff * diff) * (0.5 / D_MODEL)
            dx2v = diff * (1.0 / D_MODEL)
        else:
            dx2v = dx2_ref[...]
        dv2_ref[0:1, :] += _rowsum(dx2v * xh2)
        dv2_ref[1:2, :] += _rowsum(dx2v)
        dz2 = _ln_bwd(dx2v, xh2, r2, v2_ref[0:1, :])
        dz2_ref[...] = dz2
        dy = dz2.astype(BF16)

        up = uprev_ref[...].astype(F32)[8:16, :]
        ubuf[0:FFN_HALO, :] = jnp.where(i == n - 1, 0.0, up)
        ubuf[FFN_HALO:FFN_HALO + ts, :] = ug_ref[...].astype(F32)
        for lo, hi in FFN_BWD_CHUNKS:
            taps = _ffn_taps(ubuf, ts, lo, hi)
            sl = sl_ref[:, lo:hi].astype(F32)
            uvf = uv_ref[:, lo:hi].astype(F32)
            act = (sl * uvf).astype(BF16)
            dwacc[lo:hi, :] += _dot_tn(act, dy)
            dact = _dot(dy, wdownt_ref[:, lo:hi])
            duv_ref[:, lo:hi] = (dact * sl).astype(BF16)
            dgate = dact * uvf * dsl_ref[:, lo:hi].astype(F32)
            dgbuf[0:ts, lo:hi] = dgate
            dcf_ref[3:4, lo:hi] += _rowsum(dgate)
            for k in range(FFN_CONV_K):
                dcf_ref[k:k + 1, lo:hi] += _rowsum(dgate * taps[k])
            dug = cf_ref[2:3, lo:hi] * dgate + cf_ref[1:2, lo:hi] * dgbuf[1:1 + ts, lo:hi]
            dug = dug + cf_ref[0:1, lo:hi] * dgbuf[2:2 + ts, lo:hi]
            dug_ref[:, lo:hi] = dug.astype(BF16)

        @pl.when(i == n - 1)
        def _():
            _store_blocks(dwacc, slabs_ref, sems, row_off, D_FF_PAD // N_DEV, 0, N_DEV)

    rev = lambda width: pl.BlockSpec((ts, width), lambda i: (n - 1 - i, 0))
    prev = pl.BlockSpec((16, D_FF_PAD), lambda i: (jnp.maximum((n - 1 - i) * halo_blocks - 1, 0), 0))
    hbm = pl.BlockSpec(memory_space=pl.ANY)
    return _grid_call(
        body, "ffn_bwd_last" if last else "ffn_bwd", n,
        in_specs=[hbm, rev(D_MODEL), rev(D_MODEL)] + [rev(D_FF_PAD)] * 4 + [prev, _const((8, D_FF_PAD)),
                                                                           _const((D_MODEL, D_FF_PAD)), _const((8, D_MODEL))],
        out_specs=[hbm, rev(D_FF_PAD), rev(D_FF_PAD), rev(D_MODEL),
                   _acc((8, D_FF_PAD)), _acc((8, D_MODEL)), _acc((8, 128))],
        out_shape=[jax.ShapeDtypeStruct(slabs.shape, F32),
                   jax.ShapeDtypeStruct((seq, D_FF_PAD), BF16), jax.ShapeDtypeStruct((seq, D_FF_PAD), BF16),
                   jax.ShapeDtypeStruct((seq, D_MODEL), F32),
                   jax.ShapeDtypeStruct((8, D_FF_PAD), F32), jax.ShapeDtypeStruct((8, D_MODEL), F32),
                   jax.ShapeDtypeStruct((8, 128), F32)],
        scratch_shapes=[pltpu.VMEM((ts + FFN_HALO, D_FF_PAD), F32), pltpu.VMEM((ts + FFN_HALO, D_FF_PAD), F32),
                        pltpu.VMEM((D_FF_PAD, D_MODEL), F32), pltpu.SemaphoreType.DMA((N_DEV,))],
        args=(slabs, dx2_or_target, z2, ug, uv, sl, dsl, ug, w["cf"], w["wdownt"], w["v2"]), aliases={0: 0}, exch=exch)


def _proj_bwd_call(d, wt, xin, addend, scale, ts, name, slabs, row_off, first_block, n_blocks, exch=None):
    seq, k = d.shape
    n = seq // ts

    def body(slabs_in, d_ref, wt_ref, xin_ref, add_ref, slabs_ref, dx_ref, acc, sems):
        i = pl.program_id(0)

        @pl.when(i == 0)
        def _():
            acc[...] = jnp.zeros(acc.shape, F32)

        dv = d_ref[...]
        dx_ref[...] = _dot(dv, wt_ref[...]) + scale * add_ref[...]
        acc[...] += _dot_tn(dv, xin_ref[...].astype(BF16))

        @pl.when(i == n - 1)
        def _():
            _store_blocks(acc, slabs_ref, sems, row_off, k // n_blocks, first_block, n_blocks)

    row = lambda width: pl.BlockSpec((ts, width), lambda i: (i, 0))
    hbm = pl.BlockSpec(memory_space=pl.ANY)
    return _grid_call(
        body, name, n,
        in_specs=[hbm, row(k), _const((k, D_MODEL)), row(D_MODEL), row(D_MODEL)],
        out_specs=[hbm, row(D_MODEL)],
        out_shape=[jax.ShapeDtypeStruct(slabs.shape, F32), jax.ShapeDtypeStruct((seq, D_MODEL), F32)],
        scratch_shapes=[pltpu.VMEM((k, D_MODEL), F32), pltpu.SemaphoreType.DMA((n_blocks,))],
        args=(slabs, d, wt, xin, addend), aliases={0: 0}, exch=exch)


MEM_FOLD = BLOB_LANES // XATTN_W


def _mem_proj_call(memq, wk_flat, wv_flat):
    def body(memq_ref, wk_ref, wv_ref, kh_ref, vh_ref):
        for w_ref, o_ref in ((wk_ref, kh_ref), (wv_ref, vh_ref)):
            acc = jnp.zeros((N_MEM, XATTN_W), F32)
            for q in range(MEM_FOLD):
                acc = acc + _dot(memq_ref[q], w_ref[:, q * XATTN_W:(q + 1) * XATTN_W])
            o_ref[...] = acc

    out = jax.ShapeDtypeStruct((N_MEM, XATTN_W), F32)
    return pl.pallas_call(body, name="mem_proj", out_shape=[out, out], compiler_params=_params())(memq, wk_flat, wv_flat)


def _mem_proj_bwd_call(memq, dkh, dvh, slabs, off_k, off_v):
    rows = D_MODEL // MEM_FOLD

    def body(slabs_in, memq_ref, dkh_ref, dvh_ref, slabs_ref, acc, sems):
        for d_ref, off in ((dkh_ref, off_k), (dvh_ref, off_v)):
            dv = d_ref[...].astype(BF16)
            for q in range(MEM_FOLD):
                acc[:, q * XATTN_W:(q + 1) * XATTN_W] = _dot_tn(memq_ref[q], dv)
            _store_blocks(acc, slabs_ref, sems, off, rows // N_DEV, 0, N_DEV)

    hbm = pl.BlockSpec(memory_space=pl.ANY)
    vmem = pl.BlockSpec(memory_space=pltpu.VMEM)
    return pl.pallas_call(
        body, name="mem_proj_bwd", in_specs=[hbm, vmem, vmem, vmem], out_specs=hbm,
        out_shape=jax.ShapeDtypeStruct(slabs.shape, F32),
        scratch_shapes=[pltpu.VMEM((rows, BLOB_LANES), F32), pltpu.SemaphoreType.DMA((N_DEV,))],
        input_output_aliases={0: 0}, compiler_params=_params(),
    )(slabs, memq, dkh, dvh)


def _place():
    return lax.axis_index("x"), lax.axis_index("y"), lax.axis_index("c")


def _all_gather_call(arrs, pieces, name, swap=None):
    n_in, n_p = len(arrs), len(pieces)
    n_sw = 0 if swap is None else 1

    def body(*refs):
        ins, outs = refs[:n_in], refs[n_in + n_sw:n_in + n_sw + n_p]
        send_sems, recv_sems, local_sems = refs[n_in + 2 * n_sw + n_p:n_in + 2 * n_sw + n_p + 3]
        swapped = []
        if swap is not None:
            swapped = swap[1](refs[n_in], refs[n_in + n_sw + n_p], *refs[-2:])
            for cp in swapped:
                cp.start()
        x, y, c = _place()
        me, sibling = (x, y, c), (x, y, 1 - c)
        chips = [(1 - x, y), (x, 1 - y), (1 - x, 1 - y)]

        def src(a):
            idx, r0, rows = pieces[a]
            return ins[idx] if r0 is None else ins[idx].at[pl.ds(r0, rows)]

        def slab(a, p):
            return outs[a].at[4 * p[0] + 2 * p[1] + p[2]]

        def copy(a, k, block, to, own=False):
            return pltpu.make_async_remote_copy(
                src_ref=src(a) if own else slab(a, block), dst_ref=slab(a, block),
                send_sem=send_sems.at[a, k], recv_sem=recv_sems.at[a, k], device_id=to, device_id_type=MESH)

        mine = [pltpu.make_async_copy(src(a), slab(a, me), local_sems.at[a]) for a in range(n_p)]
        for cp in mine:
            cp.start()
        first = []
        for a in range(n_p):
            first.append(copy(a, 0, me, sibling, own=True))
            first += [copy(a, 1 + j, me, (*chip, c), own=True) for j, chip in enumerate(chips)]
        for cp in first:
            cp.start()
        passed = []
        for a in range(n_p):
            for j, chip in enumerate(chips):
                copy(a, 1 + j, (*chip, c), me).wait_recv()
                cp = copy(a, 4 + j, (*chip, c), sibling)
                cp.start()
                passed.append(cp)
        for a in range(n_p):
            copy(a, 0, sibling, me).wait_recv()
            for j, chip in enumerate(chips):
                copy(a, 4 + j, (*chip, 1 - c), me).wait_recv()
        for cp in first + passed:
            cp.wait_send()
        for cp in mine:
            cp.wait()
        for cp in swapped:
            cp.wait()

    def out_shape(piece):
        idx, r0, rows = piece
        a = arrs[idx]
        return jax.ShapeDtypeStruct((N_DEV,) + (a.shape if r0 is None else (rows,) + a.shape[1:]), a.dtype)

    hbm = pl.BlockSpec(memory_space=pl.ANY)
    out_shapes = [out_shape(p) for p in pieces]
    scratch = [pltpu.SemaphoreType.DMA((n_p, 7)), pltpu.SemaphoreType.DMA((n_p, 7)), pltpu.SemaphoreType.DMA((n_p,))]
    args = list(arrs)
    if swap is not None:
        a, _, n = swap
        args.append(a)
        out_shapes.append(jax.ShapeDtypeStruct((n,) + a.shape[1:], a.dtype))
        scratch += [pltpu.SemaphoreType.DMA((n,)), pltpu.SemaphoreType.DMA((n,))]
    return pl.pallas_call(
        body, name=name, in_specs=[hbm] * len(args), out_specs=[hbm] * len(out_shapes), out_shape=out_shapes,
        scratch_shapes=scratch,
    )(*args)


def _flip(v, f):
    return 1 - v if f else v


def _gather_exchange(arrs, pieces):
    n_peers = N_DEV - 1

    def build(ins, outs, send_sems, recv_sems, local_sems):
        x, y, c = _place()
        flips = [(fx, fy, fc) for fx in (0, 1) for fy in (0, 1) for fc in (0, 1) if fx or fy or fc]
        remote, local = [], []
        for a, (idx, r0, rows) in enumerate(pieces):
            src = ins[idx] if r0 is None else ins[idx].at[pl.ds(r0, rows)]
            dst = outs[a].at[4 * x + 2 * y + c]
            remote += [pltpu.make_async_remote_copy(
                src_ref=src, dst_ref=dst, send_sem=send_sems.at[n_peers * a + k], recv_sem=recv_sems.at[n_peers * a + k],
                device_id=(_flip(x, fx), _flip(y, fy), _flip(c, fc)), device_id_type=MESH)
                for k, (fx, fy, fc) in enumerate(flips)]
            local.append(pltpu.make_async_copy(src, dst, local_sems.at[a]))
        return remote, local

    def out_shape(piece):
        idx, r0, rows = piece
        a = arrs[idx]
        return jax.ShapeDtypeStruct((N_DEV,) + (a.shape if r0 is None else (rows,) + a.shape[1:]), a.dtype)

    return _Exchange(arrs, [out_shape(p) for p in pieces], n_peers * len(pieces), build, n_local=len(pieces))


def _swap_core_copies(g_ref, r_ref, send_sems, recv_sems):
    x, y, c = _place()
    return [pltpu.make_async_remote_copy(
        src_ref=g_ref.at[2 * k + (1 - c)], dst_ref=r_ref.at[k], send_sem=send_sems.at[k], recv_sem=recv_sems.at[k],
        device_id=(x, y, 1 - c), device_id_type=MESH) for k in range(4)]


def _swap_chip_copies(p_ref, r_ref, send_sems, recv_sems):
    x, y, c = _place()
    chips = [(1 - x, y), (x, 1 - y), (1 - x, 1 - y)]
    return [pltpu.make_async_remote_copy(
        src_ref=p_ref.at[2 * px + py], dst_ref=r_ref.at[j], send_sem=send_sems.at[j], recv_sem=recv_sems.at[j],
        device_id=(px, py, c), device_id_type=MESH) for j, (px, py) in enumerate(chips)]


def _swap_exchange(a, copies, n):
    return _Exchange([a], [jax.ShapeDtypeStruct((n,) + a.shape[1:], a.dtype)], n,
                     lambda ins, outs, send_sems, recv_sems, local_sems: (copies(ins[0], outs[0], send_sems, recv_sems), []))


def _swap_call(a, copies, n, name):
    def body(a_ref, r_ref, send_sems, recv_sems):
        cps = copies(a_ref, r_ref, send_sems, recv_sems)
        for cp in cps:
            cp.start()
        for cp in cps:
            cp.wait()

    hbm = pl.BlockSpec(memory_space=pl.ANY)
    return pl.pallas_call(
        body, name=name, in_specs=[hbm], out_specs=hbm, out_shape=jax.ShapeDtypeStruct((n,) + a.shape[1:], a.dtype),
        scratch_shapes=[pltpu.SemaphoreType.DMA((n,)), pltpu.SemaphoreType.DMA((n,))],
    )(a)


def _swap_chip_start(p, rider):
    n = 3
    land = lax.empty((n,) + p.shape[1:], p.dtype)

    def body(p_ref, land_ref, rider_ref, send_sems, recv_sems, p_thru, land_thru, rider_thru):
        for cp in _swap_chip_copies(p_ref, land_ref, send_sems, recv_sems):
            cp.start()

    hbm, sem = pl.BlockSpec(memory_space=pltpu.HBM), pl.BlockSpec(memory_space=pltpu.SEMAPHORE)
    return pl.pallas_call(
        body, name="rs_swap_chip_start",
        out_shape=(pltpu.SemaphoreType.DMA((n,)), pltpu.SemaphoreType.DMA((n,)), pltpu.HBM(p.shape, p.dtype),
                   pltpu.HBM(land.shape, land.dtype), pltpu.HBM(rider.shape, rider.dtype)),
        in_specs=(hbm, hbm, hbm), out_specs=(sem, sem, hbm, hbm, hbm), input_output_aliases={0: 2, 1: 3, 2: 4},
        compiler_params=pltpu.CompilerParams(has_side_effects=pltpu.SideEffectType.DATAFLOW_SIDE_EFFECTING),
    )(pltpu.with_memory_space_constraint(p, pltpu.HBM), pltpu.with_memory_space_constraint(land, pltpu.HBM),
      pltpu.with_memory_space_constraint(rider, pltpu.HBM))


def _swap_chip_wait(send_sems, recv_sems, p_thru, land_thru, *after):
    def body(p_ref, land_ref, send_sems, recv_sems, *rest):
        for cp in _swap_chip_copies(p_ref, land_ref, send_sems, recv_sems):
            cp.wait_send()
            cp.wait_recv()

    hbm, sem = pl.BlockSpec(memory_space=pltpu.HBM), pl.BlockSpec(memory_space=pltpu.SEMAPHORE)
    return pl.pallas_call(
        body, name="rs_swap_chip_wait",
        out_shape=(pltpu.HBM(p_thru.shape, p_thru.dtype), pltpu.HBM(land_thru.shape, land_thru.dtype)),
        in_specs=(hbm, hbm, sem, sem) + (pl.BlockSpec(memory_space=pl.ANY),) * len(after), out_specs=(hbm, hbm),
        input_output_aliases={0: 0, 1: 1},
        compiler_params=pltpu.CompilerParams(has_side_effects=pltpu.SideEffectType.DATAFLOW_SIDE_EFFECTING),
    )(p_thru, land_thru, send_sems, recv_sems, *after)[1]


ADD_ROWS = 184


def _pair_add_call(g, r, c):
    _, rows, width = g.shape

    def body(c_ref, g_ref, r_ref, o_ref):
        o_ref[...] = (g_ref[...] + r_ref[...]).astype(BF16)

    return pl.pallas_call(
        body, name="rs_pair_add",
        grid_spec=pltpu.PrefetchScalarGridSpec(
            num_scalar_prefetch=1, grid=(4, rows // ADD_ROWS),
            in_specs=[pl.BlockSpec((None, ADD_ROWS, width), lambda k, i, c_ref: (2 * k + c_ref[0], i, 0)),
                      pl.BlockSpec((None, ADD_ROWS, width), lambda k, i, c_ref: (k, i, 0))],
            out_specs=pl.BlockSpec((None, ADD_ROWS, width), lambda k, i, c_ref: (k, i, 0))),
        out_shape=jax.ShapeDtypeStruct((4, rows, width), BF16),
        compiler_params=_params(dimension_semantics=("arbitrary", "arbitrary")),
    )(c, g, r)


def _adam(w, g, m, v):
    mn = ADAM_B1 * m + (1.0 - ADAM_B1) * g
    vn = ADAM_B2 * v + (1.0 - ADAM_B2) * (g * g)
    m_hat = mn / (1.0 - ADAM_B1 ** ADAM_STEP)
    v_hat = vn / (1.0 - ADAM_B2 ** ADAM_STEP)
    return -ADAM_LR * (m_hat / (jnp.sqrt(v_hat) + ADAM_EPS) + ADAM_WD * w), mn, vn


def _chip_add_adamw_call(slabs, from_sibling, from_chips, me, chip, w, m, v):
    _, rows, width = slabs.shape

    def body(me_ref, chip_ref, own_ref, sib_ref, r_ref, w_ref, m_ref, v_ref, g_ref, d_ref, mo_ref, vo_ref):
        g = own_ref[...] + sib_ref[...]
        for j in range(3):
            g = g + r_ref[j].astype(F32)
        g_ref[...] = g
        d_ref[...], mo_ref[...], vo_ref[...] = _adam(w_ref[...], g, m_ref[...], v_ref[...])

    spec = pl.BlockSpec((ADD_ROWS, width), lambda i, me_ref, chip_ref: (i, 0))
    return pl.pallas_call(
        body, name="rs_chip_add_adamw",
        grid_spec=pltpu.PrefetchScalarGridSpec(
            num_scalar_prefetch=2, grid=(rows // ADD_ROWS,),
            in_specs=[pl.BlockSpec((None, ADD_ROWS, width), lambda i, me_ref, chip_ref: (me_ref[0], i, 0)),
                      pl.BlockSpec((None, ADD_ROWS, width), lambda i, me_ref, chip_ref: (chip_ref[0], i, 0)),
                      pl.BlockSpec((3, ADD_ROWS, width), lambda i, me_ref, chip_ref: (0, i, 0)), spec, spec, spec],
            out_specs=[spec] * 4),
        out_shape=[jax.ShapeDtypeStruct((rows, width), F32)] * 4,
        compiler_params=_params(dimension_semantics=("arbitrary",)),
    )(me, chip, slabs, from_sibling, from_chips, w, m, v)


def _adamw_whole_call(params, name):
    n = len(params)

    def body(*refs):
        ins, outs = refs[:4 * n], refs[4 * n:]
        for a in range(n):
            w_ref, g_ref, m_ref, v_ref = ins[4 * a:4 * a + 4]
            outs[3 * a][...], outs[3 * a + 1][...], outs[3 * a + 2][...] = _adam(w_ref[...], g_ref[...], m_ref[...],
                                                                              v_ref[...])

    flat = [a for p in params for a in p]
    shapes = [jax.ShapeDtypeStruct(p[0].shape, F32) for p in params for _ in range(3)]
    out = pl.pallas_call(body, name=name, out_shape=shapes, compiler_params=_params())(*flat)
    return [tuple(out[3 * a:3 * a + 3]) for a in range(n)]


GATHERED_ACCS = (("dva", (8, CONV_W)), ("dv1", (8, D_MODEL)), ("dv2", (8, D_MODEL)), ("dcf", (8, D_FF_PAD)),
                 ("dcw", (HALO, CONV_W)), ("dwcat", (CHUNK, GMLP_HEADS * CHUNK)), ("dmsum", (CHUNK, GMLP_W)))
VEC_A = ("conv_a_b", "ln_a_g", "ln_a_b", "ln_v_g", "ln_v_b")
REP_IN_KERNEL = VEC_A + ("ln1_g", "ln1_b", "ln2_g", "ln2_b", "w_s", "b_s")


def _replicated_update_call(gathered, p, mom_m, mom_v):
    n_acc = len(GATHERED_ACCS)
    n_rep = len(REP_IN_KERNEL)

    def body(*refs):
        acc_refs = refs[:DEPTH * n_acc]
        wmv = refs[DEPTH * n_acc:DEPTH * n_acc + 3 * n_rep]
        outs = refs[DEPTH * n_acc + 3 * n_rep:]
        out_par = {nm: outs[4 * a:4 * a + 4] for a, nm in enumerate(REP_IN_KERNEL)}
        out_dcf = outs[4 * n_rep:4 * n_rep + DEPTH]
        out_dcw = outs[4 * n_rep + DEPTH:4 * n_rep + 2 * DEPTH]
        par = {nm: wmv[3 * a:3 * a + 3] for a, nm in enumerate(REP_IN_KERNEL)}
        tril = (lax.broadcasted_iota(jnp.int32, (CHUNK, CHUNK), 0) >= lax.broadcasted_iota(jnp.int32, (CHUNK, CHUNK), 1))
        head = lax.broadcasted_iota(jnp.int32, (8, GMLP_W), 0) * HEAD_DIM
        lane = lax.broadcasted_iota(jnp.int32, (8, GMLP_W), 1)
        sel = jnp.where((lane >= head) & (lane < head + HEAD_DIM), 1.0, 0.0)

        def update(nm, idx, g):
            w_ref, m_ref, v_ref = par[nm]
            d, mn, vn = _adam(w_ref[idx], g, m_ref[idx], v_ref[idx])
            g_ref, d_ref, mo_ref, vo_ref = out_par[nm]
            g_ref[idx] = g
            d_ref[idx] = d
            mo_ref[idx] = mn
            vo_ref[idx] = vn

        for l in range(DEPTH):
            tot = {}
            for a, (nm, _) in enumerate(GATHERED_ACCS):
                ref = acc_refs[l * n_acc + a]
                s = ref[0]
                for j in range(1, N_DEV):
                    s = s + ref[j]
                tot[nm] = s
            out_dcf[l][...] = tot["dcf"]
            out_dcw[l][...] = tot["dcw"]
            row = (slice(l, l + 1), slice(None))
            for k, nm in enumerate(VEC_A):
                update(nm, row, tot["dva"][k:k + 1, :])
            update("ln1_g", row, tot["dv1"][0:1, :])
            update("ln1_b", row, tot["dv1"][1:2, :])
            update("ln2_g", row, tot["dv2"][0:1, :])
            update("ln2_b", row, tot["dv2"][1:2, :])
            for h in range(GMLP_HEADS):
                gw = jnp.where(tril, tot["dwcat"][:, h * CHUNK:(h + 1) * CHUNK], 0.0)
                update("w_s", (l, h), gw)
            gb = lax.dot_general(sel, tot["dmsum"], (((1,), (1,)), ((), ())), precision=lax.Precision.HIGHEST,
                                 preferred_element_type=F32)
            for h in range(GMLP_HEADS):
                update("b_s", (l, slice(h, h + 1), slice(None)), gb[h:h + 1, :])

    ins = [gathered[l][nm] for l in range(DEPTH) for nm, _ in GATHERED_ACCS]
    ins += [t[nm] for nm in REP_IN_KERNEL for t in (p, mom_m, mom_v)]
    shapes = [jax.ShapeDtypeStruct(p[nm].shape, F32) for nm in REP_IN_KERNEL for _ in range(4)]
    shapes += [jax.ShapeDtypeStruct((8, D_FF_PAD), F32)] * DEPTH + [jax.ShapeDtypeStruct((HALO, CONV_W), F32)] * DEPTH
    out = pl.pallas_call(body, name="replicated_update", out_shape=shapes, compiler_params=_params())(*ins)
    res = [{nm: out[4 * a + k] for a, nm in enumerate(REP_IN_KERNEL)} for k in range(4)]
    return res, out[4 * n_rep:4 * n_rep + DEPTH], out[4 * n_rep + DEPTH:]


BLOCK_ROWS = (("w_in", IN_W // N_DEV), ("w_out", D_MODEL // N_DEV), ("w_up", 2 * FF_GROUP_PAD),
              ("w_down", FF_GROUP_PAD), ("w_mk", D_MODEL // N_DEV // MEM_FOLD), ("w_mv", D_MODEL // N_DEV // MEM_FOLD))
LAYER_ROWS = sum(r for _, r in BLOCK_ROWS)
assert LAYER_ROWS % ADD_ROWS == 0 and all(r % 16 == 0 for _, r in BLOCK_ROWS)


def _row_off(name):
    off = 0
    for nm, r in BLOCK_ROWS:
        if nm == name:
            return off
        off += r
    raise KeyError(name)


def _to_rows(name, a):
    if name == "w_in":
        return a.T
    if name == "w_up":
        t = a.T.reshape(2, FF_GROUP, D_MODEL)
        return jnp.pad(t, ((0, 0), (0, FF_GROUP_PAD - FF_GROUP), (0, 0))).reshape(2 * FF_GROUP_PAD, D_MODEL)
    if name == "w_down":
        return jnp.pad(a, ((0, FF_GROUP_PAD - FF_GROUP), (0, 0)))
    if name == "w_out":
        return a
    return a.reshape(-1, BLOB_LANES)


def _from_rows(name, r):
    if name == "w_in":
        return r.T
    if name == "w_up":
        return r.reshape(2, FF_GROUP_PAD, D_MODEL)[:, :FF_GROUP].reshape(2 * FF_GROUP, D_MODEL).T
    if name == "w_down":
        return r[:FF_GROUP]
    if name == "w_out":
        return r
    return r.reshape(D_MODEL // N_DEV, XATTN_W)


def _blob(tree, l):
    return jnp.concatenate([_to_rows(nm, tree[nm][l]) for nm, _ in BLOCK_ROWS], axis=0)


def _unblob(blobs):
    return {nm: jnp.stack([_from_rows(nm, b[_row_off(nm):_row_off(nm) + r]) for b in blobs]) for nm, r in BLOCK_ROWS}


def _ff_interleave(a):
    lead = a.shape[:-1]
    t = a.reshape(lead + (N_DEV, FF_GROUP))
    return jnp.pad(t, [(0, 0)] * len(lead) + [(0, 0), (0, FF_GROUP_PAD - FF_GROUP)]).reshape(lead + (D_FF_PAD,))


def _ff_deinterleave(a):
    lead = a.shape[:-1]
    return a.reshape(lead + (N_DEV, FF_GROUP_PAD))[..., :FF_GROUP].reshape(lead + (D_FF,))


def _head_table():
    hd = jnp.arange(XATTN_W) // HEAD_DIM
    return (hd[None, :] == jnp.arange(XATTN_HEADS)[:, None]).astype(F32)


def _mixer_operands(mat, conv_a_w, p, l, memq):
    w = {}
    w["wint"] = mat["w_in"]
    w["win"] = mat["w_in"].T
    w["wout"] = mat["w_out"]
    w["woutt"] = mat["w_out"].T
    w["cw"] = conv_a_w
    zeros = jnp.zeros((3, CONV_W), F32)
    w["va"] = jnp.concatenate([p[nm][l][None] for nm in VEC_A] + [zeros], axis=0)
    tril = jnp.tril(jnp.ones((CHUNK, CHUNK), F32))
    w["wcat"] = (p["w_s"][l] * tril[None]).transpose(1, 0, 2).reshape(CHUNK, GMLP_HEADS * CHUNK).astype(BF16)
    w["wcatt"] = w["wcat"].T
    w["bfull"] = jnp.repeat(p["b_s"][l].T, HEAD_DIM, axis=1)
    kh, vh = _mem_proj_call(memq, mat["w_mk"], mat["w_mv"])
    hm = _head_table()
    scale = 1.0 / math.sqrt(HEAD_DIM)
    w["kt"] = (kh.T[:, None, :] * hm.T[:, :, None] * scale).reshape(XATTN_W, XATTN_HEADS * N_MEM).astype(BF16)
    w["ktt"] = w["kt"].T
    w["vm"] = (hm[:, None, :] * vh[None]).reshape(XATTN_HEADS * N_MEM, XATTN_W).astype(BF16)
    w["vmt"] = w["vm"].T
    zeros = jnp.zeros((6, D_MODEL), F32)
    w["v1"] = jnp.concatenate([p["ln1_g"][l][None], p["ln1_b"][l][None], zeros], axis=0)
    return w


def _ffn_operands(w_up, w_down, conv_f_w, p, l):
    w = {}
    w["wgt"] = w_up[:D_FF_PAD]
    w["wvt"] = w_up[D_FF_PAD:]
    w["wg"] = w["wgt"].T
    w["wv"] = w["wvt"].T
    w["wdown"] = w_down
    w["wdownt"] = w_down.T
    zeros = jnp.zeros((6, D_MODEL), F32)
    w["v2"] = jnp.concatenate([p["ln2_g"][l][None], p["ln2_b"][l][None], zeros], axis=0)
    w["cf"] = jnp.concatenate([conv_f_w, _ff_interleave(p["conv_f_b"][l][None]), jnp.zeros((4, D_FF_PAD), F32)], axis=0)
    return w


TS_MIXER = 256
TS_FFN = 256
TS_PROJ = 512
CONV_A_SHARD = CONV_W // N_DEV


def kernel(x, mem, w_in, conv_a_w, conv_a_b, ln_a_g, ln_a_b, ln_v_g, ln_v_b, w_s, b_s, w_mk, w_mv, w_out, ln1_g, ln1_b, w_up, conv_f_w, conv_f_b, w_down, ln2_g, ln2_b, loss_target, m_w_in, m_conv_a_w, m_conv_a_b, m_ln_a_g, m_ln_a_b, m_ln_v_g, m_ln_v_b, m_w_s, m_b_s, m_w_mk, m_w_mv, m_w_out, m_ln1_g, m_ln1_b, m_w_up, m_conv_f_w, m_conv_f_b, m_w_down, m_ln2_g, m_ln2_b, v_w_in, v_conv_a_w, v_conv_a_b, v_ln_a_g, v_ln_a_b, v_ln_v_g, v_ln_v_b, v_w_s, v_b_s, v_w_mk, v_w_mv, v_w_out, v_ln1_g, v_ln1_b, v_w_up, v_conv_f_w, v_conv_f_b, v_w_down, v_ln2_g, v_ln2_b):
    given = dict(locals())
    p = {nm: given[nm] for nm in WEIGHTS}
    mom_m = {nm: given["m_" + nm] for nm in WEIGHTS}
    mom_v = {nm: given["v_" + nm] for nm in WEIGHTS}
    seq = x.shape[1]
    ts_m, ts_f, ts_p = min(TS_MIXER, seq), min(TS_FFN, seq), min(TS_PROJ, seq)
    cx, cy, cc = _place()
    me = 4 * cx + 2 * cy + cc

    blobs = [_blob(p, l) for l in range(DEPTH)]
    blobs_bf = [b.astype(BF16) for b in blobs]
    conv_a_tile = jnp.pad(conv_a_w, ((0, 0), (0, HALO - CONV_K), (0, 128 - CONV_A_SHARD)))
    conv_f_tile = jnp.pad(conv_f_w, ((0, 0), (0, 8 - FFN_CONV_K), (0, 384 - FF_GROUP)))
    rows = dict(BLOCK_ROWS)
    mixer_names = ("w_in", "w_out", "w_mk", "w_mv")
    pieces = [(0, _row_off(nm), rows[nm]) for nm in mixer_names] + [(1, None, 0), (2, None, 0)]
    first = _all_gather_call([blobs_bf[0], conv_a_tile, conv_f_tile], pieces, "gather_weights")
    conv_a_all, conv_f_all = first[len(mixer_names)], first[len(mixer_names) + 1]
    conv_a = [conv_a_all[:, l, :, :CONV_A_SHARD].transpose(1, 0, 2).reshape(HALO, CONV_W) for l in range(DEPTH)]
    conv_f = [conv_f_all[:, l, :FFN_CONV_K, :FF_GROUP_PAD].transpose(1, 0, 2).reshape(FFN_CONV_K, D_FF_PAD)
              for l in range(DEPTH)]
    memq = mem[0].reshape(N_MEM, D_MODEL // MEM_FOLD, MEM_FOLD).transpose(2, 0, 1).astype(BF16)

    def matrix(gathered, nm, base):
        lo = _row_off(nm) - base
        return gathered[:, lo:lo + rows[nm]].reshape(-1, BLOB_LANES)

    ffn_base = _row_off("w_up")
    ops0 = _mixer_operands({nm: first[a].reshape(-1, BLOB_LANES) for a, nm in enumerate(mixer_names)}, conv_a[0], p, 0,
                           memq)
    mixer_saved = ("hb", "z1", "x1", "a1", "cat", "p", "gs", "rv")
    mixed0, (ffn0,) = _mixer_fwd_call(
        x[0], ops0, ts_m, _gather_exchange([blobs_bf[0]], [(0, ffn_base, rows["w_up"] + rows["w_down"])]))
    ops0.update(_ffn_operands(matrix(ffn0, "w_up", ffn_base), matrix(ffn0, "w_down", ffn_base), conv_f[0], p, 0))
    saved = [dict(zip(mixer_saved, mixed0), x=x[0])]
    (ug, uv, sl, dsl, z2, x2), (all1,) = _ffn_fwd_call(saved[0]["x1"], ops0, ts_f,
                                                       _gather_exchange([blobs_bf[1]], [(0, 0, LAYER_ROWS)]))
    saved[0].update(ug=ug, uv=uv, sl=sl, dsl=dsl, z2=z2)
    ops1 = _mixer_operands({nm: matrix(all1, nm, 0) for nm in mixer_names}, conv_a[1], p, 1, memq)
    ops1.update(_ffn_operands(matrix(all1, "w_up", 0), matrix(all1, "w_down", 0), conv_f[1], p, 1))
    mixed1, _ = _mixer_fwd_call(x2, ops1, ts_m)
    saved.append(dict(zip(mixer_saved, mixed1), x=x2))
    (ug, uv, sl, dsl, z2, _), _ = _ffn_fwd_call(saved[1]["x1"], ops1, ts_f)
    saved[1].update(ug=ug, uv=uv, sl=sl, dsl=dsl, z2=z2)
    ops = [ops0, ops1]

    hm = _head_table()
    core_id = cc.reshape(1).astype(jnp.int32)
    slabs = [lax.empty((N_DEV, LAYER_ROWS, BLOB_LANES), F32) for _ in range(DEPTH)]
    accs, gathered_accs = [None] * DEPTH, [None] * DEPTH
    acc_names = [nm for nm, _ in GATHERED_ACCS]
    whole = [(a, None, 0) for a in range(len(acc_names))]

    def acc_list(l):
        return [accs[l][nm] for nm in acc_names]

    from_sibling, from_chips = [None] * DEPTH, [None] * DEPTH
    dx = loss_target[0]
    loss = None
    for l in reversed(range(DEPTH)):
        s, w = saved[l], ops[l]
        last = l == DEPTH - 1
        ride = None if last else _swap_exchange(slabs[l + 1], _swap_core_copies, 4)
        (sl, dug, duv, dz2, dcf, dv2, loss_acc), got = _ffn_bwd_call(
            dx, s["z2"], s["ug"], s["uv"], s["sl"], s["dsl"], w, ts_f, last, slabs[l], _row_off("w_down"), ride)
        if last:
            loss = loss_acc[0, 0]
        else:
            from_sibling[l + 1] = got[0]
            chip_sum = _pair_add_call(slabs[l + 1], from_sibling[l + 1], core_id)
        off_up = _row_off("w_up")
        ride = None if last else _gather_exchange(acc_list(l + 1), whole)
        (sl, dxa), got = _proj_bwd_call(dug, w["wgt"], s["x1"], dz2, ALPHA, ts_p, "up_gate_bwd", sl, off_up, 0, 4, ride)
        if not last:
            gathered_accs[l + 1] = dict(zip(acc_names, got))
        ride = None if last else _swap_exchange(chip_sum, _swap_chip_copies, 3)
        (sl, dx1), got = _proj_bwd_call(duv, w["wvt"], s["x1"], dxa, 1.0, ts_p, "up_val_bwd", sl, off_up, 4, 4, ride)
        if not last:
            from_chips[l + 1] = got[0]
        (sl, dx, dkt, dvm, dwcat, dmsum, dva, dcw, dv1) = _mixer_bwd_call(
            dx1, s["z1"], s["hb"], s["a1"], s["cat"], s["p"], s["gs"], s["rv"], s["x"], w, ts_m, sl, _row_off("w_out"),
            _row_off("w_in"))
        dkh = jnp.einsum("hd,dhm->md", hm, dkt.reshape(XATTN_W, XATTN_HEADS, N_MEM)) * (1.0 / math.sqrt(HEAD_DIM))
        dvh = jnp.einsum("hd,hmd->md", hm, dvm.reshape(XATTN_HEADS, N_MEM, XATTN_W))
        slabs[l] = _mem_proj_bwd_call(memq, dkh, dvh, sl, _row_off("w_mk"), _row_off("w_mv"))
        accs[l] = dict(dva=dva, dv1=dv1, dv2=dv2, dcf=dcf, dcw=dcw, dwcat=dwcat, dmsum=dmsum)
    grad_x = dx[None]
    *got, from_sibling[0] = _all_gather_call(acc_list(0), whole, "gather_small_grads",
                                            swap=(slabs[0], _swap_core_copies, 4))
    gathered_accs[0] = dict(zip(acc_names, got))
    chip_sum = _pair_add_call(slabs[0], from_sibling[0], core_id)
    in_flight = _swap_chip_start(chip_sum, from_chips[1])
    from_chips[1] = in_flight[4]
    me_id, chip_id = me.reshape(1).astype(jnp.int32), (2 * cx + cy).reshape(1).astype(jnp.int32)

    def final(l):
        return _chip_add_adamw_call(slabs[l], from_sibling[l], from_chips[l], me_id, chip_id, blobs[l], _blob(mom_m, l),
                                    _blob(mom_v, l))

    per_layer = [None, final(1)]
    rep, dcf_sum, dcw_sum = _replicated_update_call(gathered_accs, p, mom_m, mom_v)
    from_chips[0] = _swap_chip_wait(*in_flight[:4], per_layer[1][0], dcf_sum[0])
    per_layer[0] = final(0)
    outs = [_unblob([per_layer[l][k] for l in range(DEPTH)]) for k in range(4)]
    for k in range(4):
        outs[k].update(rep[k])
    dcf_sum, dcw_sum = jnp.stack(dcf_sum), jnp.stack(dcw_sum)
    zero = jnp.zeros((), jnp.int32)
    g_conv_a_w = lax.dynamic_slice(dcw_sum, (zero, zero, CONV_A_SHARD * me), (DEPTH, CONV_K, CONV_A_SHARD))
    g_conv_f_w = lax.dynamic_slice(dcf_sum, (zero, zero, FF_GROUP_PAD * me), (DEPTH, FFN_CONV_K, FF_GROUP))
    g_conv_f_b = _ff_deinterleave(dcf_sum[:, FFN_CONV_K])
    conv_grads = dict(conv_a_w=g_conv_a_w, conv_f_w=g_conv_f_w, conv_f_b=g_conv_f_b)
    conv_names = tuple(conv_grads)
    upd = _adamw_whole_call([(p[nm], conv_grads[nm], mom_m[nm], mom_v[nm]) for nm in conv_names], "adamw_conv")
    for nm, (d, mn, vn) in zip(conv_names, upd):
        outs[0][nm], outs[1][nm], outs[2][nm], outs[3][nm] = conv_grads[nm], d, mn, vn

    loss = lax.psum(loss, ("x", "y", "c"))
    return (loss, grad_x, *[outs[0][nm] for nm in WEIGHTS], *[outs[1][nm] for nm in WEIGHTS],
            *[outs[2][nm] for nm in WEIGHTS], *[outs[3][nm] for nm in WEIGHTS])
```

```python
import math

import jax
import jax.numpy as jnp
from jax import lax
from jax.experimental import pallas as pl
from jax.experimental.pallas import tpu as pltpu

F32 = jnp.float32
BF16 = jnp.bfloat16

DEPTH = 2
D_MODEL = 1024
CONV_W = 384
GMLP_W = 384
XATTN_W = 256
HEAD_DIM = 64
GMLP_HEADS = 6
XATTN_HEADS = 4
IN_W = 1792
CONV_K = 31
CHUNK = 128
N_MEM = 256
D_FF = 2752
D_FF_PAD = 2816
FFN_CONV_K = 3
ALPHA = (2.0 * DEPTH) ** 0.25
LN_EPS = 1e-5
N_DEV = 8

ADAM_LR = 0.001
ADAM_B1 = 0.9
ADAM_B2 = 0.999
ADAM_EPS = 1e-08
ADAM_WD = 0.01
ADAM_STEP = 10

HALO = 32
CONV_ROWS = 32
V7X_VMEM_BYTES = 64 * 1024 * 1024
VMEM_LIMIT = V7X_VMEM_BYTES - 8 * 1024 * 1024
BLOB_LANES = 1024

MESH = pl.DeviceIdType.MESH

WEIGHTS = ("w_in", "conv_a_w", "conv_a_b", "ln_a_g", "ln_a_b", "ln_v_g", "ln_v_b", "w_s", "b_s", "w_mk", "w_mv",
           "w_out", "ln1_g", "ln1_b", "w_up", "conv_f_w", "conv_f_b", "w_down", "ln2_g", "ln2_b")


def _params(**kw):
    return pltpu.CompilerParams(vmem_limit_bytes=VMEM_LIMIT, **kw)


def _const(shape):
    nd = len(shape)
    return pl.BlockSpec(shape, lambda i: (0,) * nd, pipeline_mode=pl.Buffered(1))


def _acc(shape):
    nd = len(shape)
    return pl.BlockSpec(shape, lambda i: (0,) * nd)


class _Exchange:
    def __init__(self, arrays, out_shapes, n_copies, build, n_local=1):
        self.arrays, self.out_shapes, self.n_copies, self.build = list(arrays), list(out_shapes), n_copies, build
        self.n_local = n_local


def _carry(core, n_in, n_out, exch, n_steps):
    if exch is None:
        return core
    nx_in, nx_out = len(exch.arrays), len(exch.out_shapes)

    def body(*refs):
        o0 = n_in + nx_in
        s0 = o0 + n_out + nx_out
        x_in, x_out, sems = refs[n_in:o0], refs[o0 + n_out:s0], refs[-3:]
        i = pl.program_id(0)

        @pl.when(i == 0)
        def _():
            remote, local = exch.build(x_in, x_out, *sems)
            for cp in remote + local:
                cp.start()

        core(*refs[:n_in], *refs[o0:o0 + n_out], *refs[s0:-3])

        @pl.when(i == n_steps - 1)
        def _():
            remote, local = exch.build(x_in, x_out, *sems)
            for cp in remote + local:
                cp.wait()

    return body


def _grid_call(core, name, n_steps, in_specs, out_specs, out_shape, scratch_shapes, args, aliases=None, exch=None):
    hbm = pl.BlockSpec(memory_space=pl.ANY)
    n_in, n_out = len(in_specs), len(out_specs)
    in_specs, out_specs, out_shape, scratch_shapes, args = (list(in_specs), list(out_specs), list(out_shape),
                                                            list(scratch_shapes), list(args))
    if exch is not None:
        in_specs += [hbm] * len(exch.arrays)
        out_specs += [hbm] * len(exch.out_shapes)
        out_shape += exch.out_shapes
        scratch_shapes += [pltpu.SemaphoreType.DMA((exch.n_copies,)), pltpu.SemaphoreType.DMA((exch.n_copies,)),
                           pltpu.SemaphoreType.DMA((exch.n_local,))]
        args += exch.arrays
    out = pl.pallas_call(
        _carry(core, n_in, n_out, exch, n_steps), name=name, grid=(n_steps,), in_specs=in_specs, out_specs=out_specs,
        out_shape=out_shape, scratch_shapes=scratch_shapes, input_output_aliases=aliases or {},
        compiler_params=_params(dimension_semantics=("arbitrary",)))(*args)
    return list(out[:n_out]), list(out[n_out:])


def _sigmoid(x):
    return 1.0 / (1.0 + jnp.exp(-x))


_GELU_C = math.sqrt(2.0 / math.pi)


def _gelu(x):
    x2 = x * x
    t = jnp.tanh(_GELU_C * (x + 0.044715 * x * x2))
    g = 0.5 * x * (1.0 + t)
    dg = 0.5 * (1.0 + t) + 0.5 * x * (1.0 - t * t) * (_GELU_C * (1.0 + 3.0 * 0.044715 * x2))
    return g, dg


def _ln_stats(z):
    mu = jnp.mean(z, axis=-1, keepdims=True)
    zc = z - mu
    var = jnp.mean(zc * zc, axis=-1, keepdims=True)
    r = lax.rsqrt(var + LN_EPS)
    return zc * r, r


def _ln_bwd(dy, xh, r, g):
    dxh = dy * g
    m1 = jnp.mean(dxh, axis=-1, keepdims=True)
    m2 = jnp.mean(dxh * xh, axis=-1, keepdims=True)
    return r * (dxh - m1 - xh * m2)


def _rowsum(x):
    return jnp.sum(x, axis=0, keepdims=True)


def _dot(a, b):
    return jnp.dot(a, b, preferred_element_type=F32)


def _dot_tn(a, b):
    return lax.dot_general(a, b, (((0,), (0,)), ((), ())), preferred_element_type=F32)


def _dot_nt(a, b):
    return lax.dot_general(a, b, (((1,), (1,)), ((), ())), preferred_element_type=F32)


def _shift_copies(buf, sh, rows):
    for b in range(1, 8):
        sh[b - 1, 0:rows, :] = buf[b:b + rows, :]


def _window(buf, sh, start):
    b = start % 8
    a = start - b
    return buf[a:a + CONV_ROWS, :] if b == 0 else sh[b - 1, a:a + CONV_ROWS, :]


def _conv31_fwd(buf, sh, w_ref, bias, out, ts):
    for r0 in range(0, ts, CONV_ROWS):
        acc = jnp.broadcast_to(bias, (CONV_ROWS, CONV_W))
        for k in range(CONV_K):
            acc = acc + w_ref[k:k + 1, :] * _window(buf, sh, r0 + HALO - (CONV_K - 1) + k)
        out[r0:r0 + CONV_ROWS, :] = acc


def _conv31_dx(dbuf, dsh, w_ref, out, ts):
    for r0 in range(0, ts, CONV_ROWS):
        acc = jnp.zeros((CONV_ROWS, CONV_W), F32)
        for k in range(CONV_K):
            acc = acc + w_ref[k:k + 1, :] * _window(dbuf, dsh, r0 + (CONV_K - 1) - k)
        out[r0:r0 + CONV_ROWS, :] = acc


def _conv31_dw(buf, sh, dbuf, dw_ref, ts):
    for k in range(CONV_K):
        part = jnp.zeros((8, CONV_W), F32)
        for r0 in range(0, ts, CONV_ROWS):
            m = dbuf[r0:r0 + CONV_ROWS, :] * _window(buf, sh, r0 + HALO - (CONV_K - 1) + k)
            for q in range(0, CONV_ROWS, 8):
                part = part + m[q:q + 8, :]
        dw_ref[k:k + 1, :] += _rowsum(part)


def _head_mask(width, h):
    lane = lax.broadcasted_iota(jnp.int32, (CHUNK, width), 1)
    return (lane >= h * HEAD_DIM) & (lane < (h + 1) * HEAD_DIM)


def _stack_heads(vn_c):
    return jnp.concatenate([jnp.where(_head_mask(GMLP_W, h), vn_c, 0.0) for h in range(GMLP_HEADS)], axis=0)


def _group_a_fwd(hf, buf, sh, a1_ref, cw_ref, va_ref, ts, conv=True):
    ha = hf[:, 0:CONV_W]
    sg = _sigmoid(hf[:, CONV_W:2 * CONV_W])
    buf[HALO:HALO + ts, :] = ha * sg
    _shift_copies(buf, sh, ts + HALO - 8)
    if conv:
        _conv31_fwd(buf, sh, cw_ref, va_ref[0:1, :], a1_ref, ts)
    a2h, ra = _ln_stats(a1_ref[...])
    a2 = a2h * va_ref[1:2, :] + va_ref[2:3, :]
    sa = _sigmoid(a2)
    return dict(ha=ha, sg=sg, a2h=a2h, ra=ra, a2=a2, sa=sa, a=a2 * sa)


GROUP_LANES = ((0, 2 * CONV_W), (2 * CONV_W, 2 * CONV_W + 2 * GMLP_W), (2 * CONV_W + 2 * GMLP_W, IN_W))


def _group_b_fwd(hf, va_ref, wcat_ref, bfull_ref, ts):
    hu = hf[:, 0:GMLP_W]
    hv = hf[:, GMLP_W:2 * GMLP_W]
    u, du = _gelu(hu)
    v, dv = _gelu(hv)
    vhat, rv = _ln_stats(v)
    vn = vhat * va_ref[3:4, :] + va_ref[4:5, :]
    stacks, mixed = [], []
    for c0 in range(0, ts, CHUNK):
        st = _stack_heads(vn[c0:c0 + CHUNK, :]).astype(BF16)
        stacks.append(st)
        mixed.append(_dot(wcat_ref[...], st) + bfull_ref[...])
    mixed = jnp.concatenate(mixed, axis=0) if len(mixed) > 1 else mixed[0]
    return dict(u=u, du=du, dv=dv, vhat=vhat, rv=rv, stacks=stacks, mixed=mixed, g=u * mixed)


GROUP_B_SAVED = ("u", "du", "dv", "vhat", "mixed")


def _group_c_fwd(qb, kt_ref, vm_ref):
    s_all = _dot(qb, kt_ref[...])
    ps = []
    for g in range(XATTN_HEADS):
        s = s_all[:, g * N_MEM:(g + 1) * N_MEM]
        e = jnp.exp(s - jnp.max(s, axis=-1, keepdims=True))
        ps.append(e / jnp.sum(e, axis=-1, keepdims=True))
    p_all = jnp.concatenate(ps, axis=1)
    pb = p_all.astype(BF16)
    return dict(qb=qb, p=p_all, pb=pb, c=_dot(pb, vm_ref[...]))


def _mixer_fwd_call(x, w, ts, exch=None):
    seq = x.shape[0]
    n = seq // ts

    def body(x_ref, win_ref, cw_ref, va_ref, wcat_ref, bfull_ref, kt_ref, vm_ref, wout_ref, v1_ref,
             hb_ref, z1_ref, x1_ref, a1buf, cat_ref, p_ref, gs_ref, rv_ref, buf, sh):
        i = pl.program_id(0)

        @pl.when(i == 0)
        def _():
            buf[0:HALO, :] = jnp.zeros((HALO, CONV_W), F32)

        @pl.when(i > 0)
        def _():
            buf[0:HALO, :] = buf[ts:ts + HALO, :]

        xv = x_ref[...]
        hb = _dot(xv.astype(BF16), win_ref[...]).astype(BF16)
        hb_ref[...] = hb
        hf = hb.astype(F32)
        (a_lo, a_hi), (b_lo, b_hi), (c_lo, c_hi) = GROUP_LANES
        ga = _group_a_fwd(hf[:, a_lo:a_hi], buf, sh, a1buf, cw_ref, va_ref, ts)
        gb = _group_b_fwd(hf[:, b_lo:b_hi], va_ref, wcat_ref, bfull_ref, ts)
        gc = _group_c_fwd(hb[:, c_lo:c_hi], kt_ref, vm_ref)
        cat = jnp.concatenate([ga["a"], gb["g"], gc["c"]], axis=1).astype(BF16)
        cat_ref[...] = cat
        p_ref[...] = gc["pb"]
        gs_ref[...] = jnp.concatenate([gb[k] for k in GROUP_B_SAVED], axis=1).astype(BF16)
        rv_ref[...] = jnp.broadcast_to(gb["rv"], (ts, 128))
        z1 = ALPHA * xv + _dot(cat, wout_ref[...])
        z1_ref[...] = z1
        xh, _ = _ln_stats(z1)
        x1_ref[...] = xh * v1_ref[0:1, :] + v1_ref[1:2, :]

    row = lambda width: pl.BlockSpec((ts, width), lambda i: (i, 0))
    return _grid_call(
        body, "mixer_fwd", n,
        in_specs=[row(D_MODEL), _const((D_MODEL, IN_W)), _const((HALO, CONV_W)), _const((8, CONV_W)),
                  _const((CHUNK, GMLP_HEADS * CHUNK)), _const((CHUNK, GMLP_W)), _const((XATTN_W, XATTN_HEADS * N_MEM)),
                  _const((XATTN_HEADS * N_MEM, XATTN_W)), _const((D_MODEL, D_MODEL)), _const((8, D_MODEL))],
        out_specs=[row(IN_W), row(D_MODEL), row(D_MODEL), row(CONV_W), row(D_MODEL), row(XATTN_HEADS * N_MEM),
                   row(len(GROUP_B_SAVED) * GMLP_W), row(128)],
        out_shape=[jax.ShapeDtypeStruct((seq, IN_W), BF16), jax.ShapeDtypeStruct((seq, D_MODEL), F32),
                   jax.ShapeDtypeStruct((seq, D_MODEL), F32), jax.ShapeDtypeStruct((seq, CONV_W), F32),
                   jax.ShapeDtypeStruct((seq, D_MODEL), BF16), jax.ShapeDtypeStruct((seq, XATTN_HEADS * N_MEM), BF16),
                   jax.ShapeDtypeStruct((seq, len(GROUP_B_SAVED) * GMLP_W), BF16),
                   jax.ShapeDtypeStruct((seq, 128), F32)],
        scratch_shapes=[pltpu.VMEM((ts + HALO, CONV_W), F32), pltpu.VMEM((7, ts + HALO, CONV_W), F32)],
        args=(x, w["win"], w["cw"], w["va"], w["wcat"], w["bfull"], w["kt"], w["vm"], w["wout"], w["v1"]), exch=exch)


def _store_blocks(acc, slabs_ref, sems, row_off, rows, first_block, n_blocks):
    copies = [pltpu.make_async_copy(acc.at[pl.ds(q * rows, rows)], slabs_ref.at[first_block + q, pl.ds(row_off, rows)],
                                    sems.at[q]) for q in range(n_blocks)]
    for cp in copies:
        cp.start()
    for cp in copies:
        cp.wait()


def _mixer_bwd_call(dx1, z1, hb, a1, cat, p, gs, rv, x, w, ts, slabs, off_out, off_in):
    seq = dx1.shape[0]
    n = seq // ts
    halo_blocks = ts // HALO

    def body(slabs_in, dx1_ref, z1_ref, hb_ref, hprev_ref, a1_ref, cat_ref, p_ref, gs_ref, rv_ref, x_ref, cw_ref, va_ref,
             wcatt_ref, ktt_ref, vmt_ref, woutt_ref, wint_ref, v1_ref,
             slabs_ref, dx_ref, dkt_ref, dvm_ref, dwcat_ref, dmsum_ref, dva_ref, dcw_ref, dv1_ref,
             buf, dbuf, da0buf, dwout_ref, dwin_ref, sems, sh, dsh):
        i = pl.program_id(0)

        @pl.when(i == 0)
        def _():
            for ref in (dwout_ref, dwin_ref, dkt_ref, dvm_ref, dwcat_ref, dmsum_ref, dva_ref, dcw_ref, dv1_ref):
                ref[...] = jnp.zeros(ref.shape, F32)
            dbuf[ts:ts + HALO, :] = jnp.zeros((HALO, CONV_W), F32)

        @pl.when(i > 0)
        def _():
            dbuf[ts:ts + HALO, :] = dbuf[0:HALO, :]

        dx1v = dx1_ref[...]
        xh1, r1 = _ln_stats(z1_ref[...])
        dv1_ref[0:1, :] += _rowsum(dx1v * xh1)
        dv1_ref[1:2, :] += _rowsum(dx1v)
        dz1 = _ln_bwd(dx1v, xh1, r1, v1_ref[0:1, :])
        dmix = dz1.astype(BF16)

        hf = hb_ref[:, 0:2 * CONV_W].astype(F32)
        hp = hprev_ref[...].astype(F32)
        a0p = hp[:, 0:CONV_W] * _sigmoid(hp[:, CONV_W:2 * CONV_W])
        buf[0:HALO, :] = jnp.where(i == n - 1, 0.0, a0p)
        ga = _group_a_fwd(hf, buf, sh, a1_ref, cw_ref, va_ref, ts, conv=False)
        gb = {k: gs_ref[:, j * GMLP_W:(j + 1) * GMLP_W].astype(F32) for j, k in enumerate(GROUP_B_SAVED)}
        vn = gb["vhat"] * va_ref[3:4, :] + va_ref[4:5, :]
        pb = p_ref[...]
        gc = dict(qb=hb_ref[:, IN_W - XATTN_W:IN_W], pb=pb, p=pb.astype(F32))

        dwout_ref[...] += _dot_tn(cat_ref[...], dmix)
        dcat = _dot(dmix, woutt_ref[...])
        da = dcat[:, 0:CONV_W]
        dg = dcat[:, CONV_W:CONV_W + GMLP_W]
        dc = dcat[:, CONV_W + GMLP_W:D_MODEL].astype(BF16)

        dp = _dot(dc, vmt_ref[...])
        dvm_ref[...] += _dot_tn(gc["pb"], dc)
        dss = []
        for g in range(XATTN_HEADS):
            sl = slice(g * N_MEM, (g + 1) * N_MEM)
            pg = gc["p"][:, sl]
            dpg = dp[:, sl]
            dss.append(pg * (dpg - jnp.sum(dpg * pg, axis=-1, keepdims=True)))
        ds = jnp.concatenate(dss, axis=1).astype(BF16)
        dq = _dot(ds, ktt_ref[...])
        dkt_ref[...] += _dot_tn(gc["qb"], ds)

        dmixed = dg * gb["u"]
        dhu = dg * gb["mixed"] * gb["du"]
        dvns = []
        for j, c0 in enumerate(range(0, ts, CHUNK)):
            dm = dmixed[c0:c0 + CHUNK, :]
            dmb = dm.astype(BF16)
            dmsum_ref[...] += dm
            dwcat_ref[...] += _dot_nt(dmb, _stack_heads(vn[c0:c0 + CHUNK, :]).astype(BF16))
            dst = _dot(wcatt_ref[...], dmb)
            dvn_c = jnp.zeros((CHUNK, GMLP_W), F32)
            for h in range(GMLP_HEADS):
                dvn_c = dvn_c + jnp.where(_head_mask(GMLP_W, h), dst[h * CHUNK:(h + 1) * CHUNK, :], 0.0)
            dvns.append(dvn_c)
        dvn = jnp.concatenate(dvns, axis=0) if len(dvns) > 1 else dvns[0]
        dva_ref[3:4, :] += _rowsum(dvn * gb["vhat"])
        dva_ref[4:5, :] += _rowsum(dvn)
        dhv = _ln_bwd(dvn, gb["vhat"], rv_ref[:, 0:1], va_ref[3:4, :]) * gb["dv"]

        a2, sa = ga["a2"], ga["sa"]
        da2 = da * (sa * (1.0 + a2 * (1.0 - sa)))
        dva_ref[1:2, :] += _rowsum(da2 * ga["a2h"])
        dva_ref[2:3, :] += _rowsum(da2)
        da1 = _ln_bwd(da2, ga["a2h"], ga["ra"], va_ref[1:2, :])
        dva_ref[0:1, :] += _rowsum(da1)
        dbuf[0:ts, :] = da1
        _shift_copies(dbuf, dsh, ts + HALO - 8)
        _conv31_dw(buf, sh, dbuf, dcw_ref, ts)
        _conv31_dx(dbuf, dsh, cw_ref, da0buf, ts)
        da0 = da0buf[...]
        sg = ga["sg"]
        dha = da0 * sg
        dhg = da0 * ga["ha"] * sg * (1.0 - sg)

        dh = jnp.concatenate([dha, dhg, dhu, dhv, dq], axis=1).astype(BF16)
        dx_ref[...] = _dot(dh, wint_ref[...]) + ALPHA * dz1
        dwin_ref[...] += _dot_tn(dh, x_ref[...].astype(BF16))

        @pl.when(i == n - 1)
        def _():
            _store_blocks(dwout_ref, slabs_ref, sems, off_out, D_MODEL // N_DEV, 0, N_DEV)
            _store_blocks(dwin_ref, slabs_ref, sems, off_in, IN_W // N_DEV, 0, N_DEV)

    rev = lambda width: pl.BlockSpec((ts, width), lambda i: (n - 1 - i, 0))
    prev = pl.BlockSpec((HALO, 2 * CONV_W), lambda i: (jnp.maximum((n - 1 - i) * halo_blocks - 1, 0), 0))
    hbm = pl.BlockSpec(memory_space=pl.ANY)
    hc = GMLP_HEADS * CHUNK
    am = XATTN_HEADS * N_MEM
    return pl.pallas_call(
        body, name="mixer_bwd", grid=(n,),
        in_specs=[hbm, rev(D_MODEL), rev(D_MODEL), rev(IN_W), prev, rev(CONV_W), rev(D_MODEL), rev(am),
                  rev(len(GROUP_B_SAVED) * GMLP_W), rev(128), rev(D_MODEL), _const((HALO, CONV_W)),
                  _const((8, CONV_W)), _const((hc, CHUNK)), _const((am, XATTN_W)), _const((XATTN_W, am)),
                  _const((D_MODEL, D_MODEL)), _const((IN_W, D_MODEL)), _const((8, D_MODEL))],
        out_specs=[hbm, rev(D_MODEL), _acc((XATTN_W, am)), _acc((am, XATTN_W)),
                   _acc((CHUNK, hc)), _acc((CHUNK, GMLP_W)), _acc((8, CONV_W)), _acc((HALO, CONV_W)),
                   _acc((8, D_MODEL))],
        out_shape=[jax.ShapeDtypeStruct(slabs.shape, F32), jax.ShapeDtypeStruct((seq, D_MODEL), F32),
                   jax.ShapeDtypeStruct((XATTN_W, am), F32),
                   jax.ShapeDtypeStruct((am, XATTN_W), F32), jax.ShapeDtypeStruct((CHUNK, hc), F32),
                   jax.ShapeDtypeStruct((CHUNK, GMLP_W), F32), jax.ShapeDtypeStruct((8, CONV_W), F32),
                   jax.ShapeDtypeStruct((HALO, CONV_W), F32), jax.ShapeDtypeStruct((8, D_MODEL), F32)],
        scratch_shapes=[pltpu.VMEM((ts + HALO, CONV_W), F32),
                        pltpu.VMEM((ts + HALO, CONV_W), F32), pltpu.VMEM((ts, CONV_W), F32),
                        pltpu.VMEM((D_MODEL, D_MODEL), F32), pltpu.VMEM((IN_W, D_MODEL), F32),
                        pltpu.SemaphoreType.DMA((N_DEV,)),
                        pltpu.VMEM((7, ts + HALO, CONV_W), F32), pltpu.VMEM((7, ts + HALO, CONV_W), F32)],
        input_output_aliases={0: 0},
        compiler_params=_params(dimension_semantics=("arbitrary",)),
    )(slabs, dx1, z1, hb, hb, a1, cat, p, gs, rv, x, w["cw"], w["va"], w["wcatt"], w["ktt"], w["vmt"], w["woutt"],
      w["wint"], w["v1"])


FFN_HALO = 8
FF_GROUP = D_FF // N_DEV
FF_GROUP_PAD = D_FF_PAD // N_DEV


def _ffn_taps(ubuf, ts, lo, hi):
    return tuple(ubuf[FFN_HALO - (FFN_CONV_K - 1) + k:FFN_HALO - (FFN_CONV_K - 1) + k + ts, lo:hi]
                 for k in range(FFN_CONV_K))


def _ffn_gate(ubuf, cf_ref, ts, lo, hi):
    taps = _ffn_taps(ubuf, ts, lo, hi)
    g = cf_ref[3:4, lo:hi] + cf_ref[2:3, lo:hi] * taps[2]
    g = g + cf_ref[1:2, lo:hi] * taps[1]
    return g + cf_ref[0:1, lo:hi] * taps[0]


FFN_CHUNK = 256
FFN_CHUNKS = tuple((lo, lo + FFN_CHUNK) for lo in range(0, D_FF_PAD, FFN_CHUNK))


def _ffn_fwd_call(x1, w, ts, exch=None, want_x2=True):
    seq = x1.shape[0]
    n = seq // ts
    n_wide = 4

    def body(x1_ref, wg_ref, wv_ref, cf_ref, wdown_ref, v2_ref, ug_ref, uv_ref, sl_ref, dsl_ref, z2_ref, *rest):
        x2_ref = rest[0] if want_x2 else None
        ubuf, act_buf = rest[-2:]
        i = pl.program_id(0)

        @pl.when(i == 0)
        def _():
            ubuf[0:FFN_HALO, :] = jnp.zeros((FFN_HALO, D_FF_PAD), F32)

        @pl.when(i > 0)
        def _():
            ubuf[0:FFN_HALO, :] = ubuf[ts:ts + FFN_HALO, :]

        xv = x1_ref[...]
        xb = xv.astype(BF16)
        for lo, hi in FFN_CHUNKS:
            ug = _dot(xb, wg_ref[:, lo:hi]).astype(BF16)
            uv = _dot(xb, wv_ref[:, lo:hi]).astype(BF16)
            ug_ref[:, lo:hi] = ug
            uv_ref[:, lo:hi] = uv
            ubuf[FFN_HALO:FFN_HALO + ts, lo:hi] = ug.astype(F32)
            gate = _ffn_gate(ubuf, cf_ref, ts, lo, hi)
            sg = _sigmoid(gate)
            sl = gate * sg
            sl_ref[:, lo:hi] = sl.astype(BF16)
            dsl_ref[:, lo:hi] = (sg * (1.0 + gate * (1.0 - sg))).astype(BF16)
            act_buf[:, lo:hi] = (sl * uv.astype(F32)).astype(BF16)
        y = ALPHA * xv + _dot(act_buf[...], wdown_ref[...])
        z2_ref[...] = y
        if want_x2:
            xh, _ = _ln_stats(y)
            x2_ref[...] = xh * v2_ref[0:1, :] + v2_ref[1:2, :]

    row = lambda width: pl.BlockSpec((ts, width), lambda i: (i, 0))
    n_narrow = 2 if want_x2 else 1
    return _grid_call(
        body, "ffn_fwd" if want_x2 else "ffn_fwd_last", n,
        in_specs=[row(D_MODEL), _const((D_MODEL, D_FF_PAD)), _const((D_MODEL, D_FF_PAD)), _const((8, D_FF_PAD)),
                  _const((D_FF_PAD, D_MODEL)), _const((8, D_MODEL))],
        out_specs=[row(D_FF_PAD)] * n_wide + [row(D_MODEL)] * n_narrow,
        out_shape=[jax.ShapeDtypeStruct((seq, D_FF_PAD), BF16)] * n_wide
                  + [jax.ShapeDtypeStruct((seq, D_MODEL), F32)] * n_narrow,
        scratch_shapes=[pltpu.VMEM((ts + FFN_HALO, D_FF_PAD), F32), pltpu.VMEM((ts, D_FF_PAD), BF16)],
        args=(x1, w["wg"], w["wv"], w["cf"], w["wdown"], w["v2"]), exch=exch)


def _ffn_bwd_call(dx2_or_target, z2, ug, uv, sl, dsl, w, ts, last, slabs, row_off, exch=None):
    seq = z2.shape[0]
    n = seq // ts
    halo_blocks = ts // 16

    def body(slabs_in, dx2_ref, z2_ref, ug_ref, uv_ref, sl_ref, dsl_ref, uprev_ref, cf_ref, wdownt_ref, v2_ref,
             slabs_ref, dug_ref, duv_ref, dz2_ref, dcf_ref, dv2_ref, loss_ref,
             ubuf, dgbuf, dwacc, sems):
        i = pl.program_id(0)

        @pl.when(i == 0)
        def _():
            dwacc[...] = jnp.zeros(dwacc.shape, F32)
            dcf_ref[...] = jnp.zeros(dcf_ref.shape, F32)
            dv2_ref[...] = jnp.zeros(dv2_ref.shape, F32)
            loss_ref[...] = jnp.zeros(loss_ref.shape, F32)
            dgbuf[ts:ts + FFN_HALO, :] = jnp.zeros((FFN_HALO, D_FF_PAD), F32)

        @pl.when(i > 0)
        def _():
            dgbuf[ts:ts + FFN_HALO, :] = dgbuf[0:FFN_HALO, :]

        xh2, r2 = _ln_stats(z2_ref[...])
        if last:
            diff = xh2 * v2_ref[0:1, :] + v2_ref[1:2, :] - dx2_ref[...]
            loss_ref[...] += jnp.sum(diff * diff) * (0.5 / D_MODEL)
            dx2v = diff * (1.0 / D_MODEL)
        else:
            dx2v = dx2_ref[...]
        dv2_ref[0:1, :] += _rowsum(dx2v * xh2)
        dv2_ref[1:2, :] += _rowsum(dx2v)
        dz2 = _ln_bwd(dx2v, xh2, r2, v2_ref[0:1, :])
        dz2_ref[...] = dz2
        dy = dz2.astype(BF16)

        up = uprev_ref[...].astype(F32)[8:16, :]
        ubuf[0:FFN_HALO, :] = jnp.where(i == n - 1, 0.0, up)
        ubuf[FFN_HALO:FFN_HALO + ts, :] = ug_ref[...].astype(F32)
        for lo, hi in FFN_CHUNKS:
            taps = _ffn_taps(ubuf, ts, lo, hi)
            sl = sl_ref[:, lo:hi].astype(F32)
            uvf = uv_ref[:, lo:hi].astype(F32)
            act = (sl * uvf).astype(BF16)
            dwacc[lo:hi, :] += _dot_tn(act, dy)
            dact = _dot(dy, wdownt_ref[:, lo:hi])
            duv_ref[:, lo:hi] = (dact * sl).astype(BF16)
            dgate = dact * uvf * dsl_ref[:, lo:hi].astype(F32)
            dgbuf[0:ts, lo:hi] = dgate
            dcf_ref[3:4, lo:hi] += _rowsum(dgate)
            for k in range(FFN_CONV_K):
                dcf_ref[k:k + 1, lo:hi] += _rowsum(dgate * taps[k])
            dug = cf_ref[2:3, lo:hi] * dgate + cf_ref[1:2, lo:hi] * dgbuf[1:1 + ts, lo:hi]
            dug = dug + cf_ref[0:1, lo:hi] * dgbuf[2:2 + ts, lo:hi]
            dug_ref[:, lo:hi] = dug.astype(BF16)

        @pl.when(i == n - 1)
        def _():
            _store_blocks(dwacc, slabs_ref, sems, row_off, D_FF_PAD // N_DEV, 0, N_DEV)

    rev = lambda width: pl.BlockSpec((ts, width), lambda i: (n - 1 - i, 0))
    prev = pl.BlockSpec((16, D_FF_PAD), lambda i: (jnp.maximum((n - 1 - i) * halo_blocks - 1, 0), 0))
    hbm = pl.BlockSpec(memory_space=pl.ANY)
    return _grid_call(
        body, "ffn_bwd_last" if last else "ffn_bwd", n,
        in_specs=[hbm, rev(D_MODEL), rev(D_MODEL)] + [rev(D_FF_PAD)] * 4 + [prev, _const((8, D_FF_PAD)),
                                                                           _const((D_MODEL, D_FF_PAD)), _const((8, D_MODEL))],
        out_specs=[hbm, rev(D_FF_PAD), rev(D_FF_PAD), rev(D_MODEL),
                   _acc((8, D_FF_PAD)), _acc((8, D_MODEL)), _acc((8, 128))],
        out_shape=[jax.ShapeDtypeStruct(slabs.shape, F32),
                   jax.ShapeDtypeStruct((seq, D_FF_PAD), BF16), jax.ShapeDtypeStruct((seq, D_FF_PAD), BF16),
                   jax.ShapeDtypeStruct((seq, D_MODEL), F32),
                   jax.ShapeDtypeStruct((8, D_FF_PAD), F32), jax.ShapeDtypeStruct((8, D_MODEL), F32),
                   jax.ShapeDtypeStruct((8, 128), F32)],
        scratch_shapes=[pltpu.VMEM((ts + FFN_HALO, D_FF_PAD), F32), pltpu.VMEM((ts + FFN_HALO, D_FF_PAD), F32),
                        pltpu.VMEM((D_FF_PAD, D_MODEL), F32), pltpu.SemaphoreType.DMA((N_DEV,))],
        args=(slabs, dx2_or_target, z2, ug, uv, sl, dsl, ug, w["cf"], w["wdownt"], w["v2"]), aliases={0: 0}, exch=exch)


def _proj_bwd_call(d, wt, xin, addend, scale, ts, name, slabs, row_off, first_block, n_blocks, exch=None):
    seq, k = d.shape
    n = seq // ts

    def body(slabs_in, d_ref, wt_ref, xin_ref, add_ref, slabs_ref, dx_ref, acc, sems):
        i = pl.program_id(0)

        @pl.when(i == 0)
        def _():
            acc[...] = jnp.zeros(acc.shape, F32)

        dv = d_ref[...]
        dx_ref[...] = _dot(dv, wt_ref[...]) + scale * add_ref[...]
        acc[...] += _dot_tn(dv, xin_ref[...].astype(BF16))

        @pl.when(i == n - 1)
        def _():
            _store_blocks(acc, slabs_ref, sems, row_off, k // n_blocks, first_block, n_blocks)

    row = lambda width: pl.BlockSpec((ts, width), lambda i: (i, 0))
    hbm = pl.BlockSpec(memory_space=pl.ANY)
    return _grid_call(
        body, name, n,
        in_specs=[hbm, row(k), _const((k, D_MODEL)), row(D_MODEL), row(D_MODEL)],
        out_specs=[hbm, row(D_MODEL)],
        out_shape=[jax.ShapeDtypeStruct(slabs.shape, F32), jax.ShapeDtypeStruct((seq, D_MODEL), F32)],
        scratch_shapes=[pltpu.VMEM((k, D_MODEL), F32), pltpu.SemaphoreType.DMA((n_blocks,))],
        args=(slabs, d, wt, xin, addend), aliases={0: 0}, exch=exch)


MEM_FOLD = BLOB_LANES // XATTN_W


def _mem_proj_call(memq, wk_flat, wv_flat):
    def body(memq_ref, wk_ref, wv_ref, kh_ref, vh_ref):
        for w_ref, o_ref in ((wk_ref, kh_ref), (wv_ref, vh_ref)):
            acc = jnp.zeros((N_MEM, XATTN_W), F32)
            for q in range(MEM_FOLD):
                acc = acc + _dot(memq_ref[q], w_ref[:, q * XATTN_W:(q + 1) * XATTN_W])
            o_ref[...] = acc

    out = jax.ShapeDtypeStruct((N_MEM, XATTN_W), F32)
    return pl.pallas_call(body, name="mem_proj", out_shape=[out, out], compiler_params=_params())(memq, wk_flat, wv_flat)


def _mem_proj_bwd_call(memq, dkh, dvh, slabs, off_k, off_v):
    rows = D_MODEL // MEM_FOLD

    def body(slabs_in, memq_ref, dkh_ref, dvh_ref, slabs_ref, acc, sems):
        for d_ref, off in ((dkh_ref, off_k), (dvh_ref, off_v)):
            dv = d_ref[...].astype(BF16)
            for q in range(MEM_FOLD):
                acc[:, q * XATTN_W:(q + 1) * XATTN_W] = _dot_tn(memq_ref[q], dv)
            _store_blocks(acc, slabs_ref, sems, off, rows // N_DEV, 0, N_DEV)

    hbm = pl.BlockSpec(memory_space=pl.ANY)
    vmem = pl.BlockSpec(memory_space=pltpu.VMEM)
    return pl.pallas_call(
        body, name="mem_proj_bwd", in_specs=[hbm, vmem, vmem, vmem], out_specs=hbm,
        out_shape=jax.ShapeDtypeStruct(slabs.shape, F32),
        scratch_shapes=[pltpu.VMEM((rows, BLOB_LANES), F32), pltpu.SemaphoreType.DMA((N_DEV,))],
        input_output_aliases={0: 0}, compiler_params=_params(),
    )(slabs, memq, dkh, dvh)


def _place():
    return lax.axis_index("x"), lax.axis_index("y"), lax.axis_index("c")


def _all_gather_call(arrs, pieces, name, swap=None):
    n_in, n_p = len(arrs), len(pieces)
    n_sw = 0 if swap is None else 1

    def body(*refs):
        ins, outs = refs[:n_in], refs[n_in + n_sw:n_in + n_sw + n_p]
        send_sems, recv_sems, local_sems = refs[n_in + 2 * n_sw + n_p:n_in + 2 * n_sw + n_p + 3]
        swapped = []
        if swap is not None:
            swapped = swap[1](refs[n_in], refs[n_in + n_sw + n_p], *refs[-2:])
            for cp in swapped:
                cp.start()
        x, y, c = _place()
        me, sibling = (x, y, c), (x, y, 1 - c)
        chips = [(1 - x, y), (x, 1 - y), (1 - x, 1 - y)]

        def src(a):
            idx, r0, rows = pieces[a]
            return ins[idx] if r0 is None else ins[idx].at[pl.ds(r0, rows)]

        def slab(a, p):
            return outs[a].at[4 * p[0] + 2 * p[1] + p[2]]

        def copy(a, k, block, to, own=False):
            return pltpu.make_async_remote_copy(
                src_ref=src(a) if own else slab(a, block), dst_ref=slab(a, block),
                send_sem=send_sems.at[a, k], recv_sem=recv_sems.at[a, k], device_id=to, device_id_type=MESH)

        mine = [pltpu.make_async_copy(src(a), slab(a, me), local_sems.at[a]) for a in range(n_p)]
        for cp in mine:
            cp.start()
        first = []
        for a in range(n_p):
            first.append(copy(a, 0, me, sibling, own=True))
            first += [copy(a, 1 + j, me, (*chip, c), own=True) for j, chip in enumerate(chips)]
        for cp in first:
            cp.start()
        passed = []
        for a in range(n_p):
            for j, chip in enumerate(chips):
                copy(a, 1 + j, (*chip, c), me).wait_recv()
                cp = copy(a, 4 + j, (*chip, c), sibling)
                cp.start()
                passed.append(cp)
        for a in range(n_p):
            copy(a, 0, sibling, me).wait_recv()
            for j, chip in enumerate(chips):
                copy(a, 4 + j, (*chip, 1 - c), me).wait_recv()
        for cp in first + passed:
            cp.wait_send()
        for cp in mine:
            cp.wait()
        for cp in swapped:
            cp.wait()

    def out_shape(piece):
        idx, r0, rows = piece
        a = arrs[idx]
        return jax.ShapeDtypeStruct((N_DEV,) + (a.shape if r0 is None else (rows,) + a.shape[1:]), a.dtype)

    hbm = pl.BlockSpec(memory_space=pl.ANY)
    out_shapes = [out_shape(p) for p in pieces]
    scratch = [pltpu.SemaphoreType.DMA((n_p, 7)), pltpu.SemaphoreType.DMA((n_p, 7)), pltpu.SemaphoreType.DMA((n_p,))]
    args = list(arrs)
    if swap is not None:
        a, _, n = swap
        args.append(a)
        out_shapes.append(jax.ShapeDtypeStruct((n,) + a.shape[1:], a.dtype))
        scratch += [pltpu.SemaphoreType.DMA((n,)), pltpu.SemaphoreType.DMA((n,))]
    return pl.pallas_call(
        body, name=name, in_specs=[hbm] * len(args), out_specs=[hbm] * len(out_shapes), out_shape=out_shapes,
        scratch_shapes=scratch,
    )(*args)


def _flip(v, f):
    return 1 - v if f else v


def _gather_exchange(arrs, pieces):
    n_peers = N_DEV - 1

    def build(ins, outs, send_sems, recv_sems, local_sems):
        x, y, c = _place()
        flips = [(fx, fy, fc) for fx in (0, 1) for fy in (0, 1) for fc in (0, 1) if fx or fy or fc]
        remote, local = [], []
        for a, (idx, r0, rows) in enumerate(pieces):
            src = ins[idx] if r0 is None else ins[idx].at[pl.ds(r0, rows)]
            dst = outs[a].at[4 * x + 2 * y + c]
            remote += [pltpu.make_async_remote_copy(
                src_ref=src, dst_ref=dst, send_sem=send_sems.at[n_peers * a + k], recv_sem=recv_sems.at[n_peers * a + k],
                device_id=(_flip(x, fx), _flip(y, fy), _flip(c, fc)), device_id_type=MESH)
                for k, (fx, fy, fc) in enumerate(flips)]
            local.append(pltpu.make_async_copy(src, dst, local_sems.at[a]))
        return remote, local

    def out_shape(piece):
        idx, r0, rows = piece
        a = arrs[idx]
        return jax.ShapeDtypeStruct((N_DEV,) + (a.shape if r0 is None else (rows,) + a.shape[1:]), a.dtype)

    return _Exchange(arrs, [out_shape(p) for p in pieces], n_peers * len(pieces), build, n_local=len(pieces))


def _swap_core_copies(g_ref, r_ref, send_sems, recv_sems):
    x, y, c = _place()
    return [pltpu.make_async_remote_copy(
        src_ref=g_ref.at[2 * k + (1 - c)], dst_ref=r_ref.at[k], send_sem=send_sems.at[k], recv_sem=recv_sems.at[k],
        device_id=(x, y, 1 - c), device_id_type=MESH) for k in range(4)]


def _swap_chip_copies(p_ref, r_ref, send_sems, recv_sems):
    x, y, c = _place()
    chips = [(1 - x, y), (x, 1 - y), (1 - x, 1 - y)]
    return [pltpu.make_async_remote_copy(
        src_ref=p_ref.at[2 * px + py], dst_ref=r_ref.at[j], send_sem=send_sems.at[j], recv_sem=recv_sems.at[j],
        device_id=(px, py, c), device_id_type=MESH) for j, (px, py) in enumerate(chips)]


def _swap_exchange(a, copies, n):
    return _Exchange([a], [jax.ShapeDtypeStruct((n,) + a.shape[1:], a.dtype)], n,
                     lambda ins, outs, send_sems, recv_sems, local_sems: (copies(ins[0], outs[0], send_sems, recv_sems), []))


def _swap_call(a, copies, n, name):
    def body(a_ref, r_ref, send_sems, recv_sems):
        cps = copies(a_ref, r_ref, send_sems, recv_sems)
        for cp in cps:
            cp.start()
        for cp in cps:
            cp.wait()

    hbm = pl.BlockSpec(memory_space=pl.ANY)
    return pl.pallas_call(
        body, name=name, in_specs=[hbm], out_specs=hbm, out_shape=jax.ShapeDtypeStruct((n,) + a.shape[1:], a.dtype),
        scratch_shapes=[pltpu.SemaphoreType.DMA((n,)), pltpu.SemaphoreType.DMA((n,))],
    )(a)


def _gather_start(arrs, pieces):
    exch = _gather_exchange(arrs, pieces)
    n_in, n_p = len(arrs), len(pieces)
    lands = [lax.empty(s.shape, s.dtype) for s in exch.out_shapes]

    def body(*refs):
        ins, zones = refs[:n_in], refs[n_in:n_in + n_p]
        remote, local = exch.build(ins, zones, *refs[n_in + n_p:n_in + n_p + 3])
        for cp in remote + local:
            cp.start()

    hbm, sem = pl.BlockSpec(memory_space=pltpu.HBM), pl.BlockSpec(memory_space=pltpu.SEMAPHORE)
    through = list(arrs) + lands
    return pl.pallas_call(
        body, name="gather_weights_start",
        out_shape=tuple([pltpu.SemaphoreType.DMA((exch.n_copies,)), pltpu.SemaphoreType.DMA((exch.n_copies,)),
                         pltpu.SemaphoreType.DMA((n_p,))] + [pltpu.HBM(a.shape, a.dtype) for a in through]),
        in_specs=(hbm,) * len(through), out_specs=(sem,) * 3 + (hbm,) * len(through),
        input_output_aliases={i: 3 + i for i in range(len(through))},
        compiler_params=pltpu.CompilerParams(has_side_effects=pltpu.SideEffectType.DATAFLOW_SIDE_EFFECTING),
    )(*[pltpu.with_memory_space_constraint(a, pltpu.HBM) for a in through])


def _gather_wait(in_flight, arrs, pieces, after):
    exch = _gather_exchange(arrs, pieces)
    n_in, n_p = len(arrs), len(pieces)
    sems, through = in_flight[:3], in_flight[3:]

    def body(*refs):
        ins, zones = refs[:n_in], refs[n_in:n_in + n_p]
        remote, local = exch.build(ins, zones, *refs[n_in + n_p:n_in + n_p + 3])
        for cp in remote + local:
            cp.wait()

    hbm, sem = pl.BlockSpec(memory_space=pltpu.HBM), pl.BlockSpec(memory_space=pltpu.SEMAPHORE)
    out = pl.pallas_call(
        body, name="gather_weights_wait", out_shape=tuple(pltpu.HBM(a.shape, a.dtype) for a in through),
        in_specs=(hbm,) * len(through) + (sem,) * 3 + (pl.BlockSpec(memory_space=pl.ANY),) * len(after),
        out_specs=(hbm,) * len(through), input_output_aliases={i: i for i in range(len(through))},
        compiler_params=pltpu.CompilerParams(has_side_effects=pltpu.SideEffectType.DATAFLOW_SIDE_EFFECTING),
    )(*through, *sems, *after)
    return list(out[n_in:]), list(out[:n_in])


def _swap_chip_start(p, rider):
    n = 3
    land = lax.empty((n,) + p.shape[1:], p.dtype)

    def body(p_ref, land_ref, rider_ref, send_sems, recv_sems, p_thru, land_thru, rider_thru):
        for cp in _swap_chip_copies(p_ref, land_ref, send_sems, recv_sems):
            cp.start()

    hbm, sem = pl.BlockSpec(memory_space=pltpu.HBM), pl.BlockSpec(memory_space=pltpu.SEMAPHORE)
    return pl.pallas_call(
        body, name="rs_swap_chip_start",
        out_shape=(pltpu.SemaphoreType.DMA((n,)), pltpu.SemaphoreType.DMA((n,)), pltpu.HBM(p.shape, p.dtype),
                   pltpu.HBM(land.shape, land.dtype), pltpu.HBM(rider.shape, rider.dtype)),
        in_specs=(hbm, hbm, hbm), out_specs=(sem, sem, hbm, hbm, hbm), input_output_aliases={0: 2, 1: 3, 2: 4},
        compiler_params=pltpu.CompilerParams(has_side_effects=pltpu.SideEffectType.DATAFLOW_SIDE_EFFECTING),
    )(pltpu.with_memory_space_constraint(p, pltpu.HBM), pltpu.with_memory_space_constraint(land, pltpu.HBM),
      pltpu.with_memory_space_constraint(rider, pltpu.HBM))


def _swap_chip_wait(send_sems, recv_sems, p_thru, land_thru, *after):
    def body(p_ref, land_ref, send_sems, recv_sems, *rest):
        for cp in _swap_chip_copies(p_ref, land_ref, send_sems, recv_sems):
            cp.wait_send()
            cp.wait_recv()

    hbm, sem = pl.BlockSpec(memory_space=pltpu.HBM), pl.BlockSpec(memory_space=pltpu.SEMAPHORE)
    return pl.pallas_call(
        body, name="rs_swap_chip_wait",
        out_shape=(pltpu.HBM(p_thru.shape, p_thru.dtype), pltpu.HBM(land_thru.shape, land_thru.dtype)),
        in_specs=(hbm, hbm, sem, sem) + (pl.BlockSpec(memory_space=pl.ANY),) * len(after), out_specs=(hbm, hbm),
        input_output_aliases={0: 0, 1: 1},
        compiler_params=pltpu.CompilerParams(has_side_effects=pltpu.SideEffectType.DATAFLOW_SIDE_EFFECTING),
    )(p_thru, land_thru, send_sems, recv_sems, *after)[1]


ADD_ROWS = 184


def _pair_add_call(g, r, c):
    _, rows, width = g.shape

    def body(c_ref, g_ref, r_ref, o_ref):
        o_ref[...] = (g_ref[...] + r_ref[...]).astype(BF16)

    return pl.pallas_call(
        body, name="rs_pair_add",
        grid_spec=pltpu.PrefetchScalarGridSpec(
            num_scalar_prefetch=1, grid=(4, rows // ADD_ROWS),
            in_specs=[pl.BlockSpec((None, ADD_ROWS, width), lambda k, i, c_ref: (2 * k + c_ref[0], i, 0)),
                      pl.BlockSpec((None, ADD_ROWS, width), lambda k, i, c_ref: (k, i, 0))],
            out_specs=pl.BlockSpec((None, ADD_ROWS, width), lambda k, i, c_ref: (k, i, 0))),
        out_shape=jax.ShapeDtypeStruct((4, rows, width), BF16),
        compiler_params=_params(dimension_semantics=("arbitrary", "arbitrary")),
    )(c, g, r)


def _adam(w, g, m, v):
    mn = ADAM_B1 * m + (1.0 - ADAM_B1) * g
    vn = ADAM_B2 * v + (1.0 - ADAM_B2) * (g * g)
    m_hat = mn / (1.0 - ADAM_B1 ** ADAM_STEP)
    v_hat = vn / (1.0 - ADAM_B2 ** ADAM_STEP)
    return -ADAM_LR * (m_hat / (jnp.sqrt(v_hat) + ADAM_EPS) + ADAM_WD * w), mn, vn


def _chip_add_adamw_call(slabs, from_sibling, from_chips, me, chip, w, m, v):
    _, rows, width = slabs.shape

    def body(me_ref, chip_ref, own_ref, sib_ref, r_ref, w_ref, m_ref, v_ref, g_ref, d_ref, mo_ref, vo_ref):
        g = own_ref[...] + sib_ref[...]
        for j in range(3):
            g = g + r_ref[j].astype(F32)
        g_ref[...] = g
        d_ref[...], mo_ref[...], vo_ref[...] = _adam(w_ref[...], g, m_ref[...], v_ref[...])

    spec = pl.BlockSpec((ADD_ROWS, width), lambda i, me_ref, chip_ref: (i, 0))
    return pl.pallas_call(
        body, name="rs_chip_add_adamw",
        grid_spec=pltpu.PrefetchScalarGridSpec(
            num_scalar_prefetch=2, grid=(rows // ADD_ROWS,),
            in_specs=[pl.BlockSpec((None, ADD_ROWS, width), lambda i, me_ref, chip_ref: (me_ref[0], i, 0)),
                      pl.BlockSpec((None, ADD_ROWS, width), lambda i, me_ref, chip_ref: (chip_ref[0], i, 0)),
                      pl.BlockSpec((3, ADD_ROWS, width), lambda i, me_ref, chip_ref: (0, i, 0)), spec, spec, spec],
            out_specs=[spec] * 4),
        out_shape=[jax.ShapeDtypeStruct((rows, width), F32)] * 4,
        compiler_params=_params(dimension_semantics=("arbitrary",)),
    )(me, chip, slabs, from_sibling, from_chips, w, m, v)


def _adamw_whole_call(params, name):
    n = len(params)

    def body(*refs):
        ins, outs = refs[:4 * n], refs[4 * n:]
        for a in range(n):
            w_ref, g_ref, m_ref, v_ref = ins[4 * a:4 * a + 4]
            outs[3 * a][...], outs[3 * a + 1][...], outs[3 * a + 2][...] = _adam(w_ref[...], g_ref[...], m_ref[...],
                                                                              v_ref[...])

    flat = [a for p in params for a in p]
    shapes = [jax.ShapeDtypeStruct(p[0].shape, F32) for p in params for _ in range(3)]
    out = pl.pallas_call(body, name=name, out_shape=shapes, compiler_params=_params())(*flat)
    return [tuple(out[3 * a:3 * a + 3]) for a in range(n)]


GATHERED_ACCS = (("dva", (8, CONV_W)), ("dv1", (8, D_MODEL)), ("dv2", (8, D_MODEL)), ("dcf", (8, D_FF_PAD)),
                 ("dcw", (HALO, CONV_W)), ("dwcat", (CHUNK, GMLP_HEADS * CHUNK)), ("dmsum", (CHUNK, GMLP_W)))
VEC_A = ("conv_a_b", "ln_a_g", "ln_a_b", "ln_v_g", "ln_v_b")
REP_IN_KERNEL = VEC_A + ("ln1_g", "ln1_b", "ln2_g", "ln2_b", "w_s", "b_s")


def _replicated_update_call(gathered, p, mom_m, mom_v):
    n_acc = len(GATHERED_ACCS)
    n_rep = len(REP_IN_KERNEL)

    def body(*refs):
        acc_refs = refs[:DEPTH * n_acc]
        wmv = refs[DEPTH * n_acc:DEPTH * n_acc + 3 * n_rep]
        outs = refs[DEPTH * n_acc + 3 * n_rep:]
        out_par = {nm: outs[4 * a:4 * a + 4] for a, nm in enumerate(REP_IN_KERNEL)}
        out_dcf = outs[4 * n_rep:4 * n_rep + DEPTH]
        out_dcw = outs[4 * n_rep + DEPTH:4 * n_rep + 2 * DEPTH]
        par = {nm: wmv[3 * a:3 * a + 3] for a, nm in enumerate(REP_IN_KERNEL)}
        tril = (lax.broadcasted_iota(jnp.int32, (CHUNK, CHUNK), 0) >= lax.broadcasted_iota(jnp.int32, (CHUNK, CHUNK), 1))
        head = lax.broadcasted_iota(jnp.int32, (8, GMLP_W), 0) * HEAD_DIM
        lane = lax.broadcasted_iota(jnp.int32, (8, GMLP_W), 1)
        sel = jnp.where((lane >= head) & (lane < head + HEAD_DIM), 1.0, 0.0)

        def update(nm, idx, g):
            w_ref, m_ref, v_ref = par[nm]
            d, mn, vn = _adam(w_ref[idx], g, m_ref[idx], v_ref[idx])
            g_ref, d_ref, mo_ref, vo_ref = out_par[nm]
            g_ref[idx] = g
            d_ref[idx] = d
            mo_ref[idx] = mn
            vo_ref[idx] = vn

        for l in range(DEPTH):
            tot = {}
            for a, (nm, _) in enumerate(GATHERED_ACCS):
                ref = acc_refs[l * n_acc + a]
                s = ref[0]
                for j in range(1, N_DEV):
                    s = s + ref[j]
                tot[nm] = s
            out_dcf[l][...] = tot["dcf"]
            out_dcw[l][...] = tot["dcw"]
            row = (slice(l, l + 1), slice(None))
            for k, nm in enumerate(VEC_A):
                update(nm, row, tot["dva"][k:k + 1, :])
            update("ln1_g", row, tot["dv1"][0:1, :])
            update("ln1_b", row, tot["dv1"][1:2, :])
            update("ln2_g", row, tot["dv2"][0:1, :])
            update("ln2_b", row, tot["dv2"][1:2, :])
            for h in range(GMLP_HEADS):
                gw = jnp.where(tril, tot["dwcat"][:, h * CHUNK:(h + 1) * CHUNK], 0.0)
                update("w_s", (l, h), gw)
            gb = lax.dot_general(sel, tot["dmsum"], (((1,), (1,)), ((), ())), precision=lax.Precision.HIGHEST,
                                 preferred_element_type=F32)
            for h in range(GMLP_HEADS):
                update("b_s", (l, slice(h, h + 1), slice(None)), gb[h:h + 1, :])

    ins = [gathered[l][nm] for l in range(DEPTH) for nm, _ in GATHERED_ACCS]
    ins += [t[nm] for nm in REP_IN_KERNEL for t in (p, mom_m, mom_v)]
    shapes = [jax.ShapeDtypeStruct(p[nm].shape, F32) for nm in REP_IN_KERNEL for _ in range(4)]
    shapes += [jax.ShapeDtypeStruct((8, D_FF_PAD), F32)] * DEPTH + [jax.ShapeDtypeStruct((HALO, CONV_W), F32)] * DEPTH
    out = pl.pallas_call(body, name="replicated_update", out_shape=shapes, compiler_params=_params())(*ins)
    res = [{nm: out[4 * a + k] for a, nm in enumerate(REP_IN_KERNEL)} for k in range(4)]
    return res, out[4 * n_rep:4 * n_rep + DEPTH], out[4 * n_rep + DEPTH:]


BLOCK_ROWS = (("w_in", IN_W // N_DEV), ("w_out", D_MODEL // N_DEV), ("w_up", 2 * FF_GROUP_PAD),
              ("w_down", FF_GROUP_PAD), ("w_mk", D_MODEL // N_DEV // MEM_FOLD), ("w_mv", D_MODEL // N_DEV // MEM_FOLD))
LAYER_ROWS = sum(r for _, r in BLOCK_ROWS)
assert LAYER_ROWS % ADD_ROWS == 0 and all(r % 16 == 0 for _, r in BLOCK_ROWS)


def _row_off(name):
    off = 0
    for nm, r in BLOCK_ROWS:
        if nm == name:
            return off
        off += r
    raise KeyError(name)


def _to_rows(name, a):
    if name == "w_in":
        return a.T
    if name == "w_up":
        t = a.T.reshape(2, FF_GROUP, D_MODEL)
        return jnp.pad(t, ((0, 0), (0, FF_GROUP_PAD - FF_GROUP), (0, 0))).reshape(2 * FF_GROUP_PAD, D_MODEL)
    if name == "w_down":
        return jnp.pad(a, ((0, FF_GROUP_PAD - FF_GROUP), (0, 0)))
    if name == "w_out":
        return a
    return a.reshape(-1, BLOB_LANES)


def _from_rows(name, r):
    if name == "w_in":
        return r.T
    if name == "w_up":
        return r.reshape(2, FF_GROUP_PAD, D_MODEL)[:, :FF_GROUP].reshape(2 * FF_GROUP, D_MODEL).T
    if name == "w_down":
        return r[:FF_GROUP]
    if name == "w_out":
        return r
    return r.reshape(D_MODEL // N_DEV, XATTN_W)


def _blob(tree, l):
    return jnp.concatenate([_to_rows(nm, tree[nm][l]) for nm, _ in BLOCK_ROWS], axis=0)


def _unblob(blobs):
    return {nm: jnp.stack([_from_rows(nm, b[_row_off(nm):_row_off(nm) + r]) for b in blobs]) for nm, r in BLOCK_ROWS}


def _ff_interleave(a):
    lead = a.shape[:-1]
    t = a.reshape(lead + (N_DEV, FF_GROUP))
    return jnp.pad(t, [(0, 0)] * len(lead) + [(0, 0), (0, FF_GROUP_PAD - FF_GROUP)]).reshape(lead + (D_FF_PAD,))


def _ff_deinterleave(a):
    lead = a.shape[:-1]
    return a.reshape(lead + (N_DEV, FF_GROUP_PAD))[..., :FF_GROUP].reshape(lead + (D_FF,))


def _head_table():
    hd = jnp.arange(XATTN_W) // HEAD_DIM
    return (hd[None, :] == jnp.arange(XATTN_HEADS)[:, None]).astype(F32)


def _mixer_operands(mat, conv_a_w, p, l, memq):
    w = {}
    w["wint"] = mat["w_in"]
    w["win"] = mat["w_in"].T
    w["wout"] = mat["w_out"]
    w["woutt"] = mat["w_out"].T
    w["cw"] = conv_a_w
    zeros = jnp.zeros((3, CONV_W), F32)
    w["va"] = jnp.concatenate([p[nm][l][None] for nm in VEC_A] + [zeros], axis=0)
    tril = jnp.tril(jnp.ones((CHUNK, CHUNK), F32))
    w["wcat"] = (p["w_s"][l] * tril[None]).transpose(1, 0, 2).reshape(CHUNK, GMLP_HEADS * CHUNK).astype(BF16)
    w["wcatt"] = w["wcat"].T
    w["bfull"] = jnp.repeat(p["b_s"][l].T, HEAD_DIM, axis=1)
    kh, vh = _mem_proj_call(memq, mat["w_mk"], mat["w_mv"])
    hm = _head_table()
    scale = 1.0 / math.sqrt(HEAD_DIM)
    w["kt"] = (kh.T[:, None, :] * hm.T[:, :, None] * scale).reshape(XATTN_W, XATTN_HEADS * N_MEM).astype(BF16)
    w["ktt"] = w["kt"].T
    w["vm"] = (hm[:, None, :] * vh[None]).reshape(XATTN_HEADS * N_MEM, XATTN_W).astype(BF16)
    w["vmt"] = w["vm"].T
    zeros = jnp.zeros((6, D_MODEL), F32)
    w["v1"] = jnp.concatenate([p["ln1_g"][l][None], p["ln1_b"][l][None], zeros], axis=0)
    return w


def _ffn_operands(w_up, w_down, conv_f_w, p, l):
    w = {}
    w["wgt"] = w_up[:D_FF_PAD]
    w["wvt"] = w_up[D_FF_PAD:]
    w["wg"] = w["wgt"].T
    w["wv"] = w["wvt"].T
    w["wdown"] = w_down
    w["wdownt"] = w_down.T
    zeros = jnp.zeros((6, D_MODEL), F32)
    w["v2"] = jnp.concatenate([p["ln2_g"][l][None], p["ln2_b"][l][None], zeros], axis=0)
    w["cf"] = jnp.concatenate([conv_f_w, _ff_interleave(p["conv_f_b"][l][None]), jnp.zeros((4, D_FF_PAD), F32)], axis=0)
    return w


TS_MIXER = 256
TS_FFN = 256
TS_PROJ = 512
CONV_A_SHARD = CONV_W // N_DEV


def kernel(x, mem, w_in, conv_a_w, conv_a_b, ln_a_g, ln_a_b, ln_v_g, ln_v_b, w_s, b_s, w_mk, w_mv, w_out, ln1_g, ln1_b, w_up, conv_f_w, conv_f_b, w_down, ln2_g, ln2_b, loss_target, m_w_in, m_conv_a_w, m_conv_a_b, m_ln_a_g, m_ln_a_b, m_ln_v_g, m_ln_v_b, m_w_s, m_b_s, m_w_mk, m_w_mv, m_w_out, m_ln1_g, m_ln1_b, m_w_up, m_conv_f_w, m_conv_f_b, m_w_down, m_ln2_g, m_ln2_b, v_w_in, v_conv_a_w, v_conv_a_b, v_ln_a_g, v_ln_a_b, v_ln_v_g, v_ln_v_b, v_w_s, v_b_s, v_w_mk, v_w_mv, v_w_out, v_ln1_g, v_ln1_b, v_w_up, v_conv_f_w, v_conv_f_b, v_w_down, v_ln2_g, v_ln2_b):
    given = dict(locals())
    p = {nm: given[nm] for nm in WEIGHTS}
    mom_m = {nm: given["m_" + nm] for nm in WEIGHTS}
    mom_v = {nm: given["v_" + nm] for nm in WEIGHTS}
    seq = x.shape[1]
    ts_m, ts_f, ts_p = min(TS_MIXER, seq), min(TS_FFN, seq), min(TS_PROJ, seq)
    cx, cy, cc = _place()
    me = 4 * cx + 2 * cy + cc

    blobs = [_blob(p, l) for l in range(DEPTH)]
    blobs_bf = [b.astype(BF16) for b in blobs]
    conv_a_tile = jnp.pad(conv_a_w, ((0, 0), (0, HALO - CONV_K), (0, 128 - CONV_A_SHARD)))
    conv_f_tile = jnp.pad(conv_f_w, ((0, 0), (0, 8 - FFN_CONV_K), (0, 384 - FF_GROUP)))
    rows = dict(BLOCK_ROWS)
    mixer_names = ("w_in", "w_out", "w_mk", "w_mv")
    pieces = [(0, _row_off(nm), rows[nm]) for nm in mixer_names] + [(1, None, 0), (2, None, 0)]
    first_arrs = [blobs_bf[0], conv_a_tile, conv_f_tile]
    in_flight = _gather_start(first_arrs, pieces)
    m_blobs = [_blob(mom_m, l) for l in range(DEPTH)]
    v_blobs = [_blob(mom_v, l) for l in range(DEPTH)]
    first, first_arrs = _gather_wait(in_flight, first_arrs, pieces, m_blobs + v_blobs + [blobs_bf[1]])
    blobs_bf[0] = first_arrs[0]
    conv_a_all, conv_f_all = first[len(mixer_names)], first[len(mixer_names) + 1]
    conv_a = [conv_a_all[:, l, :, :CONV_A_SHARD].transpose(1, 0, 2).reshape(HALO, CONV_W) for l in range(DEPTH)]
    conv_f = [conv_f_all[:, l, :FFN_CONV_K, :FF_GROUP_PAD].transpose(1, 0, 2).reshape(FFN_CONV_K, D_FF_PAD)
              for l in range(DEPTH)]
    memq = mem[0].reshape(N_MEM, D_MODEL // MEM_FOLD, MEM_FOLD).transpose(2, 0, 1).astype(BF16)

    def matrix(gathered, nm, base):
        lo = _row_off(nm) - base
        return gathered[:, lo:lo + rows[nm]].reshape(-1, BLOB_LANES)

    ffn_base = _row_off("w_up")
    ops0 = _mixer_operands({nm: first[a].reshape(-1, BLOB_LANES) for a, nm in enumerate(mixer_names)}, conv_a[0], p, 0,
                           memq)
    mixer_saved = ("hb", "z1", "x1", "a1", "cat", "p", "gs", "rv")
    mixed0, (ffn0,) = _mixer_fwd_call(
        x[0], ops0, ts_m, _gather_exchange([blobs_bf[0]], [(0, ffn_base, rows["w_up"] + rows["w_down"])]))
    ops0.update(_ffn_operands(matrix(ffn0, "w_up", ffn_base), matrix(ffn0, "w_down", ffn_base), conv_f[0], p, 0))
    saved = [dict(zip(mixer_saved, mixed0), x=x[0])]
    (ug, uv, sl, dsl, z2, x2), (all1,) = _ffn_fwd_call(saved[0]["x1"], ops0, ts_f,
                                                       _gather_exchange([blobs_bf[1]], [(0, 0, LAYER_ROWS)]))
    saved[0].update(ug=ug, uv=uv, sl=sl, dsl=dsl, z2=z2)
    ops1 = _mixer_operands({nm: matrix(all1, nm, 0) for nm in mixer_names}, conv_a[1], p, 1, memq)
    ops1.update(_ffn_operands(matrix(all1, "w_up", 0), matrix(all1, "w_down", 0), conv_f[1], p, 1))
    mixed1, _ = _mixer_fwd_call(x2, ops1, ts_m)
    saved.append(dict(zip(mixer_saved, mixed1), x=x2))
    (ug, uv, sl, dsl, z2), _ = _ffn_fwd_call(saved[1]["x1"], ops1, ts_f, want_x2=False)
    saved[1].update(ug=ug, uv=uv, sl=sl, dsl=dsl, z2=z2)
    ops = [ops0, ops1]

    hm = _head_table()
    core_id = cc.reshape(1).astype(jnp.int32)
    slabs = [lax.empty((N_DEV, LAYER_ROWS, BLOB_LANES), F32) for _ in range(DEPTH)]
    accs, gathered_accs = [None] * DEPTH, [None] * DEPTH
    acc_names = [nm for nm, _ in GATHERED_ACCS]
    whole = [(a, None, 0) for a in range(len(acc_names))]

    def acc_list(l):
        return [accs[l][nm] for nm in acc_names]

    from_sibling, from_chips = [None] * DEPTH, [None] * DEPTH
    dx = loss_target[0]
    loss = None
    for l in reversed(range(DEPTH)):
        s, w = saved[l], ops[l]
        last = l == DEPTH - 1
        ride = None if last else _swap_exchange(slabs[l + 1], _swap_core_copies, 4)
        (sl, dug, duv, dz2, dcf, dv2, loss_acc), got = _ffn_bwd_call(
            dx, s["z2"], s["ug"], s["uv"], s["sl"], s["dsl"], w, ts_f, last, slabs[l], _row_off("w_down"), ride)
        if last:
            loss = loss_acc[0, 0]
        else:
            from_sibling[l + 1] = got[0]
            chip_sum = _pair_add_call(slabs[l + 1], from_sibling[l + 1], core_id)
        off_up = _row_off("w_up")
        ride = None if last else _gather_exchange(acc_list(l + 1), whole)
        (sl, dxa), got = _proj_bwd_call(dug, w["wgt"], s["x1"], dz2, ALPHA, ts_p, "up_gate_bwd", sl, off_up, 0, 4, ride)
        if not last:
            gathered_accs[l + 1] = dict(zip(acc_names, got))
        ride = None if last else _swap_exchange(chip_sum, _swap_chip_copies, 3)
        (sl, dx1), got = _proj_bwd_call(duv, w["wvt"], s["x1"], dxa, 1.0, ts_p, "up_val_bwd", sl, off_up, 4, 4, ride)
        if not last:
            from_chips[l + 1] = got[0]
        (sl, dx, dkt, dvm, dwcat, dmsum, dva, dcw, dv1) = _mixer_bwd_call(
            dx1, s["z1"], s["hb"], s["a1"], s["cat"], s["p"], s["gs"], s["rv"], s["x"], w, ts_m, sl, _row_off("w_out"),
            _row_off("w_in"))
        dkh = jnp.einsum("hd,dhm->md", hm, dkt.reshape(XATTN_W, XATTN_HEADS, N_MEM)) * (1.0 / math.sqrt(HEAD_DIM))
        dvh = jnp.einsum("hd,hmd->md", hm, dvm.reshape(XATTN_HEADS, N_MEM, XATTN_W))
        slabs[l] = _mem_proj_bwd_call(memq, dkh, dvh, sl, _row_off("w_mk"), _row_off("w_mv"))
        accs[l] = dict(dva=dva, dv1=dv1, dv2=dv2, dcf=dcf, dcw=dcw, dwcat=dwcat, dmsum=dmsum)
    grad_x = dx[None]
    *got, from_sibling[0] = _all_gather_call(acc_list(0), whole, "gather_small_grads",
                                            swap=(slabs[0], _swap_core_copies, 4))
    gathered_accs[0] = dict(zip(acc_names, got))
    chip_sum = _pair_add_call(slabs[0], from_sibling[0], core_id)
    in_flight = _swap_chip_start(chip_sum, from_chips[1])
    from_chips[1] = in_flight[4]
    me_id, chip_id = me.reshape(1).astype(jnp.int32), (2 * cx + cy).reshape(1).astype(jnp.int32)

    def final(l):
        return _chip_add_adamw_call(slabs[l], from_sibling[l], from_chips[l], me_id, chip_id, blobs[l], m_blobs[l],
                                    v_blobs[l])

    per_layer = [None, final(1)]
    rep, dcf_sum, dcw_sum = _replicated_update_call(gathered_accs, p, mom_m, mom_v)
    from_chips[0] = _swap_chip_wait(*in_flight[:4], per_layer[1][0], dcf_sum[0])
    per_layer[0] = final(0)
    outs = [_unblob([per_layer[l][k] for l in range(DEPTH)]) for k in range(4)]
    for k in range(4):
        outs[k].update(rep[k])
    dcf_sum, dcw_sum = jnp.stack(dcf_sum), jnp.stack(dcw_sum)
    zero = jnp.zeros((), jnp.int32)
    g_conv_a_w = lax.dynamic_slice(dcw_sum, (zero, zero, CONV_A_SHARD * me), (DEPTH, CONV_K, CONV_A_SHARD))
    g_conv_f_w = lax.dynamic_slice(dcf_sum, (zero, zero, FF_GROUP_PAD * me), (DEPTH, FFN_CONV_K, FF_GROUP))
    g_conv_f_b = _ff_deinterleave(dcf_sum[:, FFN_CONV_K])
    conv_grads = dict(conv_a_w=g_conv_a_w, conv_f_w=g_conv_f_w, conv_f_b=g_conv_f_b)
    conv_names = tuple(conv_grads)
    upd = _adamw_whole_call([(p[nm], conv_grads[nm], mom_m[nm], mom_v[nm]) for nm in conv_names], "adamw_conv")
    for nm, (d, mn, vn) in zip(conv_names, upd):
        outs[0][nm], outs[1][nm], outs[2][nm], outs[3][nm] = conv_grads[nm], d, mn, vn

    loss = lax.psum(loss, ("x", "y", "c"))
    return (loss, grad_x, *[outs[0][nm] for nm in WEIGHTS], *[outs[1][nm] for nm in WEIGHTS],
            *[outs[2][nm] for nm in WEIGHTS], *[outs[3][nm] for nm in WEIGHTS])
```

```python
import math

import jax
import jax.numpy as jnp
from jax import lax
from jax.experimental import pallas as pl
from jax.experimental.pallas import tpu as pltpu

F32 = jnp.float32
BF16 = jnp.bfloat16

DEPTH = 2
D_MODEL = 1024
CONV_W = 384
GMLP_W = 384
XATTN_W = 256
HEAD_DIM = 64
GMLP_HEADS = 6
XATTN_HEADS = 4
IN_W = 1792
CONV_K = 31
CHUNK = 128
N_MEM = 256
D_FF = 2752
D_FF_PAD = 2816
FFN_CONV_K = 3
ALPHA = (2.0 * DEPTH) ** 0.25
LN_EPS = 1e-5
N_DEV = 8

ADAM_LR = 0.001
ADAM_B1 = 0.9
ADAM_B2 = 0.999
ADAM_EPS = 1e-08
ADAM_WD = 0.01
ADAM_STEP = 10

HALO = 32
CONV_ROWS = 32
V7X_VMEM_BYTES = 64 * 1024 * 1024
VMEM_LIMIT = V7X_VMEM_BYTES - 8 * 1024 * 1024
BLOB_LANES = 1024

MESH = pl.DeviceIdType.MESH

WEIGHTS = ("w_in", "conv_a_w", "conv_a_b", "ln_a_g", "ln_a_b", "ln_v_g", "ln_v_b", "w_s", "b_s", "w_mk", "w_mv",
           "w_out", "ln1_g", "ln1_b", "w_up", "conv_f_w", "conv_f_b", "w_down", "ln2_g", "ln2_b")


def _params(**kw):
    return pltpu.CompilerParams(vmem_limit_bytes=VMEM_LIMIT, **kw)


def _const(shape):
    nd = len(shape)
    return pl.BlockSpec(shape, lambda i: (0,) * nd, pipeline_mode=pl.Buffered(1))


def _acc(shape):
    nd = len(shape)
    return pl.BlockSpec(shape, lambda i: (0,) * nd)


class _Exchange:
    def __init__(self, arrays, out_shapes, n_copies, build, n_local=1):
        self.arrays, self.out_shapes, self.n_copies, self.build = list(arrays), list(out_shapes), n_copies, build
        self.n_local = n_local


def _carry(core, n_in, n_out, exch, n_steps):
    if exch is None:
        return core
    nx_in, nx_out = len(exch.arrays), len(exch.out_shapes)

    def body(*refs):
        o0 = n_in + nx_in
        s0 = o0 + n_out + nx_out
        x_in, x_out, sems = refs[n_in:o0], refs[o0 + n_out:s0], refs[-3:]
        i = pl.program_id(0)

        @pl.when(i == 0)
        def _():
            remote, local = exch.build(x_in, x_out, *sems)
            for cp in remote + local:
                cp.start()

        core(*refs[:n_in], *refs[o0:o0 + n_out], *refs[s0:-3])

        @pl.when(i == n_steps - 1)
        def _():
            remote, local = exch.build(x_in, x_out, *sems)
            for cp in remote + local:
                cp.wait()

    return body


def _grid_call(core, name, n_steps, in_specs, out_specs, out_shape, scratch_shapes, args, aliases=None, exch=None):
    hbm = pl.BlockSpec(memory_space=pl.ANY)
    n_in, n_out = len(in_specs), len(out_specs)
    in_specs, out_specs, out_shape, scratch_shapes, args = (list(in_specs), list(out_specs), list(out_shape),
                                                            list(scratch_shapes), list(args))
    if exch is not None:
        in_specs += [hbm] * len(exch.arrays)
        out_specs += [hbm] * len(exch.out_shapes)
        out_shape += exch.out_shapes
        scratch_shapes += [pltpu.SemaphoreType.DMA((exch.n_copies,)), pltpu.SemaphoreType.DMA((exch.n_copies,)),
                           pltpu.SemaphoreType.DMA((exch.n_local,))]
        args += exch.arrays
    out = pl.pallas_call(
        _carry(core, n_in, n_out, exch, n_steps), name=name, grid=(n_steps,), in_specs=in_specs, out_specs=out_specs,
        out_shape=out_shape, scratch_shapes=scratch_shapes, input_output_aliases=aliases or {},
        compiler_params=_params(dimension_semantics=("arbitrary",)))(*args)
    return list(out[:n_out]), list(out[n_out:])


def _sigmoid(x):
    return 1.0 / (1.0 + jnp.exp(-x))


_GELU_C = math.sqrt(2.0 / math.pi)


def _gelu(x):
    x2 = x * x
    t = jnp.tanh(_GELU_C * (x + 0.044715 * x * x2))
    g = 0.5 * x * (1.0 + t)
    dg = 0.5 * (1.0 + t) + 0.5 * x * (1.0 - t * t) * (_GELU_C * (1.0 + 3.0 * 0.044715 * x2))
    return g, dg


def _ln_stats(z):
    mu = jnp.mean(z, axis=-1, keepdims=True)
    zc = z - mu
    var = jnp.mean(zc * zc, axis=-1, keepdims=True)
    r = lax.rsqrt(var + LN_EPS)
    return zc * r, r


def _ln_bwd(dy, xh, r, g):
    dxh = dy * g
    m1 = jnp.mean(dxh, axis=-1, keepdims=True)
    m2 = jnp.mean(dxh * xh, axis=-1, keepdims=True)
    return r * (dxh - m1 - xh * m2)


def _rowsum(x):
    return jnp.sum(x, axis=0, keepdims=True)


def _dot(a, b):
    return jnp.dot(a, b, preferred_element_type=F32)


def _dot_tn(a, b):
    return lax.dot_general(a, b, (((0,), (0,)), ((), ())), preferred_element_type=F32)


def _dot_nt(a, b):
    return lax.dot_general(a, b, (((1,), (1,)), ((), ())), preferred_element_type=F32)


def _shift_copies(buf, sh, rows):
    for b in range(1, 8):
        sh[b - 1, 0:rows, :] = buf[b:b + rows, :]


def _window(buf, sh, start):
    b = start % 8
    a = start - b
    return buf[a:a + CONV_ROWS, :] if b == 0 else sh[b - 1, a:a + CONV_ROWS, :]


def _conv31_fwd(buf, sh, w_ref, bias, out, ts):
    for r0 in range(0, ts, CONV_ROWS):
        acc = jnp.broadcast_to(bias, (CONV_ROWS, CONV_W))
        for k in range(CONV_K):
            acc = acc + w_ref[k:k + 1, :] * _window(buf, sh, r0 + HALO - (CONV_K - 1) + k)
        out[r0:r0 + CONV_ROWS, :] = acc


def _conv31_dx(dbuf, dsh, w_ref, out, ts):
    for r0 in range(0, ts, CONV_ROWS):
        acc = jnp.zeros((CONV_ROWS, CONV_W), F32)
        for k in range(CONV_K):
            acc = acc + w_ref[k:k + 1, :] * _window(dbuf, dsh, r0 + (CONV_K - 1) - k)
        out[r0:r0 + CONV_ROWS, :] = acc


def _conv31_dw(buf, sh, dbuf, dw_ref, ts):
    for k in range(CONV_K):
        part = jnp.zeros((8, CONV_W), F32)
        for r0 in range(0, ts, CONV_ROWS):
            m = dbuf[r0:r0 + CONV_ROWS, :] * _window(buf, sh, r0 + HALO - (CONV_K - 1) + k)
            for q in range(0, CONV_ROWS, 8):
                part = part + m[q:q + 8, :]
        dw_ref[k:k + 1, :] += _rowsum(part)


def _head_mask(width, h):
    lane = lax.broadcasted_iota(jnp.int32, (CHUNK, width), 1)
    return (lane >= h * HEAD_DIM) & (lane < (h + 1) * HEAD_DIM)


def _stack_heads(vn_c):
    return jnp.concatenate([jnp.where(_head_mask(GMLP_W, h), vn_c, 0.0) for h in range(GMLP_HEADS)], axis=0)


def _group_a_fwd(hf, buf, sh, a1_ref, cw_ref, va_ref, ts, conv=True):
    ha = hf[:, 0:CONV_W]
    sg = _sigmoid(hf[:, CONV_W:2 * CONV_W])
    buf[HALO:HALO + ts, :] = ha * sg
    _shift_copies(buf, sh, ts + HALO - 8)
    if conv:
        _conv31_fwd(buf, sh, cw_ref, va_ref[0:1, :], a1_ref, ts)
    a2h, ra = _ln_stats(a1_ref[...])
    a2 = a2h * va_ref[1:2, :] + va_ref[2:3, :]
    sa = _sigmoid(a2)
    return dict(ha=ha, sg=sg, a2h=a2h, ra=ra, a2=a2, sa=sa, a=a2 * sa)


GROUP_LANES = ((0, 2 * CONV_W), (2 * CONV_W, 2 * CONV_W + 2 * GMLP_W), (2 * CONV_W + 2 * GMLP_W, IN_W))


def _group_b_fwd(hf, va_ref, wcat_ref, bfull_ref, ts):
    hu = hf[:, 0:GMLP_W]
    hv = hf[:, GMLP_W:2 * GMLP_W]
    u, du = _gelu(hu)
    v, dv = _gelu(hv)
    vhat, rv = _ln_stats(v)
    vn = vhat * va_ref[3:4, :] + va_ref[4:5, :]
    stacks, mixed = [], []
    for c0 in range(0, ts, CHUNK):
        st = _stack_heads(vn[c0:c0 + CHUNK, :]).astype(BF16)
        stacks.append(st)
        mixed.append(_dot(wcat_ref[...], st) + bfull_ref[...])
    mixed = jnp.concatenate(mixed, axis=0) if len(mixed) > 1 else mixed[0]
    return dict(u=u, du=du, dv=dv, vhat=vhat, rv=rv, stacks=stacks, mixed=mixed, g=u * mixed)


GROUP_B_SAVED = ("u", "du", "dv", "vhat", "mixed")


def _group_c_fwd(qb, kt_ref, vm_ref):
    s_all = _dot(qb, kt_ref[...])
    ps = []
    for g in range(XATTN_HEADS):
        s = s_all[:, g * N_MEM:(g + 1) * N_MEM]
        e = jnp.exp(s - jnp.max(s, axis=-1, keepdims=True))
        ps.append(e / jnp.sum(e, axis=-1, keepdims=True))
    p_all = jnp.concatenate(ps, axis=1)
    pb = p_all.astype(BF16)
    return dict(qb=qb, p=p_all, pb=pb, c=_dot(pb, vm_ref[...]))


def _mixer_fwd_call(x, w, ts, exch=None):
    seq = x.shape[0]
    n = seq // ts

    def body(x_ref, win_ref, cw_ref, va_ref, wcat_ref, bfull_ref, kt_ref, vm_ref, wout_ref, v1_ref,
             hb_ref, z1_ref, x1_ref, a1buf, cat_ref, p_ref, gs_ref, rv_ref, buf, sh):
        i = pl.program_id(0)

        @pl.when(i == 0)
        def _():
            buf[0:HALO, :] = jnp.zeros((HALO, CONV_W), F32)

        @pl.when(i > 0)
        def _():
            buf[0:HALO, :] = buf[ts:ts + HALO, :]

        xv = x_ref[...]
        hb = _dot(xv.astype(BF16), win_ref[...]).astype(BF16)
        hb_ref[...] = hb
        hf = hb.astype(F32)
        (a_lo, a_hi), (b_lo, b_hi), (c_lo, c_hi) = GROUP_LANES
        ga = _group_a_fwd(hf[:, a_lo:a_hi], buf, sh, a1buf, cw_ref, va_ref, ts)
        gb = _group_b_fwd(hf[:, b_lo:b_hi], va_ref, wcat_ref, bfull_ref, ts)
        gc = _group_c_fwd(hb[:, c_lo:c_hi], kt_ref, vm_ref)
        cat = jnp.concatenate([ga["a"], gb["g"], gc["c"]], axis=1).astype(BF16)
        cat_ref[...] = cat
        p_ref[...] = gc["pb"]
        gs_ref[...] = jnp.concatenate([gb[k] for k in GROUP_B_SAVED], axis=1).astype(BF16)
        rv_ref[...] = jnp.broadcast_to(gb["rv"], (ts, 128))
        z1 = ALPHA * xv + _dot(cat, wout_ref[...])
        z1_ref[...] = z1
        xh, _ = _ln_stats(z1)
        x1_ref[...] = xh * v1_ref[0:1, :] + v1_ref[1:2, :]

    row = lambda width: pl.BlockSpec((ts, width), lambda i: (i, 0))
    return _grid_call(
        body, "mixer_fwd", n,
        in_specs=[row(D_MODEL), _const((D_MODEL, IN_W)), _const((HALO, CONV_W)), _const((8, CONV_W)),
                  _const((CHUNK, GMLP_HEADS * CHUNK)), _const((CHUNK, GMLP_W)), _const((XATTN_W, XATTN_HEADS * N_MEM)),
                  _const((XATTN_HEADS * N_MEM, XATTN_W)), _const((D_MODEL, D_MODEL)), _const((8, D_MODEL))],
        out_specs=[row(IN_W), row(D_MODEL), row(D_MODEL), row(CONV_W), row(D_MODEL), row(XATTN_HEADS * N_MEM),
                   row(len(GROUP_B_SAVED) * GMLP_W), row(128)],
        out_shape=[jax.ShapeDtypeStruct((seq, IN_W), BF16), jax.ShapeDtypeStruct((seq, D_MODEL), F32),
                   jax.ShapeDtypeStruct((seq, D_MODEL), F32), jax.ShapeDtypeStruct((seq, CONV_W), F32),
                   jax.ShapeDtypeStruct((seq, D_MODEL), BF16), jax.ShapeDtypeStruct((seq, XATTN_HEADS * N_MEM), BF16),
                   jax.ShapeDtypeStruct((seq, len(GROUP_B_SAVED) * GMLP_W), BF16),
                   jax.ShapeDtypeStruct((seq, 128), F32)],
        scratch_shapes=[pltpu.VMEM((ts + HALO, CONV_W), F32), pltpu.VMEM((7, ts + HALO, CONV_W), F32)],
        args=(x, w["win"], w["cw"], w["va"], w["wcat"], w["bfull"], w["kt"], w["vm"], w["wout"], w["v1"]), exch=exch)


def _store_blocks(acc, slabs_ref, sems, row_off, rows, first_block, n_blocks):
    copies = [pltpu.make_async_copy(acc.at[pl.ds(q * rows, rows)], slabs_ref.at[first_block + q, pl.ds(row_off, rows)],
                                    sems.at[q]) for q in range(n_blocks)]
    for cp in copies:
        cp.start()
    for cp in copies:
        cp.wait()


def _mixer_bwd_call(dx1, z1, hb, a1, cat, p, gs, rv, x, w, ts, slabs, off_out, off_in):
    seq = dx1.shape[0]
    n = seq // ts
    halo_blocks = ts // HALO

    def body(slabs_in, dx1_ref, z1_ref, hb_ref, hprev_ref, a1_ref, cat_ref, p_ref, gs_ref, rv_ref, x_ref, cw_ref, va_ref,
             wcatt_ref, ktt_ref, vmt_ref, woutt_ref, wint_ref, v1_ref,
             slabs_ref, dx_ref, dkt_ref, dvm_ref, dwcat_ref, dmsum_ref, dva_ref, dcw_ref, dv1_ref,
             buf, dbuf, da0buf, dwout_ref, dwin_ref, sems, sh, dsh):
        i = pl.program_id(0)

        @pl.when(i == 0)
        def _():
            for ref in (dwout_ref, dwin_ref, dkt_ref, dvm_ref, dwcat_ref, dmsum_ref, dva_ref, dcw_ref, dv1_ref):
                ref[...] = jnp.zeros(ref.shape, F32)
            dbuf[ts:ts + HALO, :] = jnp.zeros((HALO, CONV_W), F32)

        @pl.when(i > 0)
        def _():
            dbuf[ts:ts + HALO, :] = dbuf[0:HALO, :]

        dx1v = dx1_ref[...]
        xh1, r1 = _ln_stats(z1_ref[...])
        dv1_ref[0:1, :] += _rowsum(dx1v * xh1)
        dv1_ref[1:2, :] += _rowsum(dx1v)
        dz1 = _ln_bwd(dx1v, xh1, r1, v1_ref[0:1, :])
        dmix = dz1.astype(BF16)

        hf = hb_ref[:, 0:2 * CONV_W].astype(F32)
        hp = hprev_ref[...].astype(F32)
        a0p = hp[:, 0:CONV_W] * _sigmoid(hp[:, CONV_W:2 * CONV_W])
        buf[0:HALO, :] = jnp.where(i == n - 1, 0.0, a0p)
        ga = _group_a_fwd(hf, buf, sh, a1_ref, cw_ref, va_ref, ts, conv=False)
        gb = {k: gs_ref[:, j * GMLP_W:(j + 1) * GMLP_W].astype(F32) for j, k in enumerate(GROUP_B_SAVED)}
        vn = gb["vhat"] * va_ref[3:4, :] + va_ref[4:5, :]
        pb = p_ref[...]
        gc = dict(qb=hb_ref[:, IN_W - XATTN_W:IN_W], pb=pb, p=pb.astype(F32))

        dwout_ref[...] += _dot_tn(cat_ref[...], dmix)
        dcat = _dot(dmix, woutt_ref[...])
        da = dcat[:, 0:CONV_W]
        dg = dcat[:, CONV_W:CONV_W + GMLP_W]
        dc = dcat[:, CONV_W + GMLP_W:D_MODEL].astype(BF16)

        dp = _dot(dc, vmt_ref[...])
        dvm_ref[...] += _dot_tn(gc["pb"], dc)
        dss = []
        for g in range(XATTN_HEADS):
            sl = slice(g * N_MEM, (g + 1) * N_MEM)
            pg = gc["p"][:, sl]
            dpg = dp[:, sl]
            dss.append(pg * (dpg - jnp.sum(dpg * pg, axis=-1, keepdims=True)))
        ds = jnp.concatenate(dss, axis=1).astype(BF16)
        dq = _dot(ds, ktt_ref[...])
        dkt_ref[...] += _dot_tn(gc["qb"], ds)

        dmixed = dg * gb["u"]
        dhu = dg * gb["mixed"] * gb["du"]
        dvns = []
        for j, c0 in enumerate(range(0, ts, CHUNK)):
            dm = dmixed[c0:c0 + CHUNK, :]
            dmb = dm.astype(BF16)
            dmsum_ref[...] += dm
            dwcat_ref[...] += _dot_nt(dmb, _stack_heads(vn[c0:c0 + CHUNK, :]).astype(BF16))
            dst = _dot(wcatt_ref[...], dmb)
            dvn_c = jnp.zeros((CHUNK, GMLP_W), F32)
            for h in range(GMLP_HEADS):
                dvn_c = dvn_c + jnp.where(_head_mask(GMLP_W, h), dst[h * CHUNK:(h + 1) * CHUNK, :], 0.0)
            dvns.append(dvn_c)
        dvn = jnp.concatenate(dvns, axis=0) if len(dvns) > 1 else dvns[0]
        dva_ref[3:4, :] += _rowsum(dvn * gb["vhat"])
        dva_ref[4:5, :] += _rowsum(dvn)
        dhv = _ln_bwd(dvn, gb["vhat"], rv_ref[:, 0:1], va_ref[3:4, :]) * gb["dv"]

        a2, sa = ga["a2"], ga["sa"]
        da2 = da * (sa * (1.0 + a2 * (1.0 - sa)))
        dva_ref[1:2, :] += _rowsum(da2 * ga["a2h"])
        dva_ref[2:3, :] += _rowsum(da2)
        da1 = _ln_bwd(da2, ga["a2h"], ga["ra"], va_ref[1:2, :])
        dva_ref[0:1, :] += _rowsum(da1)
        dbuf[0:ts, :] = da1
        _shift_copies(dbuf, dsh, ts + HALO - 8)
        _conv31_dw(buf, sh, dbuf, dcw_ref, ts)
        _conv31_dx(dbuf, dsh, cw_ref, da0buf, ts)
        da0 = da0buf[...]
        sg = ga["sg"]
        dha = da0 * sg
        dhg = da0 * ga["ha"] * sg * (1.0 - sg)

        dh = jnp.concatenate([dha, dhg, dhu, dhv, dq], axis=1).astype(BF16)
        dx_ref[...] = _dot(dh, wint_ref[...]) + ALPHA * dz1
        dwin_ref[...] += _dot_tn(dh, x_ref[...].astype(BF16))

        @pl.when(i == n - 1)
        def _():
            _store_blocks(dwout_ref, slabs_ref, sems, off_out, D_MODEL // N_DEV, 0, N_DEV)
            _store_blocks(dwin_ref, slabs_ref, sems, off_in, IN_W // N_DEV, 0, N_DEV)

    rev = lambda width: pl.BlockSpec((ts, width), lambda i: (n - 1 - i, 0))
    prev = pl.BlockSpec((HALO, 2 * CONV_W), lambda i: (jnp.maximum((n - 1 - i) * halo_blocks - 1, 0), 0))
    hbm = pl.BlockSpec(memory_space=pl.ANY)
    hc = GMLP_HEADS * CHUNK
    am = XATTN_HEADS * N_MEM
    return pl.pallas_call(
        body, name="mixer_bwd", grid=(n,),
        in_specs=[hbm, rev(D_MODEL), rev(D_MODEL), rev(IN_W), prev, rev(CONV_W), rev(D_MODEL), rev(am),
                  rev(len(GROUP_B_SAVED) * GMLP_W), rev(128), rev(D_MODEL), _const((HALO, CONV_W)),
                  _const((8, CONV_W)), _const((hc, CHUNK)), _const((am, XATTN_W)), _const((XATTN_W, am)),
                  _const((D_MODEL, D_MODEL)), _const((IN_W, D_MODEL)), _const((8, D_MODEL))],
        out_specs=[hbm, rev(D_MODEL), _acc((XATTN_W, am)), _acc((am, XATTN_W)),
                   _acc((CHUNK, hc)), _acc((CHUNK, GMLP_W)), _acc((8, CONV_W)), _acc((HALO, CONV_W)),
                   _acc((8, D_MODEL))],
        out_shape=[jax.ShapeDtypeStruct(slabs.shape, F32), jax.ShapeDtypeStruct((seq, D_MODEL), F32),
                   jax.ShapeDtypeStruct((XATTN_W, am), F32),
                   jax.ShapeDtypeStruct((am, XATTN_W), F32), jax.ShapeDtypeStruct((CHUNK, hc), F32),
                   jax.ShapeDtypeStruct((CHUNK, GMLP_W), F32), jax.ShapeDtypeStruct((8, CONV_W), F32),
                   jax.ShapeDtypeStruct((HALO, CONV_W), F32), jax.ShapeDtypeStruct((8, D_MODEL), F32)],
        scratch_shapes=[pltpu.VMEM((ts + HALO, CONV_W), F32),
                        pltpu.VMEM((ts + HALO, CONV_W), F32), pltpu.VMEM((ts, CONV_W), F32),
                        pltpu.VMEM((D_MODEL, D_MODEL), F32), pltpu.VMEM((IN_W, D_MODEL), F32),
                        pltpu.SemaphoreType.DMA((N_DEV,)),
                        pltpu.VMEM((7, ts + HALO, CONV_W), F32), pltpu.VMEM((7, ts + HALO, CONV_W), F32)],
        input_output_aliases={0: 0},
        compiler_params=_params(dimension_semantics=("arbitrary",)),
    )(slabs, dx1, z1, hb, hb, a1, cat, p, gs, rv, x, w["cw"], w["va"], w["wcatt"], w["ktt"], w["vmt"], w["woutt"],
      w["wint"], w["v1"])


FFN_HALO = 8
FF_GROUP = D_FF // N_DEV
FF_GROUP_PAD = D_FF_PAD // N_DEV


def _ffn_taps(ubuf, ts, lo, hi):
    return tuple(ubuf[FFN_HALO - (FFN_CONV_K - 1) + k:FFN_HALO - (FFN_CONV_K - 1) + k + ts, lo:hi]
                 for k in range(FFN_CONV_K))


def _ffn_gate(ubuf, cf_ref, ts, lo, hi):
    taps = _ffn_taps(ubuf, ts, lo, hi)
    g = cf_ref[3:4, lo:hi] + cf_ref[2:3, lo:hi] * taps[2]
    g = g + cf_ref[1:2, lo:hi] * taps[1]
    return g + cf_ref[0:1, lo:hi] * taps[0]


FFN_CHUNK = 256
FFN_CHUNKS = tuple((lo, lo + FFN_CHUNK) for lo in range(0, D_FF_PAD, FFN_CHUNK))


def _ffn_fwd_call(x1, w, ts, exch=None, want_x2=True):
    seq = x1.shape[0]
    n = seq // ts
    n_wide = 4

    def body(x1_ref, wg_ref, wv_ref, cf_ref, wdown_ref, v2_ref, ug_ref, uv_ref, sl_ref, dsl_ref, z2_ref, *rest):
        x2_ref = rest[0] if want_x2 else None
        ubuf, act_buf = rest[-2:]
        i = pl.program_id(0)

        @pl.when(i == 0)
        def _():
            ubuf[0:FFN_HALO, :] = jnp.zeros((FFN_HALO, D_FF_PAD), F32)

        @pl.when(i > 0)
        def _():
            ubuf[0:FFN_HALO, :] = ubuf[ts:ts + FFN_HALO, :]

        xv = x1_ref[...]
        xb = xv.astype(BF16)
        for lo, hi in FFN_CHUNKS:
            ug = _dot(xb, wg_ref[:, lo:hi]).astype(BF16)
            uv = _dot(xb, wv_ref[:, lo:hi]).astype(BF16)
            ug_ref[:, lo:hi] = ug
            uv_ref[:, lo:hi] = uv
            ubuf[FFN_HALO:FFN_HALO + ts, lo:hi] = ug.astype(F32)
            gate = _ffn_gate(ubuf, cf_ref, ts, lo, hi)
            sg = _sigmoid(gate)
            sl = gate * sg
            sl_ref[:, lo:hi] = sl.astype(BF16)
            dsl_ref[:, lo:hi] = (sg * (1.0 + gate * (1.0 - sg))).astype(BF16)
            act_buf[:, lo:hi] = (sl * uv.astype(F32)).astype(BF16)
        y = ALPHA * xv + _dot(act_buf[...], wdown_ref[...])
        z2_ref[...] = y
        if want_x2:
            xh, _ = _ln_stats(y)
            x2_ref[...] = xh * v2_ref[0:1, :] + v2_ref[1:2, :]

    row = lambda width: pl.BlockSpec((ts, width), lambda i: (i, 0))
    n_narrow = 2 if want_x2 else 1
    return _grid_call(
        body, "ffn_fwd" if want_x2 else "ffn_fwd_last", n,
        in_specs=[row(D_MODEL), _const((D_MODEL, D_FF_PAD)), _const((D_MODEL, D_FF_PAD)), _const((8, D_FF_PAD)),
                  _const((D_FF_PAD, D_MODEL)), _const((8, D_MODEL))],
        out_specs=[row(D_FF_PAD)] * n_wide + [row(D_MODEL)] * n_narrow,
        out_shape=[jax.ShapeDtypeStruct((seq, D_FF_PAD), BF16)] * n_wide
                  + [jax.ShapeDtypeStruct((seq, D_MODEL), F32)] * n_narrow,
        scratch_shapes=[pltpu.VMEM((ts + FFN_HALO, D_FF_PAD), F32), pltpu.VMEM((ts, D_FF_PAD), BF16)],
        args=(x1, w["wg"], w["wv"], w["cf"], w["wdown"], w["v2"]), exch=exch)


def _ffn_bwd_call(dx2_or_target, z2, ug, uv, sl, dsl, w, ts, last, slabs, row_off, exch=None):
    seq = z2.shape[0]
    n = seq // ts
    halo_blocks = ts // 16

    def body(slabs_in, dx2_ref, z2_ref, ug_ref, uv_ref, sl_ref, dsl_ref, uprev_ref, cf_ref, wdownt_ref, v2_ref,
             slabs_ref, dug_ref, duv_ref, dz2_ref, dcf_ref, dv2_ref, loss_ref,
             ubuf, dgbuf, dwacc, sems):
        i = pl.program_id(0)

        @pl.when(i == 0)
        def _():
            dwacc[...] = jnp.zeros(dwacc.shape, F32)
            dcf_ref[...] = jnp.zeros(dcf_ref.shape, F32)
            dv2_ref[...] = jnp.zeros(dv2_ref.shape, F32)
            loss_ref[...] = jnp.zeros(loss_ref.shape, F32)
            dgbuf[ts:ts + FFN_HALO, :] = jnp.zeros((FFN_HALO, D_FF_PAD), F32)

        @pl.when(i > 0)
        def _():
            dgbuf[ts:ts + FFN_HALO, :] = dgbuf[0:FFN_HALO, :]

        xh2, r2 = _ln_stats(z2_ref[...])
        if last:
            diff = xh2 * v2_ref[0:1, :] + v2_ref[1:2, :] - dx2_ref[...]
            loss_ref[...] += jnp.sum(diff * diff) * (0.5 / D_MODEL)
            dx2v = diff * (1.0 / D_MODEL)
        else:
            dx2v = dx2_ref[...]
        dv2_ref[0:1, :] += _rowsum(dx2v * xh2)
        dv2_ref[1:2, :] += _rowsum(dx2v)
        dz2 = _ln_bwd(dx2v, xh2, r2, v2_ref[0:1, :])
        dz2_ref[...] = dz2
        dy = dz2.astype(BF16)

        up = uprev_ref[...].astype(F32)[8:16, :]
        ubuf[0:FFN_HALO, :] = jnp.where(i == n - 1, 0.0, up)
        ubuf[FFN_HALO:FFN_HALO + ts, :] = ug_ref[...].astype(F32)
        for lo, hi in FFN_CHUNKS:
            taps = _ffn_taps(ubuf, ts, lo, hi)
            sl = sl_ref[:, lo:hi].astype(F32)
            uvf = uv_ref[:, lo:hi].astype(F32)
            act = (sl * uvf).astype(BF16)
            dwacc[lo:hi, :] += _dot_tn(act, dy)
            dact = _dot(dy, wdownt_ref[:, lo:hi])
            duv_ref[:, lo:hi] = (dact * sl).astype(BF16)
            dgate = dact * uvf * dsl_ref[:, lo:hi].astype(F32)
            dgbuf[0:ts, lo:hi] = dgate
            dcf_ref[3:4, lo:hi] += _rowsum(dgate)
            for k in range(FFN_CONV_K):
                dcf_ref[k:k + 1, lo:hi] += _rowsum(dgate * taps[k])
            dug = cf_ref[2:3, lo:hi] * dgate + cf_ref[1:2, lo:hi] * dgbuf[1:1 + ts, lo:hi]
            dug = dug + cf_ref[0:1, lo:hi] * dgbuf[2:2 + ts, lo:hi]
            dug_ref[:, lo:hi] = dug.astype(BF16)

        @pl.when(i == n - 1)
        def _():
            _store_blocks(dwacc, slabs_ref, sems, row_off, D_FF_PAD // N_DEV, 0, N_DEV)

    rev = lambda width: pl.BlockSpec((ts, width), lambda i: (n - 1 - i, 0))
    prev = pl.BlockSpec((16, D_FF_PAD), lambda i: (jnp.maximum((n - 1 - i) * halo_blocks - 1, 0), 0))
    hbm = pl.BlockSpec(memory_space=pl.ANY)
    return _grid_call(
        body, "ffn_bwd_last" if last else "ffn_bwd", n,
        in_specs=[hbm, rev(D_MODEL), rev(D_MODEL)] + [rev(D_FF_PAD)] * 4 + [prev, _const((8, D_FF_PAD)),
                                                                           _const((D_MODEL, D_FF_PAD)), _const((8, D_MODEL))],
        out_specs=[hbm, rev(D_FF_PAD), rev(D_FF_PAD), rev(D_MODEL),
                   _acc((8, D_FF_PAD)), _acc((8, D_MODEL)), _acc((8, 128))],
        out_shape=[jax.ShapeDtypeStruct(slabs.shape, F32),
                   jax.ShapeDtypeStruct((seq, D_FF_PAD), BF16), jax.ShapeDtypeStruct((seq, D_FF_PAD), BF16),
                   jax.ShapeDtypeStruct((seq, D_MODEL), F32),
                   jax.ShapeDtypeStruct((8, D_FF_PAD), F32), jax.ShapeDtypeStruct((8, D_MODEL), F32),
                   jax.ShapeDtypeStruct((8, 128), F32)],
        scratch_shapes=[pltpu.VMEM((ts + FFN_HALO, D_FF_PAD), F32), pltpu.VMEM((ts + FFN_HALO, D_FF_PAD), F32),
                        pltpu.VMEM((D_FF_PAD, D_MODEL), F32), pltpu.SemaphoreType.DMA((N_DEV,))],
        args=(slabs, dx2_or_target, z2, ug, uv, sl, dsl, ug, w["cf"], w["wdownt"], w["v2"]), aliases={0: 0}, exch=exch)


def _proj_bwd_call(d, wt, xin, addend, scale, ts, name, slabs, row_off, first_block, n_blocks, exch=None):
    seq, k = d.shape
    n = seq // ts

    def body(slabs_in, d_ref, wt_ref, xin_ref, add_ref, slabs_ref, dx_ref, acc, sems):
        i = pl.program_id(0)

        @pl.when(i == 0)
        def _():
            acc[...] = jnp.zeros(acc.shape, F32)

        dv = d_ref[...]
        dx_ref[...] = _dot(dv, wt_ref[...]) + scale * add_ref[...]
        acc[...] += _dot_tn(dv, xin_ref[...].astype(BF16))

        @pl.when(i == n - 1)
        def _():
            _store_blocks(acc, slabs_ref, sems, row_off, k // n_blocks, first_block, n_blocks)

    row = lambda width: pl.BlockSpec((ts, width), lambda i: (i, 0))
    hbm = pl.BlockSpec(memory_space=pl.ANY)
    return _grid_call(
        body, name, n,
        in_specs=[hbm, row(k), _const((k, D_MODEL)), row(D_MODEL), row(D_MODEL)],
        out_specs=[hbm, row(D_MODEL)],
        out_shape=[jax.ShapeDtypeStruct(slabs.shape, F32), jax.ShapeDtypeStruct((seq, D_MODEL), F32)],
        scratch_shapes=[pltpu.VMEM((k, D_MODEL), F32), pltpu.SemaphoreType.DMA((n_blocks,))],
        args=(slabs, d, wt, xin, addend), aliases={0: 0}, exch=exch)


MEM_FOLD = BLOB_LANES // XATTN_W


def _mem_proj_call(memq, wk_flat, wv_flat):
    def body(memq_ref, wk_ref, wv_ref, kh_ref, vh_ref):
        for w_ref, o_ref in ((wk_ref, kh_ref), (wv_ref, vh_ref)):
            acc = jnp.zeros((N_MEM, XATTN_W), F32)
            for q in range(MEM_FOLD):
                acc = acc + _dot(memq_ref[q], w_ref[:, q * XATTN_W:(q + 1) * XATTN_W])
            o_ref[...] = acc

    out = jax.ShapeDtypeStruct((N_MEM, XATTN_W), F32)
    return pl.pallas_call(body, name="mem_proj", out_shape=[out, out], compiler_params=_params())(memq, wk_flat, wv_flat)


def _mem_proj_bwd_call(memq, dkh, dvh, slabs, off_k, off_v):
    rows = D_MODEL // MEM_FOLD

    def body(slabs_in, memq_ref, dkh_ref, dvh_ref, slabs_ref, acc, sems):
        for d_ref, off in ((dkh_ref, off_k), (dvh_ref, off_v)):
            dv = d_ref[...].astype(BF16)
            for q in range(MEM_FOLD):
                acc[:, q * XATTN_W:(q + 1) * XATTN_W] = _dot_tn(memq_ref[q], dv)
            _store_blocks(acc, slabs_ref, sems, off, rows // N_DEV, 0, N_DEV)

    hbm = pl.BlockSpec(memory_space=pl.ANY)
    vmem = pl.BlockSpec(memory_space=pltpu.VMEM)
    return pl.pallas_call(
        body, name="mem_proj_bwd", in_specs=[hbm, vmem, vmem, vmem], out_specs=hbm,
        out_shape=jax.ShapeDtypeStruct(slabs.shape, F32),
        scratch_shapes=[pltpu.VMEM((rows, BLOB_LANES), F32), pltpu.SemaphoreType.DMA((N_DEV,))],
        input_output_aliases={0: 0}, compiler_params=_params(),
    )(slabs, memq, dkh, dvh)


def _place():
    return lax.axis_index("x"), lax.axis_index("y"), lax.axis_index("c")


def _all_gather_call(arrs, pieces, name, swap=None):
    n_in, n_p = len(arrs), len(pieces)
    n_sw = 0 if swap is None else 1

    def body(*refs):
        ins, outs = refs[:n_in], refs[n_in + n_sw:n_in + n_sw + n_p]
        send_sems, recv_sems, local_sems = refs[n_in + 2 * n_sw + n_p:n_in + 2 * n_sw + n_p + 3]
        swapped = []
        if swap is not None:
            swapped = swap[1](refs[n_in], refs[n_in + n_sw + n_p], *refs[-2:])
            for cp in swapped:
                cp.start()
        x, y, c = _place()
        me, sibling = (x, y, c), (x, y, 1 - c)
        chips = [(1 - x, y), (x, 1 - y), (1 - x, 1 - y)]

        def src(a):
            idx, r0, rows = pieces[a]
            return ins[idx] if r0 is None else ins[idx].at[pl.ds(r0, rows)]

        def slab(a, p):
            return outs[a].at[4 * p[0] + 2 * p[1] + p[2]]

        def copy(a, k, block, to, own=False):
            return pltpu.make_async_remote_copy(
                src_ref=src(a) if own else slab(a, block), dst_ref=slab(a, block),
                send_sem=send_sems.at[a, k], recv_sem=recv_sems.at[a, k], device_id=to, device_id_type=MESH)

        mine = [pltpu.make_async_copy(src(a), slab(a, me), local_sems.at[a]) for a in range(n_p)]
        for cp in mine:
            cp.start()
        first = []
        for a in range(n_p):
            first.append(copy(a, 0, me, sibling, own=True))
            first += [copy(a, 1 + j, me, (*chip, c), own=True) for j, chip in enumerate(chips)]
        for cp in first:
            cp.start()
        passed = []
        for a in range(n_p):
            for j, chip in enumerate(chips):
                copy(a, 1 + j, (*chip, c), me).wait_recv()
                cp = copy(a, 4 + j, (*chip, c), sibling)
                cp.start()
                passed.append(cp)
        for a in range(n_p):
            copy(a, 0, sibling, me).wait_recv()
            for j, chip in enumerate(chips):
                copy(a, 4 + j, (*chip, 1 - c), me).wait_recv()
        for cp in first + passed:
            cp.wait_send()
        for cp in mine:
            cp.wait()
        for cp in swapped:
            cp.wait()

    def out_shape(piece):
        idx, r0, rows = piece
        a = arrs[idx]
        return jax.ShapeDtypeStruct((N_DEV,) + (a.shape if r0 is None else (rows,) + a.shape[1:]), a.dtype)

    hbm = pl.BlockSpec(memory_space=pl.ANY)
    out_shapes = [out_shape(p) for p in pieces]
    scratch = [pltpu.SemaphoreType.DMA((n_p, 7)), pltpu.SemaphoreType.DMA((n_p, 7)), pltpu.SemaphoreType.DMA((n_p,))]
    args = list(arrs)
    if swap is not None:
        a, _, n = swap
        args.append(a)
        out_shapes.append(jax.ShapeDtypeStruct((n,) + a.shape[1:], a.dtype))
        scratch += [pltpu.SemaphoreType.DMA((n,)), pltpu.SemaphoreType.DMA((n,))]
    return pl.pallas_call(
        body, name=name, in_specs=[hbm] * len(args), out_specs=[hbm] * len(out_shapes), out_shape=out_shapes,
        scratch_shapes=scratch,
    )(*args)


def _flip(v, f):
    return 1 - v if f else v


def _gather_exchange(arrs, pieces):
    n_peers = N_DEV - 1

    def build(ins, outs, send_sems, recv_sems, local_sems):
        x, y, c = _place()
        flips = [(fx, fy, fc) for fx in (0, 1) for fy in (0, 1) for fc in (0, 1) if fx or fy or fc]
        remote, local = [], []
        for a, (idx, r0, rows) in enumerate(pieces):
            src = ins[idx] if r0 is None else ins[idx].at[pl.ds(r0, rows)]
            dst = outs[a].at[4 * x + 2 * y + c]
            remote += [pltpu.make_async_remote_copy(
                src_ref=src, dst_ref=dst, send_sem=send_sems.at[n_peers * a + k], recv_sem=recv_sems.at[n_peers * a + k],
                device_id=(_flip(x, fx), _flip(y, fy), _flip(c, fc)), device_id_type=MESH)
                for k, (fx, fy, fc) in enumerate(flips)]
            local.append(pltpu.make_async_copy(src, dst, local_sems.at[a]))
        return remote, local

    def out_shape(piece):
        idx, r0, rows = piece
        a = arrs[idx]
        return jax.ShapeDtypeStruct((N_DEV,) + (a.shape if r0 is None else (rows,) + a.shape[1:]), a.dtype)

    return _Exchange(arrs, [out_shape(p) for p in pieces], n_peers * len(pieces), build, n_local=len(pieces))


def _swap_core_copies(g_ref, r_ref, send_sems, recv_sems):
    x, y, c = _place()
    return [pltpu.make_async_remote_copy(
        src_ref=g_ref.at[2 * k + (1 - c)], dst_ref=r_ref.at[k], send_sem=send_sems.at[k], recv_sem=recv_sems.at[k],
        device_id=(x, y, 1 - c), device_id_type=MESH) for k in range(4)]


def _swap_chip_copies(p_ref, r_ref, send_sems, recv_sems):
    x, y, c = _place()
    chips = [(1 - x, y), (x, 1 - y), (1 - x, 1 - y)]
    return [pltpu.make_async_remote_copy(
        src_ref=p_ref.at[2 * px + py], dst_ref=r_ref.at[j], send_sem=send_sems.at[j], recv_sem=recv_sems.at[j],
        device_id=(px, py, c), device_id_type=MESH) for j, (px, py) in enumerate(chips)]


def _swap_exchange(a, copies, n):
    return _Exchange([a], [jax.ShapeDtypeStruct((n,) + a.shape[1:], a.dtype)], n,
                     lambda ins, outs, send_sems, recv_sems, local_sems: (copies(ins[0], outs[0], send_sems, recv_sems), []))


def _swap_call(a, copies, n, name):
    def body(a_ref, r_ref, send_sems, recv_sems):
        cps = copies(a_ref, r_ref, send_sems, recv_sems)
        for cp in cps:
            cp.start()
        for cp in cps:
            cp.wait()

    hbm = pl.BlockSpec(memory_space=pl.ANY)
    return pl.pallas_call(
        body, name=name, in_specs=[hbm], out_specs=hbm, out_shape=jax.ShapeDtypeStruct((n,) + a.shape[1:], a.dtype),
        scratch_shapes=[pltpu.SemaphoreType.DMA((n,)), pltpu.SemaphoreType.DMA((n,))],
    )(a)


def _swap_chip_start(p, rider):
    n = 3
    land = lax.empty((n,) + p.shape[1:], p.dtype)

    def body(p_ref, land_ref, rider_ref, send_sems, recv_sems, p_thru, land_thru, rider_thru):
        for cp in _swap_chip_copies(p_ref, land_ref, send_sems, recv_sems):
            cp.start()

    hbm, sem = pl.BlockSpec(memory_space=pltpu.HBM), pl.BlockSpec(memory_space=pltpu.SEMAPHORE)
    return pl.pallas_call(
        body, name="rs_swap_chip_start",
        out_shape=(pltpu.SemaphoreType.DMA((n,)), pltpu.SemaphoreType.DMA((n,)), pltpu.HBM(p.shape, p.dtype),
                   pltpu.HBM(land.shape, land.dtype), pltpu.HBM(rider.shape, rider.dtype)),
        in_specs=(hbm, hbm, hbm), out_specs=(sem, sem, hbm, hbm, hbm), input_output_aliases={0: 2, 1: 3, 2: 4},
        compiler_params=pltpu.CompilerParams(has_side_effects=pltpu.SideEffectType.DATAFLOW_SIDE_EFFECTING),
    )(pltpu.with_memory_space_constraint(p, pltpu.HBM), pltpu.with_memory_space_constraint(land, pltpu.HBM),
      pltpu.with_memory_space_constraint(rider, pltpu.HBM))


def _swap_chip_wait(send_sems, recv_sems, p_thru, land_thru, *after):
    def body(p_ref, land_ref, send_sems, recv_sems, *rest):
        for cp in _swap_chip_copies(p_ref, land_ref, send_sems, recv_sems):
            cp.wait_send()
            cp.wait_recv()

    hbm, sem = pl.BlockSpec(memory_space=pltpu.HBM), pl.BlockSpec(memory_space=pltpu.SEMAPHORE)
    return pl.pallas_call(
        body, name="rs_swap_chip_wait",
        out_shape=(pltpu.HBM(p_thru.shape, p_thru.dtype), pltpu.HBM(land_thru.shape, land_thru.dtype)),
        in_specs=(hbm, hbm, sem, sem) + (pl.BlockSpec(memory_space=pl.ANY),) * len(after), out_specs=(hbm, hbm),
        input_output_aliases={0: 0, 1: 1},
        compiler_params=pltpu.CompilerParams(has_side_effects=pltpu.SideEffectType.DATAFLOW_SIDE_EFFECTING),
    )(p_thru, land_thru, send_sems, recv_sems, *after)[1]


ADD_ROWS = 184


def _pair_add_call(g, r, c, chip):
    _, rows, width = g.shape

    def body(c_ref, chip_ref, g_ref, r_ref, o_ref):
        o_ref[...] = (g_ref[...] + r_ref[...]).astype(BF16)

    def other(j, chip_ref):
        return j + (j >= chip_ref[0]).astype(jnp.int32)

    return pl.pallas_call(
        body, name="rs_pair_add",
        grid_spec=pltpu.PrefetchScalarGridSpec(
            num_scalar_prefetch=2, grid=(3, rows // ADD_ROWS),
            in_specs=[pl.BlockSpec((None, ADD_ROWS, width),
                                   lambda j, i, c_ref, chip_ref: (2 * other(j, chip_ref) + c_ref[0], i, 0)),
                      pl.BlockSpec((None, ADD_ROWS, width), lambda j, i, c_ref, chip_ref: (other(j, chip_ref), i, 0))],
            out_specs=pl.BlockSpec((None, ADD_ROWS, width), lambda j, i, c_ref, chip_ref: (other(j, chip_ref), i, 0))),
        out_shape=jax.ShapeDtypeStruct((4, rows, width), BF16),
        compiler_params=_params(dimension_semantics=("arbitrary", "arbitrary")),
    )(c, chip, g, r)


def _adam(w, g, m, v):
    mn = ADAM_B1 * m + (1.0 - ADAM_B1) * g
    vn = ADAM_B2 * v + (1.0 - ADAM_B2) * (g * g)
    m_hat = mn / (1.0 - ADAM_B1 ** ADAM_STEP)
    v_hat = vn / (1.0 - ADAM_B2 ** ADAM_STEP)
    return -ADAM_LR * (m_hat / (jnp.sqrt(v_hat) + ADAM_EPS) + ADAM_WD * w), mn, vn


def _chip_add_adamw_call(slabs, from_sibling, from_chips, me, chip, w, m, v):
    _, rows, width = slabs.shape

    def body(me_ref, chip_ref, own_ref, sib_ref, r_ref, w_ref, m_ref, v_ref, g_ref, d_ref, mo_ref, vo_ref):
        g = own_ref[...] + sib_ref[...]
        for j in range(3):
            g = g + r_ref[j].astype(F32)
        g_ref[...] = g
        d_ref[...], mo_ref[...], vo_ref[...] = _adam(w_ref[...], g, m_ref[...], v_ref[...])

    spec = pl.BlockSpec((ADD_ROWS, width), lambda i, me_ref, chip_ref: (i, 0))
    return pl.pallas_call(
        body, name="rs_chip_add_adamw",
        grid_spec=pltpu.PrefetchScalarGridSpec(
            num_scalar_prefetch=2, grid=(rows // ADD_ROWS,),
            in_specs=[pl.BlockSpec((None, ADD_ROWS, width), lambda i, me_ref, chip_ref: (me_ref[0], i, 0)),
                      pl.BlockSpec((None, ADD_ROWS, width), lambda i, me_ref, chip_ref: (chip_ref[0], i, 0)),
                      pl.BlockSpec((3, ADD_ROWS, width), lambda i, me_ref, chip_ref: (0, i, 0)), spec, spec, spec],
            out_specs=[spec] * 4),
        out_shape=[jax.ShapeDtypeStruct((rows, width), F32)] * 4,
        compiler_params=_params(dimension_semantics=("arbitrary",)),
    )(me, chip, slabs, from_sibling, from_chips, w, m, v)


def _adamw_whole_call(params, name):
    n = len(params)

    def body(*refs):
        ins, outs = refs[:4 * n], refs[4 * n:]
        for a in range(n):
            w_ref, g_ref, m_ref, v_ref = ins[4 * a:4 * a + 4]
            outs[3 * a][...], outs[3 * a + 1][...], outs[3 * a + 2][...] = _adam(w_ref[...], g_ref[...], m_ref[...],
                                                                              v_ref[...])

    flat = [a for p in params for a in p]
    shapes = [jax.ShapeDtypeStruct(p[0].shape, F32) for p in params for _ in range(3)]
    out = pl.pallas_call(body, name=name, out_shape=shapes, compiler_params=_params())(*flat)
    return [tuple(out[3 * a:3 * a + 3]) for a in range(n)]


GATHERED_ACCS = (("dva", (8, CONV_W)), ("dv1", (8, D_MODEL)), ("dv2", (8, D_MODEL)), ("dcf", (8, D_FF_PAD)),
                 ("dcw", (HALO, CONV_W)), ("dwcat", (CHUNK, GMLP_HEADS * CHUNK)), ("dmsum", (CHUNK, GMLP_W)))
VEC_A = ("conv_a_b", "ln_a_g", "ln_a_b", "ln_v_g", "ln_v_b")
REP_IN_KERNEL = VEC_A + ("ln1_g", "ln1_b", "ln2_g", "ln2_b", "w_s", "b_s")


def _replicated_update_call(gathered, p, mom_m, mom_v):
    n_acc = len(GATHERED_ACCS)
    n_rep = len(REP_IN_KERNEL)

    def body(*refs):
        acc_refs = refs[:DEPTH * n_acc]
        wmv = refs[DEPTH * n_acc:DEPTH * n_acc + 3 * n_rep]
        outs = refs[DEPTH * n_acc + 3 * n_rep:]
        out_par = {nm: outs[4 * a:4 * a + 4] for a, nm in enumerate(REP_IN_KERNEL)}
        out_dcf = outs[4 * n_rep:4 * n_rep + DEPTH]
        out_dcw = outs[4 * n_rep + DEPTH:4 * n_rep + 2 * DEPTH]
        par = {nm: wmv[3 * a:3 * a + 3] for a, nm in enumerate(REP_IN_KERNEL)}
        tril = (lax.broadcasted_iota(jnp.int32, (CHUNK, CHUNK), 0) >= lax.broadcasted_iota(jnp.int32, (CHUNK, CHUNK), 1))
        head = lax.broadcasted_iota(jnp.int32, (8, GMLP_W), 0) * HEAD_DIM
        lane = lax.broadcasted_iota(jnp.int32, (8, GMLP_W), 1)
        sel = jnp.where((lane >= head) & (lane < head + HEAD_DIM), 1.0, 0.0)

        def update(nm, idx, g):
            w_ref, m_ref, v_ref = par[nm]
            d, mn, vn = _adam(w_ref[idx], g, m_ref[idx], v_ref[idx])
            g_ref, d_ref, mo_ref, vo_ref = out_par[nm]
            g_ref[idx] = g
            d_ref[idx] = d
            mo_ref[idx] = mn
            vo_ref[idx] = vn

        for l in range(DEPTH):
            tot = {}
            for a, (nm, _) in enumerate(GATHERED_ACCS):
                ref = acc_refs[l * n_acc + a]
                s = ref[0]
                for j in range(1, N_DEV):
                    s = s + ref[j]
                tot[nm] = s
            out_dcf[l][...] = tot["dcf"]
            out_dcw[l][...] = tot["dcw"]
            row = (slice(l, l + 1), slice(None))
            for k, nm in enumerate(VEC_A):
                update(nm, row, tot["dva"][k:k + 1, :])
            update("ln1_g", row, tot["dv1"][0:1, :])
            update("ln1_b", row, tot["dv1"][1:2, :])
            update("ln2_g", row, tot["dv2"][0:1, :])
            update("ln2_b", row, tot["dv2"][1:2, :])
            for h in range(GMLP_HEADS):
                gw = jnp.where(tril, tot["dwcat"][:, h * CHUNK:(h + 1) * CHUNK], 0.0)
                update("w_s", (l, h), gw)
            gb = lax.dot_general(sel, tot["dmsum"], (((1,), (1,)), ((), ())), precision=lax.Precision.HIGHEST,
                                 preferred_element_type=F32)
            for h in range(GMLP_HEADS):
                update("b_s", (l, slice(h, h + 1), slice(None)), gb[h:h + 1, :])

    ins = [gathered[l][nm] for l in range(DEPTH) for nm, _ in GATHERED_ACCS]
    ins += [t[nm] for nm in REP_IN_KERNEL for t in (p, mom_m, mom_v)]
    shapes = [jax.ShapeDtypeStruct(p[nm].shape, F32) for nm in REP_IN_KERNEL for _ in range(4)]
    shapes += [jax.ShapeDtypeStruct((8, D_FF_PAD), F32)] * DEPTH + [jax.ShapeDtypeStruct((HALO, CONV_W), F32)] * DEPTH
    out = pl.pallas_call(body, name="replicated_update", out_shape=shapes, compiler_params=_params())(*ins)
    res = [{nm: out[4 * a + k] for a, nm in enumerate(REP_IN_KERNEL)} for k in range(4)]
    return res, out[4 * n_rep:4 * n_rep + DEPTH], out[4 * n_rep + DEPTH:]


BLOCK_ROWS = (("w_in", IN_W // N_DEV), ("w_out", D_MODEL // N_DEV), ("w_up", 2 * FF_GROUP_PAD),
              ("w_down", FF_GROUP_PAD), ("w_mk", D_MODEL // N_DEV // MEM_FOLD), ("w_mv", D_MODEL // N_DEV // MEM_FOLD))
LAYER_ROWS = sum(r for _, r in BLOCK_ROWS)
assert LAYER_ROWS % ADD_ROWS == 0 and all(r % 16 == 0 for _, r in BLOCK_ROWS)


def _row_off(name):
    off = 0
    for nm, r in BLOCK_ROWS:
        if nm == name:
            return off
        off += r
    raise KeyError(name)


def _to_rows(name, a):
    if name == "w_in":
        return a.T
    if name == "w_up":
        t = a.T.reshape(2, FF_GROUP, D_MODEL)
        return jnp.pad(t, ((0, 0), (0, FF_GROUP_PAD - FF_GROUP), (0, 0))).reshape(2 * FF_GROUP_PAD, D_MODEL)
    if name == "w_down":
        return jnp.pad(a, ((0, FF_GROUP_PAD - FF_GROUP), (0, 0)))
    if name == "w_out":
        return a
    return a.reshape(-1, BLOB_LANES)


def _from_rows(name, r):
    if name == "w_in":
        return r.T
    if name == "w_up":
        return r.reshape(2, FF_GROUP_PAD, D_MODEL)[:, :FF_GROUP].reshape(2 * FF_GROUP, D_MODEL).T
    if name == "w_down":
        return r[:FF_GROUP]
    if name == "w_out":
        return r
    return r.reshape(D_MODEL // N_DEV, XATTN_W)


def _blob(tree, l):
    return jnp.concatenate([_to_rows(nm, tree[nm][l]) for nm, _ in BLOCK_ROWS], axis=0)


def _unblob(blobs):
    return {nm: jnp.stack([_from_rows(nm, b[_row_off(nm):_row_off(nm) + r]) for b in blobs]) for nm, r in BLOCK_ROWS}


def _ff_interleave(a):
    lead = a.shape[:-1]
    t = a.reshape(lead + (N_DEV, FF_GROUP))
    return jnp.pad(t, [(0, 0)] * len(lead) + [(0, 0), (0, FF_GROUP_PAD - FF_GROUP)]).reshape(lead + (D_FF_PAD,))


def _ff_deinterleave(a):
    lead = a.shape[:-1]
    return a.reshape(lead + (N_DEV, FF_GROUP_PAD))[..., :FF_GROUP].reshape(lead + (D_FF,))


def _head_table():
    hd = jnp.arange(XATTN_W) // HEAD_DIM
    return (hd[None, :] == jnp.arange(XATTN_HEADS)[:, None]).astype(F32)


def _mixer_operands(mat, conv_a_w, p, l, memq):
    w = {}
    w["wint"] = mat["w_in"]
    w["win"] = mat["w_in"].T
    w["wout"] = mat["w_out"]
    w["woutt"] = mat["w_out"].T
    w["cw"] = conv_a_w
    zeros = jnp.zeros((3, CONV_W), F32)
    w["va"] = jnp.concatenate([p[nm][l][None] for nm in VEC_A] + [zeros], axis=0)
    tril = jnp.tril(jnp.ones((CHUNK, CHUNK), F32))
    w["wcat"] = (p["w_s"][l] * tril[None]).transpose(1, 0, 2).reshape(CHUNK, GMLP_HEADS * CHUNK).astype(BF16)
    w["wcatt"] = w["wcat"].T
    w["bfull"] = jnp.repeat(p["b_s"][l].T, HEAD_DIM, axis=1)
    kh, vh = _mem_proj_call(memq, mat["w_mk"], mat["w_mv"])
    hm = _head_table()
    scale = 1.0 / math.sqrt(HEAD_DIM)
    w["kt"] = (kh.T[:, None, :] * hm.T[:, :, None] * scale).reshape(XATTN_W, XATTN_HEADS * N_MEM).astype(BF16)
    w["ktt"] = w["kt"].T
    w["vm"] = (hm[:, None, :] * vh[None]).reshape(XATTN_HEADS * N_MEM, XATTN_W).astype(BF16)
    w["vmt"] = w["vm"].T
    zeros = jnp.zeros((6, D_MODEL), F32)
    w["v1"] = jnp.concatenate([p["ln1_g"][l][None], p["ln1_b"][l][None], zeros], axis=0)
    return w


def _ffn_operands(w_up, w_down, conv_f_w, p, l):
    w = {}
    w["wgt"] = w_up[:D_FF_PAD]
    w["wvt"] = w_up[D_FF_PAD:]
    w["wg"] = w["wgt"].T
    w["wv"] = w["wvt"].T
    w["wdown"] = w_down
    w["wdownt"] = w_down.T
    zeros = jnp.zeros((6, D_MODEL), F32)
    w["v2"] = jnp.concatenate([p["ln2_g"][l][None], p["ln2_b"][l][None], zeros], axis=0)
    w["cf"] = jnp.concatenate([conv_f_w, _ff_interleave(p["conv_f_b"][l][None]), jnp.zeros((4, D_FF_PAD), F32)], axis=0)
    return w


TS_MIXER = 256
TS_FFN = 256
TS_PROJ = 512
CONV_A_SHARD = CONV_W // N_DEV


def kernel(x, mem, w_in, conv_a_w, conv_a_b, ln_a_g, ln_a_b, ln_v_g, ln_v_b, w_s, b_s, w_mk, w_mv, w_out, ln1_g, ln1_b, w_up, conv_f_w, conv_f_b, w_down, ln2_g, ln2_b, loss_target, m_w_in, m_conv_a_w, m_conv_a_b, m_ln_a_g, m_ln_a_b, m_ln_v_g, m_ln_v_b, m_w_s, m_b_s, m_w_mk, m_w_mv, m_w_out, m_ln1_g, m_ln1_b, m_w_up, m_conv_f_w, m_conv_f_b, m_w_down, m_ln2_g, m_ln2_b, v_w_in, v_conv_a_w, v_conv_a_b, v_ln_a_g, v_ln_a_b, v_ln_v_g, v_ln_v_b, v_w_s, v_b_s, v_w_mk, v_w_mv, v_w_out, v_ln1_g, v_ln1_b, v_w_up, v_conv_f_w, v_conv_f_b, v_w_down, v_ln2_g, v_ln2_b):
    given = dict(locals())
    p = {nm: given[nm] for nm in WEIGHTS}
    mom_m = {nm: given["m_" + nm] for nm in WEIGHTS}
    mom_v = {nm: given["v_" + nm] for nm in WEIGHTS}
    seq = x.shape[1]
    ts_m, ts_f, ts_p = min(TS_MIXER, seq), min(TS_FFN, seq), min(TS_PROJ, seq)
    cx, cy, cc = _place()
    me = 4 * cx + 2 * cy + cc

    blobs = [_blob(p, l) for l in range(DEPTH)]
    blobs_bf = [b.astype(BF16) for b in blobs]
    conv_a_tile = jnp.pad(conv_a_w, ((0, 0), (0, HALO - CONV_K), (0, 128 - CONV_A_SHARD)))
    conv_f_tile = jnp.pad(conv_f_w, ((0, 0), (0, 8 - FFN_CONV_K), (0, 384 - FF_GROUP)))
    rows = dict(BLOCK_ROWS)
    mixer_names = ("w_in", "w_out", "w_mk", "w_mv")
    pieces = [(0, _row_off(nm), rows[nm]) for nm in mixer_names] + [(1, None, 0), (2, None, 0)]
    first = _all_gather_call([blobs_bf[0], conv_a_tile, conv_f_tile], pieces, "gather_weights")
    conv_a_all, conv_f_all = first[len(mixer_names)], first[len(mixer_names) + 1]
    conv_a = [conv_a_all[:, l, :, :CONV_A_SHARD].transpose(1, 0, 2).reshape(HALO, CONV_W) for l in range(DEPTH)]
    conv_f = [conv_f_all[:, l, :FFN_CONV_K, :FF_GROUP_PAD].transpose(1, 0, 2).reshape(FFN_CONV_K, D_FF_PAD)
              for l in range(DEPTH)]
    memq = mem[0].reshape(N_MEM, D_MODEL // MEM_FOLD, MEM_FOLD).transpose(2, 0, 1).astype(BF16)

    def matrix(gathered, nm, base):
        lo = _row_off(nm) - base
        return gathered[:, lo:lo + rows[nm]].reshape(-1, BLOB_LANES)

    ffn_base = _row_off("w_up")
    ops0 = _mixer_operands({nm: first[a].reshape(-1, BLOB_LANES) for a, nm in enumerate(mixer_names)}, conv_a[0], p, 0,
                           memq)
    mixer_saved = ("hb", "z1", "x1", "a1", "cat", "p", "gs", "rv")
    mixed0, (ffn0,) = _mixer_fwd_call(
        x[0], ops0, ts_m, _gather_exchange([blobs_bf[0]], [(0, ffn_base, rows["w_up"] + rows["w_down"])]))
    ops0.update(_ffn_operands(matrix(ffn0, "w_up", ffn_base), matrix(ffn0, "w_down", ffn_base), conv_f[0], p, 0))
    saved = [dict(zip(mixer_saved, mixed0), x=x[0])]
    (ug, uv, sl, dsl, z2, x2), (all1,) = _ffn_fwd_call(saved[0]["x1"], ops0, ts_f,
                                                       _gather_exchange([blobs_bf[1]], [(0, 0, LAYER_ROWS)]))
    saved[0].update(ug=ug, uv=uv, sl=sl, dsl=dsl, z2=z2)
    ops1 = _mixer_operands({nm: matrix(all1, nm, 0) for nm in mixer_names}, conv_a[1], p, 1, memq)
    ops1.update(_ffn_operands(matrix(all1, "w_up", 0), matrix(all1, "w_down", 0), conv_f[1], p, 1))
    mixed1, _ = _mixer_fwd_call(x2, ops1, ts_m)
    saved.append(dict(zip(mixer_saved, mixed1), x=x2))
    (ug, uv, sl, dsl, z2), _ = _ffn_fwd_call(saved[1]["x1"], ops1, ts_f, want_x2=False)
    saved[1].update(ug=ug, uv=uv, sl=sl, dsl=dsl, z2=z2)
    ops = [ops0, ops1]

    hm = _head_table()
    core_id = cc.reshape(1).astype(jnp.int32)
    me_id, chip_id = me.reshape(1).astype(jnp.int32), (2 * cx + cy).reshape(1).astype(jnp.int32)
    slabs = [lax.empty((N_DEV, LAYER_ROWS, BLOB_LANES), F32) for _ in range(DEPTH)]
    accs, gathered_accs = [None] * DEPTH, [None] * DEPTH
    acc_names = [nm for nm, _ in GATHERED_ACCS]
    whole = [(a, None, 0) for a in range(len(acc_names))]

    def acc_list(l):
        return [accs[l][nm] for nm in acc_names]

    from_sibling, from_chips = [None] * DEPTH, [None] * DEPTH
    dx = loss_target[0]
    loss = None
    for l in reversed(range(DEPTH)):
        s, w = saved[l], ops[l]
        last = l == DEPTH - 1
        ride = None if last else _swap_exchange(slabs[l + 1], _swap_core_copies, 4)
        (sl, dug, duv, dz2, dcf, dv2, loss_acc), got = _ffn_bwd_call(
            dx, s["z2"], s["ug"], s["uv"], s["sl"], s["dsl"], w, ts_f, last, slabs[l], _row_off("w_down"), ride)
        if last:
            loss = loss_acc[0, 0]
        else:
            from_sibling[l + 1] = got[0]
            chip_sum = _pair_add_call(slabs[l + 1], from_sibling[l + 1], core_id, chip_id)
        off_up = _row_off("w_up")
        ride = None if last else _gather_exchange(acc_list(l + 1), whole)
        (sl, dxa), got = _proj_bwd_call(dug, w["wgt"], s["x1"], dz2, ALPHA, ts_p, "up_gate_bwd", sl, off_up, 0, 4, ride)
        if not last:
            gathered_accs[l + 1] = dict(zip(acc_names, got))
        ride = None if last else _swap_exchange(chip_sum, _swap_chip_copies, 3)
        (sl, dx1), got = _proj_bwd_call(duv, w["wvt"], s["x1"], dxa, 1.0, ts_p, "up_val_bwd", sl, off_up, 4, 4, ride)
        if not last:
            from_chips[l + 1] = got[0]
        (sl, dx, dkt, dvm, dwcat, dmsum, dva, dcw, dv1) = _mixer_bwd_call(
            dx1, s["z1"], s["hb"], s["a1"], s["cat"], s["p"], s["gs"], s["rv"], s["x"], w, ts_m, sl, _row_off("w_out"),
            _row_off("w_in"))
        dkh = jnp.einsum("hd,dhm->md", hm, dkt.reshape(XATTN_W, XATTN_HEADS, N_MEM)) * (1.0 / math.sqrt(HEAD_DIM))
        dvh = jnp.einsum("hd,hmd->md", hm, dvm.reshape(XATTN_HEADS, N_MEM, XATTN_W))
        slabs[l] = _mem_proj_bwd_call(memq, dkh, dvh, sl, _row_off("w_mk"), _row_off("w_mv"))
        accs[l] = dict(dva=dva, dv1=dv1, dv2=dv2, dcf=dcf, dcw=dcw, dwcat=dwcat, dmsum=dmsum)
    grad_x = dx[None]
    *got, from_sibling[0] = _all_gather_call(acc_list(0), whole, "gather_small_grads",
                                            swap=(slabs[0], _swap_core_copies, 4))
    gathered_accs[0] = dict(zip(acc_names, got))
    chip_sum = _pair_add_call(slabs[0], from_sibling[0], core_id, chip_id)
    in_flight = _swap_chip_start(chip_sum, from_chips[1])
    from_chips[1] = in_flight[4]

    def final(l):
        return _chip_add_adamw_call(slabs[l], from_sibling[l], from_chips[l], me_id, chip_id, blobs[l], _blob(mom_m, l),
                                    _blob(mom_v, l))

    per_layer = [None, final(1)]
    rep, dcf_sum, dcw_sum = _replicated_update_call(gathered_accs, p, mom_m, mom_v)
    from_chips[0] = _swap_chip_wait(*in_flight[:4], per_layer[1][0], dcf_sum[0])
    per_layer[0] = final(0)
    outs = [_unblob([per_layer[l][k] for l in range(DEPTH)]) for k in range(4)]
    for k in range(4):
        outs[k].update(rep[k])
    dcf_sum, dcw_sum = jnp.stack(dcf_sum), jnp.stack(dcw_sum)
    zero = jnp.zeros((), jnp.int32)
    g_conv_a_w = lax.dynamic_slice(dcw_sum, (zero, zero, CONV_A_SHARD * me), (DEPTH, CONV_K, CONV_A_SHARD))
    g_conv_f_w = lax.dynamic_slice(dcf_sum, (zero, zero, FF_GROUP_PAD * me), (DEPTH, FFN_CONV_K, FF_GROUP))
    g_conv_f_b = _ff_deinterleave(dcf_sum[:, FFN_CONV_K])
    conv_grads = dict(conv_a_w=g_conv_a_w, conv_f_w=g_conv_f_w, conv_f_b=g_conv_f_b)
    conv_names = tuple(conv_grads)
    upd = _adamw_whole_call([(p[nm], conv_grads[nm], mom_m[nm], mom_v[nm]) for nm in conv_names], "adamw_conv")
    for nm, (d, mn, vn) in zip(conv_names, upd):
        outs[0][nm], outs[1][nm], outs[2][nm], outs[3][nm] = conv_grads[nm], d, mn, vn

    loss = lax.psum(loss, ("x", "y", "c"))
    return (loss, grad_x, *[outs[0][nm] for nm in WEIGHTS], *[outs[1][nm] for nm in WEIGHTS],
            *[outs[2][nm] for nm in WEIGHTS], *[outs[3][nm] for nm in WEIGHTS])
```

```python
import math

import jax
import jax.numpy as jnp
from jax import lax
from jax.experimental import pallas as pl
from jax.experimental.pallas import tpu as pltpu

F32 = jnp.float32
BF16 = jnp.bfloat16

DEPTH = 2
D_MODEL = 1024
CONV_W = 384
GMLP_W = 384
XATTN_W = 256
HEAD_DIM = 64
GMLP_HEADS = 6
XATTN_HEADS = 4
IN_W = 1792
CONV_K = 31
CHUNK = 128
N_MEM = 256
D_FF = 2752
D_FF_PAD = 2816
FFN_CONV_K = 3
ALPHA = (2.0 * DEPTH) ** 0.25
LN_EPS = 1e-5
N_DEV = 8

ADAM_LR = 0.001
ADAM_B1 = 0.9
ADAM_B2 = 0.999
ADAM_EPS = 1e-08
ADAM_WD = 0.01
ADAM_STEP = 10

HALO = 32
CONV_ROWS = 32
V7X_VMEM_BYTES = 64 * 1024 * 1024
VMEM_LIMIT = V7X_VMEM_BYTES - 8 * 1024 * 1024
BLOB_LANES = 1024

MESH = pl.DeviceIdType.MESH

WEIGHTS = ("w_in", "conv_a_w", "conv_a_b", "ln_a_g", "ln_a_b", "ln_v_g", "ln_v_b", "w_s", "b_s", "w_mk", "w_mv",
           "w_out", "ln1_g", "ln1_b", "w_up", "conv_f_w", "conv_f_b", "w_down", "ln2_g", "ln2_b")


def _params(**kw):
    return pltpu.CompilerParams(vmem_limit_bytes=VMEM_LIMIT, **kw)


def _const(shape):
    nd = len(shape)
    return pl.BlockSpec(shape, lambda i: (0,) * nd, pipeline_mode=pl.Buffered(1))


def _acc(shape):
    nd = len(shape)
    return pl.BlockSpec(shape, lambda i: (0,) * nd)


class _Exchange:
    def __init__(self, arrays, out_shapes, n_copies, build, n_local=1):
        self.arrays, self.out_shapes, self.n_copies, self.build = list(arrays), list(out_shapes), n_copies, build
        self.n_local = n_local


def _carry(core, n_in, n_out, exch, n_steps):
    if exch is None:
        return core
    nx_in, nx_out = len(exch.arrays), len(exch.out_shapes)

    def body(*refs):
        o0 = n_in + nx_in
        s0 = o0 + n_out + nx_out
        x_in, x_out, sems = refs[n_in:o0], refs[o0 + n_out:s0], refs[-3:]
        i = pl.program_id(0)

        @pl.when(i == 0)
        def _():
            remote, local = exch.build(x_in, x_out, *sems)
            for cp in remote + local:
                cp.start()

        core(*refs[:n_in], *refs[o0:o0 + n_out], *refs[s0:-3])

        @pl.when(i == n_steps - 1)
        def _():
            remote, local = exch.build(x_in, x_out, *sems)
            for cp in remote + local:
                cp.wait()

    return body


def _grid_call(core, name, n_steps, in_specs, out_specs, out_shape, scratch_shapes, args, aliases=None, exch=None):
    hbm = pl.BlockSpec(memory_space=pl.ANY)
    n_in, n_out = len(in_specs), len(out_specs)
    in_specs, out_specs, out_shape, scratch_shapes, args = (list(in_specs), list(out_specs), list(out_shape),
                                                            list(scratch_shapes), list(args))
    if exch is not None:
        in_specs += [hbm] * len(exch.arrays)
        out_specs += [hbm] * len(exch.out_shapes)
        out_shape += exch.out_shapes
        scratch_shapes += [pltpu.SemaphoreType.DMA((exch.n_copies,)), pltpu.SemaphoreType.DMA((exch.n_copies,)),
                           pltpu.SemaphoreType.DMA((exch.n_local,))]
        args += exch.arrays
    out = pl.pallas_call(
        _carry(core, n_in, n_out, exch, n_steps), name=name, grid=(n_steps,), in_specs=in_specs, out_specs=out_specs,
        out_shape=out_shape, scratch_shapes=scratch_shapes, input_output_aliases=aliases or {},
        compiler_params=_params(dimension_semantics=("arbitrary",)))(*args)
    return list(out[:n_out]), list(out[n_out:])


def _sigmoid(x):
    return 1.0 / (1.0 + jnp.exp(-x))


_GELU_C = math.sqrt(2.0 / math.pi)


def _gelu(x):
    x2 = x * x
    t = jnp.tanh(_GELU_C * (x + 0.044715 * x * x2))
    g = 0.5 * x * (1.0 + t)
    dg = 0.5 * (1.0 + t) + 0.5 * x * (1.0 - t * t) * (_GELU_C * (1.0 + 3.0 * 0.044715 * x2))
    return g, dg


def _ln_stats(z):
    mu = jnp.mean(z, axis=-1, keepdims=True)
    zc = z - mu
    var = jnp.mean(zc * zc, axis=-1, keepdims=True)
    r = lax.rsqrt(var + LN_EPS)
    return zc * r, r


def _ln_bwd(dy, xh, r, g):
    dxh = dy * g
    m1 = jnp.mean(dxh, axis=-1, keepdims=True)
    m2 = jnp.mean(dxh * xh, axis=-1, keepdims=True)
    return r * (dxh - m1 - xh * m2)


def _rowsum(x):
    return jnp.sum(x, axis=0, keepdims=True)


def _dot(a, b):
    return jnp.dot(a, b, preferred_element_type=F32)


def _dot_tn(a, b):
    return lax.dot_general(a, b, (((0,), (0,)), ((), ())), preferred_element_type=F32)


def _dot_nt(a, b):
    return lax.dot_general(a, b, (((1,), (1,)), ((), ())), preferred_element_type=F32)


def _shift_copies(buf, sh, rows):
    for b in range(1, 8):
        sh[b - 1, 0:rows, :] = buf[b:b + rows, :]


def _window(buf, sh, start):
    b = start % 8
    a = start - b
    return buf[a:a + CONV_ROWS, :] if b == 0 else sh[b - 1, a:a + CONV_ROWS, :]


def _conv31_fwd(buf, sh, w_ref, bias, out, ts):
    for r0 in range(0, ts, CONV_ROWS):
        acc = jnp.broadcast_to(bias, (CONV_ROWS, CONV_W))
        for k in range(CONV_K):
            acc = acc + w_ref[k:k + 1, :] * _window(buf, sh, r0 + HALO - (CONV_K - 1) + k)
        out[r0:r0 + CONV_ROWS, :] = acc


def _conv31_dx(dbuf, dsh, w_ref, out, ts):
    for r0 in range(0, ts, CONV_ROWS):
        acc = jnp.zeros((CONV_ROWS, CONV_W), F32)
        for k in range(CONV_K):
            acc = acc + w_ref[k:k + 1, :] * _window(dbuf, dsh, r0 + (CONV_K - 1) - k)
        out[r0:r0 + CONV_ROWS, :] = acc


def _conv31_dw(buf, sh, dbuf, acc, ts):
    for r0 in range(0, ts, CONV_ROWS):
        d = dbuf[r0:r0 + CONV_ROWS, :]
        for k in range(CONV_K):
            m = d * _window(buf, sh, r0 + HALO - (CONV_K - 1) + k)
            part = m[0:8, :]
            for q in range(8, CONV_ROWS, 8):
                part = part + m[q:q + 8, :]
            acc[k] += part


def _head_mask(width, h):
    lane = lax.broadcasted_iota(jnp.int32, (CHUNK, width), 1)
    return (lane >= h * HEAD_DIM) & (lane < (h + 1) * HEAD_DIM)


def _stack_heads(vn_c):
    return jnp.concatenate([jnp.where(_head_mask(GMLP_W, h), vn_c, 0.0) for h in range(GMLP_HEADS)], axis=0)


def _group_a_fwd(hf, buf, sh, a1_ref, cw_ref, va_ref, ts, conv=True):
    ha = hf[:, 0:CONV_W]
    sg = _sigmoid(hf[:, CONV_W:2 * CONV_W])
    buf[HALO:HALO + ts, :] = ha * sg
    _shift_copies(buf, sh, ts + HALO - 8)
    if conv:
        _conv31_fwd(buf, sh, cw_ref, va_ref[0:1, :], a1_ref, ts)
    a2h, ra = _ln_stats(a1_ref[...])
    a2 = a2h * va_ref[1:2, :] + va_ref[2:3, :]
    sa = _sigmoid(a2)
    return dict(ha=ha, sg=sg, a2h=a2h, ra=ra, a2=a2, sa=sa, a=a2 * sa)


GROUP_LANES = ((0, 2 * CONV_W), (2 * CONV_W, 2 * CONV_W + 2 * GMLP_W), (2 * CONV_W + 2 * GMLP_W, IN_W))


def _group_b_fwd(hf, va_ref, wcat_ref, bfull_ref, ts):
    hu = hf[:, 0:GMLP_W]
    hv = hf[:, GMLP_W:2 * GMLP_W]
    u, du = _gelu(hu)
    v, dv = _gelu(hv)
    vhat, rv = _ln_stats(v)
    vn = vhat * va_ref[3:4, :] + va_ref[4:5, :]
    stacks, mixed = [], []
    for c0 in range(0, ts, CHUNK):
        st = _stack_heads(vn[c0:c0 + CHUNK, :]).astype(BF16)
        stacks.append(st)
        mixed.append(_dot(wcat_ref[...], st) + bfull_ref[...])
    mixed = jnp.concatenate(mixed, axis=0) if len(mixed) > 1 else mixed[0]
    return dict(u=u, du=du, dv=dv, vhat=vhat, rv=rv, stacks=stacks, mixed=mixed, g=u * mixed)


GROUP_B_SAVED = ("u", "du", "dv", "vhat", "mixed")


def _group_c_fwd(qb, kt_ref, vm_ref):
    s_all = _dot(qb, kt_ref[...])
    ps = []
    for g in range(XATTN_HEADS):
        s = s_all[:, g * N_MEM:(g + 1) * N_MEM]
        e = jnp.exp(s - jnp.max(s, axis=-1, keepdims=True))
        ps.append(e / jnp.sum(e, axis=-1, keepdims=True))
    p_all = jnp.concatenate(ps, axis=1)
    pb = p_all.astype(BF16)
    return dict(qb=qb, p=p_all, pb=pb, c=_dot(pb, vm_ref[...]))


def _mixer_fwd_call(x, w, ts, exch=None):
    seq = x.shape[0]
    n = seq // ts

    def body(x_ref, win_ref, cw_ref, va_ref, wcat_ref, bfull_ref, kt_ref, vm_ref, wout_ref, v1_ref,
             hb_ref, z1_ref, x1_ref, a1buf, cat_ref, p_ref, gs_ref, rv_ref, buf, sh):
        i = pl.program_id(0)

        @pl.when(i == 0)
        def _():
            buf[0:HALO, :] = jnp.zeros((HALO, CONV_W), F32)

        @pl.when(i > 0)
        def _():
            buf[0:HALO, :] = buf[ts:ts + HALO, :]

        xv = x_ref[...]
        hb = _dot(xv.astype(BF16), win_ref[...]).astype(BF16)
        hb_ref[...] = hb
        hf = hb.astype(F32)
        (a_lo, a_hi), (b_lo, b_hi), (c_lo, c_hi) = GROUP_LANES
        ga = _group_a_fwd(hf[:, a_lo:a_hi], buf, sh, a1buf, cw_ref, va_ref, ts)
        gb = _group_b_fwd(hf[:, b_lo:b_hi], va_ref, wcat_ref, bfull_ref, ts)
        gc = _group_c_fwd(hb[:, c_lo:c_hi], kt_ref, vm_ref)
        cat = jnp.concatenate([ga["a"], gb["g"], gc["c"]], axis=1).astype(BF16)
        cat_ref[...] = cat
        p_ref[...] = gc["pb"]
        gs_ref[...] = jnp.concatenate([gb[k] for k in GROUP_B_SAVED], axis=1).astype(BF16)
        rv_ref[...] = jnp.broadcast_to(gb["rv"], (ts, 128))
        z1 = ALPHA * xv + _dot(cat, wout_ref[...])
        z1_ref[...] = z1
        xh, _ = _ln_stats(z1)
        x1_ref[...] = xh * v1_ref[0:1, :] + v1_ref[1:2, :]

    row = lambda width: pl.BlockSpec((ts, width), lambda i: (i, 0))
    return _grid_call(
        body, "mixer_fwd", n,
        in_specs=[row(D_MODEL), _const((D_MODEL, IN_W)), _const((HALO, CONV_W)), _const((8, CONV_W)),
                  _const((CHUNK, GMLP_HEADS * CHUNK)), _const((CHUNK, GMLP_W)), _const((XATTN_W, XATTN_HEADS * N_MEM)),
                  _const((XATTN_HEADS * N_MEM, XATTN_W)), _const((D_MODEL, D_MODEL)), _const((8, D_MODEL))],
        out_specs=[row(IN_W), row(D_MODEL), row(D_MODEL), row(CONV_W), row(D_MODEL), row(XATTN_HEADS * N_MEM),
                   row(len(GROUP_B_SAVED) * GMLP_W), row(128)],
        out_shape=[jax.ShapeDtypeStruct((seq, IN_W), BF16), jax.ShapeDtypeStruct((seq, D_MODEL), F32),
                   jax.ShapeDtypeStruct((seq, D_MODEL), F32), jax.ShapeDtypeStruct((seq, CONV_W), F32),
                   jax.ShapeDtypeStruct((seq, D_MODEL), BF16), jax.ShapeDtypeStruct((seq, XATTN_HEADS * N_MEM), BF16),
                   jax.ShapeDtypeStruct((seq, len(GROUP_B_SAVED) * GMLP_W), BF16),
                   jax.ShapeDtypeStruct((seq, 128), F32)],
        scratch_shapes=[pltpu.VMEM((ts + HALO, CONV_W), F32), pltpu.VMEM((7, ts + HALO, CONV_W), F32)],
        args=(x, w["win"], w["cw"], w["va"], w["wcat"], w["bfull"], w["kt"], w["vm"], w["wout"], w["v1"]), exch=exch)


def _store_blocks(acc, slabs_ref, sems, row_off, rows, first_block, n_blocks):
    copies = [pltpu.make_async_copy(acc.at[pl.ds(q * rows, rows)], slabs_ref.at[first_block + q, pl.ds(row_off, rows)],
                                    sems.at[q]) for q in range(n_blocks)]
    for cp in copies:
        cp.start()
    for cp in copies:
        cp.wait()


def _mixer_bwd_call(dx1, z1, hb, a1, cat, p, gs, rv, x, w, ts, slabs, off_out, off_in):
    seq = dx1.shape[0]
    n = seq // ts
    halo_blocks = ts // HALO

    def body(slabs_in, dx1_ref, z1_ref, hb_ref, hprev_ref, a1_ref, cat_ref, p_ref, gs_ref, rv_ref, x_ref, cw_ref, va_ref,
             wcatt_ref, ktt_ref, vmt_ref, woutt_ref, wint_ref, v1_ref,
             slabs_ref, dx_ref, dkt_ref, dvm_ref, dwcat_ref, dmsum_ref, dva_ref, dcw_ref, dv1_ref,
             buf, dbuf, da0buf, dwout_ref, dwin_ref, sems, sh, dsh, dcw_acc):
        i = pl.program_id(0)

        @pl.when(i == 0)
        def _():
            for ref in (dwout_ref, dwin_ref, dkt_ref, dvm_ref, dwcat_ref, dmsum_ref, dva_ref, dcw_ref, dv1_ref, dcw_acc):
                ref[...] = jnp.zeros(ref.shape, F32)
            dbuf[ts:ts + HALO, :] = jnp.zeros((HALO, CONV_W), F32)

        @pl.when(i > 0)
        def _():
            dbuf[ts:ts + HALO, :] = dbuf[0:HALO, :]

        dx1v = dx1_ref[...]
        xh1, r1 = _ln_stats(z1_ref[...])
        dv1_ref[0:1, :] += _rowsum(dx1v * xh1)
        dv1_ref[1:2, :] += _rowsum(dx1v)
        dz1 = _ln_bwd(dx1v, xh1, r1, v1_ref[0:1, :])
        dmix = dz1.astype(BF16)

        hf = hb_ref[:, 0:2 * CONV_W].astype(F32)
        hp = hprev_ref[...].astype(F32)
        a0p = hp[:, 0:CONV_W] * _sigmoid(hp[:, CONV_W:2 * CONV_W])
        buf[0:HALO, :] = jnp.where(i == n - 1, 0.0, a0p)
        ga = _group_a_fwd(hf, buf, sh, a1_ref, cw_ref, va_ref, ts, conv=False)
        gb = {k: gs_ref[:, j * GMLP_W:(j + 1) * GMLP_W].astype(F32) for j, k in enumerate(GROUP_B_SAVED)}
        vn = gb["vhat"] * va_ref[3:4, :] + va_ref[4:5, :]
        pb = p_ref[...]
        gc = dict(qb=hb_ref[:, IN_W - XATTN_W:IN_W], pb=pb, p=pb.astype(F32))

        dwout_ref[...] += _dot_tn(cat_ref[...], dmix)
        dcat = _dot(dmix, woutt_ref[...])
        da = dcat[:, 0:CONV_W]
        dg = dcat[:, CONV_W:CONV_W + GMLP_W]
        dc = dcat[:, CONV_W + GMLP_W:D_MODEL].astype(BF16)

        dp = _dot(dc, vmt_ref[...])
        dvm_ref[...] += _dot_tn(gc["pb"], dc)
        dss = []
        for g in range(XATTN_HEADS):
            sl = slice(g * N_MEM, (g + 1) * N_MEM)
            pg = gc["p"][:, sl]
            dpg = dp[:, sl]
            dss.append(pg * (dpg - jnp.sum(dpg * pg, axis=-1, keepdims=True)))
        ds = jnp.concatenate(dss, axis=1).astype(BF16)
        dq = _dot(ds, ktt_ref[...])
        dkt_ref[...] += _dot_tn(gc["qb"], ds)

        dmixed = dg * gb["u"]
        dhu = dg * gb["mixed"] * gb["du"]
        dvns = []
        for j, c0 in enumerate(range(0, ts, CHUNK)):
            dm = dmixed[c0:c0 + CHUNK, :]
            dmb = dm.astype(BF16)
            dmsum_ref[...] += dm
            dwcat_ref[...] += _dot_nt(dmb, _stack_heads(vn[c0:c0 + CHUNK, :]).astype(BF16))
            dst = _dot(wcatt_ref[...], dmb)
            dvn_c = jnp.zeros((CHUNK, GMLP_W), F32)
            for h in range(GMLP_HEADS):
                dvn_c = dvn_c + jnp.where(_head_mask(GMLP_W, h), dst[h * CHUNK:(h + 1) * CHUNK, :], 0.0)
            dvns.append(dvn_c)
        dvn = jnp.concatenate(dvns, axis=0) if len(dvns) > 1 else dvns[0]
        dva_ref[3:4, :] += _rowsum(dvn * gb["vhat"])
        dva_ref[4:5, :] += _rowsum(dvn)
        dhv = _ln_bwd(dvn, gb["vhat"], rv_ref[:, 0:1], va_ref[3:4, :]) * gb["dv"]

        a2, sa = ga["a2"], ga["sa"]
        da2 = da * (sa * (1.0 + a2 * (1.0 - sa)))
        dva_ref[1:2, :] += _rowsum(da2 * ga["a2h"])
        dva_ref[2:3, :] += _rowsum(da2)
        da1 = _ln_bwd(da2, ga["a2h"], ga["ra"], va_ref[1:2, :])
        dva_ref[0:1, :] += _rowsum(da1)
        dbuf[0:ts, :] = da1
        _shift_copies(dbuf, dsh, ts + HALO - 8)
        _conv31_dw(buf, sh, dbuf, dcw_acc, ts)
        _conv31_dx(dbuf, dsh, cw_ref, da0buf, ts)
        da0 = da0buf[...]
        sg = ga["sg"]
        dha = da0 * sg
        dhg = da0 * ga["ha"] * sg * (1.0 - sg)

        dh = jnp.concatenate([dha, dhg, dhu, dhv, dq], axis=1).astype(BF16)
        dx_ref[...] = _dot(dh, wint_ref[...]) + ALPHA * dz1
        dwin_ref[...] += _dot_tn(dh, x_ref[...].astype(BF16))

        @pl.when(i == n - 1)
        def _():
            for k in range(CONV_K):
                dcw_ref[k:k + 1, :] = _rowsum(dcw_acc[k])
            _store_blocks(dwout_ref, slabs_ref, sems, off_out, D_MODEL // N_DEV, 0, N_DEV)
            _store_blocks(dwin_ref, slabs_ref, sems, off_in, IN_W // N_DEV, 0, N_DEV)

    rev = lambda width: pl.BlockSpec((ts, width), lambda i: (n - 1 - i, 0))
    prev = pl.BlockSpec((HALO, 2 * CONV_W), lambda i: (jnp.maximum((n - 1 - i) * halo_blocks - 1, 0), 0))
    hbm = pl.BlockSpec(memory_space=pl.ANY)
    hc = GMLP_HEADS * CHUNK
    am = XATTN_HEADS * N_MEM
    return pl.pallas_call(
        body, name="mixer_bwd", grid=(n,),
        in_specs=[hbm, rev(D_MODEL), rev(D_MODEL), rev(IN_W), prev, rev(CONV_W), rev(D_MODEL), rev(am),
                  rev(len(GROUP_B_SAVED) * GMLP_W), rev(128), rev(D_MODEL), _const((HALO, CONV_W)),
                  _const((8, CONV_W)), _const((hc, CHUNK)), _const((am, XATTN_W)), _const((XATTN_W, am)),
                  _const((D_MODEL, D_MODEL)), _const((IN_W, D_MODEL)), _const((8, D_MODEL))],
        out_specs=[hbm, rev(D_MODEL), _acc((XATTN_W, am)), _acc((am, XATTN_W)),
                   _acc((CHUNK, hc)), _acc((CHUNK, GMLP_W)), _acc((8, CONV_W)), _acc((HALO, CONV_W)),
                   _acc((8, D_MODEL))],
        out_shape=[jax.ShapeDtypeStruct(slabs.shape, F32), jax.ShapeDtypeStruct((seq, D_MODEL), F32),
                   jax.ShapeDtypeStruct((XATTN_W, am), F32),
                   jax.ShapeDtypeStruct((am, XATTN_W), F32), jax.ShapeDtypeStruct((CHUNK, hc), F32),
                   jax.ShapeDtypeStruct((CHUNK, GMLP_W), F32), jax.ShapeDtypeStruct((8, CONV_W), F32),
                   jax.ShapeDtypeStruct((HALO, CONV_W), F32), jax.ShapeDtypeStruct((8, D_MODEL), F32)],
        scratch_shapes=[pltpu.VMEM((ts + HALO, CONV_W), F32),
                        pltpu.VMEM((ts + HALO, CONV_W), F32), pltpu.VMEM((ts, CONV_W), F32),
                        pltpu.VMEM((D_MODEL, D_MODEL), F32), pltpu.VMEM((IN_W, D_MODEL), F32),
                        pltpu.SemaphoreType.DMA((N_DEV,)),
                        pltpu.VMEM((7, ts + HALO, CONV_W), F32), pltpu.VMEM((7, ts + HALO, CONV_W), F32),
                        pltpu.VMEM((CONV_K, 8, CONV_W), F32)],
        input_output_aliases={0: 0},
        compiler_params=_params(dimension_semantics=("arbitrary",)),
    )(slabs, dx1, z1, hb, hb, a1, cat, p, gs, rv, x, w["cw"], w["va"], w["wcatt"], w["ktt"], w["vmt"], w["woutt"],
      w["wint"], w["v1"])


FFN_HALO = 8
FF_GROUP = D_FF // N_DEV
FF_GROUP_PAD = D_FF_PAD // N_DEV


def _ffn_taps(ubuf, ts, lo, hi):
    return tuple(ubuf[FFN_HALO - (FFN_CONV_K - 1) + k:FFN_HALO - (FFN_CONV_K - 1) + k + ts, lo:hi]
                 for k in range(FFN_CONV_K))


def _ffn_gate(ubuf, cf_ref, ts, lo, hi):
    taps = _ffn_taps(ubuf, ts, lo, hi)
    g = cf_ref[3:4, lo:hi] + cf_ref[2:3, lo:hi] * taps[2]
    g = g + cf_ref[1:2, lo:hi] * taps[1]
    return g + cf_ref[0:1, lo:hi] * taps[0]


FFN_CHUNK = 256
FFN_CHUNKS = tuple((lo, lo + FFN_CHUNK) for lo in range(0, D_FF_PAD, FFN_CHUNK))


def _ffn_fwd_call(x1, w, ts, exch=None, want_x2=True):
    seq = x1.shape[0]
    n = seq // ts
    n_wide = 4

    def body(x1_ref, wg_ref, wv_ref, cf_ref, wdown_ref, v2_ref, ug_ref, uv_ref, sl_ref, dsl_ref, z2_ref, *rest):
        x2_ref = rest[0] if want_x2 else None
        ubuf, act_buf = rest[-2:]
        i = pl.program_id(0)

        @pl.when(i == 0)
        def _():
            ubuf[0:FFN_HALO, :] = jnp.zeros((FFN_HALO, D_FF_PAD), F32)

        @pl.when(i > 0)
        def _():
            ubuf[0:FFN_HALO, :] = ubuf[ts:ts + FFN_HALO, :]

        xv = x1_ref[...]
        xb = xv.astype(BF16)
        for lo, hi in FFN_CHUNKS:
            ug = _dot(xb, wg_ref[:, lo:hi]).astype(BF16)
            uv = _dot(xb, wv_ref[:, lo:hi]).astype(BF16)
            ug_ref[:, lo:hi] = ug
            uv_ref[:, lo:hi] = uv
            ubuf[FFN_HALO:FFN_HALO + ts, lo:hi] = ug.astype(F32)
            gate = _ffn_gate(ubuf, cf_ref, ts, lo, hi)
            sg = _sigmoid(gate)
            sl = gate * sg
            sl_ref[:, lo:hi] = sl.astype(BF16)
            dsl_ref[:, lo:hi] = (sg * (1.0 + gate * (1.0 - sg))).astype(BF16)
            act_buf[:, lo:hi] = (sl * uv.astype(F32)).astype(BF16)
        y = ALPHA * xv + _dot(act_buf[...], wdown_ref[...])
        z2_ref[...] = y
        if want_x2:
            xh, _ = _ln_stats(y)
            x2_ref[...] = xh * v2_ref[0:1, :] + v2_ref[1:2, :]

    row = lambda width: pl.BlockSpec((ts, width), lambda i: (i, 0))
    n_narrow = 2 if want_x2 else 1
    return _grid_call(
        body, "ffn_fwd" if want_x2 else "ffn_fwd_last", n,
        in_specs=[row(D_MODEL), _const((D_MODEL, D_FF_PAD)), _const((D_MODEL, D_FF_PAD)), _const((8, D_FF_PAD)),
                  _const((D_FF_PAD, D_MODEL)), _const((8, D_MODEL))],
        out_specs=[row(D_FF_PAD)] * n_wide + [row(D_MODEL)] * n_narrow,
        out_shape=[jax.ShapeDtypeStruct((seq, D_FF_PAD), BF16)] * n_wide
                  + [jax.ShapeDtypeStruct((seq, D_MODEL), F32)] * n_narrow,
        scratch_shapes=[pltpu.VMEM((ts + FFN_HALO, D_FF_PAD), F32), pltpu.VMEM((ts, D_FF_PAD), BF16)],
        args=(x1, w["wg"], w["wv"], w["cf"], w["wdown"], w["v2"]), exch=exch)


def _ffn_bwd_call(dx2_or_target, z2, ug, uv, sl, dsl, w, ts, last, slabs, row_off, exch=None):
    seq = z2.shape[0]
    n = seq // ts
    halo_blocks = ts // 16

    def body(slabs_in, dx2_ref, z2_ref, ug_ref, uv_ref, sl_ref, dsl_ref, uprev_ref, cf_ref, wdownt_ref, v2_ref,
             slabs_ref, dug_ref, duv_ref, dz2_ref, dcf_ref, dv2_ref, loss_ref,
             ubuf, dgbuf, dwacc, sems):
        i = pl.program_id(0)

        @pl.when(i == 0)
        def _():
            dwacc[...] = jnp.zeros(dwacc.shape, F32)
            dcf_ref[...] = jnp.zeros(dcf_ref.shape, F32)
            dv2_ref[...] = jnp.zeros(dv2_ref.shape, F32)
            loss_ref[...] = jnp.zeros(loss_ref.shape, F32)
            dgbuf[ts:ts + FFN_HALO, :] = jnp.zeros((FFN_HALO, D_FF_PAD), F32)

        @pl.when(i > 0)
        def _():
            dgbuf[ts:ts + FFN_HALO, :] = dgbuf[0:FFN_HALO, :]

        xh2, r2 = _ln_stats(z2_ref[...])
        if last:
            diff = xh2 * v2_ref[0:1, :] + v2_ref[1:2, :] - dx2_ref[...]
            loss_ref[...] += jnp.sum(diff * diff) * (0.5 / D_MODEL)
            dx2v = diff * (1.0 / D_MODEL)
        else:
            dx2v = dx2_ref[...]
        dv2_ref[0:1, :] += _rowsum(dx2v * xh2)
        dv2_ref[1:2, :] += _rowsum(dx2v)
        dz2 = _ln_bwd(dx2v, xh2, r2, v2_ref[0:1, :])
        dz2_ref[...] = dz2
        dy = dz2.astype(BF16)

        up = uprev_ref[...].astype(F32)[8:16, :]
        ubuf[0:FFN_HALO, :] = jnp.where(i == n - 1, 0.0, up)
        ubuf[FFN_HALO:FFN_HALO + ts, :] = ug_ref[...].astype(F32)
        for lo, hi in FFN_CHUNKS:
            taps = _ffn_taps(ubuf, ts, lo, hi)
            sl = sl_ref[:, lo:hi].astype(F32)
            uvf = uv_ref[:, lo:hi].astype(F32)
            act = (sl * uvf).astype(BF16)
            dwacc[lo:hi, :] += _dot_tn(act, dy)
            dact = _dot(dy, wdownt_ref[:, lo:hi])
            duv_ref[:, lo:hi] = (dact * sl).astype(BF16)
            dgate = dact * uvf * dsl_ref[:, lo:hi].astype(F32)
            dgbuf[0:ts, lo:hi] = dgate
            dcf_ref[3:4, lo:hi] += _rowsum(dgate)
            for k in range(FFN_CONV_K):
                dcf_ref[k:k + 1, lo:hi] += _rowsum(dgate * taps[k])
            dug = cf_ref[2:3, lo:hi] * dgate + cf_ref[1:2, lo:hi] * dgbuf[1:1 + ts, lo:hi]
            dug = dug + cf_ref[0:1, lo:hi] * dgbuf[2:2 + ts, lo:hi]
            dug_ref[:, lo:hi] = dug.astype(BF16)

        @pl.when(i == n - 1)
        def _():
            _store_blocks(dwacc, slabs_ref, sems, row_off, D_FF_PAD // N_DEV, 0, N_DEV)

    rev = lambda width: pl.BlockSpec((ts, width), lambda i: (n - 1 - i, 0))
    prev = pl.BlockSpec((16, D_FF_PAD), lambda i: (jnp.maximum((n - 1 - i) * halo_blocks - 1, 0), 0))
    hbm = pl.BlockSpec(memory_space=pl.ANY)
    return _grid_call(
        body, "ffn_bwd_last" if last else "ffn_bwd", n,
        in_specs=[hbm, rev(D_MODEL), rev(D_MODEL)] + [rev(D_FF_PAD)] * 4 + [prev, _const((8, D_FF_PAD)),
                                                                           _const((D_MODEL, D_FF_PAD)), _const((8, D_MODEL))],
        out_specs=[hbm, rev(D_FF_PAD), rev(D_FF_PAD), rev(D_MODEL),
                   _acc((8, D_FF_PAD)), _acc((8, D_MODEL)), _acc((8, 128))],
        out_shape=[jax.ShapeDtypeStruct(slabs.shape, F32),
                   jax.ShapeDtypeStruct((seq, D_FF_PAD), BF16), jax.ShapeDtypeStruct((seq, D_FF_PAD), BF16),
                   jax.ShapeDtypeStruct((seq, D_MODEL), F32),
                   jax.ShapeDtypeStruct((8, D_FF_PAD), F32), jax.ShapeDtypeStruct((8, D_MODEL), F32),
                   jax.ShapeDtypeStruct((8, 128), F32)],
        scratch_shapes=[pltpu.VMEM((ts + FFN_HALO, D_FF_PAD), F32), pltpu.VMEM((ts + FFN_HALO, D_FF_PAD), F32),
                        pltpu.VMEM((D_FF_PAD, D_MODEL), F32), pltpu.SemaphoreType.DMA((N_DEV,))],
        args=(slabs, dx2_or_target, z2, ug, uv, sl, dsl, ug, w["cf"], w["wdownt"], w["v2"]), aliases={0: 0}, exch=exch)


def _proj_bwd_call(d, wt, xin, addend, scale, ts, name, slabs, row_off, first_block, n_blocks, exch=None):
    seq, k = d.shape
    n = seq // ts

    def body(slabs_in, d_ref, wt_ref, xin_ref, add_ref, slabs_ref, dx_ref, acc, sems):
        i = pl.program_id(0)

        @pl.when(i == 0)
        def _():
            acc[...] = jnp.zeros(acc.shape, F32)

        dv = d_ref[...]
        dx_ref[...] = _dot(dv, wt_ref[...]) + scale * add_ref[...]
        acc[...] += _dot_tn(dv, xin_ref[...].astype(BF16))

        @pl.when(i == n - 1)
        def _():
            _store_blocks(acc, slabs_ref, sems, row_off, k // n_blocks, first_block, n_blocks)

    row = lambda width: pl.BlockSpec((ts, width), lambda i: (i, 0))
    hbm = pl.BlockSpec(memory_space=pl.ANY)
    return _grid_call(
        body, name, n,
        in_specs=[hbm, row(k), _const((k, D_MODEL)), row(D_MODEL), row(D_MODEL)],
        out_specs=[hbm, row(D_MODEL)],
        out_shape=[jax.ShapeDtypeStruct(slabs.shape, F32), jax.ShapeDtypeStruct((seq, D_MODEL), F32)],
        scratch_shapes=[pltpu.VMEM((k, D_MODEL), F32), pltpu.SemaphoreType.DMA((n_blocks,))],
        args=(slabs, d, wt, xin, addend), aliases={0: 0}, exch=exch)


MEM_FOLD = BLOB_LANES // XATTN_W


def _mem_proj_call(memq, wk_flat, wv_flat):
    def body(memq_ref, wk_ref, wv_ref, kh_ref, vh_ref):
        for w_ref, o_ref in ((wk_ref, kh_ref), (wv_ref, vh_ref)):
            acc = jnp.zeros((N_MEM, XATTN_W), F32)
            for q in range(MEM_FOLD):
                acc = acc + _dot(memq_ref[q], w_ref[:, q * XATTN_W:(q + 1) * XATTN_W])
            o_ref[...] = acc

    out = jax.ShapeDtypeStruct((N_MEM, XATTN_W), F32)
    return pl.pallas_call(body, name="mem_proj", out_shape=[out, out], compiler_params=_params())(memq, wk_flat, wv_flat)


def _mem_proj_bwd_call(memq, dkh, dvh, slabs, off_k, off_v):
    rows = D_MODEL // MEM_FOLD

    def body(slabs_in, memq_ref, dkh_ref, dvh_ref, slabs_ref, acc, sems):
        for d_ref, off in ((dkh_ref, off_k), (dvh_ref, off_v)):
            dv = d_ref[...].astype(BF16)
            for q in range(MEM_FOLD):
                acc[:, q * XATTN_W:(q + 1) * XATTN_W] = _dot_tn(memq_ref[q], dv)
            _store_blocks(acc, slabs_ref, sems, off, rows // N_DEV, 0, N_DEV)

    hbm = pl.BlockSpec(memory_space=pl.ANY)
    vmem = pl.BlockSpec(memory_space=pltpu.VMEM)
    return pl.pallas_call(
        body, name="mem_proj_bwd", in_specs=[hbm, vmem, vmem, vmem], out_specs=hbm,
        out_shape=jax.ShapeDtypeStruct(slabs.shape, F32),
        scratch_shapes=[pltpu.VMEM((rows, BLOB_LANES), F32), pltpu.SemaphoreType.DMA((N_DEV,))],
        input_output_aliases={0: 0}, compiler_params=_params(),
    )(slabs, memq, dkh, dvh)


def _place():
    return lax.axis_index("x"), lax.axis_index("y"), lax.axis_index("c")


def _all_gather_call(arrs, pieces, name, swap=None):
    n_in, n_p = len(arrs), len(pieces)
    n_sw = 0 if swap is None else 1

    def body(*refs):
        ins, outs = refs[:n_in], refs[n_in + n_sw:n_in + n_sw + n_p]
        send_sems, recv_sems, local_sems = refs[n_in + 2 * n_sw + n_p:n_in + 2 * n_sw + n_p + 3]
        swapped = []
        if swap is not None:
            swapped = swap[1](refs[n_in], refs[n_in + n_sw + n_p], *refs[-2:])
            for cp in swapped:
                cp.start()
        x, y, c = _place()
        me, sibling = (x, y, c), (x, y, 1 - c)
        chips = [(1 - x, y), (x, 1 - y), (1 - x, 1 - y)]

        def src(a):
            idx, r0, rows = pieces[a]
            return ins[idx] if r0 is None else ins[idx].at[pl.ds(r0, rows)]

        def slab(a, p):
            return outs[a].at[4 * p[0] + 2 * p[1] + p[2]]

        def copy(a, k, block, to, own=False):
            return pltpu.make_async_remote_copy(
                src_ref=src(a) if own else slab(a, block), dst_ref=slab(a, block),
                send_sem=send_sems.at[a, k], recv_sem=recv_sems.at[a, k], device_id=to, device_id_type=MESH)

        mine = [pltpu.make_async_copy(src(a), slab(a, me), local_sems.at[a]) for a in range(n_p)]
        for cp in mine:
            cp.start()
        first = []
        for a in range(n_p):
            first.append(copy(a, 0, me, sibling, own=True))
            first += [copy(a, 1 + j, me, (*chip, c), own=True) for j, chip in enumerate(chips)]
        for cp in first:
            cp.start()
        passed = []
        for a in range(n_p):
            for j, chip in enumerate(chips):
                copy(a, 1 + j, (*chip, c), me).wait_recv()
                cp = copy(a, 4 + j, (*chip, c), sibling)
                cp.start()
                passed.append(cp)
        for a in range(n_p):
            copy(a, 0, sibling, me).wait_recv()
            for j, chip in enumerate(chips):
                copy(a, 4 + j, (*chip, 1 - c), me).wait_recv()
        for cp in first + passed:
            cp.wait_send()
        for cp in mine:
            cp.wait()
        for cp in swapped:
            cp.wait()

    def out_shape(piece):
        idx, r0, rows = piece
        a = arrs[idx]
        return jax.ShapeDtypeStruct((N_DEV,) + (a.shape if r0 is None else (rows,) + a.shape[1:]), a.dtype)

    hbm = pl.BlockSpec(memory_space=pl.ANY)
    out_shapes = [out_shape(p) for p in pieces]
    scratch = [pltpu.SemaphoreType.DMA((n_p, 7)), pltpu.SemaphoreType.DMA((n_p, 7)), pltpu.SemaphoreType.DMA((n_p,))]
    args = list(arrs)
    if swap is not None:
        a, _, n = swap
        args.append(a)
        out_shapes.append(jax.ShapeDtypeStruct((n,) + a.shape[1:], a.dtype))
        scratch += [pltpu.SemaphoreType.DMA((n,)), pltpu.SemaphoreType.DMA((n,))]
    return pl.pallas_call(
        body, name=name, in_specs=[hbm] * len(args), out_specs=[hbm] * len(out_shapes), out_shape=out_shapes,
        scratch_shapes=scratch,
    )(*args)


def _flip(v, f):
    return 1 - v if f else v


def _gather_exchange(arrs, pieces):
    n_peers = N_DEV - 1

    def build(ins, outs, send_sems, recv_sems, local_sems):
        x, y, c = _place()
        flips = [(fx, fy, fc) for fx in (0, 1) for fy in (0, 1) for fc in (0, 1) if fx or fy or fc]
        remote, local = [], []
        for a, (idx, r0, rows) in enumerate(pieces):
            src = ins[idx] if r0 is None else ins[idx].at[pl.ds(r0, rows)]
            dst = outs[a].at[4 * x + 2 * y + c]
            remote += [pltpu.make_async_remote_copy(
                src_ref=src, dst_ref=dst, send_sem=send_sems.at[n_peers * a + k], recv_sem=recv_sems.at[n_peers * a + k],
                device_id=(_flip(x, fx), _flip(y, fy), _flip(c, fc)), device_id_type=MESH)
                for k, (fx, fy, fc) in enumerate(flips)]
            local.append(pltpu.make_async_copy(src, dst, local_sems.at[a]))
        return remote, local

    def out_shape(piece):
        idx, r0, rows = piece
        a = arrs[idx]
        return jax.ShapeDtypeStruct((N_DEV,) + (a.shape if r0 is None else (rows,) + a.shape[1:]), a.dtype)

    return _Exchange(arrs, [out_shape(p) for p in pieces], n_peers * len(pieces), build, n_local=len(pieces))


def _swap_core_copies(g_ref, r_ref, send_sems, recv_sems):
    x, y, c = _place()
    return [pltpu.make_async_remote_copy(
        src_ref=g_ref.at[2 * k + (1 - c)], dst_ref=r_ref.at[k], send_sem=send_sems.at[k], recv_sem=recv_sems.at[k],
        device_id=(x, y, 1 - c), device_id_type=MESH) for k in range(4)]


def _swap_chip_copies(p_ref, r_ref, send_sems, recv_sems):
    x, y, c = _place()
    chips = [(1 - x, y), (x, 1 - y), (1 - x, 1 - y)]
    return [pltpu.make_async_remote_copy(
        src_ref=p_ref.at[2 * px + py], dst_ref=r_ref.at[j], send_sem=send_sems.at[j], recv_sem=recv_sems.at[j],
        device_id=(px, py, c), device_id_type=MESH) for j, (px, py) in enumerate(chips)]


def _swap_exchange(a, copies, n):
    return _Exchange([a], [jax.ShapeDtypeStruct((n,) + a.shape[1:], a.dtype)], n,
                     lambda ins, outs, send_sems, recv_sems, local_sems: (copies(ins[0], outs[0], send_sems, recv_sems), []))


def _swap_call(a, copies, n, name):
    def body(a_ref, r_ref, send_sems, recv_sems):
        cps = copies(a_ref, r_ref, send_sems, recv_sems)
        for cp in cps:
            cp.start()
        for cp in cps:
            cp.wait()

    hbm = pl.BlockSpec(memory_space=pl.ANY)
    return pl.pallas_call(
        body, name=name, in_specs=[hbm], out_specs=hbm, out_shape=jax.ShapeDtypeStruct((n,) + a.shape[1:], a.dtype),
        scratch_shapes=[pltpu.SemaphoreType.DMA((n,)), pltpu.SemaphoreType.DMA((n,))],
    )(a)


def _swap_chip_start(p, rider):
    n = 3
    land = lax.empty((n,) + p.shape[1:], p.dtype)

    def body(p_ref, land_ref, rider_ref, send_sems, recv_sems, p_thru, land_thru, rider_thru):
        for cp in _swap_chip_copies(p_ref, land_ref, send_sems, recv_sems):
            cp.start()

    hbm, sem = pl.BlockSpec(memory_space=pltpu.HBM), pl.BlockSpec(memory_space=pltpu.SEMAPHORE)
    return pl.pallas_call(
        body, name="rs_swap_chip_start",
        out_shape=(pltpu.SemaphoreType.DMA((n,)), pltpu.SemaphoreType.DMA((n,)), pltpu.HBM(p.shape, p.dtype),
                   pltpu.HBM(land.shape, land.dtype), pltpu.HBM(rider.shape, rider.dtype)),
        in_specs=(hbm, hbm, hbm), out_specs=(sem, sem, hbm, hbm, hbm), input_output_aliases={0: 2, 1: 3, 2: 4},
        compiler_params=pltpu.CompilerParams(has_side_effects=pltpu.SideEffectType.DATAFLOW_SIDE_EFFECTING),
    )(pltpu.with_memory_space_constraint(p, pltpu.HBM), pltpu.with_memory_space_constraint(land, pltpu.HBM),
      pltpu.with_memory_space_constraint(rider, pltpu.HBM))


def _swap_chip_wait(send_sems, recv_sems, p_thru, land_thru, *after):
    def body(p_ref, land_ref, send_sems, recv_sems, *rest):
        for cp in _swap_chip_copies(p_ref, land_ref, send_sems, recv_sems):
            cp.wait_send()
            cp.wait_recv()

    hbm, sem = pl.BlockSpec(memory_space=pltpu.HBM), pl.BlockSpec(memory_space=pltpu.SEMAPHORE)
    return pl.pallas_call(
        body, name="rs_swap_chip_wait",
        out_shape=(pltpu.HBM(p_thru.shape, p_thru.dtype), pltpu.HBM(land_thru.shape, land_thru.dtype)),
        in_specs=(hbm, hbm, sem, sem) + (pl.BlockSpec(memory_space=pl.ANY),) * len(after), out_specs=(hbm, hbm),
        input_output_aliases={0: 0, 1: 1},
        compiler_params=pltpu.CompilerParams(has_side_effects=pltpu.SideEffectType.DATAFLOW_SIDE_EFFECTING),
    )(p_thru, land_thru, send_sems, recv_sems, *after)[1]


ADD_ROWS = 184


def _pair_add_call(g, r, c, chip):
    _, rows, width = g.shape

    def body(c_ref, chip_ref, g_ref, r_ref, o_ref):
        o_ref[...] = (g_ref[...] + r_ref[...]).astype(BF16)

    def other(j, chip_ref):
        return j + (j >= chip_ref[0]).astype(jnp.int32)

    return pl.pallas_call(
        body, name="rs_pair_add",
        grid_spec=pltpu.PrefetchScalarGridSpec(
            num_scalar_prefetch=2, grid=(3, rows // ADD_ROWS),
            in_specs=[pl.BlockSpec((None, ADD_ROWS, width),
                                   lambda j, i, c_ref, chip_ref: (2 * other(j, chip_ref) + c_ref[0], i, 0)),
                      pl.BlockSpec((None, ADD_ROWS, width), lambda j, i, c_ref, chip_ref: (other(j, chip_ref), i, 0))],
            out_specs=pl.BlockSpec((None, ADD_ROWS, width), lambda j, i, c_ref, chip_ref: (other(j, chip_ref), i, 0))),
        out_shape=jax.ShapeDtypeStruct((4, rows, width), BF16),
        compiler_params=_params(dimension_semantics=("arbitrary", "arbitrary")),
    )(c, chip, g, r)


def _adam(w, g, m, v):
    mn = ADAM_B1 * m + (1.0 - ADAM_B1) * g
    vn = ADAM_B2 * v + (1.0 - ADAM_B2) * (g * g)
    m_hat = mn / (1.0 - ADAM_B1 ** ADAM_STEP)
    v_hat = vn / (1.0 - ADAM_B2 ** ADAM_STEP)
    return -ADAM_LR * (m_hat / (jnp.sqrt(v_hat) + ADAM_EPS) + ADAM_WD * w), mn, vn


def _chip_add_adamw_call(slabs, from_sibling, from_chips, me, chip, w, m, v):
    _, rows, width = slabs.shape

    def body(me_ref, chip_ref, own_ref, sib_ref, r_ref, w_ref, m_ref, v_ref, g_ref, d_ref, mo_ref, vo_ref):
        g = own_ref[...] + sib_ref[...]
        for j in range(3):
            g = g + r_ref[j].astype(F32)
        g_ref[...] = g
        d_ref[...], mo_ref[...], vo_ref[...] = _adam(w_ref[...], g, m_ref[...], v_ref[...])

    spec = pl.BlockSpec((ADD_ROWS, width), lambda i, me_ref, chip_ref: (i, 0))
    return pl.pallas_call(
        body, name="rs_chip_add_adamw",
        grid_spec=pltpu.PrefetchScalarGridSpec(
            num_scalar_prefetch=2, grid=(rows // ADD_ROWS,),
            in_specs=[pl.BlockSpec((None, ADD_ROWS, width), lambda i, me_ref, chip_ref: (me_ref[0], i, 0)),
                      pl.BlockSpec((None, ADD_ROWS, width), lambda i, me_ref, chip_ref: (chip_ref[0], i, 0)),
                      pl.BlockSpec((3, ADD_ROWS, width), lambda i, me_ref, chip_ref: (0, i, 0)), spec, spec, spec],
            out_specs=[spec] * 4),
        out_shape=[jax.ShapeDtypeStruct((rows, width), F32)] * 4,
        compiler_params=_params(dimension_semantics=("arbitrary",)),
    )(me, chip, slabs, from_sibling, from_chips, w, m, v)


def _adamw_whole_call(params, name):
    n = len(params)

    def body(*refs):
        ins, outs = refs[:4 * n], refs[4 * n:]
        for a in range(n):
            w_ref, g_ref, m_ref, v_ref = ins[4 * a:4 * a + 4]
            outs[3 * a][...], outs[3 * a + 1][...], outs[3 * a + 2][...] = _adam(w_ref[...], g_ref[...], m_ref[...],
                                                                              v_ref[...])

    flat = [a for p in params for a in p]
    shapes = [jax.ShapeDtypeStruct(p[0].shape, F32) for p in params for _ in range(3)]
    out = pl.pallas_call(body, name=name, out_shape=shapes, compiler_params=_params())(*flat)
    return [tuple(out[3 * a:3 * a + 3]) for a in range(n)]


GATHERED_ACCS = (("dva", (8, CONV_W)), ("dv1", (8, D_MODEL)), ("dv2", (8, D_MODEL)), ("dcf", (8, D_FF_PAD)),
                 ("dcw", (HALO, CONV_W)), ("dwcat", (CHUNK, GMLP_HEADS * CHUNK)), ("dmsum", (CHUNK, GMLP_W)))
VEC_A = ("conv_a_b", "ln_a_g", "ln_a_b", "ln_v_g", "ln_v_b")
REP_IN_KERNEL = VEC_A + ("ln1_g", "ln1_b", "ln2_g", "ln2_b", "w_s", "b_s")


def _replicated_update_call(gathered, p, mom_m, mom_v):
    n_acc = len(GATHERED_ACCS)
    n_rep = len(REP_IN_KERNEL)

    def body(*refs):
        acc_refs = refs[:DEPTH * n_acc]
        wmv = refs[DEPTH * n_acc:DEPTH * n_acc + 3 * n_rep]
        outs = refs[DEPTH * n_acc + 3 * n_rep:]
        out_par = {nm: outs[4 * a:4 * a + 4] for a, nm in enumerate(REP_IN_KERNEL)}
        out_dcf = outs[4 * n_rep:4 * n_rep + DEPTH]
        out_dcw = outs[4 * n_rep + DEPTH:4 * n_rep + 2 * DEPTH]
        par = {nm: wmv[3 * a:3 * a + 3] for a, nm in enumerate(REP_IN_KERNEL)}
        tril = (lax.broadcasted_iota(jnp.int32, (CHUNK, CHUNK), 0) >= lax.broadcasted_iota(jnp.int32, (CHUNK, CHUNK), 1))
        head = lax.broadcasted_iota(jnp.int32, (8, GMLP_W), 0) * HEAD_DIM
        lane = lax.broadcasted_iota(jnp.int32, (8, GMLP_W), 1)
        sel = jnp.where((lane >= head) & (lane < head + HEAD_DIM), 1.0, 0.0)

        def update(nm, idx, g):
            w_ref, m_ref, v_ref = par[nm]
            d, mn, vn = _adam(w_ref[idx], g, m_ref[idx], v_ref[idx])
            g_ref, d_ref, mo_ref, vo_ref = out_par[nm]
            g_ref[idx] = g
            d_ref[idx] = d
            mo_ref[idx] = mn
            vo_ref[idx] = vn

        for l in range(DEPTH):
            tot = {}
            for a, (nm, _) in enumerate(GATHERED_ACCS):
                ref = acc_refs[l * n_acc + a]
                s = ref[0]
                for j in range(1, N_DEV):
                    s = s + ref[j]
                tot[nm] = s
            out_dcf[l][...] = tot["dcf"]
            out_dcw[l][...] = tot["dcw"]
            row = (slice(l, l + 1), slice(None))
            for k, nm in enumerate(VEC_A):
                update(nm, row, tot["dva"][k:k + 1, :])
            update("ln1_g", row, tot["dv1"][0:1, :])
            update("ln1_b", row, tot["dv1"][1:2, :])
            update("ln2_g", row, tot["dv2"][0:1, :])
            update("ln2_b", row, tot["dv2"][1:2, :])
            for h in range(GMLP_HEADS):
                gw = jnp.where(tril, tot["dwcat"][:, h * CHUNK:(h + 1) * CHUNK], 0.0)
                update("w_s", (l, h), gw)
            gb = lax.dot_general(sel, tot["dmsum"], (((1,), (1,)), ((), ())), precision=lax.Precision.HIGHEST,
                                 preferred_element_type=F32)
            for h in range(GMLP_HEADS):
                update("b_s", (l, slice(h, h + 1), slice(None)), gb[h:h + 1, :])

    ins = [gathered[l][nm] for l in range(DEPTH) for nm, _ in GATHERED_ACCS]
    ins += [t[nm] for nm in REP_IN_KERNEL for t in (p, mom_m, mom_v)]
    shapes = [jax.ShapeDtypeStruct(p[nm].shape, F32) for nm in REP_IN_KERNEL for _ in range(4)]
    shapes += [jax.ShapeDtypeStruct((8, D_FF_PAD), F32)] * DEPTH + [jax.ShapeDtypeStruct((HALO, CONV_W), F32)] * DEPTH
    out = pl.pallas_call(body, name="replicated_update", out_shape=shapes, compiler_params=_params())(*ins)
    res = [{nm: out[4 * a + k] for a, nm in enumerate(REP_IN_KERNEL)} for k in range(4)]
    return res, out[4 * n_rep:4 * n_rep + DEPTH], out[4 * n_rep + DEPTH:]


BLOCK_ROWS = (("w_in", IN_W // N_DEV), ("w_out", D_MODEL // N_DEV), ("w_up", 2 * FF_GROUP_PAD),
              ("w_down", FF_GROUP_PAD), ("w_mk", D_MODEL // N_DEV // MEM_FOLD), ("w_mv", D_MODEL // N_DEV // MEM_FOLD))
LAYER_ROWS = sum(r for _, r in BLOCK_ROWS)
assert LAYER_ROWS % ADD_ROWS == 0 and all(r % 16 == 0 for _, r in BLOCK_ROWS)


def _row_off(name):
    off = 0
    for nm, r in BLOCK_ROWS:
        if nm == name:
            return off
        off += r
    raise KeyError(name)


def _to_rows(name, a):
    if name == "w_in":
        return a.T
    if name == "w_up":
        t = a.T.reshape(2, FF_GROUP, D_MODEL)
        return jnp.pad(t, ((0, 0), (0, FF_GROUP_PAD - FF_GROUP), (0, 0))).reshape(2 * FF_GROUP_PAD, D_MODEL)
    if name == "w_down":
        return jnp.pad(a, ((0, FF_GROUP_PAD - FF_GROUP), (0, 0)))
    if name == "w_out":
        return a
    return a.reshape(-1, BLOB_LANES)


def _from_rows(name, r):
    if name == "w_in":
        return r.T
    if name == "w_up":
        return r.reshape(2, FF_GROUP_PAD, D_MODEL)[:, :FF_GROUP].reshape(2 * FF_GROUP, D_MODEL).T
    if name == "w_down":
        return r[:FF_GROUP]
    if name == "w_out":
        return r
    return r.reshape(D_MODEL // N_DEV, XATTN_W)


def _blob(tree, l):
    return jnp.concatenate([_to_rows(nm, tree[nm][l]) for nm, _ in BLOCK_ROWS], axis=0)


def _unblob(blobs):
    return {nm: jnp.stack([_from_rows(nm, b[_row_off(nm):_row_off(nm) + r]) for b in blobs]) for nm, r in BLOCK_ROWS}


def _ff_interleave(a):
    lead = a.shape[:-1]
    t = a.reshape(lead + (N_DEV, FF_GROUP))
    return jnp.pad(t, [(0, 0)] * len(lead) + [(0, 0), (0, FF_GROUP_PAD - FF_GROUP)]).reshape(lead + (D_FF_PAD,))


def _ff_deinterleave(a):
    lead = a.shape[:-1]
    return a.reshape(lead + (N_DEV, FF_GROUP_PAD))[..., :FF_GROUP].reshape(lead + (D_FF,))


def _head_table():
    hd = jnp.arange(XATTN_W) // HEAD_DIM
    return (hd[None, :] == jnp.arange(XATTN_HEADS)[:, None]).astype(F32)


def _mixer_operands(mat, conv_a_w, p, l, memq):
    w = {}
    w["wint"] = mat["w_in"]
    w["win"] = mat["w_in"].T
    w["wout"] = mat["w_out"]
    w["woutt"] = mat["w_out"].T
    w["cw"] = conv_a_w
    zeros = jnp.zeros((3, CONV_W), F32)
    w["va"] = jnp.concatenate([p[nm][l][None] for nm in VEC_A] + [zeros], axis=0)
    tril = jnp.tril(jnp.ones((CHUNK, CHUNK), F32))
    w["wcat"] = (p["w_s"][l] * tril[None]).transpose(1, 0, 2).reshape(CHUNK, GMLP_HEADS * CHUNK).astype(BF16)
    w["wcatt"] = w["wcat"].T
    w["bfull"] = jnp.repeat(p["b_s"][l].T, HEAD_DIM, axis=1)
    kh, vh = _mem_proj_call(memq, mat["w_mk"], mat["w_mv"])
    hm = _head_table()
    scale = 1.0 / math.sqrt(HEAD_DIM)
    w["kt"] = (kh.T[:, None, :] * hm.T[:, :, None] * scale).reshape(XATTN_W, XATTN_HEADS * N_MEM).astype(BF16)
    w["ktt"] = w["kt"].T
    w["vm"] = (hm[:, None, :] * vh[None]).reshape(XATTN_HEADS * N_MEM, XATTN_W).astype(BF16)
    w["vmt"] = w["vm"].T
    zeros = jnp.zeros((6, D_MODEL), F32)
    w["v1"] = jnp.concatenate([p["ln1_g"][l][None], p["ln1_b"][l][None], zeros], axis=0)
    return w


def _ffn_operands(w_up, w_down, conv_f_w, p, l):
    w = {}
    w["wgt"] = w_up[:D_FF_PAD]
    w["wvt"] = w_up[D_FF_PAD:]
    w["wg"] = w["wgt"].T
    w["wv"] = w["wvt"].T
    w["wdown"] = w_down
    w["wdownt"] = w_down.T
    zeros = jnp.zeros((6, D_MODEL), F32)
    w["v2"] = jnp.concatenate([p["ln2_g"][l][None], p["ln2_b"][l][None], zeros], axis=0)
    w["cf"] = jnp.concatenate([conv_f_w, _ff_interleave(p["conv_f_b"][l][None]), jnp.zeros((4, D_FF_PAD), F32)], axis=0)
    return w


TS_MIXER = 256
TS_FFN = 256
TS_PROJ = 512
CONV_A_SHARD = CONV_W // N_DEV


def kernel(x, mem, w_in, conv_a_w, conv_a_b, ln_a_g, ln_a_b, ln_v_g, ln_v_b, w_s, b_s, w_mk, w_mv, w_out, ln1_g, ln1_b, w_up, conv_f_w, conv_f_b, w_down, ln2_g, ln2_b, loss_target, m_w_in, m_conv_a_w, m_conv_a_b, m_ln_a_g, m_ln_a_b, m_ln_v_g, m_ln_v_b, m_w_s, m_b_s, m_w_mk, m_w_mv, m_w_out, m_ln1_g, m_ln1_b, m_w_up, m_conv_f_w, m_conv_f_b, m_w_down, m_ln2_g, m_ln2_b, v_w_in, v_conv_a_w, v_conv_a_b, v_ln_a_g, v_ln_a_b, v_ln_v_g, v_ln_v_b, v_w_s, v_b_s, v_w_mk, v_w_mv, v_w_out, v_ln1_g, v_ln1_b, v_w_up, v_conv_f_w, v_conv_f_b, v_w_down, v_ln2_g, v_ln2_b):
    given = dict(locals())
    p = {nm: given[nm] for nm in WEIGHTS}
    mom_m = {nm: given["m_" + nm] for nm in WEIGHTS}
    mom_v = {nm: given["v_" + nm] for nm in WEIGHTS}
    seq = x.shape[1]
    ts_m, ts_f, ts_p = min(TS_MIXER, seq), min(TS_FFN, seq), min(TS_PROJ, seq)
    cx, cy, cc = _place()
    me = 4 * cx + 2 * cy + cc

    blobs = [_blob(p, l) for l in range(DEPTH)]
    blobs_bf = [b.astype(BF16) for b in blobs]
    conv_a_tile = jnp.pad(conv_a_w, ((0, 0), (0, HALO - CONV_K), (0, 128 - CONV_A_SHARD)))
    conv_f_tile = jnp.pad(conv_f_w, ((0, 0), (0, 8 - FFN_CONV_K), (0, 384 - FF_GROUP)))
    rows = dict(BLOCK_ROWS)
    mixer_names = ("w_in", "w_out", "w_mk", "w_mv")
    pieces = [(0, _row_off(nm), rows[nm]) for nm in mixer_names] + [(1, None, 0), (2, None, 0)]
    first = _all_gather_call([blobs_bf[0], conv_a_tile, conv_f_tile], pieces, "gather_weights")
    conv_a_all, conv_f_all = first[len(mixer_names)], first[len(mixer_names) + 1]
    conv_a = [conv_a_all[:, l, :, :CONV_A_SHARD].transpose(1, 0, 2).reshape(HALO, CONV_W) for l in range(DEPTH)]
    conv_f = [conv_f_all[:, l, :FFN_CONV_K, :FF_GROUP_PAD].transpose(1, 0, 2).reshape(FFN_CONV_K, D_FF_PAD)
              for l in range(DEPTH)]
    memq = mem[0].reshape(N_MEM, D_MODEL // MEM_FOLD, MEM_FOLD).transpose(2, 0, 1).astype(BF16)

    def gather_of(l, names):
        return _gather_exchange([blobs_bf[l]], [(0, _row_off(nm), rows[nm]) for nm in names])

    def full(pieces):
        return [g.reshape(-1, BLOB_LANES) for g in pieces]

    ffn_names = ("w_up", "w_down")
    ops0 = _mixer_operands(dict(zip(mixer_names, full(first[:len(mixer_names)]))), conv_a[0], p, 0, memq)
    mixer_saved = ("hb", "z1", "x1", "a1", "cat", "p", "gs", "rv")
    mixed0, ffn0 = _mixer_fwd_call(x[0], ops0, ts_m, gather_of(0, ffn_names))
    ops0.update(_ffn_operands(*full(ffn0), conv_f[0], p, 0))
    saved = [dict(zip(mixer_saved, mixed0), x=x[0])]
    (ug, uv, sl, dsl, z2, x2), all1 = _ffn_fwd_call(saved[0]["x1"], ops0, ts_f, gather_of(1, mixer_names + ffn_names))
    saved[0].update(ug=ug, uv=uv, sl=sl, dsl=dsl, z2=z2)
    all1 = dict(zip(mixer_names + ffn_names, full(all1)))
    ops1 = _mixer_operands(all1, conv_a[1], p, 1, memq)
    ops1.update(_ffn_operands(all1["w_up"], all1["w_down"], conv_f[1], p, 1))
    mixed1, _ = _mixer_fwd_call(x2, ops1, ts_m)
    saved.append(dict(zip(mixer_saved, mixed1), x=x2))
    (ug, uv, sl, dsl, z2), _ = _ffn_fwd_call(saved[1]["x1"], ops1, ts_f, want_x2=False)
    saved[1].update(ug=ug, uv=uv, sl=sl, dsl=dsl, z2=z2)
    ops = [ops0, ops1]

    hm = _head_table()
    core_id = cc.reshape(1).astype(jnp.int32)
    me_id, chip_id = me.reshape(1).astype(jnp.int32), (2 * cx + cy).reshape(1).astype(jnp.int32)
    slabs = [lax.empty((N_DEV, LAYER_ROWS, BLOB_LANES), F32) for _ in range(DEPTH)]
    accs, gathered_accs = [None] * DEPTH, [None] * DEPTH
    acc_names = [nm for nm, _ in GATHERED_ACCS]
    whole = [(a, None, 0) for a in range(len(acc_names))]

    def acc_list(l):
        return [accs[l][nm] for nm in acc_names]

    from_sibling, from_chips = [None] * DEPTH, [None] * DEPTH
    dx = loss_target[0]
    loss = None
    for l in reversed(range(DEPTH)):
        s, w = saved[l], ops[l]
        last = l == DEPTH - 1
        ride = None if last else _swap_exchange(slabs[l + 1], _swap_core_copies, 4)
        (sl, dug, duv, dz2, dcf, dv2, loss_acc), got = _ffn_bwd_call(
            dx, s["z2"], s["ug"], s["uv"], s["sl"], s["dsl"], w, ts_f, last, slabs[l], _row_off("w_down"), ride)
        if last:
            loss = loss_acc[0, 0]
        else:
            from_sibling[l + 1] = got[0]
            chip_sum = _pair_add_call(slabs[l + 1], from_sibling[l + 1], core_id, chip_id)
        off_up = _row_off("w_up")
        ride = None if last else _gather_exchange(acc_list(l + 1), whole)
        (sl, dxa), got = _proj_bwd_call(dug, w["wgt"], s["x1"], dz2, ALPHA, ts_p, "up_gate_bwd", sl, off_up, 0, 4, ride)
        if not last:
            gathered_accs[l + 1] = dict(zip(acc_names, got))
        ride = None if last else _swap_exchange(chip_sum, _swap_chip_copies, 3)
        (sl, dx1), got = _proj_bwd_call(duv, w["wvt"], s["x1"], dxa, 1.0, ts_p, "up_val_bwd", sl, off_up, 4, 4, ride)
        if not last:
            from_chips[l + 1] = got[0]
        (sl, dx, dkt, dvm, dwcat, dmsum, dva, dcw, dv1) = _mixer_bwd_call(
            dx1, s["z1"], s["hb"], s["a1"], s["cat"], s["p"], s["gs"], s["rv"], s["x"], w, ts_m, sl, _row_off("w_out"),
            _row_off("w_in"))
        dkh = jnp.einsum("hd,dhm->md", hm, dkt.reshape(XATTN_W, XATTN_HEADS, N_MEM)) * (1.0 / math.sqrt(HEAD_DIM))
        dvh = jnp.einsum("hd,hmd->md", hm, dvm.reshape(XATTN_HEADS, N_MEM, XATTN_W))
        slabs[l] = _mem_proj_bwd_call(memq, dkh, dvh, sl, _row_off("w_mk"), _row_off("w_mv"))
        accs[l] = dict(dva=dva, dv1=dv1, dv2=dv2, dcf=dcf, dcw=dcw, dwcat=dwcat, dmsum=dmsum)
    grad_x = dx[None]
    *got, from_sibling[0] = _all_gather_call(acc_list(0), whole, "gather_small_grads",
                                            swap=(slabs[0], _swap_core_copies, 4))
    gathered_accs[0] = dict(zip(acc_names, got))
    chip_sum = _pair_add_call(slabs[0], from_sibling[0], core_id, chip_id)
    in_flight = _swap_chip_start(chip_sum, from_chips[1])
    from_chips[1] = in_flight[4]

    def final(l):
        return _chip_add_adamw_call(slabs[l], from_sibling[l], from_chips[l], me_id, chip_id, blobs[l], _blob(mom_m, l),
                                    _blob(mom_v, l))

    per_layer = [None, final(1)]
    rep, dcf_sum, dcw_sum = _replicated_update_call(gathered_accs, p, mom_m, mom_v)
    from_chips[0] = _swap_chip_wait(*in_flight[:4], per_layer[1][0], dcf_sum[0])
    per_layer[0] = final(0)
    outs = [_unblob([per_layer[l][k] for l in range(DEPTH)]) for k in range(4)]
    for k in range(4):
        outs[k].update(rep[k])
    dcf_sum, dcw_sum = jnp.stack(dcf_sum), jnp.stack(dcw_sum)
    zero = jnp.zeros((), jnp.int32)
    g_conv_a_w = lax.dynamic_slice(dcw_sum, (zero, zero, CONV_A_SHARD * me), (DEPTH, CONV_K, CONV_A_SHARD))
    g_conv_f_w = lax.dynamic_slice(dcf_sum, (zero, zero, FF_GROUP_PAD * me), (DEPTH, FFN_CONV_K, FF_GROUP))
    g_conv_f_b = _ff_deinterleave(dcf_sum[:, FFN_CONV_K])
    conv_grads = dict(conv_a_w=g_conv_a_w, conv_f_w=g_conv_f_w, conv_f_b=g_conv_f_b)
    conv_names = tuple(conv_grads)
    upd = _adamw_whole_call([(p[nm], conv_grads[nm], mom_m[nm], mom_v[nm]) for nm in conv_names], "adamw_conv")
    for nm, (d, mn, vn) in zip(conv_names, upd):
        outs[0][nm], outs[1][nm], outs[2][nm], outs[3][nm] = conv_grads[nm], d, mn, vn

    loss = lax.psum(loss, ("x", "y", "c"))
    return (loss, grad_x, *[outs[0][nm] for nm in WEIGHTS], *[outs[1][nm] for nm in WEIGHTS],
            *[outs[2][nm] for nm in WEIGHTS], *[outs[3][nm] for nm in WEIGHTS])
```

```python
import math

import jax
import jax.numpy as jnp
from jax import lax
from jax.experimental import pallas as pl
from jax.experimental.pallas import tpu as pltpu

F32 = jnp.float32
BF16 = jnp.bfloat16

DEPTH = 2
D_MODEL = 1024
CONV_W = 384
GMLP_W = 384
XATTN_W = 256
HEAD_DIM = 64
GMLP_HEADS = 6
XATTN_HEADS = 4
IN_W = 1792
CONV_K = 31
CHUNK = 128
N_MEM = 256
D_FF = 2752
D_FF_PAD = 2816
FFN_CONV_K = 3
ALPHA = (2.0 * DEPTH) ** 0.25
LN_EPS = 1e-5
N_DEV = 8

ADAM_LR = 0.001
ADAM_B1 = 0.9
ADAM_B2 = 0.999
ADAM_EPS = 1e-08
ADAM_WD = 0.01
ADAM_STEP = 10

HALO = 32
CONV_ROWS = 32
V7X_VMEM_BYTES = 64 * 1024 * 1024
VMEM_LIMIT = V7X_VMEM_BYTES - 8 * 1024 * 1024
BLOB_LANES = 1024

MESH = pl.DeviceIdType.MESH

WEIGHTS = ("w_in", "conv_a_w", "conv_a_b", "ln_a_g", "ln_a_b", "ln_v_g", "ln_v_b", "w_s", "b_s", "w_mk", "w_mv",
           "w_out", "ln1_g", "ln1_b", "w_up", "conv_f_w", "conv_f_b", "w_down", "ln2_g", "ln2_b")


def _params(**kw):
    return pltpu.CompilerParams(vmem_limit_bytes=VMEM_LIMIT, **kw)


def _const(shape):
    nd = len(shape)
    return pl.BlockSpec(shape, lambda i: (0,) * nd, pipeline_mode=pl.Buffered(1))


def _acc(shape):
    nd = len(shape)
    return pl.BlockSpec(shape, lambda i: (0,) * nd)


class _Exchange:
    def __init__(self, arrays, out_shapes, n_copies, build, n_local=1):
        self.arrays, self.out_shapes, self.n_copies, self.build = list(arrays), list(out_shapes), n_copies, build
        self.n_local = n_local


def _carry(core, n_in, n_out, exch, n_steps):
    if exch is None:
        return core
    nx_in, nx_out = len(exch.arrays), len(exch.out_shapes)

    def body(*refs):
        o0 = n_in + nx_in
        s0 = o0 + n_out + nx_out
        x_in, x_out, sems = refs[n_in:o0], refs[o0 + n_out:s0], refs[-3:]
        i = pl.program_id(0)

        @pl.when(i == 0)
        def _():
            remote, local = exch.build(x_in, x_out, *sems)
            for cp in remote + local:
                cp.start()

        core(*refs[:n_in], *refs[o0:o0 + n_out], *refs[s0:-3])

        @pl.when(i == n_steps - 1)
        def _():
            remote, local = exch.build(x_in, x_out, *sems)
            for cp in remote + local:
                cp.wait()

    return body


def _grid_call(core, name, n_steps, in_specs, out_specs, out_shape, scratch_shapes, args, aliases=None, exch=None):
    hbm = pl.BlockSpec(memory_space=pl.ANY)
    n_in, n_out = len(in_specs), len(out_specs)
    in_specs, out_specs, out_shape, scratch_shapes, args = (list(in_specs), list(out_specs), list(out_shape),
                                                            list(scratch_shapes), list(args))
    if exch is not None:
        in_specs += [hbm] * len(exch.arrays)
        out_specs += [hbm] * len(exch.out_shapes)
        out_shape += exch.out_shapes
        scratch_shapes += [pltpu.SemaphoreType.DMA((exch.n_copies,)), pltpu.SemaphoreType.DMA((exch.n_copies,)),
                           pltpu.SemaphoreType.DMA((exch.n_local,))]
        args += exch.arrays
    out = pl.pallas_call(
        _carry(core, n_in, n_out, exch, n_steps), name=name, grid=(n_steps,), in_specs=in_specs, out_specs=out_specs,
        out_shape=out_shape, scratch_shapes=scratch_shapes, input_output_aliases=aliases or {},
        compiler_params=_params(dimension_semantics=("arbitrary",)))(*args)
    return list(out[:n_out]), list(out[n_out:])


def _sigmoid(x):
    return 1.0 / (1.0 + jnp.exp(-x))


_GELU_C = math.sqrt(2.0 / math.pi)


def _gelu(x):
    x2 = x * x
    t = jnp.tanh(_GELU_C * (x + 0.044715 * x * x2))
    g = 0.5 * x * (1.0 + t)
    dg = 0.5 * (1.0 + t) + 0.5 * x * (1.0 - t * t) * (_GELU_C * (1.0 + 3.0 * 0.044715 * x2))
    return g, dg


def _ln_stats(z):
    mu = jnp.mean(z, axis=-1, keepdims=True)
    zc = z - mu
    var = jnp.mean(zc * zc, axis=-1, keepdims=True)
    r = lax.rsqrt(var + LN_EPS)
    return zc * r, r


def _ln_bwd(dy, xh, r, g):
    dxh = dy * g
    m1 = jnp.mean(dxh, axis=-1, keepdims=True)
    m2 = jnp.mean(dxh * xh, axis=-1, keepdims=True)
    return r * (dxh - m1 - xh * m2)


def _rowsum(x):
    return jnp.sum(x, axis=0, keepdims=True)


def _part8(x):
    part = x[0:8, :]
    for q in range(8, x.shape[0], 8):
        part = part + x[q:q + 8, :]
    return part


def _dot(a, b):
    return jnp.dot(a, b, preferred_element_type=F32)


def _dot_tn(a, b):
    return lax.dot_general(a, b, (((0,), (0,)), ((), ())), preferred_element_type=F32)


def _dot_nt(a, b):
    return lax.dot_general(a, b, (((1,), (1,)), ((), ())), preferred_element_type=F32)


def _shift_copies(buf, sh, rows):
    for b in range(1, 8):
        sh[b - 1, 0:rows, :] = buf[b:b + rows, :]


def _window(buf, sh, start):
    b = start % 8
    a = start - b
    return buf[a:a + CONV_ROWS, :] if b == 0 else sh[b - 1, a:a + CONV_ROWS, :]


def _conv31_fwd(buf, sh, w_ref, bias, out, ts):
    for r0 in range(0, ts, CONV_ROWS):
        acc = jnp.broadcast_to(bias, (CONV_ROWS, CONV_W))
        for k in range(CONV_K):
            acc = acc + w_ref[k:k + 1, :] * _window(buf, sh, r0 + HALO - (CONV_K - 1) + k)
        out[r0:r0 + CONV_ROWS, :] = acc


def _conv31_dx(dbuf, dsh, w_ref, out, ts):
    for r0 in range(0, ts, CONV_ROWS):
        acc = jnp.zeros((CONV_ROWS, CONV_W), F32)
        for k in range(CONV_K):
            acc = acc + w_ref[k:k + 1, :] * _window(dbuf, dsh, r0 + (CONV_K - 1) - k)
        out[r0:r0 + CONV_ROWS, :] = acc


def _conv31_dw(buf, sh, dbuf, acc, ts):
    for r0 in range(0, ts, CONV_ROWS):
        d = dbuf[r0:r0 + CONV_ROWS, :]
        for k in range(CONV_K):
            m = d * _window(buf, sh, r0 + HALO - (CONV_K - 1) + k)
            part = m[0:8, :]
            for q in range(8, CONV_ROWS, 8):
                part = part + m[q:q + 8, :]
            acc[k] += part


def _head_mask(width, h):
    lane = lax.broadcasted_iota(jnp.int32, (CHUNK, width), 1)
    return (lane >= h * HEAD_DIM) & (lane < (h + 1) * HEAD_DIM)


def _stack_heads(vn_c):
    return jnp.concatenate([jnp.where(_head_mask(GMLP_W, h), vn_c, 0.0) for h in range(GMLP_HEADS)], axis=0)


def _group_a_fwd(hf, buf, sh, a1_ref, cw_ref, va_ref, ts, conv=True):
    ha = hf[:, 0:CONV_W]
    sg = _sigmoid(hf[:, CONV_W:2 * CONV_W])
    buf[HALO:HALO + ts, :] = ha * sg
    _shift_copies(buf, sh, ts + HALO - 8)
    if conv:
        _conv31_fwd(buf, sh, cw_ref, va_ref[0:1, :], a1_ref, ts)
    a2h, ra = _ln_stats(a1_ref[...])
    a2 = a2h * va_ref[1:2, :] + va_ref[2:3, :]
    sa = _sigmoid(a2)
    return dict(ha=ha, sg=sg, a2h=a2h, ra=ra, a2=a2, sa=sa, a=a2 * sa)


GROUP_LANES = ((0, 2 * CONV_W), (2 * CONV_W, 2 * CONV_W + 2 * GMLP_W), (2 * CONV_W + 2 * GMLP_W, IN_W))


def _group_b_fwd(hf, va_ref, wcat_ref, bfull_ref, ts):
    hu = hf[:, 0:GMLP_W]
    hv = hf[:, GMLP_W:2 * GMLP_W]
    u, du = _gelu(hu)
    v, dv = _gelu(hv)
    vhat, rv = _ln_stats(v)
    vn = vhat * va_ref[3:4, :] + va_ref[4:5, :]
    stacks, mixed = [], []
    for c0 in range(0, ts, CHUNK):
        st = _stack_heads(vn[c0:c0 + CHUNK, :]).astype(BF16)
        stacks.append(st)
        mixed.append(_dot(wcat_ref[...], st) + bfull_ref[...])
    mixed = jnp.concatenate(mixed, axis=0) if len(mixed) > 1 else mixed[0]
    return dict(u=u, du=du, dv=dv, vhat=vhat, rv=rv, stacks=stacks, mixed=mixed, g=u * mixed)


GROUP_B_SAVED = ("u", "du", "dv", "vhat", "mixed")


def _group_c_fwd(qb, kt_ref, vm_ref):
    s_all = _dot(qb, kt_ref[...])
    ps = []
    for g in range(XATTN_HEADS):
        s = s_all[:, g * N_MEM:(g + 1) * N_MEM]
        e = jnp.exp(s - jnp.max(s, axis=-1, keepdims=True))
        ps.append(e / jnp.sum(e, axis=-1, keepdims=True))
    p_all = jnp.concatenate(ps, axis=1)
    pb = p_all.astype(BF16)
    return dict(qb=qb, p=p_all, pb=pb, c=_dot(pb, vm_ref[...]))


def _mixer_fwd_call(x, w, ts, exch=None):
    seq = x.shape[0]
    n = seq // ts

    def body(x_ref, win_ref, cw_ref, va_ref, wcat_ref, bfull_ref, kt_ref, vm_ref, wout_ref, v1_ref,
             hb_ref, z1_ref, x1_ref, a1buf, cat_ref, p_ref, gs_ref, rv_ref, buf, sh):
        i = pl.program_id(0)

        @pl.when(i == 0)
        def _():
            buf[0:HALO, :] = jnp.zeros((HALO, CONV_W), F32)

        @pl.when(i > 0)
        def _():
            buf[0:HALO, :] = buf[ts:ts + HALO, :]

        xv = x_ref[...]
        hb = _dot(xv.astype(BF16), win_ref[...]).astype(BF16)
        hb_ref[...] = hb
        hf = hb.astype(F32)
        (a_lo, a_hi), (b_lo, b_hi), (c_lo, c_hi) = GROUP_LANES
        ga = _group_a_fwd(hf[:, a_lo:a_hi], buf, sh, a1buf, cw_ref, va_ref, ts)
        gb = _group_b_fwd(hf[:, b_lo:b_hi], va_ref, wcat_ref, bfull_ref, ts)
        gc = _group_c_fwd(hb[:, c_lo:c_hi], kt_ref, vm_ref)
        cat = jnp.concatenate([ga["a"], gb["g"], gc["c"]], axis=1).astype(BF16)
        cat_ref[...] = cat
        p_ref[...] = gc["pb"]
        gs_ref[...] = jnp.concatenate([gb[k] for k in GROUP_B_SAVED], axis=1).astype(BF16)
        rv_ref[...] = jnp.broadcast_to(gb["rv"], (ts, 128))
        z1 = ALPHA * xv + _dot(cat, wout_ref[...])
        z1_ref[...] = z1
        xh, _ = _ln_stats(z1)
        x1_ref[...] = xh * v1_ref[0:1, :] + v1_ref[1:2, :]

    row = lambda width: pl.BlockSpec((ts, width), lambda i: (i, 0))
    return _grid_call(
        body, "mixer_fwd", n,
        in_specs=[row(D_MODEL), _const((D_MODEL, IN_W)), _const((HALO, CONV_W)), _const((8, CONV_W)),
                  _const((CHUNK, GMLP_HEADS * CHUNK)), _const((CHUNK, GMLP_W)), _const((XATTN_W, XATTN_HEADS * N_MEM)),
                  _const((XATTN_HEADS * N_MEM, XATTN_W)), _const((D_MODEL, D_MODEL)), _const((8, D_MODEL))],
        out_specs=[row(IN_W), row(D_MODEL), row(D_MODEL), row(CONV_W), row(D_MODEL), row(XATTN_HEADS * N_MEM),
                   row(len(GROUP_B_SAVED) * GMLP_W), row(128)],
        out_shape=[jax.ShapeDtypeStruct((seq, IN_W), BF16), jax.ShapeDtypeStruct((seq, D_MODEL), F32),
                   jax.ShapeDtypeStruct((seq, D_MODEL), F32), jax.ShapeDtypeStruct((seq, CONV_W), F32),
                   jax.ShapeDtypeStruct((seq, D_MODEL), BF16), jax.ShapeDtypeStruct((seq, XATTN_HEADS * N_MEM), BF16),
                   jax.ShapeDtypeStruct((seq, len(GROUP_B_SAVED) * GMLP_W), BF16),
                   jax.ShapeDtypeStruct((seq, 128), F32)],
        scratch_shapes=[pltpu.VMEM((ts + HALO, CONV_W), F32), pltpu.VMEM((7, ts + HALO, CONV_W), F32)],
        args=(x, w["win"], w["cw"], w["va"], w["wcat"], w["bfull"], w["kt"], w["vm"], w["wout"], w["v1"]), exch=exch)


def _store_blocks(acc, slabs_ref, sems, row_off, rows, first_block, n_blocks):
    copies = [pltpu.make_async_copy(acc.at[pl.ds(q * rows, rows)], slabs_ref.at[first_block + q, pl.ds(row_off, rows)],
                                    sems.at[q]) for q in range(n_blocks)]
    for cp in copies:
        cp.start()
    for cp in copies:
        cp.wait()


def _mixer_bwd_call(dx1, z1, hb, a1, cat, p, gs, rv, x, w, ts, slabs, off_out, off_in):
    seq = dx1.shape[0]
    n = seq // ts
    halo_blocks = ts // HALO

    def body(slabs_in, dx1_ref, z1_ref, hb_ref, hprev_ref, a1_ref, cat_ref, p_ref, gs_ref, rv_ref, x_ref, cw_ref, va_ref,
             wcatt_ref, ktt_ref, vmt_ref, woutt_ref, wint_ref, v1_ref,
             slabs_ref, dx_ref, dkt_ref, dvm_ref, dwcat_ref, dmsum_ref, dva_ref, dcw_ref, dv1_ref,
             buf, dbuf, da0buf, dwout_ref, dwin_ref, sems, sh, dsh, dcw_acc):
        i = pl.program_id(0)

        @pl.when(i == 0)
        def _():
            for ref in (dwout_ref, dwin_ref, dkt_ref, dvm_ref, dwcat_ref, dmsum_ref, dva_ref, dcw_ref, dv1_ref, dcw_acc):
                ref[...] = jnp.zeros(ref.shape, F32)
            dbuf[ts:ts + HALO, :] = jnp.zeros((HALO, CONV_W), F32)

        @pl.when(i > 0)
        def _():
            dbuf[ts:ts + HALO, :] = dbuf[0:HALO, :]

        dx1v = dx1_ref[...]
        xh1, r1 = _ln_stats(z1_ref[...])
        dv1_ref[0:1, :] += _rowsum(dx1v * xh1)
        dv1_ref[1:2, :] += _rowsum(dx1v)
        dz1 = _ln_bwd(dx1v, xh1, r1, v1_ref[0:1, :])
        dmix = dz1.astype(BF16)

        hf = hb_ref[:, 0:2 * CONV_W].astype(F32)
        hp = hprev_ref[...].astype(F32)
        a0p = hp[:, 0:CONV_W] * _sigmoid(hp[:, CONV_W:2 * CONV_W])
        buf[0:HALO, :] = jnp.where(i == n - 1, 0.0, a0p)
        ga = _group_a_fwd(hf, buf, sh, a1_ref, cw_ref, va_ref, ts, conv=False)
        gb = {k: gs_ref[:, j * GMLP_W:(j + 1) * GMLP_W].astype(F32) for j, k in enumerate(GROUP_B_SAVED)}
        vn = gb["vhat"] * va_ref[3:4, :] + va_ref[4:5, :]
        pb = p_ref[...]
        gc = dict(qb=hb_ref[:, IN_W - XATTN_W:IN_W], pb=pb, p=pb.astype(F32))

        dwout_ref[...] += _dot_tn(cat_ref[...], dmix)
        dcat = _dot(dmix, woutt_ref[...])
        da = dcat[:, 0:CONV_W]
        dg = dcat[:, CONV_W:CONV_W + GMLP_W]
        dc = dcat[:, CONV_W + GMLP_W:D_MODEL].astype(BF16)

        dp = _dot(dc, vmt_ref[...])
        dvm_ref[...] += _dot_tn(gc["pb"], dc)
        dss = []
        for g in range(XATTN_HEADS):
            sl = slice(g * N_MEM, (g + 1) * N_MEM)
            pg = gc["p"][:, sl]
            dpg = dp[:, sl]
            dss.append(pg * (dpg - jnp.sum(dpg * pg, axis=-1, keepdims=True)))
        ds = jnp.concatenate(dss, axis=1).astype(BF16)
        dq = _dot(ds, ktt_ref[...])
        dkt_ref[...] += _dot_tn(gc["qb"], ds)

        dmixed = dg * gb["u"]
        dhu = dg * gb["mixed"] * gb["du"]
        dvns = []
        for j, c0 in enumerate(range(0, ts, CHUNK)):
            dm = dmixed[c0:c0 + CHUNK, :]
            dmb = dm.astype(BF16)
            dmsum_ref[...] += dm
            dwcat_ref[...] += _dot_nt(dmb, _stack_heads(vn[c0:c0 + CHUNK, :]).astype(BF16))
            dst = _dot(wcatt_ref[...], dmb)
            dvn_c = jnp.zeros((CHUNK, GMLP_W), F32)
            for h in range(GMLP_HEADS):
                dvn_c = dvn_c + jnp.where(_head_mask(GMLP_W, h), dst[h * CHUNK:(h + 1) * CHUNK, :], 0.0)
            dvns.append(dvn_c)
        dvn = jnp.concatenate(dvns, axis=0) if len(dvns) > 1 else dvns[0]
        dva_ref[3:4, :] += _rowsum(dvn * gb["vhat"])
        dva_ref[4:5, :] += _rowsum(dvn)
        dhv = _ln_bwd(dvn, gb["vhat"], rv_ref[:, 0:1], va_ref[3:4, :]) * gb["dv"]

        a2, sa = ga["a2"], ga["sa"]
        da2 = da * (sa * (1.0 + a2 * (1.0 - sa)))
        dva_ref[1:2, :] += _rowsum(da2 * ga["a2h"])
        dva_ref[2:3, :] += _rowsum(da2)
        da1 = _ln_bwd(da2, ga["a2h"], ga["ra"], va_ref[1:2, :])
        dva_ref[0:1, :] += _rowsum(da1)
        dbuf[0:ts, :] = da1
        _shift_copies(dbuf, dsh, ts + HALO - 8)
        _conv31_dw(buf, sh, dbuf, dcw_acc, ts)
        _conv31_dx(dbuf, dsh, cw_ref, da0buf, ts)
        da0 = da0buf[...]
        sg = ga["sg"]
        dha = da0 * sg
        dhg = da0 * ga["ha"] * sg * (1.0 - sg)

        dh = jnp.concatenate([dha, dhg, dhu, dhv, dq], axis=1).astype(BF16)
        dx_ref[...] = _dot(dh, wint_ref[...]) + ALPHA * dz1
        dwin_ref[...] += _dot_tn(dh, x_ref[...].astype(BF16))

        @pl.when(i == n - 1)
        def _():
            for k in range(CONV_K):
                dcw_ref[k:k + 1, :] = _rowsum(dcw_acc[k])
            _store_blocks(dwout_ref, slabs_ref, sems, off_out, D_MODEL // N_DEV, 0, N_DEV)
            _store_blocks(dwin_ref, slabs_ref, sems, off_in, IN_W // N_DEV, 0, N_DEV)

    rev = lambda width: pl.BlockSpec((ts, width), lambda i: (n - 1 - i, 0))
    prev = pl.BlockSpec((HALO, 2 * CONV_W), lambda i: (jnp.maximum((n - 1 - i) * halo_blocks - 1, 0), 0))
    hbm = pl.BlockSpec(memory_space=pl.ANY)
    hc = GMLP_HEADS * CHUNK
    am = XATTN_HEADS * N_MEM
    return pl.pallas_call(
        body, name="mixer_bwd", grid=(n,),
        in_specs=[hbm, rev(D_MODEL), rev(D_MODEL), rev(IN_W), prev, rev(CONV_W), rev(D_MODEL), rev(am),
                  rev(len(GROUP_B_SAVED) * GMLP_W), rev(128), rev(D_MODEL), _const((HALO, CONV_W)),
                  _const((8, CONV_W)), _const((hc, CHUNK)), _const((am, XATTN_W)), _const((XATTN_W, am)),
                  _const((D_MODEL, D_MODEL)), _const((IN_W, D_MODEL)), _const((8, D_MODEL))],
        out_specs=[hbm, rev(D_MODEL), _acc((XATTN_W, am)), _acc((am, XATTN_W)),
                   _acc((CHUNK, hc)), _acc((CHUNK, GMLP_W)), _acc((8, CONV_W)), _acc((HALO, CONV_W)),
                   _acc((8, D_MODEL))],
        out_shape=[jax.ShapeDtypeStruct(slabs.shape, F32), jax.ShapeDtypeStruct((seq, D_MODEL), F32),
                   jax.ShapeDtypeStruct((XATTN_W, am), F32),
                   jax.ShapeDtypeStruct((am, XATTN_W), F32), jax.ShapeDtypeStruct((CHUNK, hc), F32),
                   jax.ShapeDtypeStruct((CHUNK, GMLP_W), F32), jax.ShapeDtypeStruct((8, CONV_W), F32),
                   jax.ShapeDtypeStruct((HALO, CONV_W), F32), jax.ShapeDtypeStruct((8, D_MODEL), F32)],
        scratch_shapes=[pltpu.VMEM((ts + HALO, CONV_W), F32),
                        pltpu.VMEM((ts + HALO, CONV_W), F32), pltpu.VMEM((ts, CONV_W), F32),
                        pltpu.VMEM((D_MODEL, D_MODEL), F32), pltpu.VMEM((IN_W, D_MODEL), F32),
                        pltpu.SemaphoreType.DMA((N_DEV,)),
                        pltpu.VMEM((7, ts + HALO, CONV_W), F32), pltpu.VMEM((7, ts + HALO, CONV_W), F32),
                        pltpu.VMEM((CONV_K, 8, CONV_W), F32)],
        input_output_aliases={0: 0},
        compiler_params=_params(dimension_semantics=("arbitrary",)),
    )(slabs, dx1, z1, hb, hb, a1, cat, p, gs, rv, x, w["cw"], w["va"], w["wcatt"], w["ktt"], w["vmt"], w["woutt"],
      w["wint"], w["v1"])


FFN_HALO = 8
FF_GROUP = D_FF // N_DEV
FF_GROUP_PAD = D_FF_PAD // N_DEV


def _ffn_taps(ubuf, ts, lo, hi):
    return tuple(ubuf[FFN_HALO - (FFN_CONV_K - 1) + k:FFN_HALO - (FFN_CONV_K - 1) + k + ts, lo:hi]
                 for k in range(FFN_CONV_K))


def _ffn_gate(ubuf, cf_ref, ts, lo, hi):
    taps = _ffn_taps(ubuf, ts, lo, hi)
    g = cf_ref[3:4, lo:hi] + cf_ref[2:3, lo:hi] * taps[2]
    g = g + cf_ref[1:2, lo:hi] * taps[1]
    return g + cf_ref[0:1, lo:hi] * taps[0]


FFN_CHUNK = 256
FFN_CHUNKS = tuple((lo, lo + FFN_CHUNK) for lo in range(0, D_FF_PAD, FFN_CHUNK))


def _ffn_fwd_call(x1, w, ts, exch=None, want_x2=True):
    seq = x1.shape[0]
    n = seq // ts
    n_wide = 4

    def body(x1_ref, wg_ref, wv_ref, cf_ref, wdown_ref, v2_ref, ug_ref, uv_ref, sl_ref, dsl_ref, z2_ref, *rest):
        x2_ref = rest[0] if want_x2 else None
        ubuf, act_buf = rest[-2:]
        i = pl.program_id(0)

        @pl.when(i == 0)
        def _():
            ubuf[0:FFN_HALO, :] = jnp.zeros((FFN_HALO, D_FF_PAD), F32)

        @pl.when(i > 0)
        def _():
            ubuf[0:FFN_HALO, :] = ubuf[ts:ts + FFN_HALO, :]

        xv = x1_ref[...]
        xb = xv.astype(BF16)
        for lo, hi in FFN_CHUNKS:
            ug = _dot(xb, wg_ref[:, lo:hi]).astype(BF16)
            uv = _dot(xb, wv_ref[:, lo:hi]).astype(BF16)
            ug_ref[:, lo:hi] = ug
            uv_ref[:, lo:hi] = uv
            ubuf[FFN_HALO:FFN_HALO + ts, lo:hi] = ug.astype(F32)
            gate = _ffn_gate(ubuf, cf_ref, ts, lo, hi)
            sg = _sigmoid(gate)
            sl = gate * sg
            sl_ref[:, lo:hi] = sl.astype(BF16)
            dsl_ref[:, lo:hi] = (sg * (1.0 + gate * (1.0 - sg))).astype(BF16)
            act_buf[:, lo:hi] = (sl * uv.astype(F32)).astype(BF16)
        y = ALPHA * xv + _dot(act_buf[...], wdown_ref[...])
        z2_ref[...] = y
        if want_x2:
            xh, _ = _ln_stats(y)
            x2_ref[...] = xh * v2_ref[0:1, :] + v2_ref[1:2, :]

    row = lambda width: pl.BlockSpec((ts, width), lambda i: (i, 0))
    n_narrow = 2 if want_x2 else 1
    return _grid_call(
        body, "ffn_fwd" if want_x2 else "ffn_fwd_last", n,
        in_specs=[row(D_MODEL), _const((D_MODEL, D_FF_PAD)), _const((D_MODEL, D_FF_PAD)), _const((8, D_FF_PAD)),
                  _const((D_FF_PAD, D_MODEL)), _const((8, D_MODEL))],
        out_specs=[row(D_FF_PAD)] * n_wide + [row(D_MODEL)] * n_narrow,
        out_shape=[jax.ShapeDtypeStruct((seq, D_FF_PAD), BF16)] * n_wide
                  + [jax.ShapeDtypeStruct((seq, D_MODEL), F32)] * n_narrow,
        scratch_shapes=[pltpu.VMEM((ts + FFN_HALO, D_FF_PAD), F32), pltpu.VMEM((ts, D_FF_PAD), BF16)],
        args=(x1, w["wg"], w["wv"], w["cf"], w["wdown"], w["v2"]), exch=exch)


def _ffn_bwd_call(dx2_or_target, z2, ug, uv, sl, dsl, w, ts, last, slabs, row_off, exch=None):
    seq = z2.shape[0]
    n = seq // ts
    halo_blocks = ts // 16

    def body(slabs_in, dx2_ref, z2_ref, ug_ref, uv_ref, sl_ref, dsl_ref, uprev_ref, cf_ref, wdownt_ref, v2_ref,
             slabs_ref, dug_ref, duv_ref, dz2_ref, dcf_ref, dv2_ref, loss_ref,
             ubuf, dgbuf, dwacc, sems, dcf_acc):
        i = pl.program_id(0)

        @pl.when(i == 0)
        def _():
            dwacc[...] = jnp.zeros(dwacc.shape, F32)
            dcf_acc[...] = jnp.zeros(dcf_acc.shape, F32)
            dcf_ref[...] = jnp.zeros(dcf_ref.shape, F32)
            dv2_ref[...] = jnp.zeros(dv2_ref.shape, F32)
            loss_ref[...] = jnp.zeros(loss_ref.shape, F32)
            dgbuf[ts:ts + FFN_HALO, :] = jnp.zeros((FFN_HALO, D_FF_PAD), F32)

        @pl.when(i > 0)
        def _():
            dgbuf[ts:ts + FFN_HALO, :] = dgbuf[0:FFN_HALO, :]

        xh2, r2 = _ln_stats(z2_ref[...])
        if last:
            diff = xh2 * v2_ref[0:1, :] + v2_ref[1:2, :] - dx2_ref[...]
            loss_ref[...] += jnp.sum(diff * diff) * (0.5 / D_MODEL)
            dx2v = diff * (1.0 / D_MODEL)
        else:
            dx2v = dx2_ref[...]
        dv2_ref[0:1, :] += _rowsum(dx2v * xh2)
        dv2_ref[1:2, :] += _rowsum(dx2v)
        dz2 = _ln_bwd(dx2v, xh2, r2, v2_ref[0:1, :])
        dz2_ref[...] = dz2
        dy = dz2.astype(BF16)

        up = uprev_ref[...].astype(F32)[8:16, :]
        ubuf[0:FFN_HALO, :] = jnp.where(i == n - 1, 0.0, up)
        ubuf[FFN_HALO:FFN_HALO + ts, :] = ug_ref[...].astype(F32)
        for lo, hi in FFN_CHUNKS:
            taps = _ffn_taps(ubuf, ts, lo, hi)
            sl = sl_ref[:, lo:hi].astype(F32)
            uvf = uv_ref[:, lo:hi].astype(F32)
            act = (sl * uvf).astype(BF16)
            dwacc[lo:hi, :] += _dot_tn(act, dy)
            dact = _dot(dy, wdownt_ref[:, lo:hi])
            duv_ref[:, lo:hi] = (dact * sl).astype(BF16)
            dgate = dact * uvf * dsl_ref[:, lo:hi].astype(F32)
            dgbuf[0:ts, lo:hi] = dgate
            dcf_acc[FFN_CONV_K, :, lo:hi] += _part8(dgate)
            for k in range(FFN_CONV_K):
                dcf_acc[k, :, lo:hi] += _part8(dgate * taps[k])
            dug = cf_ref[2:3, lo:hi] * dgate + cf_ref[1:2, lo:hi] * dgbuf[1:1 + ts, lo:hi]
            dug = dug + cf_ref[0:1, lo:hi] * dgbuf[2:2 + ts, lo:hi]
            dug_ref[:, lo:hi] = dug.astype(BF16)

        @pl.when(i == n - 1)
        def _():
            for k in range(FFN_CONV_K + 1):
                dcf_ref[k:k + 1, :] = _rowsum(dcf_acc[k])
            _store_blocks(dwacc, slabs_ref, sems, row_off, D_FF_PAD // N_DEV, 0, N_DEV)

    rev = lambda width: pl.BlockSpec((ts, width), lambda i: (n - 1 - i, 0))
    prev = pl.BlockSpec((16, D_FF_PAD), lambda i: (jnp.maximum((n - 1 - i) * halo_blocks - 1, 0), 0))
    hbm = pl.BlockSpec(memory_space=pl.ANY)
    return _grid_call(
        body, "ffn_bwd_last" if last else "ffn_bwd", n,
        in_specs=[hbm, rev(D_MODEL), rev(D_MODEL)] + [rev(D_FF_PAD)] * 4 + [prev, _const((8, D_FF_PAD)),
                                                                           _const((D_MODEL, D_FF_PAD)), _const((8, D_MODEL))],
        out_specs=[hbm, rev(D_FF_PAD), rev(D_FF_PAD), rev(D_MODEL),
                   _acc((8, D_FF_PAD)), _acc((8, D_MODEL)), _acc((8, 128))],
        out_shape=[jax.ShapeDtypeStruct(slabs.shape, F32),
                   jax.ShapeDtypeStruct((seq, D_FF_PAD), BF16), jax.ShapeDtypeStruct((seq, D_FF_PAD), BF16),
                   jax.ShapeDtypeStruct((seq, D_MODEL), F32),
                   jax.ShapeDtypeStruct((8, D_FF_PAD), F32), jax.ShapeDtypeStruct((8, D_MODEL), F32),
                   jax.ShapeDtypeStruct((8, 128), F32)],
        scratch_shapes=[pltpu.VMEM((ts + FFN_HALO, D_FF_PAD), F32), pltpu.VMEM((ts + FFN_HALO, D_FF_PAD), F32),
                        pltpu.VMEM((D_FF_PAD, D_MODEL), F32), pltpu.SemaphoreType.DMA((N_DEV,)),
                        pltpu.VMEM((FFN_CONV_K + 1, 8, D_FF_PAD), F32)],
        args=(slabs, dx2_or_target, z2, ug, uv, sl, dsl, ug, w["cf"], w["wdownt"], w["v2"]), aliases={0: 0}, exch=exch)


def _proj_bwd_call(d, wt, xin, addend, scale, ts, name, slabs, row_off, first_block, n_blocks, exch=None):
    seq, k = d.shape
    n = seq // ts

    def body(slabs_in, d_ref, wt_ref, xin_ref, add_ref, slabs_ref, dx_ref, acc, sems):
        i = pl.program_id(0)

        @pl.when(i == 0)
        def _():
            acc[...] = jnp.zeros(acc.shape, F32)

        dv = d_ref[...]
        dx_ref[...] = _dot(dv, wt_ref[...]) + scale * add_ref[...]
        acc[...] += _dot_tn(dv, xin_ref[...].astype(BF16))

        @pl.when(i == n - 1)
        def _():
            _store_blocks(acc, slabs_ref, sems, row_off, k // n_blocks, first_block, n_blocks)

    row = lambda width: pl.BlockSpec((ts, width), lambda i: (i, 0))
    hbm = pl.BlockSpec(memory_space=pl.ANY)
    return _grid_call(
        body, name, n,
        in_specs=[hbm, row(k), _const((k, D_MODEL)), row(D_MODEL), row(D_MODEL)],
        out_specs=[hbm, row(D_MODEL)],
        out_shape=[jax.ShapeDtypeStruct(slabs.shape, F32), jax.ShapeDtypeStruct((seq, D_MODEL), F32)],
        scratch_shapes=[pltpu.VMEM((k, D_MODEL), F32), pltpu.SemaphoreType.DMA((n_blocks,))],
        args=(slabs, d, wt, xin, addend), aliases={0: 0}, exch=exch)


MEM_FOLD = BLOB_LANES // XATTN_W


def _mem_proj_call(memq, wk_flat, wv_flat):
    def body(memq_ref, wk_ref, wv_ref, kh_ref, vh_ref):
        for w_ref, o_ref in ((wk_ref, kh_ref), (wv_ref, vh_ref)):
            acc = jnp.zeros((N_MEM, XATTN_W), F32)
            for q in range(MEM_FOLD):
                acc = acc + _dot(memq_ref[q], w_ref[:, q * XATTN_W:(q + 1) * XATTN_W])
            o_ref[...] = acc

    out = jax.ShapeDtypeStruct((N_MEM, XATTN_W), F32)
    return pl.pallas_call(body, name="mem_proj", out_shape=[out, out], compiler_params=_params())(memq, wk_flat, wv_flat)


def _mem_proj_bwd_call(memq, dkh, dvh, slabs, off_k, off_v):
    rows = D_MODEL // MEM_FOLD

    def body(slabs_in, memq_ref, dkh_ref, dvh_ref, slabs_ref, acc, sems):
        for d_ref, off in ((dkh_ref, off_k), (dvh_ref, off_v)):
            dv = d_ref[...].astype(BF16)
            for q in range(MEM_FOLD):
                acc[:, q * XATTN_W:(q + 1) * XATTN_W] = _dot_tn(memq_ref[q], dv)
            _store_blocks(acc, slabs_ref, sems, off, rows // N_DEV, 0, N_DEV)

    hbm = pl.BlockSpec(memory_space=pl.ANY)
    vmem = pl.BlockSpec(memory_space=pltpu.VMEM)
    return pl.pallas_call(
        body, name="mem_proj_bwd", in_specs=[hbm, vmem, vmem, vmem], out_specs=hbm,
        out_shape=jax.ShapeDtypeStruct(slabs.shape, F32),
        scratch_shapes=[pltpu.VMEM((rows, BLOB_LANES), F32), pltpu.SemaphoreType.DMA((N_DEV,))],
        input_output_aliases={0: 0}, compiler_params=_params(),
    )(slabs, memq, dkh, dvh)


def _place():
    return lax.axis_index("x"), lax.axis_index("y"), lax.axis_index("c")


def _all_gather_call(arrs, pieces, name, swap=None):
    n_in, n_p = len(arrs), len(pieces)
    n_sw = 0 if swap is None else 1

    def body(*refs):
        ins, outs = refs[:n_in], refs[n_in + n_sw:n_in + n_sw + n_p]
        send_sems, recv_sems, local_sems = refs[n_in + 2 * n_sw + n_p:n_in + 2 * n_sw + n_p + 3]
        swapped = []
        if swap is not None:
            swapped = swap[1](refs[n_in], refs[n_in + n_sw + n_p], *refs[-2:])
            for cp in swapped:
                cp.start()
        x, y, c = _place()
        me, sibling = (x, y, c), (x, y, 1 - c)
        chips = [(1 - x, y), (x, 1 - y), (1 - x, 1 - y)]

        def src(a):
            idx, r0, rows = pieces[a]
            return ins[idx] if r0 is None else ins[idx].at[pl.ds(r0, rows)]

        def slab(a, p):
            return outs[a].at[4 * p[0] + 2 * p[1] + p[2]]

        def copy(a, k, block, to, own=False):
            return pltpu.make_async_remote_copy(
                src_ref=src(a) if own else slab(a, block), dst_ref=slab(a, block),
                send_sem=send_sems.at[a, k], recv_sem=recv_sems.at[a, k], device_id=to, device_id_type=MESH)

        mine = [pltpu.make_async_copy(src(a), slab(a, me), local_sems.at[a]) for a in range(n_p)]
        for cp in mine:
            cp.start()
        first = []
        for a in range(n_p):
            first.append(copy(a, 0, me, sibling, own=True))
            first += [copy(a, 1 + j, me, (*chip, c), own=True) for j, chip in enumerate(chips)]
        for cp in first:
            cp.start()
        passed = []
        for a in range(n_p):
            for j, chip in enumerate(chips):
                copy(a, 1 + j, (*chip, c), me).wait_recv()
                cp = copy(a, 4 + j, (*chip, c), sibling)
                cp.start()
                passed.append(cp)
        for a in range(n_p):
            copy(a, 0, sibling, me).wait_recv()
            for j, chip in enumerate(chips):
                copy(a, 4 + j, (*chip, 1 - c), me).wait_recv()
        for cp in first + passed:
            cp.wait_send()
        for cp in mine:
            cp.wait()
        for cp in swapped:
            cp.wait()

    def out_shape(piece):
        idx, r0, rows = piece
        a = arrs[idx]
        return jax.ShapeDtypeStruct((N_DEV,) + (a.shape if r0 is None else (rows,) + a.shape[1:]), a.dtype)

    hbm = pl.BlockSpec(memory_space=pl.ANY)
    out_shapes = [out_shape(p) for p in pieces]
    scratch = [pltpu.SemaphoreType.DMA((n_p, 7)), pltpu.SemaphoreType.DMA((n_p, 7)), pltpu.SemaphoreType.DMA((n_p,))]
    args = list(arrs)
    if swap is not None:
        a, _, n = swap
        args.append(a)
        out_shapes.append(jax.ShapeDtypeStruct((n,) + a.shape[1:], a.dtype))
        scratch += [pltpu.SemaphoreType.DMA((n,)), pltpu.SemaphoreType.DMA((n,))]
    return pl.pallas_call(
        body, name=name, in_specs=[hbm] * len(args), out_specs=[hbm] * len(out_shapes), out_shape=out_shapes,
        scratch_shapes=scratch,
    )(*args)


def _flip(v, f):
    return 1 - v if f else v


def _gather_exchange(arrs, pieces):
    n_peers = N_DEV - 1

    def build(ins, outs, send_sems, recv_sems, local_sems):
        x, y, c = _place()
        flips = [(fx, fy, fc) for fx in (0, 1) for fy in (0, 1) for fc in (0, 1) if fx or fy or fc]
        remote, local = [], []
        for a, (idx, r0, rows) in enumerate(pieces):
            src = ins[idx] if r0 is None else ins[idx].at[pl.ds(r0, rows)]
            dst = outs[a].at[4 * x + 2 * y + c]
            remote += [pltpu.make_async_remote_copy(
                src_ref=src, dst_ref=dst, send_sem=send_sems.at[n_peers * a + k], recv_sem=recv_sems.at[n_peers * a + k],
                device_id=(_flip(x, fx), _flip(y, fy), _flip(c, fc)), device_id_type=MESH)
                for k, (fx, fy, fc) in enumerate(flips)]
            local.append(pltpu.make_async_copy(src, dst, local_sems.at[a]))
        return remote, local

    def out_shape(piece):
        idx, r0, rows = piece
        a = arrs[idx]
        return jax.ShapeDtypeStruct((N_DEV,) + (a.shape if r0 is None else (rows,) + a.shape[1:]), a.dtype)

    return _Exchange(arrs, [out_shape(p) for p in pieces], n_peers * len(pieces), build, n_local=len(pieces))


def _swap_core_copies(g_ref, r_ref, send_sems, recv_sems):
    x, y, c = _place()
    return [pltpu.make_async_remote_copy(
        src_ref=g_ref.at[2 * k + (1 - c)], dst_ref=r_ref.at[k], send_sem=send_sems.at[k], recv_sem=recv_sems.at[k],
        device_id=(x, y, 1 - c), device_id_type=MESH) for k in range(4)]


def _swap_chip_copies(p_ref, r_ref, send_sems, recv_sems):
    x, y, c = _place()
    chips = [(1 - x, y), (x, 1 - y), (1 - x, 1 - y)]
    return [pltpu.make_async_remote_copy(
        src_ref=p_ref.at[2 * px + py], dst_ref=r_ref.at[j], send_sem=send_sems.at[j], recv_sem=recv_sems.at[j],
        device_id=(px, py, c), device_id_type=MESH) for j, (px, py) in enumerate(chips)]


def _swap_exchange(a, copies, n):
    return _Exchange([a], [jax.ShapeDtypeStruct((n,) + a.shape[1:], a.dtype)], n,
                     lambda ins, outs, send_sems, recv_sems, local_sems: (copies(ins[0], outs[0], send_sems, recv_sems), []))


def _swap_call(a, copies, n, name):
    def body(a_ref, r_ref, send_sems, recv_sems):
        cps = copies(a_ref, r_ref, send_sems, recv_sems)
        for cp in cps:
            cp.start()
        for cp in cps:
            cp.wait()

    hbm = pl.BlockSpec(memory_space=pl.ANY)
    return pl.pallas_call(
        body, name=name, in_specs=[hbm], out_specs=hbm, out_shape=jax.ShapeDtypeStruct((n,) + a.shape[1:], a.dtype),
        scratch_shapes=[pltpu.SemaphoreType.DMA((n,)), pltpu.SemaphoreType.DMA((n,))],
    )(a)


def _swap_chip_start(p, rider):
    n = 3
    land = lax.empty((n,) + p.shape[1:], p.dtype)

    def body(p_ref, land_ref, rider_ref, send_sems, recv_sems, p_thru, land_thru, rider_thru):
        for cp in _swap_chip_copies(p_ref, land_ref, send_sems, recv_sems):
            cp.start()

    hbm, sem = pl.BlockSpec(memory_space=pltpu.HBM), pl.BlockSpec(memory_space=pltpu.SEMAPHORE)
    return pl.pallas_call(
        body, name="rs_swap_chip_start",
        out_shape=(pltpu.SemaphoreType.DMA((n,)), pltpu.SemaphoreType.DMA((n,)), pltpu.HBM(p.shape, p.dtype),
                   pltpu.HBM(land.shape, land.dtype), pltpu.HBM(rider.shape, rider.dtype)),
        in_specs=(hbm, hbm, hbm), out_specs=(sem, sem, hbm, hbm, hbm), input_output_aliases={0: 2, 1: 3, 2: 4},
        compiler_params=pltpu.CompilerParams(has_side_effects=pltpu.SideEffectType.DATAFLOW_SIDE_EFFECTING),
    )(pltpu.with_memory_space_constraint(p, pltpu.HBM), pltpu.with_memory_space_constraint(land, pltpu.HBM),
      pltpu.with_memory_space_constraint(rider, pltpu.HBM))


def _swap_chip_wait(send_sems, recv_sems, p_thru, land_thru, *after):
    def body(p_ref, land_ref, send_sems, recv_sems, *rest):
        for cp in _swap_chip_copies(p_ref, land_ref, send_sems, recv_sems):
            cp.wait_send()
            cp.wait_recv()

    hbm, sem = pl.BlockSpec(memory_space=pltpu.HBM), pl.BlockSpec(memory_space=pltpu.SEMAPHORE)
    return pl.pallas_call(
        body, name="rs_swap_chip_wait",
        out_shape=(pltpu.HBM(p_thru.shape, p_thru.dtype), pltpu.HBM(land_thru.shape, land_thru.dtype)),
        in_specs=(hbm, hbm, sem, sem) + (pl.BlockSpec(memory_space=pl.ANY),) * len(after), out_specs=(hbm, hbm),
        input_output_aliases={0: 0, 1: 1},
        compiler_params=pltpu.CompilerParams(has_side_effects=pltpu.SideEffectType.DATAFLOW_SIDE_EFFECTING),
    )(p_thru, land_thru, send_sems, recv_sems, *after)[1]


ADD_ROWS = 184


def _pair_add_call(g, r, c, chip):
    _, rows, width = g.shape

    def body(c_ref, chip_ref, g_ref, r_ref, o_ref):
        o_ref[...] = (g_ref[...] + r_ref[...]).astype(BF16)

    def other(j, chip_ref):
        return j + (j >= chip_ref[0]).astype(jnp.int32)

    return pl.pallas_call(
        body, name="rs_pair_add",
        grid_spec=pltpu.PrefetchScalarGridSpec(
            num_scalar_prefetch=2, grid=(3, rows // ADD_ROWS),
            in_specs=[pl.BlockSpec((None, ADD_ROWS, width),
                                   lambda j, i, c_ref, chip_ref: (2 * other(j, chip_ref) + c_ref[0], i, 0)),
                      pl.BlockSpec((None, ADD_ROWS, width), lambda j, i, c_ref, chip_ref: (other(j, chip_ref), i, 0))],
            out_specs=pl.BlockSpec((None, ADD_ROWS, width), lambda j, i, c_ref, chip_ref: (other(j, chip_ref), i, 0))),
        out_shape=jax.ShapeDtypeStruct((4, rows, width), BF16),
        compiler_params=_params(dimension_semantics=("arbitrary", "arbitrary")),
    )(c, chip, g, r)


def _adam(w, g, m, v):
    mn = ADAM_B1 * m + (1.0 - ADAM_B1) * g
    vn = ADAM_B2 * v + (1.0 - ADAM_B2) * (g * g)
    m_hat = mn / (1.0 - ADAM_B1 ** ADAM_STEP)
    v_hat = vn / (1.0 - ADAM_B2 ** ADAM_STEP)
    return -ADAM_LR * (m_hat / (jnp.sqrt(v_hat) + ADAM_EPS) + ADAM_WD * w), mn, vn


def _chip_add_adamw_call(slabs, from_sibling, from_chips, me, chip, w, m, v):
    _, rows, width = slabs.shape

    def body(me_ref, chip_ref, own_ref, sib_ref, r_ref, w_ref, m_ref, v_ref, g_ref, d_ref, mo_ref, vo_ref):
        g = own_ref[...] + sib_ref[...]
        for j in range(3):
            g = g + r_ref[j].astype(F32)
        g_ref[...] = g
        d_ref[...], mo_ref[...], vo_ref[...] = _adam(w_ref[...], g, m_ref[...], v_ref[...])

    spec = pl.BlockSpec((ADD_ROWS, width), lambda i, me_ref, chip_ref: (i, 0))
    return pl.pallas_call(
        body, name="rs_chip_add_adamw",
        grid_spec=pltpu.PrefetchScalarGridSpec(
            num_scalar_prefetch=2, grid=(rows // ADD_ROWS,),
            in_specs=[pl.BlockSpec((None, ADD_ROWS, width), lambda i, me_ref, chip_ref: (me_ref[0], i, 0)),
                      pl.BlockSpec((None, ADD_ROWS, width), lambda i, me_ref, chip_ref: (chip_ref[0], i, 0)),
                      pl.BlockSpec((3, ADD_ROWS, width), lambda i, me_ref, chip_ref: (0, i, 0)), spec, spec, spec],
            out_specs=[spec] * 4),
        out_shape=[jax.ShapeDtypeStruct((rows, width), F32)] * 4,
        compiler_params=_params(dimension_semantics=("arbitrary",)),
    )(me, chip, slabs, from_sibling, from_chips, w, m, v)


def _adamw_whole_call(params, name):
    n = len(params)

    def body(*refs):
        ins, outs = refs[:4 * n], refs[4 * n:]
        for a in range(n):
            w_ref, g_ref, m_ref, v_ref = ins[4 * a:4 * a + 4]
            outs[3 * a][...], outs[3 * a + 1][...], outs[3 * a + 2][...] = _adam(w_ref[...], g_ref[...], m_ref[...],
                                                                              v_ref[...])

    flat = [a for p in params for a in p]
    shapes = [jax.ShapeDtypeStruct(p[0].shape, F32) for p in params for _ in range(3)]
    out = pl.pallas_call(body, name=name, out_shape=shapes, compiler_params=_params())(*flat)
    return [tuple(out[3 * a:3 * a + 3]) for a in range(n)]


GATHERED_ACCS = (("dva", (8, CONV_W)), ("dv1", (8, D_MODEL)), ("dv2", (8, D_MODEL)), ("dcf", (8, D_FF_PAD)),
                 ("dcw", (HALO, CONV_W)), ("dwcat", (CHUNK, GMLP_HEADS * CHUNK)), ("dmsum", (CHUNK, GMLP_W)))
VEC_A = ("conv_a_b", "ln_a_g", "ln_a_b", "ln_v_g", "ln_v_b")
REP_IN_KERNEL = VEC_A + ("ln1_g", "ln1_b", "ln2_g", "ln2_b", "w_s", "b_s")


def _replicated_update_call(gathered, p, mom_m, mom_v):
    n_acc = len(GATHERED_ACCS)
    n_rep = len(REP_IN_KERNEL)

    def body(*refs):
        acc_refs = refs[:DEPTH * n_acc]
        wmv = refs[DEPTH * n_acc:DEPTH * n_acc + 3 * n_rep]
        outs = refs[DEPTH * n_acc + 3 * n_rep:]
        out_par = {nm: outs[4 * a:4 * a + 4] for a, nm in enumerate(REP_IN_KERNEL)}
        out_dcf = outs[4 * n_rep:4 * n_rep + DEPTH]
        out_dcw = outs[4 * n_rep + DEPTH:4 * n_rep + 2 * DEPTH]
        par = {nm: wmv[3 * a:3 * a + 3] for a, nm in enumerate(REP_IN_KERNEL)}
        tril = (lax.broadcasted_iota(jnp.int32, (CHUNK, CHUNK), 0) >= lax.broadcasted_iota(jnp.int32, (CHUNK, CHUNK), 1))
        head = lax.broadcasted_iota(jnp.int32, (8, GMLP_W), 0) * HEAD_DIM
        lane = lax.broadcasted_iota(jnp.int32, (8, GMLP_W), 1)
        sel = jnp.where((lane >= head) & (lane < head + HEAD_DIM), 1.0, 0.0)

        def update(nm, idx, g):
            w_ref, m_ref, v_ref = par[nm]
            d, mn, vn = _adam(w_ref[idx], g, m_ref[idx], v_ref[idx])
            g_ref, d_ref, mo_ref, vo_ref = out_par[nm]
            g_ref[idx] = g
            d_ref[idx] = d
            mo_ref[idx] = mn
            vo_ref[idx] = vn

        for l in range(DEPTH):
            tot = {}
            for a, (nm, _) in enumerate(GATHERED_ACCS):
                ref = acc_refs[l * n_acc + a]
                s = ref[0]
                for j in range(1, N_DEV):
                    s = s + ref[j]
                tot[nm] = s
            out_dcf[l][...] = tot["dcf"]
            out_dcw[l][...] = tot["dcw"]
            row = (slice(l, l + 1), slice(None))
            for k, nm in enumerate(VEC_A):
                update(nm, row, tot["dva"][k:k + 1, :])
            update("ln1_g", row, tot["dv1"][0:1, :])
            update("ln1_b", row, tot["dv1"][1:2, :])
            update("ln2_g", row, tot["dv2"][0:1, :])
            update("ln2_b", row, tot["dv2"][1:2, :])
            for h in range(GMLP_HEADS):
                gw = jnp.where(tril, tot["dwcat"][:, h * CHUNK:(h + 1) * CHUNK], 0.0)
                update("w_s", (l, h), gw)
            gb = lax.dot_general(sel, tot["dmsum"], (((1,), (1,)), ((), ())), precision=lax.Precision.HIGHEST,
                                 preferred_element_type=F32)
            for h in range(GMLP_HEADS):
                update("b_s", (l, slice(h, h + 1), slice(None)), gb[h:h + 1, :])

    ins = [gathered[l][nm] for l in range(DEPTH) for nm, _ in GATHERED_ACCS]
    ins += [t[nm] for nm in REP_IN_KERNEL for t in (p, mom_m, mom_v)]
    shapes = [jax.ShapeDtypeStruct(p[nm].shape, F32) for nm in REP_IN_KERNEL for _ in range(4)]
    shapes += [jax.ShapeDtypeStruct((8, D_FF_PAD), F32)] * DEPTH + [jax.ShapeDtypeStruct((HALO, CONV_W), F32)] * DEPTH
    out = pl.pallas_call(body, name="replicated_update", out_shape=shapes, compiler_params=_params())(*ins)
    res = [{nm: out[4 * a + k] for a, nm in enumerate(REP_IN_KERNEL)} for k in range(4)]
    return res, out[4 * n_rep:4 * n_rep + DEPTH], out[4 * n_rep + DEPTH:]


BLOCK_ROWS = (("w_in", IN_W // N_DEV), ("w_out", D_MODEL // N_DEV), ("w_up", 2 * FF_GROUP_PAD),
              ("w_down", FF_GROUP_PAD), ("w_mk", D_MODEL // N_DEV // MEM_FOLD), ("w_mv", D_MODEL // N_DEV // MEM_FOLD))
LAYER_ROWS = sum(r for _, r in BLOCK_ROWS)
assert LAYER_ROWS % ADD_ROWS == 0 and all(r % 16 == 0 for _, r in BLOCK_ROWS)


def _row_off(name):
    off = 0
    for nm, r in BLOCK_ROWS:
        if nm == name:
            return off
        off += r
    raise KeyError(name)


def _to_rows(name, a):
    if name == "w_in":
        return a.T
    if name == "w_up":
        t = a.T.reshape(2, FF_GROUP, D_MODEL)
        return jnp.pad(t, ((0, 0), (0, FF_GROUP_PAD - FF_GROUP), (0, 0))).reshape(2 * FF_GROUP_PAD, D_MODEL)
    if name == "w_down":
        return jnp.pad(a, ((0, FF_GROUP_PAD - FF_GROUP), (0, 0)))
    if name == "w_out":
        return a
    return a.reshape(-1, BLOB_LANES)


def _from_rows(name, r):
    if name == "w_in":
        return r.T
    if name == "w_up":
        return r.reshape(2, FF_GROUP_PAD, D_MODEL)[:, :FF_GROUP].reshape(2 * FF_GROUP, D_MODEL).T
    if name == "w_down":
        return r[:FF_GROUP]
    if name == "w_out":
        return r
    return r.reshape(D_MODEL // N_DEV, XATTN_W)


def _blob(tree, l):
    return jnp.concatenate([_to_rows(nm, tree[nm][l]) for nm, _ in BLOCK_ROWS], axis=0)


def _unblob(blobs):
    return {nm: jnp.stack([_from_rows(nm, b[_row_off(nm):_row_off(nm) + r]) for b in blobs]) for nm, r in BLOCK_ROWS}


def _ff_interleave(a):
    lead = a.shape[:-1]
    t = a.reshape(lead + (N_DEV, FF_GROUP))
    return jnp.pad(t, [(0, 0)] * len(lead) + [(0, 0), (0, FF_GROUP_PAD - FF_GROUP)]).reshape(lead + (D_FF_PAD,))


def _ff_deinterleave(a):
    lead = a.shape[:-1]
    return a.reshape(lead + (N_DEV, FF_GROUP_PAD))[..., :FF_GROUP].reshape(lead + (D_FF,))


def _head_table():
    hd = jnp.arange(XATTN_W) // HEAD_DIM
    return (hd[None, :] == jnp.arange(XATTN_HEADS)[:, None]).astype(F32)


def _mixer_operands(mat, conv_a_w, p, l, memq):
    w = {}
    w["wint"] = mat["w_in"]
    w["win"] = mat["w_in"].T
    w["wout"] = mat["w_out"]
    w["woutt"] = mat["w_out"].T
    w["cw"] = conv_a_w
    zeros = jnp.zeros((3, CONV_W), F32)
    w["va"] = jnp.concatenate([p[nm][l][None] for nm in VEC_A] + [zeros], axis=0)
    tril = jnp.tril(jnp.ones((CHUNK, CHUNK), F32))
    w["wcat"] = (p["w_s"][l] * tril[None]).transpose(1, 0, 2).reshape(CHUNK, GMLP_HEADS * CHUNK).astype(BF16)
    w["wcatt"] = w["wcat"].T
    w["bfull"] = jnp.repeat(p["b_s"][l].T, HEAD_DIM, axis=1)
    kh, vh = _mem_proj_call(memq, mat["w_mk"], mat["w_mv"])
    hm = _head_table()
    scale = 1.0 / math.sqrt(HEAD_DIM)
    w["kt"] = (kh.T[:, None, :] * hm.T[:, :, None] * scale).reshape(XATTN_W, XATTN_HEADS * N_MEM).astype(BF16)
    w["ktt"] = w["kt"].T
    w["vm"] = (hm[:, None, :] * vh[None]).reshape(XATTN_HEADS * N_MEM, XATTN_W).astype(BF16)
    w["vmt"] = w["vm"].T
    zeros = jnp.zeros((6, D_MODEL), F32)
    w["v1"] = jnp.concatenate([p["ln1_g"][l][None], p["ln1_b"][l][None], zeros], axis=0)
    return w


def _ffn_operands(w_up, w_down, conv_f_w, p, l):
    w = {}
    w["wgt"] = w_up[:D_FF_PAD]
    w["wvt"] = w_up[D_FF_PAD:]
    w["wg"] = w["wgt"].T
    w["wv"] = w["wvt"].T
    w["wdown"] = w_down
    w["wdownt"] = w_down.T
    zeros = jnp.zeros((6, D_MODEL), F32)
    w["v2"] = jnp.concatenate([p["ln2_g"][l][None], p["ln2_b"][l][None], zeros], axis=0)
    w["cf"] = jnp.concatenate([conv_f_w, _ff_interleave(p["conv_f_b"][l][None]), jnp.zeros((4, D_FF_PAD), F32)], axis=0)
    return w


TS_MIXER = 256
TS_FFN = 256
TS_PROJ = 512
CONV_A_SHARD = CONV_W // N_DEV


def kernel(x, mem, w_in, conv_a_w, conv_a_b, ln_a_g, ln_a_b, ln_v_g, ln_v_b, w_s, b_s, w_mk, w_mv, w_out, ln1_g, ln1_b, w_up, conv_f_w, conv_f_b, w_down, ln2_g, ln2_b, loss_target, m_w_in, m_conv_a_w, m_conv_a_b, m_ln_a_g, m_ln_a_b, m_ln_v_g, m_ln_v_b, m_w_s, m_b_s, m_w_mk, m_w_mv, m_w_out, m_ln1_g, m_ln1_b, m_w_up, m_conv_f_w, m_conv_f_b, m_w_down, m_ln2_g, m_ln2_b, v_w_in, v_conv_a_w, v_conv_a_b, v_ln_a_g, v_ln_a_b, v_ln_v_g, v_ln_v_b, v_w_s, v_b_s, v_w_mk, v_w_mv, v_w_out, v_ln1_g, v_ln1_b, v_w_up, v_conv_f_w, v_conv_f_b, v_w_down, v_ln2_g, v_ln2_b):
    given = dict(locals())
    p = {nm: given[nm] for nm in WEIGHTS}
    mom_m = {nm: given["m_" + nm] for nm in WEIGHTS}
    mom_v = {nm: given["v_" + nm] for nm in WEIGHTS}
    seq = x.shape[1]
    ts_m, ts_f, ts_p = min(TS_MIXER, seq), min(TS_FFN, seq), min(TS_PROJ, seq)
    cx, cy, cc = _place()
    me = 4 * cx + 2 * cy + cc

    blobs = [_blob(p, l) for l in range(DEPTH)]
    blobs_bf = [b.astype(BF16) for b in blobs]
    conv_a_tile = jnp.pad(conv_a_w, ((0, 0), (0, HALO - CONV_K), (0, 128 - CONV_A_SHARD)))
    conv_f_tile = jnp.pad(conv_f_w, ((0, 0), (0, 8 - FFN_CONV_K), (0, 384 - FF_GROUP)))
    rows = dict(BLOCK_ROWS)
    mixer_names = ("w_in", "w_out", "w_mk", "w_mv")
    pieces = [(0, _row_off(nm), rows[nm]) for nm in mixer_names] + [(1, None, 0), (2, None, 0)]
    first = _all_gather_call([blobs_bf[0], conv_a_tile, conv_f_tile], pieces, "gather_weights")
    conv_a_all, conv_f_all = first[len(mixer_names)], first[len(mixer_names) + 1]
    conv_a = [conv_a_all[:, l, :, :CONV_A_SHARD].transpose(1, 0, 2).reshape(HALO, CONV_W) for l in range(DEPTH)]
    conv_f = [conv_f_all[:, l, :FFN_CONV_K, :FF_GROUP_PAD].transpose(1, 0, 2).reshape(FFN_CONV_K, D_FF_PAD)
              for l in range(DEPTH)]
    memq = mem[0].reshape(N_MEM, D_MODEL // MEM_FOLD, MEM_FOLD).transpose(2, 0, 1).astype(BF16)

    def gather_of(l, names):
        return _gather_exchange([blobs_bf[l]], [(0, _row_off(nm), rows[nm]) for nm in names])

    def full(pieces):
        return [g.reshape(-1, BLOB_LANES) for g in pieces]

    ffn_names = ("w_up", "w_down")
    ops0 = _mixer_operands(dict(zip(mixer_names, full(first[:len(mixer_names)]))), conv_a[0], p, 0, memq)
    mixer_saved = ("hb", "z1", "x1", "a1", "cat", "p", "gs", "rv")
    mixed0, ffn0 = _mixer_fwd_call(x[0], ops0, ts_m, gather_of(0, ffn_names))
    ops0.update(_ffn_operands(*full(ffn0), conv_f[0], p, 0))
    saved = [dict(zip(mixer_saved, mixed0), x=x[0])]
    (ug, uv, sl, dsl, z2, x2), all1 = _ffn_fwd_call(saved[0]["x1"], ops0, ts_f, gather_of(1, mixer_names + ffn_names))
    saved[0].update(ug=ug, uv=uv, sl=sl, dsl=dsl, z2=z2)
    all1 = dict(zip(mixer_names + ffn_names, full(all1)))
    ops1 = _mixer_operands(all1, conv_a[1], p, 1, memq)
    ops1.update(_ffn_operands(all1["w_up"], all1["w_down"], conv_f[1], p, 1))
    mixed1, _ = _mixer_fwd_call(x2, ops1, ts_m)
    saved.append(dict(zip(mixer_saved, mixed1), x=x2))
    (ug, uv, sl, dsl, z2), _ = _ffn_fwd_call(saved[1]["x1"], ops1, ts_f, want_x2=False)
    saved[1].update(ug=ug, uv=uv, sl=sl, dsl=dsl, z2=z2)
    ops = [ops0, ops1]

    hm = _head_table()
    core_id = cc.reshape(1).astype(jnp.int32)
    me_id, chip_id = me.reshape(1).astype(jnp.int32), (2 * cx + cy).reshape(1).astype(jnp.int32)
    slabs = [lax.empty((N_DEV, LAYER_ROWS, BLOB_LANES), F32) for _ in range(DEPTH)]
    accs, gathered_accs = [None] * DEPTH, [None] * DEPTH
    acc_names = [nm for nm, _ in GATHERED_ACCS]
    whole = [(a, None, 0) for a in range(len(acc_names))]

    def acc_list(l):
        return [accs[l][nm] for nm in acc_names]

    from_sibling, from_chips = [None] * DEPTH, [None] * DEPTH
    dx = loss_target[0]
    loss = None
    for l in reversed(range(DEPTH)):
        s, w = saved[l], ops[l]
        last = l == DEPTH - 1
        ride = None if last else _swap_exchange(slabs[l + 1], _swap_core_copies, 4)
        (sl, dug, duv, dz2, dcf, dv2, loss_acc), got = _ffn_bwd_call(
            dx, s["z2"], s["ug"], s["uv"], s["sl"], s["dsl"], w, ts_f, last, slabs[l], _row_off("w_down"), ride)
        if last:
            loss = loss_acc[0, 0]
        else:
            from_sibling[l + 1] = got[0]
            chip_sum = _pair_add_call(slabs[l + 1], from_sibling[l + 1], core_id, chip_id)
        off_up = _row_off("w_up")
        ride = None if last else _gather_exchange(acc_list(l + 1), whole)
        (sl, dxa), got = _proj_bwd_call(dug, w["wgt"], s["x1"], dz2, ALPHA, ts_p, "up_gate_bwd", sl, off_up, 0, 4, ride)
        if not last:
            gathered_accs[l + 1] = dict(zip(acc_names, got))
        ride = None if last else _swap_exchange(chip_sum, _swap_chip_copies, 3)
        (sl, dx1), got = _proj_bwd_call(duv, w["wvt"], s["x1"], dxa, 1.0, ts_p, "up_val_bwd", sl, off_up, 4, 4, ride)
        if not last:
            from_chips[l + 1] = got[0]
        (sl, dx, dkt, dvm, dwcat, dmsum, dva, dcw, dv1) = _mixer_bwd_call(
            dx1, s["z1"], s["hb"], s["a1"], s["cat"], s["p"], s["gs"], s["rv"], s["x"], w, ts_m, sl, _row_off("w_out"),
            _row_off("w_in"))
        dkh = jnp.einsum("hd,dhm->md", hm, dkt.reshape(XATTN_W, XATTN_HEADS, N_MEM)) * (1.0 / math.sqrt(HEAD_DIM))
        dvh = jnp.einsum("hd,hmd->md", hm, dvm.reshape(XATTN_HEADS, N_MEM, XATTN_W))
        slabs[l] = _mem_proj_bwd_call(memq, dkh, dvh, sl, _row_off("w_mk"), _row_off("w_mv"))
        accs[l] = dict(dva=dva, dv1=dv1, dv2=dv2, dcf=dcf, dcw=dcw, dwcat=dwcat, dmsum=dmsum)
    grad_x = dx[None]
    *got, from_sibling[0] = _all_gather_call(acc_list(0), whole, "gather_small_grads",
                                            swap=(slabs[0], _swap_core_copies, 4))
    gathered_accs[0] = dict(zip(acc_names, got))
    chip_sum = _pair_add_call(slabs[0], from_sibling[0], core_id, chip_id)
    in_flight = _swap_chip_start(chip_sum, from_chips[1])
    from_chips[1] = in_flight[4]

    def final(l):
        return _chip_add_adamw_call(slabs[l], from_sibling[l], from_chips[l], me_id, chip_id, blobs[l], _blob(mom_m, l),
                                    _blob(mom_v, l))

    per_layer = [None, final(1)]
    rep, dcf_sum, dcw_sum = _replicated_update_call(gathered_accs, p, mom_m, mom_v)
    from_chips[0] = _swap_chip_wait(*in_flight[:4], per_layer[1][0], dcf_sum[0])
    per_layer[0] = final(0)
    outs = [_unblob([per_layer[l][k] for l in range(DEPTH)]) for k in range(4)]
    for k in range(4):
        outs[k].update(rep[k])
    dcf_sum, dcw_sum = jnp.stack(dcf_sum), jnp.stack(dcw_sum)
    zero = jnp.zeros((), jnp.int32)
    g_conv_a_w = lax.dynamic_slice(dcw_sum, (zero, zero, CONV_A_SHARD * me), (DEPTH, CONV_K, CONV_A_SHARD))
    g_conv_f_w = lax.dynamic_slice(dcf_sum, (zero, zero, FF_GROUP_PAD * me), (DEPTH, FFN_CONV_K, FF_GROUP))
    g_conv_f_b = _ff_deinterleave(dcf_sum[:, FFN_CONV_K])
    conv_grads = dict(conv_a_w=g_conv_a_w, conv_f_w=g_conv_f_w, conv_f_b=g_conv_f_b)
    conv_names = tuple(conv_grads)
    upd = _adamw_whole_call([(p[nm], conv_grads[nm], mom_m[nm], mom_v[nm]) for nm in conv_names], "adamw_conv")
    for nm, (d, mn, vn) in zip(conv_names, upd):
        outs[0][nm], outs[1][nm], outs[2][nm], outs[3][nm] = conv_grads[nm], d, mn, vn

    loss = lax.psum(loss, ("x", "y", "c"))
    return (loss, grad_x, *[outs[0][nm] for nm in WEIGHTS], *[outs[1][nm] for nm in WEIGHTS],
            *[outs[2][nm] for nm in WEIGHTS], *[outs[3][nm] for nm in WEIGHTS])
```

```python
import math

import jax
import jax.numpy as jnp
from jax import lax
from jax.experimental import pallas as pl
from jax.experimental.pallas import tpu as pltpu

F32 = jnp.float32
BF16 = jnp.bfloat16

DEPTH = 2
D_MODEL = 1024
CONV_W = 384
GMLP_W = 384
XATTN_W = 256
HEAD_DIM = 64
GMLP_HEADS = 6
XATTN_HEADS = 4
IN_W = 1792
CONV_K = 31
CHUNK = 128
N_MEM = 256
D_FF = 2752
D_FF_PAD = 2816
FFN_CONV_K = 3
ALPHA = (2.0 * DEPTH) ** 0.25
LN_EPS = 1e-5
N_DEV = 8

ADAM_LR = 0.001
ADAM_B1 = 0.9
ADAM_B2 = 0.999
ADAM_EPS = 1e-08
ADAM_WD = 0.01
ADAM_STEP = 10

HALO = 32
CONV_ROWS = 32
V7X_VMEM_BYTES = 64 * 1024 * 1024
VMEM_LIMIT = V7X_VMEM_BYTES - 8 * 1024 * 1024
BLOB_LANES = 1024

MESH = pl.DeviceIdType.MESH

WEIGHTS = ("w_in", "conv_a_w", "conv_a_b", "ln_a_g", "ln_a_b", "ln_v_g", "ln_v_b", "w_s", "b_s", "w_mk", "w_mv",
           "w_out", "ln1_g", "ln1_b", "w_up", "conv_f_w", "conv_f_b", "w_down", "ln2_g", "ln2_b")


def _params(**kw):
    return pltpu.CompilerParams(vmem_limit_bytes=VMEM_LIMIT, **kw)


def _const(shape):
    nd = len(shape)
    return pl.BlockSpec(shape, lambda i: (0,) * nd, pipeline_mode=pl.Buffered(1))


def _acc(shape):
    nd = len(shape)
    return pl.BlockSpec(shape, lambda i: (0,) * nd)


class _Exchange:
    def __init__(self, arrays, out_shapes, n_copies, build, n_local=1):
        self.arrays, self.out_shapes, self.n_copies, self.build = list(arrays), list(out_shapes), n_copies, build
        self.n_local = n_local


def _carry(core, n_in, n_out, exch, n_steps):
    if exch is None:
        return core
    nx_in, nx_out = len(exch.arrays), len(exch.out_shapes)

    def body(*refs):
        o0 = n_in + nx_in
        s0 = o0 + n_out + nx_out
        x_in, x_out, sems = refs[n_in:o0], refs[o0 + n_out:s0], refs[-3:]
        i = pl.program_id(0)

        @pl.when(i == 0)
        def _():
            remote, local = exch.build(x_in, x_out, *sems)
            for cp in remote + local:
                cp.start()

        core(*refs[:n_in], *refs[o0:o0 + n_out], *refs[s0:-3])

        @pl.when(i == n_steps - 1)
        def _():
            remote, local = exch.build(x_in, x_out, *sems)
            for cp in remote + local:
                cp.wait()

    return body


def _grid_call(core, name, n_steps, in_specs, out_specs, out_shape, scratch_shapes, args, aliases=None, exch=None):
    hbm = pl.BlockSpec(memory_space=pl.ANY)
    n_in, n_out = len(in_specs), len(out_specs)
    in_specs, out_specs, out_shape, scratch_shapes, args = (list(in_specs), list(out_specs), list(out_shape),
                                                            list(scratch_shapes), list(args))
    if exch is not None:
        in_specs += [hbm] * len(exch.arrays)
        out_specs += [hbm] * len(exch.out_shapes)
        out_shape += exch.out_shapes
        scratch_shapes += [pltpu.SemaphoreType.DMA((exch.n_copies,)), pltpu.SemaphoreType.DMA((exch.n_copies,)),
                           pltpu.SemaphoreType.DMA((exch.n_local,))]
        args += exch.arrays
    out = pl.pallas_call(
        _carry(core, n_in, n_out, exch, n_steps), name=name, grid=(n_steps,), in_specs=in_specs, out_specs=out_specs,
        out_shape=out_shape, scratch_shapes=scratch_shapes, input_output_aliases=aliases or {},
        compiler_params=_params(dimension_semantics=("arbitrary",)))(*args)
    return list(out[:n_out]), list(out[n_out:])


def _sigmoid(x):
    return 1.0 / (1.0 + jnp.exp(-x))


_GELU_C = math.sqrt(2.0 / math.pi)


def _gelu(x):
    x2 = x * x
    t = jnp.tanh(_GELU_C * (x + 0.044715 * x * x2))
    g = 0.5 * x * (1.0 + t)
    dg = 0.5 * (1.0 + t) + 0.5 * x * (1.0 - t * t) * (_GELU_C * (1.0 + 3.0 * 0.044715 * x2))
    return g, dg


def _ln_stats(z):
    mu = jnp.mean(z, axis=-1, keepdims=True)
    zc = z - mu
    var = jnp.mean(zc * zc, axis=-1, keepdims=True)
    r = lax.rsqrt(var + LN_EPS)
    return zc * r, r


def _ln_bwd(dy, xh, r, g):
    dxh = dy * g
    m1 = jnp.mean(dxh, axis=-1, keepdims=True)
    m2 = jnp.mean(dxh * xh, axis=-1, keepdims=True)
    return r * (dxh - m1 - xh * m2)


def _rowsum(x):
    return jnp.sum(x, axis=0, keepdims=True)


def _dot(a, b):
    return jnp.dot(a, b, preferred_element_type=F32)


def _dot_tn(a, b):
    return lax.dot_general(a, b, (((0,), (0,)), ((), ())), preferred_element_type=F32)


def _dot_nt(a, b):
    return lax.dot_general(a, b, (((1,), (1,)), ((), ())), preferred_element_type=F32)


def _shift_copies(buf, sh, rows):
    for b in range(1, 8):
        sh[b - 1, 0:rows, :] = buf[b:b + rows, :]


def _window(buf, sh, start):
    b = start % 8
    a = start - b
    return buf[a:a + CONV_ROWS, :] if b == 0 else sh[b - 1, a:a + CONV_ROWS, :]


def _conv31_fwd(buf, sh, w_ref, bias, out, ts):
    for r0 in range(0, ts, CONV_ROWS):
        acc = jnp.broadcast_to(bias, (CONV_ROWS, CONV_W))
        for k in range(CONV_K):
            acc = acc + w_ref[k:k + 1, :] * _window(buf, sh, r0 + HALO - (CONV_K - 1) + k)
        out[r0:r0 + CONV_ROWS, :] = acc


def _conv31_dx(dbuf, dsh, w_ref, out, ts):
    for r0 in range(0, ts, CONV_ROWS):
        acc = jnp.zeros((CONV_ROWS, CONV_W), F32)
        for k in range(CONV_K):
            acc = acc + w_ref[k:k + 1, :] * _window(dbuf, dsh, r0 + (CONV_K - 1) - k)
        out[r0:r0 + CONV_ROWS, :] = acc


def _conv31_dw(buf, sh, dbuf, acc, ts):
    for r0 in range(0, ts, CONV_ROWS):
        d = dbuf[r0:r0 + CONV_ROWS, :]
        for k in range(CONV_K):
            m = d * _window(buf, sh, r0 + HALO - (CONV_K - 1) + k)
            part = m[0:8, :]
            for q in range(8, CONV_ROWS, 8):
                part = part + m[q:q + 8, :]
            acc[k] += part


def _head_mask(width, h):
    lane = lax.broadcasted_iota(jnp.int32, (CHUNK, width), 1)
    return (lane >= h * HEAD_DIM) & (lane < (h + 1) * HEAD_DIM)


def _stack_heads(vn_c):
    return jnp.concatenate([jnp.where(_head_mask(GMLP_W, h), vn_c, 0.0) for h in range(GMLP_HEADS)], axis=0)


def _group_a_fwd(hf, buf, sh, a1_ref, cw_ref, va_ref, ts, conv=True):
    ha = hf[:, 0:CONV_W]
    sg = _sigmoid(hf[:, CONV_W:2 * CONV_W])
    buf[HALO:HALO + ts, :] = ha * sg
    _shift_copies(buf, sh, ts + HALO - 8)
    if conv:
        _conv31_fwd(buf, sh, cw_ref, va_ref[0:1, :], a1_ref, ts)
    a2h, ra = _ln_stats(a1_ref[...])
    a2 = a2h * va_ref[1:2, :] + va_ref[2:3, :]
    sa = _sigmoid(a2)
    return dict(ha=ha, sg=sg, a2h=a2h, ra=ra, a2=a2, sa=sa, a=a2 * sa)


GROUP_LANES = ((0, 2 * CONV_W), (2 * CONV_W, 2 * CONV_W + 2 * GMLP_W), (2 * CONV_W + 2 * GMLP_W, IN_W))


def _group_b_fwd(hf, va_ref, wcat_ref, bfull_ref, ts):
    hu = hf[:, 0:GMLP_W]
    hv = hf[:, GMLP_W:2 * GMLP_W]
    u, du = _gelu(hu)
    v, dv = _gelu(hv)
    vhat, rv = _ln_stats(v)
    vn = vhat * va_ref[3:4, :] + va_ref[4:5, :]
    stacks, mixed = [], []
    for c0 in range(0, ts, CHUNK):
        st = _stack_heads(vn[c0:c0 + CHUNK, :]).astype(BF16)
        stacks.append(st)
        mixed.append(_dot(wcat_ref[...], st) + bfull_ref[...])
    mixed = jnp.concatenate(mixed, axis=0) if len(mixed) > 1 else mixed[0]
    return dict(u=u, du=du, dv=dv, vhat=vhat, rv=rv, stacks=stacks, mixed=mixed, g=u * mixed)


GROUP_B_SAVED = ("u", "du", "dv", "vhat", "mixed")


def _group_c_fwd(qb, kt_ref, vm_ref):
    s_all = _dot(qb, kt_ref[...])
    ps = []
    for g in range(XATTN_HEADS):
        s = s_all[:, g * N_MEM:(g + 1) * N_MEM]
        e = jnp.exp(s - jnp.max(s, axis=-1, keepdims=True))
        ps.append(e / jnp.sum(e, axis=-1, keepdims=True))
    p_all = jnp.concatenate(ps, axis=1)
    pb = p_all.astype(BF16)
    return dict(qb=qb, p=p_all, pb=pb, c=_dot(pb, vm_ref[...]))


def _mixer_fwd_call(x, w, ts, exch=None):
    seq = x.shape[0]
    n = seq // ts

    def body(x_ref, win_ref, cw_ref, va_ref, wcat_ref, bfull_ref, kt_ref, vm_ref, wout_ref, v1_ref,
             hb_ref, z1_ref, x1_ref, a1buf, cat_ref, p_ref, gs_ref, rv_ref, x1b_ref, buf, sh):
        i = pl.program_id(0)

        @pl.when(i == 0)
        def _():
            buf[0:HALO, :] = jnp.zeros((HALO, CONV_W), F32)

        @pl.when(i > 0)
        def _():
            buf[0:HALO, :] = buf[ts:ts + HALO, :]

        xv = x_ref[...]
        hb = _dot(xv.astype(BF16), win_ref[...]).astype(BF16)
        hb_ref[...] = hb
        hf = hb.astype(F32)
        (a_lo, a_hi), (b_lo, b_hi), (c_lo, c_hi) = GROUP_LANES
        ga = _group_a_fwd(hf[:, a_lo:a_hi], buf, sh, a1buf, cw_ref, va_ref, ts)
        gb = _group_b_fwd(hf[:, b_lo:b_hi], va_ref, wcat_ref, bfull_ref, ts)
        gc = _group_c_fwd(hb[:, c_lo:c_hi], kt_ref, vm_ref)
        cat = jnp.concatenate([ga["a"], gb["g"], gc["c"]], axis=1).astype(BF16)
        cat_ref[...] = cat
        p_ref[...] = gc["pb"]
        gs_ref[...] = jnp.concatenate([gb[k] for k in GROUP_B_SAVED], axis=1).astype(BF16)
        rv_ref[...] = jnp.broadcast_to(gb["rv"], (ts, 128))
        z1 = ALPHA * xv + _dot(cat, wout_ref[...])
        z1_ref[...] = z1
        xh, _ = _ln_stats(z1)
        x1 = xh * v1_ref[0:1, :] + v1_ref[1:2, :]
        x1_ref[...] = x1
        x1b_ref[...] = x1.astype(BF16)

    row = lambda width: pl.BlockSpec((ts, width), lambda i: (i, 0))
    return _grid_call(
        body, "mixer_fwd", n,
        in_specs=[row(D_MODEL), _const((D_MODEL, IN_W)), _const((HALO, CONV_W)), _const((8, CONV_W)),
                  _const((CHUNK, GMLP_HEADS * CHUNK)), _const((CHUNK, GMLP_W)), _const((XATTN_W, XATTN_HEADS * N_MEM)),
                  _const((XATTN_HEADS * N_MEM, XATTN_W)), _const((D_MODEL, D_MODEL)), _const((8, D_MODEL))],
        out_specs=[row(IN_W), row(D_MODEL), row(D_MODEL), row(CONV_W), row(D_MODEL), row(XATTN_HEADS * N_MEM),
                   row(len(GROUP_B_SAVED) * GMLP_W), row(128), row(D_MODEL)],
        out_shape=[jax.ShapeDtypeStruct((seq, IN_W), BF16), jax.ShapeDtypeStruct((seq, D_MODEL), F32),
                   jax.ShapeDtypeStruct((seq, D_MODEL), F32), jax.ShapeDtypeStruct((seq, CONV_W), F32),
                   jax.ShapeDtypeStruct((seq, D_MODEL), BF16), jax.ShapeDtypeStruct((seq, XATTN_HEADS * N_MEM), BF16),
                   jax.ShapeDtypeStruct((seq, len(GROUP_B_SAVED) * GMLP_W), BF16),
                   jax.ShapeDtypeStruct((seq, 128), F32), jax.ShapeDtypeStruct((seq, D_MODEL), BF16)],
        scratch_shapes=[pltpu.VMEM((ts + HALO, CONV_W), F32), pltpu.VMEM((7, ts + HALO, CONV_W), F32)],
        args=(x, w["win"], w["cw"], w["va"], w["wcat"], w["bfull"], w["kt"], w["vm"], w["wout"], w["v1"]), exch=exch)


def _store_blocks(acc, slabs_ref, sems, row_off, rows, first_block, n_blocks):
    copies = [pltpu.make_async_copy(acc.at[pl.ds(q * rows, rows)], slabs_ref.at[first_block + q, pl.ds(row_off, rows)],
                                    sems.at[q]) for q in range(n_blocks)]
    for cp in copies:
        cp.start()
    for cp in copies:
        cp.wait()


def _mixer_bwd_call(dx1, z1, hb, a1, cat, p, gs, rv, x, w, ts, slabs, off_out, off_in):
    seq = dx1.shape[0]
    n = seq // ts
    halo_blocks = ts // HALO

    def body(slabs_in, dx1_ref, z1_ref, hb_ref, hprev_ref, a1_ref, cat_ref, p_ref, gs_ref, rv_ref, x_ref, cw_ref, va_ref,
             wcatt_ref, ktt_ref, vmt_ref, woutt_ref, wint_ref, v1_ref,
             slabs_ref, dx_ref, dkt_ref, dvm_ref, dwcat_ref, dmsum_ref, dva_ref, dcw_ref, dv1_ref,
             buf, dbuf, da0buf, dwout_ref, dwin_ref, sems, sh, dsh, dcw_acc):
        i = pl.program_id(0)

        @pl.when(i == 0)
        def _():
            for ref in (dwout_ref, dwin_ref, dkt_ref, dvm_ref, dwcat_ref, dmsum_ref, dva_ref, dcw_ref, dv1_ref, dcw_acc):
                ref[...] = jnp.zeros(ref.shape, F32)
            dbuf[ts:ts + HALO, :] = jnp.zeros((HALO, CONV_W), F32)

        @pl.when(i > 0)
        def _():
            dbuf[ts:ts + HALO, :] = dbuf[0:HALO, :]

        dx1v = dx1_ref[...]
        xh1, r1 = _ln_stats(z1_ref[...])
        dv1_ref[0:1, :] += _rowsum(dx1v * xh1)
        dv1_ref[1:2, :] += _rowsum(dx1v)
        dz1 = _ln_bwd(dx1v, xh1, r1, v1_ref[0:1, :])
        dmix = dz1.astype(BF16)

        hf = hb_ref[:, 0:2 * CONV_W].astype(F32)
        hp = hprev_ref[...].astype(F32)
        a0p = hp[:, 0:CONV_W] * _sigmoid(hp[:, CONV_W:2 * CONV_W])
        buf[0:HALO, :] = jnp.where(i == n - 1, 0.0, a0p)
        ga = _group_a_fwd(hf, buf, sh, a1_ref, cw_ref, va_ref, ts, conv=False)
        gb = {k: gs_ref[:, j * GMLP_W:(j + 1) * GMLP_W].astype(F32) for j, k in enumerate(GROUP_B_SAVED)}
        vn = gb["vhat"] * va_ref[3:4, :] + va_ref[4:5, :]
        pb = p_ref[...]
        gc = dict(qb=hb_ref[:, IN_W - XATTN_W:IN_W], pb=pb, p=pb.astype(F32))

        dwout_ref[...] += _dot_tn(cat_ref[...], dmix)
        dcat = _dot(dmix, woutt_ref[...])
        da = dcat[:, 0:CONV_W]
        dg = dcat[:, CONV_W:CONV_W + GMLP_W]
        dc = dcat[:, CONV_W + GMLP_W:D_MODEL].astype(BF16)

        dp = _dot(dc, vmt_ref[...])
        dvm_ref[...] += _dot_tn(gc["pb"], dc)
        dss = []
        for g in range(XATTN_HEADS):
            sl = slice(g * N_MEM, (g + 1) * N_MEM)
            pg = gc["p"][:, sl]
            dpg = dp[:, sl]
            dss.append(pg * (dpg - jnp.sum(dpg * pg, axis=-1, keepdims=True)))
        ds = jnp.concatenate(dss, axis=1).astype(BF16)
        dq = _dot(ds, ktt_ref[...])
        dkt_ref[...] += _dot_tn(gc["qb"], ds)

        dmixed = dg * gb["u"]
        dhu = dg * gb["mixed"] * gb["du"]
        dvns = []
        for j, c0 in enumerate(range(0, ts, CHUNK)):
            dm = dmixed[c0:c0 + CHUNK, :]
            dmb = dm.astype(BF16)
            dmsum_ref[...] += dm
            dwcat_ref[...] += _dot_nt(dmb, _stack_heads(vn[c0:c0 + CHUNK, :]).astype(BF16))
            dst = _dot(wcatt_ref[...], dmb)
            dvn_c = jnp.zeros((CHUNK, GMLP_W), F32)
            for h in range(GMLP_HEADS):
                dvn_c = dvn_c + jnp.where(_head_mask(GMLP_W, h), dst[h * CHUNK:(h + 1) * CHUNK, :], 0.0)
            dvns.append(dvn_c)
        dvn = jnp.concatenate(dvns, axis=0) if len(dvns) > 1 else dvns[0]
        dva_ref[3:4, :] += _rowsum(dvn * gb["vhat"])
        dva_ref[4:5, :] += _rowsum(dvn)
        dhv = _ln_bwd(dvn, gb["vhat"], rv_ref[:, 0:1], va_ref[3:4, :]) * gb["dv"]

        a2, sa = ga["a2"], ga["sa"]
        da2 = da * (sa * (1.0 + a2 * (1.0 - sa)))
        dva_ref[1:2, :] += _rowsum(da2 * ga["a2h"])
        dva_ref[2:3, :] += _rowsum(da2)
        da1 = _ln_bwd(da2, ga["a2h"], ga["ra"], va_ref[1:2, :])
        dva_ref[0:1, :] += _rowsum(da1)
        dbuf[0:ts, :] = da1
        _shift_copies(dbuf, dsh, ts + HALO - 8)
        _conv31_dw(buf, sh, dbuf, dcw_acc, ts)
        _conv31_dx(dbuf, dsh, cw_ref, da0buf, ts)
        da0 = da0buf[...]
        sg = ga["sg"]
        dha = da0 * sg
        dhg = da0 * ga["ha"] * sg * (1.0 - sg)

        dh = jnp.concatenate([dha, dhg, dhu, dhv, dq], axis=1).astype(BF16)
        dx_ref[...] = _dot(dh, wint_ref[...]) + ALPHA * dz1
        dwin_ref[...] += _dot_tn(dh, x_ref[...].astype(BF16))

        @pl.when(i == n - 1)
        def _():
            for k in range(CONV_K):
                dcw_ref[k:k + 1, :] = _rowsum(dcw_acc[k])
            _store_blocks(dwout_ref, slabs_ref, sems, off_out, D_MODEL // N_DEV, 0, N_DEV)
            _store_blocks(dwin_ref, slabs_ref, sems, off_in, IN_W // N_DEV, 0, N_DEV)

    rev = lambda width: pl.BlockSpec((ts, width), lambda i: (n - 1 - i, 0))
    prev = pl.BlockSpec((HALO, 2 * CONV_W), lambda i: (jnp.maximum((n - 1 - i) * halo_blocks - 1, 0), 0))
    hbm = pl.BlockSpec(memory_space=pl.ANY)
    hc = GMLP_HEADS * CHUNK
    am = XATTN_HEADS * N_MEM
    return pl.pallas_call(
        body, name="mixer_bwd", grid=(n,),
        in_specs=[hbm, rev(D_MODEL), rev(D_MODEL), rev(IN_W), prev, rev(CONV_W), rev(D_MODEL), rev(am),
                  rev(len(GROUP_B_SAVED) * GMLP_W), rev(128), rev(D_MODEL), _const((HALO, CONV_W)),
                  _const((8, CONV_W)), _const((hc, CHUNK)), _const((am, XATTN_W)), _const((XATTN_W, am)),
                  _const((D_MODEL, D_MODEL)), _const((IN_W, D_MODEL)), _const((8, D_MODEL))],
        out_specs=[hbm, rev(D_MODEL), _acc((XATTN_W, am)), _acc((am, XATTN_W)),
                   _acc((CHUNK, hc)), _acc((CHUNK, GMLP_W)), _acc((8, CONV_W)), _acc((HALO, CONV_W)),
                   _acc((8, D_MODEL))],
        out_shape=[jax.ShapeDtypeStruct(slabs.shape, F32), jax.ShapeDtypeStruct((seq, D_MODEL), F32),
                   jax.ShapeDtypeStruct((XATTN_W, am), F32),
                   jax.ShapeDtypeStruct((am, XATTN_W), F32), jax.ShapeDtypeStruct((CHUNK, hc), F32),
                   jax.ShapeDtypeStruct((CHUNK, GMLP_W), F32), jax.ShapeDtypeStruct((8, CONV_W), F32),
                   jax.ShapeDtypeStruct((HALO, CONV_W), F32), jax.ShapeDtypeStruct((8, D_MODEL), F32)],
        scratch_shapes=[pltpu.VMEM((ts + HALO, CONV_W), F32),
                        pltpu.VMEM((ts + HALO, CONV_W), F32), pltpu.VMEM((ts, CONV_W), F32),
                        pltpu.VMEM((D_MODEL, D_MODEL), F32), pltpu.VMEM((IN_W, D_MODEL), F32),
                        pltpu.SemaphoreType.DMA((N_DEV,)),
                        pltpu.VMEM((7, ts + HALO, CONV_W), F32), pltpu.VMEM((7, ts + HALO, CONV_W), F32),
                        pltpu.VMEM((CONV_K, 8, CONV_W), F32)],
        input_output_aliases={0: 0},
        compiler_params=_params(dimension_semantics=("arbitrary",)),
    )(slabs, dx1, z1, hb, hb, a1, cat, p, gs, rv, x, w["cw"], w["va"], w["wcatt"], w["ktt"], w["vmt"], w["woutt"],
      w["wint"], w["v1"])


FFN_HALO = 8
FF_GROUP = D_FF // N_DEV
FF_GROUP_PAD = D_FF_PAD // N_DEV


def _ffn_taps(ubuf, ts, lo, hi):
    return tuple(ubuf[FFN_HALO - (FFN_CONV_K - 1) + k:FFN_HALO - (FFN_CONV_K - 1) + k + ts, lo:hi]
                 for k in range(FFN_CONV_K))


def _ffn_gate(ubuf, cf_ref, ts, lo, hi):
    taps = _ffn_taps(ubuf, ts, lo, hi)
    g = cf_ref[3:4, lo:hi] + cf_ref[2:3, lo:hi] * taps[2]
    g = g + cf_ref[1:2, lo:hi] * taps[1]
    return g + cf_ref[0:1, lo:hi] * taps[0]


FFN_CHUNK = 256
FFN_CHUNKS = tuple((lo, lo + FFN_CHUNK) for lo in range(0, D_FF_PAD, FFN_CHUNK))


def _ffn_fwd_call(x1, w, ts, exch=None, want_x2=True):
    seq = x1.shape[0]
    n = seq // ts
    n_wide = 4

    def body(x1_ref, wg_ref, wv_ref, cf_ref, wdown_ref, v2_ref, ug_ref, uv_ref, sl_ref, dsl_ref, z2_ref, *rest):
        x2_ref = rest[0] if want_x2 else None
        ubuf, act_buf = rest[-2:]
        i = pl.program_id(0)

        @pl.when(i == 0)
        def _():
            ubuf[0:FFN_HALO, :] = jnp.zeros((FFN_HALO, D_FF_PAD), F32)

        @pl.when(i > 0)
        def _():
            ubuf[0:FFN_HALO, :] = ubuf[ts:ts + FFN_HALO, :]

        xv = x1_ref[...]
        xb = xv.astype(BF16)
        for lo, hi in FFN_CHUNKS:
            ug = _dot(xb, wg_ref[:, lo:hi]).astype(BF16)
            uv = _dot(xb, wv_ref[:, lo:hi]).astype(BF16)
            ug_ref[:, lo:hi] = ug
            uv_ref[:, lo:hi] = uv
            ubuf[FFN_HALO:FFN_HALO + ts, lo:hi] = ug.astype(F32)
            gate = _ffn_gate(ubuf, cf_ref, ts, lo, hi)
            sg = _sigmoid(gate)
            sl = gate * sg
            sl_ref[:, lo:hi] = sl.astype(BF16)
            dsl_ref[:, lo:hi] = (sg * (1.0 + gate * (1.0 - sg))).astype(BF16)
            act_buf[:, lo:hi] = (sl * uv.astype(F32)).astype(BF16)
        y = ALPHA * xv + _dot(act_buf[...], wdown_ref[...])
        z2_ref[...] = y
        if want_x2:
            xh, _ = _ln_stats(y)
            x2_ref[...] = xh * v2_ref[0:1, :] + v2_ref[1:2, :]

    row = lambda width: pl.BlockSpec((ts, width), lambda i: (i, 0))
    n_narrow = 2 if want_x2 else 1
    return _grid_call(
        body, "ffn_fwd" if want_x2 else "ffn_fwd_last", n,
        in_specs=[row(D_MODEL), _const((D_MODEL, D_FF_PAD)), _const((D_MODEL, D_FF_PAD)), _const((8, D_FF_PAD)),
                  _const((D_FF_PAD, D_MODEL)), _const((8, D_MODEL))],
        out_specs=[row(D_FF_PAD)] * n_wide + [row(D_MODEL)] * n_narrow,
        out_shape=[jax.ShapeDtypeStruct((seq, D_FF_PAD), BF16)] * n_wide
                  + [jax.ShapeDtypeStruct((seq, D_MODEL), F32)] * n_narrow,
        scratch_shapes=[pltpu.VMEM((ts + FFN_HALO, D_FF_PAD), F32), pltpu.VMEM((ts, D_FF_PAD), BF16)],
        args=(x1, w["wg"], w["wv"], w["cf"], w["wdown"], w["v2"]), exch=exch)


def _ffn_bwd_call(dx2_or_target, z2, ug, uv, sl, dsl, w, ts, last, slabs, row_off, exch=None):
    seq = z2.shape[0]
    n = seq // ts
    halo_blocks = ts // 16

    def body(slabs_in, dx2_ref, z2_ref, ug_ref, uv_ref, sl_ref, dsl_ref, uprev_ref, cf_ref, wdownt_ref, v2_ref,
             slabs_ref, dug_ref, duv_ref, dz2_ref, dcf_ref, dv2_ref, loss_ref,
             ubuf, dgbuf, dwacc, sems):
        i = pl.program_id(0)

        @pl.when(i == 0)
        def _():
            dwacc[...] = jnp.zeros(dwacc.shape, F32)
            dcf_ref[...] = jnp.zeros(dcf_ref.shape, F32)
            dv2_ref[...] = jnp.zeros(dv2_ref.shape, F32)
            loss_ref[...] = jnp.zeros(loss_ref.shape, F32)
            dgbuf[ts:ts + FFN_HALO, :] = jnp.zeros((FFN_HALO, D_FF_PAD), F32)

        @pl.when(i > 0)
        def _():
            dgbuf[ts:ts + FFN_HALO, :] = dgbuf[0:FFN_HALO, :]

        xh2, r2 = _ln_stats(z2_ref[...])
        if last:
            diff = xh2 * v2_ref[0:1, :] + v2_ref[1:2, :] - dx2_ref[...]
            loss_ref[...] += jnp.sum(diff * diff) * (0.5 / D_MODEL)
            dx2v = diff * (1.0 / D_MODEL)
        else:
            dx2v = dx2_ref[...]
        dv2_ref[0:1, :] += _rowsum(dx2v * xh2)
        dv2_ref[1:2, :] += _rowsum(dx2v)
        dz2 = _ln_bwd(dx2v, xh2, r2, v2_ref[0:1, :])
        dz2_ref[...] = dz2
        dy = dz2.astype(BF16)

        up = uprev_ref[...].astype(F32)[8:16, :]
        ubuf[0:FFN_HALO, :] = jnp.where(i == n - 1, 0.0, up)
        ubuf[FFN_HALO:FFN_HALO + ts, :] = ug_ref[...].astype(F32)
        for lo, hi in FFN_CHUNKS:
            taps = _ffn_taps(ubuf, ts, lo, hi)
            sl = sl_ref[:, lo:hi].astype(F32)
            uvf = uv_ref[:, lo:hi].astype(F32)
            act = (sl * uvf).astype(BF16)
            dwacc[lo:hi, :] += _dot_tn(act, dy)
            dact = _dot(dy, wdownt_ref[:, lo:hi])
            duv_ref[:, lo:hi] = (dact * sl).astype(BF16)
            dgate = dact * uvf * dsl_ref[:, lo:hi].astype(F32)
            dgbuf[0:ts, lo:hi] = dgate
            dcf_ref[3:4, lo:hi] += _rowsum(dgate)
            for k in range(FFN_CONV_K):
                dcf_ref[k:k + 1, lo:hi] += _rowsum(dgate * taps[k])
            dug = cf_ref[2:3, lo:hi] * dgate + cf_ref[1:2, lo:hi] * dgbuf[1:1 + ts, lo:hi]
            dug = dug + cf_ref[0:1, lo:hi] * dgbuf[2:2 + ts, lo:hi]
            dug_ref[:, lo:hi] = dug.astype(BF16)

        @pl.when(i == n - 1)
        def _():
            _store_blocks(dwacc, slabs_ref, sems, row_off, D_FF_PAD // N_DEV, 0, N_DEV)

    rev = lambda width: pl.BlockSpec((ts, width), lambda i: (n - 1 - i, 0))
    prev = pl.BlockSpec((16, D_FF_PAD), lambda i: (jnp.maximum((n - 1 - i) * halo_blocks - 1, 0), 0))
    hbm = pl.BlockSpec(memory_space=pl.ANY)
    return _grid_call(
        body, "ffn_bwd_last" if last else "ffn_bwd", n,
        in_specs=[hbm, rev(D_MODEL), rev(D_MODEL)] + [rev(D_FF_PAD)] * 4 + [prev, _const((8, D_FF_PAD)),
                                                                           _const((D_MODEL, D_FF_PAD)), _const((8, D_MODEL))],
        out_specs=[hbm, rev(D_FF_PAD), rev(D_FF_PAD), rev(D_MODEL),
                   _acc((8, D_FF_PAD)), _acc((8, D_MODEL)), _acc((8, 128))],
        out_shape=[jax.ShapeDtypeStruct(slabs.shape, F32),
                   jax.ShapeDtypeStruct((seq, D_FF_PAD), BF16), jax.ShapeDtypeStruct((seq, D_FF_PAD), BF16),
                   jax.ShapeDtypeStruct((seq, D_MODEL), F32),
                   jax.ShapeDtypeStruct((8, D_FF_PAD), F32), jax.ShapeDtypeStruct((8, D_MODEL), F32),
                   jax.ShapeDtypeStruct((8, 128), F32)],
        scratch_shapes=[pltpu.VMEM((ts + FFN_HALO, D_FF_PAD), F32), pltpu.VMEM((ts + FFN_HALO, D_FF_PAD), F32),
                        pltpu.VMEM((D_FF_PAD, D_MODEL), F32), pltpu.SemaphoreType.DMA((N_DEV,))],
        args=(slabs, dx2_or_target, z2, ug, uv, sl, dsl, ug, w["cf"], w["wdownt"], w["v2"]), aliases={0: 0}, exch=exch)


def _proj_bwd_call(d, wt, xin, addend, scale, ts, name, slabs, row_off, first_block, n_blocks, exch=None):
    seq, k = d.shape
    n = seq // ts

    def body(slabs_in, d_ref, wt_ref, xin_ref, add_ref, slabs_ref, dx_ref, acc, sems):
        i = pl.program_id(0)

        @pl.when(i == 0)
        def _():
            acc[...] = jnp.zeros(acc.shape, F32)

        dv = d_ref[...]
        dx_ref[...] = _dot(dv, wt_ref[...]) + scale * add_ref[...]
        acc[...] += _dot_tn(dv, xin_ref[...].astype(BF16))

        @pl.when(i == n - 1)
        def _():
            _store_blocks(acc, slabs_ref, sems, row_off, k // n_blocks, first_block, n_blocks)

    row = lambda width: pl.BlockSpec((ts, width), lambda i: (i, 0))
    hbm = pl.BlockSpec(memory_space=pl.ANY)
    return _grid_call(
        body, name, n,
        in_specs=[hbm, row(k), _const((k, D_MODEL)), row(D_MODEL), row(D_MODEL)],
        out_specs=[hbm, row(D_MODEL)],
        out_shape=[jax.ShapeDtypeStruct(slabs.shape, F32), jax.ShapeDtypeStruct((seq, D_MODEL), F32)],
        scratch_shapes=[pltpu.VMEM((k, D_MODEL), F32), pltpu.SemaphoreType.DMA((n_blocks,))],
        args=(slabs, d, wt, xin, addend), aliases={0: 0}, exch=exch)


MEM_FOLD = BLOB_LANES // XATTN_W


def _mem_proj_call(memq, wk_flat, wv_flat):
    def body(memq_ref, wk_ref, wv_ref, kh_ref, vh_ref):
        for w_ref, o_ref in ((wk_ref, kh_ref), (wv_ref, vh_ref)):
            acc = jnp.zeros((N_MEM, XATTN_W), F32)
            for q in range(MEM_FOLD):
                acc = acc + _dot(memq_ref[q], w_ref[:, q * XATTN_W:(q + 1) * XATTN_W])
            o_ref[...] = acc

    out = jax.ShapeDtypeStruct((N_MEM, XATTN_W), F32)
    return pl.pallas_call(body, name="mem_proj", out_shape=[out, out], compiler_params=_params())(memq, wk_flat, wv_flat)


def _mem_proj_bwd_call(memq, dkh, dvh, slabs, off_k, off_v):
    rows = D_MODEL // MEM_FOLD

    def body(slabs_in, memq_ref, dkh_ref, dvh_ref, slabs_ref, acc, sems):
        for d_ref, off in ((dkh_ref, off_k), (dvh_ref, off_v)):
            dv = d_ref[...].astype(BF16)
            for q in range(MEM_FOLD):
                acc[:, q * XATTN_W:(q + 1) * XATTN_W] = _dot_tn(memq_ref[q], dv)
            _store_blocks(acc, slabs_ref, sems, off, rows // N_DEV, 0, N_DEV)

    hbm = pl.BlockSpec(memory_space=pl.ANY)
    vmem = pl.BlockSpec(memory_space=pltpu.VMEM)
    return pl.pallas_call(
        body, name="mem_proj_bwd", in_specs=[hbm, vmem, vmem, vmem], out_specs=hbm,
        out_shape=jax.ShapeDtypeStruct(slabs.shape, F32),
        scratch_shapes=[pltpu.VMEM((rows, BLOB_LANES), F32), pltpu.SemaphoreType.DMA((N_DEV,))],
        input_output_aliases={0: 0}, compiler_params=_params(),
    )(slabs, memq, dkh, dvh)


def _place():
    return lax.axis_index("x"), lax.axis_index("y"), lax.axis_index("c")


def _all_gather_call(arrs, pieces, name, swap=None):
    n_in, n_p = len(arrs), len(pieces)
    n_sw = 0 if swap is None else 1

    def body(*refs):
        ins, outs = refs[:n_in], refs[n_in + n_sw:n_in + n_sw + n_p]
        send_sems, recv_sems, local_sems = refs[n_in + 2 * n_sw + n_p:n_in + 2 * n_sw + n_p + 3]
        swapped = []
        if swap is not None:
            swapped = swap[1](refs[n_in], refs[n_in + n_sw + n_p], *refs[-2:])
            for cp in swapped:
                cp.start()
        x, y, c = _place()
        me, sibling = (x, y, c), (x, y, 1 - c)
        chips = [(1 - x, y), (x, 1 - y), (1 - x, 1 - y)]

        def src(a):
            idx, r0, rows = pieces[a]
            return ins[idx] if r0 is None else ins[idx].at[pl.ds(r0, rows)]

        def slab(a, p):
            return outs[a].at[4 * p[0] + 2 * p[1] + p[2]]

        def copy(a, k, block, to, own=False):
            return pltpu.make_async_remote_copy(
                src_ref=src(a) if own else slab(a, block), dst_ref=slab(a, block),
                send_sem=send_sems.at[a, k], recv_sem=recv_sems.at[a, k], device_id=to, device_id_type=MESH)

        mine = [pltpu.make_async_copy(src(a), slab(a, me), local_sems.at[a]) for a in range(n_p)]
        for cp in mine:
            cp.start()
        first = []
        for a in range(n_p):
            first.append(copy(a, 0, me, sibling, own=True))
            first += [copy(a, 1 + j, me, (*chip, c), own=True) for j, chip in enumerate(chips)]
        for cp in first:
            cp.start()
        passed = []
        for a in range(n_p):
            for j, chip in enumerate(chips):
                copy(a, 1 + j, (*chip, c), me).wait_recv()
                cp = copy(a, 4 + j, (*chip, c), sibling)
                cp.start()
                passed.append(cp)
        for a in range(n_p):
            copy(a, 0, sibling, me).wait_recv()
            for j, chip in enumerate(chips):
                copy(a, 4 + j, (*chip, 1 - c), me).wait_recv()
        for cp in first + passed:
            cp.wait_send()
        for cp in mine:
            cp.wait()
        for cp in swapped:
            cp.wait()

    def out_shape(piece):
        idx, r0, rows = piece
        a = arrs[idx]
        return jax.ShapeDtypeStruct((N_DEV,) + (a.shape if r0 is None else (rows,) + a.shape[1:]), a.dtype)

    hbm = pl.BlockSpec(memory_space=pl.ANY)
    out_shapes = [out_shape(p) for p in pieces]
    scratch = [pltpu.SemaphoreType.DMA((n_p, 7)), pltpu.SemaphoreType.DMA((n_p, 7)), pltpu.SemaphoreType.DMA((n_p,))]
    args = list(arrs)
    if swap is not None:
        a, _, n = swap
        args.append(a)
        out_shapes.append(jax.ShapeDtypeStruct((n,) + a.shape[1:], a.dtype))
        scratch += [pltpu.SemaphoreType.DMA((n,)), pltpu.SemaphoreType.DMA((n,))]
    return pl.pallas_call(
        body, name=name, in_specs=[hbm] * len(args), out_specs=[hbm] * len(out_shapes), out_shape=out_shapes,
        scratch_shapes=scratch,
    )(*args)


def _flip(v, f):
    return 1 - v if f else v


def _gather_exchange(arrs, pieces):
    n_peers = N_DEV - 1

    def build(ins, outs, send_sems, recv_sems, local_sems):
        x, y, c = _place()
        flips = [(fx, fy, fc) for fx in (0, 1) for fy in (0, 1) for fc in (0, 1) if fx or fy or fc]
        remote, local = [], []
        for a, (idx, r0, rows) in enumerate(pieces):
            src = ins[idx] if r0 is None else ins[idx].at[pl.ds(r0, rows)]
            dst = outs[a].at[4 * x + 2 * y + c]
            remote += [pltpu.make_async_remote_copy(
                src_ref=src, dst_ref=dst, send_sem=send_sems.at[n_peers * a + k], recv_sem=recv_sems.at[n_peers * a + k],
                device_id=(_flip(x, fx), _flip(y, fy), _flip(c, fc)), device_id_type=MESH)
                for k, (fx, fy, fc) in enumerate(flips)]
            local.append(pltpu.make_async_copy(src, dst, local_sems.at[a]))
        return remote, local

    def out_shape(piece):
        idx, r0, rows = piece
        a = arrs[idx]
        return jax.ShapeDtypeStruct((N_DEV,) + (a.shape if r0 is None else (rows,) + a.shape[1:]), a.dtype)

    return _Exchange(arrs, [out_shape(p) for p in pieces], n_peers * len(pieces), build, n_local=len(pieces))


def _swap_core_copies(g_ref, r_ref, send_sems, recv_sems):
    x, y, c = _place()
    return [pltpu.make_async_remote_copy(
        src_ref=g_ref.at[2 * k + (1 - c)], dst_ref=r_ref.at[k], send_sem=send_sems.at[k], recv_sem=recv_sems.at[k],
        device_id=(x, y, 1 - c), device_id_type=MESH) for k in range(4)]


def _swap_chip_copies(p_ref, r_ref, send_sems, recv_sems):
    x, y, c = _place()
    chips = [(1 - x, y), (x, 1 - y), (1 - x, 1 - y)]
    return [pltpu.make_async_remote_copy(
        src_ref=p_ref.at[2 * px + py], dst_ref=r_ref.at[j], send_sem=send_sems.at[j], recv_sem=recv_sems.at[j],
        device_id=(px, py, c), device_id_type=MESH) for j, (px, py) in enumerate(chips)]


def _swap_exchange(a, copies, n):
    return _Exchange([a], [jax.ShapeDtypeStruct((n,) + a.shape[1:], a.dtype)], n,
                     lambda ins, outs, send_sems, recv_sems, local_sems: (copies(ins[0], outs[0], send_sems, recv_sems), []))


def _swap_call(a, copies, n, name):
    def body(a_ref, r_ref, send_sems, recv_sems):
        cps = copies(a_ref, r_ref, send_sems, recv_sems)
        for cp in cps:
            cp.start()
        for cp in cps:
            cp.wait()

    hbm = pl.BlockSpec(memory_space=pl.ANY)
    return pl.pallas_call(
        body, name=name, in_specs=[hbm], out_specs=hbm, out_shape=jax.ShapeDtypeStruct((n,) + a.shape[1:], a.dtype),
        scratch_shapes=[pltpu.SemaphoreType.DMA((n,)), pltpu.SemaphoreType.DMA((n,))],
    )(a)


def _swap_chip_start(p, rider):
    n = 3
    land = lax.empty((n,) + p.shape[1:], p.dtype)

    def body(p_ref, land_ref, rider_ref, send_sems, recv_sems, p_thru, land_thru, rider_thru):
        for cp in _swap_chip_copies(p_ref, land_ref, send_sems, recv_sems):
            cp.start()

    hbm, sem = pl.BlockSpec(memory_space=pltpu.HBM), pl.BlockSpec(memory_space=pltpu.SEMAPHORE)
    return pl.pallas_call(
        body, name="rs_swap_chip_start",
        out_shape=(pltpu.SemaphoreType.DMA((n,)), pltpu.SemaphoreType.DMA((n,)), pltpu.HBM(p.shape, p.dtype),
                   pltpu.HBM(land.shape, land.dtype), pltpu.HBM(rider.shape, rider.dtype)),
        in_specs=(hbm, hbm, hbm), out_specs=(sem, sem, hbm, hbm, hbm), input_output_aliases={0: 2, 1: 3, 2: 4},
        compiler_params=pltpu.CompilerParams(has_side_effects=pltpu.SideEffectType.DATAFLOW_SIDE_EFFECTING),
    )(pltpu.with_memory_space_constraint(p, pltpu.HBM), pltpu.with_memory_space_constraint(land, pltpu.HBM),
      pltpu.with_memory_space_constraint(rider, pltpu.HBM))


def _swap_chip_wait(send_sems, recv_sems, p_thru, land_thru, *after):
    def body(p_ref, land_ref, send_sems, recv_sems, *rest):
        for cp in _swap_chip_copies(p_ref, land_ref, send_sems, recv_sems):
            cp.wait_send()
            cp.wait_recv()

    hbm, sem = pl.BlockSpec(memory_space=pltpu.HBM), pl.BlockSpec(memory_space=pltpu.SEMAPHORE)
    return pl.pallas_call(
        body, name="rs_swap_chip_wait",
        out_shape=(pltpu.HBM(p_thru.shape, p_thru.dtype), pltpu.HBM(land_thru.shape, land_thru.dtype)),
        in_specs=(hbm, hbm, sem, sem) + (pl.BlockSpec(memory_space=pl.ANY),) * len(after), out_specs=(hbm, hbm),
        input_output_aliases={0: 0, 1: 1},
        compiler_params=pltpu.CompilerParams(has_side_effects=pltpu.SideEffectType.DATAFLOW_SIDE_EFFECTING),
    )(p_thru, land_thru, send_sems, recv_sems, *after)[1]


ADD_ROWS = 184


def _pair_add_call(g, r, c, chip):
    _, rows, width = g.shape

    def body(c_ref, chip_ref, g_ref, r_ref, o_ref):
        o_ref[...] = (g_ref[...] + r_ref[...]).astype(BF16)

    def other(j, chip_ref):
        return j + (j >= chip_ref[0]).astype(jnp.int32)

    return pl.pallas_call(
        body, name="rs_pair_add",
        grid_spec=pltpu.PrefetchScalarGridSpec(
            num_scalar_prefetch=2, grid=(3, rows // ADD_ROWS),
            in_specs=[pl.BlockSpec((None, ADD_ROWS, width),
                                   lambda j, i, c_ref, chip_ref: (2 * other(j, chip_ref) + c_ref[0], i, 0)),
                      pl.BlockSpec((None, ADD_ROWS, width), lambda j, i, c_ref, chip_ref: (other(j, chip_ref), i, 0))],
            out_specs=pl.BlockSpec((None, ADD_ROWS, width), lambda j, i, c_ref, chip_ref: (other(j, chip_ref), i, 0))),
        out_shape=jax.ShapeDtypeStruct((4, rows, width), BF16),
        compiler_params=_params(dimension_semantics=("arbitrary", "arbitrary")),
    )(c, chip, g, r)


def _adam(w, g, m, v):
    mn = ADAM_B1 * m + (1.0 - ADAM_B1) * g
    vn = ADAM_B2 * v + (1.0 - ADAM_B2) * (g * g)
    m_hat = mn / (1.0 - ADAM_B1 ** ADAM_STEP)
    v_hat = vn / (1.0 - ADAM_B2 ** ADAM_STEP)
    return -ADAM_LR * (m_hat / (jnp.sqrt(v_hat) + ADAM_EPS) + ADAM_WD * w), mn, vn


def _chip_add_adamw_call(slabs, from_sibling, from_chips, me, chip, w, m, v):
    _, rows, width = slabs.shape

    def body(me_ref, chip_ref, own_ref, sib_ref, r_ref, w_ref, m_ref, v_ref, g_ref, d_ref, mo_ref, vo_ref):
        g = own_ref[...] + sib_ref[...]
        for j in range(3):
            g = g + r_ref[j].astype(F32)
        g_ref[...] = g
        d_ref[...], mo_ref[...], vo_ref[...] = _adam(w_ref[...], g, m_ref[...], v_ref[...])

    spec = pl.BlockSpec((ADD_ROWS, width), lambda i, me_ref, chip_ref: (i, 0))
    return pl.pallas_call(
        body, name="rs_chip_add_adamw",
        grid_spec=pltpu.PrefetchScalarGridSpec(
            num_scalar_prefetch=2, grid=(rows // ADD_ROWS,),
            in_specs=[pl.BlockSpec((None, ADD_ROWS, width), lambda i, me_ref, chip_ref: (me_ref[0], i, 0)),
                      pl.BlockSpec((None, ADD_ROWS, width), lambda i, me_ref, chip_ref: (chip_ref[0], i, 0)),
                      pl.BlockSpec((3, ADD_ROWS, width), lambda i, me_ref, chip_ref: (0, i, 0)), spec, spec, spec],
            out_specs=[spec] * 4),
        out_shape=[jax.ShapeDtypeStruct((rows, width), F32)] * 4,
        compiler_params=_params(dimension_semantics=("arbitrary",)),
    )(me, chip, slabs, from_sibling, from_chips, w, m, v)


def _adamw_whole_call(params, name):
    n = len(params)

    def body(*refs):
        ins, outs = refs[:4 * n], refs[4 * n:]
        for a in range(n):
            w_ref, g_ref, m_ref, v_ref = ins[4 * a:4 * a + 4]
            outs[3 * a][...], outs[3 * a + 1][...], outs[3 * a + 2][...] = _adam(w_ref[...], g_ref[...], m_ref[...],
                                                                              v_ref[...])

    flat = [a for p in params for a in p]
    shapes = [jax.ShapeDtypeStruct(p[0].shape, F32) for p in params for _ in range(3)]
    out = pl.pallas_call(body, name=name, out_shape=shapes, compiler_params=_params())(*flat)
    return [tuple(out[3 * a:3 * a + 3]) for a in range(n)]


GATHERED_ACCS = (("dva", (8, CONV_W)), ("dv1", (8, D_MODEL)), ("dv2", (8, D_MODEL)), ("dcf", (8, D_FF_PAD)),
                 ("dcw", (HALO, CONV_W)), ("dwcat", (CHUNK, GMLP_HEADS * CHUNK)), ("dmsum", (CHUNK, GMLP_W)))
VEC_A = ("conv_a_b", "ln_a_g", "ln_a_b", "ln_v_g", "ln_v_b")
REP_IN_KERNEL = VEC_A + ("ln1_g", "ln1_b", "ln2_g", "ln2_b", "w_s", "b_s")


def _replicated_update_call(gathered, p, mom_m, mom_v):
    n_acc = len(GATHERED_ACCS)
    n_rep = len(REP_IN_KERNEL)

    def body(*refs):
        acc_refs = refs[:DEPTH * n_acc]
        wmv = refs[DEPTH * n_acc:DEPTH * n_acc + 3 * n_rep]
        outs = refs[DEPTH * n_acc + 3 * n_rep:]
        out_par = {nm: outs[4 * a:4 * a + 4] for a, nm in enumerate(REP_IN_KERNEL)}
        out_dcf = outs[4 * n_rep:4 * n_rep + DEPTH]
        out_dcw = outs[4 * n_rep + DEPTH:4 * n_rep + 2 * DEPTH]
        par = {nm: wmv[3 * a:3 * a + 3] for a, nm in enumerate(REP_IN_KERNEL)}
        tril = (lax.broadcasted_iota(jnp.int32, (CHUNK, CHUNK), 0) >= lax.broadcasted_iota(jnp.int32, (CHUNK, CHUNK), 1))
        head = lax.broadcasted_iota(jnp.int32, (8, GMLP_W), 0) * HEAD_DIM
        lane = lax.broadcasted_iota(jnp.int32, (8, GMLP_W), 1)
        sel = jnp.where((lane >= head) & (lane < head + HEAD_DIM), 1.0, 0.0)

        def update(nm, idx, g):
            w_ref, m_ref, v_ref = par[nm]
            d, mn, vn = _adam(w_ref[idx], g, m_ref[idx], v_ref[idx])
            g_ref, d_ref, mo_ref, vo_ref = out_par[nm]
            g_ref[idx] = g
            d_ref[idx] = d
            mo_ref[idx] = mn
            vo_ref[idx] = vn

        for l in range(DEPTH):
            tot = {}
            for a, (nm, _) in enumerate(GATHERED_ACCS):
                ref = acc_refs[l * n_acc + a]
                s = ref[0]
                for j in range(1, N_DEV):
                    s = s + ref[j]
                tot[nm] = s
            out_dcf[l][...] = tot["dcf"]
            out_dcw[l][...] = tot["dcw"]
            row = (slice(l, l + 1), slice(None))
            for k, nm in enumerate(VEC_A):
                update(nm, row, tot["dva"][k:k + 1, :])
            update("ln1_g", row, tot["dv1"][0:1, :])
            update("ln1_b", row, tot["dv1"][1:2, :])
            update("ln2_g", row, tot["dv2"][0:1, :])
            update("ln2_b", row, tot["dv2"][1:2, :])
            for h in range(GMLP_HEADS):
                gw = jnp.where(tril, tot["dwcat"][:, h * CHUNK:(h + 1) * CHUNK], 0.0)
                update("w_s", (l, h), gw)
            gb = lax.dot_general(sel, tot["dmsum"], (((1,), (1,)), ((), ())), precision=lax.Precision.HIGHEST,
                                 preferred_element_type=F32)
            for h in range(GMLP_HEADS):
                update("b_s", (l, slice(h, h + 1), slice(None)), gb[h:h + 1, :])

    ins = [gathered[l][nm] for l in range(DEPTH) for nm, _ in GATHERED_ACCS]
    ins += [t[nm] for nm in REP_IN_KERNEL for t in (p, mom_m, mom_v)]
    shapes = [jax.ShapeDtypeStruct(p[nm].shape, F32) for nm in REP_IN_KERNEL for _ in range(4)]
    shapes += [jax.ShapeDtypeStruct((8, D_FF_PAD), F32)] * DEPTH + [jax.ShapeDtypeStruct((HALO, CONV_W), F32)] * DEPTH
    out = pl.pallas_call(body, name="replicated_update", out_shape=shapes, compiler_params=_params())(*ins)
    res = [{nm: out[4 * a + k] for a, nm in enumerate(REP_IN_KERNEL)} for k in range(4)]
    return res, out[4 * n_rep:4 * n_rep + DEPTH], out[4 * n_rep + DEPTH:]


BLOCK_ROWS = (("w_in", IN_W // N_DEV), ("w_out", D_MODEL // N_DEV), ("w_up", 2 * FF_GROUP_PAD),
              ("w_down", FF_GROUP_PAD), ("w_mk", D_MODEL // N_DEV // MEM_FOLD), ("w_mv", D_MODEL // N_DEV // MEM_FOLD))
LAYER_ROWS = sum(r for _, r in BLOCK_ROWS)
assert LAYER_ROWS % ADD_ROWS == 0 and all(r % 16 == 0 for _, r in BLOCK_ROWS)


def _row_off(name):
    off = 0
    for nm, r in BLOCK_ROWS:
        if nm == name:
            return off
        off += r
    raise KeyError(name)


def _to_rows(name, a):
    if name == "w_in":
        return a.T
    if name == "w_up":
        t = a.T.reshape(2, FF_GROUP, D_MODEL)
        return jnp.pad(t, ((0, 0), (0, FF_GROUP_PAD - FF_GROUP), (0, 0))).reshape(2 * FF_GROUP_PAD, D_MODEL)
    if name == "w_down":
        return jnp.pad(a, ((0, FF_GROUP_PAD - FF_GROUP), (0, 0)))
    if name == "w_out":
        return a
    return a.reshape(-1, BLOB_LANES)


def _from_rows(name, r):
    if name == "w_in":
        return r.T
    if name == "w_up":
        return r.reshape(2, FF_GROUP_PAD, D_MODEL)[:, :FF_GROUP].reshape(2 * FF_GROUP, D_MODEL).T
    if name == "w_down":
        return r[:FF_GROUP]
    if name == "w_out":
        return r
    return r.reshape(D_MODEL // N_DEV, XATTN_W)


def _blob(tree, l):
    return jnp.concatenate([_to_rows(nm, tree[nm][l]) for nm, _ in BLOCK_ROWS], axis=0)


def _unblob(blobs):
    return {nm: jnp.stack([_from_rows(nm, b[_row_off(nm):_row_off(nm) + r]) for b in blobs]) for nm, r in BLOCK_ROWS}


def _ff_interleave(a):
    lead = a.shape[:-1]
    t = a.reshape(lead + (N_DEV, FF_GROUP))
    return jnp.pad(t, [(0, 0)] * len(lead) + [(0, 0), (0, FF_GROUP_PAD - FF_GROUP)]).reshape(lead + (D_FF_PAD,))


def _ff_deinterleave(a):
    lead = a.shape[:-1]
    return a.reshape(lead + (N_DEV, FF_GROUP_PAD))[..., :FF_GROUP].reshape(lead + (D_FF,))


def _head_table():
    hd = jnp.arange(XATTN_W) // HEAD_DIM
    return (hd[None, :] == jnp.arange(XATTN_HEADS)[:, None]).astype(F32)


def _mixer_operands(mat, conv_a_w, p, l, memq):
    w = {}
    w["wint"] = mat["w_in"]
    w["win"] = mat["w_in"].T
    w["wout"] = mat["w_out"]
    w["woutt"] = mat["w_out"].T
    w["cw"] = conv_a_w
    zeros = jnp.zeros((3, CONV_W), F32)
    w["va"] = jnp.concatenate([p[nm][l][None] for nm in VEC_A] + [zeros], axis=0)
    tril = jnp.tril(jnp.ones((CHUNK, CHUNK), F32))
    w["wcat"] = (p["w_s"][l] * tril[None]).transpose(1, 0, 2).reshape(CHUNK, GMLP_HEADS * CHUNK).astype(BF16)
    w["wcatt"] = w["wcat"].T
    w["bfull"] = jnp.repeat(p["b_s"][l].T, HEAD_DIM, axis=1)
    kh, vh = _mem_proj_call(memq, mat["w_mk"], mat["w_mv"])
    hm = _head_table()
    scale = 1.0 / math.sqrt(HEAD_DIM)
    w["kt"] = (kh.T[:, None, :] * hm.T[:, :, None] * scale).reshape(XATTN_W, XATTN_HEADS * N_MEM).astype(BF16)
    w["ktt"] = w["kt"].T
    w["vm"] = (hm[:, None, :] * vh[None]).reshape(XATTN_HEADS * N_MEM, XATTN_W).astype(BF16)
    w["vmt"] = w["vm"].T
    zeros = jnp.zeros((6, D_MODEL), F32)
    w["v1"] = jnp.concatenate([p["ln1_g"][l][None], p["ln1_b"][l][None], zeros], axis=0)
    return w


def _ffn_operands(w_up, w_down, conv_f_w, p, l):
    w = {}
    w["wgt"] = w_up[:D_FF_PAD]
    w["wvt"] = w_up[D_FF_PAD:]
    w["wg"] = w["wgt"].T
    w["wv"] = w["wvt"].T
    w["wdown"] = w_down
    w["wdownt"] = w_down.T
    zeros = jnp.zeros((6, D_MODEL), F32)
    w["v2"] = jnp.concatenate([p["ln2_g"][l][None], p["ln2_b"][l][None], zeros], axis=0)
    w["cf"] = jnp.concatenate([conv_f_w, _ff_interleave(p["conv_f_b"][l][None]), jnp.zeros((4, D_FF_PAD), F32)], axis=0)
    return w


TS_MIXER = 256
TS_FFN = 256
TS_PROJ = 512
CONV_A_SHARD = CONV_W // N_DEV


def kernel(x, mem, w_in, conv_a_w, conv_a_b, ln_a_g, ln_a_b, ln_v_g, ln_v_b, w_s, b_s, w_mk, w_mv, w_out, ln1_g, ln1_b, w_up, conv_f_w, conv_f_b, w_down, ln2_g, ln2_b, loss_target, m_w_in, m_conv_a_w, m_conv_a_b, m_ln_a_g, m_ln_a_b, m_ln_v_g, m_ln_v_b, m_w_s, m_b_s, m_w_mk, m_w_mv, m_w_out, m_ln1_g, m_ln1_b, m_w_up, m_conv_f_w, m_conv_f_b, m_w_down, m_ln2_g, m_ln2_b, v_w_in, v_conv_a_w, v_conv_a_b, v_ln_a_g, v_ln_a_b, v_ln_v_g, v_ln_v_b, v_w_s, v_b_s, v_w_mk, v_w_mv, v_w_out, v_ln1_g, v_ln1_b, v_w_up, v_conv_f_w, v_conv_f_b, v_w_down, v_ln2_g, v_ln2_b):
    given = dict(locals())
    p = {nm: given[nm] for nm in WEIGHTS}
    mom_m = {nm: given["m_" + nm] for nm in WEIGHTS}
    mom_v = {nm: given["v_" + nm] for nm in WEIGHTS}
    seq = x.shape[1]
    ts_m, ts_f, ts_p = min(TS_MIXER, seq), min(TS_FFN, seq), min(TS_PROJ, seq)
    cx, cy, cc = _place()
    me = 4 * cx + 2 * cy + cc

    blobs = [_blob(p, l) for l in range(DEPTH)]
    blobs_bf = [b.astype(BF16) for b in blobs]
    conv_a_tile = jnp.pad(conv_a_w, ((0, 0), (0, HALO - CONV_K), (0, 128 - CONV_A_SHARD)))
    conv_f_tile = jnp.pad(conv_f_w, ((0, 0), (0, 8 - FFN_CONV_K), (0, 384 - FF_GROUP)))
    rows = dict(BLOCK_ROWS)
    mixer_names = ("w_in", "w_out", "w_mk", "w_mv")
    pieces = [(0, _row_off(nm), rows[nm]) for nm in mixer_names] + [(1, None, 0), (2, None, 0)]
    first = _all_gather_call([blobs_bf[0], conv_a_tile, conv_f_tile], pieces, "gather_weights")
    conv_a_all, conv_f_all = first[len(mixer_names)], first[len(mixer_names) + 1]
    conv_a = [conv_a_all[:, l, :, :CONV_A_SHARD].transpose(1, 0, 2).reshape(HALO, CONV_W) for l in range(DEPTH)]
    conv_f = [conv_f_all[:, l, :FFN_CONV_K, :FF_GROUP_PAD].transpose(1, 0, 2).reshape(FFN_CONV_K, D_FF_PAD)
              for l in range(DEPTH)]
    memq = mem[0].reshape(N_MEM, D_MODEL // MEM_FOLD, MEM_FOLD).transpose(2, 0, 1).astype(BF16)

    def gather_of(l, names):
        return _gather_exchange([blobs_bf[l]], [(0, _row_off(nm), rows[nm]) for nm in names])

    def full(pieces):
        return [g.reshape(-1, BLOB_LANES) for g in pieces]

    ffn_names = ("w_up", "w_down")
    ops0 = _mixer_operands(dict(zip(mixer_names, full(first[:len(mixer_names)]))), conv_a[0], p, 0, memq)
    mixer_saved = ("hb", "z1", "x1", "a1", "cat", "p", "gs", "rv", "x1b")
    mixed0, ffn0 = _mixer_fwd_call(x[0], ops0, ts_m, gather_of(0, ffn_names))
    ops0.update(_ffn_operands(*full(ffn0), conv_f[0], p, 0))
    saved = [dict(zip(mixer_saved, mixed0), x=x[0])]
    (ug, uv, sl, dsl, z2, x2), all1 = _ffn_fwd_call(saved[0]["x1"], ops0, ts_f, gather_of(1, mixer_names + ffn_names))
    saved[0].update(ug=ug, uv=uv, sl=sl, dsl=dsl, z2=z2)
    all1 = dict(zip(mixer_names + ffn_names, full(all1)))
    ops1 = _mixer_operands(all1, conv_a[1], p, 1, memq)
    ops1.update(_ffn_operands(all1["w_up"], all1["w_down"], conv_f[1], p, 1))
    mixed1, _ = _mixer_fwd_call(x2, ops1, ts_m)
    saved.append(dict(zip(mixer_saved, mixed1), x=x2))
    (ug, uv, sl, dsl, z2), _ = _ffn_fwd_call(saved[1]["x1"], ops1, ts_f, want_x2=False)
    saved[1].update(ug=ug, uv=uv, sl=sl, dsl=dsl, z2=z2)
    ops = [ops0, ops1]

    hm = _head_table()
    core_id = cc.reshape(1).astype(jnp.int32)
    me_id, chip_id = me.reshape(1).astype(jnp.int32), (2 * cx + cy).reshape(1).astype(jnp.int32)
    slabs = [lax.empty((N_DEV, LAYER_ROWS, BLOB_LANES), F32) for _ in range(DEPTH)]
    accs, gathered_accs = [None] * DEPTH, [None] * DEPTH
    acc_names = [nm for nm, _ in GATHERED_ACCS]
    whole = [(a, None, 0) for a in range(len(acc_names))]

    def acc_list(l):
        return [accs[l][nm] for nm in acc_names]

    from_sibling, from_chips = [None] * DEPTH, [None] * DEPTH
    dx = loss_target[0]
    loss = None
    for l in reversed(range(DEPTH)):
        s, w = saved[l], ops[l]
        last = l == DEPTH - 1
        ride = None if last else _swap_exchange(slabs[l + 1], _swap_core_copies, 4)
        (sl, dug, duv, dz2, dcf, dv2, loss_acc), got = _ffn_bwd_call(
            dx, s["z2"], s["ug"], s["uv"], s["sl"], s["dsl"], w, ts_f, last, slabs[l], _row_off("w_down"), ride)
        if last:
            loss = loss_acc[0, 0]
        else:
            from_sibling[l + 1] = got[0]
            chip_sum = _pair_add_call(slabs[l + 1], from_sibling[l + 1], core_id, chip_id)
        off_up = _row_off("w_up")
        ride = None if last else _gather_exchange(acc_list(l + 1), whole)
        (sl, dxa), got = _proj_bwd_call(dug, w["wgt"], s["x1b"], dz2, ALPHA, ts_p, "up_gate_bwd", sl, off_up, 0, 4, ride)
        if not last:
            gathered_accs[l + 1] = dict(zip(acc_names, got))
        ride = None if last else _swap_exchange(chip_sum, _swap_chip_copies, 3)
        (sl, dx1), got = _proj_bwd_call(duv, w["wvt"], s["x1b"], dxa, 1.0, ts_p, "up_val_bwd", sl, off_up, 4, 4, ride)
        if not last:
            from_chips[l + 1] = got[0]
        (sl, dx, dkt, dvm, dwcat, dmsum, dva, dcw, dv1) = _mixer_bwd_call(
            dx1, s["z1"], s["hb"], s["a1"], s["cat"], s["p"], s["gs"], s["rv"], s["x"], w, ts_m, sl, _row_off("w_out"),
            _row_off("w_in"))
        dkh = jnp.einsum("hd,dhm->md", hm, dkt.reshape(XATTN_W, XATTN_HEADS, N_MEM)) * (1.0 / math.sqrt(HEAD_DIM))
        dvh = jnp.einsum("hd,hmd->md", hm, dvm.reshape(XATTN_HEADS, N_MEM, XATTN_W))
        slabs[l] = _mem_proj_bwd_call(memq, dkh, dvh, sl, _row_off("w_mk"), _row_off("w_mv"))
        accs[l] = dict(dva=dva, dv1=dv1, dv2=dv2, dcf=dcf, dcw=dcw, dwcat=dwcat, dmsum=dmsum)
    grad_x = dx[None]
    *got, from_sibling[0] = _all_gather_call(acc_list(0), whole, "gather_small_grads",
                                            swap=(slabs[0], _swap_core_copies, 4))
    gathered_accs[0] = dict(zip(acc_names, got))
    chip_sum = _pair_add_call(slabs[0], from_sibling[0], core_id, chip_id)
    in_flight = _swap_chip_start(chip_sum, from_chips[1])
    from_chips[1] = in_flight[4]

    def final(l):
        return _chip_add_adamw_call(slabs[l], from_sibling[l], from_chips[l], me_id, chip_id, blobs[l], _blob(mom_m, l),
                                    _blob(mom_v, l))

    per_layer = [None, final(1)]
    rep, dcf_sum, dcw_sum = _replicated_update_call(gathered_accs, p, mom_m, mom_v)
    from_chips[0] = _swap_chip_wait(*in_flight[:4], per_layer[1][0], dcf_sum[0])
    per_layer[0] = final(0)
    outs = [_unblob([per_layer[l][k] for l in range(DEPTH)]) for k in range(4)]
    for k in range(4):
        outs[k].update(rep[k])
    dcf_sum, dcw_sum = jnp.stack(dcf_sum), jnp.stack(dcw_sum)
    zero = jnp.zeros((), jnp.int32)
    g_conv_a_w = lax.dynamic_slice(dcw_sum, (zero, zero, CONV_A_SHARD * me), (DEPTH, CONV_K, CONV_A_SHARD))
    g_conv_f_w = lax.dynamic_slice(dcf_sum, (zero, zero, FF_GROUP_PAD * me), (DEPTH, FFN_CONV_K, FF_GROUP))
    g_conv_f_b = _ff_deinterleave(dcf_sum[:, FFN_CONV_K])
    conv_grads = dict(conv_a_w=g_conv_a_w, conv_f_w=g_conv_f_w, conv_f_b=g_conv_f_b)
    conv_names = tuple(conv_grads)
    upd = _adamw_whole_call([(p[nm], conv_grads[nm], mom_m[nm], mom_v[nm]) for nm in conv_names], "adamw_conv")
    for nm, (d, mn, vn) in zip(conv_names, upd):
        outs[0][nm], outs[1][nm], outs[2][nm], outs[3][nm] = conv_grads[nm], d, mn, vn

    loss = lax.psum(loss, ("x", "y", "c"))
    return (loss, grad_x, *[outs[0][nm] for nm in WEIGHTS], *[outs[1][nm] for nm in WEIGHTS],
            *[outs[2][nm] for nm in WEIGHTS], *[outs[3][nm] for nm in WEIGHTS])
```

```python
import math

import jax
import jax.numpy as jnp
from jax import lax
from jax.experimental import pallas as pl
from jax.experimental.pallas import tpu as pltpu

F32 = jnp.float32
BF16 = jnp.bfloat16

DEPTH = 2
D_MODEL = 1024
CONV_W = 384
GMLP_W = 384
XATTN_W = 256
HEAD_DIM = 64
GMLP_HEADS = 6
XATTN_HEADS = 4
IN_W = 1792
CONV_K = 31
CHUNK = 128
N_MEM = 256
D_FF = 2752
D_FF_PAD = 2816
FFN_CONV_K = 3
ALPHA = (2.0 * DEPTH) ** 0.25
LN_EPS = 1e-5
N_DEV = 8

ADAM_LR = 0.001
ADAM_B1 = 0.9
ADAM_B2 = 0.999
ADAM_EPS = 1e-08
ADAM_WD = 0.01
ADAM_STEP = 10

HALO = 32
CONV_ROWS = 32
V7X_VMEM_BYTES = 64 * 1024 * 1024
VMEM_LIMIT = V7X_VMEM_BYTES - 8 * 1024 * 1024
BLOB_LANES = 1024

MESH = pl.DeviceIdType.MESH

WEIGHTS = ("w_in", "conv_a_w", "conv_a_b", "ln_a_g", "ln_a_b", "ln_v_g", "ln_v_b", "w_s", "b_s", "w_mk", "w_mv",
           "w_out", "ln1_g", "ln1_b", "w_up", "conv_f_w", "conv_f_b", "w_down", "ln2_g", "ln2_b")


def _params(**kw):
    return pltpu.CompilerParams(vmem_limit_bytes=VMEM_LIMIT, **kw)


def _const(shape):
    nd = len(shape)
    return pl.BlockSpec(shape, lambda i: (0,) * nd, pipeline_mode=pl.Buffered(1))


def _acc(shape):
    nd = len(shape)
    return pl.BlockSpec(shape, lambda i: (0,) * nd)


class _Exchange:
    def __init__(self, arrays, out_shapes, n_copies, build, n_local=1):
        self.arrays, self.out_shapes, self.n_copies, self.build = list(arrays), list(out_shapes), n_copies, build
        self.n_local = n_local


def _carry(core, n_in, n_out, exch, n_steps):
    if exch is None:
        return core
    nx_in, nx_out = len(exch.arrays), len(exch.out_shapes)

    def body(*refs):
        o0 = n_in + nx_in
        s0 = o0 + n_out + nx_out
        x_in, x_out, sems = refs[n_in:o0], refs[o0 + n_out:s0], refs[-3:]
        i = pl.program_id(0)

        @pl.when(i == 0)
        def _():
            remote, local = exch.build(x_in, x_out, *sems)
            for cp in remote + local:
                cp.start()

        core(*refs[:n_in], *refs[o0:o0 + n_out], *refs[s0:-3])

        @pl.when(i == n_steps - 1)
        def _():
            remote, local = exch.build(x_in, x_out, *sems)
            for cp in remote + local:
                cp.wait()

    return body


def _grid_call(core, name, n_steps, in_specs, out_specs, out_shape, scratch_shapes, args, aliases=None, exch=None):
    hbm = pl.BlockSpec(memory_space=pl.ANY)
    n_in, n_out = len(in_specs), len(out_specs)
    in_specs, out_specs, out_shape, scratch_shapes, args = (list(in_specs), list(out_specs), list(out_shape),
                                                            list(scratch_shapes), list(args))
    if exch is not None:
        in_specs += [hbm] * len(exch.arrays)
        out_specs += [hbm] * len(exch.out_shapes)
        out_shape += exch.out_shapes
        scratch_shapes += [pltpu.SemaphoreType.DMA((exch.n_copies,)), pltpu.SemaphoreType.DMA((exch.n_copies,)),
                           pltpu.SemaphoreType.DMA((exch.n_local,))]
        args += exch.arrays
    out = pl.pallas_call(
        _carry(core, n_in, n_out, exch, n_steps), name=name, grid=(n_steps,), in_specs=in_specs, out_specs=out_specs,
        out_shape=out_shape, scratch_shapes=scratch_shapes, input_output_aliases=aliases or {},
        compiler_params=_params(dimension_semantics=("arbitrary",)))(*args)
    return list(out[:n_out]), list(out[n_out:])


def _sigmoid(x):
    return 1.0 / (1.0 + jnp.exp(-x))


_GELU_C = math.sqrt(2.0 / math.pi)


def _gelu(x):
    x2 = x * x
    t = jnp.tanh(_GELU_C * (x + 0.044715 * x * x2))
    g = 0.5 * x * (1.0 + t)
    dg = 0.5 * (1.0 + t) + 0.5 * x * (1.0 - t * t) * (_GELU_C * (1.0 + 3.0 * 0.044715 * x2))
    return g, dg


def _ln_stats(z):
    mu = jnp.mean(z, axis=-1, keepdims=True)
    zc = z - mu
    var = jnp.mean(zc * zc, axis=-1, keepdims=True)
    r = lax.rsqrt(var + LN_EPS)
    return zc * r, r


def _ln_bwd(dy, xh, r, g):
    dxh = dy * g
    m1 = jnp.mean(dxh, axis=-1, keepdims=True)
    m2 = jnp.mean(dxh * xh, axis=-1, keepdims=True)
    return r * (dxh - m1 - xh * m2)


def _rowsum(x):
    return jnp.sum(x, axis=0, keepdims=True)


def _dot(a, b):
    return jnp.dot(a, b, preferred_element_type=F32)


def _dot_tn(a, b):
    return lax.dot_general(a, b, (((0,), (0,)), ((), ())), preferred_element_type=F32)


def _dot_nt(a, b):
    return lax.dot_general(a, b, (((1,), (1,)), ((), ())), preferred_element_type=F32)


def _shift_copies(buf, sh, rows):
    for b in range(1, 8):
        sh[b - 1, 0:rows, :] = buf[b:b + rows, :]


def _window(buf, sh, start):
    b = start % 8
    a = start - b
    return buf[a:a + CONV_ROWS, :] if b == 0 else sh[b - 1, a:a + CONV_ROWS, :]


def _conv31_fwd(buf, sh, w_ref, bias, out, ts):
    for r0 in range(0, ts, CONV_ROWS):
        acc = jnp.broadcast_to(bias, (CONV_ROWS, CONV_W))
        for k in range(CONV_K):
            acc = acc + w_ref[k:k + 1, :] * _window(buf, sh, r0 + HALO - (CONV_K - 1) + k)
        out[r0:r0 + CONV_ROWS, :] = acc


def _conv31_dx(dbuf, dsh, w_ref, out, ts):
    for r0 in range(0, ts, CONV_ROWS):
        acc = jnp.zeros((CONV_ROWS, CONV_W), F32)
        for k in range(CONV_K):
            acc = acc + w_ref[k:k + 1, :] * _window(dbuf, dsh, r0 + (CONV_K - 1) - k)
        out[r0:r0 + CONV_ROWS, :] = acc


def _conv31_dw(buf, sh, dbuf, acc, ts):
    for r0 in range(0, ts, CONV_ROWS):
        d = dbuf[r0:r0 + CONV_ROWS, :]
        for k in range(CONV_K):
            m = d * _window(buf, sh, r0 + HALO - (CONV_K - 1) + k)
            part = m[0:8, :]
            for q in range(8, CONV_ROWS, 8):
                part = part + m[q:q + 8, :]
            acc[k] += part


def _head_mask(width, h):
    lane = lax.broadcasted_iota(jnp.int32, (CHUNK, width), 1)
    return (lane >= h * HEAD_DIM) & (lane < (h + 1) * HEAD_DIM)


def _stack_heads(vn_c):
    return jnp.concatenate([jnp.where(_head_mask(GMLP_W, h), vn_c, 0.0) for h in range(GMLP_HEADS)], axis=0)


def _group_a_fwd(hf, buf, sh, a1_ref, cw_ref, va_ref, ts, conv=True):
    ha = hf[:, 0:CONV_W]
    sg = _sigmoid(hf[:, CONV_W:2 * CONV_W])
    buf[HALO:HALO + ts, :] = ha * sg
    _shift_copies(buf, sh, ts + HALO - 8)
    if conv:
        _conv31_fwd(buf, sh, cw_ref, va_ref[0:1, :], a1_ref, ts)
    a2h, ra = _ln_stats(a1_ref[...])
    a2 = a2h * va_ref[1:2, :] + va_ref[2:3, :]
    sa = _sigmoid(a2)
    return dict(ha=ha, sg=sg, a2h=a2h, ra=ra, a2=a2, sa=sa, a=a2 * sa)


GROUP_LANES = ((0, 2 * CONV_W), (2 * CONV_W, 2 * CONV_W + 2 * GMLP_W), (2 * CONV_W + 2 * GMLP_W, IN_W))


def _group_b_fwd(hf, va_ref, wcat_ref, bfull_ref, ts):
    hu = hf[:, 0:GMLP_W]
    hv = hf[:, GMLP_W:2 * GMLP_W]
    u, du = _gelu(hu)
    v, dv = _gelu(hv)
    vhat, rv = _ln_stats(v)
    vn = vhat * va_ref[3:4, :] + va_ref[4:5, :]
    stacks, mixed = [], []
    for c0 in range(0, ts, CHUNK):
        st = _stack_heads(vn[c0:c0 + CHUNK, :]).astype(BF16)
        stacks.append(st)
        mixed.append(_dot(wcat_ref[...], st) + bfull_ref[...])
    mixed = jnp.concatenate(mixed, axis=0) if len(mixed) > 1 else mixed[0]
    return dict(u=u, du=du, dv=dv, vhat=vhat, rv=rv, stacks=stacks, mixed=mixed, g=u * mixed)


GROUP_B_SAVED = ("u", "du", "dv", "vhat", "mixed")


def _group_c_fwd(qb, kt_ref, vm_ref):
    s_all = _dot(qb, kt_ref[...])
    ps = []
    for g in range(XATTN_HEADS):
        s = s_all[:, g * N_MEM:(g + 1) * N_MEM]
        e = jnp.exp(s - jnp.max(s, axis=-1, keepdims=True))
        ps.append(e / jnp.sum(e, axis=-1, keepdims=True))
    p_all = jnp.concatenate(ps, axis=1)
    pb = p_all.astype(BF16)
    return dict(qb=qb, p=p_all, pb=pb, c=_dot(pb, vm_ref[...]))


def _mixer_fwd_call(x, w, ts, exch=None):
    seq = x.shape[0]
    n = seq // ts

    def body(x_ref, win_ref, cw_ref, va_ref, wcat_ref, bfull_ref, kt_ref, vm_ref, wout_ref, v1_ref,
             hb_ref, z1_ref, x1_ref, a1buf, cat_ref, p_ref, gs_ref, rv_ref, buf, sh):
        i = pl.program_id(0)

        @pl.when(i == 0)
        def _():
            buf[0:HALO, :] = jnp.zeros((HALO, CONV_W), F32)

        @pl.when(i > 0)
        def _():
            buf[0:HALO, :] = buf[ts:ts + HALO, :]

        xv = x_ref[...]
        hb = _dot(xv.astype(BF16), win_ref[...]).astype(BF16)
        hb_ref[...] = hb
        hf = hb.astype(F32)
        (a_lo, a_hi), (b_lo, b_hi), (c_lo, c_hi) = GROUP_LANES
        ga = _group_a_fwd(hf[:, a_lo:a_hi], buf, sh, a1buf, cw_ref, va_ref, ts)
        gb = _group_b_fwd(hf[:, b_lo:b_hi], va_ref, wcat_ref, bfull_ref, ts)
        gc = _group_c_fwd(hb[:, c_lo:c_hi], kt_ref, vm_ref)
        cat = jnp.concatenate([ga["a"], gb["g"], gc["c"]], axis=1).astype(BF16)
        cat_ref[...] = cat
        p_ref[...] = gc["pb"]
        gs_ref[...] = jnp.concatenate([gb[k] for k in GROUP_B_SAVED], axis=1).astype(BF16)
        rv_ref[...] = jnp.broadcast_to(gb["rv"], (ts, 128))
        z1 = ALPHA * xv + _dot(cat, wout_ref[...])
        z1_ref[...] = z1
        xh, _ = _ln_stats(z1)
        x1_ref[...] = xh * v1_ref[0:1, :] + v1_ref[1:2, :]

    row = lambda width: pl.BlockSpec((ts, width), lambda i: (i, 0))
    return _grid_call(
        body, "mixer_fwd", n,
        in_specs=[row(D_MODEL), _const((D_MODEL, IN_W)), _const((HALO, CONV_W)), _const((8, CONV_W)),
                  _const((CHUNK, GMLP_HEADS * CHUNK)), _const((CHUNK, GMLP_W)), _const((XATTN_W, XATTN_HEADS * N_MEM)),
                  _const((XATTN_HEADS * N_MEM, XATTN_W)), _const((D_MODEL, D_MODEL)), _const((8, D_MODEL))],
        out_specs=[row(IN_W), row(D_MODEL), row(D_MODEL), row(CONV_W), row(D_MODEL), row(XATTN_HEADS * N_MEM),
                   row(len(GROUP_B_SAVED) * GMLP_W), row(128)],
        out_shape=[jax.ShapeDtypeStruct((seq, IN_W), BF16), jax.ShapeDtypeStruct((seq, D_MODEL), F32),
                   jax.ShapeDtypeStruct((seq, D_MODEL), F32), jax.ShapeDtypeStruct((seq, CONV_W), F32),
                   jax.ShapeDtypeStruct((seq, D_MODEL), BF16), jax.ShapeDtypeStruct((seq, XATTN_HEADS * N_MEM), BF16),
                   jax.ShapeDtypeStruct((seq, len(GROUP_B_SAVED) * GMLP_W), BF16),
                   jax.ShapeDtypeStruct((seq, 128), F32)],
        scratch_shapes=[pltpu.VMEM((ts + HALO, CONV_W), F32), pltpu.VMEM((7, ts + HALO, CONV_W), F32)],
        args=(x, w["win"], w["cw"], w["va"], w["wcat"], w["bfull"], w["kt"], w["vm"], w["wout"], w["v1"]), exch=exch)


def _store_blocks(acc, slabs_ref, sems, row_off, rows, first_block, n_blocks):
    copies = [pltpu.make_async_copy(acc.at[pl.ds(q * rows, rows)], slabs_ref.at[first_block + q, pl.ds(row_off, rows)],
                                    sems.at[q]) for q in range(n_blocks)]
    for cp in copies:
        cp.start()
    for cp in copies:
        cp.wait()


def _mixer_bwd_call(dx1, z1, hb, a1, cat, p, gs, rv, x, w, ts, slabs, off_out, off_in):
    seq = dx1.shape[0]
    n = seq // ts
    halo_blocks = ts // HALO

    def body(slabs_in, dx1_ref, z1_ref, hb_ref, hprev_ref, a1_ref, cat_ref, p_ref, gs_ref, rv_ref, x_ref, cw_ref, va_ref,
             wcatt_ref, ktt_ref, vmt_ref, woutt_ref, wint_ref, v1_ref,
             slabs_ref, dx_ref, dkt_ref, dvm_ref, dwcat_ref, dmsum_ref, dva_ref, dcw_ref, dv1_ref,
             buf, dbuf, da0buf, dwout_ref, dwin_ref, sems, sh, dsh, dcw_acc):
        i = pl.program_id(0)

        @pl.when(i == 0)
        def _():
            for ref in (dwout_ref, dwin_ref, dkt_ref, dvm_ref, dwcat_ref, dmsum_ref, dva_ref, dcw_ref, dv1_ref, dcw_acc):
                ref[...] = jnp.zeros(ref.shape, F32)
            dbuf[ts:ts + HALO, :] = jnp.zeros((HALO, CONV_W), F32)

        @pl.when(i > 0)
        def _():
            dbuf[ts:ts + HALO, :] = dbuf[0:HALO, :]

        dx1v = dx1_ref[...]
        xh1, r1 = _ln_stats(z1_ref[...])
        dv1_ref[0:1, :] += _rowsum(dx1v * xh1)
        dv1_ref[1:2, :] += _rowsum(dx1v)
        dz1 = _ln_bwd(dx1v, xh1, r1, v1_ref[0:1, :])
        dmix = dz1.astype(BF16)

        hf = hb_ref[:, 0:2 * CONV_W].astype(F32)
        hp = hprev_ref[...].astype(F32)
        a0p = hp[:, 0:CONV_W] * _sigmoid(hp[:, CONV_W:2 * CONV_W])
        buf[0:HALO, :] = jnp.where(i == n - 1, 0.0, a0p)
        ga = _group_a_fwd(hf, buf, sh, a1_ref, cw_ref, va_ref, ts, conv=False)
        gb = {k: gs_ref[:, j * GMLP_W:(j + 1) * GMLP_W].astype(F32) for j, k in enumerate(GROUP_B_SAVED)}
        vn = gb["vhat"] * va_ref[3:4, :] + va_ref[4:5, :]
        pb = p_ref[...]
        gc = dict(qb=hb_ref[:, IN_W - XATTN_W:IN_W], pb=pb, p=pb.astype(F32))

        dwout_ref[...] += _dot_tn(cat_ref[...], dmix)
        dcat = _dot(dmix, woutt_ref[...])
        da = dcat[:, 0:CONV_W]
        dg = dcat[:, CONV_W:CONV_W + GMLP_W]
        dc = dcat[:, CONV_W + GMLP_W:D_MODEL].astype(BF16)

        dp = _dot(dc, vmt_ref[...])
        dvm_ref[...] += _dot_tn(gc["pb"], dc)
        dss = []
        for g in range(XATTN_HEADS):
            sl = slice(g * N_MEM, (g + 1) * N_MEM)
            pg = gc["p"][:, sl]
            dpg = dp[:, sl]
            dss.append(pg * (dpg - jnp.sum(dpg * pg, axis=-1, keepdims=True)))
        ds = jnp.concatenate(dss, axis=1).astype(BF16)
        dq = _dot(ds, ktt_ref[...])
        dkt_ref[...] += _dot_tn(gc["qb"], ds)

        dmixed = dg * gb["u"]
        dhu = dg * gb["mixed"] * gb["du"]
        dvns = []
        for j, c0 in enumerate(range(0, ts, CHUNK)):
            dm = dmixed[c0:c0 + CHUNK, :]
            dmb = dm.astype(BF16)
            dmsum_ref[...] += dm
            dwcat_ref[...] += _dot_nt(dmb, _stack_heads(vn[c0:c0 + CHUNK, :]).astype(BF16))
            dst = _dot(wcatt_ref[...], dmb)
            dvn_c = jnp.zeros((CHUNK, GMLP_W), F32)
            for h in range(GMLP_HEADS):
                dvn_c = dvn_c + jnp.where(_head_mask(GMLP_W, h), dst[h * CHUNK:(h + 1) * CHUNK, :], 0.0)
            dvns.append(dvn_c)
        dvn = jnp.concatenate(dvns, axis=0) if len(dvns) > 1 else dvns[0]
        dva_ref[3:4, :] += _rowsum(dvn * gb["vhat"])
        dva_ref[4:5, :] += _rowsum(dvn)
        dhv = _ln_bwd(dvn, gb["vhat"], rv_ref[:, 0:1], va_ref[3:4, :]) * gb["dv"]

        a2, sa = ga["a2"], ga["sa"]
        da2 = da * (sa * (1.0 + a2 * (1.0 - sa)))
        dva_ref[1:2, :] += _rowsum(da2 * ga["a2h"])
        dva_ref[2:3, :] += _rowsum(da2)
        da1 = _ln_bwd(da2, ga["a2h"], ga["ra"], va_ref[1:2, :])
        dva_ref[0:1, :] += _rowsum(da1)
        dbuf[0:ts, :] = da1
        _shift_copies(dbuf, dsh, ts + HALO - 8)
        _conv31_dw(buf, sh, dbuf, dcw_acc, ts)
        _conv31_dx(dbuf, dsh, cw_ref, da0buf, ts)
        da0 = da0buf[...]
        sg = ga["sg"]
        dha = da0 * sg
        dhg = da0 * ga["ha"] * sg * (1.0 - sg)

        dh = jnp.concatenate([dha, dhg, dhu, dhv, dq], axis=1).astype(BF16)
        dx_ref[...] = _dot(dh, wint_ref[...]) + ALPHA * dz1
        dwin_ref[...] += _dot_tn(dh, x_ref[...].astype(BF16))

        @pl.when(i == n - 1)
        def _():
            for k in range(CONV_K):
                dcw_ref[k:k + 1, :] = _rowsum(dcw_acc[k])
            _store_blocks(dwout_ref, slabs_ref, sems, off_out, D_MODEL // N_DEV, 0, N_DEV)
            _store_blocks(dwin_ref, slabs_ref, sems, off_in, IN_W // N_DEV, 0, N_DEV)

    rev = lambda width: pl.BlockSpec((ts, width), lambda i: (n - 1 - i, 0))
    prev = pl.BlockSpec((HALO, 2 * CONV_W), lambda i: (jnp.maximum((n - 1 - i) * halo_blocks - 1, 0), 0))
    hbm = pl.BlockSpec(memory_space=pl.ANY)
    hc = GMLP_HEADS * CHUNK
    am = XATTN_HEADS * N_MEM
    return pl.pallas_call(
        body, name="mixer_bwd", grid=(n,),
        in_specs=[hbm, rev(D_MODEL), rev(D_MODEL), rev(IN_W), prev, rev(CONV_W), rev(D_MODEL), rev(am),
                  rev(len(GROUP_B_SAVED) * GMLP_W), rev(128), rev(D_MODEL), _const((HALO, CONV_W)),
                  _const((8, CONV_W)), _const((hc, CHUNK)), _const((am, XATTN_W)), _const((XATTN_W, am)),
                  _const((D_MODEL, D_MODEL)), _const((IN_W, D_MODEL)), _const((8, D_MODEL))],
        out_specs=[hbm, rev(D_MODEL), _acc((XATTN_W, am)), _acc((am, XATTN_W)),
                   _acc((CHUNK, hc)), _acc((CHUNK, GMLP_W)), _acc((8, CONV_W)), _acc((HALO, CONV_W)),
                   _acc((8, D_MODEL))],
        out_shape=[jax.ShapeDtypeStruct(slabs.shape, F32), jax.ShapeDtypeStruct((seq, D_MODEL), F32),
                   jax.ShapeDtypeStruct((XATTN_W, am), F32),
                   jax.ShapeDtypeStruct((am, XATTN_W), F32), jax.ShapeDtypeStruct((CHUNK, hc), F32),
                   jax.ShapeDtypeStruct((CHUNK, GMLP_W), F32), jax.ShapeDtypeStruct((8, CONV_W), F32),
                   jax.ShapeDtypeStruct((HALO, CONV_W), F32), jax.ShapeDtypeStruct((8, D_MODEL), F32)],
        scratch_shapes=[pltpu.VMEM((ts + HALO, CONV_W), F32),
                        pltpu.VMEM((ts + HALO, CONV_W), F32), pltpu.VMEM((ts, CONV_W), F32),
                        pltpu.VMEM((D_MODEL, D_MODEL), F32), pltpu.VMEM((IN_W, D_MODEL), F32),
                        pltpu.SemaphoreType.DMA((N_DEV,)),
                        pltpu.VMEM((7, ts + HALO, CONV_W), F32), pltpu.VMEM((7, ts + HALO, CONV_W), F32),
                        pltpu.VMEM((CONV_K, 8, CONV_W), F32)],
        input_output_aliases={0: 0},
        compiler_params=_params(dimension_semantics=("arbitrary",)),
    )(slabs, dx1, z1, hb, hb, a1, cat, p, gs, rv, x, w["cw"], w["va"], w["wcatt"], w["ktt"], w["vmt"], w["woutt"],
      w["wint"], w["v1"])


FFN_HALO = 8
FF_GROUP = D_FF // N_DEV
FF_GROUP_PAD = D_FF_PAD // N_DEV


def _ffn_taps(ubuf, ts, lo, hi):
    return tuple(ubuf[FFN_HALO - (FFN_CONV_K - 1) + k:FFN_HALO - (FFN_CONV_K - 1) + k + ts, lo:hi]
                 for k in range(FFN_CONV_K))


def _ffn_gate(ubuf, cf_ref, ts, lo, hi):
    taps = _ffn_taps(ubuf, ts, lo, hi)
    g = cf_ref[3:4, lo:hi] + cf_ref[2:3, lo:hi] * taps[2]
    g = g + cf_ref[1:2, lo:hi] * taps[1]
    return g + cf_ref[0:1, lo:hi] * taps[0]


FFN_CHUNK = 256
FFN_CHUNKS = tuple((lo, lo + FFN_CHUNK) for lo in range(0, D_FF_PAD, FFN_CHUNK))


def _ffn_fwd_call(x1, w, ts, exch=None, want_x2=True):
    seq = x1.shape[0]
    n = seq // ts
    n_wide = 4

    def body(x1_ref, wg_ref, wv_ref, cf_ref, wdown_ref, v2_ref, ug_ref, uv_ref, sl_ref, dsl_ref, z2_ref, *rest):
        x2_ref = rest[0] if want_x2 else None
        ubuf, act_buf = rest[-2:]
        i = pl.program_id(0)

        @pl.when(i == 0)
        def _():
            ubuf[0:FFN_HALO, :] = jnp.zeros((FFN_HALO, D_FF_PAD), F32)

        @pl.when(i > 0)
        def _():
            ubuf[0:FFN_HALO, :] = ubuf[ts:ts + FFN_HALO, :]

        xv = x1_ref[...]
        xb = xv.astype(BF16)
        for lo, hi in FFN_CHUNKS:
            ug = _dot(xb, wg_ref[:, lo:hi]).astype(BF16)
            uv = _dot(xb, wv_ref[:, lo:hi]).astype(BF16)
            ug_ref[:, lo:hi] = ug
            uv_ref[:, lo:hi] = uv
            ubuf[FFN_HALO:FFN_HALO + ts, lo:hi] = ug.astype(F32)
            gate = _ffn_gate(ubuf, cf_ref, ts, lo, hi)
            sg = _sigmoid(gate)
            sl = gate * sg
            sl_ref[:, lo:hi] = sl.astype(BF16)
            dsl_ref[:, lo:hi] = (sg * (1.0 + gate * (1.0 - sg))).astype(BF16)
            act_buf[:, lo:hi] = (sl * uv.astype(F32)).astype(BF16)
        y = ALPHA * xv + _dot(act_buf[...], wdown_ref[...])
        z2_ref[...] = y
        if want_x2:
            xh, _ = _ln_stats(y)
            x2_ref[...] = xh * v2_ref[0:1, :] + v2_ref[1:2, :]

    row = lambda width: pl.BlockSpec((ts, width), lambda i: (i, 0))
    n_narrow = 2 if want_x2 else 1
    return _grid_call(
        body, "ffn_fwd" if want_x2 else "ffn_fwd_last", n,
        in_specs=[row(D_MODEL), _const((D_MODEL, D_FF_PAD)), _const((D_MODEL, D_FF_PAD)), _const((8, D_FF_PAD)),
                  _const((D_FF_PAD, D_MODEL)), _const((8, D_MODEL))],
        out_specs=[row(D_FF_PAD)] * n_wide + [row(D_MODEL)] * n_narrow,
        out_shape=[jax.ShapeDtypeStruct((seq, D_FF_PAD), BF16)] * n_wide
                  + [jax.ShapeDtypeStruct((seq, D_MODEL), F32)] * n_narrow,
        scratch_shapes=[pltpu.VMEM((ts + FFN_HALO, D_FF_PAD), F32), pltpu.VMEM((ts, D_FF_PAD), BF16)],
        args=(x1, w["wg"], w["wv"], w["cf"], w["wdown"], w["v2"]), exch=exch)


def _ffn_bwd_call(dx2_or_target, z2, ug, uv, sl, dsl, w, ts, last, slabs, row_off, exch=None):
    seq = z2.shape[0]
    n = seq // ts
    halo_blocks = ts // 16

    def body(slabs_in, dx2_ref, z2_ref, ug_ref, uv_ref, sl_ref, dsl_ref, uprev_ref, cf_ref, wdownt_ref, v2_ref,
             slabs_ref, dug_ref, duv_ref, dz2_ref, dcf_ref, dv2_ref, loss_ref,
             ubuf, dgbuf, dwacc, sems):
        i = pl.program_id(0)

        @pl.when(i == 0)
        def _():
            dwacc[...] = jnp.zeros(dwacc.shape, F32)
            dcf_ref[...] = jnp.zeros(dcf_ref.shape, F32)
            dv2_ref[...] = jnp.zeros(dv2_ref.shape, F32)
            loss_ref[...] = jnp.zeros(loss_ref.shape, F32)
            dgbuf[ts:ts + FFN_HALO, :] = jnp.zeros((FFN_HALO, D_FF_PAD), F32)

        @pl.when(i > 0)
        def _():
            dgbuf[ts:ts + FFN_HALO, :] = dgbuf[0:FFN_HALO, :]

        xh2, r2 = _ln_stats(z2_ref[...])
        if last:
            diff = xh2 * v2_ref[0:1, :] + v2_ref[1:2, :] - dx2_ref[...]
            loss_ref[...] += jnp.sum(diff * diff) * (0.5 / D_MODEL)
            dx2v = diff * (1.0 / D_MODEL)
        else:
            dx2v = dx2_ref[...]
        dv2_ref[0:1, :] += _rowsum(dx2v * xh2)
        dv2_ref[1:2, :] += _rowsum(dx2v)
        dz2 = _ln_bwd(dx2v, xh2, r2, v2_ref[0:1, :])
        dz2_ref[...] = dz2
        dy = dz2.astype(BF16)

        up = uprev_ref[...].astype(F32)[8:16, :]
        ubuf[0:FFN_HALO, :] = jnp.where(i == n - 1, 0.0, up)
        ubuf[FFN_HALO:FFN_HALO + ts, :] = ug_ref[...].astype(F32)
        for lo, hi in FFN_CHUNKS:
            taps = _ffn_taps(ubuf, ts, lo, hi)
            sl = sl_ref[:, lo:hi].astype(F32)
            uvf = uv_ref[:, lo:hi].astype(F32)
            act = (sl * uvf).astype(BF16)
            dwacc[lo:hi, :] += _dot_tn(act, dy)
            dact = _dot(dy, wdownt_ref[:, lo:hi])
            duv_ref[:, lo:hi] = (dact * sl).astype(BF16)
            dgate = dact * uvf * dsl_ref[:, lo:hi].astype(F32)
            dgbuf[0:ts, lo:hi] = dgate
            dcf_ref[3:4, lo:hi] += _rowsum(dgate)
            for k in range(FFN_CONV_K):
                dcf_ref[k:k + 1, lo:hi] += _rowsum(dgate * taps[k])
            dug = cf_ref[2:3, lo:hi] * dgate + cf_ref[1:2, lo:hi] * dgbuf[1:1 + ts, lo:hi]
            dug = dug + cf_ref[0:1, lo:hi] * dgbuf[2:2 + ts, lo:hi]
            dug_ref[:, lo:hi] = dug.astype(BF16)

        @pl.when(i == n - 1)
        def _():
            _store_blocks(dwacc, slabs_ref, sems, row_off, D_FF_PAD // N_DEV, 0, N_DEV)

    rev = lambda width: pl.BlockSpec((ts, width), lambda i: (n - 1 - i, 0))
    prev = pl.BlockSpec((16, D_FF_PAD), lambda i: (jnp.maximum((n - 1 - i) * halo_blocks - 1, 0), 0))
    hbm = pl.BlockSpec(memory_space=pl.ANY)
    return _grid_call(
        body, "ffn_bwd_last" if last else "ffn_bwd", n,
        in_specs=[hbm, rev(D_MODEL), rev(D_MODEL)] + [rev(D_FF_PAD)] * 4 + [prev, _const((8, D_FF_PAD)),
                                                                           _const((D_MODEL, D_FF_PAD)), _const((8, D_MODEL))],
        out_specs=[hbm, rev(D_FF_PAD), rev(D_FF_PAD), rev(D_MODEL),
                   _acc((8, D_FF_PAD)), _acc((8, D_MODEL)), _acc((8, 128))],
        out_shape=[jax.ShapeDtypeStruct(slabs.shape, F32),
                   jax.ShapeDtypeStruct((seq, D_FF_PAD), BF16), jax.ShapeDtypeStruct((seq, D_FF_PAD), BF16),
                   jax.ShapeDtypeStruct((seq, D_MODEL), F32),
                   jax.ShapeDtypeStruct((8, D_FF_PAD), F32), jax.ShapeDtypeStruct((8, D_MODEL), F32),
                   jax.ShapeDtypeStruct((8, 128), F32)],
        scratch_shapes=[pltpu.VMEM((ts + FFN_HALO, D_FF_PAD), F32), pltpu.VMEM((ts + FFN_HALO, D_FF_PAD), F32),
                        pltpu.VMEM((D_FF_PAD, D_MODEL), F32), pltpu.SemaphoreType.DMA((N_DEV,))],
        args=(slabs, dx2_or_target, z2, ug, uv, sl, dsl, ug, w["cf"], w["wdownt"], w["v2"]), aliases={0: 0}, exch=exch)


def _proj_bwd_call(d, wt, xin, addend, scale, ts, name, slabs, row_off, first_block, n_blocks, exch=None):
    seq, k = d.shape
    n = seq // ts

    def body(slabs_in, d_ref, wt_ref, xin_ref, add_ref, slabs_ref, dx_ref, acc, sems):
        i = pl.program_id(0)

        @pl.when(i == 0)
        def _():
            acc[...] = jnp.zeros(acc.shape, F32)

        dv = d_ref[...]
        dx_ref[...] = _dot(dv, wt_ref[...]) + scale * add_ref[...]
        acc[...] += _dot_tn(dv, xin_ref[...].astype(BF16))

        @pl.when(i == n - 1)
        def _():
            _store_blocks(acc, slabs_ref, sems, row_off, k // n_blocks, first_block, n_blocks)

    row = lambda width: pl.BlockSpec((ts, width), lambda i: (i, 0))
    hbm = pl.BlockSpec(memory_space=pl.ANY)
    return _grid_call(
        body, name, n,
        in_specs=[hbm, row(k), _const((k, D_MODEL)), row(D_MODEL), row(D_MODEL)],
        out_specs=[hbm, row(D_MODEL)],
        out_shape=[jax.ShapeDtypeStruct(slabs.shape, F32), jax.ShapeDtypeStruct((seq, D_MODEL), F32)],
        scratch_shapes=[pltpu.VMEM((k, D_MODEL), F32), pltpu.SemaphoreType.DMA((n_blocks,))],
        args=(slabs, d, wt, xin, addend), aliases={0: 0}, exch=exch)


MEM_FOLD = BLOB_LANES // XATTN_W


def _mem_proj_call(memq, wk_flat, wv_flat):
    def body(memq_ref, wk_ref, wv_ref, kh_ref, vh_ref):
        for w_ref, o_ref in ((wk_ref, kh_ref), (wv_ref, vh_ref)):
            acc = jnp.zeros((N_MEM, XATTN_W), F32)
            for q in range(MEM_FOLD):
                acc = acc + _dot(memq_ref[q], w_ref[:, q * XATTN_W:(q + 1) * XATTN_W])
            o_ref[...] = acc

    out = jax.ShapeDtypeStruct((N_MEM, XATTN_W), F32)
    return pl.pallas_call(body, name="mem_proj", out_shape=[out, out], compiler_params=_params())(memq, wk_flat, wv_flat)


def _mem_proj_bwd_call(memq, dkh, dvh, slabs, off_k, off_v):
    rows = D_MODEL // MEM_FOLD

    def body(slabs_in, memq_ref, dkh_ref, dvh_ref, slabs_ref, acc, sems):
        for d_ref, off in ((dkh_ref, off_k), (dvh_ref, off_v)):
            dv = d_ref[...].astype(BF16)
            for q in range(MEM_FOLD):
                acc[:, q * XATTN_W:(q + 1) * XATTN_W] = _dot_tn(memq_ref[q], dv)
            _store_blocks(acc, slabs_ref, sems, off, rows // N_DEV, 0, N_DEV)

    hbm = pl.BlockSpec(memory_space=pl.ANY)
    vmem = pl.BlockSpec(memory_space=pltpu.VMEM)
    return pl.pallas_call(
        body, name="mem_proj_bwd", in_specs=[hbm, vmem, vmem, vmem], out_specs=hbm,
        out_shape=jax.ShapeDtypeStruct(slabs.shape, F32),
        scratch_shapes=[pltpu.VMEM((rows, BLOB_LANES), F32), pltpu.SemaphoreType.DMA((N_DEV,))],
        input_output_aliases={0: 0}, compiler_params=_params(),
    )(slabs, memq, dkh, dvh)


def _place():
    return lax.axis_index("x"), lax.axis_index("y"), lax.axis_index("c")


def _all_gather_call(arrs, pieces, name, swap=None):
    n_in, n_p = len(arrs), len(pieces)
    n_sw = 0 if swap is None else 1

    def body(*refs):
        ins, outs = refs[:n_in], refs[n_in + n_sw:n_in + n_sw + n_p]
        send_sems, recv_sems, local_sems = refs[n_in + 2 * n_sw + n_p:n_in + 2 * n_sw + n_p + 3]
        swapped = []
        if swap is not None:
            swapped = swap[1](refs[n_in], refs[n_in + n_sw + n_p], *refs[-2:])
            for cp in swapped:
                cp.start()
        x, y, c = _place()
        me, sibling = (x, y, c), (x, y, 1 - c)
        chips = [(1 - x, y), (x, 1 - y), (1 - x, 1 - y)]

        def src(a):
            idx, r0, rows = pieces[a]
            return ins[idx] if r0 is None else ins[idx].at[pl.ds(r0, rows)]

        def slab(a, p):
            return outs[a].at[4 * p[0] + 2 * p[1] + p[2]]

        def copy(a, k, block, to, own=False):
            return pltpu.make_async_remote_copy(
                src_ref=src(a) if own else slab(a, block), dst_ref=slab(a, block),
                send_sem=send_sems.at[a, k], recv_sem=recv_sems.at[a, k], device_id=to, device_id_type=MESH)

        mine = [pltpu.make_async_copy(src(a), slab(a, me), local_sems.at[a]) for a in range(n_p)]
        for cp in mine:
            cp.start()
        first = []
        for a in range(n_p):
            first.append(copy(a, 0, me, sibling, own=True))
            first += [copy(a, 1 + j, me, (*chip, c), own=True) for j, chip in enumerate(chips)]
        for cp in first:
            cp.start()
        passed = []
        for a in range(n_p):
            for j, chip in enumerate(chips):
                copy(a, 1 + j, (*chip, c), me).wait_recv()
                cp = copy(a, 4 + j, (*chip, c), sibling)
                cp.start()
                passed.append(cp)
        for a in range(n_p):
            copy(a, 0, sibling, me).wait_recv()
            for j, chip in enumerate(chips):
                copy(a, 4 + j, (*chip, 1 - c), me).wait_recv()
        for cp in first + passed:
            cp.wait_send()
        for cp in mine:
            cp.wait()
        for cp in swapped:
            cp.wait()

    def out_shape(piece):
        idx, r0, rows = piece
        a = arrs[idx]
        return jax.ShapeDtypeStruct((N_DEV,) + (a.shape if r0 is None else (rows,) + a.shape[1:]), a.dtype)

    hbm = pl.BlockSpec(memory_space=pl.ANY)
    out_shapes = [out_shape(p) for p in pieces]
    scratch = [pltpu.SemaphoreType.DMA((n_p, 7)), pltpu.SemaphoreType.DMA((n_p, 7)), pltpu.SemaphoreType.DMA((n_p,))]
    args = list(arrs)
    if swap is not None:
        a, _, n = swap
        args.append(a)
        out_shapes.append(jax.ShapeDtypeStruct((n,) + a.shape[1:], a.dtype))
        scratch += [pltpu.SemaphoreType.DMA((n,)), pltpu.SemaphoreType.DMA((n,))]
    return pl.pallas_call(
        body, name=name, in_specs=[hbm] * len(args), out_specs=[hbm] * len(out_shapes), out_shape=out_shapes,
        scratch_shapes=scratch,
    )(*args)


def _flip(v, f):
    return 1 - v if f else v


def _gather_exchange(arrs, pieces):
    n_peers = N_DEV - 1

    def build(ins, outs, send_sems, recv_sems, local_sems):
        x, y, c = _place()
        flips = [(fx, fy, fc) for fx in (0, 1) for fy in (0, 1) for fc in (0, 1) if fx or fy or fc]
        remote, local = [], []
        for a, (idx, r0, rows) in enumerate(pieces):
            src = ins[idx] if r0 is None else ins[idx].at[pl.ds(r0, rows)]
            dst = outs[a].at[4 * x + 2 * y + c]
            remote += [pltpu.make_async_remote_copy(
                src_ref=src, dst_ref=dst, send_sem=send_sems.at[n_peers * a + k], recv_sem=recv_sems.at[n_peers * a + k],
                device_id=(_flip(x, fx), _flip(y, fy), _flip(c, fc)), device_id_type=MESH)
                for k, (fx, fy, fc) in enumerate(flips)]
            local.append(pltpu.make_async_copy(src, dst, local_sems.at[a]))
        return remote, local

    def out_shape(piece):
        idx, r0, rows = piece
        a = arrs[idx]
        return jax.ShapeDtypeStruct((N_DEV,) + (a.shape if r0 is None else (rows,) + a.shape[1:]), a.dtype)

    return _Exchange(arrs, [out_shape(p) for p in pieces], n_peers * len(pieces), build, n_local=len(pieces))


def _swap_core_copies(g_ref, r_ref, send_sems, recv_sems):
    x, y, c = _place()
    return [pltpu.make_async_remote_copy(
        src_ref=g_ref.at[2 * k + (1 - c)], dst_ref=r_ref.at[k], send_sem=send_sems.at[k], recv_sem=recv_sems.at[k],
        device_id=(x, y, 1 - c), device_id_type=MESH) for k in range(4)]


def _swap_chip_copies(p_ref, r_ref, send_sems, recv_sems):
    x, y, c = _place()
    chips = [(1 - x, y), (x, 1 - y), (1 - x, 1 - y)]
    return [pltpu.make_async_remote_copy(
        src_ref=p_ref.at[2 * px + py], dst_ref=r_ref.at[j], send_sem=send_sems.at[j], recv_sem=recv_sems.at[j],
        device_id=(px, py, c), device_id_type=MESH) for j, (px, py) in enumerate(chips)]


def _swap_exchange(a, copies, n):
    return _Exchange([a], [jax.ShapeDtypeStruct((n,) + a.shape[1:], a.dtype)], n,
                     lambda ins, outs, send_sems, recv_sems, local_sems: (copies(ins[0], outs[0], send_sems, recv_sems), []))


def _swap_call(a, copies, n, name):
    def body(a_ref, r_ref, send_sems, recv_sems):
        cps = copies(a_ref, r_ref, send_sems, recv_sems)
        for cp in cps:
            cp.start()
        for cp in cps:
            cp.wait()

    hbm = pl.BlockSpec(memory_space=pl.ANY)
    return pl.pallas_call(
        body, name=name, in_specs=[hbm], out_specs=hbm, out_shape=jax.ShapeDtypeStruct((n,) + a.shape[1:], a.dtype),
        scratch_shapes=[pltpu.SemaphoreType.DMA((n,)), pltpu.SemaphoreType.DMA((n,))],
    )(a)


def _swap_chip_start(p, rider):
    n = 3
    land = lax.empty((n,) + p.shape[1:], p.dtype)

    def body(p_ref, land_ref, rider_ref, send_sems, recv_sems, p_thru, land_thru, rider_thru):
        for cp in _swap_chip_copies(p_ref, land_ref, send_sems, recv_sems):
            cp.start()

    hbm, sem = pl.BlockSpec(memory_space=pltpu.HBM), pl.BlockSpec(memory_space=pltpu.SEMAPHORE)
    return pl.pallas_call(
        body, name="rs_swap_chip_start",
        out_shape=(pltpu.SemaphoreType.DMA((n,)), pltpu.SemaphoreType.DMA((n,)), pltpu.HBM(p.shape, p.dtype),
                   pltpu.HBM(land.shape, land.dtype), pltpu.HBM(rider.shape, rider.dtype)),
        in_specs=(hbm, hbm, hbm), out_specs=(sem, sem, hbm, hbm, hbm), input_output_aliases={0: 2, 1: 3, 2: 4},
        compiler_params=pltpu.CompilerParams(has_side_effects=pltpu.SideEffectType.DATAFLOW_SIDE_EFFECTING),
    )(pltpu.with_memory_space_constraint(p, pltpu.HBM), pltpu.with_memory_space_constraint(land, pltpu.HBM),
      pltpu.with_memory_space_constraint(rider, pltpu.HBM))


def _swap_chip_wait(send_sems, recv_sems, p_thru, land_thru, *after):
    def body(p_ref, land_ref, send_sems, recv_sems, *rest):
        for cp in _swap_chip_copies(p_ref, land_ref, send_sems, recv_sems):
            cp.wait_send()
            cp.wait_recv()

    hbm, sem = pl.BlockSpec(memory_space=pltpu.HBM), pl.BlockSpec(memory_space=pltpu.SEMAPHORE)
    return pl.pallas_call(
        body, name="rs_swap_chip_wait",
        out_shape=(pltpu.HBM(p_thru.shape, p_thru.dtype), pltpu.HBM(land_thru.shape, land_thru.dtype)),
        in_specs=(hbm, hbm, sem, sem) + (pl.BlockSpec(memory_space=pl.ANY),) * len(after), out_specs=(hbm, hbm),
        input_output_aliases={0: 0, 1: 1},
        compiler_params=pltpu.CompilerParams(has_side_effects=pltpu.SideEffectType.DATAFLOW_SIDE_EFFECTING),
    )(p_thru, land_thru, send_sems, recv_sems, *after)[1]


ADD_ROWS = 184


def _pair_add_call(g, r, c, chip):
    _, rows, width = g.shape

    def body(c_ref, chip_ref, g_ref, r_ref, o_ref):
        o_ref[...] = (g_ref[...] + r_ref[...]).astype(BF16)

    def other(j, chip_ref):
        return j + (j >= chip_ref[0]).astype(jnp.int32)

    return pl.pallas_call(
        body, name="rs_pair_add",
        grid_spec=pltpu.PrefetchScalarGridSpec(
            num_scalar_prefetch=2, grid=(3, rows // ADD_ROWS),
            in_specs=[pl.BlockSpec((None, ADD_ROWS, width),
                                   lambda j, i, c_ref, chip_ref: (2 * other(j, chip_ref) + c_ref[0], i, 0)),
                      pl.BlockSpec((None, ADD_ROWS, width), lambda j, i, c_ref, chip_ref: (other(j, chip_ref), i, 0))],
            out_specs=pl.BlockSpec((None, ADD_ROWS, width), lambda j, i, c_ref, chip_ref: (other(j, chip_ref), i, 0))),
        out_shape=jax.ShapeDtypeStruct((4, rows, width), BF16),
        compiler_params=_params(dimension_semantics=("arbitrary", "arbitrary")),
    )(c, chip, g, r)


def _adam(w, g, m, v):
    mn = ADAM_B1 * m + (1.0 - ADAM_B1) * g
    vn = ADAM_B2 * v + (1.0 - ADAM_B2) * (g * g)
    m_hat = mn / (1.0 - ADAM_B1 ** ADAM_STEP)
    v_hat = vn / (1.0 - ADAM_B2 ** ADAM_STEP)
    return -ADAM_LR * (m_hat / (jnp.sqrt(v_hat) + ADAM_EPS) + ADAM_WD * w), mn, vn


def _chip_add_adamw_call(slabs, from_sibling, from_chips, me, chip, w, m, v):
    _, rows, width = slabs.shape

    def body(me_ref, chip_ref, own_ref, sib_ref, r_ref, w_ref, m_ref, v_ref, g_ref, d_ref, mo_ref, vo_ref):
        g = own_ref[...] + sib_ref[...]
        for j in range(3):
            g = g + r_ref[j].astype(F32)
        g_ref[...] = g
        d_ref[...], mo_ref[...], vo_ref[...] = _adam(w_ref[...], g, m_ref[...], v_ref[...])

    spec = pl.BlockSpec((ADD_ROWS, width), lambda i, me_ref, chip_ref: (i, 0))
    return pl.pallas_call(
        body, name="rs_chip_add_adamw",
        grid_spec=pltpu.PrefetchScalarGridSpec(
            num_scalar_prefetch=2, grid=(rows // ADD_ROWS,),
            in_specs=[pl.BlockSpec((None, ADD_ROWS, width), lambda i, me_ref, chip_ref: (me_ref[0], i, 0)),
                      pl.BlockSpec((None, ADD_ROWS, width), lambda i, me_ref, chip_ref: (chip_ref[0], i, 0)),
                      pl.BlockSpec((3, ADD_ROWS, width), lambda i, me_ref, chip_ref: (0, i, 0)), spec, spec, spec],
            out_specs=[spec] * 4),
        out_shape=[jax.ShapeDtypeStruct((rows, width), F32)] * 4,
        compiler_params=_params(dimension_semantics=("arbitrary",)),
    )(me, chip, slabs, from_sibling, from_chips, w, m, v)


def _adamw_whole_call(params, name):
    n = len(params)

    def body(*refs):
        ins, outs = refs[:4 * n], refs[4 * n:]
        for a in range(n):
            w_ref, g_ref, m_ref, v_ref = ins[4 * a:4 * a + 4]
            outs[3 * a][...], outs[3 * a + 1][...], outs[3 * a + 2][...] = _adam(w_ref[...], g_ref[...], m_ref[...],
                                                                              v_ref[...])

    flat = [a for p in params for a in p]
    shapes = [jax.ShapeDtypeStruct(p[0].shape, F32) for p in params for _ in range(3)]
    out = pl.pallas_call(body, name=name, out_shape=shapes, compiler_params=_params())(*flat)
    return [tuple(out[3 * a:3 * a + 3]) for a in range(n)]


GATHERED_ACCS = (("dva", (8, CONV_W)), ("dv1", (8, D_MODEL)), ("dv2", (8, D_MODEL)), ("dcf", (8, D_FF_PAD)),
                 ("dcw", (HALO, CONV_W)), ("dwcat", (CHUNK, GMLP_HEADS * CHUNK)), ("dmsum", (CHUNK, GMLP_W)))
VEC_A = ("conv_a_b", "ln_a_g", "ln_a_b", "ln_v_g", "ln_v_b")
REP_IN_KERNEL = VEC_A + ("ln1_g", "ln1_b", "ln2_g", "ln2_b", "w_s", "b_s")


def _replicated_update_call(gathered, p, mom_m, mom_v):
    n_acc = len(GATHERED_ACCS)
    n_rep = len(REP_IN_KERNEL)

    def body(*refs):
        acc_refs = refs[:DEPTH * n_acc]
        wmv = refs[DEPTH * n_acc:DEPTH * n_acc + 3 * n_rep]
        outs = refs[DEPTH * n_acc + 3 * n_rep:]
        out_par = {nm: outs[4 * a:4 * a + 4] for a, nm in enumerate(REP_IN_KERNEL)}
        out_dcf = outs[4 * n_rep:4 * n_rep + DEPTH]
        out_dcw = outs[4 * n_rep + DEPTH:4 * n_rep + 2 * DEPTH]
        par = {nm: wmv[3 * a:3 * a + 3] for a, nm in enumerate(REP_IN_KERNEL)}
        tril = (lax.broadcasted_iota(jnp.int32, (CHUNK, CHUNK), 0) >= lax.broadcasted_iota(jnp.int32, (CHUNK, CHUNK), 1))
        head = lax.broadcasted_iota(jnp.int32, (8, GMLP_W), 0) * HEAD_DIM
        lane = lax.broadcasted_iota(jnp.int32, (8, GMLP_W), 1)
        sel = jnp.where((lane >= head) & (lane < head + HEAD_DIM), 1.0, 0.0)

        def update(nm, idx, g):
            w_ref, m_ref, v_ref = par[nm]
            d, mn, vn = _adam(w_ref[idx], g, m_ref[idx], v_ref[idx])
            g_ref, d_ref, mo_ref, vo_ref = out_par[nm]
            g_ref[idx] = g
            d_ref[idx] = d
            mo_ref[idx] = mn
            vo_ref[idx] = vn

        for l in range(DEPTH):
            tot = {}
            for a, (nm, _) in enumerate(GATHERED_ACCS):
                ref = acc_refs[l * n_acc + a]
                s = ref[0]
                for j in range(1, N_DEV):
                    s = s + ref[j]
                tot[nm] = s
            out_dcf[l][...] = tot["dcf"]
            out_dcw[l][...] = tot["dcw"]
            row = (slice(l, l + 1), slice(None))
            for k, nm in enumerate(VEC_A):
                update(nm, row, tot["dva"][k:k + 1, :])
            update("ln1_g", row, tot["dv1"][0:1, :])
            update("ln1_b", row, tot["dv1"][1:2, :])
            update("ln2_g", row, tot["dv2"][0:1, :])
            update("ln2_b", row, tot["dv2"][1:2, :])
            for h in range(GMLP_HEADS):
                gw = jnp.where(tril, tot["dwcat"][:, h * CHUNK:(h + 1) * CHUNK], 0.0)
                update("w_s", (l, h), gw)
            gb = lax.dot_general(sel, tot["dmsum"], (((1,), (1,)), ((), ())), precision=lax.Precision.HIGHEST,
                                 preferred_element_type=F32)
            for h in range(GMLP_HEADS):
                update("b_s", (l, slice(h, h + 1), slice(None)), gb[h:h + 1, :])

    ins = [gathered[l][nm] for l in range(DEPTH) for nm, _ in GATHERED_ACCS]
    ins += [t[nm] for nm in REP_IN_KERNEL for t in (p, mom_m, mom_v)]
    shapes = [jax.ShapeDtypeStruct(p[nm].shape, F32) for nm in REP_IN_KERNEL for _ in range(4)]
    shapes += [jax.ShapeDtypeStruct((8, D_FF_PAD), F32)] * DEPTH + [jax.ShapeDtypeStruct((HALO, CONV_W), F32)] * DEPTH
    out = pl.pallas_call(body, name="replicated_update", out_shape=shapes, compiler_params=_params())(*ins)
    res = [{nm: out[4 * a + k] for a, nm in enumerate(REP_IN_KERNEL)} for k in range(4)]
    return res, out[4 * n_rep:4 * n_rep + DEPTH], out[4 * n_rep + DEPTH:]


BLOCK_ROWS = (("w_in", IN_W // N_DEV), ("w_out", D_MODEL // N_DEV), ("w_up", 2 * FF_GROUP_PAD),
              ("w_down", FF_GROUP_PAD), ("w_mk", D_MODEL // N_DEV // MEM_FOLD), ("w_mv", D_MODEL // N_DEV // MEM_FOLD))
LAYER_ROWS = sum(r for _, r in BLOCK_ROWS)
assert LAYER_ROWS % ADD_ROWS == 0 and all(r % 16 == 0 for _, r in BLOCK_ROWS)


def _row_off(name):
    off = 0
    for nm, r in BLOCK_ROWS:
        if nm == name:
            return off
        off += r
    raise KeyError(name)


def _to_rows(name, a):
    if name == "w_in":
        return a.T
    if name == "w_up":
        t = a.T.reshape(2, FF_GROUP, D_MODEL)
        return jnp.pad(t, ((0, 0), (0, FF_GROUP_PAD - FF_GROUP), (0, 0))).reshape(2 * FF_GROUP_PAD, D_MODEL)
    if name == "w_down":
        return jnp.pad(a, ((0, FF_GROUP_PAD - FF_GROUP), (0, 0)))
    if name == "w_out":
        return a
    return a.reshape(-1, BLOB_LANES)


def _from_rows(name, r):
    if name == "w_in":
        return r.T
    if name == "w_up":
        return r.reshape(2, FF_GROUP_PAD, D_MODEL)[:, :FF_GROUP].reshape(2 * FF_GROUP, D_MODEL).T
    if name == "w_down":
        return r[:FF_GROUP]
    if name == "w_out":
        return r
    return r.reshape(D_MODEL // N_DEV, XATTN_W)


def _blob(tree, l):
    return jnp.concatenate([_to_rows(nm, tree[nm][l]) for nm, _ in BLOCK_ROWS], axis=0)


def _unblob(blobs):
    return {nm: jnp.stack([_from_rows(nm, b[_row_off(nm):_row_off(nm) + r]) for b in blobs]) for nm, r in BLOCK_ROWS}


def _ff_interleave(a):
    lead = a.shape[:-1]
    t = a.reshape(lead + (N_DEV, FF_GROUP))
    return jnp.pad(t, [(0, 0)] * len(lead) + [(0, 0), (0, FF_GROUP_PAD - FF_GROUP)]).reshape(lead + (D_FF_PAD,))


def _ff_deinterleave(a):
    lead = a.shape[:-1]
    return a.reshape(lead + (N_DEV, FF_GROUP_PAD))[..., :FF_GROUP].reshape(lead + (D_FF,))


def _head_table():
    hd = jnp.arange(XATTN_W) // HEAD_DIM
    return (hd[None, :] == jnp.arange(XATTN_HEADS)[:, None]).astype(F32)


def _mixer_operands(mat, conv_a_w, p, l, memq):
    w = {}
    w["wint"] = mat["w_in"]
    w["win"] = mat["w_in"].T
    w["wout"] = mat["w_out"]
    w["woutt"] = mat["w_out"].T
    w["cw"] = conv_a_w
    zeros = jnp.zeros((3, CONV_W), F32)
    w["va"] = jnp.concatenate([p[nm][l][None] for nm in VEC_A] + [zeros], axis=0)
    tril = jnp.tril(jnp.ones((CHUNK, CHUNK), F32))
    w["wcat"] = (p["w_s"][l] * tril[None]).transpose(1, 0, 2).reshape(CHUNK, GMLP_HEADS * CHUNK).astype(BF16)
    w["wcatt"] = w["wcat"].T
    w["bfull"] = jnp.repeat(p["b_s"][l].T, HEAD_DIM, axis=1)
    kh, vh = _mem_proj_call(memq, mat["w_mk"], mat["w_mv"])
    hm = _head_table()
    scale = 1.0 / math.sqrt(HEAD_DIM)
    w["kt"] = (kh.T[:, None, :] * hm.T[:, :, None] * scale).reshape(XATTN_W, XATTN_HEADS * N_MEM).astype(BF16)
    w["ktt"] = w["kt"].T
    w["vm"] = (hm[:, None, :] * vh[None]).reshape(XATTN_HEADS * N_MEM, XATTN_W).astype(BF16)
    w["vmt"] = w["vm"].T
    zeros = jnp.zeros((6, D_MODEL), F32)
    w["v1"] = jnp.concatenate([p["ln1_g"][l][None], p["ln1_b"][l][None], zeros], axis=0)
    return w


def _ffn_operands(w_up, w_down, conv_f_w, p, l):
    w = {}
    w["wgt"] = w_up[:D_FF_PAD]
    w["wvt"] = w_up[D_FF_PAD:]
    w["wg"] = w["wgt"].T
    w["wv"] = w["wvt"].T
    w["wdown"] = w_down
    w["wdownt"] = w_down.T
    zeros = jnp.zeros((6, D_MODEL), F32)
    w["v2"] = jnp.concatenate([p["ln2_g"][l][None], p["ln2_b"][l][None], zeros], axis=0)
    w["cf"] = jnp.concatenate([conv_f_w, _ff_interleave(p["conv_f_b"][l][None]), jnp.zeros((4, D_FF_PAD), F32)], axis=0)
    return w


TS_MIXER = 256
TS_FFN = 256
TS_PROJ = 512
CONV_A_SHARD = CONV_W // N_DEV


def kernel(x, mem, w_in, conv_a_w, conv_a_b, ln_a_g, ln_a_b, ln_v_g, ln_v_b, w_s, b_s, w_mk, w_mv, w_out, ln1_g, ln1_b, w_up, conv_f_w, conv_f_b, w_down, ln2_g, ln2_b, loss_target, m_w_in, m_conv_a_w, m_conv_a_b, m_ln_a_g, m_ln_a_b, m_ln_v_g, m_ln_v_b, m_w_s, m_b_s, m_w_mk, m_w_mv, m_w_out, m_ln1_g, m_ln1_b, m_w_up, m_conv_f_w, m_conv_f_b, m_w_down, m_ln2_g, m_ln2_b, v_w_in, v_conv_a_w, v_conv_a_b, v_ln_a_g, v_ln_a_b, v_ln_v_g, v_ln_v_b, v_w_s, v_b_s, v_w_mk, v_w_mv, v_w_out, v_ln1_g, v_ln1_b, v_w_up, v_conv_f_w, v_conv_f_b, v_w_down, v_ln2_g, v_ln2_b):
    given = dict(locals())
    p = {nm: given[nm] for nm in WEIGHTS}
    mom_m = {nm: given["m_" + nm] for nm in WEIGHTS}
    mom_v = {nm: given["v_" + nm] for nm in WEIGHTS}
    seq = x.shape[1]
    ts_m, ts_f, ts_p = min(TS_MIXER, seq), min(TS_FFN, seq), min(TS_PROJ, seq)
    cx, cy, cc = _place()
    me = 4 * cx + 2 * cy + cc

    blobs = [_blob(p, l) for l in range(DEPTH)]
    blobs_bf = [b.astype(BF16) for b in blobs]
    conv_a_tile = jnp.pad(conv_a_w, ((0, 0), (0, HALO - CONV_K), (0, 128 - CONV_A_SHARD)))
    conv_f_tile = jnp.pad(conv_f_w, ((0, 0), (0, 8 - FFN_CONV_K), (0, 384 - FF_GROUP)))
    rows = dict(BLOCK_ROWS)
    mixer_names = ("w_in", "w_out", "w_mk", "w_mv")
    pieces = [(0, _row_off(nm), rows[nm]) for nm in mixer_names] + [(1, None, 0), (2, None, 0)]
    first = _all_gather_call([blobs_bf[0], conv_a_tile, conv_f_tile], pieces, "gather_weights")
    conv_a_all, conv_f_all = first[len(mixer_names)], first[len(mixer_names) + 1]
    conv_a = [conv_a_all[:, l, :, :CONV_A_SHARD].transpose(1, 0, 2).reshape(HALO, CONV_W) for l in range(DEPTH)]
    conv_f = [conv_f_all[:, l, :FFN_CONV_K, :FF_GROUP_PAD].transpose(1, 0, 2).reshape(FFN_CONV_K, D_FF_PAD)
              for l in range(DEPTH)]
    memq = mem[0].reshape(N_MEM, D_MODEL // MEM_FOLD, MEM_FOLD).transpose(2, 0, 1).astype(BF16)

    def gather_of(l, names):
        return _gather_exchange([blobs_bf[l]], [(0, _row_off(nm), rows[nm]) for nm in names])

    def full(pieces):
        return [g.reshape(-1, BLOB_LANES) for g in pieces]

    ffn_names = ("w_up", "w_down")
    ops0 = _mixer_operands(dict(zip(mixer_names, full(first[:len(mixer_names)]))), conv_a[0], p, 0, memq)
    mixer_saved = ("hb", "z1", "x1", "a1", "cat", "p", "gs", "rv")
    mixed0, ffn0 = _mixer_fwd_call(x[0], ops0, ts_m, gather_of(0, ffn_names))
    ops0.update(_ffn_operands(*full(ffn0), conv_f[0], p, 0))
    saved = [dict(zip(mixer_saved, mixed0), x=x[0])]
    (ug, uv, sl, dsl, z2, x2), all1 = _ffn_fwd_call(saved[0]["x1"], ops0, ts_f, gather_of(1, mixer_names + ffn_names))
    saved[0].update(ug=ug, uv=uv, sl=sl, dsl=dsl, z2=z2)
    all1 = dict(zip(mixer_names + ffn_names, full(all1)))
    ops1 = _mixer_operands(all1, conv_a[1], p, 1, memq)
    ops1.update(_ffn_operands(all1["w_up"], all1["w_down"], conv_f[1], p, 1))
    mixed1, _ = _mixer_fwd_call(x2, ops1, ts_m)
    saved.append(dict(zip(mixer_saved, mixed1), x=x2))
    (ug, uv, sl, dsl, z2), _ = _ffn_fwd_call(saved[1]["x1"], ops1, ts_f, want_x2=False)
    saved[1].update(ug=ug, uv=uv, sl=sl, dsl=dsl, z2=z2)
    ops = [ops0, ops1]

    hm = _head_table()
    core_id = cc.reshape(1).astype(jnp.int32)
    me_id, chip_id = me.reshape(1).astype(jnp.int32), (2 * cx + cy).reshape(1).astype(jnp.int32)
    slabs = [lax.empty((N_DEV, LAYER_ROWS, BLOB_LANES), F32) for _ in range(DEPTH)]
    accs, gathered_accs = [None] * DEPTH, [None] * DEPTH
    acc_names = [nm for nm, _ in GATHERED_ACCS]
    whole = [(a, None, 0) for a in range(len(acc_names))]

    def acc_list(l):
        return [accs[l][nm] for nm in acc_names]

    from_sibling, from_chips = [None] * DEPTH, [None] * DEPTH
    dx = loss_target[0]
    loss = None
    for l in reversed(range(DEPTH)):
        s, w = saved[l], ops[l]
        last = l == DEPTH - 1
        ride = None if last else _swap_exchange(slabs[l + 1], _swap_core_copies, 4)
        (sl, dug, duv, dz2, dcf, dv2, loss_acc), got = _ffn_bwd_call(
            dx, s["z2"], s["ug"], s["uv"], s["sl"], s["dsl"], w, ts_f, last, slabs[l], _row_off("w_down"), ride)
        if last:
            loss = loss_acc[0, 0]
        else:
            from_sibling[l + 1] = got[0]
            chip_sum = _pair_add_call(slabs[l + 1], from_sibling[l + 1], core_id, chip_id)
        off_up = _row_off("w_up")
        ride = None if last else _gather_exchange(acc_list(l + 1), whole)
        (sl, dxa), got = _proj_bwd_call(dug, w["wgt"], s["x1"], dz2, ALPHA, ts_p, "up_gate_bwd", sl, off_up, 0, 4, ride)
        if not last:
            gathered_accs[l + 1] = dict(zip(acc_names, got))
        ride = None if last else _swap_exchange(chip_sum, _swap_chip_copies, 3)
        (sl, dx1), got = _proj_bwd_call(duv, w["wvt"], s["x1"], dxa, 1.0, ts_p, "up_val_bwd", sl, off_up, 4, 4, ride)
        if not last:
            from_chips[l + 1] = got[0]
        (sl, dx, dkt, dvm, dwcat, dmsum, dva, dcw, dv1) = _mixer_bwd_call(
            dx1, s["z1"], s["hb"], s["a1"], s["cat"], s["p"], s["gs"], s["rv"], s["x"], w, ts_m, sl, _row_off("w_out"),
            _row_off("w_in"))
        dkh = jnp.einsum("hd,dhm->md", hm, dkt.reshape(XATTN_W, XATTN_HEADS, N_MEM)) * (1.0 / math.sqrt(HEAD_DIM))
        dvh = jnp.einsum("hd,hmd->md", hm, dvm.reshape(XATTN_HEADS, N_MEM, XATTN_W))
        slabs[l] = _mem_proj_bwd_call(memq, dkh, dvh, sl, _row_off("w_mk"), _row_off("w_mv"))
        accs[l] = dict(dva=dva, dv1=dv1, dv2=dv2, dcf=dcf, dcw=dcw, dwcat=dwcat, dmsum=dmsum)
    grad_x = dx[None]
    *got, from_sibling[0] = _all_gather_call(acc_list(0), whole, "gather_small_grads",
                                            swap=(slabs[0], _swap_core_copies, 4))
    gathered_accs[0] = dict(zip(acc_names, got))
    chip_sum = _pair_add_call(slabs[0], from_sibling[0], core_id, chip_id)
    in_flight = _swap_chip_start(chip_sum, from_chips[1])
    from_chips[1] = in_flight[4]

    def final(l):
        return _chip_add_adamw_call(slabs[l], from_sibling[l], from_chips[l], me_id, chip_id, blobs[l], _blob(mom_m, l),
                                    _blob(mom_v, l))

    per_layer = [None, final(1)]
    rep, dcf_sum, dcw_sum = _replicated_update_call(gathered_accs, p, mom_m, mom_v)
    dcf_sum, dcw_sum = jnp.stack(dcf_sum), jnp.stack(dcw_sum)
    zero = jnp.zeros((), jnp.int32)
    g_conv_a_w = lax.dynamic_slice(dcw_sum, (zero, zero, CONV_A_SHARD * me), (DEPTH, CONV_K, CONV_A_SHARD))
    g_conv_f_w = lax.dynamic_slice(dcf_sum, (zero, zero, FF_GROUP_PAD * me), (DEPTH, FFN_CONV_K, FF_GROUP))
    g_conv_f_b = _ff_deinterleave(dcf_sum[:, FFN_CONV_K])
    conv_grads = dict(conv_a_w=g_conv_a_w, conv_f_w=g_conv_f_w, conv_f_b=g_conv_f_b)
    conv_names = tuple(conv_grads)
    upd = _adamw_whole_call([(p[nm], conv_grads[nm], mom_m[nm], mom_v[nm]) for nm in conv_names], "adamw_conv")
    from_chips[0] = _swap_chip_wait(*in_flight[:4], per_layer[1][0], *[u for trio in upd for u in trio])
    per_layer[0] = final(0)
    outs = [_unblob([per_layer[l][k] for l in range(DEPTH)]) for k in range(4)]
    for k in range(4):
        outs[k].update(rep[k])
    for nm, (d, mn, vn) in zip(conv_names, upd):
        outs[0][nm], outs[1][nm], outs[2][nm], outs[3][nm] = conv_grads[nm], d, mn, vn

    loss = lax.psum(loss, ("x", "y", "c"))
    return (loss, grad_x, *[outs[0][nm] for nm in WEIGHTS], *[outs[1][nm] for nm in WEIGHTS],
            *[outs[2][nm] for nm in WEIGHTS], *[outs[3][nm] for nm in WEIGHTS])
```
